```python
import jax, jax.numpy as jnp
from jax import lax
import numpy as np

D_MODEL = 1024
BATCH = 8
SEQ = 16384
DEPTH = 2

CHUNK = 64
Q_BLOCK = 128
N_MEM = 256
EPS = 1e-6
FOX_HEADS = 8
FOX_HEAD_DIM = 64
FOX_WIDTH = FOX_HEADS * FOX_HEAD_DIM
GDN_HEADS = 4
GDN_HEAD_DIM = 128
GDN_WIDTH = GDN_HEADS * GDN_HEAD_DIM
CONV_K = 4
MEM_HEADS = 4
MEM_HEAD_DIM = 128
MEM_WIDTH = MEM_HEADS * MEM_HEAD_DIM
N_BRANCH = 3
BRANCH_WIDTH = 512

IN_SIZES = (FOX_WIDTH, FOX_WIDTH, FOX_WIDTH, FOX_HEADS, FOX_WIDTH,
            GDN_WIDTH, GDN_WIDTH, GDN_WIDTH, GDN_HEADS, GDN_HEADS, GDN_WIDTH,
            MEM_WIDTH, MEM_WIDTH,
            N_BRANCH * D_MODEL)
N_IN = sum(IN_SIZES)

kernel_name = 'hybrid_fox_gdn_memory_gated_merge'


def _split_cols(z, sizes):
    idx = []
    acc = 0
    for s in sizes[:-1]:
        acc += s
        idx.append(acc)
    return jnp.split(z, idx, axis=-1)


def rms_norm(x, g):
    xf = x.astype(jnp.float32)
    y = xf * lax.rsqrt(jnp.mean(xf * xf, axis=-1, keepdims=True) + EPS)
    return (y * g.astype(jnp.float32)).astype(x.dtype)


def l2_normalize(x):
    return x * lax.rsqrt(jnp.sum(x * x, axis=-1, keepdims=True) + EPS)


def forgetting_attention(q, k, v, f_logit):
    B, S, H, d = q.shape
    log_f = jax.nn.log_sigmoid(f_logit.astype(jnp.float32))
    F = jnp.cumsum(log_f, axis=1).transpose(0, 2, 1)
    nb = S // Q_BLOCK
    pos = jnp.arange(S)
    q_blocks = q.reshape(B, nb, Q_BLOCK, H, d).swapaxes(0, 1)
    F_blocks = F.reshape(B, H, nb, Q_BLOCK).transpose(2, 0, 1, 3)
    p_blocks = pos.reshape(nb, Q_BLOCK)
    scale = d ** -0.5

    def block(args):
        q_blk, F_blk, p_blk = args
        s = jnp.einsum('bqhd,bkhd->bhqk', q_blk, k,
                       preferred_element_type=jnp.float32) * scale
        s = s + F_blk[..., :, None] - F[:, :, None, :]
        s = jnp.where(pos[None, None, None, :] <= p_blk[None, None, :, None], s, -jnp.inf)
        p = jax.nn.softmax(s, axis=-1)
        return jnp.einsum('bhqk,bkhd->bqhd', p.astype(v.dtype), v)

    o = lax.map(block, (q_blocks, F_blocks, p_blocks))
    return o.swapaxes(0, 1).reshape(B, S, H * d)


def causal_dwconv(x, w):
    K = w.shape[0]
    return lax.conv_general_dilated(
        x, w[:, None, :].astype(x.dtype), window_strides=(1,),
        padding=[(K - 1, 0)], dimension_numbers=('NWC', 'WIO', 'NWC'),
        feature_group_count=x.shape[-1])


def gated_delta_rule(q, k, v, g, beta):
    B, S, H, dk = q.shape
    dv = v.shape[-1]
    N = S // CHUNK
    C = CHUNK

    def chunks(t):
        t = t.reshape((B, N, C, H) + t.shape[3:])
        return jnp.moveaxis(t, 3, 1)

    qc, kc, vc = chunks(q), chunks(k), chunks(v)
    gc, bc = chunks(g), chunks(beta)
    G = jnp.cumsum(gc, axis=-1)
    tril_incl = jnp.tril(jnp.ones((C, C), dtype=bool))
    tril_strict = jnp.tril(jnp.ones((C, C), dtype=bool), -1)
    gamma = jnp.exp(jnp.where(tril_incl, G[..., :, None] - G[..., None, :], -jnp.inf))
    kb = kc * bc[..., None]
    A = jnp.where(tril_strict, jnp.einsum('bhncd,bhnsd->bhncs', kb, kc) * gamma, 0.0)
    eye = jnp.eye(C, dtype=A.dtype)
    rhs = jnp.concatenate([vc * bc[..., None], kb * jnp.exp(G)[..., None]], axis=-1)
    sol = lax.linalg.triangular_solve(A + eye, rhs, left_side=True, lower=True,
                                      unit_diagonal=True)
    u, w = sol[..., :dv], sol[..., dv:]
    a_qk = jnp.einsum('bhncd,bhnsd->bhncs', qc, kc) * gamma
    q_dec = qc * jnp.exp(G)[..., None]
    G_last = G[..., -1]
    k_dec = kc * jnp.exp(G_last[..., None] - G)[..., None]

    def step(state, inp):
        dq, kd, uu, ww, aqk, gl = inp
        v_new = uu - jnp.einsum('bhcd,bhde->bhce', ww, state)
        o = (jnp.einsum('bhcd,bhde->bhce', dq, state)
             + jnp.einsum('bhcs,bhse->bhce', aqk, v_new))
        state = state * jnp.exp(gl)[..., None, None] + jnp.einsum('bhcd,bhce->bhde', kd, v_new)
        return state, o

    xs = tuple(jnp.moveaxis(t, 2, 0) for t in (q_dec, k_dec, u, w, a_qk, G_last))
    state0 = jnp.zeros((B, H, dk, dv), jnp.float32)
    _, o = lax.scan(step, state0, xs)
    o = jnp.moveaxis(o, 0, 2)
    return jnp.moveaxis(o, 1, 3).reshape(B, S, H, dv)


def hybrid_layer(x, mem, norm_g, w_in, b_fg, b_merge, conv_w, a_log, dt_bias,
                 gdn_norm_g, mem_norm_g, w_mem_kv, w_branch, w_out):
    B, S, D = x.shape
    dt = x.dtype
    h = rms_norm(x, norm_g)
    z = h @ w_in
    (aq, ak, av, af, az, bq, bk, bv, ba, bb, bz, mq, mz, gates) = _split_cols(z, IN_SIZES)

    o_a = forgetting_attention(aq.reshape(B, S, FOX_HEADS, FOX_HEAD_DIM),
                               ak.reshape(B, S, FOX_HEADS, FOX_HEAD_DIM),
                               av.reshape(B, S, FOX_HEADS, FOX_HEAD_DIM),
                               af + b_fg)
    y_a = (o_a * jax.nn.silu(az)).astype(dt)

    qkv = jax.nn.silu(causal_dwconv(jnp.concatenate([bq, bk, bv], axis=-1), conv_w))
    gq, gk, gv = jnp.split(qkv.astype(jnp.float32), 3, axis=-1)
    gq = l2_normalize(gq.reshape(B, S, GDN_HEADS, GDN_HEAD_DIM)) * (GDN_HEAD_DIM ** -0.5)
    gk = l2_normalize(gk.reshape(B, S, GDN_HEADS, GDN_HEAD_DIM))
    gv = gv.reshape(B, S, GDN_HEADS, GDN_HEAD_DIM)
    g_log = -jnp.exp(a_log.astype(jnp.float32)) * jax.nn.softplus(
        ba.astype(jnp.float32) + dt_bias.astype(jnp.float32))
    beta = jax.nn.sigmoid(bb.astype(jnp.float32))
    o_b = gated_delta_rule(gq, gk, gv, g_log, beta)
    y_b = (rms_norm(o_b, gdn_norm_g).reshape(B, S, GDN_WIDTH) * jax.nn.silu(bz)).astype(dt)

    mem_n = rms_norm(mem, mem_norm_g)
    mk, mv = jnp.split(mem_n @ w_mem_kv, 2, axis=-1)
    M = mem.shape[1]
    mk = mk.reshape(B, M, MEM_HEADS, MEM_HEAD_DIM)
    mv = mv.reshape(B, M, MEM_HEADS, MEM_HEAD_DIM)
    s_m = jnp.einsum('bshd,bmhd->bhsm', mq.reshape(B, S, MEM_HEADS, MEM_HEAD_DIM), mk,
                     preferred_element_type=jnp.float32) * (MEM_HEAD_DIM ** -0.5)
    p_m = jax.nn.softmax(s_m, axis=-1)
    o_m = jnp.einsum('bhsm,bmhd->bshd', p_m.astype(mv.dtype), mv).reshape(B, S, MEM_WIDTH)
    y_m = (o_m * jax.nn.silu(mz)).astype(dt)

    ys = jnp.stack([y_a, y_b, y_m], axis=2)
    proj = jnp.einsum('bsnc,ncd->bsnd', ys, w_branch)
    gate = jax.nn.sigmoid(gates + b_merge).reshape(B, S, N_BRANCH, D)
    merged = jnp.sum(gate * proj, axis=2)
    return x + merged @ w_out


def _fwd_setup_inputs(seed: int = 0) -> dict:
    key = jax.random.key(seed)
    ks = jax.random.split(key, 16)
    f32 = jnp.float32
    x = jax.random.normal(ks[0], (BATCH, SEQ, D_MODEL), f32)
    mem = jax.random.normal(ks[1], (BATCH, N_MEM, D_MODEL), f32)
    norm_g = 1.0 + 0.02 * jax.random.normal(ks[2], (DEPTH, D_MODEL), f32)
    w_in = jax.random.normal(ks[3], (DEPTH, D_MODEL, N_IN), f32) * (D_MODEL ** -0.5)
    b_fg = 1.0 + 3.0 * jax.random.uniform(ks[4], (DEPTH, FOX_HEADS), f32)
    b_merge = 0.02 * jax.random.normal(ks[5], (DEPTH, N_BRANCH * D_MODEL), f32)
    conv_w = jax.random.normal(ks[6], (DEPTH, CONV_K, 3 * GDN_WIDTH), f32) * (CONV_K ** -0.5)
    a_log = jnp.log(jax.random.uniform(ks[7], (DEPTH, GDN_HEADS), f32, 1.0, 16.0))
    dt0 = jnp.exp(jax.random.uniform(ks[8], (DEPTH, GDN_HEADS), f32,
                                     float(np.log(1e-3)), float(np.log(1e-1))))
    dt_bias = dt0 + jnp.log(-jnp.expm1(-dt0))
    gdn_norm_g = 1.0 + 0.02 * jax.random.normal(ks[9], (DEPTH, GDN_HEAD_DIM), f32)
    mem_norm_g = 1.0 + 0.02 * jax.random.normal(ks[10], (DEPTH, D_MODEL), f32)
    w_mem_kv = jax.random.normal(ks[11], (DEPTH, D_MODEL, 2 * MEM_WIDTH), f32) * (D_MODEL ** -0.5)
    w_branch = jax.random.normal(ks[12], (DEPTH, N_BRANCH, BRANCH_WIDTH, D_MODEL), f32) * (BRANCH_WIDTH ** -0.5)
    w_out = jax.random.normal(ks[13], (DEPTH, D_MODEL, D_MODEL), f32) * (0.5 * D_MODEL ** -0.5)
    final_norm_g = 1.0 + 0.02 * jax.random.normal(ks[14], (D_MODEL,), f32)
    return {'x': x, 'mem': mem, 'norm_g': norm_g, 'w_in': w_in, 'b_fg': b_fg,
            'b_merge': b_merge, 'conv_w': conv_w, 'a_log': a_log, 'dt_bias': dt_bias,
            'gdn_norm_g': gdn_norm_g, 'mem_norm_g': mem_norm_g, 'w_mem_kv': w_mem_kv,
            'w_branch': w_branch, 'w_out': w_out, 'final_norm_g': final_norm_g}


def _fwd_reference(x, mem, norm_g, w_in, b_fg, b_merge, conv_w, a_log, dt_bias,
              gdn_norm_g, mem_norm_g, w_mem_kv, w_branch, w_out, final_norm_g):
    for l in range(DEPTH):
        x = hybrid_layer(x, mem, norm_g[l], w_in[l], b_fg[l], b_merge[l], conv_w[l],
                         a_log[l], dt_bias[l], gdn_norm_g[l], mem_norm_g[l],
                         w_mem_kv[l], w_branch[l], w_out[l])
    return rms_norm(x, final_norm_g)


import jax as _jax
import jax.numpy as _jnp

TWIN_FORMAT = 'train_step'
FWD_PARAMS = ['x', 'mem', 'norm_g', 'w_in', 'b_fg', 'b_merge', 'conv_w', 'a_log', 'dt_bias', 'gdn_norm_g', 'mem_norm_g', 'w_mem_kv', 'w_branch', 'w_out', 'final_norm_g']
TWIN_WEIGHTS = ['norm_g', 'w_in', 'b_fg', 'b_merge', 'conv_w', 'a_log', 'dt_bias', 'gdn_norm_g', 'mem_norm_g', 'w_mem_kv', 'w_branch', 'w_out', 'final_norm_g']
TWIN_DIFF_INPUT = 'x'
TWIN_INPUTS = ['x', 'mem', 'norm_g', 'w_in', 'b_fg', 'b_merge', 'conv_w', 'a_log', 'dt_bias', 'gdn_norm_g', 'mem_norm_g', 'w_mem_kv', 'w_branch', 'w_out', 'final_norm_g', 'loss_target', 'm_norm_g', 'm_w_in', 'm_b_fg', 'm_b_merge', 'm_conv_w', 'm_a_log', 'm_dt_bias', 'm_gdn_norm_g', 'm_mem_norm_g', 'm_w_mem_kv', 'm_w_branch', 'm_w_out', 'm_final_norm_g', 'v_norm_g', 'v_w_in', 'v_b_fg', 'v_b_merge', 'v_conv_w', 'v_a_log', 'v_dt_bias', 'v_gdn_norm_g', 'v_mem_norm_g', 'v_w_mem_kv', 'v_w_branch', 'v_w_out', 'v_final_norm_g']
TWIN_OUTPUTS = ['loss', 'grad_x', 'grad_norm_g', 'grad_w_in', 'grad_b_fg', 'grad_b_merge', 'grad_conv_w', 'grad_a_log', 'grad_dt_bias', 'grad_gdn_norm_g', 'grad_mem_norm_g', 'grad_w_mem_kv', 'grad_w_branch', 'grad_w_out', 'grad_final_norm_g', 'delta_norm_g', 'delta_w_in', 'delta_b_fg', 'delta_b_merge', 'delta_conv_w', 'delta_a_log', 'delta_dt_bias', 'delta_gdn_norm_g', 'delta_mem_norm_g', 'delta_w_mem_kv', 'delta_w_branch', 'delta_w_out', 'delta_final_norm_g', 'new_m_norm_g', 'new_m_w_in', 'new_m_b_fg', 'new_m_b_merge', 'new_m_conv_w', 'new_m_a_log', 'new_m_dt_bias', 'new_m_gdn_norm_g', 'new_m_mem_norm_g', 'new_m_w_mem_kv', 'new_m_w_branch', 'new_m_w_out', 'new_m_final_norm_g', 'new_v_norm_g', 'new_v_w_in', 'new_v_b_fg', 'new_v_b_merge', 'new_v_conv_w', 'new_v_a_log', 'new_v_dt_bias', 'new_v_gdn_norm_g', 'new_v_mem_norm_g', 'new_v_w_mem_kv', 'new_v_w_branch', 'new_v_w_out', 'new_v_final_norm_g']
TWIN_LEAF_KINDS = {'loss': 'loss', 'grad_x': 'grad_x', 'grad_norm_g': 'grad_w', 'grad_w_in': 'grad_w', 'grad_b_fg': 'grad_w', 'grad_b_merge': 'grad_w', 'grad_conv_w': 'grad_w', 'grad_a_log': 'grad_w', 'grad_dt_bias': 'grad_w', 'grad_gdn_norm_g': 'grad_w', 'grad_mem_norm_g': 'grad_w', 'grad_w_mem_kv': 'grad_w', 'grad_w_branch': 'grad_w', 'grad_w_out': 'grad_w', 'grad_final_norm_g': 'grad_w', 'delta_norm_g': 'delta_w', 'delta_w_in': 'delta_w', 'delta_b_fg': 'delta_w', 'delta_b_merge': 'delta_w', 'delta_conv_w': 'delta_w', 'delta_a_log': 'delta_w', 'delta_dt_bias': 'delta_w', 'delta_gdn_norm_g': 'delta_w', 'delta_mem_norm_g': 'delta_w', 'delta_w_mem_kv': 'delta_w', 'delta_w_branch': 'delta_w', 'delta_w_out': 'delta_w', 'delta_final_norm_g': 'delta_w', 'new_m_norm_g': 'new_m', 'new_m_w_in': 'new_m', 'new_m_b_fg': 'new_m', 'new_m_b_merge': 'new_m', 'new_m_conv_w': 'new_m', 'new_m_a_log': 'new_m', 'new_m_dt_bias': 'new_m', 'new_m_gdn_norm_g': 'new_m', 'new_m_mem_norm_g': 'new_m', 'new_m_w_mem_kv': 'new_m', 'new_m_w_branch': 'new_m', 'new_m_w_out': 'new_m', 'new_m_final_norm_g': 'new_m', 'new_v_norm_g': 'new_v', 'new_v_w_in': 'new_v', 'new_v_b_fg': 'new_v', 'new_v_b_merge': 'new_v', 'new_v_conv_w': 'new_v', 'new_v_a_log': 'new_v', 'new_v_dt_bias': 'new_v', 'new_v_gdn_norm_g': 'new_v', 'new_v_mem_norm_g': 'new_v', 'new_v_w_mem_kv': 'new_v', 'new_v_w_branch': 'new_v', 'new_v_w_out': 'new_v', 'new_v_final_norm_g': 'new_v'}


def _forward(args):
    return _fwd_reference(*[args[k] for k in FWD_PARAMS])


def _output_shape():
    def fwd():
        inp = _fwd_setup_inputs(0)
        return _fwd_reference(*[inp[k] for k in FWD_PARAMS])
    out = _jax.eval_shape(fwd)
    return out.shape, out.dtype

N_MICROBATCH = 1
ADAM_LR = 0.001
ADAM_B1 = 0.9
ADAM_B2 = 0.999
ADAM_EPS = 1e-08
ADAM_WD = 0.01
ADAM_STEP = 10
PER_EXAMPLE_BATCH_AXIS = {'x': 0, 'mem': 0, 'loss_target': 0}
SHARED_INPUTS = []
_WEIGHT_DTYPES = {'norm_g': _jnp.float32, 'w_in': _jnp.float32, 'b_fg': _jnp.float32, 'b_merge': _jnp.float32, 'conv_w': _jnp.float32, 'a_log': _jnp.float32, 'dt_bias': _jnp.float32, 'gdn_norm_g': _jnp.float32, 'mem_norm_g': _jnp.float32, 'w_mem_kv': _jnp.float32, 'w_branch': _jnp.float32, 'w_out': _jnp.float32, 'final_norm_g': _jnp.float32}
MOMENT_SCALE = {'norm_g': 1.046490e-01, 'w_in': 3.729752e-02, 'b_fg': 1.305492e-01, 'b_merge': 1.419289e-02, 'conv_w': 6.169137e-02, 'a_log': 2.744179e-01, 'dt_bias': 2.697670e-01, 'gdn_norm_g': 1.618305e-01, 'mem_norm_g': 9.969253e-03, 'w_mem_kv': 8.509514e-03, 'w_branch': 3.530436e-02, 'w_out': 1.231382e-01, 'final_norm_g': 1.279684e+02}


def _to_microbatches(a, axis):
    t = _jnp.moveaxis(a, axis, 0)
    t = t.reshape((N_MICROBATCH, t.shape[0] // N_MICROBATCH) + t.shape[1:])
    return _jnp.moveaxis(t, 1, axis + 1)


def setup_inputs(seed: int = 0) -> dict:
    inp = _fwd_setup_inputs(seed)
    key = _jax.random.fold_in(_jax.random.key(seed), 7919)
    shape, _ = _output_shape()
    out = dict(inp)
    out["loss_target"] = _jax.random.normal(_jax.random.fold_in(key, 0), shape, _jnp.float32)
    for i, name in enumerate(TWIN_WEIGHTS):
        w = inp[name].astype(_jnp.float32)
        if MOMENT_SCALE is None:
            s = _jnp.sqrt(_jnp.mean(_jnp.square(w)) + 1e-30)
        else:
            s = MOMENT_SCALE[name]
        km, kv = _jax.random.split(_jax.random.fold_in(key, i + 1))
        out[name] = w
        out["m_" + name] = s * _jax.random.normal(km, w.shape, _jnp.float32)
        out["v_" + name] = (s * s) * _jax.random.uniform(kv, w.shape, _jnp.float32, 0.5, 1.5)
    if N_MICROBATCH > 1:
        for name, axis in PER_EXAMPLE_BATCH_AXIS.items():
            out[name] = _to_microbatches(out[name], axis)
    return {'x': out['x'], 'mem': out['mem'], 'norm_g': out['norm_g'], 'w_in': out['w_in'], 'b_fg': out['b_fg'], 'b_merge': out['b_merge'], 'conv_w': out['conv_w'], 'a_log': out['a_log'], 'dt_bias': out['dt_bias'], 'gdn_norm_g': out['gdn_norm_g'], 'mem_norm_g': out['mem_norm_g'], 'w_mem_kv': out['w_mem_kv'], 'w_branch': out['w_branch'], 'w_out': out['w_out'], 'final_norm_g': out['final_norm_g'], 'loss_target': out['loss_target'], 'm_norm_g': out['m_norm_g'], 'm_w_in': out['m_w_in'], 'm_b_fg': out['m_b_fg'], 'm_b_merge': out['m_b_merge'], 'm_conv_w': out['m_conv_w'], 'm_a_log': out['m_a_log'], 'm_dt_bias': out['m_dt_bias'], 'm_gdn_norm_g': out['m_gdn_norm_g'], 'm_mem_norm_g': out['m_mem_norm_g'], 'm_w_mem_kv': out['m_w_mem_kv'], 'm_w_branch': out['m_w_branch'], 'm_w_out': out['m_w_out'], 'm_final_norm_g': out['m_final_norm_g'], 'v_norm_g': out['v_norm_g'], 'v_w_in': out['v_w_in'], 'v_b_fg': out['v_b_fg'], 'v_b_merge': out['v_b_merge'], 'v_conv_w': out['v_conv_w'], 'v_a_log': out['v_a_log'], 'v_dt_bias': out['v_dt_bias'], 'v_gdn_norm_g': out['v_gdn_norm_g'], 'v_mem_norm_g': out['v_mem_norm_g'], 'v_w_mem_kv': out['v_w_mem_kv'], 'v_w_branch': out['v_w_branch'], 'v_w_out': out['v_w_out'], 'v_final_norm_g': out['v_final_norm_g']}


def _loss(weights, diff, rest, loss_target):
    with _jax.named_scope("forward"):
        args = {**rest, TWIN_DIFF_INPUT: diff, **{k: w.astype(_WEIGHT_DTYPES[k]) for k, w in weights.items()}}
        y = _forward(args)
    with _jax.named_scope("loss_head"):
        err = _jnp.square(y.astype(_jnp.float32) - loss_target)
        return 0.5 * _jnp.sum(_jnp.mean(err, axis=-1)) if err.ndim else 0.5 * err


def _adamw(w, g, m, v):
    m = ADAM_B1 * m + (1.0 - ADAM_B1) * g
    v = ADAM_B2 * v + (1.0 - ADAM_B2) * _jnp.square(g)
    m_hat = m / (1.0 - ADAM_B1 ** ADAM_STEP)
    v_hat = v / (1.0 - ADAM_B2 ** ADAM_STEP)
    delta = -ADAM_LR * (m_hat / (_jnp.sqrt(v_hat) + ADAM_EPS) + ADAM_WD * w)
    return delta, m, v


def reference(x, mem, norm_g, w_in, b_fg, b_merge, conv_w, a_log, dt_bias, gdn_norm_g, mem_norm_g, w_mem_kv, w_branch, w_out, final_norm_g, loss_target, m_norm_g, m_w_in, m_b_fg, m_b_merge, m_conv_w, m_a_log, m_dt_bias, m_gdn_norm_g, m_mem_norm_g, m_w_mem_kv, m_w_branch, m_w_out, m_final_norm_g, v_norm_g, v_w_in, v_b_fg, v_b_merge, v_conv_w, v_a_log, v_dt_bias, v_gdn_norm_g, v_mem_norm_g, v_w_mem_kv, v_w_branch, v_w_out, v_final_norm_g):
    given = dict(x=x, mem=mem, norm_g=norm_g, w_in=w_in, b_fg=b_fg, b_merge=b_merge, conv_w=conv_w, a_log=a_log, dt_bias=dt_bias, gdn_norm_g=gdn_norm_g, mem_norm_g=mem_norm_g, w_mem_kv=w_mem_kv, w_branch=w_branch, w_out=w_out, final_norm_g=final_norm_g, loss_target=loss_target, m_norm_g=m_norm_g, m_w_in=m_w_in, m_b_fg=m_b_fg, m_b_merge=m_b_merge, m_conv_w=m_conv_w, m_a_log=m_a_log, m_dt_bias=m_dt_bias, m_gdn_norm_g=m_gdn_norm_g, m_mem_norm_g=m_mem_norm_g, m_w_mem_kv=m_w_mem_kv, m_w_branch=m_w_branch, m_w_out=m_w_out, m_final_norm_g=m_final_norm_g, v_norm_g=v_norm_g, v_w_in=v_w_in, v_b_fg=v_b_fg, v_b_merge=v_b_merge, v_conv_w=v_conv_w, v_a_log=v_a_log, v_dt_bias=v_dt_bias, v_gdn_norm_g=v_gdn_norm_g, v_mem_norm_g=v_mem_norm_g, v_w_mem_kv=v_w_mem_kv, v_w_branch=v_w_branch, v_w_out=v_w_out, v_final_norm_g=v_final_norm_g)
    weights = {n: given[n] for n in TWIN_WEIGHTS}
    shared = {n: given[n] for n in SHARED_INPUTS}
    per_example = {n: given[n] for n in ['x', 'mem']}
    grad_fn = _jax.value_and_grad(_loss, argnums=(0, 1))

    def one_microbatch(ex, loss_target):
        ex = dict(ex)
        diff = ex.pop(TWIN_DIFF_INPUT)
        return grad_fn(weights, diff, {**shared, **ex}, loss_target)

    if N_MICROBATCH == 1:
        loss, (grad_w, grad_x) = one_microbatch(per_example, given["loss_target"])
    else:
        def body(carry, xs):
            loss_sum, grad_sum = carry
            l_k, (gw_k, gx_k) = one_microbatch(xs[0], xs[1])
            with _jax.named_scope("update"):
                return (loss_sum + l_k, _jax.tree.map(_jnp.add, grad_sum, gw_k)), gx_k

        init = (_jnp.zeros((), _jnp.float32), _jax.tree.map(_jnp.zeros_like, weights))
        (loss, grad_w), grad_x = _jax.lax.scan(body, init, (per_example, given["loss_target"]))
    with _jax.named_scope("update"):
        delta_w, new_m, new_v = {}, {}, {}
        for n in TWIN_WEIGHTS:
            delta_w[n], new_m[n], new_v[n] = _adamw(weights[n], grad_w[n], given["m_" + n], given["v_" + n])
    return (loss, grad_x, *[grad_w[n] for n in TWIN_WEIGHTS], *[delta_w[n] for n in TWIN_WEIGHTS],
            *[new_m[n] for n in TWIN_WEIGHTS], *[new_v[n] for n in TWIN_WEIGHTS])
```

```python
import functools

import jax
import jax.numpy as jnp
from jax import lax
from jax.experimental import pallas as pl
from jax.experimental.pallas import tpu as pltpu

F32 = jnp.float32
BF16 = jnp.bfloat16
HIGHEST = lax.Precision.HIGHEST
MESH_ID = pl.DeviceIdType.MESH

D_MODEL = 1024
DEPTH = 2
CHUNK = 64
EPS = 1e-6
FOX_HEADS, FOX_DIM = 8, 64
GDN_HEADS, GDN_DIM = 4, 128
MEM_HEADS, MEM_DIM = 4, 128
WIDTH = 512
N_BRANCH = 3
N_IN = 8208
N_AL = 8320
N_CHIPS = 4
NEG = -1e30

ADAM_LR, ADAM_B1, ADAM_B2, ADAM_EPS, ADAM_WD, ADAM_STEP = 0.001, 0.9, 0.999, 1e-08, 0.01, 10

CB_GATES = 0
CB_BQKV = 2
CB_AQ, CB_AK, CB_AV, CB_AZ, CB_BZ, CB_MQ, CB_MZ = 9, 10, 11, 12, 13, 14, 15
CB_SMALL = 64
LANE_AF, LANE_BA, LANE_BB = 0, 8, 12

NN = ((1,), (0,))
NT = ((1,), (1,))
TN = ((0,), (0,))

VMEM_LIMIT_BYTES = 56 * 1024 * 1024


def _dot(a, b, dims=NN, prec=None):
    return lax.dot_general(a, b, (dims, ((), ())), preferred_element_type=F32, precision=prec)


def _bdot(a, b, ca, cb, prec=None):
    return lax.dot_general(a, b, (((ca,), (cb,)), ((0,), (0,))), preferred_element_type=F32,
                           precision=prec)


def _b16(a):
    return a.astype(BF16)


def _sigmoid(x):
    return 1.0 / (1.0 + jnp.exp(-x))


def _softplus(x):
    return jnp.maximum(x, 0.0) + jnp.log(1.0 + jnp.exp(-jnp.abs(x)))


def _silu_and_grad(x):
    s = _sigmoid(x)
    return x * s, s * (1.0 + x * (1.0 - s))


def _params(semantics):
    return pltpu.CompilerParams(dimension_semantics=semantics, vmem_limit_bytes=VMEM_LIMIT_BYTES)


def _rows(a, ts):
    nd = a.ndim
    return (a, (ts,) + a.shape[1:], lambda i, nd=nd: (i,) + (0,) * (nd - 1))


def _cols(a, ts, width, cb):
    return (a, (ts, width), lambda i, cb=cb: (i, cb))


def _full(a):
    nd = a.ndim
    return (a, a.shape, lambda i, nd=nd: (0,) * nd)


def _orow(S, tail, dtype, ts):
    nd = 1 + len(tail)
    return ((S,) + tuple(tail), dtype, (ts,) + tuple(tail), lambda i, nd=nd: (i,) + (0,) * (nd - 1))


def _oacc(shape, dtype):
    nd = len(shape)
    return (tuple(shape), dtype, tuple(shape), lambda i, nd=nd: (0,) * nd)


def _tiled(name, body, n_steps, ins, outs, scratch=(), reverse=False):
    def rev(imap):
        if not reverse:
            return imap
        return lambda i: imap(n_steps - 1 - i)

    in_specs = [pl.BlockSpec(blk, rev(imap)) for (_, blk, imap) in ins]
    out_specs = [pl.BlockSpec(blk, rev(imap)) for (_, _, blk, imap) in outs]
    out_shape = [jax.ShapeDtypeStruct(shape, dt) for (shape, dt, _, _) in outs]
    n_in, n_out = len(ins), len(outs)

    def kern(*refs):
        step = pl.program_id(0)
        t = (n_steps - 1 - step) if reverse else step
        body(t, step == 0, refs[:n_in], refs[n_in:n_in + n_out], refs[n_in + n_out:])

    res = pl.pallas_call(
        kern, name=name, grid=(n_steps,), in_specs=in_specs, out_specs=out_specs,
        out_shape=out_shape, scratch_shapes=list(scratch),
        compiler_params=_params(("arbitrary",)),
    )(*[a for (a, _, _) in ins])
    return res


def _pick(n, pref):
    if n <= pref:
        return n
    best = None
    for t in range(128, pref + 1, 128):
        if n % t == 0:
            best = t
    assert best is not None, (n, pref)
    return best


def _mm(name, a, b, ta=False, tb=False, out_dtype=F32, tm=512, tn=512, tk=512):
    if ta:
        K, M = a.shape
    else:
        M, K = a.shape
    if tb:
        N, K2 = b.shape
    else:
        K2, N = b.shape
    assert K == K2, (a.shape, b.shape, ta, tb)
    tm, tn, tk = _pick(M, tm), _pick(N, tn), _pick(K, tk)
    nk = K // tk
    a_spec = (pl.BlockSpec((tk, tm), lambda i, j, k: (k, i)) if ta
              else pl.BlockSpec((tm, tk), lambda i, j, k: (i, k)))
    b_spec = (pl.BlockSpec((tn, tk), lambda i, j, k: (j, k)) if tb
              else pl.BlockSpec((tk, tn), lambda i, j, k: (k, j)))
    dims = ((0,) if ta else (1,), (1,) if tb else (0,))

    def kern(a_ref, b_ref, o_ref, acc_ref):
        k = pl.program_id(2)

        @pl.when(k == 0)
        def _():
            acc_ref[...] = jnp.zeros_like(acc_ref)

        acc_ref[...] += _dot(_b16(a_ref[...]), _b16(b_ref[...]), dims)

        @pl.when(k == nk - 1)
        def _():
            o_ref[...] = acc_ref[...].astype(o_ref.dtype)

    return pl.pallas_call(
        kern, name=name, grid=(M // tm, N // tn, nk), in_specs=[a_spec, b_spec],
        out_specs=pl.BlockSpec((tm, tn), lambda i, j, k: (i, j)),
        out_shape=jax.ShapeDtypeStruct((M, N), out_dtype),
        scratch_shapes=[pltpu.VMEM((tm, tn), F32)],
        compiler_params=_params(("parallel", "parallel", "arbitrary")),
    )(a, b)


def _rms_fwd(name, x, g, ts):
    S, D = x.shape

    def body(t, first, ins, outs, scratch):
        x_ref, g_ref = ins
        h_ref, r_ref = outs
        xv = x_ref[...]
        r = lax.rsqrt(jnp.mean(xv * xv, axis=1, keepdims=True) + EPS)
        h_ref[...] = (xv * r * g_ref[...]).astype(h_ref.dtype)
        r_ref[...] = r

    return _tiled(name, body, S // ts, [_rows(x, ts), _full(g.reshape(1, D))],
                  [_orow(S, (D,), BF16, ts), _orow(S, (1,), F32, ts)])


def _rms_bwd(name, dh, x, rstd, g, dres, ts):
    S, D = x.shape

    def body(t, first, ins, outs, scratch):
        dh_ref, x_ref, r_ref, g_ref, dres_ref = ins
        dx_ref, dg_ref = outs
        r = r_ref[...]
        xh = x_ref[...] * r
        dhv = dh_ref[...]
        dxh = dhv * g_ref[...]
        dx_ref[...] = dres_ref[...] + r * (dxh - xh * jnp.mean(dxh * xh, axis=1, keepdims=True))

        @pl.when(first)
        def _():
            dg_ref[...] = jnp.zeros_like(dg_ref)

        dg_ref[0:1, :] += jnp.sum(dhv * xh, axis=0, keepdims=True)

    dx, dg = _tiled(name, body, S // ts,
                    [_rows(dh, ts), _rows(x, ts), _rows(rstd, ts), _full(g.reshape(1, D)), _rows(dres, ts)],
                    [_orow(S, (D,), F32, ts), _oacc((8, D), F32)])
    return dx, dg[0]


def _loss_head(x, g, target, ts):
    S, D = x.shape

    def body(t, first, ins, outs, scratch):
        x_ref, g_ref, tgt_ref = ins
        dx_ref, dg_ref, loss_ref = outs
        xv = x_ref[...]
        gv = g_ref[...]
        r = lax.rsqrt(jnp.mean(xv * xv, axis=1, keepdims=True) + EPS)
        xh = xv * r
        err = xh * gv - tgt_ref[...]
        dy = err * (1.0 / D)
        dxh = dy * gv
        dx_ref[...] = r * (dxh - xh * jnp.mean(dxh * xh, axis=1, keepdims=True))

        @pl.when(first)
        def _():
            dg_ref[...] = jnp.zeros_like(dg_ref)
            loss_ref[...] = jnp.zeros_like(loss_ref)

        dg_ref[0:1, :] += jnp.sum(dy * xh, axis=0, keepdims=True)
        per_lane = jnp.sum(err * err, axis=0, keepdims=True)
        loss_ref[0:1, :] += per_lane * (0.5 / D)

    dx, dg, loss = _tiled("loss_head", body, S // ts,
                          [_rows(x, ts), _full(g.reshape(1, D)), _rows(target, ts)],
                          [_orow(S, (D,), F32, ts), _oacc((8, D), F32), _oacc((8, D), F32)])
    return dx, dg[0], loss[0]


def _scan_rows(x, length, seg, reverse=False):
    row = lax.broadcasted_iota(jnp.int32, x.shape, 0) % seg
    k = 1
    while k < seg:
        if reverse:
            x = x + jnp.where(row < seg - k, pltpu.roll(x, length - k, 0), 0.0)
        else:
            x = x + jnp.where(row >= k, pltpu.roll(x, k, 0), 0.0)
        k *= 2
    return x


def _fox_decay(z, b_fg128, ts):
    S = z.shape[0]

    def body(t, first, ins, outs, scratch):
        zs_ref, b_ref = ins
        (f_ref,) = outs
        (carry,) = scratch

        @pl.when(first)
        def _():
            carry[...] = jnp.zeros_like(carry)

        logf = -_softplus(-(zs_ref[...] + b_ref[...]))
        run = _scan_rows(logf, ts, ts) + carry[0:1, :]
        f_ref[...] = run
        carry[0:1, :] = run[ts - 1:ts, :]

    (f128,) = _tiled("fox_decay", body, S // ts,
                     [_cols(z, ts, 128, CB_SMALL), _full(b_fg128)],
                     [_orow(S, (128,), F32, ts)], scratch=[pltpu.VMEM((8, 128), F32)])
    return f128


def _fox_decay_bwd(dFk128, dFq128, z, b_fg128, ts):
    S = z.shape[0]

    def body(t, first, ins, outs, scratch):
        dfk_ref, dfq_ref, zs_ref, b_ref = ins
        daf_ref, db_ref = outs
        (carry,) = scratch

        @pl.when(first)
        def _():
            carry[...] = jnp.zeros_like(carry)
            db_ref[...] = jnp.zeros_like(db_ref)

        run = _scan_rows(dfk_ref[...] + dfq_ref[...], ts, ts, reverse=True) + carry[0:1, :]
        carry[0:1, :] = run[0:1, :]
        daf = run * _sigmoid(-(zs_ref[...] + b_ref[...]))
        daf_ref[...] = daf
        db_ref[0:1, :] += jnp.sum(daf, axis=0, keepdims=True)

    daf, db = _tiled("fox_decay_bwd", body, S // ts,
                     [_rows(dFk128, ts), _rows(dFq128, ts), _cols(z, ts, 128, CB_SMALL), _full(b_fg128)],
                     [_orow(S, (128,), F32, ts), _oacc((8, 128), F32)],
                     scratch=[pltpu.VMEM((8, 128), F32)], reverse=True)
    return daf, db[0]


def _fox_fwd(q, kT, v, fq_col, fk_row, tq):
    H, S, d = q.shape
    tk = tq
    scale = d ** -0.5

    def kern(q_ref, kT_ref, v_ref, fq_ref, fk_ref, o_ref, lse_ref):
        i = pl.program_id(1)
        qb = q_ref[...] * scale
        fq = fq_ref[...]

        def step(j, carry, masked):
            m, l, acc = carry
            off = pl.multiple_of(j * tk, tk)
            s = _dot(qb, kT_ref[:, pl.ds(off, tk)])
            s = s + (fq - fk_ref[:, pl.ds(off, tk)])
            if masked:
                r = lax.broadcasted_iota(jnp.int32, (tq, tk), 0)
                c = lax.broadcasted_iota(jnp.int32, (tq, tk), 1)
                s = jnp.where(c <= r, s, NEG)
            m_new = jnp.maximum(m, jnp.max(s, axis=1, keepdims=True))
            alpha = jnp.exp(m - m_new)
            p = jnp.exp(s - m_new)
            l = alpha * l + jnp.sum(p, axis=1, keepdims=True)
            acc = alpha * acc + _dot(_b16(p), v_ref[pl.ds(off, tk), :])
            return m_new, l, acc

        init = (jnp.full((tq, 1), NEG, F32), jnp.zeros((tq, 1), F32), jnp.zeros((tq, d), F32))
        carry = lax.fori_loop(0, i, lambda j, c: step(j, c, False), init)
        m, l, acc = step(i, carry, True)
        o_ref[...] = acc / l
        lse_ref[...] = m + jnp.log(l)

    return pl.pallas_call(
        kern, name="fox_fwd", grid=(H, S // tq),
        in_specs=[pl.BlockSpec((None, tq, d), lambda h, i: (h, i, 0)),
                  pl.BlockSpec((None, d, S), lambda h, i: (h, 0, 0)),
                  pl.BlockSpec((None, S, d), lambda h, i: (h, 0, 0)),
                  pl.BlockSpec((None, tq, 1), lambda h, i: (h, i, 0)),
                  pl.BlockSpec((None, 1, S), lambda h, i: (h, 0, 0))],
        out_specs=[pl.BlockSpec((None, tq, d), lambda h, i: (h, i, 0)),
                   pl.BlockSpec((None, tq, 1), lambda h, i: (h, i, 0))],
        out_shape=[jax.ShapeDtypeStruct((H, S, d), F32), jax.ShapeDtypeStruct((H, S, 1), F32)],
        compiler_params=_params(("parallel", "arbitrary")),
    )(q, kT, v, fq_col, fk_row)


def _fox_bwd(q, k, kT, v, do, fq_row, fk_col, lse_row, delta_row, tq):
    H, S, d = q.shape
    tk = tq
    nq = S // tq
    scale = d ** -0.5

    def kern(q_ref, k_ref, kT_ref, v_ref, do_ref, fq_ref, fk_ref, lse_ref, dl_ref,
             dqT_ref, dk_ref, dv_ref, dfk_ref, dfq_ref):
        j = pl.program_id(1)

        @pl.when(j == 0)
        def _():
            dqT_ref[...] = jnp.zeros_like(dqT_ref)
            dfq_ref[...] = jnp.zeros_like(dfq_ref)

        kb = k_ref[...]
        kTb = kT_ref[...]
        vb = v_ref[...]
        fk = fk_ref[...]

        def step(i, carry, masked):
            dk, dv, dfk = carry
            off = pl.multiple_of(i * tq, tq)
            qi = q_ref[pl.ds(off, tq), :] * scale
            doi = do_ref[pl.ds(off, tq), :]
            sT = _dot(kb, qi, NT) + (fq_ref[:, pl.ds(off, tq)] - fk)
            if masked:
                r = lax.broadcasted_iota(jnp.int32, (tk, tq), 0)
                c = lax.broadcasted_iota(jnp.int32, (tk, tq), 1)
                sT = jnp.where(r <= c, sT, NEG)
            pT = jnp.exp(sT - lse_ref[:, pl.ds(off, tq)])
            dv = dv + _dot(_b16(pT), doi)
            dpT = _dot(vb, doi, NT)
            dsT = pT * (dpT - dl_ref[:, pl.ds(off, tq)])
            dfk = dfk - jnp.sum(dsT, axis=1, keepdims=True)
            dfq_ref[:, pl.ds(off, tq)] += jnp.sum(dsT, axis=0, keepdims=True)
            dsb = _b16(dsT)
            dk = dk + _dot(dsb, qi)
            dqT_ref[:, pl.ds(off, tq)] += _dot(kTb, dsb) * scale
            return dk, dv, dfk

        init = (jnp.zeros((tk, d), F32), jnp.zeros((tk, d), F32), jnp.zeros((tk, 1), F32))
        carry = step(j, init, True)
        dk, dv, dfk = lax.fori_loop(j + 1, nq, lambda i, c: step(i, c, False), carry)
        dk_ref[...] = dk
        dv_ref[...] = dv
        dfk_ref[...] = dfk

    tile = lambda h, j: (h, j, 0)
    whole = lambda h, j: (h, 0, 0)
    return pl.pallas_call(
        kern, name="fox_bwd", grid=(H, S // tk),
        in_specs=[pl.BlockSpec((None, S, d), whole),
                  pl.BlockSpec((None, tk, d), tile),
                  pl.BlockSpec((None, d, tk), lambda h, j: (h, 0, j)),
                  pl.BlockSpec((None, tk, d), tile),
                  pl.BlockSpec((None, S, d), whole),
                  pl.BlockSpec((None, 1, S), whole),
                  pl.BlockSpec((None, tk, 1), tile),
                  pl.BlockSpec((None, 1, S), whole),
                  pl.BlockSpec((None, 1, S), whole)],
        out_specs=[pl.BlockSpec((None, d, S), whole),
                   pl.BlockSpec((None, tk, d), tile),
                   pl.BlockSpec((None, tk, d), tile),
                   pl.BlockSpec((None, tk, 1), tile),
                   pl.BlockSpec((None, 1, S), whole)],
        out_shape=[jax.ShapeDtypeStruct((H, d, S), F32), jax.ShapeDtypeStruct((H, S, d), F32),
                   jax.ShapeDtypeStruct((H, S, d), F32), jax.ShapeDtypeStruct((H, S, 1), F32),
                   jax.ShapeDtypeStruct((H, 1, S), F32)],
        compiler_params=_params(("parallel", "arbitrary")),
    )(q, k, kT, v, do, fq_row, fk_col, lse_row, delta_row)


def _heads_major(a, H, d):
    S = a.shape[0]
    return a.reshape(S, H, d).transpose(1, 0, 2)


def _heads_minor(a):
    H, S, d = a.shape
    return a.transpose(1, 0, 2).reshape(S, H * d)


def _lane_pick(x128, lane):
    return x128[:, lane:lane + 1]


def _l2_fwd(y):
    return lax.rsqrt(jnp.sum(y * y, axis=1, keepdims=True) + EPS)


def _gdn_prep(z, conv_w, a128, dt128, ts):
    S = z.shape[0]
    C3 = 3 * WIDTH
    hb = ts // 8

    def body(t, first, ins, outs, scratch):
        x_ref, halo_ref, zs_ref, w_ref, a_ref, dt_ref = ins
        qkv_ref, c_ref, gb_ref = outs
        halo = jnp.where(t > 0, halo_ref[...], 0.0)
        xe = jnp.concatenate([halo, x_ref[...]], axis=0)
        w = w_ref[...]
        c = w[3:4, :] * xe[8:, :]
        for back in (1, 2, 3):
            c = c + w[3 - back:4 - back, :] * pltpu.roll(xe, back, 0)[8:, :]
        c_ref[...] = c
        y = c * _sigmoid(c)
        for h in range(GDN_HEADS):
            lo = h * GDN_DIM
            yq = y[:, lo:lo + GDN_DIM]
            qkv_ref[:, lo:lo + GDN_DIM] = yq * (_l2_fwd(yq) * (GDN_DIM ** -0.5))
            yk = y[:, WIDTH + lo:WIDTH + lo + GDN_DIM]
            qkv_ref[:, WIDTH + lo:WIDTH + lo + GDN_DIM] = yk * _l2_fwd(yk)
        qkv_ref[:, 2 * WIDTH:] = y[:, 2 * WIDTH:]
        zs = zs_ref[...]
        lane = lax.broadcasted_iota(jnp.int32, zs.shape, 1)
        g = -jnp.exp(a_ref[...]) * _softplus(zs + dt_ref[...])
        G = _scan_rows(g, ts, CHUNK)
        beta = _sigmoid(zs)
        out = jnp.where(lane < 8, pltpu.roll(g, 128 - LANE_BA, 1), jnp.where(lane < LANE_BB, G, beta))
        gb_ref[...] = out

    x_in = (z, (ts, C3), lambda i: (i, CB_BQKV))
    halo_in = (z, (8, C3), lambda i: (jnp.maximum(i * hb - 1, 0), CB_BQKV))
    return _tiled("gdn_prep", body, S // ts,
                  [x_in, halo_in, _cols(z, ts, 128, CB_SMALL), _full(conv_w), _full(a128), _full(dt128)],
                  [_orow(S, (C3,), F32, ts), _orow(S, (C3,), F32, ts), _orow(S, (128,), F32, ts)])


def _chunk_masks(nc):
    r = lax.broadcasted_iota(jnp.int32, (nc, CHUNK, CHUNK), 1)
    c = lax.broadcasted_iota(jnp.int32, (nc, CHUNK, CHUNK), 2)
    return c <= r, c < r, c == r


def _chunk_local(qh, kh, vh, Gc, Gr, beta):
    nc = qh.shape[0]
    incl, strict, _ = _chunk_masks(nc)
    gamma = jnp.exp(jnp.where(incl, Gc - Gr, NEG))
    kb = kh * beta
    P = _bdot(_b16(kb), _b16(kh), 2, 2)
    Qk = _bdot(_b16(qh), _b16(kh), 2, 2)
    eG = jnp.exp(Gc)
    Gl = Gc[:, CHUNK - 1:CHUNK, :]
    edec = jnp.exp(Gl - Gc)
    return incl, strict, gamma, kb, P, Qk, eG, edec


def _gdn_local_fwd(qkv, gb, grow, ts):
    S = qkv.shape[0]
    nc = ts // CHUNK

    def body(t, first, ins, outs, scratch):
        q_ref, k_ref, v_ref, gb_ref, gr_ref = ins
        u_ref, w_ref, qd_ref, kd_ref, aqk_ref, T_ref = outs
        gbv = gb_ref[...]
        for h in range(GDN_HEADS):
            lo = h * GDN_DIM
            qh = q_ref[:, lo:lo + GDN_DIM].reshape(nc, CHUNK, GDN_DIM)
            kh = k_ref[:, lo:lo + GDN_DIM].reshape(nc, CHUNK, GDN_DIM)
            vh = v_ref[:, lo:lo + GDN_DIM].reshape(nc, CHUNK, GDN_DIM)
            Gc = _lane_pick(gbv, LANE_BA + h).reshape(nc, CHUNK, 1)
            beta = _lane_pick(gbv, LANE_BB + h).reshape(nc, CHUNK, 1)
            Gr = gr_ref[h].reshape(nc, 1, CHUNK)
            incl, strict, gamma, kb, P, Qk, eG, edec = _chunk_local(qh, kh, vh, Gc, Gr, beta)
            A = jnp.where(strict, P * gamma, 0.0)
            _, _, eye = _chunk_masks(nc)
            T = jnp.where(eye, 1.0, 0.0) - A
            X = A
            for _ in range(5):
                X = _bdot(X, X, 2, 1, HIGHEST)
                T = T + _bdot(T, X, 2, 1, HIGHEST)
            u = _bdot(T, vh * beta, 2, 1, HIGHEST)
            w = _bdot(T, kb * eG, 2, 1, HIGHEST)
            u_ref[:, lo:lo + GDN_DIM] = u.reshape(ts, GDN_DIM)
            w_ref[:, lo:lo + GDN_DIM] = w.reshape(ts, GDN_DIM)
            qd_ref[:, lo:lo + GDN_DIM] = (qh * eG).reshape(ts, GDN_DIM)
            kd_ref[:, lo:lo + GDN_DIM] = (kh * edec).reshape(ts, GDN_DIM)
            aqk_ref[h] = jnp.where(incl, Qk * gamma, 0.0).reshape(ts, CHUNK)
            T_ref[h] = T.reshape(ts, CHUNK)

    wide = _orow(S, (WIDTH,), F32, ts)
    perhead = ((GDN_HEADS, S, CHUNK), F32, (GDN_HEADS, ts, CHUNK), lambda i: (0, i, 0))
    return _tiled("gdn_local_fwd", body, S // ts,
                  [_cols(qkv, ts, WIDTH, 0), _cols(qkv, ts, WIDTH, 1), _cols(qkv, ts, WIDTH, 2),
                   _rows(gb, ts), (grow, (GDN_HEADS, nc, CHUNK), lambda i: (0, i, 0))],
                  [wide, wide, wide, wide, perhead, perhead])


def _gdn_scan_fwd(u, w, qd, kd, aqk, gb, ts):
    S = u.shape[0]
    nc = ts // CHUNK
    N = S // CHUNK

    def body(t, first, ins, outs, scratch):
        u_ref, w_ref, qd_ref, kd_ref, aqk_ref, gb_ref = ins
        o_ref, vn_ref, st_ref = outs
        (state,) = scratch

        @pl.when(first)
        def _():
            state[...] = jnp.zeros_like(state)

        def chunk(c, _):
            r0 = pl.multiple_of(c * CHUNK, CHUNK)
            rows = pl.ds(r0, CHUNK)
            glast = gb_ref[pl.ds(r0 + CHUNK - 1, 1), :]
            for h in range(GDN_HEADS):
                lo = h * GDN_DIM
                cols = slice(lo, lo + GDN_DIM)
                Sh = state[h]
                st_ref[c, h] = Sh
                Sb = _b16(Sh)
                vn = u_ref[rows, cols] - _dot(_b16(w_ref[rows, cols]), Sb)
                vnb = _b16(vn)
                o = _dot(_b16(qd_ref[rows, cols]), Sb) + _dot(_b16(aqk_ref[h, rows, :]), vnb)
                egl = jnp.exp(glast[:, LANE_BA + h:LANE_BA + h + 1])
                state[h] = Sh * egl + _dot(_b16(kd_ref[rows, cols]), vnb, TN)
                o_ref[rows, cols] = o
                vn_ref[rows, cols] = vn
            return 0

        lax.fori_loop(0, nc, chunk, 0)

    wide_in = lambda a: _rows(a, ts)
    wide = _orow(S, (WIDTH,), F32, ts)
    states = ((N, GDN_HEADS, GDN_DIM, GDN_DIM), F32, (nc, GDN_HEADS, GDN_DIM, GDN_DIM),
              lambda i: (i, 0, 0, 0))
    return _tiled("gdn_scan_fwd", body, S // ts,
                  [wide_in(u), wide_in(w), wide_in(qd), wide_in(kd),
                   (aqk, (GDN_HEADS, ts, CHUNK), lambda i: (0, i, 0)), _rows(gb, ts)],
                  [wide, wide, states],
                  scratch=[pltpu.VMEM((GDN_HEADS, GDN_DIM, GDN_DIM), F32)])


def _gdn_scan_bwd(do, w, qd, kd, aqk, vn, states, gb, ts):
    S = do.shape[0]
    nc = ts // CHUNK
    N = S // CHUNK

    def body(t, first, ins, outs, scratch):
        do_ref, w_ref, qd_ref, kd_ref, aqk_ref, vn_ref, st_ref, gb_ref = ins
        du_ref, dw_ref, dqd_ref, dkd_ref, daqk_ref, dgl_ref = outs
        (dstate,) = scratch

        @pl.when(first)
        def _():
            dstate[...] = jnp.zeros_like(dstate)

        r = lax.broadcasted_iota(jnp.int32, (CHUNK, CHUNK), 0)
        cc = lax.broadcasted_iota(jnp.int32, (CHUNK, CHUNK), 1)
        incl = cc <= r
        lane = lax.broadcasted_iota(jnp.int32, (1, 128), 1)

        def chunk(k, _):
            c = nc - 1 - k
            r0 = pl.multiple_of(c * CHUNK, CHUNK)
            rows = pl.ds(r0, CHUNK)
            glast = gb_ref[pl.ds(r0 + CHUNK - 1, 1), :]
            dgl_row = jnp.zeros((1, 128), F32)
            for h in range(GDN_HEADS):
                lo = h * GDN_DIM
                cols = slice(lo, lo + GDN_DIM)
                Sh = st_ref[c, h]
                Sb = _b16(Sh)
                dS = dstate[h]
                dSb = _b16(dS)
                dob = _b16(do_ref[rows, cols])
                aqkb = _b16(aqk_ref[h, rows, :])
                vnb = _b16(vn_ref[rows, cols])
                kdb = _b16(kd_ref[rows, cols])
                dvn = _dot(aqkb, dob, TN) + _dot(kdb, dSb)
                dvnb = _b16(dvn)
                daqk_ref[h, rows, :] = jnp.where(incl, _dot(dob, vnb, NT), 0.0)
                dqd_ref[rows, cols] = _dot(dob, Sb, NT)
                dkd_ref[rows, cols] = _dot(vnb, dSb, NT)
                dw_ref[rows, cols] = -_dot(dvnb, Sb, NT)
                du_ref[rows, cols] = dvn
                egl = jnp.exp(glast[:, LANE_BA + h:LANE_BA + h + 1])
                dgl = egl * jnp.sum(jnp.sum(dS * Sh, axis=1, keepdims=True), axis=0, keepdims=True)
                dgl_row = jnp.where(lane == h, dgl, dgl_row)
                dstate[h] = (_dot(_b16(qd_ref[rows, cols]), dob, TN) + egl * dS
                             - _dot(_b16(w_ref[rows, cols]), dvnb, TN))
            dgl_ref[pl.ds(c, 1), :] = dgl_row
            return 0

        lax.fori_loop(0, nc, chunk, 0)

    wide_in = lambda a: _rows(a, ts)
    wide = _orow(S, (WIDTH,), F32, ts)
    perhead_in = lambda a: (a, (GDN_HEADS, ts, CHUNK), lambda i: (0, i, 0))
    perhead = ((GDN_HEADS, S, CHUNK), F32, (GDN_HEADS, ts, CHUNK), lambda i: (0, i, 0))
    return _tiled("gdn_scan_bwd", body, S // ts,
                  [wide_in(do), wide_in(w), wide_in(qd), wide_in(kd), perhead_in(aqk), wide_in(vn),
                   (states, (nc, GDN_HEADS, GDN_DIM, GDN_DIM), lambda i: (i, 0, 0, 0)), _rows(gb, ts)],
                  [wide, wide, wide, wide, perhead, ((N, 128), F32, (nc, 128), lambda i: (i, 0))],
                  scratch=[pltpu.VMEM((GDN_HEADS, GDN_DIM, GDN_DIM), F32)], reverse=True)


def _gdn_local_bwd(qkv, gb, grow, T, du, dw, dqd, dkd, daqk, dgl, ts):
    S = qkv.shape[0]
    nc = ts // CHUNK

    def body(t, first, ins, outs, scratch):
        (q_ref, k_ref, v_ref, gb_ref, gr_ref, T_ref, du_ref, dw_ref, dqd_ref, dkd_ref,
         daqk_ref, dgl_ref) = ins
        dqkv_ref, dgb_ref = outs
        gbv = gb_ref[...]
        dglv = dgl_ref[...]
        lane = lax.broadcasted_iota(jnp.int32, (ts, 128), 1)
        dG_all = jnp.zeros((ts, 128), F32)
        dbeta_all = jnp.zeros((ts, 128), F32)
        for h in range(GDN_HEADS):
            lo = h * GDN_DIM
            cols = slice(lo, lo + GDN_DIM)
            r3 = lambda ref: ref[:, cols].reshape(nc, CHUNK, GDN_DIM)
            qh, kh, vh = r3(q_ref), r3(k_ref), r3(v_ref)
            duh, dwh, dqdh, dkdh = r3(du_ref), r3(dw_ref), r3(dqd_ref), r3(dkd_ref)
            Gc = _lane_pick(gbv, LANE_BA + h).reshape(nc, CHUNK, 1)
            beta = _lane_pick(gbv, LANE_BB + h).reshape(nc, CHUNK, 1)
            Gr = gr_ref[h].reshape(nc, 1, CHUNK)
            Th = T_ref[h].reshape(nc, CHUNK, CHUNK)
            daq = daqk_ref[h].reshape(nc, CHUNK, CHUNK)
            incl, strict, gamma, kb, P, Qk, eG, edec = _chunk_local(qh, kh, vh, Gc, Gr, beta)
            _, _, eye = _chunk_masks(nc)
            vb = vh * beta
            kbg = kb * eG
            dvb = _bdot(Th, duh, 1, 1, HIGHEST)
            dkbg = _bdot(Th, dwh, 1, 1, HIGHEST)
            dT = _bdot(duh, vb, 2, 2, HIGHEST) + _bdot(dwh, kbg, 2, 2, HIGHEST)
            M1 = _bdot(Th, dT, 1, 1, HIGHEST)
            dA = jnp.where(strict, -_bdot(M1, Th, 2, 2, HIGHEST), 0.0)
            dP = dA * gamma
            dQ = daq * gamma
            dgam = (dA * P + daq * Qk) * gamma
            dPb, dQb = _b16(dP), _b16(dQ)
            khb, qhb, kbb = _b16(kh), _b16(qh), _b16(kb)
            dq = _bdot(dQb, khb, 2, 1) + dqdh * eG
            dkb = _bdot(dPb, khb, 2, 1) + dkbg * eG
            dk = (_bdot(dQb, qhb, 1, 1) + _bdot(dPb, kbb, 1, 1) + dkdh * edec + dkb * beta)
            dbeta = (jnp.sum(dkb * kh, axis=2, keepdims=True) + jnp.sum(dvb * vh, axis=2, keepdims=True))
            dv = dvb * beta
            col_as_col = jnp.sum(jnp.where(eye, jnp.sum(dgam, axis=1, keepdims=True), 0.0),
                                 axis=2, keepdims=True)
            kd_term = jnp.sum(dkdh * kh * edec, axis=2, keepdims=True)
            dG = (jnp.sum(dgam, axis=2, keepdims=True) - col_as_col
                  + jnp.sum(dqdh * qh * eG, axis=2, keepdims=True)
                  + jnp.sum(dkbg * kbg, axis=2, keepdims=True) - kd_term)
            dgl_h = dglv[:, h:h + 1].reshape(nc, 1, 1) + jnp.sum(kd_term, axis=1, keepdims=True)
            last = lax.broadcasted_iota(jnp.int32, (nc, CHUNK, 1), 1) == CHUNK - 1
            dG = dG + jnp.where(last, dgl_h, 0.0)
            dqkv_ref[:, cols] = dq.reshape(ts, GDN_DIM)
            dqkv_ref[:, WIDTH + lo:WIDTH + lo + GDN_DIM] = dk.reshape(ts, GDN_DIM)
            dqkv_ref[:, 2 * WIDTH + lo:2 * WIDTH + lo + GDN_DIM] = dv.reshape(ts, GDN_DIM)
            dG_all = jnp.where(lane == LANE_BA + h, dG.reshape(ts, 1), dG_all)
            dbeta_all = jnp.where(lane == LANE_BB + h, dbeta.reshape(ts, 1), dbeta_all)
        dg_all = _scan_rows(dG_all, ts, CHUNK, reverse=True)
        dgb_ref[...] = jnp.where(lane < LANE_BB, dg_all, dbeta_all)

    wide_in = lambda a: _rows(a, ts)
    perhead_in = lambda a: (a, (GDN_HEADS, ts, CHUNK), lambda i: (0, i, 0))
    return _tiled("gdn_local_bwd", body, S // ts,
                  [_cols(qkv, ts, WIDTH, 0), _cols(qkv, ts, WIDTH, 1), _cols(qkv, ts, WIDTH, 2),
                   _rows(gb, ts), (grow, (GDN_HEADS, nc, CHUNK), lambda i: (0, i, 0)), perhead_in(T),
                   wide_in(du), wide_in(dw), wide_in(dqd), wide_in(dkd), perhead_in(daqk),
                   (dgl, (nc, 128), lambda i: (i, 0))],
                  [_orow(S, (3 * WIDTH,), F32, ts), _orow(S, (128,), F32, ts)])


def _gdn_prep_bwd(dqkv, dgb, cpre, z, conv_w, a128, dt128, ts):
    S = z.shape[0]
    C3 = 3 * WIDTH
    hb = ts // 8
    n_tiles = S // ts

    def dpre(dq, c):
        y, dsil = _silu_and_grad(c)
        parts = []
        for h in range(GDN_HEADS):
            lo = h * GDN_DIM
            yq = y[:, lo:lo + GDN_DIM]
            rq = _l2_fwd(yq)
            nq = yq * rq
            dn = dq[:, lo:lo + GDN_DIM] * (GDN_DIM ** -0.5)
            parts.append(rq * (dn - nq * jnp.sum(dn * nq, axis=1, keepdims=True)))
        for h in range(GDN_HEADS):
            lo = WIDTH + h * GDN_DIM
            yk = y[:, lo:lo + GDN_DIM]
            rk = _l2_fwd(yk)
            nk = yk * rk
            dn = dq[:, lo:lo + GDN_DIM]
            parts.append(rk * (dn - nk * jnp.sum(dn * nk, axis=1, keepdims=True)))
        parts.append(dq[:, 2 * WIDTH:])
        return jnp.concatenate(parts, axis=1) * dsil

    def body(t, first, ins, outs, scratch):
        (dq_ref, dqn_ref, c_ref, cn_ref, x_ref, xp_ref, zs_ref, dgb_ref, w_ref, a_ref, dt_ref) = ins
        dx_ref, dzs_ref, dw_ref, dad_ref = outs

        @pl.when(first)
        def _():
            dw_ref[...] = jnp.zeros_like(dw_ref)
            dad_ref[...] = jnp.zeros_like(dad_ref)

        dc = dpre(dq_ref[...], c_ref[...])
        dcn = jnp.where(t < n_tiles - 1, dpre(dqn_ref[...], cn_ref[...]), 0.0)
        dce = jnp.concatenate([dc, dcn], axis=0)
        w = w_ref[...]
        dx = w[3:4, :] * dc
        for back in (1, 2, 3):
            dx = dx + w[3 - back:4 - back, :] * pltpu.roll(dce, ts + 8 - back, 0)[:ts, :]
        dx_ref[...] = dx
        halo = jnp.where(t > 0, xp_ref[...], 0.0)
        xe = jnp.concatenate([halo, x_ref[...]], axis=0)
        dw_ref[3:4, :] += jnp.sum(dc * xe[8:, :], axis=0, keepdims=True)
        for back in (1, 2, 3):
            dw_ref[3 - back:4 - back, :] += jnp.sum(dc * pltpu.roll(xe, back, 0)[8:, :], axis=0,
                                                     keepdims=True)
        zs = zs_ref[...]
        dgb = dgb_ref[...]
        lane = lax.broadcasted_iota(jnp.int32, zs.shape, 1)
        arg = zs + dt_ref[...]
        nega = -jnp.exp(a_ref[...])
        dba = dgb * nega * _sigmoid(arg)
        beta = _sigmoid(zs)
        dbb = dgb * beta * (1.0 - beta)
        dzs_ref[...] = jnp.where((lane >= LANE_BA) & (lane < LANE_BB), dba,
                                 jnp.where((lane >= LANE_BB) & (lane < LANE_BB + 4), dbb, 0.0))
        dad_ref[0:1, :] += jnp.sum(dgb * nega * _softplus(arg), axis=0, keepdims=True)
        dad_ref[1:2, :] += jnp.sum(dba, axis=0, keepdims=True)

    nxt = lambda i: (jnp.minimum((i + 1) * hb, S // 8 - 1), 0)
    prv = lambda i: (jnp.maximum(i * hb - 1, 0), CB_BQKV)
    return _tiled("gdn_prep_bwd", body, n_tiles,
                  [_rows(dqkv, ts), (dqkv, (8, C3), nxt), _rows(cpre, ts), (cpre, (8, C3), nxt),
                   (z, (ts, C3), lambda i: (i, CB_BQKV)), (z, (8, C3), prv),
                   _cols(z, ts, 128, CB_SMALL), _rows(dgb, ts), _full(conv_w), _full(a128), _full(dt128)],
                  [_orow(S, (C3,), F32, ts), _orow(S, (128,), F32, ts), _oacc((8, C3), F32),
                   _oacc((8, 128), F32)])


def _mem_attn_fwd(z, mk, mv, ts):
    S = z.shape[0]

    def body(t, first, ins, outs, scratch):
        q_ref, mk_ref, mv_ref = ins
        (o_ref,) = outs
        for h in range(MEM_HEADS):
            cols = slice(h * MEM_DIM, (h + 1) * MEM_DIM)
            s = _dot(_b16(q_ref[:, cols]), _b16(mk_ref[:, cols]), NT) * (MEM_DIM ** -0.5)
            m = jnp.max(s, axis=1, keepdims=True)
            p = jnp.exp(s - m)
            p = p / jnp.sum(p, axis=1, keepdims=True)
            o_ref[:, cols] = _dot(_b16(p), _b16(mv_ref[:, cols]))

    (o,) = _tiled("mem_attn_fwd", body, S // ts, [_cols(z, ts, WIDTH, CB_MQ), _full(mk), _full(mv)],
                  [_orow(S, (WIDTH,), F32, ts)])
    return o


def _mem_attn_bwd(do, z, mk, mv, ts):
    S = z.shape[0]
    M = mk.shape[0]

    def body(t, first, ins, outs, scratch):
        do_ref, q_ref, mk_ref, mv_ref = ins
        dq_ref, dmk_ref, dmv_ref = outs

        @pl.when(first)
        def _():
            dmk_ref[...] = jnp.zeros_like(dmk_ref)
            dmv_ref[...] = jnp.zeros_like(dmv_ref)

        scale = MEM_DIM ** -0.5
        for h in range(MEM_HEADS):
            cols = slice(h * MEM_DIM, (h + 1) * MEM_DIM)
            qb = _b16(q_ref[:, cols])
            kb = _b16(mk_ref[:, cols])
            dob = _b16(do_ref[:, cols])
            s = _dot(qb, kb, NT) * scale
            m = jnp.max(s, axis=1, keepdims=True)
            p = jnp.exp(s - m)
            p = p / jnp.sum(p, axis=1, keepdims=True)
            dmv_ref[:, cols] += _dot(_b16(p), dob, TN)
            dp = _dot(dob, _b16(mv_ref[:, cols]), NT)
            ds = p * (dp - jnp.sum(dp * p, axis=1, keepdims=True)) * scale
            dsb = _b16(ds)
            dq_ref[:, cols] = _dot(dsb, kb)
            dmk_ref[:, cols] += _dot(dsb, qb, TN)

    return _tiled("mem_attn_bwd", body, S // ts,
                  [_rows(do, ts), _cols(z, ts, WIDTH, CB_MQ), _full(mk), _full(mv)],
                  [_orow(S, (WIDTH,), F32, ts), _oacc((M, WIDTH), F32), _oacc((M, WIDTH), F32)])


def _head_norm(ob, g):
    xs, rs = [], []
    for h in range(GDN_HEADS):
        o = ob[:, h * GDN_DIM:(h + 1) * GDN_DIM]
        r = lax.rsqrt(jnp.mean(o * o, axis=1, keepdims=True) + EPS)
        xs.append(o * r)
        rs.append(r)
    return xs, rs


def _merge_fwd(x, z, o_a, o_b, o_m, gdn_g, b_merge, wb, wout, ts):
    S, D = x.shape

    def body(t, first, ins, outs, scratch):
        (x_ref, g_ref, oa_ref, az_ref, ob_ref, bz_ref, om_ref, mz_ref, gg_ref, bm_ref, wb_ref,
         wo_ref) = ins
        xo_ref, ya_ref, yb_ref, ym_ref, mg_ref = outs
        ya = oa_ref[...] * _silu_and_grad(az_ref[...])[0]
        xs, _ = _head_norm(ob_ref[...], None)
        nb = jnp.concatenate([xh * gg_ref[...] for xh in xs], axis=1)
        yb = nb * _silu_and_grad(bz_ref[...])[0]
        ym = om_ref[...] * _silu_and_grad(mz_ref[...])[0]
        merged = jnp.zeros((ts, D), F32)
        for n, (y, y_ref) in enumerate(((ya, ya_ref), (yb, yb_ref), (ym, ym_ref))):
            yb16 = _b16(y)
            y_ref[...] = yb16
            gate = _sigmoid(g_ref[:, n * D:(n + 1) * D] + bm_ref[:, n * D:(n + 1) * D])
            merged = merged + gate * _dot(yb16, wb_ref[n])
        mb = _b16(merged)
        mg_ref[...] = mb
        xo_ref[...] = x_ref[...] + _dot(mb, wo_ref[...])

    half = lambda a: _rows(a, ts)
    return _tiled("merge_fwd", body, S // ts,
                  [_rows(x, ts), _cols(z, ts, 3 * D, CB_GATES), half(o_a), _cols(z, ts, WIDTH, CB_AZ),
                   half(o_b), _cols(z, ts, WIDTH, CB_BZ), half(o_m), _cols(z, ts, WIDTH, CB_MZ),
                   _full(gdn_g.reshape(1, GDN_DIM)), _full(b_merge.reshape(1, 3 * D)), _full(wb), _full(wout)],
                  [_orow(S, (D,), F32, ts), _orow(S, (WIDTH,), BF16, ts), _orow(S, (WIDTH,), BF16, ts),
                   _orow(S, (WIDTH,), BF16, ts), _orow(S, (D,), BF16, ts)])


def _merge_bwd(dout, z, o_a, o_b, o_m, ya, yb, ym, gdn_g, b_merge, wb, wout, hsum, ts):
    S, D = dout.shape

    def body(t, first, ins, outs, scratch):
        (do_ref, g_ref, oa_ref, az_ref, ob_ref, bz_ref, om_ref, mz_ref, ya_ref, yb_ref, ym_ref,
         gg_ref, bm_ref, wb_ref, wo_ref, hs_ref) = ins
        (dg_ref, dpa_ref, dpb_ref, dpm_ref, doa_ref, dob_ref, dom_ref, daz_ref, dbz_ref, dmz_ref,
         dl_ref, dbm_ref, dgg_ref) = outs

        @pl.when(first)
        def _():
            dbm_ref[...] = jnp.zeros_like(dbm_ref)
            dgg_ref[...] = jnp.zeros_like(dgg_ref)

        dmerged = _dot(_b16(do_ref[...]), wo_ref[...], NT)
        dys = []
        for n, (y_ref, dp_ref) in enumerate(((ya_ref, dpa_ref), (yb_ref, dpb_ref), (ym_ref, dpm_ref))):
            sl = slice(n * D, (n + 1) * D)
            gate = _sigmoid(g_ref[:, sl] + bm_ref[:, sl])
            proj = _dot(y_ref[...], wb_ref[n])
            dproj = _b16(gate * dmerged)
            dp_ref[...] = dproj
            dgp = dmerged * proj * gate * (1.0 - gate)
            dg_ref[:, sl] = dgp.astype(dg_ref.dtype)
            dbm_ref[0:1, sl] += jnp.sum(dgp, axis=0, keepdims=True)
            dys.append(_dot(dproj, wb_ref[n], NT))
        dya, dyb, dym = dys
        sa, dsa = _silu_and_grad(az_ref[...])
        oa = oa_ref[...]
        doa = dya * sa
        doa_ref[...] = doa
        daz_ref[...] = dya * oa * dsa
        dl_ref[...] = _dot(doa * oa, hs_ref[...], NN, HIGHEST)
        sm, dsm = _silu_and_grad(mz_ref[...])
        dom_ref[...] = dym * sm
        dmz_ref[...] = dym * om_ref[...] * dsm
        sb, dsb = _silu_and_grad(bz_ref[...])
        xs, rs = _head_norm(ob_ref[...], None)
        gg = gg_ref[...]
        dgg = jnp.zeros((1, GDN_DIM), F32)
        for h in range(GDN_HEADS):
            cols = slice(h * GDN_DIM, (h + 1) * GDN_DIM)
            dn = dyb[:, cols] * sb[:, cols]
            dbz_ref[:, cols] = dyb[:, cols] * (xs[h] * gg) * dsb[:, cols]
            dgg = dgg + jnp.sum(dn * xs[h], axis=0, keepdims=True)
            dxh = dn * gg
            dob_ref[:, cols] = rs[h] * (dxh - xs[h] * jnp.mean(dxh * xs[h], axis=1, keepdims=True))
        dgg_ref[0:1, :] += dgg

    half = lambda a: _rows(a, ts)
    w512 = lambda dt: _orow(S, (WIDTH,), dt, ts)
    return _tiled("merge_bwd", body, S // ts,
                  [_rows(dout, ts), _cols(z, ts, 3 * D, CB_GATES), half(o_a), _cols(z, ts, WIDTH, CB_AZ),
                   half(o_b), _cols(z, ts, WIDTH, CB_BZ), half(o_m), _cols(z, ts, WIDTH, CB_MZ),
                   half(ya), half(yb), half(ym), _full(gdn_g.reshape(1, GDN_DIM)),
                   _full(b_merge.reshape(1, 3 * D)), _full(wb), _full(wout), _full(hsum)],
                  [_orow(S, (3 * D,), BF16, ts), _orow(S, (D,), BF16, ts), _orow(S, (D,), BF16, ts),
                   _orow(S, (D,), BF16, ts), w512(F32), w512(F32), w512(F32), w512(F32), w512(F32),
                   w512(F32), _orow(S, (128,), F32, ts), _oacc((8, 3 * D), F32), _oacc((8, GDN_DIM), F32)])


def _to_aligned(w):
    sizes = (512, 512, 512, 8, 512, 512, 512, 512, 4, 4, 512, 512, 512, 3072)
    names = ("aq", "ak", "av", "af", "az", "bq", "bk", "bv", "ba", "bb", "bz", "mq", "mz", "gates")
    p, off = {}, 0
    for n, s in zip(names, sizes):
        p[n] = w[..., off:off + s]
        off += s
    pad = jnp.zeros(w.shape[:-1] + (128 - 16,), w.dtype)
    return jnp.concatenate([p["gates"], p["bq"], p["bk"], p["bv"], p["aq"], p["ak"], p["av"], p["az"],
                            p["bz"], p["mq"], p["mz"], p["af"], p["ba"], p["bb"], pad], axis=-1)


def _from_aligned(w):
    c = lambda lo, n: w[..., lo:lo + n]
    gates, bq, bk, bv = c(0, 3072), c(3072, 512), c(3584, 512), c(4096, 512)
    aq, ak, av, az = c(4608, 512), c(5120, 512), c(5632, 512), c(6144, 512)
    bz, mq, mz = c(6656, 512), c(7168, 512), c(7680, 512)
    af, ba, bb = c(8192, 8), c(8200, 4), c(8204, 4)
    return jnp.concatenate([aq, ak, av, af, az, bq, bk, bv, ba, bb, bz, mq, mz, gates], axis=-1)


def _lanes128(v, lane0):
    out = jnp.zeros((1, 128), F32)
    return out.at[0, lane0:lane0 + v.shape[0]].set(v.astype(F32))


def _tiles(S):
    ts = min(512, S // 2)
    return dict(ts=ts, ts_small=min(256, S // 2), tq=min(512, S // 4))


def _layer_fwd(x, mem, p):
    S = x.shape[0]
    tl = _tiles(S)
    ts, tss, tq = tl["ts"], tl["ts_small"], tl["tq"]
    h, rstd = _rms_fwd("norm_fwd", x, p["norm_g"], ts)
    z = _mm("in_proj", h, p["w_in_al"], tn=640)

    b_fg128 = _lanes128(p["b_fg"], LANE_AF)
    f128 = _fox_decay(z, b_fg128, ts)
    F = f128[:, :FOX_HEADS]
    aq = z[:, CB_AQ * WIDTH:(CB_AQ + 1) * WIDTH]
    ak = z[:, CB_AK * WIDTH:(CB_AK + 1) * WIDTH]
    av = z[:, CB_AV * WIDTH:(CB_AV + 1) * WIDTH]
    qh = _heads_major(aq, FOX_HEADS, FOX_DIM).astype(BF16)
    kh = _heads_major(ak, FOX_HEADS, FOX_DIM).astype(BF16)
    vh = _heads_major(av, FOX_HEADS, FOX_DIM).astype(BF16)
    khT = kh.transpose(0, 2, 1)
    FT = F.T
    f_col, f_row = FT[:, :, None], FT[:, None, :]
    o_h, lse = _fox_fwd(qh, khT, vh, f_col, f_row, tq)
    o_a = _heads_minor(o_h)

    a128 = _lanes128(p["a_log"], LANE_BA)
    dt128 = _lanes128(p["dt_bias"], LANE_BA)
    qkv, cpre, gb = _gdn_prep(z, p["conv_w"], a128, dt128, ts)
    grow = gb[:, LANE_BA:LANE_BA + GDN_HEADS].T.reshape(GDN_HEADS, S // CHUNK, CHUNK)
    u, w, qd, kd, aqk, T = _gdn_local_fwd(qkv, gb, grow, ts)
    o_b, vn, states = _gdn_scan_fwd(u, w, qd, kd, aqk, gb, ts)

    mem_h, mem_r = _rms_fwd("mem_norm_fwd", mem, p["mem_norm_g"], mem.shape[0])
    mkv = _mm("mem_kv", mem_h, p["w_mem_kv"])
    mk, mv = mkv[:, :WIDTH], mkv[:, WIDTH:]
    o_m = _mem_attn_fwd(z, mk, mv, ts)

    x_next, ya, yb, ym, merged = _merge_fwd(x, z, o_a, o_b, o_m, p["gdn_norm_g"], p["b_merge"],
                                            p["w_branch"], p["w_out"], tss)
    saved = dict(x=x, h=h, rstd=rstd, z=z, b_fg128=b_fg128, qh=qh, kh=kh, khT=khT, vh=vh, f_col=f_col,
                 f_row=f_row, lse=lse, o_a=o_a, a128=a128, dt128=dt128, qkv=qkv, cpre=cpre, gb=gb,
                 grow=grow, w=w, qd=qd, kd=kd, aqk=aqk, T=T, o_b=o_b, vn=vn, states=states,
                 mem_h=mem_h, mem_r=mem_r, mk=mk, mv=mv, o_m=o_m, ya=ya, yb=yb, ym=ym, merged=merged)
    return x_next, saved


def _layer_bwd(dout, mem, p, s):
    S = dout.shape[0]
    tl = _tiles(S)
    ts, tss, tq = tl["ts"], tl["ts_small"], tl["tq"]
    z = s["z"]
    hsum = (jnp.arange(WIDTH)[:, None] // FOX_DIM == jnp.arange(128)[None, :]).astype(F32)
    (dgates, dpa, dpb, dpm, do_a, do_b, do_m, daz, dbz, dmz, delta128, db_merge, dgdn_g) = _merge_bwd(
        dout, z, s["o_a"], s["o_b"], s["o_m"], s["ya"], s["yb"], s["ym"], p["gdn_norm_g"],
        p["b_merge"], p["w_branch"], p["w_out"], hsum, tss)
    g = {}
    g["b_merge"] = db_merge[0]
    g["gdn_norm_g"] = dgdn_g[0]
    g["w_out"] = _mm("dw_out", s["merged"], dout, ta=True)
    g["w_branch"] = jnp.stack([_mm("dw_branch", y, dp, ta=True)
                               for y, dp in ((s["ya"], dpa), (s["yb"], dpb), (s["ym"], dpm))])

    do_h = _heads_major(do_a, FOX_HEADS, FOX_DIM).astype(BF16)
    delta_row = delta128[:, :FOX_HEADS].T[:, None, :]
    lse_row = s["lse"].reshape(FOX_HEADS, 1, S)
    dqT, dk_h, dv_h, dfk, dfq = _fox_bwd(s["qh"], s["kh"], s["khT"], s["vh"], do_h, s["f_row"], s["f_col"],
                                    lse_row, delta_row, tq)
    daq = _heads_minor(dqT.transpose(0, 2, 1))
    dak = _heads_minor(dk_h)
    dav = _heads_minor(dv_h)
    lane_pad = ((0, 0), (0, 128 - FOX_HEADS))
    daf128, db_fg = _fox_decay_bwd(jnp.pad(dfk[:, :, 0].T, lane_pad), jnp.pad(dfq[:, 0, :].T, lane_pad),
                                   z, s["b_fg128"], ts)
    g["b_fg"] = db_fg[:FOX_HEADS]

    du, dw, dqd, dkd, daqk, dgl = _gdn_scan_bwd(do_b, s["w"], s["qd"], s["kd"], s["aqk"], s["vn"],
                                                s["states"], s["gb"], ts)
    dqkv, dgb = _gdn_local_bwd(s["qkv"], s["gb"], s["grow"], s["T"], du, dw, dqd, dkd, daqk, dgl, ts)
    dbqkv, dzs_b, dconv, dad = _gdn_prep_bwd(dqkv, dgb, s["cpre"], z, p["conv_w"], s["a128"],
                                             s["dt128"], ts)
    g["conv_w"] = dconv[:4]
    g["a_log"] = dad[0, LANE_BA:LANE_BA + GDN_HEADS]
    g["dt_bias"] = dad[1, LANE_BA:LANE_BA + GDN_HEADS]

    dmq, dmk, dmv = _mem_attn_bwd(do_m, z, s["mk"], s["mv"], ts)
    dmkv = jnp.concatenate([dmk, dmv], axis=1)
    g["w_mem_kv"] = _mm("dw_mem_kv", s["mem_h"], dmkv, ta=True)
    dmem_h = _mm("dmem_h", dmkv, p["w_mem_kv"], tb=True)
    M = mem.shape[0]
    _, g["mem_norm_g"] = _rms_bwd("mem_norm_bwd", dmem_h, mem, s["mem_r"], p["mem_norm_g"],
                                  jnp.zeros_like(mem), M)

    lane = jnp.arange(128)[None, :]
    dsmall = jnp.where(lane < 8, daf128, dzs_b)
    dz = jnp.concatenate([dgates, _b16(dbqkv), _b16(daq), _b16(dak), _b16(dav), _b16(daz), _b16(dbz),
                          _b16(dmq), _b16(dmz), _b16(dsmall)], axis=1)
    g["w_in_al"] = _mm("dw_in", s["h"], dz, ta=True, tn=640)
    dh = _mm("dh", dz, p["w_in_al"], tb=True, tk=640)
    dx, g["norm_g"] = _rms_bwd("norm_bwd", dh, s["x"], s["rstd"], p["norm_g"], dout, ts)
    return dx, g


def _local_step(x, mem, layers, final_norm_g, loss_target):
    S = x.shape[0]
    saves = []
    cur = x
    for p in layers:
        cur, sv = _layer_fwd(cur, mem, p)
        saves.append(sv)
    dx, dgf, loss_lanes = _loss_head(cur, final_norm_g, loss_target, _tiles(S)["ts"])
    grads = [None] * len(layers)
    for l in reversed(range(len(layers))):
        dx, grads[l] = _layer_bwd(dx, mem, layers[l], saves[l])
    return loss_lanes, dx, grads, dgf


HBM_SPEC = pl.BlockSpec(memory_space=pltpu.HBM)


def _mesh_pos():
    return lax.axis_index("x"), lax.axis_index("y"), lax.axis_index("c")


def _gather_chips(name, shard):
    R, C = shard.shape

    def body(x_ref, out_ref, send_sems, recv_sems, local_sem):
        mx, my, mc = _mesh_pos()
        me = 2 * mx + my
        chips = [(1 - mx, my), (mx, 1 - my), (1 - mx, 1 - my)]
        local = pltpu.make_async_copy(x_ref, out_ref.at[me], local_sem)
        local.start()

        def copy(k, block, to):
            return pltpu.make_async_remote_copy(
                src_ref=x_ref, dst_ref=out_ref.at[block], send_sem=send_sems.at[k],
                recv_sem=recv_sems.at[k], device_id=to, device_id_type=MESH_ID)

        sends = [copy(k, me, (px, py, mc)) for k, (px, py) in enumerate(chips)]
        for cp in sends:
            cp.start()
        for k, (px, py) in enumerate(chips):
            copy(k, 2 * px + py, (px, py, mc)).wait_recv()
        for cp in sends:
            cp.wait_send()
        local.wait()

    return pl.pallas_call(
        body, name=name, out_shape=jax.ShapeDtypeStruct((N_CHIPS, R, C), shard.dtype),
        in_specs=[HBM_SPEC], out_specs=HBM_SPEC,
        scratch_shapes=[pltpu.SemaphoreType.DMA((3,)), pltpu.SemaphoreType.DMA((3,)),
                        pltpu.SemaphoreType.DMA(())],
    )(shard)


def _sibling_swap_half(g4, half_rows):
    n, R, C = g4.shape

    def body(g_ref, out_ref, send_sem, recv_sem):
        mx, my, mc = _mesh_pos()
        start = pl.multiple_of((1 - mc) * half_rows, 8)
        cp = pltpu.make_async_remote_copy(
            src_ref=g_ref.at[:, pl.ds(start, half_rows), :], dst_ref=out_ref, send_sem=send_sem,
            recv_sem=recv_sem, device_id=(mx, my, 1 - mc), device_id_type=MESH_ID)
        cp.start()
        cp.wait()

    return pl.pallas_call(
        body, name="grad_sibling_swap", out_shape=jax.ShapeDtypeStruct((n, half_rows, C), g4.dtype),
        in_specs=[HBM_SPEC], out_specs=HBM_SPEC,
        scratch_shapes=[pltpu.SemaphoreType.DMA(()), pltpu.SemaphoreType.DMA(())],
    )(g4)


def _chip_exchange(p4):
    n, H, C = p4.shape

    def body(p_ref, out_ref, send_sems, recv_sems, local_sem):
        mx, my, mc = _mesh_pos()
        me = 2 * mx + my
        chips = [(1 - mx, my), (mx, 1 - my), (1 - mx, 1 - my)]
        local = pltpu.make_async_copy(p_ref.at[me], out_ref.at[me], local_sem)
        local.start()

        def copy(k, src_block, dst_slot, to):
            return pltpu.make_async_remote_copy(
                src_ref=p_ref.at[src_block], dst_ref=out_ref.at[dst_slot], send_sem=send_sems.at[k],
                recv_sem=recv_sems.at[k], device_id=to, device_id_type=MESH_ID)

        sends = [copy(k, 2 * px + py, me, (px, py, mc)) for k, (px, py) in enumerate(chips)]
        for cp in sends:
            cp.start()
        for k, (px, py) in enumerate(chips):
            copy(k, me, 2 * px + py, (px, py, mc)).wait_recv()
        for cp in sends:
            cp.wait_send()
        local.wait()

    return pl.pallas_call(
        body, name="grad_chip_exchange", out_shape=jax.ShapeDtypeStruct((n, H, C), p4.dtype),
        in_specs=[HBM_SPEC], out_specs=HBM_SPEC,
        scratch_shapes=[pltpu.SemaphoreType.DMA((3,)), pltpu.SemaphoreType.DMA((3,)),
                        pltpu.SemaphoreType.DMA(())],
    )(p4)


def _sibling_gather(half):
    H, C = half.shape

    def body(h_ref, out_ref, send_sem, recv_sem, local_sem):
        mx, my, mc = _mesh_pos()
        mine = pl.ds(pl.multiple_of(mc * H, 8), H)
        theirs = pl.ds(pl.multiple_of((1 - mc) * H, 8), H)
        local = pltpu.make_async_copy(h_ref, out_ref.at[mine, :], local_sem)
        local.start()
        cp = pltpu.make_async_remote_copy(
            src_ref=h_ref, dst_ref=out_ref.at[mine, :], send_sem=send_sem, recv_sem=recv_sem,
            device_id=(mx, my, 1 - mc), device_id_type=MESH_ID)
        cp.start()
        pltpu.make_async_remote_copy(
            src_ref=h_ref, dst_ref=out_ref.at[theirs, :], send_sem=send_sem, recv_sem=recv_sem,
            device_id=(mx, my, 1 - mc), device_id_type=MESH_ID).wait_recv()
        cp.wait_send()
        local.wait()

    return pl.pallas_call(
        body, name="grad_sibling_gather", out_shape=jax.ShapeDtypeStruct((2 * H, C), half.dtype),
        in_specs=[HBM_SPEC], out_specs=HBM_SPEC,
        scratch_shapes=[pltpu.SemaphoreType.DMA(()), pltpu.SemaphoreType.DMA(()),
                        pltpu.SemaphoreType.DMA(())],
    )(half)


def _add_pairs(a, b, tr):
    n, H, C = a.shape

    def kern(a_ref, b_ref, o_ref):
        o_ref[...] = a_ref[...] + b_ref[...]

    spec = pl.BlockSpec((None, tr, C), lambda j, i: (j, i, 0))
    return pl.pallas_call(
        kern, name="grad_pair_sum", grid=(n, H // tr), in_specs=[spec, spec], out_specs=spec,
        out_shape=jax.ShapeDtypeStruct((n, H, C), a.dtype),
        compiler_params=_params(("parallel", "parallel")),
    )(a, b)


def _sum_slots(r4, tr):
    n, H, C = r4.shape

    def kern(r_ref, o_ref):
        o_ref[...] = ((r_ref[0] + r_ref[1]) + r_ref[2]) + r_ref[3]

    return pl.pallas_call(
        kern, name="grad_chip_sum", grid=(H // tr,),
        in_specs=[pl.BlockSpec((n, tr, C), lambda i: (0, i, 0))],
        out_specs=pl.BlockSpec((tr, C), lambda i: (i, 0)),
        out_shape=jax.ShapeDtypeStruct((H, C), r4.dtype),
        compiler_params=_params(("parallel",)),
    )(r4)


def _adamw(w, g, m, v, tr):
    R, C = w.shape
    c1 = 1.0 - ADAM_B1
    c2 = 1.0 - ADAM_B2
    bc1 = 1.0 - ADAM_B1 ** ADAM_STEP
    bc2 = 1.0 - ADAM_B2 ** ADAM_STEP

    def kern(w_ref, g_ref, m_ref, v_ref, d_ref, mo_ref, vo_ref):
        gv = g_ref[...]
        mn = ADAM_B1 * m_ref[...] + c1 * gv
        vn = ADAM_B2 * v_ref[...] + c2 * (gv * gv)
        m_hat = mn / bc1
        v_hat = vn / bc2
        d_ref[...] = -ADAM_LR * (m_hat / (jnp.sqrt(v_hat) + ADAM_EPS) + ADAM_WD * w_ref[...])
        mo_ref[...] = mn
        vo_ref[...] = vn

    spec = pl.BlockSpec((tr, C), lambda i: (i, 0))
    shape = jax.ShapeDtypeStruct((R, C), F32)
    return pl.pallas_call(
        kern, name="adamw", grid=(R // tr,), in_specs=[spec] * 4, out_specs=[spec] * 3,
        out_shape=[shape] * 3, compiler_params=_params(("parallel",)),
    )(w, g, m, v)


PACK_COLS = 1024
SHARDED = ("w_in", "conv_w", "w_mem_kv", "w_branch", "w_out")
SMALL = ("norm_g", "b_fg", "b_merge", "a_log", "dt_bias", "gdn_norm_g", "mem_norm_g", "final_norm_g")
ALL_WEIGHTS = ("norm_g", "w_in", "b_fg", "b_merge", "conv_w", "a_log", "dt_bias", "gdn_norm_g",
               "mem_norm_g", "w_mem_kv", "w_branch", "w_out", "final_norm_g")
SHARD_AXIS = {"w_in": 2, "conv_w": 2, "w_mem_kv": 1, "w_branch": 3, "w_out": 1}


def _pack(arrays, row_multiple):
    flat = jnp.concatenate([a.reshape(-1) for a in arrays])
    n = flat.shape[0]
    rows = -(-n // PACK_COLS)
    rows = -(-rows // row_multiple) * row_multiple
    flat = jnp.pad(flat, (0, rows * PACK_COLS - n))
    return flat.reshape(rows, PACK_COLS)


def _unpack(slab, shapes):
    flat = slab.reshape(-1)
    out, off = [], 0
    for shp in shapes:
        n = 1
        for d in shp:
            n *= d
        out.append(flat[off:off + n].reshape(shp))
        off += n
    return out


def _shard_of(full, name, j):
    ax = SHARD_AXIS[name]
    n = full.shape[ax] // N_CHIPS
    return lax.slice_in_dim(full, j * n, (j + 1) * n, axis=ax)


def _row_tile(rows, pref=512):
    best = 8
    for t in range(8, pref + 1, 8):
        if rows % t == 0:
            best = t
    return best


def kernel(x, mem, norm_g, w_in, b_fg, b_merge, conv_w, a_log, dt_bias, gdn_norm_g, mem_norm_g, w_mem_kv, w_branch, w_out, final_norm_g, loss_target, m_norm_g, m_w_in, m_b_fg, m_b_merge, m_conv_w, m_a_log, m_dt_bias, m_gdn_norm_g, m_mem_norm_g, m_w_mem_kv, m_w_branch, m_w_out, m_final_norm_g, v_norm_g, v_w_in, v_b_fg, v_b_merge, v_conv_w, v_a_log, v_dt_bias, v_gdn_norm_g, v_mem_norm_g, v_w_mem_kv, v_w_branch, v_w_out, v_final_norm_g):
    wts = dict(norm_g=norm_g, w_in=w_in, b_fg=b_fg, b_merge=b_merge, conv_w=conv_w, a_log=a_log,
               dt_bias=dt_bias, gdn_norm_g=gdn_norm_g, mem_norm_g=mem_norm_g, w_mem_kv=w_mem_kv,
               w_branch=w_branch, w_out=w_out, final_norm_g=final_norm_g)
    mom = dict(norm_g=m_norm_g, w_in=m_w_in, b_fg=m_b_fg, b_merge=m_b_merge, conv_w=m_conv_w,
               a_log=m_a_log, dt_bias=m_dt_bias, gdn_norm_g=m_gdn_norm_g, mem_norm_g=m_mem_norm_g,
               w_mem_kv=m_w_mem_kv, w_branch=m_w_branch, w_out=m_w_out, final_norm_g=m_final_norm_g)
    vel = dict(norm_g=v_norm_g, w_in=v_w_in, b_fg=v_b_fg, b_merge=v_b_merge, conv_w=v_conv_w,
               a_log=v_a_log, dt_bias=v_dt_bias, gdn_norm_g=v_gdn_norm_g, mem_norm_g=v_mem_norm_g,
               w_mem_kv=v_w_mem_kv, w_branch=v_w_branch, w_out=v_w_out, final_norm_g=v_final_norm_g)

    big = ("w_in", "w_mem_kv", "w_branch", "w_out")
    slab_w = _pack([wts[n].astype(BF16) for n in big], 16)
    all_w = _gather_chips("weight_gather", slab_w)
    all_conv = _gather_chips("conv_gather", _pack([conv_w], 8))
    full = {}
    per_chip = [_unpack(all_w[j], [wts[n].shape for n in big]) for j in range(N_CHIPS)]
    for i, n in enumerate(big):
        full[n] = jnp.concatenate([per_chip[j][i] for j in range(N_CHIPS)], axis=SHARD_AXIS[n])
    full["conv_w"] = jnp.concatenate(
        [_unpack(all_conv[j], [conv_w.shape])[0] for j in range(N_CHIPS)], axis=2)
    w_in_al = _to_aligned(full["w_in"])

    layers = []
    for l in range(DEPTH):
        layers.append(dict(norm_g=norm_g[l], w_in_al=w_in_al[l], b_fg=b_fg[l], b_merge=b_merge[l],
                           conv_w=jnp.pad(full["conv_w"][l], ((0, 4), (0, 0))), a_log=a_log[l],
                           dt_bias=dt_bias[l], gdn_norm_g=gdn_norm_g[l], mem_norm_g=mem_norm_g[l],
                           w_mem_kv=full["w_mem_kv"][l], w_branch=full["w_branch"][l],
                           w_out=full["w_out"][l]))

    loss_lanes, dx, grads, dgf = _local_step(x[0], mem[0], layers, final_norm_g, loss_target[0])

    gfull = {n: jnp.stack([grads[l][n] for l in range(DEPTH)])
             for n in ("norm_g", "b_fg", "b_merge", "conv_w", "a_log", "dt_bias", "gdn_norm_g",
                       "mem_norm_g", "w_mem_kv", "w_branch", "w_out")}
    gfull["w_in"] = _from_aligned(jnp.stack([grads[l]["w_in_al"] for l in range(DEPTH)]))
    gfull["final_norm_g"] = dgf
    loss_local = jnp.sum(loss_lanes).reshape(1)
    small_g = [gfull[n] for n in SMALL] + [loss_local]

    g4 = jnp.stack([_pack([_shard_of(gfull[n], n, j) for n in SHARDED] + small_g, 16)
                    for j in range(N_CHIPS)])
    R = g4.shape[1]
    H = R // 2
    mc = lax.axis_index("c")
    tr = _row_tile(H)
    from_sibling = _sibling_swap_half(g4, H)
    mine = lax.dynamic_slice_in_dim(g4, mc * H, H, axis=1)
    pair = _add_pairs(mine, from_sibling, tr)
    slots = _chip_exchange(pair)
    half = _sum_slots(slots, tr)
    gsum = _sibling_gather(half)

    zero1 = jnp.zeros((1,), F32)
    slab = lambda d: _pack([d[n] for n in SHARDED] + [d[n] for n in SMALL] + [zero1], 16)
    delta_s, m_s, v_s = _adamw(slab(wts), gsum, slab(mom), slab(vel), _row_tile(R))

    names = list(SHARDED) + list(SMALL)
    shapes = [wts[n].shape for n in names] + [(1,)]
    g_un = dict(zip(names + ["loss"], _unpack(gsum, shapes)))
    d_un = dict(zip(names, _unpack(delta_s, shapes[:-1])))
    m_un = dict(zip(names, _unpack(m_s, shapes[:-1])))
    v_un = dict(zip(names, _unpack(v_s, shapes[:-1])))

    loss = g_un["loss"][0]
    return (loss, dx[None], *[g_un[n] for n in ALL_WEIGHTS], *[d_un[n] for n in ALL_WEIGHTS],
            *[m_un[n] for n in ALL_WEIGHTS], *[v_un[n] for n in ALL_WEIGHTS])
```

```python
import functools

import jax
import jax.numpy as jnp
from jax import lax
from jax.experimental import pallas as pl
from jax.experimental.pallas import tpu as pltpu

F32 = jnp.float32
BF16 = jnp.bfloat16
HIGHEST = lax.Precision.HIGHEST
MESH_ID = pl.DeviceIdType.MESH

D_MODEL = 1024
DEPTH = 2
CHUNK = 64
EPS = 1e-6
FOX_HEADS, FOX_DIM = 8, 64
GDN_HEADS, GDN_DIM = 4, 128
MEM_HEADS, MEM_DIM = 4, 128
WIDTH = 512
N_BRANCH = 3
N_IN = 8208
N_AL = 8320
N_CHIPS = 4
NEG = -1e30
LOG2E = 1.4426950408889634
LN2 = 0.6931471805599453

ADAM_LR, ADAM_B1, ADAM_B2, ADAM_EPS, ADAM_WD, ADAM_STEP = 0.001, 0.9, 0.999, 1e-08, 0.01, 10

CB_GATES = 0
CB_BQKV = 2
CB_AQ, CB_AK, CB_AV, CB_AZ, CB_BZ, CB_MQ, CB_MZ = 9, 10, 11, 12, 13, 14, 15
CB_SMALL = 64
LANE_AF, LANE_BA, LANE_BB = 0, 8, 12

NN = ((1,), (0,))
NT = ((1,), (1,))
TN = ((0,), (0,))

VMEM_LIMIT_BYTES = 56 * 1024 * 1024


def _dot(a, b, dims=NN, prec=None):
    return lax.dot_general(a, b, (dims, ((), ())), preferred_element_type=F32, precision=prec)


def _bdot(a, b, ca, cb, prec=None):
    return lax.dot_general(a, b, (((ca,), (cb,)), ((0,), (0,))), preferred_element_type=F32,
                           precision=prec)


def _b16(a):
    return a.astype(BF16)


def _sigmoid(x):
    return 1.0 / (1.0 + jnp.exp(-x))


def _softplus(x):
    return jnp.maximum(x, 0.0) + jnp.log(1.0 + jnp.exp(-jnp.abs(x)))


def _silu_and_grad(x):
    s = _sigmoid(x)
    return x * s, s * (1.0 + x * (1.0 - s))


def _params(semantics):
    return pltpu.CompilerParams(dimension_semantics=semantics, vmem_limit_bytes=VMEM_LIMIT_BYTES)


def _rows(a, ts):
    nd = a.ndim
    return (a, (ts,) + a.shape[1:], lambda i, nd=nd: (i,) + (0,) * (nd - 1))


def _cols(a, ts, width, cb):
    return (a, (ts, width), lambda i, cb=cb: (i, cb))


def _full(a):
    nd = a.ndim
    return (a, a.shape, lambda i, nd=nd: (0,) * nd)


def _orow(S, tail, dtype, ts):
    nd = 1 + len(tail)
    return ((S,) + tuple(tail), dtype, (ts,) + tuple(tail), lambda i, nd=nd: (i,) + (0,) * (nd - 1))


def _oacc(shape, dtype):
    nd = len(shape)
    return (tuple(shape), dtype, tuple(shape), lambda i, nd=nd: (0,) * nd)


def _tiled(name, body, n_steps, ins, outs, scratch=(), reverse=False):
    def rev(imap):
        if not reverse:
            return imap
        return lambda i: imap(n_steps - 1 - i)

    in_specs = [pl.BlockSpec(blk, rev(imap)) for (_, blk, imap) in ins]
    out_specs = [pl.BlockSpec(blk, rev(imap)) for (_, _, blk, imap) in outs]
    out_shape = [jax.ShapeDtypeStruct(shape, dt) for (shape, dt, _, _) in outs]
    n_in, n_out = len(ins), len(outs)

    def kern(*refs):
        step = pl.program_id(0)
        t = (n_steps - 1 - step) if reverse else step
        body(t, step == 0, refs[:n_in], refs[n_in:n_in + n_out], refs[n_in + n_out:])

    res = pl.pallas_call(
        kern, name=name, grid=(n_steps,), in_specs=in_specs, out_specs=out_specs,
        out_shape=out_shape, scratch_shapes=list(scratch),
        compiler_params=_params(("arbitrary",)),
    )(*[a for (a, _, _) in ins])
    return res


def _pick(n, pref):
    if n <= pref:
        return n
    best = None
    for t in range(128, pref + 1, 128):
        if n % t == 0:
            best = t
    assert best is not None, (n, pref)
    return best


def _mm(name, a, b, ta=False, tb=False, out_dtype=F32, tm=1024, tn=1024, tk=1024):
    if ta:
        K, M = a.shape
    else:
        M, K = a.shape
    if tb:
        N, K2 = b.shape
    else:
        K2, N = b.shape
    assert K == K2, (a.shape, b.shape, ta, tb)
    tm, tn, tk = _pick(M, tm), _pick(N, tn), _pick(K, tk)
    nk = K // tk
    a_spec = (pl.BlockSpec((tk, tm), lambda i, j, k: (k, i)) if ta
              else pl.BlockSpec((tm, tk), lambda i, j, k: (i, k)))
    b_spec = (pl.BlockSpec((tn, tk), lambda i, j, k: (j, k)) if tb
              else pl.BlockSpec((tk, tn), lambda i, j, k: (k, j)))
    dims = ((0,) if ta else (1,), (1,) if tb else (0,))

    def kern_single(a_ref, b_ref, o_ref):
        o_ref[...] = _dot(_b16(a_ref[...]), _b16(b_ref[...]), dims).astype(o_ref.dtype)

    def kern_acc(a_ref, b_ref, o_ref, acc_ref):
        k = pl.program_id(2)

        @pl.when(k == 0)
        def _():
            acc_ref[...] = jnp.zeros_like(acc_ref)

        acc_ref[...] += _dot(_b16(a_ref[...]), _b16(b_ref[...]), dims)

        @pl.when(k == nk - 1)
        def _():
            o_ref[...] = acc_ref[...].astype(o_ref.dtype)

    return pl.pallas_call(
        kern_single if nk == 1 else kern_acc, name=name, grid=(M // tm, N // tn, nk),
        in_specs=[a_spec, b_spec],
        out_specs=pl.BlockSpec((tm, tn), lambda i, j, k: (i, j)),
        out_shape=jax.ShapeDtypeStruct((M, N), out_dtype),
        scratch_shapes=[] if nk == 1 else [pltpu.VMEM((tm, tn), F32)],
        compiler_params=_params(("parallel", "parallel", "arbitrary")),
    )(a, b)


def _rms_fwd(name, x, g, ts):
    S, D = x.shape

    def body(t, first, ins, outs, scratch):
        x_ref, g_ref = ins
        h_ref, r_ref = outs
        xv = x_ref[...]
        r = lax.rsqrt(jnp.mean(xv * xv, axis=1, keepdims=True) + EPS)
        h_ref[...] = (xv * r * g_ref[...]).astype(h_ref.dtype)
        r_ref[...] = r

    return _tiled(name, body, S // ts, [_rows(x, ts), _full(g.reshape(1, D))],
                  [_orow(S, (D,), BF16, ts), _orow(S, (1,), F32, ts)])


def _rms_bwd(name, dh, x, rstd, g, dres, ts):
    S, D = x.shape

    def body(t, first, ins, outs, scratch):
        dh_ref, x_ref, r_ref, g_ref, dres_ref = ins
        dx_ref, dg_ref = outs
        r = r_ref[...]
        xh = x_ref[...] * r
        dhv = dh_ref[...]
        dxh = dhv * g_ref[...]
        dx_ref[...] = dres_ref[...] + r * (dxh - xh * jnp.mean(dxh * xh, axis=1, keepdims=True))

        @pl.when(first)
        def _():
            dg_ref[...] = jnp.zeros_like(dg_ref)

        dg_ref[0:1, :] += jnp.sum(dhv * xh, axis=0, keepdims=True)

    dx, dg = _tiled(name, body, S // ts,
                    [_rows(dh, ts), _rows(x, ts), _rows(rstd, ts), _full(g.reshape(1, D)), _rows(dres, ts)],
                    [_orow(S, (D,), F32, ts), _oacc((8, D), F32)])
    return dx, dg[0]


def _loss_head(x, g, target, ts):
    S, D = x.shape

    def body(t, first, ins, outs, scratch):
        x_ref, g_ref, tgt_ref = ins
        dx_ref, dg_ref, loss_ref = outs
        xv = x_ref[...]
        gv = g_ref[...]
        r = lax.rsqrt(jnp.mean(xv * xv, axis=1, keepdims=True) + EPS)
        xh = xv * r
        err = xh * gv - tgt_ref[...]
        dy = err * (1.0 / D)
        dxh = dy * gv
        dx_ref[...] = r * (dxh - xh * jnp.mean(dxh * xh, axis=1, keepdims=True))

        @pl.when(first)
        def _():
            dg_ref[...] = jnp.zeros_like(dg_ref)
            loss_ref[...] = jnp.zeros_like(loss_ref)

        dg_ref[0:1, :] += jnp.sum(dy * xh, axis=0, keepdims=True)
        per_lane = jnp.sum(err * err, axis=0, keepdims=True)
        loss_ref[0:1, :] += per_lane * (0.5 / D)

    dx, dg, loss = _tiled("loss_head", body, S // ts,
                          [_rows(x, ts), _full(g.reshape(1, D)), _rows(target, ts)],
                          [_orow(S, (D,), F32, ts), _oacc((8, D), F32), _oacc((8, D), F32)])
    return dx, dg[0], loss[0]


def _scan_rows(x, length, seg, reverse=False):
    row = lax.broadcasted_iota(jnp.int32, x.shape, 0) % seg
    k = 1
    while k < seg:
        if reverse:
            x = x + jnp.where(row < seg - k, pltpu.roll(x, length - k, 0), 0.0)
        else:
            x = x + jnp.where(row >= k, pltpu.roll(x, k, 0), 0.0)
        k *= 2
    return x


def _fox_decay(z, b_fg128, ts):
    S = z.shape[0]

    def body(t, first, ins, outs, scratch):
        zs_ref, b_ref = ins
        f_ref, hi_ref, mid_ref, lo_ref = outs
        (carry,) = scratch

        @pl.when(first)
        def _():
            carry[...] = jnp.zeros_like(carry)

        logf = -_softplus(-(zs_ref[...] + b_ref[...]))
        run = _scan_rows(logf, ts, ts) + carry[0:1, :]
        f_ref[...] = run
        carry[0:1, :] = run[ts - 1:ts, :]
        f2 = run * LOG2E
        hi = f2.astype(BF16)
        r1 = f2 - hi.astype(F32)
        mid = r1.astype(BF16)
        hi_ref[...] = hi
        mid_ref[...] = mid
        lo_ref[...] = (r1 - mid.astype(F32)).astype(BF16)

    piece = _orow(S, (128,), BF16, ts)
    return _tiled("fox_decay", body, S // ts,
                  [_cols(z, ts, 128, CB_SMALL), _full(b_fg128)],
                  [_orow(S, (128,), F32, ts), piece, piece, piece], scratch=[pltpu.VMEM((8, 128), F32)])


def _fox_decay_bwd(dFk128, dFq128, z, b_fg128, ts):
    S = z.shape[0]

    def body(t, first, ins, outs, scratch):
        dfk_ref, dfq_ref, zs_ref, b_ref = ins
        daf_ref, db_ref = outs
        (carry,) = scratch

        @pl.when(first)
        def _():
            carry[...] = jnp.zeros_like(carry)
            db_ref[...] = jnp.zeros_like(db_ref)

        run = _scan_rows(dfk_ref[...] + dfq_ref[...], ts, ts, reverse=True) + carry[0:1, :]
        carry[0:1, :] = run[0:1, :]
        daf = run * _sigmoid(-(zs_ref[...] + b_ref[...]))
        daf_ref[...] = daf
        db_ref[0:1, :] += jnp.sum(daf, axis=0, keepdims=True)

    daf, db = _tiled("fox_decay_bwd", body, S // ts,
                     [_rows(dFk128, ts), _rows(dFq128, ts), _cols(z, ts, 128, CB_SMALL), _full(b_fg128)],
                     [_orow(S, (128,), F32, ts), _oacc((8, 128), F32)],
                     scratch=[pltpu.VMEM((8, 128), F32)], reverse=True)
    return daf, db[0]


FOX_AUG = 80


def _fox_fwd(q_aug, kT_aug, v_aug, tq):
    H, S, da = q_aug.shape
    dv = v_aug.shape[2]
    d = FOX_DIM
    tk = tq
    qscale = (d ** -0.5) * LOG2E

    def kern(q_ref, kT_ref, v_ref, o_ref, lse_ref):
        i = pl.program_id(1)
        col = lax.broadcasted_iota(jnp.int32, (1, da), 1)
        qb = _b16(q_ref[...] * jnp.where(col < d, qscale, 1.0))

        def scores(j):
            off = pl.multiple_of(j * tk, tk)
            return _dot(qb, kT_ref[:, pl.ds(off, tk)])

        def update(j, s, m, acc):
            off = pl.multiple_of(j * tk, tk)
            m_new = jnp.maximum(m, jnp.max(s, axis=1, keepdims=True))
            alpha = jnp.exp2(m - m_new)
            p = jnp.exp2(s - m_new)
            acc = alpha * acc + _dot(_b16(p), v_ref[pl.ds(off, tk), :])
            return m_new, acc

        def body(j, carry):
            m, acc, s = carry
            s_next = scores(j + 1)
            m, acc = update(j, s, m, acc)
            return m, acc, s_next

        init = (jnp.full((tq, 1), NEG, F32), jnp.zeros((tq, dv), F32), scores(0))
        m, acc, s = lax.fori_loop(0, i, body, init)
        r = lax.broadcasted_iota(jnp.int32, (tq, tk), 0)
        c = lax.broadcasted_iota(jnp.int32, (tq, tk), 1)
        m, acc = update(i, jnp.where(c <= r, s, NEG), m, acc)
        l = acc[:, d:d + 1]
        o_ref[...] = acc[:, :d] / l
        lse_ref[...] = m * LN2 + jnp.log(l)

    return pl.pallas_call(
        kern, name="fox_fwd", grid=(H, S // tq),
        in_specs=[pl.BlockSpec((None, tq, da), lambda h, i: (h, i, 0)),
                  pl.BlockSpec((None, da, S), lambda h, i: (h, 0, 0)),
                  pl.BlockSpec((None, S, dv), lambda h, i: (h, 0, 0))],
        out_specs=[pl.BlockSpec((None, tq, d), lambda h, i: (h, i, 0)),
                   pl.BlockSpec((None, tq, 1), lambda h, i: (h, i, 0))],
        out_shape=[jax.ShapeDtypeStruct((H, S, d), F32), jax.ShapeDtypeStruct((H, S, 1), F32)],
        compiler_params=_params(("parallel", "arbitrary")),
    )(q_aug, kT_aug, v_aug)


def _fox_bwd(q, k, kT, v, do, fq_row, fk_col, lse_row, delta_row, tq):
    H, S, d = q.shape
    tk = tq
    nq = S // tq
    scale = d ** -0.5

    def kern(q_ref, k_ref, kT_ref, v_ref, do_ref, fq_ref, fk_ref, lse_ref, dl_ref,
             dqT_ref, dk_ref, dv_ref, dfk_ref, dfq_ref):
        j = pl.program_id(1)

        @pl.when(j == 0)
        def _():
            dqT_ref[...] = jnp.zeros_like(dqT_ref)
            dfq_ref[...] = jnp.zeros_like(dfq_ref)

        kb = k_ref[...]
        kTb = kT_ref[...]
        vb = v_ref[...]
        fk = fk_ref[...]

        def step(i, carry, masked):
            dk, dv, dfk = carry
            off = pl.multiple_of(i * tq, tq)
            qi = q_ref[pl.ds(off, tq), :] * scale
            doi = do_ref[pl.ds(off, tq), :]
            sT = _dot(kb, qi, NT) + (fq_ref[:, pl.ds(off, tq)] - fk)
            if masked:
                r = lax.broadcasted_iota(jnp.int32, (tk, tq), 0)
                c = lax.broadcasted_iota(jnp.int32, (tk, tq), 1)
                sT = jnp.where(r <= c, sT, NEG)
            pT = jnp.exp(sT - lse_ref[:, pl.ds(off, tq)])
            dv = dv + _dot(_b16(pT), doi)
            dpT = _dot(vb, doi, NT)
            dsT = pT * (dpT - dl_ref[:, pl.ds(off, tq)])
            dfk = dfk - jnp.sum(dsT, axis=1, keepdims=True)
            dfq_ref[:, pl.ds(off, tq)] += jnp.sum(dsT, axis=0, keepdims=True)
            dsb = _b16(dsT)
            dk = dk + _dot(dsb, qi)
            dqT_ref[:, pl.ds(off, tq)] += _dot(kTb, dsb) * scale
            return dk, dv, dfk

        init = (jnp.zeros((tk, d), F32), jnp.zeros((tk, d), F32), jnp.zeros((tk, 1), F32))
        carry = step(j, init, True)
        dk, dv, dfk = lax.fori_loop(j + 1, nq, lambda i, c: step(i, c, False), carry)
        dk_ref[...] = dk
        dv_ref[...] = dv
        dfk_ref[...] = dfk

    tile = lambda h, j: (h, j, 0)
    whole = lambda h, j: (h, 0, 0)
    return pl.pallas_call(
        kern, name="fox_bwd", grid=(H, S // tk),
        in_specs=[pl.BlockSpec((None, S, d), whole),
                  pl.BlockSpec((None, tk, d), tile),
                  pl.BlockSpec((None, d, tk), lambda h, j: (h, 0, j)),
                  pl.BlockSpec((None, tk, d), tile),
                  pl.BlockSpec((None, S, d), whole),
                  pl.BlockSpec((None, 1, S), whole),
                  pl.BlockSpec((None, tk, 1), tile),
                  pl.BlockSpec((None, 1, S), whole),
                  pl.BlockSpec((None, 1, S), whole)],
        out_specs=[pl.BlockSpec((None, d, S), whole),
                   pl.BlockSpec((None, tk, d), tile),
                   pl.BlockSpec((None, tk, d), tile),
                   pl.BlockSpec((None, tk, 1), tile),
                   pl.BlockSpec((None, 1, S), whole)],
        out_shape=[jax.ShapeDtypeStruct((H, d, S), F32), jax.ShapeDtypeStruct((H, S, d), F32),
                   jax.ShapeDtypeStruct((H, S, d), F32), jax.ShapeDtypeStruct((H, S, 1), F32),
                   jax.ShapeDtypeStruct((H, 1, S), F32)],
        compiler_params=_params(("parallel", "arbitrary")),
    )(q, k, kT, v, do, fq_row, fk_col, lse_row, delta_row)


def _heads_major(a, H, d):
    S = a.shape[0]
    return a.reshape(S, H, d).transpose(1, 0, 2)


def _heads_minor(a):
    H, S, d = a.shape
    return a.transpose(1, 0, 2).reshape(S, H * d)


def _lane_pick(x128, lane):
    return x128[:, lane:lane + 1]


def _l2_fwd(y):
    return lax.rsqrt(jnp.sum(y * y, axis=1, keepdims=True) + EPS)


def _gdn_prep(z, conv_w, a128, dt128, ts):
    S = z.shape[0]
    C3 = 3 * WIDTH
    hb = ts // 8

    def body(t, first, ins, outs, scratch):
        x_ref, halo_ref, zs_ref, w_ref, a_ref, dt_ref = ins
        qkv_ref, c_ref, gb_ref = outs
        halo = jnp.where(t > 0, halo_ref[...], 0.0)
        xe = jnp.concatenate([halo, x_ref[...]], axis=0)
        w = w_ref[...]
        c = w[3:4, :] * xe[8:, :]
        for back in (1, 2, 3):
            c = c + w[3 - back:4 - back, :] * pltpu.roll(xe, back, 0)[8:, :]
        c_ref[...] = c
        y = c * _sigmoid(c)
        for h in range(GDN_HEADS):
            lo = h * GDN_DIM
            yq = y[:, lo:lo + GDN_DIM]
            qkv_ref[:, lo:lo + GDN_DIM] = yq * (_l2_fwd(yq) * (GDN_DIM ** -0.5))
            yk = y[:, WIDTH + lo:WIDTH + lo + GDN_DIM]
            qkv_ref[:, WIDTH + lo:WIDTH + lo + GDN_DIM] = yk * _l2_fwd(yk)
        qkv_ref[:, 2 * WIDTH:] = y[:, 2 * WIDTH:]
        zs = zs_ref[...]
        lane = lax.broadcasted_iota(jnp.int32, zs.shape, 1)
        g = -jnp.exp(a_ref[...]) * _softplus(zs + dt_ref[...])
        G = _scan_rows(g, ts, CHUNK)
        beta = _sigmoid(zs)
        out = jnp.where(lane < 8, pltpu.roll(g, 128 - LANE_BA, 1), jnp.where(lane < LANE_BB, G, beta))
        gb_ref[...] = out

    x_in = (z, (ts, C3), lambda i: (i, CB_BQKV))
    halo_in = (z, (8, C3), lambda i: (jnp.maximum(i * hb - 1, 0), CB_BQKV))
    return _tiled("gdn_prep", body, S // ts,
                  [x_in, halo_in, _cols(z, ts, 128, CB_SMALL), _full(conv_w), _full(a128), _full(dt128)],
                  [_orow(S, (C3,), F32, ts), _orow(S, (C3,), F32, ts), _orow(S, (128,), F32, ts)])


def _chunk_masks(nc):
    r = lax.broadcasted_iota(jnp.int32, (nc, CHUNK, CHUNK), 1)
    c = lax.broadcasted_iota(jnp.int32, (nc, CHUNK, CHUNK), 2)
    return c <= r, c < r, c == r


def _chunk_local(qh, kh, vh, Gc, Gr, beta):
    nc = qh.shape[0]
    incl, strict, _ = _chunk_masks(nc)
    gamma = jnp.exp(jnp.where(incl, Gc - Gr, NEG))
    kb = kh * beta
    P = _bdot(_b16(kb), _b16(kh), 2, 2)
    Qk = _bdot(_b16(qh), _b16(kh), 2, 2)
    eG = jnp.exp(Gc)
    Gl = Gc[:, CHUNK - 1:CHUNK, :]
    edec = jnp.exp(Gl - Gc)
    return incl, strict, gamma, kb, P, Qk, eG, edec


def _gdn_local_fwd(qkv, gb, grow, ts):
    S = qkv.shape[0]
    nc = ts // CHUNK

    def body(t, first, ins, outs, scratch):
        q_ref, k_ref, v_ref, gb_ref, gr_ref = ins
        u_ref, w_ref, qd_ref, kd_ref, aqk_ref, T_ref = outs
        gbv = gb_ref[...]
        for h in range(GDN_HEADS):
            lo = h * GDN_DIM
            qh = q_ref[:, lo:lo + GDN_DIM].reshape(nc, CHUNK, GDN_DIM)
            kh = k_ref[:, lo:lo + GDN_DIM].reshape(nc, CHUNK, GDN_DIM)
            vh = v_ref[:, lo:lo + GDN_DIM].reshape(nc, CHUNK, GDN_DIM)
            Gc = _lane_pick(gbv, LANE_BA + h).reshape(nc, CHUNK, 1)
            beta = _lane_pick(gbv, LANE_BB + h).reshape(nc, CHUNK, 1)
            Gr = gr_ref[h].reshape(nc, 1, CHUNK)
            incl, strict, gamma, kb, P, Qk, eG, edec = _chunk_local(qh, kh, vh, Gc, Gr, beta)
            A = jnp.where(strict, P * gamma, 0.0)
            _, _, eye = _chunk_masks(nc)
            T = jnp.where(eye, 1.0, 0.0) - A
            X = A
            for _ in range(5):
                X = _bdot(X, X, 2, 1, HIGHEST)
                T = T + _bdot(T, X, 2, 1, HIGHEST)
            u = _bdot(T, vh * beta, 2, 1, HIGHEST)
            w = _bdot(T, kb * eG, 2, 1, HIGHEST)
            u_ref[:, lo:lo + GDN_DIM] = u.reshape(ts, GDN_DIM)
            w_ref[:, lo:lo + GDN_DIM] = w.reshape(ts, GDN_DIM)
            qd_ref[:, lo:lo + GDN_DIM] = (qh * eG).reshape(ts, GDN_DIM)
            kd_ref[:, lo:lo + GDN_DIM] = (kh * edec).reshape(ts, GDN_DIM)
            aqk_ref[h] = jnp.where(incl, Qk * gamma, 0.0).reshape(ts, CHUNK)
            T_ref[h] = T.reshape(ts, CHUNK)

    wide = _orow(S, (WIDTH,), F32, ts)
    perhead = ((GDN_HEADS, S, CHUNK), F32, (GDN_HEADS, ts, CHUNK), lambda i: (0, i, 0))
    return _tiled("gdn_local_fwd", body, S // ts,
                  [_cols(qkv, ts, WIDTH, 0), _cols(qkv, ts, WIDTH, 1), _cols(qkv, ts, WIDTH, 2),
                   _rows(gb, ts), (grow, (GDN_HEADS, nc, CHUNK), lambda i: (0, i, 0))],
                  [wide, wide, wide, wide, perhead, perhead])


def _gdn_scan_fwd(u, w, qd, kd, aqk, gb, ts):
    S = u.shape[0]
    nc = ts // CHUNK
    N = S // CHUNK

    def body(t, first, ins, outs, scratch):
        u_ref, w_ref, qd_ref, kd_ref, aqk_ref, gb_ref = ins
        o_ref, vn_ref, st_ref = outs
        (state,) = scratch

        @pl.when(first)
        def _():
            state[...] = jnp.zeros_like(state)

        def chunk(c, _):
            r0 = pl.multiple_of(c * CHUNK, CHUNK)
            rows = pl.ds(r0, CHUNK)
            glast = gb_ref[pl.ds(r0 + CHUNK - 1, 1), :]
            for h in range(GDN_HEADS):
                lo = h * GDN_DIM
                cols = slice(lo, lo + GDN_DIM)
                Sh = state[h]
                st_ref[c, h] = Sh
                Sb = _b16(Sh)
                vn = u_ref[rows, cols] - _dot(_b16(w_ref[rows, cols]), Sb)
                vnb = _b16(vn)
                o = _dot(_b16(qd_ref[rows, cols]), Sb) + _dot(_b16(aqk_ref[h, rows, :]), vnb)
                egl = jnp.exp(glast[:, LANE_BA + h:LANE_BA + h + 1])
                state[h] = Sh * egl + _dot(_b16(kd_ref[rows, cols]), vnb, TN)
                o_ref[rows, cols] = o
                vn_ref[rows, cols] = vn
            return 0

        lax.fori_loop(0, nc, chunk, 0)

    wide_in = lambda a: _rows(a, ts)
    wide = _orow(S, (WIDTH,), F32, ts)
    states = ((N, GDN_HEADS, GDN_DIM, GDN_DIM), F32, (nc, GDN_HEADS, GDN_DIM, GDN_DIM),
              lambda i: (i, 0, 0, 0))
    return _tiled("gdn_scan_fwd", body, S // ts,
                  [wide_in(u), wide_in(w), wide_in(qd), wide_in(kd),
                   (aqk, (GDN_HEADS, ts, CHUNK), lambda i: (0, i, 0)), _rows(gb, ts)],
                  [wide, wide, states],
                  scratch=[pltpu.VMEM((GDN_HEADS, GDN_DIM, GDN_DIM), F32)])


def _gdn_scan_bwd(do, w, qd, kd, aqk, vn, states, gb, ts):
    S = do.shape[0]
    nc = ts // CHUNK
    N = S // CHUNK

    def body(t, first, ins, outs, scratch):
        do_ref, w_ref, qd_ref, kd_ref, aqk_ref, vn_ref, st_ref, gb_ref = ins
        du_ref, dw_ref, dqd_ref, dkd_ref, daqk_ref, dgl_ref = outs
        (dstate,) = scratch

        @pl.when(first)
        def _():
            dstate[...] = jnp.zeros_like(dstate)

        r = lax.broadcasted_iota(jnp.int32, (CHUNK, CHUNK), 0)
        cc = lax.broadcasted_iota(jnp.int32, (CHUNK, CHUNK), 1)
        incl = cc <= r
        lane = lax.broadcasted_iota(jnp.int32, (1, 128), 1)

        def chunk(k, _):
            c = nc - 1 - k
            r0 = pl.multiple_of(c * CHUNK, CHUNK)
            rows = pl.ds(r0, CHUNK)
            glast = gb_ref[pl.ds(r0 + CHUNK - 1, 1), :]
            dgl_row = jnp.zeros((1, 128), F32)
            for h in range(GDN_HEADS):
                lo = h * GDN_DIM
                cols = slice(lo, lo + GDN_DIM)
                Sh = st_ref[c, h]
                Sb = _b16(Sh)
                dS = dstate[h]
                dSb = _b16(dS)
                dob = _b16(do_ref[rows, cols])
                aqkb = _b16(aqk_ref[h, rows, :])
                vnb = _b16(vn_ref[rows, cols])
                kdb = _b16(kd_ref[rows, cols])
                dvn = _dot(aqkb, dob, TN) + _dot(kdb, dSb)
                dvnb = _b16(dvn)
                daqk_ref[h, rows, :] = jnp.where(incl, _dot(dob, vnb, NT), 0.0)
                dqd_ref[rows, cols] = _dot(dob, Sb, NT)
                dkd_ref[rows, cols] = _dot(vnb, dSb, NT)
                dw_ref[rows, cols] = -_dot(dvnb, Sb, NT)
                du_ref[rows, cols] = dvn
                egl = jnp.exp(glast[:, LANE_BA + h:LANE_BA + h + 1])
                dgl = egl * jnp.sum(jnp.sum(dS * Sh, axis=1, keepdims=True), axis=0, keepdims=True)
                dgl_row = jnp.where(lane == h, dgl, dgl_row)
                dstate[h] = (_dot(_b16(qd_ref[rows, cols]), dob, TN) + egl * dS
                             - _dot(_b16(w_ref[rows, cols]), dvnb, TN))
            dgl_ref[pl.ds(c, 1), :] = dgl_row
            return 0

        lax.fori_loop(0, nc, chunk, 0)

    wide_in = lambda a: _rows(a, ts)
    wide = _orow(S, (WIDTH,), F32, ts)
    perhead_in = lambda a: (a, (GDN_HEADS, ts, CHUNK), lambda i: (0, i, 0))
    perhead = ((GDN_HEADS, S, CHUNK), F32, (GDN_HEADS, ts, CHUNK), lambda i: (0, i, 0))
    return _tiled("gdn_scan_bwd", body, S // ts,
                  [wide_in(do), wide_in(w), wide_in(qd), wide_in(kd), perhead_in(aqk), wide_in(vn),
                   (states, (nc, GDN_HEADS, GDN_DIM, GDN_DIM), lambda i: (i, 0, 0, 0)), _rows(gb, ts)],
                  [wide, wide, wide, wide, perhead, ((N, 128), F32, (nc, 128), lambda i: (i, 0))],
                  scratch=[pltpu.VMEM((GDN_HEADS, GDN_DIM, GDN_DIM), F32)], reverse=True)


def _gdn_local_bwd(qkv, gb, grow, T, du, dw, dqd, dkd, daqk, dgl, ts):
    S = qkv.shape[0]
    nc = ts // CHUNK

    def body(t, first, ins, outs, scratch):
        (q_ref, k_ref, v_ref, gb_ref, gr_ref, T_ref, du_ref, dw_ref, dqd_ref, dkd_ref,
         daqk_ref, dgl_ref) = ins
        dqkv_ref, dgb_ref = outs
        gbv = gb_ref[...]
        dglv = dgl_ref[...]
        lane = lax.broadcasted_iota(jnp.int32, (ts, 128), 1)
        dG_all = jnp.zeros((ts, 128), F32)
        dbeta_all = jnp.zeros((ts, 128), F32)
        for h in range(GDN_HEADS):
            lo = h * GDN_DIM
            cols = slice(lo, lo + GDN_DIM)
            r3 = lambda ref: ref[:, cols].reshape(nc, CHUNK, GDN_DIM)
            qh, kh, vh = r3(q_ref), r3(k_ref), r3(v_ref)
            duh, dwh, dqdh, dkdh = r3(du_ref), r3(dw_ref), r3(dqd_ref), r3(dkd_ref)
            Gc = _lane_pick(gbv, LANE_BA + h).reshape(nc, CHUNK, 1)
            beta = _lane_pick(gbv, LANE_BB + h).reshape(nc, CHUNK, 1)
            Gr = gr_ref[h].reshape(nc, 1, CHUNK)
            Th = T_ref[h].reshape(nc, CHUNK, CHUNK)
            daq = daqk_ref[h].reshape(nc, CHUNK, CHUNK)
            incl, strict, gamma, kb, P, Qk, eG, edec = _chunk_local(qh, kh, vh, Gc, Gr, beta)
            _, _, eye = _chunk_masks(nc)
            vb = vh * beta
            kbg = kb * eG
            dvb = _bdot(Th, duh, 1, 1, HIGHEST)
            dkbg = _bdot(Th, dwh, 1, 1, HIGHEST)
            dT = _bdot(duh, vb, 2, 2, HIGHEST) + _bdot(dwh, kbg, 2, 2, HIGHEST)
            M1 = _bdot(Th, dT, 1, 1, HIGHEST)
            dA = jnp.where(strict, -_bdot(M1, Th, 2, 2, HIGHEST), 0.0)
            dP = dA * gamma
            dQ = daq * gamma
            dgam = (dA * P + daq * Qk) * gamma
            dPb, dQb = _b16(dP), _b16(dQ)
            khb, qhb, kbb = _b16(kh), _b16(qh), _b16(kb)
            dq = _bdot(dQb, khb, 2, 1) + dqdh * eG
            dkb = _bdot(dPb, khb, 2, 1) + dkbg * eG
            dk = (_bdot(dQb, qhb, 1, 1) + _bdot(dPb, kbb, 1, 1) + dkdh * edec + dkb * beta)
            dbeta = (jnp.sum(dkb * kh, axis=2, keepdims=True) + jnp.sum(dvb * vh, axis=2, keepdims=True))
            dv = dvb * beta
            col_as_col = jnp.sum(jnp.where(eye, jnp.sum(dgam, axis=1, keepdims=True), 0.0),
                                 axis=2, keepdims=True)
            kd_term = jnp.sum(dkdh * kh * edec, axis=2, keepdims=True)
            dG = (jnp.sum(dgam, axis=2, keepdims=True) - col_as_col
                  + jnp.sum(dqdh * qh * eG, axis=2, keepdims=True)
                  + jnp.sum(dkbg * kbg, axis=2, keepdims=True) - kd_term)
            dgl_h = dglv[:, h:h + 1].reshape(nc, 1, 1) + jnp.sum(kd_term, axis=1, keepdims=True)
            last = lax.broadcasted_iota(jnp.int32, (nc, CHUNK, 1), 1) == CHUNK - 1
            dG = dG + jnp.where(last, dgl_h, 0.0)
            dqkv_ref[:, cols] = dq.reshape(ts, GDN_DIM)
            dqkv_ref[:, WIDTH + lo:WIDTH + lo + GDN_DIM] = dk.reshape(ts, GDN_DIM)
            dqkv_ref[:, 2 * WIDTH + lo:2 * WIDTH + lo + GDN_DIM] = dv.reshape(ts, GDN_DIM)
            dG_all = jnp.where(lane == LANE_BA + h, dG.reshape(ts, 1), dG_all)
            dbeta_all = jnp.where(lane == LANE_BB + h, dbeta.reshape(ts, 1), dbeta_all)
        dg_all = _scan_rows(dG_all, ts, CHUNK, reverse=True)
        dgb_ref[...] = jnp.where(lane < LANE_BB, dg_all, dbeta_all)

    wide_in = lambda a: _rows(a, ts)
    perhead_in = lambda a: (a, (GDN_HEADS, ts, CHUNK), lambda i: (0, i, 0))
    return _tiled("gdn_local_bwd", body, S // ts,
                  [_cols(qkv, ts, WIDTH, 0), _cols(qkv, ts, WIDTH, 1), _cols(qkv, ts, WIDTH, 2),
                   _rows(gb, ts), (grow, (GDN_HEADS, nc, CHUNK), lambda i: (0, i, 0)), perhead_in(T),
                   wide_in(du), wide_in(dw), wide_in(dqd), wide_in(dkd), perhead_in(daqk),
                   (dgl, (nc, 128), lambda i: (i, 0))],
                  [_orow(S, (3 * WIDTH,), F32, ts), _orow(S, (128,), F32, ts)])


def _gdn_prep_bwd(dqkv, dgb, cpre, z, conv_w, a128, dt128, ts):
    S = z.shape[0]
    C3 = 3 * WIDTH
    hb = ts // 8
    n_tiles = S // ts

    def dpre(dq, c):
        y, dsil = _silu_and_grad(c)
        parts = []
        for h in range(GDN_HEADS):
            lo = h * GDN_DIM
            yq = y[:, lo:lo + GDN_DIM]
            rq = _l2_fwd(yq)
            nq = yq * rq
            dn = dq[:, lo:lo + GDN_DIM] * (GDN_DIM ** -0.5)
            parts.append(rq * (dn - nq * jnp.sum(dn * nq, axis=1, keepdims=True)))
        for h in range(GDN_HEADS):
            lo = WIDTH + h * GDN_DIM
            yk = y[:, lo:lo + GDN_DIM]
            rk = _l2_fwd(yk)
            nk = yk * rk
            dn = dq[:, lo:lo + GDN_DIM]
            parts.append(rk * (dn - nk * jnp.sum(dn * nk, axis=1, keepdims=True)))
        parts.append(dq[:, 2 * WIDTH:])
        return jnp.concatenate(parts, axis=1) * dsil

    def body(t, first, ins, outs, scratch):
        (dq_ref, dqn_ref, c_ref, cn_ref, x_ref, xp_ref, zs_ref, dgb_ref, w_ref, a_ref, dt_ref) = ins
        dx_ref, dzs_ref, dw_ref, dad_ref = outs

        @pl.when(first)
        def _():
            dw_ref[...] = jnp.zeros_like(dw_ref)
            dad_ref[...] = jnp.zeros_like(dad_ref)

        dc = dpre(dq_ref[...], c_ref[...])
        dcn = jnp.where(t < n_tiles - 1, dpre(dqn_ref[...], cn_ref[...]), 0.0)
        dce = jnp.concatenate([dc, dcn], axis=0)
        w = w_ref[...]
        dx = w[3:4, :] * dc
        for back in (1, 2, 3):
            dx = dx + w[3 - back:4 - back, :] * pltpu.roll(dce, ts + 8 - back, 0)[:ts, :]
        dx_ref[...] = dx
        halo = jnp.where(t > 0, xp_ref[...], 0.0)
        xe = jnp.concatenate([halo, x_ref[...]], axis=0)
        dw_ref[3:4, :] += jnp.sum(dc * xe[8:, :], axis=0, keepdims=True)
        for back in (1, 2, 3):
            dw_ref[3 - back:4 - back, :] += jnp.sum(dc * pltpu.roll(xe, back, 0)[8:, :], axis=0,
                                                     keepdims=True)
        zs = zs_ref[...]
        dgb = dgb_ref[...]
        lane = lax.broadcasted_iota(jnp.int32, zs.shape, 1)
        arg = zs + dt_ref[...]
        nega = -jnp.exp(a_ref[...])
        dba = dgb * nega * _sigmoid(arg)
        beta = _sigmoid(zs)
        dbb = dgb * beta * (1.0 - beta)
        dzs_ref[...] = jnp.where((lane >= LANE_BA) & (lane < LANE_BB), dba,
                                 jnp.where((lane >= LANE_BB) & (lane < LANE_BB + 4), dbb, 0.0))
        dad_ref[0:1, :] += jnp.sum(dgb * nega * _softplus(arg), axis=0, keepdims=True)
        dad_ref[1:2, :] += jnp.sum(dba, axis=0, keepdims=True)

    nxt = lambda i: (jnp.minimum((i + 1) * hb, S // 8 - 1), 0)
    prv = lambda i: (jnp.maximum(i * hb - 1, 0), CB_BQKV)
    return _tiled("gdn_prep_bwd", body, n_tiles,
                  [_rows(dqkv, ts), (dqkv, (8, C3), nxt), _rows(cpre, ts), (cpre, (8, C3), nxt),
                   (z, (ts, C3), lambda i: (i, CB_BQKV)), (z, (8, C3), prv),
                   _cols(z, ts, 128, CB_SMALL), _rows(dgb, ts), _full(conv_w), _full(a128), _full(dt128)],
                  [_orow(S, (C3,), F32, ts), _orow(S, (128,), F32, ts), _oacc((8, C3), F32),
                   _oacc((8, 128), F32)])


def _mem_attn_fwd(z, mk, mv, ts):
    S = z.shape[0]

    def body(t, first, ins, outs, scratch):
        q_ref, mk_ref, mv_ref = ins
        (o_ref,) = outs
        for h in range(MEM_HEADS):
            cols = slice(h * MEM_DIM, (h + 1) * MEM_DIM)
            s = _dot(_b16(q_ref[:, cols]), _b16(mk_ref[:, cols]), NT) * (MEM_DIM ** -0.5)
            m = jnp.max(s, axis=1, keepdims=True)
            p = jnp.exp(s - m)
            p = p / jnp.sum(p, axis=1, keepdims=True)
            o_ref[:, cols] = _dot(_b16(p), _b16(mv_ref[:, cols]))

    (o,) = _tiled("mem_attn_fwd", body, S // ts, [_cols(z, ts, WIDTH, CB_MQ), _full(mk), _full(mv)],
                  [_orow(S, (WIDTH,), F32, ts)])
    return o


def _mem_attn_bwd(do, z, mk, mv, ts):
    S = z.shape[0]
    M = mk.shape[0]

    def body(t, first, ins, outs, scratch):
        do_ref, q_ref, mk_ref, mv_ref = ins
        dq_ref, dmk_ref, dmv_ref = outs

        @pl.when(first)
        def _():
            dmk_ref[...] = jnp.zeros_like(dmk_ref)
            dmv_ref[...] = jnp.zeros_like(dmv_ref)

        scale = MEM_DIM ** -0.5
        for h in range(MEM_HEADS):
            cols = slice(h * MEM_DIM, (h + 1) * MEM_DIM)
            qb = _b16(q_ref[:, cols])
            kb = _b16(mk_ref[:, cols])
            dob = _b16(do_ref[:, cols])
            s = _dot(qb, kb, NT) * scale
            m = jnp.max(s, axis=1, keepdims=True)
            p = jnp.exp(s - m)
            p = p / jnp.sum(p, axis=1, keepdims=True)
            dmv_ref[:, cols] += _dot(_b16(p), dob, TN)
            dp = _dot(dob, _b16(mv_ref[:, cols]), NT)
            ds = p * (dp - jnp.sum(dp * p, axis=1, keepdims=True)) * scale
            dsb = _b16(ds)
            dq_ref[:, cols] = _dot(dsb, kb)
            dmk_ref[:, cols] += _dot(dsb, qb, TN)

    return _tiled("mem_attn_bwd", body, S // ts,
                  [_rows(do, ts), _cols(z, ts, WIDTH, CB_MQ), _full(mk), _full(mv)],
                  [_orow(S, (WIDTH,), F32, ts), _oacc((M, WIDTH), F32), _oacc((M, WIDTH), F32)])


def _head_norm(ob, g):
    xs, rs = [], []
    for h in range(GDN_HEADS):
        o = ob[:, h * GDN_DIM:(h + 1) * GDN_DIM]
        r = lax.rsqrt(jnp.mean(o * o, axis=1, keepdims=True) + EPS)
        xs.append(o * r)
        rs.append(r)
    return xs, rs


def _merge_fwd(x, z, o_a, o_b, o_m, gdn_g, b_merge, wb, wout, ts):
    S, D = x.shape

    def body(t, first, ins, outs, scratch):
        (x_ref, g_ref, oa_ref, az_ref, ob_ref, bz_ref, om_ref, mz_ref, gg_ref, bm_ref, wb_ref,
         wo_ref) = ins
        xo_ref, ya_ref, yb_ref, ym_ref, mg_ref = outs
        ya = oa_ref[...] * _silu_and_grad(az_ref[...])[0]
        xs, _ = _head_norm(ob_ref[...], None)
        nb = jnp.concatenate([xh * gg_ref[...] for xh in xs], axis=1)
        yb = nb * _silu_and_grad(bz_ref[...])[0]
        ym = om_ref[...] * _silu_and_grad(mz_ref[...])[0]
        merged = jnp.zeros((ts, D), F32)
        for n, (y, y_ref) in enumerate(((ya, ya_ref), (yb, yb_ref), (ym, ym_ref))):
            yb16 = _b16(y)
            y_ref[...] = yb16
            gate = _sigmoid(g_ref[:, n * D:(n + 1) * D] + bm_ref[:, n * D:(n + 1) * D])
            merged = merged + gate * _dot(yb16, wb_ref[n])
        mb = _b16(merged)
        mg_ref[...] = mb
        xo_ref[...] = x_ref[...] + _dot(mb, wo_ref[...])

    half = lambda a: _rows(a, ts)
    return _tiled("merge_fwd", body, S // ts,
                  [_rows(x, ts), _cols(z, ts, 3 * D, CB_GATES), half(o_a), _cols(z, ts, WIDTH, CB_AZ),
                   half(o_b), _cols(z, ts, WIDTH, CB_BZ), half(o_m), _cols(z, ts, WIDTH, CB_MZ),
                   _full(gdn_g.reshape(1, GDN_DIM)), _full(b_merge.reshape(1, 3 * D)), _full(wb), _full(wout)],
                  [_orow(S, (D,), F32, ts), _orow(S, (WIDTH,), BF16, ts), _orow(S, (WIDTH,), BF16, ts),
                   _orow(S, (WIDTH,), BF16, ts), _orow(S, (D,), BF16, ts)])


def _merge_bwd(dout, z, o_a, o_b, o_m, ya, yb, ym, gdn_g, b_merge, wb, wout, hsum, ts):
    S, D = dout.shape

    def body(t, first, ins, outs, scratch):
        (do_ref, g_ref, oa_ref, az_ref, ob_ref, bz_ref, om_ref, mz_ref, ya_ref, yb_ref, ym_ref,
         gg_ref, bm_ref, wb_ref, wo_ref, hs_ref) = ins
        (dg_ref, dpa_ref, dpb_ref, dpm_ref, doa_ref, dob_ref, dom_ref, daz_ref, dbz_ref, dmz_ref,
         dl_ref, dbm_ref, dgg_ref) = outs

        @pl.when(first)
        def _():
            dbm_ref[...] = jnp.zeros_like(dbm_ref)
            dgg_ref[...] = jnp.zeros_like(dgg_ref)

        dmerged = _dot(_b16(do_ref[...]), wo_ref[...], NT)
        dys = []
        for n, (y_ref, dp_ref) in enumerate(((ya_ref, dpa_ref), (yb_ref, dpb_ref), (ym_ref, dpm_ref))):
            sl = slice(n * D, (n + 1) * D)
            gate = _sigmoid(g_ref[:, sl] + bm_ref[:, sl])
            proj = _dot(y_ref[...], wb_ref[n])
            dproj = _b16(gate * dmerged)
            dp_ref[...] = dproj
            dgp = dmerged * proj * gate * (1.0 - gate)
            dg_ref[:, sl] = dgp.astype(dg_ref.dtype)
            dbm_ref[0:1, sl] += jnp.sum(dgp, axis=0, keepdims=True)
            dys.append(_dot(dproj, wb_ref[n], NT))
        dya, dyb, dym = dys
        sa, dsa = _silu_and_grad(az_ref[...])
        oa = oa_ref[...]
        doa = dya * sa
        doa_ref[...] = doa
        daz_ref[...] = dya * oa * dsa
        dl_ref[...] = _dot(doa * oa, hs_ref[...], NN, HIGHEST)
        sm, dsm = _silu_and_grad(mz_ref[...])
        dom_ref[...] = dym * sm
        dmz_ref[...] = dym * om_ref[...] * dsm
        sb, dsb = _silu_and_grad(bz_ref[...])
        xs, rs = _head_norm(ob_ref[...], None)
        gg = gg_ref[...]
        dgg = jnp.zeros((1, GDN_DIM), F32)
        for h in range(GDN_HEADS):
            cols = slice(h * GDN_DIM, (h + 1) * GDN_DIM)
            dn = dyb[:, cols] * sb[:, cols]
            dbz_ref[:, cols] = dyb[:, cols] * (xs[h] * gg) * dsb[:, cols]
            dgg = dgg + jnp.sum(dn * xs[h], axis=0, keepdims=True)
            dxh = dn * gg
            dob_ref[:, cols] = rs[h] * (dxh - xs[h] * jnp.mean(dxh * xs[h], axis=1, keepdims=True))
        dgg_ref[0:1, :] += dgg

    half = lambda a: _rows(a, ts)
    w512 = lambda dt: _orow(S, (WIDTH,), dt, ts)
    return _tiled("merge_bwd", body, S // ts,
                  [_rows(dout, ts), _cols(z, ts, 3 * D, CB_GATES), half(o_a), _cols(z, ts, WIDTH, CB_AZ),
                   half(o_b), _cols(z, ts, WIDTH, CB_BZ), half(o_m), _cols(z, ts, WIDTH, CB_MZ),
                   half(ya), half(yb), half(ym), _full(gdn_g.reshape(1, GDN_DIM)),
                   _full(b_merge.reshape(1, 3 * D)), _full(wb), _full(wout), _full(hsum)],
                  [_orow(S, (3 * D,), BF16, ts), _orow(S, (D,), BF16, ts), _orow(S, (D,), BF16, ts),
                   _orow(S, (D,), BF16, ts), w512(F32), w512(F32), w512(F32), w512(F32), w512(F32),
                   w512(F32), _orow(S, (128,), F32, ts), _oacc((8, 3 * D), F32), _oacc((8, GDN_DIM), F32)])


def _to_aligned(w):
    sizes = (512, 512, 512, 8, 512, 512, 512, 512, 4, 4, 512, 512, 512, 3072)
    names = ("aq", "ak", "av", "af", "az", "bq", "bk", "bv", "ba", "bb", "bz", "mq", "mz", "gates")
    p, off = {}, 0
    for n, s in zip(names, sizes):
        p[n] = w[..., off:off + s]
        off += s
    pad = jnp.zeros(w.shape[:-1] + (128 - 16,), w.dtype)
    return jnp.concatenate([p["gates"], p["bq"], p["bk"], p["bv"], p["aq"], p["ak"], p["av"], p["az"],
                            p["bz"], p["mq"], p["mz"], p["af"], p["ba"], p["bb"], pad], axis=-1)


def _from_aligned(w):
    c = lambda lo, n: w[..., lo:lo + n]
    gates, bq, bk, bv = c(0, 3072), c(3072, 512), c(3584, 512), c(4096, 512)
    aq, ak, av, az = c(4608, 512), c(5120, 512), c(5632, 512), c(6144, 512)
    bz, mq, mz = c(6656, 512), c(7168, 512), c(7680, 512)
    af, ba, bb = c(8192, 8), c(8200, 4), c(8204, 4)
    return jnp.concatenate([aq, ak, av, af, az, bq, bk, bv, ba, bb, bz, mq, mz, gates], axis=-1)


def _lanes128(v, lane0):
    return jnp.pad(v.astype(F32)[None, :], ((0, 0), (lane0, 128 - lane0 - v.shape[0])))


def _tiles(S):
    ts = min(512, S // 2)
    return dict(ts=ts, ts_small=min(256, S // 2), tq=min(512, S // 4))


def _layer_fwd(x, mem, p):
    S = x.shape[0]
    tl = _tiles(S)
    ts, tss, tq = tl["ts"], tl["ts_small"], tl["tq"]
    h, rstd = _rms_fwd("norm_fwd", x, p["norm_g"], ts)
    z = _mm("in_proj", h, p["w_in_al"], tn=1664)

    b_fg128 = _lanes128(p["b_fg"], LANE_AF)
    f128, f_hi, f_mid, f_lo = _fox_decay(z, b_fg128, ts)
    F = f128[:, :FOX_HEADS]
    aq = z[:, CB_AQ * WIDTH:(CB_AQ + 1) * WIDTH]
    ak = z[:, CB_AK * WIDTH:(CB_AK + 1) * WIDTH]
    av = z[:, CB_AV * WIDTH:(CB_AV + 1) * WIDTH]
    q32 = _heads_major(aq, FOX_HEADS, FOX_DIM)
    qh = q32.astype(BF16)
    kh = _heads_major(ak, FOX_HEADS, FOX_DIM).astype(BF16)
    vh = _heads_major(av, FOX_HEADS, FOX_DIM).astype(BF16)
    khT = kh.transpose(0, 2, 1)
    FT = F.T
    f_col, f_row = FT[:, :, None], FT[:, None, :]
    pieces = jnp.stack([f[:, :FOX_HEADS].T for f in (f_hi, f_mid, f_lo)], axis=-1)
    ones3 = jnp.ones((FOX_HEADS, S, 3), BF16)
    padk = jnp.zeros((FOX_HEADS, S, FOX_AUG - FOX_DIM - 6), BF16)
    q_aug = jnp.concatenate([q32, pieces.astype(F32), ones3.astype(F32), padk.astype(F32)], axis=-1)
    kT_aug = jnp.concatenate([kh, ones3, -pieces, padk], axis=-1).transpose(0, 2, 1)
    v_aug = jnp.concatenate([vh, ones3[:, :, :1], jnp.zeros((FOX_HEADS, S, 128 - FOX_DIM - 1), BF16)],
                            axis=-1)
    o_h, lse = _fox_fwd(q_aug, kT_aug, v_aug, tq)
    o_a = _heads_minor(o_h)

    a128 = _lanes128(p["a_log"], LANE_BA)
    dt128 = _lanes128(p["dt_bias"], LANE_BA)
    qkv, cpre, gb = _gdn_prep(z, p["conv_w"], a128, dt128, ts)
    grow = gb[:, LANE_BA:LANE_BA + GDN_HEADS].T.reshape(GDN_HEADS, S // CHUNK, CHUNK)
    u, w, qd, kd, aqk, T = _gdn_local_fwd(qkv, gb, grow, ts)
    o_b, vn, states = _gdn_scan_fwd(u, w, qd, kd, aqk, gb, ts)

    mem_h, mem_r = _rms_fwd("mem_norm_fwd", mem, p["mem_norm_g"], mem.shape[0])
    mkv = _mm("mem_kv", mem_h, p["w_mem_kv"])
    mk, mv = mkv[:, :WIDTH], mkv[:, WIDTH:]
    o_m = _mem_attn_fwd(z, mk, mv, ts)

    x_next, ya, yb, ym, merged = _merge_fwd(x, z, o_a, o_b, o_m, p["gdn_norm_g"], p["b_merge"],
                                            p["w_branch"], p["w_out"], tss)
    saved = dict(x=x, h=h, rstd=rstd, z=z, b_fg128=b_fg128, qh=qh, kh=kh, khT=khT, vh=vh, f_col=f_col,
                 f_row=f_row, lse=lse, o_a=o_a, a128=a128, dt128=dt128, qkv=qkv, cpre=cpre, gb=gb,
                 grow=grow, w=w, qd=qd, kd=kd, aqk=aqk, T=T, o_b=o_b, vn=vn, states=states,
                 mem_h=mem_h, mem_r=mem_r, mk=mk, mv=mv, o_m=o_m, ya=ya, yb=yb, ym=ym, merged=merged)
    return x_next, saved


def _layer_bwd(dout, mem, p, s):
    S = dout.shape[0]
    tl = _tiles(S)
    ts, tss, tq = tl["ts"], tl["ts_small"], tl["tq"]
    z = s["z"]
    hsum = (jnp.arange(WIDTH)[:, None] // FOX_DIM == jnp.arange(128)[None, :]).astype(F32)
    (dgates, dpa, dpb, dpm, do_a, do_b, do_m, daz, dbz, dmz, delta128, db_merge, dgdn_g) = _merge_bwd(
        dout, z, s["o_a"], s["o_b"], s["o_m"], s["ya"], s["yb"], s["ym"], p["gdn_norm_g"],
        p["b_merge"], p["w_branch"], p["w_out"], hsum, tss)
    g = {}
    g["b_merge"] = db_merge[0]
    g["gdn_norm_g"] = dgdn_g[0]
    g["w_out"] = _mm("dw_out", s["merged"], dout, ta=True)
    g["w_branch"] = jnp.stack([_mm("dw_branch", y, dp, ta=True)
                               for y, dp in ((s["ya"], dpa), (s["yb"], dpb), (s["ym"], dpm))])

    do_h = _heads_major(do_a, FOX_HEADS, FOX_DIM).astype(BF16)
    delta_row = delta128[:, :FOX_HEADS].T[:, None, :]
    lse_row = s["lse"].reshape(FOX_HEADS, 1, S)
    dqT, dk_h, dv_h, dfk, dfq = _fox_bwd(s["qh"], s["kh"], s["khT"], s["vh"], do_h, s["f_row"], s["f_col"],
                                    lse_row, delta_row, tq)
    daq = _heads_minor(dqT.transpose(0, 2, 1))
    dak = _heads_minor(dk_h)
    dav = _heads_minor(dv_h)
    lane_pad = ((0, 0), (0, 128 - FOX_HEADS))
    daf128, db_fg = _fox_decay_bwd(jnp.pad(dfk[:, :, 0].T, lane_pad), jnp.pad(dfq[:, 0, :].T, lane_pad),
                                   z, s["b_fg128"], ts)
    g["b_fg"] = db_fg[:FOX_HEADS]

    du, dw, dqd, dkd, daqk, dgl = _gdn_scan_bwd(do_b, s["w"], s["qd"], s["kd"], s["aqk"], s["vn"],
                                                s["states"], s["gb"], ts)
    dqkv, dgb = _gdn_local_bwd(s["qkv"], s["gb"], s["grow"], s["T"], du, dw, dqd, dkd, daqk, dgl, ts)
    dbqkv, dzs_b, dconv, dad = _gdn_prep_bwd(dqkv, dgb, s["cpre"], z, p["conv_w"], s["a128"],
                                             s["dt128"], ts)
    g["conv_w"] = dconv[:4]
    g["a_log"] = dad[0, LANE_BA:LANE_BA + GDN_HEADS]
    g["dt_bias"] = dad[1, LANE_BA:LANE_BA + GDN_HEADS]

    dmq, dmk, dmv = _mem_attn_bwd(do_m, z, s["mk"], s["mv"], ts)
    dmkv = jnp.concatenate([dmk, dmv], axis=1)
    g["w_mem_kv"] = _mm("dw_mem_kv", s["mem_h"], dmkv, ta=True)
    dmem_h = _mm("dmem_h", dmkv, p["w_mem_kv"], tb=True)
    M = mem.shape[0]
    _, g["mem_norm_g"] = _rms_bwd("mem_norm_bwd", dmem_h, mem, s["mem_r"], p["mem_norm_g"],
                                  jnp.zeros_like(mem), M)

    lane = jnp.arange(128)[None, :]
    dsmall = jnp.where(lane < 8, daf128, dzs_b)
    dz = jnp.concatenate([dgates, _b16(dbqkv), _b16(daq), _b16(dak), _b16(dav), _b16(daz), _b16(dbz),
                          _b16(dmq), _b16(dmz), _b16(dsmall)], axis=1)
    g["w_in_al"] = _mm("dw_in", s["h"], dz, ta=True, tn=1664)
    dh = _mm("dh", dz, p["w_in_al"], tb=True, tk=1664)
    dx, g["norm_g"] = _rms_bwd("norm_bwd", dh, s["x"], s["rstd"], p["norm_g"], dout, ts)
    return dx, g


def _local_step(x, mem, layers, final_norm_g, loss_target):
    S = x.shape[0]
    saves = []
    cur = x
    for p in layers:
        cur, sv = _layer_fwd(cur, mem, p)
        saves.append(sv)
    dx, dgf, loss_lanes = _loss_head(cur, final_norm_g, loss_target, _tiles(S)["ts"])
    grads = [None] * len(layers)
    for l in reversed(range(len(layers))):
        dx, grads[l] = _layer_bwd(dx, mem, layers[l], saves[l])
    return loss_lanes, dx, grads, dgf


HBM_SPEC = pl.BlockSpec(memory_space=pltpu.HBM)


def _mesh_pos():
    return lax.axis_index("x"), lax.axis_index("y"), lax.axis_index("c")


def _comm_call(name, body, arrays, out_shapes, n_remote, n_local):
    n = len(arrays)

    def kern(*refs):
        body(refs[:n], refs[n:2 * n], refs[2 * n], refs[2 * n + 1], refs[2 * n + 2])

    return pl.pallas_call(
        kern, name=name, out_shape=out_shapes, in_specs=[HBM_SPEC] * n, out_specs=[HBM_SPEC] * n,
        scratch_shapes=[pltpu.SemaphoreType.DMA((n_remote,)), pltpu.SemaphoreType.DMA((n_remote,)),
                        pltpu.SemaphoreType.DMA((max(n_local, 1),))],
    )(*arrays)


def _remote(src, dst, send_sems, recv_sems, k, to):
    return pltpu.make_async_remote_copy(src_ref=src, dst_ref=dst, send_sem=send_sems.at[k],
                                        recv_sem=recv_sems.at[k], device_id=to, device_id_type=MESH_ID)


def _other_chips(mx, my):
    return [(1 - mx, my), (mx, 1 - my), (1 - mx, 1 - my)]


def _gather_chips(name, shards):
    n = len(shards)

    def body(ins, outs, send_sems, recv_sems, local_sems):
        mx, my, mc = _mesh_pos()
        me = 2 * mx + my
        chips = _other_chips(mx, my)
        locals_, sends = [], []
        for a in range(n):
            lc = pltpu.make_async_copy(ins[a], outs[a].at[me], local_sems.at[a])
            lc.start()
            locals_.append(lc)
            for k, (px, py) in enumerate(chips):
                cp = _remote(ins[a], outs[a].at[me], send_sems, recv_sems, 3 * a + k, (px, py, mc))
                cp.start()
                sends.append(cp)
        for a in range(n):
            for k, (px, py) in enumerate(chips):
                _remote(ins[a], outs[a].at[2 * px + py], send_sems, recv_sems, 3 * a + k,
                        (px, py, mc)).wait_recv()
        for cp in sends:
            cp.wait_send()
        for lc in locals_:
            lc.wait()

    shapes = [jax.ShapeDtypeStruct((N_CHIPS,) + s.shape, s.dtype) for s in shards]
    return _comm_call(name, body, shards, shapes, 3 * n, n)


def _sibling_swap(gs):
    n = len(gs)

    def body(ins, outs, send_sems, recv_sems, local_sems):
        mx, my, mc = _mesh_pos()
        sends = []
        for a in range(n):
            cp = _remote(ins[a].at[:, 1 - mc], outs[a], send_sems, recv_sems, a, (mx, my, 1 - mc))
            cp.start()
            sends.append(cp)
        for cp in sends:
            cp.wait()

    shapes = [jax.ShapeDtypeStruct((g.shape[0],) + g.shape[2:], g.dtype) for g in gs]
    return _comm_call("grad_sibling_swap", body, gs, shapes, n, 0)


def _chip_exchange(ps):
    n = len(ps)

    def body(ins, outs, send_sems, recv_sems, local_sems):
        mx, my, mc = _mesh_pos()
        me = 2 * mx + my
        chips = _other_chips(mx, my)
        locals_, sends = [], []
        for a in range(n):
            lc = pltpu.make_async_copy(ins[a].at[me], outs[a].at[me], local_sems.at[a])
            lc.start()
            locals_.append(lc)
            for k, (px, py) in enumerate(chips):
                cp = _remote(ins[a].at[2 * px + py], outs[a].at[me], send_sems, recv_sems, 3 * a + k,
                             (px, py, mc))
                cp.start()
                sends.append(cp)
        for a in range(n):
            for k, (px, py) in enumerate(chips):
                _remote(ins[a].at[me], outs[a].at[2 * px + py], send_sems, recv_sems, 3 * a + k,
                        (px, py, mc)).wait_recv()
        for cp in sends:
            cp.wait_send()
        for lc in locals_:
            lc.wait()

    shapes = [jax.ShapeDtypeStruct(p.shape, p.dtype) for p in ps]
    return _comm_call("grad_chip_exchange", body, ps, shapes, 3 * n, n)


def _sibling_gather(hs):
    n = len(hs)

    def body(ins, outs, send_sems, recv_sems, local_sems):
        mx, my, mc = _mesh_pos()
        locals_, sends = [], []
        for a in range(n):
            lc = pltpu.make_async_copy(ins[a], outs[a].at[mc], local_sems.at[a])
            lc.start()
            locals_.append(lc)
            cp = _remote(ins[a], outs[a].at[mc], send_sems, recv_sems, a, (mx, my, 1 - mc))
            cp.start()
            sends.append(cp)
        for a in range(n):
            _remote(ins[a], outs[a].at[1 - mc], send_sems, recv_sems, a, (mx, my, 1 - mc)).wait_recv()
        for cp in sends:
            cp.wait_send()
        for lc in locals_:
            lc.wait()

    shapes = [jax.ShapeDtypeStruct((2,) + h.shape, h.dtype) for h in hs]
    return _comm_call("grad_sibling_gather", body, hs, shapes, n, n)


def _add_pairs(a, b, tr):
    n, H, C = a.shape

    def kern(a_ref, b_ref, o_ref):
        o_ref[...] = a_ref[...] + b_ref[...]

    spec = pl.BlockSpec((None, tr, C), lambda j, i: (j, i, 0))
    return pl.pallas_call(
        kern, name="grad_pair_sum", grid=(n, H // tr), in_specs=[spec, spec], out_specs=spec,
        out_shape=jax.ShapeDtypeStruct((n, H, C), a.dtype),
        compiler_params=_params(("parallel", "parallel")),
    )(a, b)


def _sum_slots(r4, tr):
    n, H, C = r4.shape

    def kern(r_ref, o_ref):
        o_ref[...] = ((r_ref[0] + r_ref[1]) + r_ref[2]) + r_ref[3]

    return pl.pallas_call(
        kern, name="grad_chip_sum", grid=(H // tr,),
        in_specs=[pl.BlockSpec((n, tr, C), lambda i: (0, i, 0))],
        out_specs=pl.BlockSpec((tr, C), lambda i: (i, 0)),
        out_shape=jax.ShapeDtypeStruct((H, C), r4.dtype),
        compiler_params=_params(("parallel",)),
    )(r4)


def _adamw(w, g, m, v, tr):
    R, C = w.shape
    c1 = 1.0 - ADAM_B1
    c2 = 1.0 - ADAM_B2
    bc1 = 1.0 - ADAM_B1 ** ADAM_STEP
    bc2 = 1.0 - ADAM_B2 ** ADAM_STEP

    def kern(w_ref, g_ref, m_ref, v_ref, d_ref, mo_ref, vo_ref):
        gv = g_ref[...]
        mn = ADAM_B1 * m_ref[...] + c1 * gv
        vn = ADAM_B2 * v_ref[...] + c2 * (gv * gv)
        m_hat = mn / bc1
        v_hat = vn / bc2
        d_ref[...] = -ADAM_LR * (m_hat / (jnp.sqrt(v_hat) + ADAM_EPS) + ADAM_WD * w_ref[...])
        mo_ref[...] = mn
        vo_ref[...] = vn

    spec = pl.BlockSpec((tr, C), lambda i: (i, 0))
    shape = jax.ShapeDtypeStruct((R, C), F32)
    return pl.pallas_call(
        kern, name="adamw", grid=(R // tr,), in_specs=[spec] * 4, out_specs=[spec] * 3,
        out_shape=[shape] * 3, compiler_params=_params(("parallel",)),
    )(w, g, m, v)


PACK_COLS = 1024
PACK_ROWS = 512
W_SHARD = N_IN // N_CHIPS
SLAB = ("conv_w", "w_mem_kv", "w_branch", "w_out")
SMALL =("norm_g", "b_fg", "b_merge", "a_log", "dt_bias", "gdn_norm_g", "mem_norm_g", "final_norm_g")
ALL_WEIGHTS = ("norm_g", "w_in", "b_fg", "b_merge", "conv_w", "a_log", "dt_bias", "gdn_norm_g",
               "mem_norm_g", "w_mem_kv", "w_branch", "w_out", "final_norm_g")
SHARD_AXIS = {"w_in": 2, "conv_w": 2, "w_mem_kv": 1, "w_branch": 3, "w_out": 1}


def _pack(arrays, row_multiple):
    flat = jnp.concatenate([a.reshape(-1) for a in arrays])
    n = flat.shape[0]
    rows = -(-n // PACK_COLS)
    rows = -(-rows // row_multiple) * row_multiple
    flat = jnp.pad(flat, (0, rows * PACK_COLS - n))
    return flat.reshape(rows, PACK_COLS)


def _unpack(slab, shapes):
    flat = slab.reshape(-1)
    out, off = [], 0
    for shp in shapes:
        n = 1
        for d in shp:
            n *= d
        out.append(flat[off:off + n].reshape(shp))
        off += n
    return out


def _shard_of(full, name, j):
    ax = SHARD_AXIS[name]
    n = full.shape[ax] // N_CHIPS
    return lax.slice_in_dim(full, j * n, (j + 1) * n, axis=ax)


W_IN_PIECES = ((0, 512, 4608), (512, 512, 5120), (1024, 512, 5632), (1536, 8, 8192), (1544, 512, 6144),
               (2056, 512, 3072), (2568, 512, 3584), (3080, 512, 4096), (3592, 4, 8200), (3596, 4, 8204),
               (3600, 512, 6656), (4112, 512, 7168), (4624, 512, 7680), (5136, 3072, 0))


def _aligned_from_shards(shards):
    def cols(lo, n):
        parts = []
        while n > 0:
            j, off = divmod(lo, W_SHARD)
            take = min(n, W_SHARD - off)
            parts.append(shards[j][..., off:off + take])
            lo, n = lo + take, n - take
        return parts

    out = []
    for lo, n, _ in sorted(W_IN_PIECES, key=lambda p: p[2]):
        out += cols(lo, n)
    out.append(jnp.zeros(shards[0].shape[:-1] + (N_AL - N_IN,), shards[0].dtype))
    return jnp.concatenate(out, axis=-1)


def _shard_from_aligned(w_al, j):
    lo_j, hi_j = j * W_SHARD, (j + 1) * W_SHARD
    parts = []
    for lo, n, al in W_IN_PIECES:
        a, b = max(lo, lo_j), min(lo + n, hi_j)
        if a < b:
            parts.append(w_al[..., al + a - lo:al + b - lo])
    return jnp.concatenate(parts, axis=-1)


def kernel(x, mem, norm_g, w_in, b_fg, b_merge, conv_w, a_log, dt_bias, gdn_norm_g, mem_norm_g, w_mem_kv, w_branch, w_out, final_norm_g, loss_target, m_norm_g, m_w_in, m_b_fg, m_b_merge, m_conv_w, m_a_log, m_dt_bias, m_gdn_norm_g, m_mem_norm_g, m_w_mem_kv, m_w_branch, m_w_out, m_final_norm_g, v_norm_g, v_w_in, v_b_fg, v_b_merge, v_conv_w, v_a_log, v_dt_bias, v_gdn_norm_g, v_mem_norm_g, v_w_mem_kv, v_w_branch, v_w_out, v_final_norm_g):
    wts = dict(norm_g=norm_g, w_in=w_in, b_fg=b_fg, b_merge=b_merge, conv_w=conv_w, a_log=a_log,
               dt_bias=dt_bias, gdn_norm_g=gdn_norm_g, mem_norm_g=mem_norm_g, w_mem_kv=w_mem_kv,
               w_branch=w_branch, w_out=w_out, final_norm_g=final_norm_g)
    mom = dict(norm_g=m_norm_g, w_in=m_w_in, b_fg=m_b_fg, b_merge=m_b_merge, conv_w=m_conv_w,
               a_log=m_a_log, dt_bias=m_dt_bias, gdn_norm_g=m_gdn_norm_g, mem_norm_g=m_mem_norm_g,
               w_mem_kv=m_w_mem_kv, w_branch=m_w_branch, w_out=m_w_out, final_norm_g=m_final_norm_g)
    vel = dict(norm_g=v_norm_g, w_in=v_w_in, b_fg=v_b_fg, b_merge=v_b_merge, conv_w=v_conv_w,
               a_log=v_a_log, dt_bias=v_dt_bias, gdn_norm_g=v_gdn_norm_g, mem_norm_g=v_mem_norm_g,
               w_mem_kv=v_w_mem_kv, w_branch=v_w_branch, w_out=v_w_out, final_norm_g=v_final_norm_g)

    big = ("w_in", "w_mem_kv", "w_branch", "w_out")
    gathered = _gather_chips("weight_gather", [wts[n].astype(BF16) for n in big] + [conv_w])
    all_w = dict(zip(big + ("conv_w",), gathered))
    w_in_al = _aligned_from_shards([all_w["w_in"][j] for j in range(N_CHIPS)])

    layers = []
    for l in range(DEPTH):
        rows_of = lambda n: all_w[n][:, l].reshape(D_MODEL, D_MODEL)
        last_of = lambda n: jnp.concatenate([all_w[n][j, l] for j in range(N_CHIPS)], axis=-1)
        layers.append(dict(norm_g=norm_g[l], w_in_al=w_in_al[l], b_fg=b_fg[l], b_merge=b_merge[l],
                           conv_w=jnp.pad(last_of("conv_w"), ((0, 4), (0, 0))), a_log=a_log[l],
                           dt_bias=dt_bias[l], gdn_norm_g=gdn_norm_g[l], mem_norm_g=mem_norm_g[l],
                           w_mem_kv=rows_of("w_mem_kv"), w_branch=last_of("w_branch"),
                           w_out=rows_of("w_out")))

    loss_lanes, dx, grads, dgf = _local_step(x[0], mem[0], layers, final_norm_g, loss_target[0])

    gfull = {n: jnp.stack([grads[l][n] for l in range(DEPTH)])
             for n in ("norm_g", "b_fg", "b_merge", "conv_w", "a_log", "dt_bias", "gdn_norm_g",
                       "mem_norm_g", "w_mem_kv", "w_branch", "w_out")}
    gfull["final_norm_g"] = dgf
    loss_local = jnp.sum(loss_lanes).reshape(1)
    small_g = [gfull[n] for n in SMALL] + [loss_local]
    dw_al = jnp.stack([grads[l]["w_in_al"] for l in range(DEPTH)])
    ga = jnp.stack([_shard_from_aligned(dw_al, j) for j in range(N_CHIPS)])
    gb = jnp.stack([_pack([_shard_of(gfull[n], n, j) for n in SLAB] + small_g, PACK_ROWS)
                    for j in range(N_CHIPS)])
    R = gb.shape[1]
    gb = gb.reshape(N_CHIPS, 2, R // 2, PACK_COLS)

    mc = lax.axis_index("c")
    tr = 256
    from_sibling = _sibling_swap([ga, gb])
    mine = [lax.dynamic_index_in_dim(g, mc, axis=1, keepdims=False) for g in (ga, gb)]
    pair = [_add_pairs(a, b, tr) for a, b in zip(mine, from_sibling)]
    slots = _chip_exchange(pair)
    half = [_sum_slots(s, tr) for s in slots]
    ga_sum, gb_sum = _sibling_gather(half)
    gb_sum = gb_sum.reshape(R, PACK_COLS)

    zero1 = jnp.zeros((1,), F32)
    slab = lambda d: _pack([d[n] for n in SLAB] + [d[n] for n in SMALL] + [zero1], PACK_ROWS)
    delta_s, m_s, v_s = _adamw(slab(wts), gb_sum, slab(mom), slab(vel), tr)
    flat_in = lambda a: a.reshape(DEPTH * D_MODEL, W_SHARD)
    in_res = _adamw(flat_in(w_in), flat_in(ga_sum), flat_in(m_w_in), flat_in(v_w_in), tr)

    names = list(SLAB) + list(SMALL)
    shapes = [wts[n].shape for n in names] + [(1,)]
    g_un = dict(zip(names + ["loss"], _unpack(gb_sum, shapes)))
    d_un = dict(zip(names, _unpack(delta_s, shapes[:-1])))
    m_un = dict(zip(names, _unpack(m_s, shapes[:-1])))
    v_un = dict(zip(names, _unpack(v_s, shapes[:-1])))
    g_un["w_in"] = ga_sum
    d_un["w_in"], m_un["w_in"], v_un["w_in"] = [r.reshape(w_in.shape) for r in in_res]

    loss = g_un["loss"][0]
    return (loss, dx[None], *[g_un[n] for n in ALL_WEIGHTS], *[d_un[n] for n in ALL_WEIGHTS],
            *[m_un[n] for n in ALL_WEIGHTS], *[v_un[n] for n in ALL_WEIGHTS])
```

```python
import functools

import jax
import jax.numpy as jnp
from jax import lax
from jax.experimental import pallas as pl
from jax.experimental.pallas import tpu as pltpu

F32 = jnp.float32
BF16 = jnp.bfloat16
HIGHEST = lax.Precision.HIGHEST
MESH_ID = pl.DeviceIdType.MESH

D_MODEL = 1024
DEPTH = 2
CHUNK = 64
EPS = 1e-6
FOX_HEADS, FOX_DIM = 8, 64
GDN_HEADS, GDN_DIM = 4, 128
MEM_HEADS, MEM_DIM = 4, 128
WIDTH = 512
N_BRANCH = 3
N_IN = 8208
N_AL = 8320
N_CHIPS = 4
NEG = -1e30
LOG2E = 1.4426950408889634
LN2 = 0.6931471805599453

ADAM_LR, ADAM_B1, ADAM_B2, ADAM_EPS, ADAM_WD, ADAM_STEP = 0.001, 0.9, 0.999, 1e-08, 0.01, 10

CB_GATES = 0
CB_BQKV = 2
CB_AQ, CB_AK, CB_AV, CB_AZ, CB_BZ, CB_MQ, CB_MZ = 9, 10, 11, 12, 13, 14, 15
CB_SMALL = 64
LANE_AF, LANE_BA, LANE_BB = 0, 8, 12

NN = ((1,), (0,))
NT = ((1,), (1,))
TN = ((0,), (0,))

VMEM_LIMIT_BYTES = 56 * 1024 * 1024


def _dot(a, b, dims=NN, prec=None):
    return lax.dot_general(a, b, (dims, ((), ())), preferred_element_type=F32, precision=prec)


def _bdot(a, b, ca, cb, prec=None):
    return lax.dot_general(a, b, (((ca,), (cb,)), ((0,), (0,))), preferred_element_type=F32,
                           precision=prec)


def _b16(a):
    return a.astype(BF16)


def _eye(n, dtype=F32):
    r = lax.broadcasted_iota(jnp.int32, (n, n), 0)
    c = lax.broadcasted_iota(jnp.int32, (n, n), 1)
    return jnp.where(r == c, 1.0, 0.0).astype(dtype)


def _transpose_exact(x):
    return _dot(_eye(x.shape[1]), x, NT, HIGHEST)


def _col_to_row(col):
    n = col.shape[0]
    return jnp.sum(jnp.where(_eye(n) > 0.5, col, 0.0), axis=0, keepdims=True)


def _row_to_col(row):
    n = row.shape[1]
    return jnp.sum(jnp.where(_eye(n) > 0.5, row, 0.0), axis=1, keepdims=True)


def _sigmoid(x):
    return 1.0 / (1.0 + jnp.exp(-x))


def _softplus(x):
    return jnp.maximum(x, 0.0) + jnp.log(1.0 + jnp.exp(-jnp.abs(x)))


def _silu_and_grad(x):
    s = _sigmoid(x)
    return x * s, s * (1.0 + x * (1.0 - s))


def _params(semantics):
    return pltpu.CompilerParams(dimension_semantics=semantics, vmem_limit_bytes=VMEM_LIMIT_BYTES)


def _rows(a, ts):
    nd = a.ndim
    return (a, (ts,) + a.shape[1:], lambda i, nd=nd: (i,) + (0,) * (nd - 1))


def _cols(a, ts, width, cb):
    return (a, (ts, width), lambda i, cb=cb: (i, cb))


def _full(a):
    nd = a.ndim
    return (a, a.shape, lambda i, nd=nd: (0,) * nd)


def _orow(S, tail, dtype, ts):
    nd = 1 + len(tail)
    return ((S,) + tuple(tail), dtype, (ts,) + tuple(tail), lambda i, nd=nd: (i,) + (0,) * (nd - 1))


def _oacc(shape, dtype):
    nd = len(shape)
    return (tuple(shape), dtype, tuple(shape), lambda i, nd=nd: (0,) * nd)


def _tiled(name, body, n_steps, ins, outs, scratch=(), reverse=False):
    def rev(imap):
        if not reverse:
            return imap
        return lambda i: imap(n_steps - 1 - i)

    in_specs = [pl.BlockSpec(blk, rev(imap)) for (_, blk, imap) in ins]
    out_specs = [pl.BlockSpec(blk, rev(imap)) for (_, _, blk, imap) in outs]
    out_shape = [jax.ShapeDtypeStruct(shape, dt) for (shape, dt, _, _) in outs]
    n_in, n_out = len(ins), len(outs)

    def kern(*refs):
        step = pl.program_id(0)
        t = (n_steps - 1 - step) if reverse else step
        body(t, step == 0, refs[:n_in], refs[n_in:n_in + n_out], refs[n_in + n_out:])

    res = pl.pallas_call(
        kern, name=name, grid=(n_steps,), in_specs=in_specs, out_specs=out_specs,
        out_shape=out_shape, scratch_shapes=list(scratch),
        compiler_params=_params(("arbitrary",)),
    )(*[a for (a, _, _) in ins])
    return res


def _pick(n, pref):
    if n <= pref:
        return n
    best = None
    for t in range(128, pref + 1, 128):
        if n % t == 0:
            best = t
    assert best is not None, (n, pref)
    return best


def _mm(name, a, b, ta=False, tb=False, out_dtype=F32, tm=1024, tn=1024, tk=1024):
    if ta:
        K, M = a.shape
    else:
        M, K = a.shape
    if tb:
        N, K2 = b.shape
    else:
        K2, N = b.shape
    assert K == K2, (a.shape, b.shape, ta, tb)
    tm, tn, tk = _pick(M, tm), _pick(N, tn), _pick(K, tk)
    nk = K // tk
    a_spec = (pl.BlockSpec((tk, tm), lambda i, j, k: (k, i)) if ta
              else pl.BlockSpec((tm, tk), lambda i, j, k: (i, k)))
    b_spec = (pl.BlockSpec((tn, tk), lambda i, j, k: (j, k)) if tb
              else pl.BlockSpec((tk, tn), lambda i, j, k: (k, j)))
    dims = ((0,) if ta else (1,), (1,) if tb else (0,))

    def kern_single(a_ref, b_ref, o_ref):
        o_ref[...] = _dot(_b16(a_ref[...]), _b16(b_ref[...]), dims).astype(o_ref.dtype)

    def kern_acc(a_ref, b_ref, o_ref, acc_ref):
        k = pl.program_id(2)

        @pl.when(k == 0)
        def _():
            acc_ref[...] = jnp.zeros_like(acc_ref)

        acc_ref[...] += _dot(_b16(a_ref[...]), _b16(b_ref[...]), dims)

        @pl.when(k == nk - 1)
        def _():
            o_ref[...] = acc_ref[...].astype(o_ref.dtype)

    return pl.pallas_call(
        kern_single if nk == 1 else kern_acc, name=name, grid=(M // tm, N // tn, nk),
        in_specs=[a_spec, b_spec],
        out_specs=pl.BlockSpec((tm, tn), lambda i, j, k: (i, j)),
        out_shape=jax.ShapeDtypeStruct((M, N), out_dtype),
        scratch_shapes=[] if nk == 1 else [pltpu.VMEM((tm, tn), F32)],
        compiler_params=_params(("parallel", "parallel", "arbitrary")),
    )(a, b)


def _rms_fwd(name, x, g, ts):
    S, D = x.shape

    def body(t, first, ins, outs, scratch):
        x_ref, g_ref = ins
        h_ref, r_ref = outs
        xv = x_ref[...]
        r = lax.rsqrt(jnp.mean(xv * xv, axis=1, keepdims=True) + EPS)
        h_ref[...] = (xv * r * g_ref[...]).astype(h_ref.dtype)
        r_ref[...] = r

    return _tiled(name, body, S // ts, [_rows(x, ts), _full(g.reshape(1, D))],
                  [_orow(S, (D,), BF16, ts), _orow(S, (1,), F32, ts)])


def _rms_bwd(name, dh, x, rstd, g, dres, ts):
    S, D = x.shape

    def body(t, first, ins, outs, scratch):
        dh_ref, x_ref, r_ref, g_ref, dres_ref = ins
        dx_ref, dg_ref = outs
        r = r_ref[...]
        xh = x_ref[...] * r
        dhv = dh_ref[...]
        dxh = dhv * g_ref[...]
        dx_ref[...] = dres_ref[...] + r * (dxh - xh * jnp.mean(dxh * xh, axis=1, keepdims=True))

        @pl.when(first)
        def _():
            dg_ref[...] = jnp.zeros_like(dg_ref)

        dg_ref[0:1, :] += jnp.sum(dhv * xh, axis=0, keepdims=True)

    dx, dg = _tiled(name, body, S // ts,
                    [_rows(dh, ts), _rows(x, ts), _rows(rstd, ts), _full(g.reshape(1, D)), _rows(dres, ts)],
                    [_orow(S, (D,), F32, ts), _oacc((8, D), F32)])
    return dx, dg[0]


def _loss_head(x, g, target, ts):
    S, D = x.shape

    def body(t, first, ins, outs, scratch):
        x_ref, g_ref, tgt_ref = ins
        dx_ref, dg_ref, loss_ref = outs
        xv = x_ref[...]
        gv = g_ref[...]
        r = lax.rsqrt(jnp.mean(xv * xv, axis=1, keepdims=True) + EPS)
        xh = xv * r
        err = xh * gv - tgt_ref[...]
        dy = err * (1.0 / D)
        dxh = dy * gv
        dx_ref[...] = r * (dxh - xh * jnp.mean(dxh * xh, axis=1, keepdims=True))

        @pl.when(first)
        def _():
            dg_ref[...] = jnp.zeros_like(dg_ref)
            loss_ref[...] = jnp.zeros_like(loss_ref)

        dg_ref[0:1, :] += jnp.sum(dy * xh, axis=0, keepdims=True)
        per_lane = jnp.sum(err * err, axis=0, keepdims=True)
        loss_ref[0:1, :] += per_lane * (0.5 / D)

    dx, dg, loss = _tiled("loss_head", body, S // ts,
                          [_rows(x, ts), _full(g.reshape(1, D)), _rows(target, ts)],
                          [_orow(S, (D,), F32, ts), _oacc((8, D), F32), _oacc((8, D), F32)])
    return dx, dg[0], loss[0]


def _scan_rows(x, length, seg, reverse=False):
    row = lax.broadcasted_iota(jnp.int32, x.shape, 0) % seg
    k = 1
    while k < seg:
        if reverse:
            x = x + jnp.where(row < seg - k, pltpu.roll(x, length - k, 0), 0.0)
        else:
            x = x + jnp.where(row >= k, pltpu.roll(x, k, 0), 0.0)
        k *= 2
    return x


def _fox_decay(z, b_fg128, ts):
    S = z.shape[0]

    def body(t, first, ins, outs, scratch):
        zs_ref, b_ref = ins
        f_ref, hi_ref, mid_ref, lo_ref = outs
        (carry,) = scratch

        @pl.when(first)
        def _():
            carry[...] = jnp.zeros_like(carry)

        logf = -_softplus(-(zs_ref[...] + b_ref[...]))
        run = _scan_rows(logf, ts, ts) + carry[0:1, :]
        carry[0:1, :] = run[ts - 1:ts, :]
        f_ref[...] = _transpose_exact(run)
        f2 = run * LOG2E
        hi = f2.astype(BF16)
        r1 = f2 - hi.astype(F32)
        mid = r1.astype(BF16)
        lo = (r1 - mid.astype(F32)).astype(BF16)
        eye = _eye(128, BF16)
        hi_ref[...] = _dot(eye, hi, NT).astype(BF16)
        mid_ref[...] = _dot(eye, mid, NT).astype(BF16)
        lo_ref[...] = _dot(eye, lo, NT).astype(BF16)

    tcol = lambda dt: ((128, S), dt, (128, ts), lambda i: (0, i))
    return _tiled("fox_decay", body, S // ts,
                  [_cols(z, ts, 128, CB_SMALL), _full(b_fg128)],
                  [tcol(F32), tcol(BF16), tcol(BF16), tcol(BF16)], scratch=[pltpu.VMEM((8, 128), F32)])


def _fox_decay_bwd(dfk_rows, dfq_rows, z, b_fg128, ts):
    S = z.shape[0]
    H = dfk_rows.shape[0]

    def body(t, first, ins, outs, scratch):
        dfk_ref, dfq_ref, zs_ref, b_ref = ins
        daf_ref, db_ref = outs
        (carry,) = scratch

        @pl.when(first)
        def _():
            carry[...] = jnp.zeros_like(carry)
            db_ref[...] = jnp.zeros_like(db_ref)

        r = lax.broadcasted_iota(jnp.int32, (H, 128), 0)
        c = lax.broadcasted_iota(jnp.int32, (H, 128), 1)
        place = jnp.where(r == c, 1.0, 0.0)
        df = _dot(dfk_ref[...] + dfq_ref[...], place, TN, HIGHEST)
        run = _scan_rows(df, ts, ts, reverse=True) + carry[0:1, :]
        carry[0:1, :] = run[0:1, :]
        daf = run * _sigmoid(-(zs_ref[...] + b_ref[...]))
        daf_ref[...] = daf
        db_ref[0:1, :] += jnp.sum(daf, axis=0, keepdims=True)

    rowsin = lambda a: (a, (H, ts), lambda i: (0, i))
    daf, db = _tiled("fox_decay_bwd", body, S // ts,
                     [rowsin(dfk_rows), rowsin(dfq_rows), _cols(z, ts, 128, CB_SMALL), _full(b_fg128)],
                     [_orow(S, (128,), F32, ts), _oacc((8, 128), F32)],
                     scratch=[pltpu.VMEM((8, 128), F32)], reverse=True)
    return daf, db[0]


FOX_AUG = 80


def _fox_fwd(q_aug, kT_aug, v_aug, tq):
    H, S, da = q_aug.shape
    dv = v_aug.shape[2]
    d = FOX_DIM
    tk = tq // 2
    qscale = (d ** -0.5) * LOG2E

    def kern(q_ref, kT_ref, v_ref, o_ref, lse_ref, s_buf, p_buf, m_scr, acc_scr):
        i = pl.program_id(1)
        col = lax.broadcasted_iota(jnp.int32, (1, da), 1)
        qb = _b16(q_ref[...] * jnp.where(col < d, qscale, 1.0))

        def keys(t):
            return pl.ds(pl.multiple_of(t * tk, tk), tk)

        def stage(t, slot, mask_off, look_ahead):
            if look_ahead:
                s_buf[1 - slot] = _dot(qb, kT_ref[:, keys(t + 1)])
            pv = _dot(p_buf[1 - slot], v_ref[keys(jnp.maximum(t - 1, 0)), :])

            def scores():
                s = s_buf[slot]
                if mask_off is None:
                    return s
                r = lax.broadcasted_iota(jnp.int32, (tq, tk), 0)
                c = lax.broadcasted_iota(jnp.int32, (tq, tk), 1)
                return jnp.where(c + mask_off <= r, s, NEG)

            m = m_scr[...]
            m_new = jnp.maximum(m, jnp.max(scores(), axis=1, keepdims=True))
            alpha = jnp.exp2(m - m_new)
            p_buf[slot] = _b16(jnp.exp2(scores() - m_new))
            m_scr[...] = m_new
            acc_scr[...] = (acc_scr[...] + pv) * alpha

        s_buf[0] = _dot(qb, kT_ref[:, keys(0)])
        p_buf[1] = jnp.zeros((tq, tk), BF16)
        m_scr[...] = jnp.full((tq, 1), NEG, F32)
        acc_scr[...] = jnp.zeros((tq, dv), F32)

        def pair(n, _):
            stage(2 * n, 0, None, True)
            stage(2 * n + 1, 1, None, True)
            return 0

        lax.fori_loop(0, i, pair, 0)
        stage(2 * i, 0, 0, True)
        stage(2 * i + 1, 1, tk, False)
        acc = acc_scr[...] + _dot(p_buf[1], v_ref[keys(2 * i + 1), :])
        l = acc[:, d:d + 1]
        o_ref[...] = acc[:, :d] / l
        lse_ref[...] = _col_to_row(m_scr[...] * LN2 + jnp.log(l))

    return pl.pallas_call(
        kern, name="fox_fwd", grid=(H, S // tq),
        in_specs=[pl.BlockSpec((None, tq, da), lambda h, i: (h, i, 0)),
                  pl.BlockSpec((None, da, S), lambda h, i: (h, 0, 0)),
                  pl.BlockSpec((None, S, dv), lambda h, i: (h, 0, 0))],
        out_specs=[pl.BlockSpec((None, tq, d), lambda h, i: (h, i, 0)),
                   pl.BlockSpec((None, 1, tq), lambda h, i: (h, 0, i))],
        out_shape=[jax.ShapeDtypeStruct((H, S, d), F32), jax.ShapeDtypeStruct((H, 1, S), F32)],
        scratch_shapes=[pltpu.VMEM((2, tq, tk), F32), pltpu.VMEM((2, tq, tk), BF16),
                        pltpu.VMEM((tq, 1), F32), pltpu.VMEM((tq, dv), F32)],
        compiler_params=_params(("parallel", "arbitrary")),
    )(q_aug, kT_aug, v_aug)


def _fox_bwd(q, k, kT, v, do, f_row, lse_row, delta_row, tq):
    H, S, d = q.shape
    tk = tq
    nq = S // tq
    scale = d ** -0.5

    ts2 = tq // 2
    last = 2 * nq - 1

    def kern(q_ref, k_ref, kT_ref, v_ref, do_ref, fq_ref, fk_ref, lse_ref, dl_ref,
             dqT_ref, dk_ref, dv_ref, dfk_ref, dfq_ref,
             kq_buf, dp_buf, pb_buf, ds_buf, dk_scr, dv_scr, dfk_scr):
        j = pl.program_id(1)

        @pl.when(j == 0)
        def _():
            dqT_ref[...] = jnp.zeros_like(dqT_ref)
            dfq_ref[...] = jnp.zeros_like(dfq_ref)

        kb = k_ref[...]
        kTb = kT_ref[...]
        vb = v_ref[...]
        fk = _row_to_col(fk_ref[...])
        dk_scr[...] = jnp.zeros_like(dk_scr)
        dv_scr[...] = jnp.zeros_like(dv_scr)
        dfk_scr[...] = jnp.zeros_like(dfk_scr)

        def queries(t):
            return pl.ds(pl.multiple_of(t * ts2, ts2), ts2)

        def products(t, slot):
            rows = queries(t)
            kq_buf[slot] = _dot(kb, q_ref[rows, :] * scale, NT)
            dp_buf[slot] = _dot(vb, do_ref[rows, :], NT)

        def pointwise(t, slot, mask_off):
            rows = queries(t)
            sT = kq_buf[slot] + (fq_ref[:, rows] - fk)
            if mask_off is not None:
                r = lax.broadcasted_iota(jnp.int32, (tk, ts2), 0)
                c = lax.broadcasted_iota(jnp.int32, (tk, ts2), 1)
                sT = jnp.where(r <= c + mask_off, sT, NEG)
            pT = jnp.exp(sT - lse_ref[:, rows])
            dsT = pT * (dp_buf[slot] - dl_ref[:, rows])
            pb_buf[slot] = _b16(pT)
            ds_buf[slot] = _b16(dsT)
            dfk_scr[...] -= jnp.sum(dsT, axis=1, keepdims=True)
            dfq_ref[:, rows] += jnp.sum(dsT, axis=0, keepdims=True)

        def accumulate(t, slot):
            rows = queries(t)
            dsb = ds_buf[slot]
            dv_scr[...] += _dot(pb_buf[slot], do_ref[rows, :])
            dk_scr[...] += _dot(dsb, q_ref[rows, :] * scale)
            dqT_ref[:, rows] += _dot(kTb, dsb) * scale

        def stage(t, slot, mask_off, has_prev):
            products(jnp.minimum(t + 1, last), 1 - slot)
            if has_prev:
                accumulate(t - 1, 1 - slot)
            pointwise(t, slot, mask_off)

        products(2 * j, 0)
        stage(2 * j, 0, 0, False)
        stage(2 * j + 1, 1, ts2, True)

        def pair(n, _):
            stage(2 * n, 0, None, True)
            stage(2 * n + 1, 1, None, True)
            return 0

        lax.fori_loop(j + 1, nq, pair, 0)
        accumulate(last, 1)
        dk_ref[...] = dk_scr[...]
        dv_ref[...] = dv_scr[...]
        dfk_ref[...] = _col_to_row(dfk_scr[...])

    tile = lambda h, j: (h, j, 0)
    whole = lambda h, j: (h, 0, 0)
    rowtile = lambda h, j: (h, 0, j)
    return pl.pallas_call(
        kern, name="fox_bwd", grid=(H, S // tk),
        in_specs=[pl.BlockSpec((None, S, d), whole),
                  pl.BlockSpec((None, tk, d), tile),
                  pl.BlockSpec((None, d, tk), lambda h, j: (h, 0, j)),
                  pl.BlockSpec((None, tk, d), tile),
                  pl.BlockSpec((None, S, d), whole),
                  pl.BlockSpec((None, 1, S), whole),
                  pl.BlockSpec((None, 1, tk), rowtile),
                  pl.BlockSpec((None, 1, S), whole),
                  pl.BlockSpec((None, 1, S), whole)],
        out_specs=[pl.BlockSpec((None, d, S), whole),
                   pl.BlockSpec((None, tk, d), tile),
                   pl.BlockSpec((None, tk, d), tile),
                   pl.BlockSpec((None, 1, tk), rowtile),
                   pl.BlockSpec((None, 1, S), whole)],
        out_shape=[jax.ShapeDtypeStruct((H, d, S), F32), jax.ShapeDtypeStruct((H, S, d), F32),
                   jax.ShapeDtypeStruct((H, S, d), F32), jax.ShapeDtypeStruct((H, 1, S), F32),
                   jax.ShapeDtypeStruct((H, 1, S), F32)],
        scratch_shapes=[pltpu.VMEM((2, tk, ts2), F32), pltpu.VMEM((2, tk, ts2), F32),
                        pltpu.VMEM((2, tk, ts2), BF16), pltpu.VMEM((2, tk, ts2), BF16),
                        pltpu.VMEM((tk, d), F32), pltpu.VMEM((tk, d), F32), pltpu.VMEM((tk, 1), F32)],
        compiler_params=_params(("parallel", "arbitrary")),
    )(q, k, kT, v, do, f_row, f_row, lse_row, delta_row)


def _heads_major(a, H, d):
    S = a.shape[0]
    return a.reshape(S, H, d).transpose(1, 0, 2)


def _heads_minor(a):
    H, S, d = a.shape
    return a.transpose(1, 0, 2).reshape(S, H * d)


def _lane_pick(x128, lane):
    return x128[:, lane:lane + 1]


def _l2_fwd(y):
    return lax.rsqrt(jnp.sum(y * y, axis=1, keepdims=True) + EPS)


def _gdn_prep(z, conv_w, a128, dt128, ts):
    S = z.shape[0]
    C3 = 3 * WIDTH
    hb = ts // 8

    def body(t, first, ins, outs, scratch):
        x_ref, halo_ref, zs_ref, w_ref, a_ref, dt_ref = ins
        qkv_ref, c_ref, gb_ref, gbT_ref = outs
        halo = jnp.where(t > 0, halo_ref[...], 0.0)
        xe = jnp.concatenate([halo, x_ref[...]], axis=0)
        w = w_ref[...]
        c = w[3:4, :] * xe[8:, :]
        for back in (1, 2, 3):
            c = c + w[3 - back:4 - back, :] * pltpu.roll(xe, back, 0)[8:, :]
        c_ref[...] = c
        y = c * _sigmoid(c)
        for h in range(GDN_HEADS):
            lo = h * GDN_DIM
            yq = y[:, lo:lo + GDN_DIM]
            qkv_ref[:, lo:lo + GDN_DIM] = yq * (_l2_fwd(yq) * (GDN_DIM ** -0.5))
            yk = y[:, WIDTH + lo:WIDTH + lo + GDN_DIM]
            qkv_ref[:, WIDTH + lo:WIDTH + lo + GDN_DIM] = yk * _l2_fwd(yk)
        qkv_ref[:, 2 * WIDTH:] = y[:, 2 * WIDTH:]
        zs = zs_ref[...]
        lane = lax.broadcasted_iota(jnp.int32, zs.shape, 1)
        g = -jnp.exp(a_ref[...]) * _softplus(zs + dt_ref[...])
        G = _scan_rows(g, ts, CHUNK)
        beta = _sigmoid(zs)
        out = jnp.where(lane < 8, pltpu.roll(g, 128 - LANE_BA, 1), jnp.where(lane < LANE_BB, G, beta))
        gb_ref[...] = out
        gbT_ref[...] = _transpose_exact(out)

    x_in = (z, (ts, C3), lambda i: (i, CB_BQKV))
    halo_in = (z, (8, C3), lambda i: (jnp.maximum(i * hb - 1, 0), CB_BQKV))
    return _tiled("gdn_prep", body, S // ts,
                  [x_in, halo_in, _cols(z, ts, 128, CB_SMALL), _full(conv_w), _full(a128), _full(dt128)],
                  [_orow(S, (C3,), F32, ts), _orow(S, (C3,), F32, ts), _orow(S, (128,), F32, ts),
                   ((128, S), F32, (128, ts), lambda i: (0, i))])


def _chunk_masks(nc):
    r = lax.broadcasted_iota(jnp.int32, (nc, CHUNK, CHUNK), 1)
    c = lax.broadcasted_iota(jnp.int32, (nc, CHUNK, CHUNK), 2)
    return c <= r, c < r, c == r


def _chunk_local(qh, kh, vh, Gc, Gr, beta):
    nc = qh.shape[0]
    incl, strict, _ = _chunk_masks(nc)
    gamma = jnp.exp(jnp.where(incl, Gc - Gr, NEG))
    kb = kh * beta
    P = _bdot(_b16(kb), _b16(kh), 2, 2)
    Qk = _bdot(_b16(qh), _b16(kh), 2, 2)
    eG = jnp.exp(Gc)
    Gl = Gc[:, CHUNK - 1:CHUNK, :]
    edec = jnp.exp(Gl - Gc)
    return incl, strict, gamma, kb, P, Qk, eG, edec


def _gdn_local_fwd(qkv, gb, grow, ts):
    S = qkv.shape[0]
    nc = ts // CHUNK

    def body(t, first, ins, outs, scratch):
        q_ref, k_ref, v_ref, gb_ref, gr_ref = ins
        u_ref, w_ref, qd_ref, kd_ref, aqk_ref, T_ref = outs
        gbv = gb_ref[...]
        for h in range(GDN_HEADS):
            lo = h * GDN_DIM
            qh = q_ref[:, lo:lo + GDN_DIM].reshape(nc, CHUNK, GDN_DIM)
            kh = k_ref[:, lo:lo + GDN_DIM].reshape(nc, CHUNK, GDN_DIM)
            vh = v_ref[:, lo:lo + GDN_DIM].reshape(nc, CHUNK, GDN_DIM)
            Gc = _lane_pick(gbv, LANE_BA + h).reshape(nc, CHUNK, 1)
            beta = _lane_pick(gbv, LANE_BB + h).reshape(nc, CHUNK, 1)
            Gr = gr_ref[h].reshape(nc, 1, CHUNK)
            incl, strict, gamma, kb, P, Qk, eG, edec = _chunk_local(qh, kh, vh, Gc, Gr, beta)
            A = jnp.where(strict, P * gamma, 0.0)
            _, _, eye = _chunk_masks(nc)
            T = jnp.where(eye, 1.0, 0.0) - A
            X = A
            for _ in range(5):
                X = _bdot(X, X, 2, 1, HIGHEST)
                T = T + _bdot(T, X, 2, 1, HIGHEST)
            u = _bdot(T, vh * beta, 2, 1, HIGHEST)
            w = _bdot(T, kb * eG, 2, 1, HIGHEST)
            u_ref[:, lo:lo + GDN_DIM] = u.reshape(ts, GDN_DIM)
            w_ref[:, lo:lo + GDN_DIM] = w.reshape(ts, GDN_DIM)
            qd_ref[:, lo:lo + GDN_DIM] = (qh * eG).reshape(ts, GDN_DIM)
            kd_ref[:, lo:lo + GDN_DIM] = (kh * edec).reshape(ts, GDN_DIM)
            aqk_ref[h] = jnp.where(incl, Qk * gamma, 0.0).reshape(ts, CHUNK)
            T_ref[h] = T.reshape(ts, CHUNK)

    wide = _orow(S, (WIDTH,), F32, ts)
    perhead = ((GDN_HEADS, S, CHUNK), F32, (GDN_HEADS, ts, CHUNK), lambda i: (0, i, 0))
    return _tiled("gdn_local_fwd", body, S // ts,
                  [_cols(qkv, ts, WIDTH, 0), _cols(qkv, ts, WIDTH, 1), _cols(qkv, ts, WIDTH, 2),
                   _rows(gb, ts), (grow, (GDN_HEADS, nc, CHUNK), lambda i: (0, i, 0))],
                  [wide, wide, wide, wide, perhead, perhead])


def _gdn_scan_fwd(u, w, qd, kd, aqk, gb, ts):
    S = u.shape[0]
    nc = ts // CHUNK
    N = S // CHUNK

    def body(t, first, ins, outs, scratch):
        u_ref, w_ref, qd_ref, kd_ref, aqk_ref, gb_ref = ins
        o_ref, vn_ref, st_ref = outs
        (state,) = scratch

        @pl.when(first)
        def _():
            state[...] = jnp.zeros_like(state)

        def chunk(c, _):
            r0 = pl.multiple_of(c * CHUNK, CHUNK)
            rows = pl.ds(r0, CHUNK)
            glast = gb_ref[pl.ds(r0 + CHUNK - 1, 1), :]
            for h in range(GDN_HEADS):
                lo = h * GDN_DIM
                cols = slice(lo, lo + GDN_DIM)
                Sh = state[h]
                st_ref[c, h] = Sh
                Sb = _b16(Sh)
                vn = u_ref[rows, cols] - _dot(_b16(w_ref[rows, cols]), Sb)
                vnb = _b16(vn)
                o = _dot(_b16(qd_ref[rows, cols]), Sb) + _dot(_b16(aqk_ref[h, rows, :]), vnb)
                egl = jnp.exp(glast[:, LANE_BA + h:LANE_BA + h + 1])
                state[h] = Sh * egl + _dot(_b16(kd_ref[rows, cols]), vnb, TN)
                o_ref[rows, cols] = o
                vn_ref[rows, cols] = vn
            return 0

        lax.fori_loop(0, nc, chunk, 0)

    wide_in = lambda a: _rows(a, ts)
    wide = _orow(S, (WIDTH,), F32, ts)
    states = ((N, GDN_HEADS, GDN_DIM, GDN_DIM), F32, (nc, GDN_HEADS, GDN_DIM, GDN_DIM),
              lambda i: (i, 0, 0, 0))
    return _tiled("gdn_scan_fwd", body, S // ts,
                  [wide_in(u), wide_in(w), wide_in(qd), wide_in(kd),
                   (aqk, (GDN_HEADS, ts, CHUNK), lambda i: (0, i, 0)), _rows(gb, ts)],
                  [wide, wide, states],
                  scratch=[pltpu.VMEM((GDN_HEADS, GDN_DIM, GDN_DIM), F32)])


def _gdn_scan_bwd(do, w, qd, kd, aqk, vn, states, gb, ts):
    S = do.shape[0]
    nc = ts // CHUNK
    N = S // CHUNK

    def body(t, first, ins, outs, scratch):
        do_ref, w_ref, qd_ref, kd_ref, aqk_ref, vn_ref, st_ref, gb_ref = ins
        du_ref, dw_ref, dqd_ref, dkd_ref, daqk_ref, dgl_ref = outs
        (dstate,) = scratch

        @pl.when(first)
        def _():
            dstate[...] = jnp.zeros_like(dstate)

        r = lax.broadcasted_iota(jnp.int32, (CHUNK, CHUNK), 0)
        cc = lax.broadcasted_iota(jnp.int32, (CHUNK, CHUNK), 1)
        incl = cc <= r
        lane = lax.broadcasted_iota(jnp.int32, (1, 128), 1)

        def chunk(k, _):
            c = nc - 1 - k
            r0 = pl.multiple_of(c * CHUNK, CHUNK)
            rows = pl.ds(r0, CHUNK)
            glast = gb_ref[pl.ds(r0 + CHUNK - 1, 1), :]
            dgl_row = jnp.zeros((1, 128), F32)
            for h in range(GDN_HEADS):
                lo = h * GDN_DIM
                cols = slice(lo, lo + GDN_DIM)
                Sh = st_ref[c, h]
                Sb = _b16(Sh)
                dS = dstate[h]
                dSb = _b16(dS)
                dob = _b16(do_ref[rows, cols])
                aqkb = _b16(aqk_ref[h, rows, :])
                vnb = _b16(vn_ref[rows, cols])
                kdb = _b16(kd_ref[rows, cols])
                dvn = _dot(aqkb, dob, TN) + _dot(kdb, dSb)
                dvnb = _b16(dvn)
                daqk_ref[h, rows, :] = jnp.where(incl, _dot(dob, vnb, NT), 0.0)
                dqd_ref[rows, cols] = _dot(dob, Sb, NT)
                dkd_ref[rows, cols] = _dot(vnb, dSb, NT)
                dw_ref[rows, cols] = -_dot(dvnb, Sb, NT)
                du_ref[rows, cols] = dvn
                egl = jnp.exp(glast[:, LANE_BA + h:LANE_BA + h + 1])
                dgl = egl * jnp.sum(jnp.sum(dS * Sh, axis=1, keepdims=True), axis=0, keepdims=True)
                dgl_row = jnp.where(lane == h, dgl, dgl_row)
                dstate[h] = (_dot(_b16(qd_ref[rows, cols]), dob, TN) + egl * dS
                             - _dot(_b16(w_ref[rows, cols]), dvnb, TN))
            dgl_ref[pl.ds(c, 1), :] = dgl_row
            return 0

        lax.fori_loop(0, nc, chunk, 0)

    wide_in = lambda a: _rows(a, ts)
    wide = _orow(S, (WIDTH,), F32, ts)
    perhead_in = lambda a: (a, (GDN_HEADS, ts, CHUNK), lambda i: (0, i, 0))
    perhead = ((GDN_HEADS, S, CHUNK), F32, (GDN_HEADS, ts, CHUNK), lambda i: (0, i, 0))
    return _tiled("gdn_scan_bwd", body, S // ts,
                  [wide_in(do), wide_in(w), wide_in(qd), wide_in(kd), perhead_in(aqk), wide_in(vn),
                   (states, (nc, GDN_HEADS, GDN_DIM, GDN_DIM), lambda i: (i, 0, 0, 0)), _rows(gb, ts)],
                  [wide, wide, wide, wide, perhead, ((N, 128), F32, (nc, 128), lambda i: (i, 0))],
                  scratch=[pltpu.VMEM((GDN_HEADS, GDN_DIM, GDN_DIM), F32)], reverse=True)


def _gdn_local_bwd(qkv, gb, grow, T, du, dw, dqd, dkd, daqk, dgl, ts):
    S = qkv.shape[0]
    nc = ts // CHUNK

    def body(t, first, ins, outs, scratch):
        (q_ref, k_ref, v_ref, gb_ref, gr_ref, T_ref, du_ref, dw_ref, dqd_ref, dkd_ref,
         daqk_ref, dgl_ref) = ins
        dqkv_ref, dgb_ref = outs
        gbv = gb_ref[...]
        dglv = dgl_ref[...]
        lane = lax.broadcasted_iota(jnp.int32, (ts, 128), 1)
        dG_all = jnp.zeros((ts, 128), F32)
        dbeta_all = jnp.zeros((ts, 128), F32)
        for h in range(GDN_HEADS):
            lo = h * GDN_DIM
            cols = slice(lo, lo + GDN_DIM)
            r3 = lambda ref: ref[:, cols].reshape(nc, CHUNK, GDN_DIM)
            qh, kh, vh = r3(q_ref), r3(k_ref), r3(v_ref)
            duh, dwh, dqdh, dkdh = r3(du_ref), r3(dw_ref), r3(dqd_ref), r3(dkd_ref)
            Gc = _lane_pick(gbv, LANE_BA + h).reshape(nc, CHUNK, 1)
            beta = _lane_pick(gbv, LANE_BB + h).reshape(nc, CHUNK, 1)
            Gr = gr_ref[h].reshape(nc, 1, CHUNK)
            Th = T_ref[h].reshape(nc, CHUNK, CHUNK)
            daq = daqk_ref[h].reshape(nc, CHUNK, CHUNK)
            incl, strict, gamma, kb, P, Qk, eG, edec = _chunk_local(qh, kh, vh, Gc, Gr, beta)
            _, _, eye = _chunk_masks(nc)
            vb = vh * beta
            kbg = kb * eG
            dvb = _bdot(Th, duh, 1, 1, HIGHEST)
            dkbg = _bdot(Th, dwh, 1, 1, HIGHEST)
            dT = _bdot(duh, vb, 2, 2, HIGHEST) + _bdot(dwh, kbg, 2, 2, HIGHEST)
            M1 = _bdot(Th, dT, 1, 1, HIGHEST)
            dA = jnp.where(strict, -_bdot(M1, Th, 2, 2, HIGHEST), 0.0)
            dP = dA * gamma
            dQ = daq * gamma
            dgam = (dA * P + daq * Qk) * gamma
            dPb, dQb = _b16(dP), _b16(dQ)
            khb, qhb, kbb = _b16(kh), _b16(qh), _b16(kb)
            dq = _bdot(dQb, khb, 2, 1) + dqdh * eG
            dkb = _bdot(dPb, khb, 2, 1) + dkbg * eG
            dk = (_bdot(dQb, qhb, 1, 1) + _bdot(dPb, kbb, 1, 1) + dkdh * edec + dkb * beta)
            dbeta = (jnp.sum(dkb * kh, axis=2, keepdims=True) + jnp.sum(dvb * vh, axis=2, keepdims=True))
            dv = dvb * beta
            col_as_col = jnp.sum(jnp.where(eye, jnp.sum(dgam, axis=1, keepdims=True), 0.0),
                                 axis=2, keepdims=True)
            kd_term = jnp.sum(dkdh * kh * edec, axis=2, keepdims=True)
            dG = (jnp.sum(dgam, axis=2, keepdims=True) - col_as_col
                  + jnp.sum(dqdh * qh * eG, axis=2, keepdims=True)
                  + jnp.sum(dkbg * kbg, axis=2, keepdims=True) - kd_term)
            dgl_h = dglv[:, h:h + 1].reshape(nc, 1, 1) + jnp.sum(kd_term, axis=1, keepdims=True)
            last = lax.broadcasted_iota(jnp.int32, (nc, CHUNK, 1), 1) == CHUNK - 1
            dG = dG + jnp.where(last, dgl_h, 0.0)
            dqkv_ref[:, cols] = dq.reshape(ts, GDN_DIM)
            dqkv_ref[:, WIDTH + lo:WIDTH + lo + GDN_DIM] = dk.reshape(ts, GDN_DIM)
            dqkv_ref[:, 2 * WIDTH + lo:2 * WIDTH + lo + GDN_DIM] = dv.reshape(ts, GDN_DIM)
            dG_all = jnp.where(lane == LANE_BA + h, dG.reshape(ts, 1), dG_all)
            dbeta_all = jnp.where(lane == LANE_BB + h, dbeta.reshape(ts, 1), dbeta_all)
        dg_all = _scan_rows(dG_all, ts, CHUNK, reverse=True)
        dgb_ref[...] = jnp.where(lane < LANE_BB, dg_all, dbeta_all)

    wide_in = lambda a: _rows(a, ts)
    perhead_in = lambda a: (a, (GDN_HEADS, ts, CHUNK), lambda i: (0, i, 0))
    return _tiled("gdn_local_bwd", body, S // ts,
                  [_cols(qkv, ts, WIDTH, 0), _cols(qkv, ts, WIDTH, 1), _cols(qkv, ts, WIDTH, 2),
                   _rows(gb, ts), (grow, (GDN_HEADS, nc, CHUNK), lambda i: (0, i, 0)), perhead_in(T),
                   wide_in(du), wide_in(dw), wide_in(dqd), wide_in(dkd), perhead_in(daqk),
                   (dgl, (nc, 128), lambda i: (i, 0))],
                  [_orow(S, (3 * WIDTH,), F32, ts), _orow(S, (128,), F32, ts)])


def _gdn_prep_bwd(dqkv, dgb, cpre, z, conv_w, a128, dt128, ts):
    S = z.shape[0]
    C3 = 3 * WIDTH
    hb = ts // 8
    n_tiles = S // ts

    def dpre(dq, c):
        y, dsil = _silu_and_grad(c)
        parts = []
        for h in range(GDN_HEADS):
            lo = h * GDN_DIM
            yq = y[:, lo:lo + GDN_DIM]
            rq = _l2_fwd(yq)
            nq = yq * rq
            dn = dq[:, lo:lo + GDN_DIM] * (GDN_DIM ** -0.5)
            parts.append(rq * (dn - nq * jnp.sum(dn * nq, axis=1, keepdims=True)))
        for h in range(GDN_HEADS):
            lo = WIDTH + h * GDN_DIM
            yk = y[:, lo:lo + GDN_DIM]
            rk = _l2_fwd(yk)
            nk = yk * rk
            dn = dq[:, lo:lo + GDN_DIM]
            parts.append(rk * (dn - nk * jnp.sum(dn * nk, axis=1, keepdims=True)))
        parts.append(dq[:, 2 * WIDTH:])
        return jnp.concatenate(parts, axis=1) * dsil

    def body(t, first, ins, outs, scratch):
        (dq_ref, dqn_ref, c_ref, cn_ref, x_ref, xp_ref, zs_ref, dgb_ref, w_ref, a_ref, dt_ref) = ins
        dx_ref, dzs_ref, dw_ref, dad_ref = outs

        @pl.when(first)
        def _():
            dw_ref[...] = jnp.zeros_like(dw_ref)
            dad_ref[...] = jnp.zeros_like(dad_ref)

        dc = dpre(dq_ref[...], c_ref[...])
        dcn = jnp.where(t < n_tiles - 1, dpre(dqn_ref[...], cn_ref[...]), 0.0)
        dce = jnp.concatenate([dc, dcn], axis=0)
        w = w_ref[...]
        dx = w[3:4, :] * dc
        for back in (1, 2, 3):
            dx = dx + w[3 - back:4 - back, :] * pltpu.roll(dce, ts + 8 - back, 0)[:ts, :]
        dx_ref[...] = dx
        halo = jnp.where(t > 0, xp_ref[...], 0.0)
        xe = jnp.concatenate([halo, x_ref[...]], axis=0)
        dw_ref[3:4, :] += jnp.sum(dc * xe[8:, :], axis=0, keepdims=True)
        for back in (1, 2, 3):
            dw_ref[3 - back:4 - back, :] += jnp.sum(dc * pltpu.roll(xe, back, 0)[8:, :], axis=0,
                                                     keepdims=True)
        zs = zs_ref[...]
        dgb = dgb_ref[...]
        lane = lax.broadcasted_iota(jnp.int32, zs.shape, 1)
        arg = zs + dt_ref[...]
        nega = -jnp.exp(a_ref[...])
        dba = dgb * nega * _sigmoid(arg)
        beta = _sigmoid(zs)
        dbb = dgb * beta * (1.0 - beta)
        dzs_ref[...] = jnp.where((lane >= LANE_BA) & (lane < LANE_BB), dba,
                                 jnp.where((lane >= LANE_BB) & (lane < LANE_BB + 4), dbb, 0.0))
        dad_ref[0:1, :] += jnp.sum(dgb * nega * _softplus(arg), axis=0, keepdims=True)
        dad_ref[1:2, :] += jnp.sum(dba, axis=0, keepdims=True)

    nxt = lambda i: (jnp.minimum((i + 1) * hb, S // 8 - 1), 0)
    prv = lambda i: (jnp.maximum(i * hb - 1, 0), CB_BQKV)
    return _tiled("gdn_prep_bwd", body, n_tiles,
                  [_rows(dqkv, ts), (dqkv, (8, C3), nxt), _rows(cpre, ts), (cpre, (8, C3), nxt),
                   (z, (ts, C3), lambda i: (i, CB_BQKV)), (z, (8, C3), prv),
                   _cols(z, ts, 128, CB_SMALL), _rows(dgb, ts), _full(conv_w), _full(a128), _full(dt128)],
                  [_orow(S, (C3,), F32, ts), _orow(S, (128,), F32, ts), _oacc((8, C3), F32),
                   _oacc((8, 128), F32)])


def _mem_attn_fwd(z, mk, mv, ts):
    S = z.shape[0]

    def body(t, first, ins, outs, scratch):
        q_ref, mk_ref, mv_ref = ins
        (o_ref,) = outs
        for h in range(MEM_HEADS):
            cols = slice(h * MEM_DIM, (h + 1) * MEM_DIM)
            s = _dot(_b16(q_ref[:, cols]), _b16(mk_ref[:, cols]), NT) * (MEM_DIM ** -0.5)
            m = jnp.max(s, axis=1, keepdims=True)
            p = jnp.exp(s - m)
            p = p / jnp.sum(p, axis=1, keepdims=True)
            o_ref[:, cols] = _dot(_b16(p), _b16(mv_ref[:, cols]))

    (o,) = _tiled("mem_attn_fwd", body, S // ts, [_cols(z, ts, WIDTH, CB_MQ), _full(mk), _full(mv)],
                  [_orow(S, (WIDTH,), F32, ts)])
    return o


def _mem_attn_bwd(do, z, mk, mv, ts):
    S = z.shape[0]
    M = mk.shape[0]

    def body(t, first, ins, outs, scratch):
        do_ref, q_ref, mk_ref, mv_ref = ins
        dq_ref, dmk_ref, dmv_ref = outs

        @pl.when(first)
        def _():
            dmk_ref[...] = jnp.zeros_like(dmk_ref)
            dmv_ref[...] = jnp.zeros_like(dmv_ref)

        scale = MEM_DIM ** -0.5
        for h in range(MEM_HEADS):
            cols = slice(h * MEM_DIM, (h + 1) * MEM_DIM)
            qb = _b16(q_ref[:, cols])
            kb = _b16(mk_ref[:, cols])
            dob = _b16(do_ref[:, cols])
            s = _dot(qb, kb, NT) * scale
            m = jnp.max(s, axis=1, keepdims=True)
            p = jnp.exp(s - m)
            p = p / jnp.sum(p, axis=1, keepdims=True)
            dmv_ref[:, cols] += _dot(_b16(p), dob, TN)
            dp = _dot(dob, _b16(mv_ref[:, cols]), NT)
            ds = p * (dp - jnp.sum(dp * p, axis=1, keepdims=True)) * scale
            dsb = _b16(ds)
            dq_ref[:, cols] = _dot(dsb, kb)
            dmk_ref[:, cols] += _dot(dsb, qb, TN)

    return _tiled("mem_attn_bwd", body, S // ts,
                  [_rows(do, ts), _cols(z, ts, WIDTH, CB_MQ), _full(mk), _full(mv)],
                  [_orow(S, (WIDTH,), F32, ts), _oacc((M, WIDTH), F32), _oacc((M, WIDTH), F32)])


def _head_norm(ob, g):
    xs, rs = [], []
    for h in range(GDN_HEADS):
        o = ob[:, h * GDN_DIM:(h + 1) * GDN_DIM]
        r = lax.rsqrt(jnp.mean(o * o, axis=1, keepdims=True) + EPS)
        xs.append(o * r)
        rs.append(r)
    return xs, rs


def _merge_fwd(x, z, o_a, o_b, o_m, gdn_g, b_merge, wb, wout, ts):
    S, D = x.shape

    def body(t, first, ins, outs, scratch):
        (x_ref, g_ref, oa_ref, az_ref, ob_ref, bz_ref, om_ref, mz_ref, gg_ref, bm_ref, wb_ref,
         wo_ref) = ins
        xo_ref, ya_ref, yb_ref, ym_ref, mg_ref = outs
        ya = oa_ref[...] * _silu_and_grad(az_ref[...])[0]
        xs, _ = _head_norm(ob_ref[...], None)
        nb = jnp.concatenate([xh * gg_ref[...] for xh in xs], axis=1)
        yb = nb * _silu_and_grad(bz_ref[...])[0]
        ym = om_ref[...] * _silu_and_grad(mz_ref[...])[0]
        merged = jnp.zeros((ts, D), F32)
        for n, (y, y_ref) in enumerate(((ya, ya_ref), (yb, yb_ref), (ym, ym_ref))):
            yb16 = _b16(y)
            y_ref[...] = yb16
            gate = _sigmoid(g_ref[:, n * D:(n + 1) * D] + bm_ref[:, n * D:(n + 1) * D])
            merged = merged + gate * _dot(yb16, wb_ref[n])
        mb = _b16(merged)
        mg_ref[...] = mb
        xo_ref[...] = x_ref[...] + _dot(mb, wo_ref[...])

    half = lambda a: _rows(a, ts)
    return _tiled("merge_fwd", body, S // ts,
                  [_rows(x, ts), _cols(z, ts, 3 * D, CB_GATES), half(o_a), _cols(z, ts, WIDTH, CB_AZ),
                   half(o_b), _cols(z, ts, WIDTH, CB_BZ), half(o_m), _cols(z, ts, WIDTH, CB_MZ),
                   _full(gdn_g.reshape(1, GDN_DIM)), _full(b_merge.reshape(1, 3 * D)), _full(wb), _full(wout)],
                  [_orow(S, (D,), F32, ts), _orow(S, (WIDTH,), BF16, ts), _orow(S, (WIDTH,), BF16, ts),
                   _orow(S, (WIDTH,), BF16, ts), _orow(S, (D,), BF16, ts)])


def _merge_bwd(dout, z, o_a, o_b, o_m, ya, yb, ym, gdn_g, b_merge, wb, wout, hsum, ts):
    S, D = dout.shape

    def body(t, first, ins, outs, scratch):
        (do_ref, g_ref, oa_ref, az_ref, ob_ref, bz_ref, om_ref, mz_ref, ya_ref, yb_ref, ym_ref,
         gg_ref, bm_ref, wb_ref, wo_ref, hs_ref) = ins
        (dg_ref, dpa_ref, dpb_ref, dpm_ref, doa_ref, dob_ref, dom_ref, daz_ref, dbz_ref, dmz_ref,
         dl_ref, dbm_ref, dgg_ref) = outs

        @pl.when(first)
        def _():
            dbm_ref[...] = jnp.zeros_like(dbm_ref)
            dgg_ref[...] = jnp.zeros_like(dgg_ref)

        dmerged = _dot(_b16(do_ref[...]), wo_ref[...], NT)
        dys = []
        for n, (y_ref, dp_ref) in enumerate(((ya_ref, dpa_ref), (yb_ref, dpb_ref), (ym_ref, dpm_ref))):
            sl = slice(n * D, (n + 1) * D)
            gate = _sigmoid(g_ref[:, sl] + bm_ref[:, sl])
            proj = _dot(y_ref[...], wb_ref[n])
            dproj = _b16(gate * dmerged)
            dp_ref[...] = dproj
            dgp = dmerged * proj * gate * (1.0 - gate)
            dg_ref[:, sl] = dgp.astype(dg_ref.dtype)
            dbm_ref[0:1, sl] += jnp.sum(dgp, axis=0, keepdims=True)
            dys.append(_dot(dproj, wb_ref[n], NT))
        dya, dyb, dym = dys
        sa, dsa = _silu_and_grad(az_ref[...])
        oa = oa_ref[...]
        doa = dya * sa
        doa_ref[...] = doa
        daz_ref[...] = dya * oa * dsa
        dl_ref[...] = _dot(hs_ref[...], doa * oa, NT, HIGHEST)
        sm, dsm = _silu_and_grad(mz_ref[...])
        dom_ref[...] = dym * sm
        dmz_ref[...] = dym * om_ref[...] * dsm
        sb, dsb = _silu_and_grad(bz_ref[...])
        xs, rs = _head_norm(ob_ref[...], None)
        gg = gg_ref[...]
        dgg = jnp.zeros((1, GDN_DIM), F32)
        for h in range(GDN_HEADS):
            cols = slice(h * GDN_DIM, (h + 1) * GDN_DIM)
            dn = dyb[:, cols] * sb[:, cols]
            dbz_ref[:, cols] = dyb[:, cols] * (xs[h] * gg) * dsb[:, cols]
            dgg = dgg + jnp.sum(dn * xs[h], axis=0, keepdims=True)
            dxh = dn * gg
            dob_ref[:, cols] = rs[h] * (dxh - xs[h] * jnp.mean(dxh * xs[h], axis=1, keepdims=True))
        dgg_ref[0:1, :] += dgg

    half = lambda a: _rows(a, ts)
    w512 = lambda dt: _orow(S, (WIDTH,), dt, ts)
    return _tiled("merge_bwd", body, S // ts,
                  [_rows(dout, ts), _cols(z, ts, 3 * D, CB_GATES), half(o_a), _cols(z, ts, WIDTH, CB_AZ),
                   half(o_b), _cols(z, ts, WIDTH, CB_BZ), half(o_m), _cols(z, ts, WIDTH, CB_MZ),
                   half(ya), half(yb), half(ym), _full(gdn_g.reshape(1, GDN_DIM)),
                   _full(b_merge.reshape(1, 3 * D)), _full(wb), _full(wout), _full(hsum)],
                  [_orow(S, (3 * D,), BF16, ts), _orow(S, (D,), BF16, ts), _orow(S, (D,), BF16, ts),
                   _orow(S, (D,), BF16, ts), w512(F32), w512(F32), w512(F32), w512(F32), w512(F32),
                   w512(F32), ((128, S), F32, (128, ts), lambda i: (0, i)), _oacc((8, 3 * D), F32),
                   _oacc((8, GDN_DIM), F32)])


def _to_aligned(w):
    sizes = (512, 512, 512, 8, 512, 512, 512, 512, 4, 4, 512, 512, 512, 3072)
    names = ("aq", "ak", "av", "af", "az", "bq", "bk", "bv", "ba", "bb", "bz", "mq", "mz", "gates")
    p, off = {}, 0
    for n, s in zip(names, sizes):
        p[n] = w[..., off:off + s]
        off += s
    pad = jnp.zeros(w.shape[:-1] + (128 - 16,), w.dtype)
    return jnp.concatenate([p["gates"], p["bq"], p["bk"], p["bv"], p["aq"], p["ak"], p["av"], p["az"],
                            p["bz"], p["mq"], p["mz"], p["af"], p["ba"], p["bb"], pad], axis=-1)


def _from_aligned(w):
    c = lambda lo, n: w[..., lo:lo + n]
    gates, bq, bk, bv = c(0, 3072), c(3072, 512), c(3584, 512), c(4096, 512)
    aq, ak, av, az = c(4608, 512), c(5120, 512), c(5632, 512), c(6144, 512)
    bz, mq, mz = c(6656, 512), c(7168, 512), c(7680, 512)
    af, ba, bb = c(8192, 8), c(8200, 4), c(8204, 4)
    return jnp.concatenate([aq, ak, av, af, az, bq, bk, bv, ba, bb, bz, mq, mz, gates], axis=-1)


def _lanes128(v, lane0):
    return jnp.pad(v.astype(F32)[None, :], ((0, 0), (lane0, 128 - lane0 - v.shape[0])))


def _tiles(S):
    ts = min(512, S // 2)
    return dict(ts=ts, ts_small=min(256, S // 2), tq=min(512, S // 4), tq_fwd=min(1024, S // 2))


def _layer_fwd(x, mem, p):
    S = x.shape[0]
    tl = _tiles(S)
    ts, tss, tq = tl["ts"], tl["ts_small"], tl["tq"]
    h, rstd = _rms_fwd("norm_fwd", x, p["norm_g"], ts)
    z = _mm("in_proj", h, p["w_in_al"], tn=1664)

    b_fg128 = _lanes128(p["b_fg"], LANE_AF)
    fT, f_hi, f_mid, f_lo = _fox_decay(z, b_fg128, ts)
    aq = z[:, CB_AQ * WIDTH:(CB_AQ + 1) * WIDTH]
    ak = z[:, CB_AK * WIDTH:(CB_AK + 1) * WIDTH]
    av = z[:, CB_AV * WIDTH:(CB_AV + 1) * WIDTH]
    q32 = _heads_major(aq, FOX_HEADS, FOX_DIM)
    qh = q32.astype(BF16)
    kh = _heads_major(ak, FOX_HEADS, FOX_DIM).astype(BF16)
    vh = _heads_major(av, FOX_HEADS, FOX_DIM).astype(BF16)
    khT = kh.transpose(0, 2, 1)
    f_row = fT[:FOX_HEADS, None, :]
    piecesT = jnp.stack([f[:FOX_HEADS] for f in (f_hi, f_mid, f_lo)], axis=1)
    ones3 = jnp.ones((FOX_HEADS, S, 3), BF16)
    padk = jnp.zeros((FOX_HEADS, S, FOX_AUG - FOX_DIM - 6), BF16)
    q_aug = jnp.concatenate([q32, piecesT.transpose(0, 2, 1).astype(F32), ones3.astype(F32),
                             padk.astype(F32)], axis=-1)
    kT_aug = jnp.concatenate([khT, ones3.transpose(0, 2, 1), -piecesT, padk.transpose(0, 2, 1)], axis=1)
    v_aug = jnp.concatenate([vh, ones3[:, :, :1], jnp.zeros((FOX_HEADS, S, 128 - FOX_DIM - 1), BF16)],
                            axis=-1)
    o_h, lse = _fox_fwd(q_aug, kT_aug, v_aug, tl["tq_fwd"])
    o_a = _heads_minor(o_h)

    a128 = _lanes128(p["a_log"], LANE_BA)
    dt128 = _lanes128(p["dt_bias"], LANE_BA)
    qkv, cpre, gb, gbT = _gdn_prep(z, p["conv_w"], a128, dt128, ts)
    grow = gbT[LANE_BA:LANE_BA + GDN_HEADS].reshape(GDN_HEADS, S // CHUNK, CHUNK)
    u, w, qd, kd, aqk, T = _gdn_local_fwd(qkv, gb, grow, ts)
    o_b, vn, states = _gdn_scan_fwd(u, w, qd, kd, aqk, gb, ts)

    mem_h, mem_r = _rms_fwd("mem_norm_fwd", mem, p["mem_norm_g"], mem.shape[0])
    mkv = _mm("mem_kv", mem_h, p["w_mem_kv"])
    mk, mv = mkv[:, :WIDTH], mkv[:, WIDTH:]
    o_m = _mem_attn_fwd(z, mk, mv, ts)

    x_next, ya, yb, ym, merged = _merge_fwd(x, z, o_a, o_b, o_m, p["gdn_norm_g"], p["b_merge"],
                                            p["w_branch"], p["w_out"], tss)
    saved = dict(x=x, h=h, rstd=rstd, z=z, b_fg128=b_fg128, qh=qh, kh=kh, khT=khT, vh=vh,
                 f_row=f_row, lse=lse, o_a=o_a, a128=a128, dt128=dt128, qkv=qkv, cpre=cpre, gb=gb,
                 grow=grow, w=w, qd=qd, kd=kd, aqk=aqk, T=T, o_b=o_b, vn=vn, states=states,
                 mem_h=mem_h, mem_r=mem_r, mk=mk, mv=mv, o_m=o_m, ya=ya, yb=yb, ym=ym, merged=merged)
    return x_next, saved


def _layer_bwd(dout, mem, p, s):
    S = dout.shape[0]
    tl = _tiles(S)
    ts, tss, tq = tl["ts"], tl["ts_small"], tl["tq"]
    z = s["z"]
    hsum = (jnp.arange(128)[:, None] == jnp.arange(WIDTH)[None, :] // FOX_DIM).astype(F32)
    (dgates, dpa, dpb, dpm, do_a, do_b, do_m, daz, dbz, dmz, deltaT, db_merge, dgdn_g) = _merge_bwd(
        dout, z, s["o_a"], s["o_b"], s["o_m"], s["ya"], s["yb"], s["ym"], p["gdn_norm_g"],
        p["b_merge"], p["w_branch"], p["w_out"], hsum, tss)
    g = {}
    g["b_merge"] = db_merge[0]
    g["gdn_norm_g"] = dgdn_g[0]
    g["w_out"] = _mm("dw_out", s["merged"], dout, ta=True)
    g["w_branch"] = jnp.stack([_mm("dw_branch", y, dp, ta=True)
                               for y, dp in ((s["ya"], dpa), (s["yb"], dpb), (s["ym"], dpm))])

    do_h = _heads_major(do_a, FOX_HEADS, FOX_DIM).astype(BF16)
    delta_row = deltaT[:FOX_HEADS, None, :]
    dqT, dk_h, dv_h, dfk, dfq = _fox_bwd(s["qh"], s["kh"], s["khT"], s["vh"], do_h, s["f_row"],
                                         s["lse"], delta_row, tq)
    daq = _heads_minor(dqT.transpose(0, 2, 1))
    dak = _heads_minor(dk_h)
    dav = _heads_minor(dv_h)
    daf128, db_fg = _fox_decay_bwd(dfk[:, 0, :], dfq[:, 0, :], z, s["b_fg128"], ts)
    g["b_fg"] = db_fg[:FOX_HEADS]

    du, dw, dqd, dkd, daqk, dgl = _gdn_scan_bwd(do_b, s["w"], s["qd"], s["kd"], s["aqk"], s["vn"],
                                                s["states"], s["gb"], ts)
    dqkv, dgb = _gdn_local_bwd(s["qkv"], s["gb"], s["grow"], s["T"], du, dw, dqd, dkd, daqk, dgl, ts)
    dbqkv, dzs_b, dconv, dad = _gdn_prep_bwd(dqkv, dgb, s["cpre"], z, p["conv_w"], s["a128"],
                                             s["dt128"], ts)
    g["conv_w"] = dconv[:4]
    g["a_log"] = dad[0, LANE_BA:LANE_BA + GDN_HEADS]
    g["dt_bias"] = dad[1, LANE_BA:LANE_BA + GDN_HEADS]

    dmq, dmk, dmv = _mem_attn_bwd(do_m, z, s["mk"], s["mv"], ts)
    dmkv = jnp.concatenate([dmk, dmv], axis=1)
    g["w_mem_kv"] = _mm("dw_mem_kv", s["mem_h"], dmkv, ta=True)
    dmem_h = _mm("dmem_h", dmkv, p["w_mem_kv"], tb=True)
    M = mem.shape[0]
    _, g["mem_norm_g"] = _rms_bwd("mem_norm_bwd", dmem_h, mem, s["mem_r"], p["mem_norm_g"],
                                  jnp.zeros_like(mem), M)

    lane = jnp.arange(128)[None, :]
    dsmall = jnp.where(lane < 8, daf128, dzs_b)
    dz = jnp.concatenate([dgates, _b16(dbqkv), _b16(daq), _b16(dak), _b16(dav), _b16(daz), _b16(dbz),
                          _b16(dmq), _b16(dmz), _b16(dsmall)], axis=1)
    g["w_in_al"] = _mm("dw_in", s["h"], dz, ta=True, tn=1664)
    dh = _mm("dh", dz, p["w_in_al"], tb=True, tk=1664)
    dx, g["norm_g"] = _rms_bwd("norm_bwd", dh, s["x"], s["rstd"], p["norm_g"], dout, ts)
    return dx, g


def _local_step(x, mem, layers, final_norm_g, loss_target):
    S = x.shape[0]
    saves = []
    cur = x
    for p in layers:
        cur, sv = _layer_fwd(cur, mem, p)
        saves.append(sv)
    dx, dgf, loss_lanes = _loss_head(cur, final_norm_g, loss_target, _tiles(S)["ts"])
    grads = [None] * len(layers)
    for l in reversed(range(len(layers))):
        dx, grads[l] = _layer_bwd(dx, mem, layers[l], saves[l])
    return loss_lanes, dx, grads, dgf


HBM_SPEC = pl.BlockSpec(memory_space=pltpu.HBM)


def _mesh_pos():
    return lax.axis_index("x"), lax.axis_index("y"), lax.axis_index("c")


def _comm_call(name, body, arrays, out_shapes, n_remote, n_local):
    n = len(arrays)

    def kern(*refs):
        body(refs[:n], refs[n:2 * n], refs[2 * n], refs[2 * n + 1], refs[2 * n + 2])

    return pl.pallas_call(
        kern, name=name, out_shape=out_shapes, in_specs=[HBM_SPEC] * n, out_specs=[HBM_SPEC] * n,
        scratch_shapes=[pltpu.SemaphoreType.DMA((n_remote,)), pltpu.SemaphoreType.DMA((n_remote,)),
                        pltpu.SemaphoreType.DMA((max(n_local, 1),))],
    )(*arrays)


def _remote(src, dst, send_sems, recv_sems, k, to):
    return pltpu.make_async_remote_copy(src_ref=src, dst_ref=dst, send_sem=send_sems.at[k],
                                        recv_sem=recv_sems.at[k], device_id=to, device_id_type=MESH_ID)


def _other_chips(mx, my):
    return [(1 - mx, my), (mx, 1 - my), (1 - mx, 1 - my)]


def _gather_chips(name, shards):
    n = len(shards)

    def body(ins, outs, send_sems, recv_sems, local_sems):
        mx, my, mc = _mesh_pos()
        me = 2 * mx + my
        chips = _other_chips(mx, my)
        locals_, sends = [], []
        for a in range(n):
            lc = pltpu.make_async_copy(ins[a], outs[a].at[me], local_sems.at[a])
            lc.start()
            locals_.append(lc)
            for k, (px, py) in enumerate(chips):
                cp = _remote(ins[a], outs[a].at[me], send_sems, recv_sems, 3 * a + k, (px, py, mc))
                cp.start()
                sends.append(cp)
        for a in range(n):
            for k, (px, py) in enumerate(chips):
                _remote(ins[a], outs[a].at[2 * px + py], send_sems, recv_sems, 3 * a + k,
                        (px, py, mc)).wait_recv()
        for cp in sends:
            cp.wait_send()
        for lc in locals_:
            lc.wait()

    shapes = [jax.ShapeDtypeStruct((N_CHIPS,) + s.shape, s.dtype) for s in shards]
    return _comm_call(name, body, shards, shapes, 3 * n, n)


def _sibling_swap(gs):
    n = len(gs)

    def body(ins, outs, send_sems, recv_sems, local_sems):
        mx, my, mc = _mesh_pos()
        sends = []
        for a in range(n):
            cp = _remote(ins[a].at[:, 1 - mc], outs[a], send_sems, recv_sems, a, (mx, my, 1 - mc))
            cp.start()
            sends.append(cp)
        for cp in sends:
            cp.wait()

    shapes = [jax.ShapeDtypeStruct((g.shape[0],) + g.shape[2:], g.dtype) for g in gs]
    return _comm_call("grad_sibling_swap", body, gs, shapes, n, 0)


def _chip_exchange(ps):
    n = len(ps)

    def body(ins, outs, send_sems, recv_sems, local_sems):
        mx, my, mc = _mesh_pos()
        me = 2 * mx + my
        chips = _other_chips(mx, my)
        locals_, sends = [], []
        for a in range(n):
            lc = pltpu.make_async_copy(ins[a].at[me], outs[a].at[me], local_sems.at[a])
            lc.start()
            locals_.append(lc)
            for k, (px, py) in enumerate(chips):
                cp = _remote(ins[a].at[2 * px + py], outs[a].at[me], send_sems, recv_sems, 3 * a + k,
                             (px, py, mc))
                cp.start()
                sends.append(cp)
        for a in range(n):
            for k, (px, py) in enumerate(chips):
                _remote(ins[a].at[me], outs[a].at[2 * px + py], send_sems, recv_sems, 3 * a + k,
                        (px, py, mc)).wait_recv()
        for cp in sends:
            cp.wait_send()
        for lc in locals_:
            lc.wait()

    shapes = [jax.ShapeDtypeStruct(p.shape, p.dtype) for p in ps]
    return _comm_call("grad_chip_exchange", body, ps, shapes, 3 * n, n)


def _sibling_gather(hs):
    n = len(hs)

    def body(ins, outs, send_sems, recv_sems, local_sems):
        mx, my, mc = _mesh_pos()
        locals_, sends = [], []
        for a in range(n):
            lc = pltpu.make_async_copy(ins[a], outs[a].at[mc], local_sems.at[a])
            lc.start()
            locals_.append(lc)
            cp = _remote(ins[a], outs[a].at[mc], send_sems, recv_sems, a, (mx, my, 1 - mc))
            cp.start()
            sends.append(cp)
        for a in range(n):
            _remote(ins[a], outs[a].at[1 - mc], send_sems, recv_sems, a, (mx, my, 1 - mc)).wait_recv()
        for cp in sends:
            cp.wait_send()
        for lc in locals_:
            lc.wait()

    shapes = [jax.ShapeDtypeStruct((2,) + h.shape, h.dtype) for h in hs]
    return _comm_call("grad_sibling_gather", body, hs, shapes, n, n)


def _add_pairs(a, b, tr):
    n, H, C = a.shape

    def kern(a_ref, b_ref, o_ref):
        o_ref[...] = a_ref[...] + b_ref[...]

    spec = pl.BlockSpec((None, tr, C), lambda j, i: (j, i, 0))
    return pl.pallas_call(
        kern, name="grad_pair_sum", grid=(n, H // tr), in_specs=[spec, spec], out_specs=spec,
        out_shape=jax.ShapeDtypeStruct((n, H, C), a.dtype),
        compiler_params=_params(("parallel", "parallel")),
    )(a, b)


def _sum_slots(r4, tr):
    n, H, C = r4.shape

    def kern(r_ref, o_ref):
        o_ref[...] = ((r_ref[0] + r_ref[1]) + r_ref[2]) + r_ref[3]

    return pl.pallas_call(
        kern, name="grad_chip_sum", grid=(H // tr,),
        in_specs=[pl.BlockSpec((n, tr, C), lambda i: (0, i, 0))],
        out_specs=pl.BlockSpec((tr, C), lambda i: (i, 0)),
        out_shape=jax.ShapeDtypeStruct((H, C), r4.dtype),
        compiler_params=_params(("parallel",)),
    )(r4)


def _adamw(w, g, m, v, tr):
    R, C = w.shape
    c1 = 1.0 - ADAM_B1
    c2 = 1.0 - ADAM_B2
    bc1 = 1.0 - ADAM_B1 ** ADAM_STEP
    bc2 = 1.0 - ADAM_B2 ** ADAM_STEP

    def kern(w_ref, g_ref, m_ref, v_ref, d_ref, mo_ref, vo_ref):
        gv = g_ref[...]
        mn = ADAM_B1 * m_ref[...] + c1 * gv
        vn = ADAM_B2 * v_ref[...] + c2 * (gv * gv)
        m_hat = mn / bc1
        v_hat = vn / bc2
        d_ref[...] = -ADAM_LR * (m_hat / (jnp.sqrt(v_hat) + ADAM_EPS) + ADAM_WD * w_ref[...])
        mo_ref[...] = mn
        vo_ref[...] = vn

    spec = pl.BlockSpec((tr, C), lambda i: (i, 0))
    shape = jax.ShapeDtypeStruct((R, C), F32)
    return pl.pallas_call(
        kern, name="adamw", grid=(R // tr,), in_specs=[spec] * 4, out_specs=[spec] * 3,
        out_shape=[shape] * 3, compiler_params=_params(("parallel",)),
    )(w, g, m, v)


PACK_COLS = 1024
PACK_ROWS = 512
W_SHARD = N_IN // N_CHIPS
SLAB = ("conv_w", "w_mem_kv", "w_branch", "w_out")
SMALL =("norm_g", "b_fg", "b_merge", "a_log", "dt_bias", "gdn_norm_g", "mem_norm_g", "final_norm_g")
ALL_WEIGHTS = ("norm_g", "w_in", "b_fg", "b_merge", "conv_w", "a_log", "dt_bias", "gdn_norm_g",
               "mem_norm_g", "w_mem_kv", "w_branch", "w_out", "final_norm_g")
SHARD_AXIS = {"w_in": 2, "conv_w": 2, "w_mem_kv": 1, "w_branch": 3, "w_out": 1}


def _pack(arrays, row_multiple):
    flat = jnp.concatenate([a.reshape(-1) for a in arrays])
    n = flat.shape[0]
    rows = -(-n // PACK_COLS)
    rows = -(-rows // row_multiple) * row_multiple
    flat = jnp.pad(flat, (0, rows * PACK_COLS - n))
    return flat.reshape(rows, PACK_COLS)


def _unpack(slab, shapes):
    flat = slab.reshape(-1)
    out, off = [], 0
    for shp in shapes:
        n = 1
        for d in shp:
            n *= d
        out.append(flat[off:off + n].reshape(shp))
        off += n
    return out


def _shard_of(full, name, j):
    ax = SHARD_AXIS[name]
    n = full.shape[ax] // N_CHIPS
    return lax.slice_in_dim(full, j * n, (j + 1) * n, axis=ax)


W_IN_PIECES = ((0, 512, 4608), (512, 512, 5120), (1024, 512, 5632), (1536, 8, 8192), (1544, 512, 6144),
               (2056, 512, 3072), (2568, 512, 3584), (3080, 512, 4096), (3592, 4, 8200), (3596, 4, 8204),
               (3600, 512, 6656), (4112, 512, 7168), (4624, 512, 7680), (5136, 3072, 0))


def _aligned_from_shards(shards):
    def cols(lo, n):
        parts = []
        while n > 0:
            j, off = divmod(lo, W_SHARD)
            take = min(n, W_SHARD - off)
            parts.append(shards[j][..., off:off + take])
            lo, n = lo + take, n - take
        return parts

    out = []
    for lo, n, _ in sorted(W_IN_PIECES, key=lambda p: p[2]):
        out += cols(lo, n)
    out.append(jnp.zeros(shards[0].shape[:-1] + (N_AL - N_IN,), shards[0].dtype))
    return jnp.concatenate(out, axis=-1)


def _shard_from_aligned(w_al, j):
    lo_j, hi_j = j * W_SHARD, (j + 1) * W_SHARD
    parts = []
    for lo, n, al in W_IN_PIECES:
        a, b = max(lo, lo_j), min(lo + n, hi_j)
        if a < b:
            parts.append(w_al[..., al + a - lo:al + b - lo])
    return jnp.concatenate(parts, axis=-1)


def kernel(x, mem, norm_g, w_in, b_fg, b_merge, conv_w, a_log, dt_bias, gdn_norm_g, mem_norm_g, w_mem_kv, w_branch, w_out, final_norm_g, loss_target, m_norm_g, m_w_in, m_b_fg, m_b_merge, m_conv_w, m_a_log, m_dt_bias, m_gdn_norm_g, m_mem_norm_g, m_w_mem_kv, m_w_branch, m_w_out, m_final_norm_g, v_norm_g, v_w_in, v_b_fg, v_b_merge, v_conv_w, v_a_log, v_dt_bias, v_gdn_norm_g, v_mem_norm_g, v_w_mem_kv, v_w_branch, v_w_out, v_final_norm_g):
    wts = dict(norm_g=norm_g, w_in=w_in, b_fg=b_fg, b_merge=b_merge, conv_w=conv_w, a_log=a_log,
               dt_bias=dt_bias, gdn_norm_g=gdn_norm_g, mem_norm_g=mem_norm_g, w_mem_kv=w_mem_kv,
               w_branch=w_branch, w_out=w_out, final_norm_g=final_norm_g)
    mom = dict(norm_g=m_norm_g, w_in=m_w_in, b_fg=m_b_fg, b_merge=m_b_merge, conv_w=m_conv_w,
               a_log=m_a_log, dt_bias=m_dt_bias, gdn_norm_g=m_gdn_norm_g, mem_norm_g=m_mem_norm_g,
               w_mem_kv=m_w_mem_kv, w_branch=m_w_branch, w_out=m_w_out, final_norm_g=m_final_norm_g)
    vel = dict(norm_g=v_norm_g, w_in=v_w_in, b_fg=v_b_fg, b_merge=v_b_merge, conv_w=v_conv_w,
               a_log=v_a_log, dt_bias=v_dt_bias, gdn_norm_g=v_gdn_norm_g, mem_norm_g=v_mem_norm_g,
               w_mem_kv=v_w_mem_kv, w_branch=v_w_branch, w_out=v_w_out, final_norm_g=v_final_norm_g)

    big = ("w_in", "w_mem_kv", "w_branch", "w_out")
    gathered = _gather_chips("weight_gather", [wts[n].astype(BF16) for n in big] + [conv_w])
    all_w = dict(zip(big + ("conv_w",), gathered))
    w_in_al = _aligned_from_shards([all_w["w_in"][j] for j in range(N_CHIPS)])

    layers = []
    for l in range(DEPTH):
        rows_of = lambda n: all_w[n][:, l].reshape(D_MODEL, D_MODEL)
        last_of = lambda n: jnp.concatenate([all_w[n][j, l] for j in range(N_CHIPS)], axis=-1)
        layers.append(dict(norm_g=norm_g[l], w_in_al=w_in_al[l], b_fg=b_fg[l], b_merge=b_merge[l],
                           conv_w=jnp.pad(last_of("conv_w"), ((0, 4), (0, 0))), a_log=a_log[l],
                           dt_bias=dt_bias[l], gdn_norm_g=gdn_norm_g[l], mem_norm_g=mem_norm_g[l],
                           w_mem_kv=rows_of("w_mem_kv"), w_branch=last_of("w_branch"),
                           w_out=rows_of("w_out")))

    loss_lanes, dx, grads, dgf = _local_step(x[0], mem[0], layers, final_norm_g, loss_target[0])

    gfull = {n: jnp.stack([grads[l][n] for l in range(DEPTH)])
             for n in ("norm_g", "b_fg", "b_merge", "conv_w", "a_log", "dt_bias", "gdn_norm_g",
                       "mem_norm_g", "w_mem_kv", "w_branch", "w_out")}
    gfull["final_norm_g"] = dgf
    loss_local = jnp.sum(loss_lanes).reshape(1)
    small_g = [gfull[n] for n in SMALL] + [loss_local]
    dw_al = jnp.stack([grads[l]["w_in_al"] for l in range(DEPTH)])
    ga = jnp.stack([_shard_from_aligned(dw_al, j) for j in range(N_CHIPS)])
    gb = jnp.stack([_pack([_shard_of(gfull[n], n, j) for n in SLAB] + small_g, PACK_ROWS)
                    for j in range(N_CHIPS)])
    R = gb.shape[1]
    gb = gb.reshape(N_CHIPS, 2, R // 2, PACK_COLS)

    mc = lax.axis_index("c")
    tr = 256
    from_sibling = _sibling_swap([ga, gb])
    mine = [lax.dynamic_index_in_dim(g, mc, axis=1, keepdims=False) for g in (ga, gb)]
    pair = [_add_pairs(a, b, tr) for a, b in zip(mine, from_sibling)]
    slots = _chip_exchange(pair)
    half = [_sum_slots(s, tr) for s in slots]
    ga_sum, gb_sum = _sibling_gather(half)
    gb_sum = gb_sum.reshape(R, PACK_COLS)

    zero1 = jnp.zeros((1,), F32)
    slab = lambda d: _pack([d[n] for n in SLAB] + [d[n] for n in SMALL] + [zero1], PACK_ROWS)
    delta_s, m_s, v_s = _adamw(slab(wts), gb_sum, slab(mom), slab(vel), tr)
    flat_in = lambda a: a.reshape(DEPTH * D_MODEL, W_SHARD)
    in_res = _adamw(flat_in(w_in), flat_in(ga_sum), flat_in(m_w_in), flat_in(v_w_in), tr)

    names = list(SLAB) + list(SMALL)
    shapes = [wts[n].shape for n in names] + [(1,)]
    g_un = dict(zip(names + ["loss"], _unpack(gb_sum, shapes)))
    d_un = dict(zip(names, _unpack(delta_s, shapes[:-1])))
    m_un = dict(zip(names, _unpack(m_s, shapes[:-1])))
    v_un = dict(zip(names, _unpack(v_s, shapes[:-1])))
    g_un["w_in"] = ga_sum
    d_un["w_in"], m_un["w_in"], v_un["w_in"] = [r.reshape(w_in.shape) for r in in_res]

    loss = g_un["loss"][0]
    return (loss, dx[None], *[g_un[n] for n in ALL_WEIGHTS], *[d_un[n] for n in ALL_WEIGHTS],
            *[m_un[n] for n in ALL_WEIGHTS], *[v_un[n] for n in ALL_WEIGHTS])
```

```python
import functools

import jax
import jax.numpy as jnp
from jax import lax
from jax.experimental import pallas as pl
from jax.experimental.pallas import tpu as pltpu

F32 = jnp.float32
BF16 = jnp.bfloat16
HIGHEST = lax.Precision.HIGHEST
MESH_ID = pl.DeviceIdType.MESH

D_MODEL = 1024
DEPTH = 2
CHUNK = 64
EPS = 1e-6
FOX_HEADS, FOX_DIM = 8, 64
GDN_HEADS, GDN_DIM = 4, 128
MEM_HEADS, MEM_DIM = 4, 128
WIDTH = 512
N_BRANCH = 3
N_IN = 8208
N_AL = 8320
N_CHIPS = 4
NEG = -1e30
LOG2E = 1.4426950408889634
LN2 = 0.6931471805599453

ADAM_LR, ADAM_B1, ADAM_B2, ADAM_EPS, ADAM_WD, ADAM_STEP = 0.001, 0.9, 0.999, 1e-08, 0.01, 10

CB_GATES = 0
CB_BQKV = 2
CB_AQ, CB_AK, CB_AV, CB_AZ, CB_BZ, CB_MQ, CB_MZ = 9, 10, 11, 12, 13, 14, 15
CB_SMALL = 64
LANE_AF, LANE_BA, LANE_BB = 0, 8, 12

NN = ((1,), (0,))
NT = ((1,), (1,))
TN = ((0,), (0,))

VMEM_LIMIT_BYTES = 56 * 1024 * 1024


def _dot(a, b, dims=NN, prec=None):
    return lax.dot_general(a, b, (dims, ((), ())), preferred_element_type=F32, precision=prec)


def _bdot(a, b, ca, cb, prec=None):
    return lax.dot_general(a, b, (((ca,), (cb,)), ((0,), (0,))), preferred_element_type=F32,
                           precision=prec)


def _b16(a):
    return a.astype(BF16)


def _eye(n, dtype=F32):
    r = lax.broadcasted_iota(jnp.int32, (n, n), 0)
    c = lax.broadcasted_iota(jnp.int32, (n, n), 1)
    return jnp.where(r == c, 1.0, 0.0).astype(dtype)


def _transpose_exact(x):
    return _dot(_eye(x.shape[1]), x, NT, HIGHEST)


def _col_to_row(col):
    n = col.shape[0]
    return jnp.sum(jnp.where(_eye(n) > 0.5, col, 0.0), axis=0, keepdims=True)


def _row_to_col(row):
    n = row.shape[1]
    return jnp.sum(jnp.where(_eye(n) > 0.5, row, 0.0), axis=1, keepdims=True)


def _sigmoid(x):
    return 1.0 / (1.0 + jnp.exp(-x))


def _softplus(x):
    return jnp.maximum(x, 0.0) + jnp.log(1.0 + jnp.exp(-jnp.abs(x)))


def _silu_and_grad(x):
    s = _sigmoid(x)
    return x * s, s * (1.0 + x * (1.0 - s))


def _params(semantics):
    return pltpu.CompilerParams(dimension_semantics=semantics, vmem_limit_bytes=VMEM_LIMIT_BYTES)


def _rows(a, ts):
    nd = a.ndim
    return (a, (ts,) + a.shape[1:], lambda i, nd=nd: (i,) + (0,) * (nd - 1))


def _cols(a, ts, width, cb):
    return (a, (ts, width), lambda i, cb=cb: (i, cb))


def _full(a):
    nd = a.ndim
    return (a, a.shape, lambda i, nd=nd: (0,) * nd)


def _orow(S, tail, dtype, ts):
    nd = 1 + len(tail)
    return ((S,) + tuple(tail), dtype, (ts,) + tuple(tail), lambda i, nd=nd: (i,) + (0,) * (nd - 1))


def _oacc(shape, dtype):
    nd = len(shape)
    return (tuple(shape), dtype, tuple(shape), lambda i, nd=nd: (0,) * nd)


def _tiled(name, body, n_steps, ins, outs, scratch=(), reverse=False):
    def rev(imap):
        if not reverse:
            return imap
        return lambda i: imap(n_steps - 1 - i)

    in_specs = [pl.BlockSpec(blk, rev(imap)) for (_, blk, imap) in ins]
    out_specs = [pl.BlockSpec(blk, rev(imap)) for (_, _, blk, imap) in outs]
    out_shape = [jax.ShapeDtypeStruct(shape, dt) for (shape, dt, _, _) in outs]
    n_in, n_out = len(ins), len(outs)

    def kern(*refs):
        step = pl.program_id(0)
        t = (n_steps - 1 - step) if reverse else step
        body(t, step == 0, refs[:n_in], refs[n_in:n_in + n_out], refs[n_in + n_out:])

    res = pl.pallas_call(
        kern, name=name, grid=(n_steps,), in_specs=in_specs, out_specs=out_specs,
        out_shape=out_shape, scratch_shapes=list(scratch),
        compiler_params=_params(("arbitrary",)),
    )(*[a for (a, _, _) in ins])
    return res


def _pick(n, pref):
    if n <= pref:
        return n
    best = None
    for t in range(128, pref + 1, 128):
        if n % t == 0:
            best = t
    assert best is not None, (n, pref)
    return best


def _mm(name, a, b, ta=False, tb=False, out_dtype=F32, tm=1024, tn=1024, tk=1024):
    if ta:
        K, M = a.shape
    else:
        M, K = a.shape
    if tb:
        N, K2 = b.shape
    else:
        K2, N = b.shape
    assert K == K2, (a.shape, b.shape, ta, tb)
    tm, tn, tk = _pick(M, tm), _pick(N, tn), _pick(K, tk)
    nk = K // tk
    a_spec = (pl.BlockSpec((tk, tm), lambda i, j, k: (k, i)) if ta
              else pl.BlockSpec((tm, tk), lambda i, j, k: (i, k)))
    b_spec = (pl.BlockSpec((tn, tk), lambda i, j, k: (j, k)) if tb
              else pl.BlockSpec((tk, tn), lambda i, j, k: (k, j)))
    dims = ((0,) if ta else (1,), (1,) if tb else (0,))

    def kern_single(a_ref, b_ref, o_ref):
        o_ref[...] = _dot(_b16(a_ref[...]), _b16(b_ref[...]), dims).astype(o_ref.dtype)

    def kern_acc(a_ref, b_ref, o_ref, acc_ref):
        k = pl.program_id(2)

        @pl.when(k == 0)
        def _():
            acc_ref[...] = jnp.zeros_like(acc_ref)

        acc_ref[...] += _dot(_b16(a_ref[...]), _b16(b_ref[...]), dims)

        @pl.when(k == nk - 1)
        def _():
            o_ref[...] = acc_ref[...].astype(o_ref.dtype)

    return pl.pallas_call(
        kern_single if nk == 1 else kern_acc, name=name, grid=(M // tm, N // tn, nk),
        in_specs=[a_spec, b_spec],
        out_specs=pl.BlockSpec((tm, tn), lambda i, j, k: (i, j)),
        out_shape=jax.ShapeDtypeStruct((M, N), out_dtype),
        scratch_shapes=[] if nk == 1 else [pltpu.VMEM((tm, tn), F32)],
        compiler_params=_params(("parallel", "parallel", "arbitrary")),
    )(a, b)


def _rms_fwd(name, x, g, ts):
    S, D = x.shape

    def body(t, first, ins, outs, scratch):
        x_ref, g_ref = ins
        h_ref, r_ref = outs
        xv = x_ref[...]
        r = lax.rsqrt(jnp.mean(xv * xv, axis=1, keepdims=True) + EPS)
        h_ref[...] = (xv * r * g_ref[...]).astype(h_ref.dtype)
        r_ref[...] = r

    return _tiled(name, body, S // ts, [_rows(x, ts), _full(g.reshape(1, D))],
                  [_orow(S, (D,), BF16, ts), _orow(S, (1,), F32, ts)])


def _rms_bwd(name, dh, x, rstd, g, dres, ts):
    S, D = x.shape

    def body(t, first, ins, outs, scratch):
        dh_ref, x_ref, r_ref, g_ref, dres_ref = ins
        dx_ref, dg_ref = outs
        r = r_ref[...]
        xh = x_ref[...] * r
        dhv = dh_ref[...]
        dxh = dhv * g_ref[...]
        dx_ref[...] = dres_ref[...] + r * (dxh - xh * jnp.mean(dxh * xh, axis=1, keepdims=True))

        @pl.when(first)
        def _():
            dg_ref[...] = jnp.zeros_like(dg_ref)

        dg_ref[0:1, :] += jnp.sum(dhv * xh, axis=0, keepdims=True)

    dx, dg = _tiled(name, body, S // ts,
                    [_rows(dh, ts), _rows(x, ts), _rows(rstd, ts), _full(g.reshape(1, D)), _rows(dres, ts)],
                    [_orow(S, (D,), F32, ts), _oacc((8, D), F32)])
    return dx, dg[0]


def _loss_head(x, g, target, ts):
    S, D = x.shape

    def body(t, first, ins, outs, scratch):
        x_ref, g_ref, tgt_ref = ins
        dx_ref, dg_ref, loss_ref = outs
        xv = x_ref[...]
        gv = g_ref[...]
        r = lax.rsqrt(jnp.mean(xv * xv, axis=1, keepdims=True) + EPS)
        xh = xv * r
        err = xh * gv - tgt_ref[...]
        dy = err * (1.0 / D)
        dxh = dy * gv
        dx_ref[...] = r * (dxh - xh * jnp.mean(dxh * xh, axis=1, keepdims=True))

        @pl.when(first)
        def _():
            dg_ref[...] = jnp.zeros_like(dg_ref)
            loss_ref[...] = jnp.zeros_like(loss_ref)

        dg_ref[0:1, :] += jnp.sum(dy * xh, axis=0, keepdims=True)
        per_lane = jnp.sum(err * err, axis=0, keepdims=True)
        loss_ref[0:1, :] += per_lane * (0.5 / D)

    dx, dg, loss = _tiled("loss_head", body, S // ts,
                          [_rows(x, ts), _full(g.reshape(1, D)), _rows(target, ts)],
                          [_orow(S, (D,), F32, ts), _oacc((8, D), F32), _oacc((8, D), F32)])
    return dx, dg[0], loss[0]


def _scan_rows(x, length, seg, reverse=False):
    row = lax.broadcasted_iota(jnp.int32, x.shape, 0) % seg
    k = 1
    while k < seg:
        if reverse:
            x = x + jnp.where(row < seg - k, pltpu.roll(x, length - k, 0), 0.0)
        else:
            x = x + jnp.where(row >= k, pltpu.roll(x, k, 0), 0.0)
        k *= 2
    return x


def _fox_decay(z, b_fg128, ts):
    S = z.shape[0]

    def body(t, first, ins, outs, scratch):
        zs_ref, b_ref = ins
        f_ref, hi_ref, mid_ref, lo_ref = outs
        (carry,) = scratch

        @pl.when(first)
        def _():
            carry[...] = jnp.zeros_like(carry)

        logf = -_softplus(-(zs_ref[...] + b_ref[...]))
        run = _scan_rows(logf, ts, ts) + carry[0:1, :]
        carry[0:1, :] = run[ts - 1:ts, :]
        f_ref[...] = _transpose_exact(run)
        f2 = run * LOG2E
        hi = f2.astype(BF16)
        r1 = f2 - hi.astype(F32)
        mid = r1.astype(BF16)
        lo = (r1 - mid.astype(F32)).astype(BF16)
        eye = _eye(128, BF16)
        hi_ref[...] = _dot(eye, hi, NT).astype(BF16)
        mid_ref[...] = _dot(eye, mid, NT).astype(BF16)
        lo_ref[...] = _dot(eye, lo, NT).astype(BF16)

    tcol = lambda dt: ((128, S), dt, (128, ts), lambda i: (0, i))
    return _tiled("fox_decay", body, S // ts,
                  [_cols(z, ts, 128, CB_SMALL), _full(b_fg128)],
                  [tcol(F32), tcol(BF16), tcol(BF16), tcol(BF16)], scratch=[pltpu.VMEM((8, 128), F32)])


def _fox_decay_bwd(dfk_rows, dfq_rows, z, b_fg128, ts):
    S = z.shape[0]
    H = dfk_rows.shape[0]

    def body(t, first, ins, outs, scratch):
        dfk_ref, dfq_ref, zs_ref, b_ref = ins
        daf_ref, db_ref = outs
        (carry,) = scratch

        @pl.when(first)
        def _():
            carry[...] = jnp.zeros_like(carry)
            db_ref[...] = jnp.zeros_like(db_ref)

        r = lax.broadcasted_iota(jnp.int32, (H, 128), 0)
        c = lax.broadcasted_iota(jnp.int32, (H, 128), 1)
        place = jnp.where(r == c, 1.0, 0.0)
        df = _dot(dfk_ref[...] + dfq_ref[...], place, TN, HIGHEST)
        run = _scan_rows(df, ts, ts, reverse=True) + carry[0:1, :]
        carry[0:1, :] = run[0:1, :]
        daf = run * _sigmoid(-(zs_ref[...] + b_ref[...]))
        daf_ref[...] = daf
        db_ref[0:1, :] += jnp.sum(daf, axis=0, keepdims=True)

    rowsin = lambda a: (a, (H, ts), lambda i: (0, i))
    daf, db = _tiled("fox_decay_bwd", body, S // ts,
                     [rowsin(dfk_rows), rowsin(dfq_rows), _cols(z, ts, 128, CB_SMALL), _full(b_fg128)],
                     [_orow(S, (128,), F32, ts), _oacc((8, 128), F32)],
                     scratch=[pltpu.VMEM((8, 128), F32)], reverse=True)
    return daf, db[0]


FOX_AUG = 80


def _fox_fwd(q_aug, kT_aug, v_aug, tq):
    H, S, da = q_aug.shape
    dv = v_aug.shape[2]
    d = FOX_DIM
    tk = tq // 2
    qscale = (d ** -0.5) * LOG2E

    def kern(q_ref, kT_ref, v_ref, o_ref, lse_ref, qs_ref, s_buf, p_buf, m_scr, acc_scr):
        i = pl.program_id(1)
        col = lax.broadcasted_iota(jnp.int32, (1, da), 1)
        qb = _b16(q_ref[...] * jnp.where(col < d, qscale, 1.0))
        qs_ref[...] = qb

        def keys(t):
            return pl.ds(pl.multiple_of(t * tk, tk), tk)

        def stage(t, slot, mask_off, look_ahead):
            if look_ahead:
                s_buf[1 - slot] = _dot(qb, kT_ref[:, keys(t + 1)])
            pv = _dot(p_buf[1 - slot], v_ref[keys(jnp.maximum(t - 1, 0)), :])

            def scores():
                s = s_buf[slot]
                if mask_off is None:
                    return s
                r = lax.broadcasted_iota(jnp.int32, (tq, tk), 0)
                c = lax.broadcasted_iota(jnp.int32, (tq, tk), 1)
                return jnp.where(c + mask_off <= r, s, NEG)

            m = m_scr[...]
            m_new = jnp.maximum(m, jnp.max(scores(), axis=1, keepdims=True))
            alpha = jnp.exp2(m - m_new)
            p_buf[slot] = _b16(jnp.exp2(scores() - m_new))
            m_scr[...] = m_new
            acc_scr[...] = (acc_scr[...] + pv) * alpha

        s_buf[0] = _dot(qb, kT_ref[:, keys(0)])
        p_buf[1] = jnp.zeros((tq, tk), BF16)
        m_scr[...] = jnp.full((tq, 1), NEG, F32)
        acc_scr[...] = jnp.zeros((tq, dv), F32)

        def pair(n, _):
            stage(2 * n, 0, None, True)
            stage(2 * n + 1, 1, None, True)
            return 0

        lax.fori_loop(0, i, pair, 0)
        stage(2 * i, 0, 0, True)
        stage(2 * i + 1, 1, tk, False)
        acc = acc_scr[...] + _dot(p_buf[1], v_ref[keys(2 * i + 1), :])
        l = acc[:, d:d + 1]
        o_ref[...] = acc[:, :d] / l
        lse_ref[...] = _col_to_row(m_scr[...] + jnp.log(l) * LOG2E)

    return pl.pallas_call(
        kern, name="fox_fwd", grid=(H, S // tq),
        in_specs=[pl.BlockSpec((None, tq, da), lambda h, i: (h, i, 0)),
                  pl.BlockSpec((None, da, S), lambda h, i: (h, 0, 0)),
                  pl.BlockSpec((None, S, dv), lambda h, i: (h, 0, 0))],
        out_specs=[pl.BlockSpec((None, tq, d), lambda h, i: (h, i, 0)),
                   pl.BlockSpec((None, 1, tq), lambda h, i: (h, 0, i)),
                   pl.BlockSpec((None, tq, da), lambda h, i: (h, i, 0))],
        out_shape=[jax.ShapeDtypeStruct((H, S, d), F32), jax.ShapeDtypeStruct((H, 1, S), F32),
                   jax.ShapeDtypeStruct((H, S, da), BF16)],
        scratch_shapes=[pltpu.VMEM((2, tq, tk), F32), pltpu.VMEM((2, tq, tk), BF16),
                        pltpu.VMEM((tq, 1), F32), pltpu.VMEM((tq, dv), F32)],
        compiler_params=_params(("parallel", "arbitrary")),
    )(q_aug, kT_aug, v_aug)


def _fox_bwd(qs, k_aug, kT, v, do, lse_row, delta_row, tq):
    H, S, da = qs.shape
    d = FOX_DIM
    tk = tq
    nq = S // tq
    scale = d ** -0.5

    ts2 = tq // 2
    last = 2 * nq - 1

    def kern(q_ref, k_ref, kT_ref, v_ref, do_ref, lse_ref, dl_ref,
             dqT_ref, dk_ref, dv_ref, dfk_ref, dfq_ref,
             kq_buf, dp_buf, pb_buf, ds_buf, dk_scr, dv_scr, dfk_scr):
        j = pl.program_id(1)

        @pl.when(j == 0)
        def _():
            dqT_ref[...] = jnp.zeros_like(dqT_ref)
            dfq_ref[...] = jnp.zeros_like(dfq_ref)

        kb = k_ref[...]
        kTb = kT_ref[...]
        vb = v_ref[...]
        dk_scr[...] = jnp.zeros_like(dk_scr)
        dv_scr[...] = jnp.zeros_like(dv_scr)
        dfk_scr[...] = jnp.zeros_like(dfk_scr)

        def queries(t):
            return pl.ds(pl.multiple_of(t * ts2, ts2), ts2)

        def products(t, slot):
            rows = queries(t)
            kq_buf[slot] = _dot(kb, q_ref[rows, :], NT)
            dp_buf[slot] = _dot(vb, do_ref[rows, :], NT)

        def pointwise(t, slot, mask_off):
            rows = queries(t)
            sT = kq_buf[slot]
            if mask_off is not None:
                r = lax.broadcasted_iota(jnp.int32, (tk, ts2), 0)
                c = lax.broadcasted_iota(jnp.int32, (tk, ts2), 1)
                sT = jnp.where(r <= c + mask_off, sT, NEG)
            pT = jnp.exp2(sT - lse_ref[:, rows])
            dsT = pT * (dp_buf[slot] - dl_ref[:, rows])
            pb_buf[slot] = _b16(pT)
            ds_buf[slot] = _b16(dsT)
            dfk_scr[...] -= jnp.sum(dsT, axis=1, keepdims=True)
            dfq_ref[:, rows] += jnp.sum(dsT, axis=0, keepdims=True)

        def accumulate(t, slot):
            rows = queries(t)
            dsb = ds_buf[slot]
            dv_scr[...] += _dot(pb_buf[slot], do_ref[rows, :])
            dk_scr[...] += _dot(dsb, q_ref[rows, :])
            dqT_ref[:, rows] += _dot(kTb, dsb) * scale

        def stage(t, slot, mask_off, has_prev):
            products(jnp.minimum(t + 1, last), 1 - slot)
            if has_prev:
                accumulate(t - 1, 1 - slot)
            pointwise(t, slot, mask_off)

        products(2 * j, 0)
        stage(2 * j, 0, 0, False)
        stage(2 * j + 1, 1, ts2, True)

        def pair(n, _):
            stage(2 * n, 0, None, True)
            stage(2 * n + 1, 1, None, True)
            return 0

        lax.fori_loop(j + 1, nq, pair, 0)
        accumulate(last, 1)
        dk_ref[...] = dk_scr[:, :d] * LN2
        dv_ref[...] = dv_scr[...]
        dfk_ref[...] = _col_to_row(dfk_scr[...])

    tile = lambda h, j: (h, j, 0)
    whole = lambda h, j: (h, 0, 0)
    rowtile = lambda h, j: (h, 0, j)
    return pl.pallas_call(
        kern, name="fox_bwd", grid=(H, S // tk),
        in_specs=[pl.BlockSpec((None, S, da), whole),
                  pl.BlockSpec((None, tk, da), tile),
                  pl.BlockSpec((None, d, tk), lambda h, j: (h, 0, j)),
                  pl.BlockSpec((None, tk, d), tile),
                  pl.BlockSpec((None, S, d), whole),
                  pl.BlockSpec((None, 1, S), whole),
                  pl.BlockSpec((None, 1, S), whole)],
        out_specs=[pl.BlockSpec((None, d, S), whole),
                   pl.BlockSpec((None, tk, d), tile),
                   pl.BlockSpec((None, tk, d), tile),
                   pl.BlockSpec((None, 1, tk), rowtile),
                   pl.BlockSpec((None, 1, S), whole)],
        out_shape=[jax.ShapeDtypeStruct((H, d, S), F32), jax.ShapeDtypeStruct((H, S, d), F32),
                   jax.ShapeDtypeStruct((H, S, d), F32), jax.ShapeDtypeStruct((H, 1, S), F32),
                   jax.ShapeDtypeStruct((H, 1, S), F32)],
        scratch_shapes=[pltpu.VMEM((2, tk, ts2), F32), pltpu.VMEM((2, tk, ts2), F32),
                        pltpu.VMEM((2, tk, ts2), BF16), pltpu.VMEM((2, tk, ts2), BF16),
                        pltpu.VMEM((tk, da), F32), pltpu.VMEM((tk, d), F32), pltpu.VMEM((tk, 1), F32)],
        compiler_params=_params(("parallel", "arbitrary")),
    )(qs, k_aug, kT, v, do, lse_row, delta_row)


def _heads_major(a, H, d):
    S = a.shape[0]
    return a.reshape(S, H, d).transpose(1, 0, 2)


def _heads_minor(a):
    H, S, d = a.shape
    return a.transpose(1, 0, 2).reshape(S, H * d)


def _lane_pick(x128, lane):
    return x128[:, lane:lane + 1]


def _l2_fwd(y):
    return lax.rsqrt(jnp.sum(y * y, axis=1, keepdims=True) + EPS)


def _gdn_prep(z, conv_w, a128, dt128, ts):
    S = z.shape[0]
    C3 = 3 * WIDTH
    hb = ts // 8

    def body(t, first, ins, outs, scratch):
        x_ref, halo_ref, zs_ref, w_ref, a_ref, dt_ref = ins
        qkv_ref, c_ref, gb_ref, gbT_ref = outs
        halo = jnp.where(t > 0, halo_ref[...], 0.0)
        xe = jnp.concatenate([halo, x_ref[...]], axis=0)
        w = w_ref[...]
        c = w[3:4, :] * xe[8:, :]
        for back in (1, 2, 3):
            c = c + w[3 - back:4 - back, :] * pltpu.roll(xe, back, 0)[8:, :]
        c_ref[...] = c
        y = c * _sigmoid(c)
        for h in range(GDN_HEADS):
            lo = h * GDN_DIM
            yq = y[:, lo:lo + GDN_DIM]
            qkv_ref[:, lo:lo + GDN_DIM] = yq * (_l2_fwd(yq) * (GDN_DIM ** -0.5))
            yk = y[:, WIDTH + lo:WIDTH + lo + GDN_DIM]
            qkv_ref[:, WIDTH + lo:WIDTH + lo + GDN_DIM] = yk * _l2_fwd(yk)
        qkv_ref[:, 2 * WIDTH:] = y[:, 2 * WIDTH:]
        zs = zs_ref[...]
        lane = lax.broadcasted_iota(jnp.int32, zs.shape, 1)
        g = -jnp.exp(a_ref[...]) * _softplus(zs + dt_ref[...])
        G = _scan_rows(g, ts, CHUNK)
        beta = _sigmoid(zs)
        out = jnp.where(lane < 8, pltpu.roll(g, 128 - LANE_BA, 1), jnp.where(lane < LANE_BB, G, beta))
        gb_ref[...] = out
        gbT_ref[...] = _transpose_exact(out)

    x_in = (z, (ts, C3), lambda i: (i, CB_BQKV))
    halo_in = (z, (8, C3), lambda i: (jnp.maximum(i * hb - 1, 0), CB_BQKV))
    return _tiled("gdn_prep", body, S // ts,
                  [x_in, halo_in, _cols(z, ts, 128, CB_SMALL), _full(conv_w), _full(a128), _full(dt128)],
                  [_orow(S, (C3,), F32, ts), _orow(S, (C3,), F32, ts), _orow(S, (128,), F32, ts),
                   ((128, S), F32, (128, ts), lambda i: (0, i))])


def _chunk_masks(nc):
    r = lax.broadcasted_iota(jnp.int32, (nc, CHUNK, CHUNK), 1)
    c = lax.broadcasted_iota(jnp.int32, (nc, CHUNK, CHUNK), 2)
    return c <= r, c < r, c == r


def _chunk_local(qh, kh, vh, Gc, Gr, beta):
    nc = qh.shape[0]
    incl, strict, _ = _chunk_masks(nc)
    gamma = jnp.exp(jnp.where(incl, Gc - Gr, NEG))
    kb = kh * beta
    P = _bdot(_b16(kb), _b16(kh), 2, 2)
    Qk = _bdot(_b16(qh), _b16(kh), 2, 2)
    eG = jnp.exp(Gc)
    Gl = Gc[:, CHUNK - 1:CHUNK, :]
    edec = jnp.exp(Gl - Gc)
    return incl, strict, gamma, kb, P, Qk, eG, edec


def _gdn_local_fwd(qkv, gb, grow, ts):
    S = qkv.shape[0]
    nc = ts // CHUNK

    def body(t, first, ins, outs, scratch):
        q_ref, k_ref, v_ref, gb_ref, gr_ref = ins
        u_ref, w_ref, qd_ref, kd_ref, aqk_ref, T_ref = outs
        gbv = gb_ref[...]
        for h in range(GDN_HEADS):
            lo = h * GDN_DIM
            qh = q_ref[:, lo:lo + GDN_DIM].reshape(nc, CHUNK, GDN_DIM)
            kh = k_ref[:, lo:lo + GDN_DIM].reshape(nc, CHUNK, GDN_DIM)
            vh = v_ref[:, lo:lo + GDN_DIM].reshape(nc, CHUNK, GDN_DIM)
            Gc = _lane_pick(gbv, LANE_BA + h).reshape(nc, CHUNK, 1)
            beta = _lane_pick(gbv, LANE_BB + h).reshape(nc, CHUNK, 1)
            Gr = gr_ref[h].reshape(nc, 1, CHUNK)
            incl, strict, gamma, kb, P, Qk, eG, edec = _chunk_local(qh, kh, vh, Gc, Gr, beta)
            A = jnp.where(strict, P * gamma, 0.0)
            _, _, eye = _chunk_masks(nc)
            T = jnp.where(eye, 1.0, 0.0) - A
            X = A
            for _ in range(5):
                X = _bdot(X, X, 2, 1, HIGHEST)
                T = T + _bdot(T, X, 2, 1, HIGHEST)
            u = _bdot(T, vh * beta, 2, 1, HIGHEST)
            w = _bdot(T, kb * eG, 2, 1, HIGHEST)
            u_ref[:, lo:lo + GDN_DIM] = u.reshape(ts, GDN_DIM)
            w_ref[:, lo:lo + GDN_DIM] = w.reshape(ts, GDN_DIM)
            qd_ref[:, lo:lo + GDN_DIM] = (qh * eG).reshape(ts, GDN_DIM)
            kd_ref[:, lo:lo + GDN_DIM] = (kh * edec).reshape(ts, GDN_DIM)
            aqk_ref[h] = jnp.where(incl, Qk * gamma, 0.0).reshape(ts, CHUNK)
            T_ref[h] = T.reshape(ts, CHUNK)

    wide = _orow(S, (WIDTH,), F32, ts)
    perhead = ((GDN_HEADS, S, CHUNK), F32, (GDN_HEADS, ts, CHUNK), lambda i: (0, i, 0))
    return _tiled("gdn_local_fwd", body, S // ts,
                  [_cols(qkv, ts, WIDTH, 0), _cols(qkv, ts, WIDTH, 1), _cols(qkv, ts, WIDTH, 2),
                   _rows(gb, ts), (grow, (GDN_HEADS, nc, CHUNK), lambda i: (0, i, 0))],
                  [wide, wide, wide, wide, perhead, perhead])


def _gdn_scan_fwd(u, w, qd, kd, aqk, gb, ts):
    S = u.shape[0]
    nc = ts // CHUNK
    N = S // CHUNK

    def body(t, first, ins, outs, scratch):
        u_ref, w_ref, qd_ref, kd_ref, aqk_ref, gb_ref = ins
        o_ref, vn_ref, st_ref = outs
        (state,) = scratch

        @pl.when(first)
        def _():
            state[...] = jnp.zeros_like(state)

        def chunk(c, _):
            r0 = pl.multiple_of(c * CHUNK, CHUNK)
            rows = pl.ds(r0, CHUNK)
            glast = gb_ref[pl.ds(r0 + CHUNK - 1, 1), :]
            for h in range(GDN_HEADS):
                lo = h * GDN_DIM
                cols = slice(lo, lo + GDN_DIM)
                Sh = state[h]
                st_ref[c, h] = Sh
                Sb = _b16(Sh)
                vn = u_ref[rows, cols] - _dot(_b16(w_ref[rows, cols]), Sb)
                vnb = _b16(vn)
                o = _dot(_b16(qd_ref[rows, cols]), Sb) + _dot(_b16(aqk_ref[h, rows, :]), vnb)
                egl = jnp.exp(glast[:, LANE_BA + h:LANE_BA + h + 1])
                state[h] = Sh * egl + _dot(_b16(kd_ref[rows, cols]), vnb, TN)
                o_ref[rows, cols] = o
                vn_ref[rows, cols] = vn
            return 0

        lax.fori_loop(0, nc, chunk, 0)

    wide_in = lambda a: _rows(a, ts)
    wide = _orow(S, (WIDTH,), F32, ts)
    states = ((N, GDN_HEADS, GDN_DIM, GDN_DIM), F32, (nc, GDN_HEADS, GDN_DIM, GDN_DIM),
              lambda i: (i, 0, 0, 0))
    return _tiled("gdn_scan_fwd", body, S // ts,
                  [wide_in(u), wide_in(w), wide_in(qd), wide_in(kd),
                   (aqk, (GDN_HEADS, ts, CHUNK), lambda i: (0, i, 0)), _rows(gb, ts)],
                  [wide, wide, states],
                  scratch=[pltpu.VMEM((GDN_HEADS, GDN_DIM, GDN_DIM), F32)])


def _gdn_scan_bwd(do, w, qd, kd, aqk, vn, states, gb, ts):
    S = do.shape[0]
    nc = ts // CHUNK
    N = S // CHUNK

    def body(t, first, ins, outs, scratch):
        do_ref, w_ref, qd_ref, kd_ref, aqk_ref, vn_ref, st_ref, gb_ref = ins
        du_ref, dw_ref, dqd_ref, dkd_ref, daqk_ref, dgl_ref = outs
        (dstate,) = scratch

        @pl.when(first)
        def _():
            dstate[...] = jnp.zeros_like(dstate)

        r = lax.broadcasted_iota(jnp.int32, (CHUNK, CHUNK), 0)
        cc = lax.broadcasted_iota(jnp.int32, (CHUNK, CHUNK), 1)
        incl = cc <= r
        lane = lax.broadcasted_iota(jnp.int32, (1, 128), 1)

        def chunk(k, _):
            c = nc - 1 - k
            r0 = pl.multiple_of(c * CHUNK, CHUNK)
            rows = pl.ds(r0, CHUNK)
            glast = gb_ref[pl.ds(r0 + CHUNK - 1, 1), :]
            dgl_row = jnp.zeros((1, 128), F32)
            for h in range(GDN_HEADS):
                lo = h * GDN_DIM
                cols = slice(lo, lo + GDN_DIM)
                Sh = st_ref[c, h]
                Sb = _b16(Sh)
                dS = dstate[h]
                dSb = _b16(dS)
                dob = _b16(do_ref[rows, cols])
                aqkb = _b16(aqk_ref[h, rows, :])
                vnb = _b16(vn_ref[rows, cols])
                kdb = _b16(kd_ref[rows, cols])
                dvn = _dot(aqkb, dob, TN) + _dot(kdb, dSb)
                dvnb = _b16(dvn)
                daqk_ref[h, rows, :] = jnp.where(incl, _dot(dob, vnb, NT), 0.0)
                dqd_ref[rows, cols] = _dot(dob, Sb, NT)
                dkd_ref[rows, cols] = _dot(vnb, dSb, NT)
                dw_ref[rows, cols] = -_dot(dvnb, Sb, NT)
                du_ref[rows, cols] = dvn
                egl = jnp.exp(glast[:, LANE_BA + h:LANE_BA + h + 1])
                dgl = egl * jnp.sum(jnp.sum(dS * Sh, axis=1, keepdims=True), axis=0, keepdims=True)
                dgl_row = jnp.where(lane == h, dgl, dgl_row)
                dstate[h] = (_dot(_b16(qd_ref[rows, cols]), dob, TN) + egl * dS
                             - _dot(_b16(w_ref[rows, cols]), dvnb, TN))
            dgl_ref[pl.ds(c, 1), :] = dgl_row
            return 0

        lax.fori_loop(0, nc, chunk, 0)

    wide_in = lambda a: _rows(a, ts)
    wide = _orow(S, (WIDTH,), F32, ts)
    perhead_in = lambda a: (a, (GDN_HEADS, ts, CHUNK), lambda i: (0, i, 0))
    perhead = ((GDN_HEADS, S, CHUNK), F32, (GDN_HEADS, ts, CHUNK), lambda i: (0, i, 0))
    return _tiled("gdn_scan_bwd", body, S // ts,
                  [wide_in(do), wide_in(w), wide_in(qd), wide_in(kd), perhead_in(aqk), wide_in(vn),
                   (states, (nc, GDN_HEADS, GDN_DIM, GDN_DIM), lambda i: (i, 0, 0, 0)), _rows(gb, ts)],
                  [wide, wide, wide, wide, perhead, ((N, 128), F32, (nc, 128), lambda i: (i, 0))],
                  scratch=[pltpu.VMEM((GDN_HEADS, GDN_DIM, GDN_DIM), F32)], reverse=True)


def _gdn_local_bwd(qkv, gb, grow, T, du, dw, dqd, dkd, daqk, dgl, ts):
    S = qkv.shape[0]
    nc = ts // CHUNK

    def body(t, first, ins, outs, scratch):
        (q_ref, k_ref, v_ref, gb_ref, gr_ref, T_ref, du_ref, dw_ref, dqd_ref, dkd_ref,
         daqk_ref, dgl_ref) = ins
        dqkv_ref, dgb_ref = outs
        gbv = gb_ref[...]
        dglv = dgl_ref[...]
        lane = lax.broadcasted_iota(jnp.int32, (ts, 128), 1)
        dG_all = jnp.zeros((ts, 128), F32)
        dbeta_all = jnp.zeros((ts, 128), F32)
        for h in range(GDN_HEADS):
            lo = h * GDN_DIM
            cols = slice(lo, lo + GDN_DIM)
            r3 = lambda ref: ref[:, cols].reshape(nc, CHUNK, GDN_DIM)
            qh, kh, vh = r3(q_ref), r3(k_ref), r3(v_ref)
            duh, dwh, dqdh, dkdh = r3(du_ref), r3(dw_ref), r3(dqd_ref), r3(dkd_ref)
            Gc = _lane_pick(gbv, LANE_BA + h).reshape(nc, CHUNK, 1)
            beta = _lane_pick(gbv, LANE_BB + h).reshape(nc, CHUNK, 1)
            Gr = gr_ref[h].reshape(nc, 1, CHUNK)
            Th = T_ref[h].reshape(nc, CHUNK, CHUNK)
            daq = daqk_ref[h].reshape(nc, CHUNK, CHUNK)
            incl, strict, gamma, kb, P, Qk, eG, edec = _chunk_local(qh, kh, vh, Gc, Gr, beta)
            _, _, eye = _chunk_masks(nc)
            vb = vh * beta
            kbg = kb * eG
            dvb = _bdot(Th, duh, 1, 1, HIGHEST)
            dkbg = _bdot(Th, dwh, 1, 1, HIGHEST)
            dT = _bdot(duh, vb, 2, 2, HIGHEST) + _bdot(dwh, kbg, 2, 2, HIGHEST)
            M1 = _bdot(Th, dT, 1, 1, HIGHEST)
            dA = jnp.where(strict, -_bdot(M1, Th, 2, 2, HIGHEST), 0.0)
            dP = dA * gamma
            dQ = daq * gamma
            dgam = (dA * P + daq * Qk) * gamma
            dPb, dQb = _b16(dP), _b16(dQ)
            khb, qhb, kbb = _b16(kh), _b16(qh), _b16(kb)
            dq = _bdot(dQb, khb, 2, 1) + dqdh * eG
            dkb = _bdot(dPb, khb, 2, 1) + dkbg * eG
            dk = (_bdot(dQb, qhb, 1, 1) + _bdot(dPb, kbb, 1, 1) + dkdh * edec + dkb * beta)
            dbeta = (jnp.sum(dkb * kh, axis=2, keepdims=True) + jnp.sum(dvb * vh, axis=2, keepdims=True))
            dv = dvb * beta
            col_as_col = jnp.sum(jnp.where(eye, jnp.sum(dgam, axis=1, keepdims=True), 0.0),
                                 axis=2, keepdims=True)
            kd_term = jnp.sum(dkdh * kh * edec, axis=2, keepdims=True)
            dG = (jnp.sum(dgam, axis=2, keepdims=True) - col_as_col
                  + jnp.sum(dqdh * qh * eG, axis=2, keepdims=True)
                  + jnp.sum(dkbg * kbg, axis=2, keepdims=True) - kd_term)
            dgl_h = dglv[:, h:h + 1].reshape(nc, 1, 1) + jnp.sum(kd_term, axis=1, keepdims=True)
            last = lax.broadcasted_iota(jnp.int32, (nc, CHUNK, 1), 1) == CHUNK - 1
            dG = dG + jnp.where(last, dgl_h, 0.0)
            dqkv_ref[:, cols] = dq.reshape(ts, GDN_DIM)
            dqkv_ref[:, WIDTH + lo:WIDTH + lo + GDN_DIM] = dk.reshape(ts, GDN_DIM)
            dqkv_ref[:, 2 * WIDTH + lo:2 * WIDTH + lo + GDN_DIM] = dv.reshape(ts, GDN_DIM)
            dG_all = jnp.where(lane == LANE_BA + h, dG.reshape(ts, 1), dG_all)
            dbeta_all = jnp.where(lane == LANE_BB + h, dbeta.reshape(ts, 1), dbeta_all)
        dg_all = _scan_rows(dG_all, ts, CHUNK, reverse=True)
        dgb_ref[...] = jnp.where(lane < LANE_BB, dg_all, dbeta_all)

    wide_in = lambda a: _rows(a, ts)
    perhead_in = lambda a: (a, (GDN_HEADS, ts, CHUNK), lambda i: (0, i, 0))
    return _tiled("gdn_local_bwd", body, S // ts,
                  [_cols(qkv, ts, WIDTH, 0), _cols(qkv, ts, WIDTH, 1), _cols(qkv, ts, WIDTH, 2),
                   _rows(gb, ts), (grow, (GDN_HEADS, nc, CHUNK), lambda i: (0, i, 0)), perhead_in(T),
                   wide_in(du), wide_in(dw), wide_in(dqd), wide_in(dkd), perhead_in(daqk),
                   (dgl, (nc, 128), lambda i: (i, 0))],
                  [_orow(S, (3 * WIDTH,), F32, ts), _orow(S, (128,), F32, ts)])


def _gdn_prep_bwd(dqkv, dgb, cpre, z, conv_w, a128, dt128, ts):
    S = z.shape[0]
    C3 = 3 * WIDTH
    hb = ts // 8
    n_tiles = S // ts

    def dpre(dq, c):
        y, dsil = _silu_and_grad(c)
        parts = []
        for h in range(GDN_HEADS):
            lo = h * GDN_DIM
            yq = y[:, lo:lo + GDN_DIM]
            rq = _l2_fwd(yq)
            nq = yq * rq
            dn = dq[:, lo:lo + GDN_DIM] * (GDN_DIM ** -0.5)
            parts.append(rq * (dn - nq * jnp.sum(dn * nq, axis=1, keepdims=True)))
        for h in range(GDN_HEADS):
            lo = WIDTH + h * GDN_DIM
            yk = y[:, lo:lo + GDN_DIM]
            rk = _l2_fwd(yk)
            nk = yk * rk
            dn = dq[:, lo:lo + GDN_DIM]
            parts.append(rk * (dn - nk * jnp.sum(dn * nk, axis=1, keepdims=True)))
        parts.append(dq[:, 2 * WIDTH:])
        return jnp.concatenate(parts, axis=1) * dsil

    def body(t, first, ins, outs, scratch):
        (dq_ref, dqn_ref, c_ref, cn_ref, x_ref, xp_ref, zs_ref, dgb_ref, w_ref, a_ref, dt_ref) = ins
        dx_ref, dzs_ref, dw_ref, dad_ref = outs

        @pl.when(first)
        def _():
            dw_ref[...] = jnp.zeros_like(dw_ref)
            dad_ref[...] = jnp.zeros_like(dad_ref)

        dc = dpre(dq_ref[...], c_ref[...])
        dcn = jnp.where(t < n_tiles - 1, dpre(dqn_ref[...], cn_ref[...]), 0.0)
        dce = jnp.concatenate([dc, dcn], axis=0)
        w = w_ref[...]
        dx = w[3:4, :] * dc
        for back in (1, 2, 3):
            dx = dx + w[3 - back:4 - back, :] * pltpu.roll(dce, ts + 8 - back, 0)[:ts, :]
        dx_ref[...] = dx
        halo = jnp.where(t > 0, xp_ref[...], 0.0)
        xe = jnp.concatenate([halo, x_ref[...]], axis=0)
        dw_ref[3:4, :] += jnp.sum(dc * xe[8:, :], axis=0, keepdims=True)
        for back in (1, 2, 3):
            dw_ref[3 - back:4 - back, :] += jnp.sum(dc * pltpu.roll(xe, back, 0)[8:, :], axis=0,
                                                     keepdims=True)
        zs = zs_ref[...]
        dgb = dgb_ref[...]
        lane = lax.broadcasted_iota(jnp.int32, zs.shape, 1)
        arg = zs + dt_ref[...]
        nega = -jnp.exp(a_ref[...])
        dba = dgb * nega * _sigmoid(arg)
        beta = _sigmoid(zs)
        dbb = dgb * beta * (1.0 - beta)
        dzs_ref[...] = jnp.where((lane >= LANE_BA) & (lane < LANE_BB), dba,
                                 jnp.where((lane >= LANE_BB) & (lane < LANE_BB + 4), dbb, 0.0))
        dad_ref[0:1, :] += jnp.sum(dgb * nega * _softplus(arg), axis=0, keepdims=True)
        dad_ref[1:2, :] += jnp.sum(dba, axis=0, keepdims=True)

    nxt = lambda i: (jnp.minimum((i + 1) * hb, S // 8 - 1), 0)
    prv = lambda i: (jnp.maximum(i * hb - 1, 0), CB_BQKV)
    return _tiled("gdn_prep_bwd", body, n_tiles,
                  [_rows(dqkv, ts), (dqkv, (8, C3), nxt), _rows(cpre, ts), (cpre, (8, C3), nxt),
                   (z, (ts, C3), lambda i: (i, CB_BQKV)), (z, (8, C3), prv),
                   _cols(z, ts, 128, CB_SMALL), _rows(dgb, ts), _full(conv_w), _full(a128), _full(dt128)],
                  [_orow(S, (C3,), F32, ts), _orow(S, (128,), F32, ts), _oacc((8, C3), F32),
                   _oacc((8, 128), F32)])


def _mem_attn_fwd(z, mk, mv, ts):
    S = z.shape[0]

    def body(t, first, ins, outs, scratch):
        q_ref, mk_ref, mv_ref = ins
        (o_ref,) = outs
        for h in range(MEM_HEADS):
            cols = slice(h * MEM_DIM, (h + 1) * MEM_DIM)
            s = _dot(_b16(q_ref[:, cols]), _b16(mk_ref[:, cols]), NT) * (MEM_DIM ** -0.5)
            m = jnp.max(s, axis=1, keepdims=True)
            p = jnp.exp(s - m)
            p = p / jnp.sum(p, axis=1, keepdims=True)
            o_ref[:, cols] = _dot(_b16(p), _b16(mv_ref[:, cols]))

    (o,) = _tiled("mem_attn_fwd", body, S // ts, [_cols(z, ts, WIDTH, CB_MQ), _full(mk), _full(mv)],
                  [_orow(S, (WIDTH,), F32, ts)])
    return o


def _mem_attn_bwd(do, z, mk, mv, ts):
    S = z.shape[0]
    M = mk.shape[0]

    def body(t, first, ins, outs, scratch):
        do_ref, q_ref, mk_ref, mv_ref = ins
        dq_ref, dmk_ref, dmv_ref = outs

        @pl.when(first)
        def _():
            dmk_ref[...] = jnp.zeros_like(dmk_ref)
            dmv_ref[...] = jnp.zeros_like(dmv_ref)

        scale = MEM_DIM ** -0.5
        for h in range(MEM_HEADS):
            cols = slice(h * MEM_DIM, (h + 1) * MEM_DIM)
            qb = _b16(q_ref[:, cols])
            kb = _b16(mk_ref[:, cols])
            dob = _b16(do_ref[:, cols])
            s = _dot(qb, kb, NT) * scale
            m = jnp.max(s, axis=1, keepdims=True)
            p = jnp.exp(s - m)
            p = p / jnp.sum(p, axis=1, keepdims=True)
            dmv_ref[:, cols] += _dot(_b16(p), dob, TN)
            dp = _dot(dob, _b16(mv_ref[:, cols]), NT)
            ds = p * (dp - jnp.sum(dp * p, axis=1, keepdims=True)) * scale
            dsb = _b16(ds)
            dq_ref[:, cols] = _dot(dsb, kb)
            dmk_ref[:, cols] += _dot(dsb, qb, TN)

    return _tiled("mem_attn_bwd", body, S // ts,
                  [_rows(do, ts), _cols(z, ts, WIDTH, CB_MQ), _full(mk), _full(mv)],
                  [_orow(S, (WIDTH,), F32, ts), _oacc((M, WIDTH), F32), _oacc((M, WIDTH), F32)])


def _head_norm(ob, g):
    xs, rs = [], []
    for h in range(GDN_HEADS):
        o = ob[:, h * GDN_DIM:(h + 1) * GDN_DIM]
        r = lax.rsqrt(jnp.mean(o * o, axis=1, keepdims=True) + EPS)
        xs.append(o * r)
        rs.append(r)
    return xs, rs


def _merge_fwd(x, z, o_a, o_b, o_m, gdn_g, b_merge, wb, wout, ts):
    S, D = x.shape

    def body(t, first, ins, outs, scratch):
        (x_ref, g_ref, oa_ref, az_ref, ob_ref, bz_ref, om_ref, mz_ref, gg_ref, bm_ref, wb_ref,
         wo_ref) = ins
        xo_ref, ya_ref, yb_ref, ym_ref, mg_ref = outs
        ya = oa_ref[...] * _silu_and_grad(az_ref[...])[0]
        xs, _ = _head_norm(ob_ref[...], None)
        nb = jnp.concatenate([xh * gg_ref[...] for xh in xs], axis=1)
        yb = nb * _silu_and_grad(bz_ref[...])[0]
        ym = om_ref[...] * _silu_and_grad(mz_ref[...])[0]
        merged = jnp.zeros((ts, D), F32)
        for n, (y, y_ref) in enumerate(((ya, ya_ref), (yb, yb_ref), (ym, ym_ref))):
            yb16 = _b16(y)
            y_ref[...] = yb16
            gate = _sigmoid(g_ref[:, n * D:(n + 1) * D] + bm_ref[:, n * D:(n + 1) * D])
            merged = merged + gate * _dot(yb16, wb_ref[n])
        mb = _b16(merged)
        mg_ref[...] = mb
        xo_ref[...] = x_ref[...] + _dot(mb, wo_ref[...])

    half = lambda a: _rows(a, ts)
    return _tiled("merge_fwd", body, S // ts,
                  [_rows(x, ts), _cols(z, ts, 3 * D, CB_GATES), half(o_a), _cols(z, ts, WIDTH, CB_AZ),
                   half(o_b), _cols(z, ts, WIDTH, CB_BZ), half(o_m), _cols(z, ts, WIDTH, CB_MZ),
                   _full(gdn_g.reshape(1, GDN_DIM)), _full(b_merge.reshape(1, 3 * D)), _full(wb), _full(wout)],
                  [_orow(S, (D,), F32, ts), _orow(S, (WIDTH,), BF16, ts), _orow(S, (WIDTH,), BF16, ts),
                   _orow(S, (WIDTH,), BF16, ts), _orow(S, (D,), BF16, ts)])


def _merge_bwd(dout, z, o_a, o_b, o_m, ya, yb, ym, gdn_g, b_merge, wb, wout, hsum, ts):
    S, D = dout.shape

    def body(t, first, ins, outs, scratch):
        (do_ref, g_ref, oa_ref, az_ref, ob_ref, bz_ref, om_ref, mz_ref, ya_ref, yb_ref, ym_ref,
         gg_ref, bm_ref, wb_ref, wo_ref, hs_ref) = ins
        (dg_ref, dpa_ref, dpb_ref, dpm_ref, doa_ref, dob_ref, dom_ref, daz_ref, dbz_ref, dmz_ref,
         dl_ref, dbm_ref, dgg_ref) = outs

        @pl.when(first)
        def _():
            dbm_ref[...] = jnp.zeros_like(dbm_ref)
            dgg_ref[...] = jnp.zeros_like(dgg_ref)

        dmerged = _dot(_b16(do_ref[...]), wo_ref[...], NT)
        dys = []
        for n, (y_ref, dp_ref) in enumerate(((ya_ref, dpa_ref), (yb_ref, dpb_ref), (ym_ref, dpm_ref))):
            sl = slice(n * D, (n + 1) * D)
            gate = _sigmoid(g_ref[:, sl] + bm_ref[:, sl])
            proj = _dot(y_ref[...], wb_ref[n])
            dproj = _b16(gate * dmerged)
            dp_ref[...] = dproj
            dgp = dmerged * proj * gate * (1.0 - gate)
            dg_ref[:, sl] = dgp.astype(dg_ref.dtype)
            dbm_ref[0:1, sl] += jnp.sum(dgp, axis=0, keepdims=True)
            dys.append(_dot(dproj, wb_ref[n], NT))
        dya, dyb, dym = dys
        sa, dsa = _silu_and_grad(az_ref[...])
        oa = oa_ref[...]
        doa = dya * sa
        doa_ref[...] = doa
        daz_ref[...] = dya * oa * dsa
        dl_ref[...] = _dot(hs_ref[...], doa * oa, NT, HIGHEST)
        sm, dsm = _silu_and_grad(mz_ref[...])
        dom_ref[...] = dym * sm
        dmz_ref[...] = dym * om_ref[...] * dsm
        sb, dsb = _silu_and_grad(bz_ref[...])
        xs, rs = _head_norm(ob_ref[...], None)
        gg = gg_ref[...]
        dgg = jnp.zeros((1, GDN_DIM), F32)
        for h in range(GDN_HEADS):
            cols = slice(h * GDN_DIM, (h + 1) * GDN_DIM)
            dn = dyb[:, cols] * sb[:, cols]
            dbz_ref[:, cols] = dyb[:, cols] * (xs[h] * gg) * dsb[:, cols]
            dgg = dgg + jnp.sum(dn * xs[h], axis=0, keepdims=True)
            dxh = dn * gg
            dob_ref[:, cols] = rs[h] * (dxh - xs[h] * jnp.mean(dxh * xs[h], axis=1, keepdims=True))
        dgg_ref[0:1, :] += dgg

    half = lambda a: _rows(a, ts)
    w512 = lambda dt: _orow(S, (WIDTH,), dt, ts)
    return _tiled("merge_bwd", body, S // ts,
                  [_rows(dout, ts), _cols(z, ts, 3 * D, CB_GATES), half(o_a), _cols(z, ts, WIDTH, CB_AZ),
                   half(o_b), _cols(z, ts, WIDTH, CB_BZ), half(o_m), _cols(z, ts, WIDTH, CB_MZ),
                   half(ya), half(yb), half(ym), _full(gdn_g.reshape(1, GDN_DIM)),
                   _full(b_merge.reshape(1, 3 * D)), _full(wb), _full(wout), _full(hsum)],
                  [_orow(S, (3 * D,), BF16, ts), _orow(S, (D,), BF16, ts), _orow(S, (D,), BF16, ts),
                   _orow(S, (D,), BF16, ts), w512(F32), w512(F32), w512(F32), w512(F32), w512(F32),
                   w512(F32), ((128, S), F32, (128, ts), lambda i: (0, i)), _oacc((8, 3 * D), F32),
                   _oacc((8, GDN_DIM), F32)])


def _to_aligned(w):
    sizes = (512, 512, 512, 8, 512, 512, 512, 512, 4, 4, 512, 512, 512, 3072)
    names = ("aq", "ak", "av", "af", "az", "bq", "bk", "bv", "ba", "bb", "bz", "mq", "mz", "gates")
    p, off = {}, 0
    for n, s in zip(names, sizes):
        p[n] = w[..., off:off + s]
        off += s
    pad = jnp.zeros(w.shape[:-1] + (128 - 16,), w.dtype)
    return jnp.concatenate([p["gates"], p["bq"], p["bk"], p["bv"], p["aq"], p["ak"], p["av"], p["az"],
                            p["bz"], p["mq"], p["mz"], p["af"], p["ba"], p["bb"], pad], axis=-1)


def _from_aligned(w):
    c = lambda lo, n: w[..., lo:lo + n]
    gates, bq, bk, bv = c(0, 3072), c(3072, 512), c(3584, 512), c(4096, 512)
    aq, ak, av, az = c(4608, 512), c(5120, 512), c(5632, 512), c(6144, 512)
    bz, mq, mz = c(6656, 512), c(7168, 512), c(7680, 512)
    af, ba, bb = c(8192, 8), c(8200, 4), c(8204, 4)
    return jnp.concatenate([aq, ak, av, af, az, bq, bk, bv, ba, bb, bz, mq, mz, gates], axis=-1)


def _lanes128(v, lane0):
    return jnp.pad(v.astype(F32)[None, :], ((0, 0), (lane0, 128 - lane0 - v.shape[0])))


def _tiles(S):
    ts = min(512, S // 2)
    return dict(ts=ts, ts_small=min(256, S // 2), tq=min(512, S // 4), tq_fwd=min(1024, S // 2))


def _layer_fwd(x, mem, p):
    S = x.shape[0]
    tl = _tiles(S)
    ts, tss, tq = tl["ts"], tl["ts_small"], tl["tq"]
    h, rstd = _rms_fwd("norm_fwd", x, p["norm_g"], ts)
    z = _mm("in_proj", h, p["w_in_al"], tn=1664)

    b_fg128 = _lanes128(p["b_fg"], LANE_AF)
    fT, f_hi, f_mid, f_lo = _fox_decay(z, b_fg128, ts)
    aq = z[:, CB_AQ * WIDTH:(CB_AQ + 1) * WIDTH]
    ak = z[:, CB_AK * WIDTH:(CB_AK + 1) * WIDTH]
    av = z[:, CB_AV * WIDTH:(CB_AV + 1) * WIDTH]
    q32 = _heads_major(aq, FOX_HEADS, FOX_DIM)
    kh = _heads_major(ak, FOX_HEADS, FOX_DIM).astype(BF16)
    vh = _heads_major(av, FOX_HEADS, FOX_DIM).astype(BF16)
    khT = kh.transpose(0, 2, 1)
    piecesT = jnp.stack([f[:FOX_HEADS] for f in (f_hi, f_mid, f_lo)], axis=1)
    pieces = piecesT.transpose(0, 2, 1)
    ones3 = jnp.ones((FOX_HEADS, S, 3), BF16)
    padk = jnp.zeros((FOX_HEADS, S, FOX_AUG - FOX_DIM - 6), BF16)
    q_aug = jnp.concatenate([q32, pieces.astype(F32), ones3.astype(F32), padk.astype(F32)], axis=-1)
    k_aug = jnp.concatenate([kh, ones3, -pieces, padk], axis=-1)
    kT_aug = jnp.concatenate([khT, ones3.transpose(0, 2, 1), -piecesT, padk.transpose(0, 2, 1)], axis=1)
    v_aug = jnp.concatenate([vh, ones3[:, :, :1], jnp.zeros((FOX_HEADS, S, 128 - FOX_DIM - 1), BF16)],
                            axis=-1)
    o_h, lse, qs = _fox_fwd(q_aug, kT_aug, v_aug, tl["tq_fwd"])
    o_a = _heads_minor(o_h)

    a128 = _lanes128(p["a_log"], LANE_BA)
    dt128 = _lanes128(p["dt_bias"], LANE_BA)
    qkv, cpre, gb, gbT = _gdn_prep(z, p["conv_w"], a128, dt128, ts)
    grow = gbT[LANE_BA:LANE_BA + GDN_HEADS].reshape(GDN_HEADS, S // CHUNK, CHUNK)
    u, w, qd, kd, aqk, T = _gdn_local_fwd(qkv, gb, grow, ts)
    o_b, vn, states = _gdn_scan_fwd(u, w, qd, kd, aqk, gb, ts)

    mem_h, mem_r = _rms_fwd("mem_norm_fwd", mem, p["mem_norm_g"], mem.shape[0])
    mkv = _mm("mem_kv", mem_h, p["w_mem_kv"])
    mk, mv = mkv[:, :WIDTH], mkv[:, WIDTH:]
    o_m = _mem_attn_fwd(z, mk, mv, ts)

    x_next, ya, yb, ym, merged = _merge_fwd(x, z, o_a, o_b, o_m, p["gdn_norm_g"], p["b_merge"],
                                            p["w_branch"], p["w_out"], tss)
    saved = dict(x=x, h=h, rstd=rstd, z=z, b_fg128=b_fg128, qs=qs, k_aug=k_aug, khT=khT, vh=vh, lse=lse, o_a=o_a, a128=a128, dt128=dt128, qkv=qkv, cpre=cpre, gb=gb,
                 grow=grow, w=w, qd=qd, kd=kd, aqk=aqk, T=T, o_b=o_b, vn=vn, states=states,
                 mem_h=mem_h, mem_r=mem_r, mk=mk, mv=mv, o_m=o_m, ya=ya, yb=yb, ym=ym, merged=merged)
    return x_next, saved


def _layer_bwd(dout, mem, p, s):
    S = dout.shape[0]
    tl = _tiles(S)
    ts, tss, tq = tl["ts"], tl["ts_small"], tl["tq"]
    z = s["z"]
    hsum = (jnp.arange(128)[:, None] == jnp.arange(WIDTH)[None, :] // FOX_DIM).astype(F32)
    (dgates, dpa, dpb, dpm, do_a, do_b, do_m, daz, dbz, dmz, deltaT, db_merge, dgdn_g) = _merge_bwd(
        dout, z, s["o_a"], s["o_b"], s["o_m"], s["ya"], s["yb"], s["ym"], p["gdn_norm_g"],
        p["b_merge"], p["w_branch"], p["w_out"], hsum, tss)
    g = {}
    g["b_merge"] = db_merge[0]
    g["gdn_norm_g"] = dgdn_g[0]
    g["w_out"] = _mm("dw_out", s["merged"], dout, ta=True)
    g["w_branch"] = jnp.stack([_mm("dw_branch", y, dp, ta=True)
                               for y, dp in ((s["ya"], dpa), (s["yb"], dpb), (s["ym"], dpm))])

    do_h = _heads_major(do_a, FOX_HEADS, FOX_DIM).astype(BF16)
    delta_row = deltaT[:FOX_HEADS, None, :]
    dqT, dk_h, dv_h, dfk, dfq = _fox_bwd(s["qs"], s["k_aug"], s["khT"], s["vh"], do_h, s["lse"],
                                         delta_row, tq)
    daq = _heads_minor(dqT.transpose(0, 2, 1))
    dak = _heads_minor(dk_h)
    dav = _heads_minor(dv_h)
    daf128, db_fg = _fox_decay_bwd(dfk[:, 0, :], dfq[:, 0, :], z, s["b_fg128"], ts)
    g["b_fg"] = db_fg[:FOX_HEADS]

    du, dw, dqd, dkd, daqk, dgl = _gdn_scan_bwd(do_b, s["w"], s["qd"], s["kd"], s["aqk"], s["vn"],
                                                s["states"], s["gb"], ts)
    dqkv, dgb = _gdn_local_bwd(s["qkv"], s["gb"], s["grow"], s["T"], du, dw, dqd, dkd, daqk, dgl, ts)
    dbqkv, dzs_b, dconv, dad = _gdn_prep_bwd(dqkv, dgb, s["cpre"], z, p["conv_w"], s["a128"],
                                             s["dt128"], ts)
    g["conv_w"] = dconv[:4]
    g["a_log"] = dad[0, LANE_BA:LANE_BA + GDN_HEADS]
    g["dt_bias"] = dad[1, LANE_BA:LANE_BA + GDN_HEADS]

    dmq, dmk, dmv = _mem_attn_bwd(do_m, z, s["mk"], s["mv"], ts)
    dmkv = jnp.concatenate([dmk, dmv], axis=1)
    g["w_mem_kv"] = _mm("dw_mem_kv", s["mem_h"], dmkv, ta=True)
    dmem_h = _mm("dmem_h", dmkv, p["w_mem_kv"], tb=True)
    M = mem.shape[0]
    _, g["mem_norm_g"] = _rms_bwd("mem_norm_bwd", dmem_h, mem, s["mem_r"], p["mem_norm_g"],
                                  jnp.zeros_like(mem), M)

    lane = jnp.arange(128)[None, :]
    dsmall = jnp.where(lane < 8, daf128, dzs_b)
    dz = jnp.concatenate([dgates, _b16(dbqkv), _b16(daq), _b16(dak), _b16(dav), _b16(daz), _b16(dbz),
                          _b16(dmq), _b16(dmz), _b16(dsmall)], axis=1)
    g["w_in_al"] = _mm("dw_in", s["h"], dz, ta=True, tn=1664)
    dh = _mm("dh", dz, p["w_in_al"], tb=True, tk=1664)
    dx, g["norm_g"] = _rms_bwd("norm_bwd", dh, s["x"], s["rstd"], p["norm_g"], dout, ts)
    return dx, g


def _local_step(x, mem, layers, final_norm_g, loss_target):
    S = x.shape[0]
    saves = []
    cur = x
    for p in layers:
        cur, sv = _layer_fwd(cur, mem, p)
        saves.append(sv)
    dx, dgf, loss_lanes = _loss_head(cur, final_norm_g, loss_target, _tiles(S)["ts"])
    grads = [None] * len(layers)
    for l in reversed(range(len(layers))):
        dx, grads[l] = _layer_bwd(dx, mem, layers[l], saves[l])
    return loss_lanes, dx, grads, dgf


HBM_SPEC = pl.BlockSpec(memory_space=pltpu.HBM)


def _mesh_pos():
    return lax.axis_index("x"), lax.axis_index("y"), lax.axis_index("c")


def _comm_call(name, body, arrays, out_shapes, n_remote, n_local):
    n = len(arrays)

    def kern(*refs):
        body(refs[:n], refs[n:2 * n], refs[2 * n], refs[2 * n + 1], refs[2 * n + 2])

    return pl.pallas_call(
        kern, name=name, out_shape=out_shapes, in_specs=[HBM_SPEC] * n, out_specs=[HBM_SPEC] * n,
        scratch_shapes=[pltpu.SemaphoreType.DMA((n_remote,)), pltpu.SemaphoreType.DMA((n_remote,)),
                        pltpu.SemaphoreType.DMA((max(n_local, 1),))],
    )(*arrays)


def _remote(src, dst, send_sems, recv_sems, k, to):
    return pltpu.make_async_remote_copy(src_ref=src, dst_ref=dst, send_sem=send_sems.at[k],
                                        recv_sem=recv_sems.at[k], device_id=to, device_id_type=MESH_ID)


def _other_chips(mx, my):
    return [(1 - mx, my), (mx, 1 - my), (1 - mx, 1 - my)]


def _gather_chips(name, shards):
    n = len(shards)

    def body(ins, outs, send_sems, recv_sems, local_sems):
        mx, my, mc = _mesh_pos()
        me = 2 * mx + my
        chips = _other_chips(mx, my)
        locals_, sends = [], []
        for a in range(n):
            lc = pltpu.make_async_copy(ins[a], outs[a].at[me], local_sems.at[a])
            lc.start()
            locals_.append(lc)
            for k, (px, py) in enumerate(chips):
                cp = _remote(ins[a], outs[a].at[me], send_sems, recv_sems, 3 * a + k, (px, py, mc))
                cp.start()
                sends.append(cp)
        for a in range(n):
            for k, (px, py) in enumerate(chips):
                _remote(ins[a], outs[a].at[2 * px + py], send_sems, recv_sems, 3 * a + k,
                        (px, py, mc)).wait_recv()
        for cp in sends:
            cp.wait_send()
        for lc in locals_:
            lc.wait()

    shapes = [jax.ShapeDtypeStruct((N_CHIPS,) + s.shape, s.dtype) for s in shards]
    return _comm_call(name, body, shards, shapes, 3 * n, n)


def _sibling_swap(gs):
    n = len(gs)

    def body(ins, outs, send_sems, recv_sems, local_sems):
        mx, my, mc = _mesh_pos()
        sends = []
        for a in range(n):
            cp = _remote(ins[a].at[:, 1 - mc], outs[a], send_sems, recv_sems, a, (mx, my, 1 - mc))
            cp.start()
            sends.append(cp)
        for cp in sends:
            cp.wait()

    shapes = [jax.ShapeDtypeStruct((g.shape[0],) + g.shape[2:], g.dtype) for g in gs]
    return _comm_call("grad_sibling_swap", body, gs, shapes, n, 0)


def _chip_exchange(ps):
    n = len(ps)

    def body(ins, outs, send_sems, recv_sems, local_sems):
        mx, my, mc = _mesh_pos()
        me = 2 * mx + my
        chips = _other_chips(mx, my)
        locals_, sends = [], []
        for a in range(n):
            lc = pltpu.make_async_copy(ins[a].at[me], outs[a].at[me], local_sems.at[a])
            lc.start()
            locals_.append(lc)
            for k, (px, py) in enumerate(chips):
                cp = _remote(ins[a].at[2 * px + py], outs[a].at[me], send_sems, recv_sems, 3 * a + k,
                             (px, py, mc))
                cp.start()
                sends.append(cp)
        for a in range(n):
            for k, (px, py) in enumerate(chips):
                _remote(ins[a].at[me], outs[a].at[2 * px + py], send_sems, recv_sems, 3 * a + k,
                        (px, py, mc)).wait_recv()
        for cp in sends:
            cp.wait_send()
        for lc in locals_:
            lc.wait()

    shapes = [jax.ShapeDtypeStruct(p.shape, p.dtype) for p in ps]
    return _comm_call("grad_chip_exchange", body, ps, shapes, 3 * n, n)


def _sibling_gather(hs):
    n = len(hs)

    def body(ins, outs, send_sems, recv_sems, local_sems):
        mx, my, mc = _mesh_pos()
        locals_, sends = [], []
        for a in range(n):
            lc = pltpu.make_async_copy(ins[a], outs[a].at[mc], local_sems.at[a])
            lc.start()
            locals_.append(lc)
            cp = _remote(ins[a], outs[a].at[mc], send_sems, recv_sems, a, (mx, my, 1 - mc))
            cp.start()
            sends.append(cp)
        for a in range(n):
            _remote(ins[a], outs[a].at[1 - mc], send_sems, recv_sems, a, (mx, my, 1 - mc)).wait_recv()
        for cp in sends:
            cp.wait_send()
        for lc in locals_:
            lc.wait()

    shapes = [jax.ShapeDtypeStruct((2,) + h.shape, h.dtype) for h in hs]
    return _comm_call("grad_sibling_gather", body, hs, shapes, n, n)


def _add_pairs(a, b, tr, out_dtype):
    n, H, C = a.shape

    def kern(a_ref, b_ref, o_ref):
        o_ref[...] = (a_ref[...] + b_ref[...]).astype(o_ref.dtype)

    spec = pl.BlockSpec((None, tr, C), lambda j, i: (j, i, 0))
    return pl.pallas_call(
        kern, name="grad_pair_sum", grid=(n, H // tr), in_specs=[spec, spec], out_specs=spec,
        out_shape=jax.ShapeDtypeStruct((n, H, C), out_dtype),
        compiler_params=_params(("parallel", "parallel")),
    )(a, b)


def _sum_slots(r4, tr):
    n, H, C = r4.shape

    def kern(r_ref, o_ref):
        f = lambda k: r_ref[k].astype(F32)
        o_ref[...] = ((f(0) + f(1)) + f(2)) + f(3)

    return pl.pallas_call(
        kern, name="grad_chip_sum", grid=(H // tr,),
        in_specs=[pl.BlockSpec((n, tr, C), lambda i: (0, i, 0))],
        out_specs=pl.BlockSpec((tr, C), lambda i: (i, 0)),
        out_shape=jax.ShapeDtypeStruct((H, C), F32),
        compiler_params=_params(("parallel",)),
    )(r4)


def _adamw(w, g, m, v, tr):
    R, C = w.shape
    c1 = 1.0 - ADAM_B1
    c2 = 1.0 - ADAM_B2
    bc1 = 1.0 - ADAM_B1 ** ADAM_STEP
    bc2 = 1.0 - ADAM_B2 ** ADAM_STEP

    def kern(w_ref, g_ref, m_ref, v_ref, d_ref, mo_ref, vo_ref):
        gv = g_ref[...]
        mn = ADAM_B1 * m_ref[...] + c1 * gv
        vn = ADAM_B2 * v_ref[...] + c2 * (gv * gv)
        m_hat = mn / bc1
        v_hat = vn / bc2
        d_ref[...] = -ADAM_LR * (m_hat / (jnp.sqrt(v_hat) + ADAM_EPS) + ADAM_WD * w_ref[...])
        mo_ref[...] = mn
        vo_ref[...] = vn

    spec = pl.BlockSpec((tr, C), lambda i: (i, 0))
    shape = jax.ShapeDtypeStruct((R, C), F32)
    return pl.pallas_call(
        kern, name="adamw", grid=(R // tr,), in_specs=[spec] * 4, out_specs=[spec] * 3,
        out_shape=[shape] * 3, compiler_params=_params(("parallel",)),
    )(w, g, m, v)


PACK_COLS = 1024
PACK_ROWS = 512
W_SHARD = N_IN // N_CHIPS
SLAB = ("conv_w", "w_mem_kv", "w_branch", "w_out")
SMALL =("norm_g", "b_fg", "b_merge", "a_log", "dt_bias", "gdn_norm_g", "mem_norm_g", "final_norm_g")
ALL_WEIGHTS = ("norm_g", "w_in", "b_fg", "b_merge", "conv_w", "a_log", "dt_bias", "gdn_norm_g",
               "mem_norm_g", "w_mem_kv", "w_branch", "w_out", "final_norm_g")
SHARD_AXIS = {"w_in": 2, "conv_w": 2, "w_mem_kv": 1, "w_branch": 3, "w_out": 1}


def _pack(arrays, row_multiple):
    flat = jnp.concatenate([a.reshape(-1) for a in arrays])
    n = flat.shape[0]
    rows = -(-n // PACK_COLS)
    rows = -(-rows // row_multiple) * row_multiple
    flat = jnp.pad(flat, (0, rows * PACK_COLS - n))
    return flat.reshape(rows, PACK_COLS)


def _unpack(slab, shapes):
    out, off = [], 0
    for shp in shapes:
        n = 1
        for d in shp:
            n *= d
        r0, r1 = off // PACK_COLS, -(-(off + n) // PACK_COLS)
        rows = slab[r0:r1].reshape(-1)
        out.append(rows[off - r0 * PACK_COLS:off - r0 * PACK_COLS + n].reshape(shp))
        off += n
    return out


def _shard_of(full, name, j):
    ax = SHARD_AXIS[name]
    n = full.shape[ax] // N_CHIPS
    return lax.slice_in_dim(full, j * n, (j + 1) * n, axis=ax)


W_IN_PIECES = ((0, 512, 4608), (512, 512, 5120), (1024, 512, 5632), (1536, 8, 8192), (1544, 512, 6144),
               (2056, 512, 3072), (2568, 512, 3584), (3080, 512, 4096), (3592, 4, 8200), (3596, 4, 8204),
               (3600, 512, 6656), (4112, 512, 7168), (4624, 512, 7680), (5136, 3072, 0))


def _aligned_from_shards(shards):
    def cols(lo, n):
        parts = []
        while n > 0:
            j, off = divmod(lo, W_SHARD)
            take = min(n, W_SHARD - off)
            parts.append(shards[j][..., off:off + take])
            lo, n = lo + take, n - take
        return parts

    out = []
    for lo, n, _ in sorted(W_IN_PIECES, key=lambda p: p[2]):
        out += cols(lo, n)
    out.append(jnp.zeros(shards[0].shape[:-1] + (N_AL - N_IN,), shards[0].dtype))
    return jnp.concatenate(out, axis=-1)


def _shard_from_aligned(w_al, j):
    lo_j, hi_j = j * W_SHARD, (j + 1) * W_SHARD
    parts = []
    for lo, n, al in W_IN_PIECES:
        a, b = max(lo, lo_j), min(lo + n, hi_j)
        if a < b:
            parts.append(w_al[..., al + a - lo:al + b - lo])
    return jnp.concatenate(parts, axis=-1)


def kernel(x, mem, norm_g, w_in, b_fg, b_merge, conv_w, a_log, dt_bias, gdn_norm_g, mem_norm_g, w_mem_kv, w_branch, w_out, final_norm_g, loss_target, m_norm_g, m_w_in, m_b_fg, m_b_merge, m_conv_w, m_a_log, m_dt_bias, m_gdn_norm_g, m_mem_norm_g, m_w_mem_kv, m_w_branch, m_w_out, m_final_norm_g, v_norm_g, v_w_in, v_b_fg, v_b_merge, v_conv_w, v_a_log, v_dt_bias, v_gdn_norm_g, v_mem_norm_g, v_w_mem_kv, v_w_branch, v_w_out, v_final_norm_g):
    wts = dict(norm_g=norm_g, w_in=w_in, b_fg=b_fg, b_merge=b_merge, conv_w=conv_w, a_log=a_log,
               dt_bias=dt_bias, gdn_norm_g=gdn_norm_g, mem_norm_g=mem_norm_g, w_mem_kv=w_mem_kv,
               w_branch=w_branch, w_out=w_out, final_norm_g=final_norm_g)
    mom = dict(norm_g=m_norm_g, w_in=m_w_in, b_fg=m_b_fg, b_merge=m_b_merge, conv_w=m_conv_w,
               a_log=m_a_log, dt_bias=m_dt_bias, gdn_norm_g=m_gdn_norm_g, mem_norm_g=m_mem_norm_g,
               w_mem_kv=m_w_mem_kv, w_branch=m_w_branch, w_out=m_w_out, final_norm_g=m_final_norm_g)
    vel = dict(norm_g=v_norm_g, w_in=v_w_in, b_fg=v_b_fg, b_merge=v_b_merge, conv_w=v_conv_w,
               a_log=v_a_log, dt_bias=v_dt_bias, gdn_norm_g=v_gdn_norm_g, mem_norm_g=v_mem_norm_g,
               w_mem_kv=v_w_mem_kv, w_branch=v_w_branch, w_out=v_w_out, final_norm_g=v_final_norm_g)

    big = ("w_in", "w_mem_kv", "w_branch", "w_out")
    gathered = _gather_chips("weight_gather", [wts[n].astype(BF16) for n in big] + [conv_w])
    all_w = dict(zip(big + ("conv_w",), gathered))
    w_in_al = _aligned_from_shards([all_w["w_in"][j] for j in range(N_CHIPS)])

    layers = []
    for l in range(DEPTH):
        rows_of = lambda n: all_w[n][:, l].reshape(D_MODEL, D_MODEL)
        last_of = lambda n: jnp.concatenate([all_w[n][j, l] for j in range(N_CHIPS)], axis=-1)
        layers.append(dict(norm_g=norm_g[l], w_in_al=w_in_al[l], b_fg=b_fg[l], b_merge=b_merge[l],
                           conv_w=jnp.pad(last_of("conv_w"), ((0, 4), (0, 0))), a_log=a_log[l],
                           dt_bias=dt_bias[l], gdn_norm_g=gdn_norm_g[l], mem_norm_g=mem_norm_g[l],
                           w_mem_kv=rows_of("w_mem_kv"), w_branch=last_of("w_branch"),
                           w_out=rows_of("w_out")))

    loss_lanes, dx, grads, dgf = _local_step(x[0], mem[0], layers, final_norm_g, loss_target[0])

    gfull = {n: jnp.stack([grads[l][n] for l in range(DEPTH)])
             for n in ("norm_g", "b_fg", "b_merge", "conv_w", "a_log", "dt_bias", "gdn_norm_g",
                       "mem_norm_g", "w_mem_kv", "w_branch", "w_out")}
    gfull["final_norm_g"] = dgf
    loss_local = jnp.sum(loss_lanes).reshape(1)
    small_g = [gfull[n] for n in SMALL] + [loss_local]
    dw_al = jnp.stack([grads[l]["w_in_al"] for l in range(DEPTH)])
    ga = jnp.stack([_shard_from_aligned(dw_al, j) for j in range(N_CHIPS)])
    gb = jnp.stack([_pack([_shard_of(gfull[n], n, j) for n in SLAB] + small_g, PACK_ROWS)
                    for j in range(N_CHIPS)])
    R = gb.shape[1]
    gb = gb.reshape(N_CHIPS, 2, R // 2, PACK_COLS)

    mc = lax.axis_index("c")
    tr = 256
    from_sibling = _sibling_swap([ga, gb])
    mine = [lax.dynamic_index_in_dim(g, mc, axis=1, keepdims=False) for g in (ga, gb)]
    pair = [_add_pairs(a, b, tr, dt) for a, b, dt in zip(mine, from_sibling, (BF16, F32))]
    slots = _chip_exchange(pair)
    half = [_sum_slots(s, tr) for s in slots]
    ga_sum, gb_sum = _sibling_gather(half)
    gb_sum = gb_sum.reshape(R, PACK_COLS)

    zero1 = jnp.zeros((1,), F32)
    slab = lambda d: _pack([d[n] for n in SLAB] + [d[n] for n in SMALL] + [zero1], PACK_ROWS)
    delta_s, m_s, v_s = _adamw(slab(wts), gb_sum, slab(mom), slab(vel), tr)
    flat_in = lambda a: a.reshape(DEPTH * D_MODEL, W_SHARD)
    in_res = _adamw(flat_in(w_in), flat_in(ga_sum), flat_in(m_w_in), flat_in(v_w_in), tr)

    names = list(SLAB) + list(SMALL)
    shapes = [wts[n].shape for n in names] + [(1,)]
    g_un = dict(zip(names + ["loss"], _unpack(gb_sum, shapes)))
    d_un = dict(zip(names, _unpack(delta_s, shapes[:-1])))
    m_un = dict(zip(names, _unpack(m_s, shapes[:-1])))
    v_un = dict(zip(names, _unpack(v_s, shapes[:-1])))
    g_un["w_in"] = ga_sum
    d_un["w_in"], m_un["w_in"], v_un["w_in"] = [r.reshape(w_in.shape) for r in in_res]

    loss = g_un["loss"][0]
    return (loss, dx[None], *[g_un[n] for n in ALL_WEIGHTS], *[d_un[n] for n in ALL_WEIGHTS],
            *[m_un[n] for n in ALL_WEIGHTS], *[v_un[n] for n in ALL_WEIGHTS])
```

```python
import functools

import jax
import jax.numpy as jnp
from jax import lax
from jax.experimental import pallas as pl
from jax.experimental.pallas import tpu as pltpu

F32 = jnp.float32
BF16 = jnp.bfloat16
HIGHEST = lax.Precision.HIGHEST
PREC_UT = lax.Precision.HIGH
MESH_ID = pl.DeviceIdType.MESH

D_MODEL = 1024
DEPTH = 2
CHUNK = 64
EPS = 1e-6
FOX_HEADS, FOX_DIM = 8, 64
GDN_HEADS, GDN_DIM = 4, 128
MEM_HEADS, MEM_DIM = 4, 128
WIDTH = 512
N_BRANCH = 3
N_IN = 8208
N_AL = 8320
N_CHIPS = 4
NEG = -1e30
LOG2E = 1.4426950408889634
LN2 = 0.6931471805599453

ADAM_LR, ADAM_B1, ADAM_B2, ADAM_EPS, ADAM_WD, ADAM_STEP = 0.001, 0.9, 0.999, 1e-08, 0.01, 10

CB_GATES = 0
CB_BQKV = 2
CB_AQ, CB_AK, CB_AV, CB_AZ, CB_BZ, CB_MQ, CB_MZ = 9, 10, 11, 12, 13, 14, 15
CB_SMALL = 64
LANE_AF, LANE_BA, LANE_BB = 0, 8, 12

NN = ((1,), (0,))
NT = ((1,), (1,))
TN = ((0,), (0,))

VMEM_LIMIT_BYTES = 56 * 1024 * 1024


def _dot(a, b, dims=NN, prec=None):
    return lax.dot_general(a, b, (dims, ((), ())), preferred_element_type=F32, precision=prec)


def _bdot(a, b, ca, cb, prec=None):
    return lax.dot_general(a, b, (((ca,), (cb,)), ((0,), (0,))), preferred_element_type=F32,
                           precision=prec)


def _b16(a):
    return a.astype(BF16)


def _eye(n, dtype=F32):
    r = lax.broadcasted_iota(jnp.int32, (n, n), 0)
    c = lax.broadcasted_iota(jnp.int32, (n, n), 1)
    return jnp.where(r == c, 1.0, 0.0).astype(dtype)


def _transpose_exact(x):
    return _dot(_eye(x.shape[1]), x, NT, HIGHEST)


def _col_to_row(col):
    n = col.shape[0]
    return jnp.sum(jnp.where(_eye(n) > 0.5, col, 0.0), axis=0, keepdims=True)


def _row_to_col(row):
    n = row.shape[1]
    return jnp.sum(jnp.where(_eye(n) > 0.5, row, 0.0), axis=1, keepdims=True)


def _sigmoid(x):
    return 1.0 / (1.0 + jnp.exp(-x))


def _softplus(x):
    return jnp.maximum(x, 0.0) + jnp.log(1.0 + jnp.exp(-jnp.abs(x)))


def _silu_and_grad(x):
    s = _sigmoid(x)
    return x * s, s * (1.0 + x * (1.0 - s))


def _params(semantics):
    return pltpu.CompilerParams(dimension_semantics=semantics, vmem_limit_bytes=VMEM_LIMIT_BYTES)


def _rows(a, ts):
    nd = a.ndim
    return (a, (ts,) + a.shape[1:], lambda i, nd=nd: (i,) + (0,) * (nd - 1))


def _cols(a, ts, width, cb):
    return (a, (ts, width), lambda i, cb=cb: (i, cb))


def _full(a):
    nd = a.ndim
    return (a, a.shape, lambda i, nd=nd: (0,) * nd)


def _orow(S, tail, dtype, ts):
    nd = 1 + len(tail)
    return ((S,) + tuple(tail), dtype, (ts,) + tuple(tail), lambda i, nd=nd: (i,) + (0,) * (nd - 1))


def _oacc(shape, dtype):
    nd = len(shape)
    return (tuple(shape), dtype, tuple(shape), lambda i, nd=nd: (0,) * nd)


def _tiled(name, body, n_steps, ins, outs, scratch=(), reverse=False):
    def rev(imap):
        if not reverse:
            return imap
        return lambda i: imap(n_steps - 1 - i)

    in_specs = [pl.BlockSpec(blk, rev(imap)) for (_, blk, imap) in ins]
    out_specs = [pl.BlockSpec(blk, rev(imap)) for (_, _, blk, imap) in outs]
    out_shape = [jax.ShapeDtypeStruct(shape, dt) for (shape, dt, _, _) in outs]
    n_in, n_out = len(ins), len(outs)

    def kern(*refs):
        step = pl.program_id(0)
        t = (n_steps - 1 - step) if reverse else step
        body(t, step == 0, refs[:n_in], refs[n_in:n_in + n_out], refs[n_in + n_out:])

    res = pl.pallas_call(
        kern, name=name, grid=(n_steps,), in_specs=in_specs, out_specs=out_specs,
        out_shape=out_shape, scratch_shapes=list(scratch),
        compiler_params=_params(("arbitrary",)),
    )(*[a for (a, _, _) in ins])
    return res


def _pick(n, pref):
    if n <= pref:
        return n
    best = None
    for t in range(128, pref + 1, 128):
        if n % t == 0:
            best = t
    assert best is not None, (n, pref)
    return best


def _mm(name, a, b, ta=False, tb=False, out_dtype=F32, tm=1024, tn=1024, tk=1024):
    if ta:
        K, M = a.shape
    else:
        M, K = a.shape
    if tb:
        N, K2 = b.shape
    else:
        K2, N = b.shape
    assert K == K2, (a.shape, b.shape, ta, tb)
    tm, tn, tk = _pick(M, tm), _pick(N, tn), _pick(K, tk)
    nk = K // tk
    a_spec = (pl.BlockSpec((tk, tm), lambda i, j, k: (k, i)) if ta
              else pl.BlockSpec((tm, tk), lambda i, j, k: (i, k)))
    b_spec = (pl.BlockSpec((tn, tk), lambda i, j, k: (j, k)) if tb
              else pl.BlockSpec((tk, tn), lambda i, j, k: (k, j)))
    dims = ((0,) if ta else (1,), (1,) if tb else (0,))

    def kern_single(a_ref, b_ref, o_ref):
        o_ref[...] = _dot(_b16(a_ref[...]), _b16(b_ref[...]), dims).astype(o_ref.dtype)

    def kern_acc(a_ref, b_ref, o_ref, acc_ref):
        k = pl.program_id(2)

        @pl.when(k == 0)
        def _():
            acc_ref[...] = jnp.zeros_like(acc_ref)

        acc_ref[...] += _dot(_b16(a_ref[...]), _b16(b_ref[...]), dims)

        @pl.when(k == nk - 1)
        def _():
            o_ref[...] = acc_ref[...].astype(o_ref.dtype)

    return pl.pallas_call(
        kern_single if nk == 1 else kern_acc, name=name, grid=(M // tm, N // tn, nk),
        in_specs=[a_spec, b_spec],
        out_specs=pl.BlockSpec((tm, tn), lambda i, j, k: (i, j)),
        out_shape=jax.ShapeDtypeStruct((M, N), out_dtype),
        scratch_shapes=[] if nk == 1 else [pltpu.VMEM((tm, tn), F32)],
        compiler_params=_params(("parallel", "parallel", "arbitrary")),
    )(a, b)


def _rms_fwd(name, x, g, ts):
    S, D = x.shape

    def body(t, first, ins, outs, scratch):
        x_ref, g_ref = ins
        h_ref, r_ref = outs
        xv = x_ref[...]
        r = lax.rsqrt(jnp.mean(xv * xv, axis=1, keepdims=True) + EPS)
        h_ref[...] = (xv * r * g_ref[...]).astype(h_ref.dtype)
        r_ref[...] = r

    return _tiled(name, body, S // ts, [_rows(x, ts), _full(g.reshape(1, D))],
                  [_orow(S, (D,), BF16, ts), _orow(S, (1,), F32, ts)])


def _rms_bwd(name, dh, x, rstd, g, dres, ts):
    S, D = x.shape

    def body(t, first, ins, outs, scratch):
        dh_ref, x_ref, r_ref, g_ref, dres_ref = ins
        dx_ref, dg_ref = outs
        r = r_ref[...]
        xh = x_ref[...] * r
        dhv = dh_ref[...]
        dxh = dhv * g_ref[...]
        dx_ref[...] = dres_ref[...] + r * (dxh - xh * jnp.mean(dxh * xh, axis=1, keepdims=True))

        @pl.when(first)
        def _():
            dg_ref[...] = jnp.zeros_like(dg_ref)

        dg_ref[0:1, :] += jnp.sum(dhv * xh, axis=0, keepdims=True)

    dx, dg = _tiled(name, body, S // ts,
                    [_rows(dh, ts), _rows(x, ts), _rows(rstd, ts), _full(g.reshape(1, D)), _rows(dres, ts)],
                    [_orow(S, (D,), F32, ts), _oacc((8, D), F32)])
    return dx, dg[0]


def _loss_head(x, g, target, ts):
    S, D = x.shape

    def body(t, first, ins, outs, scratch):
        x_ref, g_ref, tgt_ref = ins
        dx_ref, dg_ref, loss_ref = outs
        xv = x_ref[...]
        gv = g_ref[...]
        r = lax.rsqrt(jnp.mean(xv * xv, axis=1, keepdims=True) + EPS)
        xh = xv * r
        err = xh * gv - tgt_ref[...]
        dy = err * (1.0 / D)
        dxh = dy * gv
        dx_ref[...] = r * (dxh - xh * jnp.mean(dxh * xh, axis=1, keepdims=True))

        @pl.when(first)
        def _():
            dg_ref[...] = jnp.zeros_like(dg_ref)
            loss_ref[...] = jnp.zeros_like(loss_ref)

        dg_ref[0:1, :] += jnp.sum(dy * xh, axis=0, keepdims=True)
        per_lane = jnp.sum(err * err, axis=0, keepdims=True)
        loss_ref[0:1, :] += per_lane * (0.5 / D)

    dx, dg, loss = _tiled("loss_head", body, S // ts,
                          [_rows(x, ts), _full(g.reshape(1, D)), _rows(target, ts)],
                          [_orow(S, (D,), F32, ts), _oacc((8, D), F32), _oacc((8, D), F32)])
    return dx, dg[0], loss[0]


def _scan_rows(x, length, seg, reverse=False):
    row = lax.broadcasted_iota(jnp.int32, x.shape, 0) % seg
    k = 1
    while k < seg:
        if reverse:
            x = x + jnp.where(row < seg - k, pltpu.roll(x, length - k, 0), 0.0)
        else:
            x = x + jnp.where(row >= k, pltpu.roll(x, k, 0), 0.0)
        k *= 2
    return x


def _fox_decay(z, b_fg128, ts):
    S = z.shape[0]

    def body(t, first, ins, outs, scratch):
        zs_ref, b_ref = ins
        f_ref, hi_ref, mid_ref, lo_ref = outs
        (carry,) = scratch

        @pl.when(first)
        def _():
            carry[...] = jnp.zeros_like(carry)

        logf = -_softplus(-(zs_ref[...] + b_ref[...]))
        run = _scan_rows(logf, ts, ts) + carry[0:1, :]
        carry[0:1, :] = run[ts - 1:ts, :]
        f_ref[...] = _transpose_exact(run)
        f2 = run * LOG2E
        hi = f2.astype(BF16)
        r1 = f2 - hi.astype(F32)
        mid = r1.astype(BF16)
        lo = (r1 - mid.astype(F32)).astype(BF16)
        eye = _eye(128, BF16)
        hi_ref[...] = _dot(eye, hi, NT).astype(BF16)
        mid_ref[...] = _dot(eye, mid, NT).astype(BF16)
        lo_ref[...] = _dot(eye, lo, NT).astype(BF16)

    tcol = lambda dt: ((128, S), dt, (128, ts), lambda i: (0, i))
    return _tiled("fox_decay", body, S // ts,
                  [_cols(z, ts, 128, CB_SMALL), _full(b_fg128)],
                  [tcol(F32), tcol(BF16), tcol(BF16), tcol(BF16)], scratch=[pltpu.VMEM((8, 128), F32)])


def _fox_decay_bwd(dfk_rows, dfq_rows, z, b_fg128, ts):
    S = z.shape[0]
    H = dfk_rows.shape[0]

    def body(t, first, ins, outs, scratch):
        dfk_ref, dfq_ref, zs_ref, b_ref = ins
        daf_ref, db_ref = outs
        (carry,) = scratch

        @pl.when(first)
        def _():
            carry[...] = jnp.zeros_like(carry)
            db_ref[...] = jnp.zeros_like(db_ref)

        r = lax.broadcasted_iota(jnp.int32, (H, 128), 0)
        c = lax.broadcasted_iota(jnp.int32, (H, 128), 1)
        place = jnp.where(r == c, 1.0, 0.0)
        df = _dot(dfk_ref[...] + dfq_ref[...], place, TN, HIGHEST)
        run = _scan_rows(df, ts, ts, reverse=True) + carry[0:1, :]
        carry[0:1, :] = run[0:1, :]
        daf = run * _sigmoid(-(zs_ref[...] + b_ref[...]))
        daf_ref[...] = daf
        db_ref[0:1, :] += jnp.sum(daf, axis=0, keepdims=True)

    rowsin = lambda a: (a, (H, ts), lambda i: (0, i))
    daf, db = _tiled("fox_decay_bwd", body, S // ts,
                     [rowsin(dfk_rows), rowsin(dfq_rows), _cols(z, ts, 128, CB_SMALL), _full(b_fg128)],
                     [_orow(S, (128,), F32, ts), _oacc((8, 128), F32)],
                     scratch=[pltpu.VMEM((8, 128), F32)], reverse=True)
    return daf, db[0]


FOX_AUG = 80


def _fox_fwd(q_aug, kT_aug, v_aug, tq):
    H, S, da = q_aug.shape
    dv = v_aug.shape[2]
    d = FOX_DIM
    tk = tq // 2
    qscale = (d ** -0.5) * LOG2E

    def kern(q_ref, kT_ref, v_ref, o_ref, lse_ref, qs_ref, s_buf, p_buf, m_scr, acc_scr):
        i = pl.program_id(1)
        col = lax.broadcasted_iota(jnp.int32, (1, da), 1)
        qb = _b16(q_ref[...] * jnp.where(col < d, qscale, 1.0))
        qs_ref[...] = qb

        def keys(t):
            return pl.ds(pl.multiple_of(t * tk, tk), tk)

        def stage(t, slot, mask_off, look_ahead):
            if look_ahead:
                s_buf[1 - slot] = _dot(qb, kT_ref[:, keys(t + 1)])
            pv = _dot(p_buf[1 - slot], v_ref[keys(jnp.maximum(t - 1, 0)), :])

            def scores():
                s = s_buf[slot]
                if mask_off is None:
                    return s
                r = lax.broadcasted_iota(jnp.int32, (tq, tk), 0)
                c = lax.broadcasted_iota(jnp.int32, (tq, tk), 1)
                return jnp.where(c + mask_off <= r, s, NEG)

            m = m_scr[...]
            m_new = jnp.maximum(m, jnp.max(scores(), axis=1, keepdims=True))
            alpha = jnp.exp2(m - m_new)
            p_buf[slot] = _b16(jnp.exp2(scores() - m_new))
            m_scr[...] = m_new
            acc_scr[...] = (acc_scr[...] + pv) * alpha

        s_buf[0] = _dot(qb, kT_ref[:, keys(0)])
        p_buf[1] = jnp.zeros((tq, tk), BF16)
        m_scr[...] = jnp.full((tq, 1), NEG, F32)
        acc_scr[...] = jnp.zeros((tq, dv), F32)

        def pair(n):
            stage(2 * n, 0, None, True)
            stage(2 * n + 1, 1, None, True)

        def quad(m, _):
            pair(2 * m)
            pair(2 * m + 1)
            return 0

        lax.fori_loop(0, i // 2, quad, 0)

        @pl.when(i % 2 == 1)
        def _():
            pair(i - 1)

        stage(2 * i, 0, 0, True)
        stage(2 * i + 1, 1, tk, False)
        acc = acc_scr[...] + _dot(p_buf[1], v_ref[keys(2 * i + 1), :])
        l = acc[:, d:d + 1]
        o_ref[...] = acc[:, :d] / l
        lse_ref[...] = _col_to_row(m_scr[...] + jnp.log(l) * LOG2E)

    return pl.pallas_call(
        kern, name="fox_fwd", grid=(H, S // tq),
        in_specs=[pl.BlockSpec((None, tq, da), lambda h, i: (h, i, 0)),
                  pl.BlockSpec((None, da, S), lambda h, i: (h, 0, 0)),
                  pl.BlockSpec((None, S, dv), lambda h, i: (h, 0, 0))],
        out_specs=[pl.BlockSpec((None, tq, d), lambda h, i: (h, i, 0)),
                   pl.BlockSpec((None, 1, tq), lambda h, i: (h, 0, i)),
                   pl.BlockSpec((None, tq, da), lambda h, i: (h, i, 0))],
        out_shape=[jax.ShapeDtypeStruct((H, S, d), F32), jax.ShapeDtypeStruct((H, 1, S), F32),
                   jax.ShapeDtypeStruct((H, S, da), BF16)],
        scratch_shapes=[pltpu.VMEM((2, tq, tk), F32), pltpu.VMEM((2, tq, tk), BF16),
                        pltpu.VMEM((tq, 1), F32), pltpu.VMEM((tq, dv), F32)],
        compiler_params=_params(("parallel", "arbitrary")),
    )(q_aug, kT_aug, v_aug)


def _fox_bwd(qs, k_aug, kT, v, do, lse_row, delta_row, tq):
    H, S, da = qs.shape
    d = FOX_DIM
    tk = tq
    nq = S // tq
    scale = d ** -0.5

    ts2 = tq // 2
    last = 2 * nq - 1

    def kern(q_ref, k_ref, kT_ref, v_ref, do_ref, lse_ref, dl_ref,
             dqT_ref, dk_ref, dv_ref, dfk_ref, dfq_ref,
             kq_buf, dp_buf, pb_buf, ds_buf, dk_scr, dv_scr, dfk_scr):
        j = pl.program_id(1)

        @pl.when(j == 0)
        def _():
            dqT_ref[...] = jnp.zeros_like(dqT_ref)
            dfq_ref[...] = jnp.zeros_like(dfq_ref)

        kb = k_ref[...]
        kTb = kT_ref[...]
        vb = v_ref[...]
        dk_scr[...] = jnp.zeros_like(dk_scr)
        dv_scr[...] = jnp.zeros_like(dv_scr)
        dfk_scr[...] = jnp.zeros_like(dfk_scr)

        def queries(t):
            return pl.ds(pl.multiple_of(t * ts2, ts2), ts2)

        def products(t, slot):
            rows = queries(t)
            kq_buf[slot] = _dot(kb, q_ref[rows, :], NT)
            dp_buf[slot] = _dot(vb, do_ref[rows, :], NT)

        def pointwise(t, slot, mask_off):
            rows = queries(t)
            sT = kq_buf[slot]
            if mask_off is not None:
                r = lax.broadcasted_iota(jnp.int32, (tk, ts2), 0)
                c = lax.broadcasted_iota(jnp.int32, (tk, ts2), 1)
                sT = jnp.where(r <= c + mask_off, sT, NEG)
            pT = jnp.exp2(sT - lse_ref[:, rows])
            dsT = pT * (dp_buf[slot] - dl_ref[:, rows])
            pb_buf[slot] = _b16(pT)
            ds_buf[slot] = _b16(dsT)
            dfk_scr[...] -= jnp.sum(dsT, axis=1, keepdims=True)
            dfq_ref[:, rows] += jnp.sum(dsT, axis=0, keepdims=True)

        def accumulate(t, slot):
            rows = queries(t)
            dsb = ds_buf[slot]
            dv_scr[...] += _dot(pb_buf[slot], do_ref[rows, :])
            dk_scr[...] += _dot(dsb, q_ref[rows, :])
            dqT_ref[:, rows] += _dot(kTb, dsb) * scale

        def stage(t, slot, mask_off, has_prev):
            products(jnp.minimum(t + 1, last), 1 - slot)
            if has_prev:
                accumulate(t - 1, 1 - slot)
            pointwise(t, slot, mask_off)

        products(2 * j, 0)
        stage(2 * j, 0, 0, False)
        stage(2 * j + 1, 1, ts2, True)

        def pair(n):
            stage(2 * n, 0, None, True)
            stage(2 * n + 1, 1, None, True)

        def quad(m, _):
            pair(j + 1 + 2 * m)
            pair(j + 2 + 2 * m)
            return 0

        n_rest = nq - 1 - j
        lax.fori_loop(0, n_rest // 2, quad, 0)

        @pl.when(n_rest % 2 == 1)
        def _():
            pair(nq - 1)

        accumulate(last, 1)
        dk_ref[...] = dk_scr[:, :d] * LN2
        dv_ref[...] = dv_scr[...]
        dfk_ref[...] = _col_to_row(dfk_scr[...])

    tile = lambda h, j: (h, j, 0)
    whole = lambda h, j: (h, 0, 0)
    rowtile = lambda h, j: (h, 0, j)
    return pl.pallas_call(
        kern, name="fox_bwd", grid=(H, S // tk),
        in_specs=[pl.BlockSpec((None, S, da), whole),
                  pl.BlockSpec((None, tk, da), tile),
                  pl.BlockSpec((None, d, tk), lambda h, j: (h, 0, j)),
                  pl.BlockSpec((None, tk, d), tile),
                  pl.BlockSpec((None, S, d), whole),
                  pl.BlockSpec((None, 1, S), whole),
                  pl.BlockSpec((None, 1, S), whole)],
        out_specs=[pl.BlockSpec((None, d, S), whole),
                   pl.BlockSpec((None, tk, d), tile),
                   pl.BlockSpec((None, tk, d), tile),
                   pl.BlockSpec((None, 1, tk), rowtile),
                   pl.BlockSpec((None, 1, S), whole)],
        out_shape=[jax.ShapeDtypeStruct((H, d, S), F32), jax.ShapeDtypeStruct((H, S, d), F32),
                   jax.ShapeDtypeStruct((H, S, d), F32), jax.ShapeDtypeStruct((H, 1, S), F32),
                   jax.ShapeDtypeStruct((H, 1, S), F32)],
        scratch_shapes=[pltpu.VMEM((2, tk, ts2), F32), pltpu.VMEM((2, tk, ts2), F32),
                        pltpu.VMEM((2, tk, ts2), BF16), pltpu.VMEM((2, tk, ts2), BF16),
                        pltpu.VMEM((tk, da), F32), pltpu.VMEM((tk, d), F32), pltpu.VMEM((tk, 1), F32)],
        compiler_params=_params(("parallel", "arbitrary")),
    )(qs, k_aug, kT, v, do, lse_row, delta_row)


def _heads_major(a, H, d):
    S = a.shape[0]
    return a.reshape(S, H, d).transpose(1, 0, 2)


def _heads_minor(a):
    H, S, d = a.shape
    return a.transpose(1, 0, 2).reshape(S, H * d)


def _lane_pick(x128, lane):
    return x128[:, lane:lane + 1]


def _l2_fwd(y):
    return lax.rsqrt(jnp.sum(y * y, axis=1, keepdims=True) + EPS)


def _gdn_prep(z, conv_w, a128, dt128, ts):
    S = z.shape[0]
    C3 = 3 * WIDTH
    hb = ts // 8

    def body(t, first, ins, outs, scratch):
        x_ref, halo_ref, zs_ref, w_ref, a_ref, dt_ref = ins
        qkv_ref, c_ref, gb_ref, gbT_ref = outs
        halo = jnp.where(t > 0, halo_ref[...], 0.0)
        xe = jnp.concatenate([halo, x_ref[...]], axis=0)
        w = w_ref[...]
        c = w[3:4, :] * xe[8:, :]
        for back in (1, 2, 3):
            c = c + w[3 - back:4 - back, :] * pltpu.roll(xe, back, 0)[8:, :]
        c_ref[...] = c
        y = c * _sigmoid(c)
        for h in range(GDN_HEADS):
            lo = h * GDN_DIM
            yq = y[:, lo:lo + GDN_DIM]
            qkv_ref[:, lo:lo + GDN_DIM] = yq * (_l2_fwd(yq) * (GDN_DIM ** -0.5))
            yk = y[:, WIDTH + lo:WIDTH + lo + GDN_DIM]
            qkv_ref[:, WIDTH + lo:WIDTH + lo + GDN_DIM] = yk * _l2_fwd(yk)
        qkv_ref[:, 2 * WIDTH:] = y[:, 2 * WIDTH:]
        zs = zs_ref[...]
        lane = lax.broadcasted_iota(jnp.int32, zs.shape, 1)
        g = -jnp.exp(a_ref[...]) * _softplus(zs + dt_ref[...])
        G = _scan_rows(g, ts, CHUNK)
        beta = _sigmoid(zs)
        out = jnp.where(lane < 8, pltpu.roll(g, 128 - LANE_BA, 1), jnp.where(lane < LANE_BB, G, beta))
        gb_ref[...] = out
        gbT_ref[...] = _transpose_exact(out)

    x_in = (z, (ts, C3), lambda i: (i, CB_BQKV))
    halo_in = (z, (8, C3), lambda i: (jnp.maximum(i * hb - 1, 0), CB_BQKV))
    return _tiled("gdn_prep", body, S // ts,
                  [x_in, halo_in, _cols(z, ts, 128, CB_SMALL), _full(conv_w), _full(a128), _full(dt128)],
                  [_orow(S, (C3,), F32, ts), _orow(S, (C3,), F32, ts), _orow(S, (128,), F32, ts),
                   ((128, S), F32, (128, ts), lambda i: (0, i))])


def _chunk_masks(nc):
    r = lax.broadcasted_iota(jnp.int32, (nc, CHUNK, CHUNK), 1)
    c = lax.broadcasted_iota(jnp.int32, (nc, CHUNK, CHUNK), 2)
    return c <= r, c < r, c == r


def _chunk_local(qh, kh, vh, Gc, Gr, beta):
    nc = qh.shape[0]
    incl, strict, _ = _chunk_masks(nc)
    gamma = jnp.exp(jnp.where(incl, Gc - Gr, NEG))
    kb = kh * beta
    P = _bdot(_b16(kb), _b16(kh), 2, 2)
    Qk = _bdot(_b16(qh), _b16(kh), 2, 2)
    eG = jnp.exp(Gc)
    Gl = Gc[:, CHUNK - 1:CHUNK, :]
    edec = jnp.exp(Gl - Gc)
    return incl, strict, gamma, kb, P, Qk, eG, edec


def _gdn_local_fwd(qkv, gb, grow, ts):
    S = qkv.shape[0]
    nc = ts // CHUNK

    def body(t, first, ins, outs, scratch):
        q_ref, k_ref, v_ref, gb_ref, gr_ref = ins
        u_ref, w_ref, qd_ref, kd_ref, aqk_ref, T_ref = outs
        gbv = gb_ref[...]
        for h in range(GDN_HEADS):
            lo = h * GDN_DIM
            qh = q_ref[:, lo:lo + GDN_DIM].reshape(nc, CHUNK, GDN_DIM)
            kh = k_ref[:, lo:lo + GDN_DIM].reshape(nc, CHUNK, GDN_DIM)
            vh = v_ref[:, lo:lo + GDN_DIM].reshape(nc, CHUNK, GDN_DIM)
            Gc = _lane_pick(gbv, LANE_BA + h).reshape(nc, CHUNK, 1)
            beta = _lane_pick(gbv, LANE_BB + h).reshape(nc, CHUNK, 1)
            Gr = gr_ref[h].reshape(nc, 1, CHUNK)
            incl, strict, gamma, kb, P, Qk, eG, edec = _chunk_local(qh, kh, vh, Gc, Gr, beta)
            A = jnp.where(strict, P * gamma, 0.0)
            _, _, eye = _chunk_masks(nc)
            T = jnp.where(eye, 1.0, 0.0) - A
            X = A
            for _ in range(5):
                X = _bdot(X, X, 2, 1, PREC_UT)
                T = T + _bdot(T, X, 2, 1, PREC_UT)
            u = _bdot(T, vh * beta, 2, 1, PREC_UT)
            w = _bdot(T, kb * eG, 2, 1, PREC_UT)
            u_ref[:, lo:lo + GDN_DIM] = u.reshape(ts, GDN_DIM)
            w_ref[:, lo:lo + GDN_DIM] = w.reshape(ts, GDN_DIM)
            qd_ref[:, lo:lo + GDN_DIM] = (qh * eG).reshape(ts, GDN_DIM)
            kd_ref[:, lo:lo + GDN_DIM] = (kh * edec).reshape(ts, GDN_DIM)
            aqk_ref[h] = jnp.where(incl, Qk * gamma, 0.0).reshape(ts, CHUNK)
            T_ref[h] = T.reshape(ts, CHUNK)

    wide = _orow(S, (WIDTH,), F32, ts)
    perhead = ((GDN_HEADS, S, CHUNK), F32, (GDN_HEADS, ts, CHUNK), lambda i: (0, i, 0))
    return _tiled("gdn_local_fwd", body, S // ts,
                  [_cols(qkv, ts, WIDTH, 0), _cols(qkv, ts, WIDTH, 1), _cols(qkv, ts, WIDTH, 2),
                   _rows(gb, ts), (grow, (GDN_HEADS, nc, CHUNK), lambda i: (0, i, 0))],
                  [wide, wide, wide, wide, perhead, perhead])


def _gdn_scan_fwd(u, w, qd, kd, aqk, gb, ts):
    S = u.shape[0]
    nc = ts // CHUNK
    N = S // CHUNK

    def body(t, first, ins, outs, scratch):
        u_ref, w_ref, qd_ref, kd_ref, aqk_ref, gb_ref = ins
        o_ref, vn_ref, st_ref = outs
        (state,) = scratch

        @pl.when(first)
        def _():
            state[...] = jnp.zeros_like(state)

        def chunk(c, _):
            r0 = pl.multiple_of(c * CHUNK, CHUNK)
            rows = pl.ds(r0, CHUNK)
            glast = gb_ref[pl.ds(r0 + CHUNK - 1, 1), :]
            heads = range(GDN_HEADS)
            cols = [slice(h * GDN_DIM, (h + 1) * GDN_DIM) for h in heads]
            S_old = [state[h] for h in heads]
            u_h = [u_ref[rows, cols[h]] for h in heads]
            w_h = [_b16(w_ref[rows, cols[h]]) for h in heads]
            qd_h = [_b16(qd_ref[rows, cols[h]]) for h in heads]
            kd_h = [_b16(kd_ref[rows, cols[h]]) for h in heads]
            aqk_h = [_b16(aqk_ref[h, rows, :]) for h in heads]
            S_new, o_h, vn_h = [], [], []
            for h in heads:
                Sb = _b16(S_old[h])
                vn = u_h[h] - _dot(w_h[h], Sb)
                vnb = _b16(vn)
                o_h.append(_dot(qd_h[h], Sb) + _dot(aqk_h[h], vnb))
                egl = jnp.exp(glast[:, LANE_BA + h:LANE_BA + h + 1])
                S_new.append(S_old[h] * egl + _dot(kd_h[h], vnb, TN))
                vn_h.append(vn)
            for h in heads:
                st_ref[c, h] = S_old[h]
                state[h] = S_new[h]
                o_ref[rows, cols[h]] = o_h[h]
                vn_ref[rows, cols[h]] = vn_h[h]
            return 0

        lax.fori_loop(0, nc, chunk, 0)

    wide_in = lambda a: _rows(a, ts)
    wide = _orow(S, (WIDTH,), F32, ts)
    states = ((N, GDN_HEADS, GDN_DIM, GDN_DIM), F32, (nc, GDN_HEADS, GDN_DIM, GDN_DIM),
              lambda i: (i, 0, 0, 0))
    return _tiled("gdn_scan_fwd", body, S // ts,
                  [wide_in(u), wide_in(w), wide_in(qd), wide_in(kd),
                   (aqk, (GDN_HEADS, ts, CHUNK), lambda i: (0, i, 0)), _rows(gb, ts)],
                  [wide, wide, states],
                  scratch=[pltpu.VMEM((GDN_HEADS, GDN_DIM, GDN_DIM), F32)])


def _gdn_scan_bwd(do, w, qd, kd, aqk, vn, states, gb, ts):
    S = do.shape[0]
    nc = ts // CHUNK
    N = S // CHUNK

    def body(t, first, ins, outs, scratch):
        do_ref, w_ref, qd_ref, kd_ref, aqk_ref, vn_ref, st_ref, gb_ref = ins
        du_ref, dw_ref, dqd_ref, dkd_ref, daqk_ref, dgl_ref = outs
        (dstate,) = scratch

        @pl.when(first)
        def _():
            dstate[...] = jnp.zeros_like(dstate)

        r = lax.broadcasted_iota(jnp.int32, (CHUNK, CHUNK), 0)
        cc = lax.broadcasted_iota(jnp.int32, (CHUNK, CHUNK), 1)
        incl = cc <= r
        lane = lax.broadcasted_iota(jnp.int32, (1, 128), 1)

        def chunk(k, _):
            c = nc - 1 - k
            r0 = pl.multiple_of(c * CHUNK, CHUNK)
            rows = pl.ds(r0, CHUNK)
            glast = gb_ref[pl.ds(r0 + CHUNK - 1, 1), :]
            dgl_row = jnp.zeros((1, 128), F32)
            heads = range(GDN_HEADS)
            cols = [slice(h * GDN_DIM, (h + 1) * GDN_DIM) for h in heads]
            S_h = [st_ref[c, h] for h in heads]
            dS_h = [dstate[h] for h in heads]
            do_h = [_b16(do_ref[rows, cols[h]]) for h in heads]
            aqk_h = [_b16(aqk_ref[h, rows, :]) for h in heads]
            vn_h = [_b16(vn_ref[rows, cols[h]]) for h in heads]
            kd_h = [_b16(kd_ref[rows, cols[h]]) for h in heads]
            qd_h = [_b16(qd_ref[rows, cols[h]]) for h in heads]
            w_h = [_b16(w_ref[rows, cols[h]]) for h in heads]
            res = []
            for h in heads:
                Sb, dSb, dob, vnb = _b16(S_h[h]), _b16(dS_h[h]), do_h[h], vn_h[h]
                dvn = _dot(aqk_h[h], dob, TN) + _dot(kd_h[h], dSb)
                dvnb = _b16(dvn)
                daqk = jnp.where(incl, _dot(dob, vnb, NT), 0.0)
                dqd = _dot(dob, Sb, NT)
                dkd = _dot(vnb, dSb, NT)
                dw = -_dot(dvnb, Sb, NT)
                egl = jnp.exp(glast[:, LANE_BA + h:LANE_BA + h + 1])
                dgl = egl * jnp.sum(jnp.sum(dS_h[h] * S_h[h], axis=1, keepdims=True), axis=0,
                                    keepdims=True)
                dgl_row = jnp.where(lane == h, dgl, dgl_row)
                dS_new = _dot(qd_h[h], dob, TN) + egl * dS_h[h] - _dot(w_h[h], dvnb, TN)
                res.append((daqk, dqd, dkd, dw, dvn, dS_new))
            for h in heads:
                daqk, dqd, dkd, dw, dvn, dS_new = res[h]
                daqk_ref[h, rows, :] = daqk
                dqd_ref[rows, cols[h]] = dqd
                dkd_ref[rows, cols[h]] = dkd
                dw_ref[rows, cols[h]] = dw
                du_ref[rows, cols[h]] = dvn
                dstate[h] = dS_new
            dgl_ref[pl.ds(c, 1), :] = dgl_row
            return 0

        lax.fori_loop(0, nc, chunk, 0)

    wide_in = lambda a: _rows(a, ts)
    wide = _orow(S, (WIDTH,), F32, ts)
    perhead_in = lambda a: (a, (GDN_HEADS, ts, CHUNK), lambda i: (0, i, 0))
    perhead = ((GDN_HEADS, S, CHUNK), F32, (GDN_HEADS, ts, CHUNK), lambda i: (0, i, 0))
    return _tiled("gdn_scan_bwd", body, S // ts,
                  [wide_in(do), wide_in(w), wide_in(qd), wide_in(kd), perhead_in(aqk), wide_in(vn),
                   (states, (nc, GDN_HEADS, GDN_DIM, GDN_DIM), lambda i: (i, 0, 0, 0)), _rows(gb, ts)],
                  [wide, wide, wide, wide, perhead, ((N, 128), F32, (nc, 128), lambda i: (i, 0))],
                  scratch=[pltpu.VMEM((GDN_HEADS, GDN_DIM, GDN_DIM), F32)], reverse=True)


def _gdn_local_bwd(qkv, gb, grow, T, du, dw, dqd, dkd, daqk, dgl, ts):
    S = qkv.shape[0]
    nc = ts // CHUNK

    def body(t, first, ins, outs, scratch):
        (q_ref, k_ref, v_ref, gb_ref, gr_ref, T_ref, du_ref, dw_ref, dqd_ref, dkd_ref,
         daqk_ref, dgl_ref) = ins
        dqkv_ref, dgb_ref = outs
        gbv = gb_ref[...]
        dglv = dgl_ref[...]
        lane = lax.broadcasted_iota(jnp.int32, (ts, 128), 1)
        dG_all = jnp.zeros((ts, 128), F32)
        dbeta_all = jnp.zeros((ts, 128), F32)
        for h in range(GDN_HEADS):
            lo = h * GDN_DIM
            cols = slice(lo, lo + GDN_DIM)
            r3 = lambda ref: ref[:, cols].reshape(nc, CHUNK, GDN_DIM)
            qh, kh, vh = r3(q_ref), r3(k_ref), r3(v_ref)
            duh, dwh, dqdh, dkdh = r3(du_ref), r3(dw_ref), r3(dqd_ref), r3(dkd_ref)
            Gc = _lane_pick(gbv, LANE_BA + h).reshape(nc, CHUNK, 1)
            beta = _lane_pick(gbv, LANE_BB + h).reshape(nc, CHUNK, 1)
            Gr = gr_ref[h].reshape(nc, 1, CHUNK)
            Th = T_ref[h].reshape(nc, CHUNK, CHUNK)
            daq = daqk_ref[h].reshape(nc, CHUNK, CHUNK)
            incl, strict, gamma, kb, P, Qk, eG, edec = _chunk_local(qh, kh, vh, Gc, Gr, beta)
            _, _, eye = _chunk_masks(nc)
            vb = vh * beta
            kbg = kb * eG
            dvb = _bdot(Th, duh, 1, 1, PREC_UT)
            dkbg = _bdot(Th, dwh, 1, 1, PREC_UT)
            dT = _bdot(duh, vb, 2, 2, PREC_UT) + _bdot(dwh, kbg, 2, 2, PREC_UT)
            M1 = _bdot(Th, dT, 1, 1, PREC_UT)
            dA = jnp.where(strict, -_bdot(M1, Th, 2, 2, PREC_UT), 0.0)
            dP = dA * gamma
            dQ = daq * gamma
            dgam = (dA * P + daq * Qk) * gamma
            dPb, dQb = _b16(dP), _b16(dQ)
            khb, qhb, kbb = _b16(kh), _b16(qh), _b16(kb)
            dq = _bdot(dQb, khb, 2, 1) + dqdh * eG
            dkb = _bdot(dPb, khb, 2, 1) + dkbg * eG
            dk = (_bdot(dQb, qhb, 1, 1) + _bdot(dPb, kbb, 1, 1) + dkdh * edec + dkb * beta)
            dbeta = (jnp.sum(dkb * kh, axis=2, keepdims=True) + jnp.sum(dvb * vh, axis=2, keepdims=True))
            dv = dvb * beta
            col_as_col = jnp.sum(jnp.where(eye, jnp.sum(dgam, axis=1, keepdims=True), 0.0),
                                 axis=2, keepdims=True)
            kd_term = jnp.sum(dkdh * kh * edec, axis=2, keepdims=True)
            dG = (jnp.sum(dgam, axis=2, keepdims=True) - col_as_col
                  + jnp.sum(dqdh * qh * eG, axis=2, keepdims=True)
                  + jnp.sum(dkbg * kbg, axis=2, keepdims=True) - kd_term)
            dgl_h = dglv[:, h:h + 1].reshape(nc, 1, 1) + jnp.sum(kd_term, axis=1, keepdims=True)
            last = lax.broadcasted_iota(jnp.int32, (nc, CHUNK, 1), 1) == CHUNK - 1
            dG = dG + jnp.where(last, dgl_h, 0.0)
            dqkv_ref[:, cols] = dq.reshape(ts, GDN_DIM)
            dqkv_ref[:, WIDTH + lo:WIDTH + lo + GDN_DIM] = dk.reshape(ts, GDN_DIM)
            dqkv_ref[:, 2 * WIDTH + lo:2 * WIDTH + lo + GDN_DIM] = dv.reshape(ts, GDN_DIM)
            dG_all = jnp.where(lane == LANE_BA + h, dG.reshape(ts, 1), dG_all)
            dbeta_all = jnp.where(lane == LANE_BB + h, dbeta.reshape(ts, 1), dbeta_all)
        dg_all = _scan_rows(dG_all, ts, CHUNK, reverse=True)
        dgb_ref[...] = jnp.where(lane < LANE_BB, dg_all, dbeta_all)

    wide_in = lambda a: _rows(a, ts)
    perhead_in = lambda a: (a, (GDN_HEADS, ts, CHUNK), lambda i: (0, i, 0))
    return _tiled("gdn_local_bwd", body, S // ts,
                  [_cols(qkv, ts, WIDTH, 0), _cols(qkv, ts, WIDTH, 1), _cols(qkv, ts, WIDTH, 2),
                   _rows(gb, ts), (grow, (GDN_HEADS, nc, CHUNK), lambda i: (0, i, 0)), perhead_in(T),
                   wide_in(du), wide_in(dw), wide_in(dqd), wide_in(dkd), perhead_in(daqk),
                   (dgl, (nc, 128), lambda i: (i, 0))],
                  [_orow(S, (3 * WIDTH,), F32, ts), _orow(S, (128,), F32, ts)])


def _gdn_prep_bwd(dqkv, dgb, cpre, z, conv_w, a128, dt128, ts):
    S = z.shape[0]
    C3 = 3 * WIDTH
    hb = ts // 8
    n_tiles = S // ts

    def dpre(dq, c):
        y, dsil = _silu_and_grad(c)
        parts = []
        for h in range(GDN_HEADS):
            lo = h * GDN_DIM
            yq = y[:, lo:lo + GDN_DIM]
            rq = _l2_fwd(yq)
            nq = yq * rq
            dn = dq[:, lo:lo + GDN_DIM] * (GDN_DIM ** -0.5)
            parts.append(rq * (dn - nq * jnp.sum(dn * nq, axis=1, keepdims=True)))
        for h in range(GDN_HEADS):
            lo = WIDTH + h * GDN_DIM
            yk = y[:, lo:lo + GDN_DIM]
            rk = _l2_fwd(yk)
            nk = yk * rk
            dn = dq[:, lo:lo + GDN_DIM]
            parts.append(rk * (dn - nk * jnp.sum(dn * nk, axis=1, keepdims=True)))
        parts.append(dq[:, 2 * WIDTH:])
        return jnp.concatenate(parts, axis=1) * dsil

    def body(t, first, ins, outs, scratch):
        (dq_ref, dqn_ref, c_ref, cn_ref, x_ref, xp_ref, zs_ref, dgb_ref, w_ref, a_ref, dt_ref) = ins
        dx_ref, dzs_ref, dw_ref, dad_ref = outs

        @pl.when(first)
        def _():
            dw_ref[...] = jnp.zeros_like(dw_ref)
            dad_ref[...] = jnp.zeros_like(dad_ref)

        dc = dpre(dq_ref[...], c_ref[...])
        dcn = jnp.where(t < n_tiles - 1, dpre(dqn_ref[...], cn_ref[...]), 0.0)
        dce = jnp.concatenate([dc, dcn], axis=0)
        w = w_ref[...]
        dx = w[3:4, :] * dc
        for back in (1, 2, 3):
            dx = dx + w[3 - back:4 - back, :] * pltpu.roll(dce, ts + 8 - back, 0)[:ts, :]
        dx_ref[...] = dx
        halo = jnp.where(t > 0, xp_ref[...], 0.0)
        xe = jnp.concatenate([halo, x_ref[...]], axis=0)
        dw_ref[3:4, :] += jnp.sum(dc * xe[8:, :], axis=0, keepdims=True)
        for back in (1, 2, 3):
            dw_ref[3 - back:4 - back, :] += jnp.sum(dc * pltpu.roll(xe, back, 0)[8:, :], axis=0,
                                                     keepdims=True)
        zs = zs_ref[...]
        dgb = dgb_ref[...]
        lane = lax.broadcasted_iota(jnp.int32, zs.shape, 1)
        arg = zs + dt_ref[...]
        nega = -jnp.exp(a_ref[...])
        dba = dgb * nega * _sigmoid(arg)
        beta = _sigmoid(zs)
        dbb = dgb * beta * (1.0 - beta)
        dzs_ref[...] = jnp.where((lane >= LANE_BA) & (lane < LANE_BB), dba,
                                 jnp.where((lane >= LANE_BB) & (lane < LANE_BB + 4), dbb, 0.0))
        dad_ref[0:1, :] += jnp.sum(dgb * nega * _softplus(arg), axis=0, keepdims=True)
        dad_ref[1:2, :] += jnp.sum(dba, axis=0, keepdims=True)

    nxt = lambda i: (jnp.minimum((i + 1) * hb, S // 8 - 1), 0)
    prv = lambda i: (jnp.maximum(i * hb - 1, 0), CB_BQKV)
    return _tiled("gdn_prep_bwd", body, n_tiles,
                  [_rows(dqkv, ts), (dqkv, (8, C3), nxt), _rows(cpre, ts), (cpre, (8, C3), nxt),
                   (z, (ts, C3), lambda i: (i, CB_BQKV)), (z, (8, C3), prv),
                   _cols(z, ts, 128, CB_SMALL), _rows(dgb, ts), _full(conv_w), _full(a128), _full(dt128)],
                  [_orow(S, (C3,), F32, ts), _orow(S, (128,), F32, ts), _oacc((8, C3), F32),
                   _oacc((8, 128), F32)])


def _mem_attn_fwd(z, mk, mv, ts):
    S = z.shape[0]

    def body(t, first, ins, outs, scratch):
        q_ref, mk_ref, mv_ref = ins
        (o_ref,) = outs
        for h in range(MEM_HEADS):
            cols = slice(h * MEM_DIM, (h + 1) * MEM_DIM)
            s = _dot(_b16(q_ref[:, cols]), _b16(mk_ref[:, cols]), NT) * (MEM_DIM ** -0.5)
            m = jnp.max(s, axis=1, keepdims=True)
            p = jnp.exp(s - m)
            p = p / jnp.sum(p, axis=1, keepdims=True)
            o_ref[:, cols] = _dot(_b16(p), _b16(mv_ref[:, cols]))

    (o,) = _tiled("mem_attn_fwd", body, S // ts, [_cols(z, ts, WIDTH, CB_MQ), _full(mk), _full(mv)],
                  [_orow(S, (WIDTH,), F32, ts)])
    return o


def _mem_attn_bwd(do, z, mk, mv, ts):
    S = z.shape[0]
    M = mk.shape[0]

    def body(t, first, ins, outs, scratch):
        do_ref, q_ref, mk_ref, mv_ref = ins
        dq_ref, dmk_ref, dmv_ref = outs

        @pl.when(first)
        def _():
            dmk_ref[...] = jnp.zeros_like(dmk_ref)
            dmv_ref[...] = jnp.zeros_like(dmv_ref)

        scale = MEM_DIM ** -0.5
        for h in range(MEM_HEADS):
            cols = slice(h * MEM_DIM, (h + 1) * MEM_DIM)
            qb = _b16(q_ref[:, cols])
            kb = _b16(mk_ref[:, cols])
            dob = _b16(do_ref[:, cols])
            s = _dot(qb, kb, NT) * scale
            m = jnp.max(s, axis=1, keepdims=True)
            p = jnp.exp(s - m)
            p = p / jnp.sum(p, axis=1, keepdims=True)
            dmv_ref[:, cols] += _dot(_b16(p), dob, TN)
            dp = _dot(dob, _b16(mv_ref[:, cols]), NT)
            ds = p * (dp - jnp.sum(dp * p, axis=1, keepdims=True)) * scale
            dsb = _b16(ds)
            dq_ref[:, cols] = _dot(dsb, kb)
            dmk_ref[:, cols] += _dot(dsb, qb, TN)

    return _tiled("mem_attn_bwd", body, S // ts,
                  [_rows(do, ts), _cols(z, ts, WIDTH, CB_MQ), _full(mk), _full(mv)],
                  [_orow(S, (WIDTH,), F32, ts), _oacc((M, WIDTH), F32), _oacc((M, WIDTH), F32)])


def _head_norm(ob, g):
    xs, rs = [], []
    for h in range(GDN_HEADS):
        o = ob[:, h * GDN_DIM:(h + 1) * GDN_DIM]
        r = lax.rsqrt(jnp.mean(o * o, axis=1, keepdims=True) + EPS)
        xs.append(o * r)
        rs.append(r)
    return xs, rs


def _merge_fwd(x, z, o_a, o_b, o_m, gdn_g, b_merge, wb, wout, ts):
    S, D = x.shape

    def body(t, first, ins, outs, scratch):
        (x_ref, g_ref, oa_ref, az_ref, ob_ref, bz_ref, om_ref, mz_ref, gg_ref, bm_ref, wb_ref,
         wo_ref) = ins
        xo_ref, ya_ref, yb_ref, ym_ref, mg_ref = outs
        ya = oa_ref[...] * _silu_and_grad(az_ref[...])[0]
        xs, _ = _head_norm(ob_ref[...], None)
        nb = jnp.concatenate([xh * gg_ref[...] for xh in xs], axis=1)
        yb = nb * _silu_and_grad(bz_ref[...])[0]
        ym = om_ref[...] * _silu_and_grad(mz_ref[...])[0]
        merged = jnp.zeros((ts, D), F32)
        for n, (y, y_ref) in enumerate(((ya, ya_ref), (yb, yb_ref), (ym, ym_ref))):
            yb16 = _b16(y)
            y_ref[...] = yb16
            gate = _sigmoid(g_ref[:, n * D:(n + 1) * D] + bm_ref[:, n * D:(n + 1) * D])
            merged = merged + gate * _dot(yb16, wb_ref[n])
        mb = _b16(merged)
        mg_ref[...] = mb
        xo_ref[...] = x_ref[...] + _dot(mb, wo_ref[...])

    half = lambda a: _rows(a, ts)
    return _tiled("merge_fwd", body, S // ts,
                  [_rows(x, ts), _cols(z, ts, 3 * D, CB_GATES), half(o_a), _cols(z, ts, WIDTH, CB_AZ),
                   half(o_b), _cols(z, ts, WIDTH, CB_BZ), half(o_m), _cols(z, ts, WIDTH, CB_MZ),
                   _full(gdn_g.reshape(1, GDN_DIM)), _full(b_merge.reshape(1, 3 * D)), _full(wb), _full(wout)],
                  [_orow(S, (D,), F32, ts), _orow(S, (WIDTH,), BF16, ts), _orow(S, (WIDTH,), BF16, ts),
                   _orow(S, (WIDTH,), BF16, ts), _orow(S, (D,), BF16, ts)])


def _merge_bwd(dout, z, o_a, o_b, o_m, ya, yb, ym, gdn_g, b_merge, wb, wout, hsum, ts):
    S, D = dout.shape

    def body(t, first, ins, outs, scratch):
        (do_ref, g_ref, oa_ref, az_ref, ob_ref, bz_ref, om_ref, mz_ref, ya_ref, yb_ref, ym_ref,
         gg_ref, bm_ref, wb_ref, wo_ref, hs_ref) = ins
        (dg_ref, dpa_ref, dpb_ref, dpm_ref, doa_ref, dob_ref, dom_ref, daz_ref, dbz_ref, dmz_ref,
         dl_ref, dbm_ref, dgg_ref) = outs

        @pl.when(first)
        def _():
            dbm_ref[...] = jnp.zeros_like(dbm_ref)
            dgg_ref[...] = jnp.zeros_like(dgg_ref)

        dmerged = _dot(_b16(do_ref[...]), wo_ref[...], NT)
        dys = []
        for n, (y_ref, dp_ref) in enumerate(((ya_ref, dpa_ref), (yb_ref, dpb_ref), (ym_ref, dpm_ref))):
            sl = slice(n * D, (n + 1) * D)
            gate = _sigmoid(g_ref[:, sl] + bm_ref[:, sl])
            proj = _dot(y_ref[...], wb_ref[n])
            dproj = _b16(gate * dmerged)
            dp_ref[...] = dproj
            dgp = dmerged * proj * gate * (1.0 - gate)
            dg_ref[:, sl] = dgp.astype(dg_ref.dtype)
            dbm_ref[0:1, sl] += jnp.sum(dgp, axis=0, keepdims=True)
            dys.append(_dot(dproj, wb_ref[n], NT))
        dya, dyb, dym = dys
        sa, dsa = _silu_and_grad(az_ref[...])
        oa = oa_ref[...]
        doa = dya * sa
        doa_ref[...] = doa
        daz_ref[...] = dya * oa * dsa
        dl_ref[...] = _dot(hs_ref[...], doa * oa, NT, HIGHEST)
        sm, dsm = _silu_and_grad(mz_ref[...])
        dom_ref[...] = dym * sm
        dmz_ref[...] = dym * om_ref[...] * dsm
        sb, dsb = _silu_and_grad(bz_ref[...])
        xs, rs = _head_norm(ob_ref[...], None)
        gg = gg_ref[...]
        dgg = jnp.zeros((1, GDN_DIM), F32)
        for h in range(GDN_HEADS):
            cols = slice(h * GDN_DIM, (h + 1) * GDN_DIM)
            dn = dyb[:, cols] * sb[:, cols]
            dbz_ref[:, cols] = dyb[:, cols] * (xs[h] * gg) * dsb[:, cols]
            dgg = dgg + jnp.sum(dn * xs[h], axis=0, keepdims=True)
            dxh = dn * gg
            dob_ref[:, cols] = rs[h] * (dxh - xs[h] * jnp.mean(dxh * xs[h], axis=1, keepdims=True))
        dgg_ref[0:1, :] += dgg

    half = lambda a: _rows(a, ts)
    w512 = lambda dt: _orow(S, (WIDTH,), dt, ts)
    return _tiled("merge_bwd", body, S // ts,
                  [_rows(dout, ts), _cols(z, ts, 3 * D, CB_GATES), half(o_a), _cols(z, ts, WIDTH, CB_AZ),
                   half(o_b), _cols(z, ts, WIDTH, CB_BZ), half(o_m), _cols(z, ts, WIDTH, CB_MZ),
                   half(ya), half(yb), half(ym), _full(gdn_g.reshape(1, GDN_DIM)),
                   _full(b_merge.reshape(1, 3 * D)), _full(wb), _full(wout), _full(hsum)],
                  [_orow(S, (3 * D,), BF16, ts), _orow(S, (D,), BF16, ts), _orow(S, (D,), BF16, ts),
                   _orow(S, (D,), BF16, ts), w512(F32), w512(F32), w512(F32), w512(F32), w512(F32),
                   w512(F32), ((128, S), F32, (128, ts), lambda i: (0, i)), _oacc((8, 3 * D), F32),
                   _oacc((8, GDN_DIM), F32)])


def _to_aligned(w):
    sizes = (512, 512, 512, 8, 512, 512, 512, 512, 4, 4, 512, 512, 512, 3072)
    names = ("aq", "ak", "av", "af", "az", "bq", "bk", "bv", "ba", "bb", "bz", "mq", "mz", "gates")
    p, off = {}, 0
    for n, s in zip(names, sizes):
        p[n] = w[..., off:off + s]
        off += s
    pad = jnp.zeros(w.shape[:-1] + (128 - 16,), w.dtype)
    return jnp.concatenate([p["gates"], p["bq"], p["bk"], p["bv"], p["aq"], p["ak"], p["av"], p["az"],
                            p["bz"], p["mq"], p["mz"], p["af"], p["ba"], p["bb"], pad], axis=-1)


def _from_aligned(w):
    c = lambda lo, n: w[..., lo:lo + n]
    gates, bq, bk, bv = c(0, 3072), c(3072, 512), c(3584, 512), c(4096, 512)
    aq, ak, av, az = c(4608, 512), c(5120, 512), c(5632, 512), c(6144, 512)
    bz, mq, mz = c(6656, 512), c(7168, 512), c(7680, 512)
    af, ba, bb = c(8192, 8), c(8200, 4), c(8204, 4)
    return jnp.concatenate([aq, ak, av, af, az, bq, bk, bv, ba, bb, bz, mq, mz, gates], axis=-1)


def _lanes128(v, lane0):
    return jnp.pad(v.astype(F32)[None, :], ((0, 0), (lane0, 128 - lane0 - v.shape[0])))


def _tiles(S):
    ts = min(512, S // 2)
    return dict(ts=ts, ts_small=min(256, S // 2), tq=min(512, S // 4), tq_fwd=min(1024, S // 2))


def _layer_fwd(x, mem, p):
    S = x.shape[0]
    tl = _tiles(S)
    ts, tss, tq = tl["ts"], tl["ts_small"], tl["tq"]
    h, rstd = _rms_fwd("norm_fwd", x, p["norm_g"], ts)
    z = _mm("in_proj", h, p["w_in_al"], tn=1664)

    b_fg128 = _lanes128(p["b_fg"], LANE_AF)
    fT, f_hi, f_mid, f_lo = _fox_decay(z, b_fg128, ts)
    aq = z[:, CB_AQ * WIDTH:(CB_AQ + 1) * WIDTH]
    ak = z[:, CB_AK * WIDTH:(CB_AK + 1) * WIDTH]
    av = z[:, CB_AV * WIDTH:(CB_AV + 1) * WIDTH]
    q32 = _heads_major(aq, FOX_HEADS, FOX_DIM)
    kh = _heads_major(ak, FOX_HEADS, FOX_DIM).astype(BF16)
    vh = _heads_major(av, FOX_HEADS, FOX_DIM).astype(BF16)
    khT = kh.transpose(0, 2, 1)
    piecesT = jnp.stack([f[:FOX_HEADS] for f in (f_hi, f_mid, f_lo)], axis=1)
    pieces = piecesT.transpose(0, 2, 1)
    ones3 = jnp.ones((FOX_HEADS, S, 3), BF16)
    padk = jnp.zeros((FOX_HEADS, S, FOX_AUG - FOX_DIM - 6), BF16)
    q_aug = jnp.concatenate([q32, pieces.astype(F32), ones3.astype(F32), padk.astype(F32)], axis=-1)
    k_aug = jnp.concatenate([kh, ones3, -pieces, padk], axis=-1)
    kT_aug = jnp.concatenate([khT, ones3.transpose(0, 2, 1), -piecesT, padk.transpose(0, 2, 1)], axis=1)
    v_aug = jnp.concatenate([vh, ones3[:, :, :1], jnp.zeros((FOX_HEADS, S, 128 - FOX_DIM - 1), BF16)],
                            axis=-1)
    o_h, lse, qs = _fox_fwd(q_aug, kT_aug, v_aug, tl["tq_fwd"])
    o_a = _heads_minor(o_h)

    a128 = _lanes128(p["a_log"], LANE_BA)
    dt128 = _lanes128(p["dt_bias"], LANE_BA)
    qkv, cpre, gb, gbT = _gdn_prep(z, p["conv_w"], a128, dt128, ts)
    grow = gbT[LANE_BA:LANE_BA + GDN_HEADS].reshape(GDN_HEADS, S // CHUNK, CHUNK)
    u, w, qd, kd, aqk, T = _gdn_local_fwd(qkv, gb, grow, ts)
    o_b, vn, states = _gdn_scan_fwd(u, w, qd, kd, aqk, gb, ts)

    mem_h, mem_r = _rms_fwd("mem_norm_fwd", mem, p["mem_norm_g"], mem.shape[0])
    mkv = _mm("mem_kv", mem_h, p["w_mem_kv"])
    mk, mv = mkv[:, :WIDTH], mkv[:, WIDTH:]
    o_m = _mem_attn_fwd(z, mk, mv, ts)

    x_next, ya, yb, ym, merged = _merge_fwd(x, z, o_a, o_b, o_m, p["gdn_norm_g"], p["b_merge"],
                                            p["w_branch"], p["w_out"], tss)
    saved = dict(x=x, h=h, rstd=rstd, z=z, b_fg128=b_fg128, qs=qs, k_aug=k_aug, khT=khT, vh=vh, lse=lse, o_a=o_a, a128=a128, dt128=dt128, qkv=qkv, cpre=cpre, gb=gb,
                 grow=grow, w=w, qd=qd, kd=kd, aqk=aqk, T=T, o_b=o_b, vn=vn, states=states,
                 mem_h=mem_h, mem_r=mem_r, mk=mk, mv=mv, o_m=o_m, ya=ya, yb=yb, ym=ym, merged=merged)
    return x_next, saved


def _layer_bwd(dout, mem, p, s):
    S = dout.shape[0]
    tl = _tiles(S)
    ts, tss, tq = tl["ts"], tl["ts_small"], tl["tq"]
    z = s["z"]
    hsum = (jnp.arange(128)[:, None] == jnp.arange(WIDTH)[None, :] // FOX_DIM).astype(F32)
    (dgates, dpa, dpb, dpm, do_a, do_b, do_m, daz, dbz, dmz, deltaT, db_merge, dgdn_g) = _merge_bwd(
        dout, z, s["o_a"], s["o_b"], s["o_m"], s["ya"], s["yb"], s["ym"], p["gdn_norm_g"],
        p["b_merge"], p["w_branch"], p["w_out"], hsum, tss)
    g = {}
    g["b_merge"] = db_merge[0]
    g["gdn_norm_g"] = dgdn_g[0]
    g["w_out"] = _mm("dw_out", s["merged"], dout, ta=True)
    g["w_branch"] = jnp.stack([_mm("dw_branch", y, dp, ta=True)
                               for y, dp in ((s["ya"], dpa), (s["yb"], dpb), (s["ym"], dpm))])

    do_h = _heads_major(do_a, FOX_HEADS, FOX_DIM).astype(BF16)
    delta_row = deltaT[:FOX_HEADS, None, :]
    dqT, dk_h, dv_h, dfk, dfq = _fox_bwd(s["qs"], s["k_aug"], s["khT"], s["vh"], do_h, s["lse"],
                                         delta_row, tq)
    daq = _heads_minor(dqT.transpose(0, 2, 1))
    dak = _heads_minor(dk_h)
    dav = _heads_minor(dv_h)
    daf128, db_fg = _fox_decay_bwd(dfk[:, 0, :], dfq[:, 0, :], z, s["b_fg128"], ts)
    g["b_fg"] = db_fg[:FOX_HEADS]

    du, dw, dqd, dkd, daqk, dgl = _gdn_scan_bwd(do_b, s["w"], s["qd"], s["kd"], s["aqk"], s["vn"],
                                                s["states"], s["gb"], ts)
    dqkv, dgb = _gdn_local_bwd(s["qkv"], s["gb"], s["grow"], s["T"], du, dw, dqd, dkd, daqk, dgl, ts)
    dbqkv, dzs_b, dconv, dad = _gdn_prep_bwd(dqkv, dgb, s["cpre"], z, p["conv_w"], s["a128"],
                                             s["dt128"], ts)
    g["conv_w"] = dconv[:4]
    g["a_log"] = dad[0, LANE_BA:LANE_BA + GDN_HEADS]
    g["dt_bias"] = dad[1, LANE_BA:LANE_BA + GDN_HEADS]

    dmq, dmk, dmv = _mem_attn_bwd(do_m, z, s["mk"], s["mv"], ts)
    dmkv = jnp.concatenate([dmk, dmv], axis=1)
    g["w_mem_kv"] = _mm("dw_mem_kv", s["mem_h"], dmkv, ta=True)
    dmem_h = _mm("dmem_h", dmkv, p["w_mem_kv"], tb=True)
    M = mem.shape[0]
    _, g["mem_norm_g"] = _rms_bwd("mem_norm_bwd", dmem_h, mem, s["mem_r"], p["mem_norm_g"],
                                  jnp.zeros_like(mem), M)

    lane = jnp.arange(128)[None, :]
    dsmall = jnp.where(lane < 8, daf128, dzs_b)
    dz = jnp.concatenate([dgates, _b16(dbqkv), _b16(daq), _b16(dak), _b16(dav), _b16(daz), _b16(dbz),
                          _b16(dmq), _b16(dmz), _b16(dsmall)], axis=1)
    g["w_in_al"] = _mm("dw_in", s["h"], dz, ta=True, tn=1664)
    dh = _mm("dh", dz, p["w_in_al"], tb=True, tk=1664)
    dx, g["norm_g"] = _rms_bwd("norm_bwd", dh, s["x"], s["rstd"], p["norm_g"], dout, ts)
    return dx, g


def _local_step(x, mem, layers, final_norm_g, loss_target):
    S = x.shape[0]
    saves = []
    cur = x
    for p in layers:
        cur, sv = _layer_fwd(cur, mem, p)
        saves.append(sv)
    dx, dgf, loss_lanes = _loss_head(cur, final_norm_g, loss_target, _tiles(S)["ts"])
    grads = [None] * len(layers)
    for l in reversed(range(len(layers))):
        dx, grads[l] = _layer_bwd(dx, mem, layers[l], saves[l])
    return loss_lanes, dx, grads, dgf


HBM_SPEC = pl.BlockSpec(memory_space=pltpu.HBM)


def _mesh_pos():
    return lax.axis_index("x"), lax.axis_index("y"), lax.axis_index("c")


def _comm_call(name, body, arrays, out_shapes, n_remote, n_local):
    n = len(arrays)

    def kern(*refs):
        body(refs[:n], refs[n:2 * n], refs[2 * n], refs[2 * n + 1], refs[2 * n + 2])

    return pl.pallas_call(
        kern, name=name, out_shape=out_shapes, in_specs=[HBM_SPEC] * n, out_specs=[HBM_SPEC] * n,
        scratch_shapes=[pltpu.SemaphoreType.DMA((n_remote,)), pltpu.SemaphoreType.DMA((n_remote,)),
                        pltpu.SemaphoreType.DMA((max(n_local, 1),))],
    )(*arrays)


def _remote(src, dst, send_sems, recv_sems, k, to):
    return pltpu.make_async_remote_copy(src_ref=src, dst_ref=dst, send_sem=send_sems.at[k],
                                        recv_sem=recv_sems.at[k], device_id=to, device_id_type=MESH_ID)


def _other_chips(mx, my):
    return [(1 - mx, my), (mx, 1 - my), (1 - mx, 1 - my)]


def _gather_chips(name, shards):
    n = len(shards)

    def body(ins, outs, send_sems, recv_sems, local_sems):
        mx, my, mc = _mesh_pos()
        me = 2 * mx + my
        sibling = (mx, my, 1 - mc)
        chips = _other_chips(mx, my)
        locals_, sends = [], []
        for a in range(n):
            lc = pltpu.make_async_copy(ins[a], outs[a].at[me], local_sems.at[a])
            lc.start()
            locals_.append(lc)
            for k, (px, py) in enumerate(chips):
                cp = _remote(ins[a].at[mc], outs[a].at[me, mc], send_sems, recv_sems, 6 * a + k,
                             (px, py, mc))
                cp.start()
                sends.append(cp)
        for a in range(n):
            for k, (px, py) in enumerate(chips):
                j = 2 * px + py
                _remote(ins[a].at[mc], outs[a].at[j, mc], send_sems, recv_sems, 6 * a + k,
                        (px, py, mc)).wait_recv()
                cp = _remote(outs[a].at[j, mc], outs[a].at[j, mc], send_sems, recv_sems, 6 * a + 3 + k,
                             sibling)
                cp.start()
                sends.append(cp)
        for a in range(n):
            for k, (px, py) in enumerate(chips):
                j = 2 * px + py
                _remote(outs[a].at[j, 1 - mc], outs[a].at[j, 1 - mc], send_sems, recv_sems,
                        6 * a + 3 + k, sibling).wait_recv()
        for cp in sends:
            cp.wait_send()
        for lc in locals_:
            lc.wait()

    shapes = [jax.ShapeDtypeStruct((N_CHIPS,) + s.shape, s.dtype) for s in shards]
    return _comm_call(name, body, shards, shapes, 6 * n, n)


def _sibling_swap(gs):
    n = len(gs)

    def body(ins, outs, send_sems, recv_sems, local_sems):
        mx, my, mc = _mesh_pos()
        sends = []
        for a in range(n):
            cp = _remote(ins[a].at[:, 1 - mc], outs[a], send_sems, recv_sems, a, (mx, my, 1 - mc))
            cp.start()
            sends.append(cp)
        for cp in sends:
            cp.wait()

    shapes = [jax.ShapeDtypeStruct((g.shape[0],) + g.shape[2:], g.dtype) for g in gs]
    return _comm_call("grad_sibling_swap", body, gs, shapes, n, 0)


def _chip_exchange(ps):
    n = len(ps)

    def body(ins, outs, send_sems, recv_sems, local_sems):
        mx, my, mc = _mesh_pos()
        me = 2 * mx + my
        chips = _other_chips(mx, my)
        locals_, sends = [], []
        for a in range(n):
            lc = pltpu.make_async_copy(ins[a].at[me], outs[a].at[me], local_sems.at[a])
            lc.start()
            locals_.append(lc)
            for k, (px, py) in enumerate(chips):
                cp = _remote(ins[a].at[2 * px + py], outs[a].at[me], send_sems, recv_sems, 3 * a + k,
                             (px, py, mc))
                cp.start()
                sends.append(cp)
        for a in range(n):
            for k, (px, py) in enumerate(chips):
                _remote(ins[a].at[me], outs[a].at[2 * px + py], send_sems, recv_sems, 3 * a + k,
                        (px, py, mc)).wait_recv()
        for cp in sends:
            cp.wait_send()
        for lc in locals_:
            lc.wait()

    shapes = [jax.ShapeDtypeStruct(p.shape, p.dtype) for p in ps]
    return _comm_call("grad_chip_exchange", body, ps, shapes, 3 * n, n)


def _sibling_gather(hs):
    n = len(hs)

    def body(ins, outs, send_sems, recv_sems, local_sems):
        mx, my, mc = _mesh_pos()
        locals_, sends = [], []
        for a in range(n):
            lc = pltpu.make_async_copy(ins[a], outs[a].at[mc], local_sems.at[a])
            lc.start()
            locals_.append(lc)
            cp = _remote(ins[a], outs[a].at[mc], send_sems, recv_sems, a, (mx, my, 1 - mc))
            cp.start()
            sends.append(cp)
        for a in range(n):
            _remote(ins[a], outs[a].at[1 - mc], send_sems, recv_sems, a, (mx, my, 1 - mc)).wait_recv()
        for cp in sends:
            cp.wait_send()
        for lc in locals_:
            lc.wait()

    shapes = [jax.ShapeDtypeStruct((2,) + h.shape, h.dtype) for h in hs]
    return _comm_call("grad_sibling_gather", body, hs, shapes, n, n)


def _add_pairs(a, b, tr, out_dtype):
    n, H, C = a.shape

    def kern(a_ref, b_ref, o_ref):
        o_ref[...] = (a_ref[...] + b_ref[...]).astype(o_ref.dtype)

    spec = pl.BlockSpec((None, tr, C), lambda j, i: (j, i, 0))
    return pl.pallas_call(
        kern, name="grad_pair_sum", grid=(n, H // tr), in_specs=[spec, spec], out_specs=spec,
        out_shape=jax.ShapeDtypeStruct((n, H, C), out_dtype),
        compiler_params=_params(("parallel", "parallel")),
    )(a, b)


def _sum_slots(r4, tr):
    n, H, C = r4.shape

    def kern(r_ref, o_ref):
        f = lambda k: r_ref[k].astype(F32)
        o_ref[...] = ((f(0) + f(1)) + f(2)) + f(3)

    return pl.pallas_call(
        kern, name="grad_chip_sum", grid=(H // tr,),
        in_specs=[pl.BlockSpec((n, tr, C), lambda i: (0, i, 0))],
        out_specs=pl.BlockSpec((tr, C), lambda i: (i, 0)),
        out_shape=jax.ShapeDtypeStruct((H, C), F32),
        compiler_params=_params(("parallel",)),
    )(r4)


def _adamw(w, g, m, v, tr):
    R, C = w.shape
    c1 = 1.0 - ADAM_B1
    c2 = 1.0 - ADAM_B2
    bc1 = 1.0 - ADAM_B1 ** ADAM_STEP
    bc2 = 1.0 - ADAM_B2 ** ADAM_STEP

    def kern(w_ref, g_ref, m_ref, v_ref, d_ref, mo_ref, vo_ref):
        gv = g_ref[...]
        mn = ADAM_B1 * m_ref[...] + c1 * gv
        vn = ADAM_B2 * v_ref[...] + c2 * (gv * gv)
        m_hat = mn / bc1
        v_hat = vn / bc2
        d_ref[...] = -ADAM_LR * (m_hat / (jnp.sqrt(v_hat) + ADAM_EPS) + ADAM_WD * w_ref[...])
        mo_ref[...] = mn
        vo_ref[...] = vn

    spec = pl.BlockSpec((tr, C), lambda i: (i, 0))
    shape = jax.ShapeDtypeStruct((R, C), F32)
    return pl.pallas_call(
        kern, name="adamw", grid=(R // tr,), in_specs=[spec] * 4, out_specs=[spec] * 3,
        out_shape=[shape] * 3, compiler_params=_params(("parallel",)),
    )(w, g, m, v)


PACK_COLS = 1024
PACK_ROWS = 512
W_SHARD = N_IN // N_CHIPS
SLAB = ("conv_w", "w_mem_kv", "w_branch", "w_out")
SMALL =("norm_g", "b_fg", "b_merge", "a_log", "dt_bias", "gdn_norm_g", "mem_norm_g", "final_norm_g")
ALL_WEIGHTS = ("norm_g", "w_in", "b_fg", "b_merge", "conv_w", "a_log", "dt_bias", "gdn_norm_g",
               "mem_norm_g", "w_mem_kv", "w_branch", "w_out", "final_norm_g")
SHARD_AXIS = {"w_in": 2, "conv_w": 2, "w_mem_kv": 1, "w_branch": 3, "w_out": 1}


def _pack(arrays, row_multiple):
    flat = jnp.concatenate([a.reshape(-1) for a in arrays])
    n = flat.shape[0]
    rows = -(-n // PACK_COLS)
    rows = -(-rows // row_multiple) * row_multiple
    flat = jnp.pad(flat, (0, rows * PACK_COLS - n))
    return flat.reshape(rows, PACK_COLS)


def _unpack(slab, shapes):
    out, off = [], 0
    for shp in shapes:
        n = 1
        for d in shp:
            n *= d
        r0, r1 = off // PACK_COLS, -(-(off + n) // PACK_COLS)
        rows = slab[r0:r1].reshape(-1)
        out.append(rows[off - r0 * PACK_COLS:off - r0 * PACK_COLS + n].reshape(shp))
        off += n
    return out


def _shard_of(full, name, j):
    ax = SHARD_AXIS[name]
    n = full.shape[ax] // N_CHIPS
    return lax.slice_in_dim(full, j * n, (j + 1) * n, axis=ax)


W_IN_PIECES = ((0, 512, 4608), (512, 512, 5120), (1024, 512, 5632), (1536, 8, 8192), (1544, 512, 6144),
               (2056, 512, 3072), (2568, 512, 3584), (3080, 512, 4096), (3592, 4, 8200), (3596, 4, 8204),
               (3600, 512, 6656), (4112, 512, 7168), (4624, 512, 7680), (5136, 3072, 0))


def _aligned_from_shards(shards):
    def cols(lo, n):
        parts = []
        while n > 0:
            j, off = divmod(lo, W_SHARD)
            take = min(n, W_SHARD - off)
            parts.append(shards[j][..., off:off + take])
            lo, n = lo + take, n - take
        return parts

    out = []
    for lo, n, _ in sorted(W_IN_PIECES, key=lambda p: p[2]):
        out += cols(lo, n)
    out.append(jnp.zeros(shards[0].shape[:-1] + (N_AL - N_IN,), shards[0].dtype))
    return jnp.concatenate(out, axis=-1)


def _shard_from_aligned(w_al, j):
    lo_j, hi_j = j * W_SHARD, (j + 1) * W_SHARD
    parts = []
    for lo, n, al in W_IN_PIECES:
        a, b = max(lo, lo_j), min(lo + n, hi_j)
        if a < b:
            parts.append(w_al[..., al + a - lo:al + b - lo])
    return jnp.concatenate(parts, axis=-1)


def kernel(x, mem, norm_g, w_in, b_fg, b_merge, conv_w, a_log, dt_bias, gdn_norm_g, mem_norm_g, w_mem_kv, w_branch, w_out, final_norm_g, loss_target, m_norm_g, m_w_in, m_b_fg, m_b_merge, m_conv_w, m_a_log, m_dt_bias, m_gdn_norm_g, m_mem_norm_g, m_w_mem_kv, m_w_branch, m_w_out, m_final_norm_g, v_norm_g, v_w_in, v_b_fg, v_b_merge, v_conv_w, v_a_log, v_dt_bias, v_gdn_norm_g, v_mem_norm_g, v_w_mem_kv, v_w_branch, v_w_out, v_final_norm_g):
    wts = dict(norm_g=norm_g, w_in=w_in, b_fg=b_fg, b_merge=b_merge, conv_w=conv_w, a_log=a_log,
               dt_bias=dt_bias, gdn_norm_g=gdn_norm_g, mem_norm_g=mem_norm_g, w_mem_kv=w_mem_kv,
               w_branch=w_branch, w_out=w_out, final_norm_g=final_norm_g)
    mom = dict(norm_g=m_norm_g, w_in=m_w_in, b_fg=m_b_fg, b_merge=m_b_merge, conv_w=m_conv_w,
               a_log=m_a_log, dt_bias=m_dt_bias, gdn_norm_g=m_gdn_norm_g, mem_norm_g=m_mem_norm_g,
               w_mem_kv=m_w_mem_kv, w_branch=m_w_branch, w_out=m_w_out, final_norm_g=m_final_norm_g)
    vel = dict(norm_g=v_norm_g, w_in=v_w_in, b_fg=v_b_fg, b_merge=v_b_merge, conv_w=v_conv_w,
               a_log=v_a_log, dt_bias=v_dt_bias, gdn_norm_g=v_gdn_norm_g, mem_norm_g=v_mem_norm_g,
               w_mem_kv=v_w_mem_kv, w_branch=v_w_branch, w_out=v_w_out, final_norm_g=v_final_norm_g)

    big = ("w_in", "w_mem_kv", "w_branch", "w_out")
    gathered = _gather_chips("weight_gather", [wts[n].astype(BF16) for n in big] + [conv_w])
    all_w = dict(zip(big + ("conv_w",), gathered))
    w_in_al = _aligned_from_shards([all_w["w_in"][j] for j in range(N_CHIPS)])

    layers = []
    for l in range(DEPTH):
        rows_of = lambda n: all_w[n][:, l].reshape(D_MODEL, D_MODEL)
        last_of = lambda n: jnp.concatenate([all_w[n][j, l] for j in range(N_CHIPS)], axis=-1)
        layers.append(dict(norm_g=norm_g[l], w_in_al=w_in_al[l], b_fg=b_fg[l], b_merge=b_merge[l],
                           conv_w=jnp.pad(last_of("conv_w"), ((0, 4), (0, 0))), a_log=a_log[l],
                           dt_bias=dt_bias[l], gdn_norm_g=gdn_norm_g[l], mem_norm_g=mem_norm_g[l],
                           w_mem_kv=rows_of("w_mem_kv"), w_branch=last_of("w_branch"),
                           w_out=rows_of("w_out")))

    loss_lanes, dx, grads, dgf = _local_step(x[0], mem[0], layers, final_norm_g, loss_target[0])

    gfull = {n: jnp.stack([grads[l][n] for l in range(DEPTH)])
             for n in ("norm_g", "b_fg", "b_merge", "conv_w", "a_log", "dt_bias", "gdn_norm_g",
                       "mem_norm_g", "w_mem_kv", "w_branch", "w_out")}
    gfull["final_norm_g"] = dgf
    loss_local = jnp.sum(loss_lanes).reshape(1)
    small_g = [gfull[n] for n in SMALL] + [loss_local]
    dw_al = jnp.stack([grads[l]["w_in_al"] for l in range(DEPTH)])
    ga = jnp.stack([_shard_from_aligned(dw_al, j) for j in range(N_CHIPS)])
    gb = jnp.stack([_pack([_shard_of(gfull[n], n, j) for n in SLAB] + small_g, PACK_ROWS)
                    for j in range(N_CHIPS)])
    R = gb.shape[1]
    gb = gb.reshape(N_CHIPS, 2, R // 2, PACK_COLS)

    mc = lax.axis_index("c")
    tr = 256
    from_sibling = _sibling_swap([ga, gb])
    mine = [lax.dynamic_index_in_dim(g, mc, axis=1, keepdims=False) for g in (ga, gb)]
    pair = [_add_pairs(a, b, tr, dt) for a, b, dt in zip(mine, from_sibling, (BF16, F32))]
    slots = _chip_exchange(pair)
    half = [_sum_slots(s, tr) for s in slots]
    ga_sum, gb_sum = _sibling_gather(half)
    gb_sum = gb_sum.reshape(R, PACK_COLS)

    zero1 = jnp.zeros((1,), F32)
    slab = lambda d: _pack([d[n] for n in SLAB] + [d[n] for n in SMALL] + [zero1], PACK_ROWS)
    delta_s, m_s, v_s = _adamw(slab(wts), gb_sum, slab(mom), slab(vel), tr)
    flat_in = lambda a: a.reshape(DEPTH * D_MODEL, W_SHARD)
    in_res = _adamw(flat_in(w_in), flat_in(ga_sum), flat_in(m_w_in), flat_in(v_w_in), tr)

    names = list(SLAB) + list(SMALL)
    shapes = [wts[n].shape for n in names] + [(1,)]
    g_un = dict(zip(names + ["loss"], _unpack(gb_sum, shapes)))
    d_un = dict(zip(names, _unpack(delta_s, shapes[:-1])))
    m_un = dict(zip(names, _unpack(m_s, shapes[:-1])))
    v_un = dict(zip(names, _unpack(v_s, shapes[:-1])))
    g_un["w_in"] = ga_sum
    d_un["w_in"], m_un["w_in"], v_un["w_in"] = [r.reshape(w_in.shape) for r in in_res]

    loss = g_un["loss"][0]
    return (loss, dx[None], *[g_un[n] for n in ALL_WEIGHTS], *[d_un[n] for n in ALL_WEIGHTS],
            *[m_un[n] for n in ALL_WEIGHTS], *[v_un[n] for n in ALL_WEIGHTS])
```

```python
import functools

import jax
import jax.numpy as jnp
from jax import lax
from jax.experimental import pallas as pl
from jax.experimental.pallas import tpu as pltpu

F32 = jnp.float32
BF16 = jnp.bfloat16
HIGHEST = lax.Precision.HIGHEST
PREC_UT = lax.Precision.HIGH
MESH_ID = pl.DeviceIdType.MESH

D_MODEL = 1024
DEPTH = 2
CHUNK = 64
EPS = 1e-6
FOX_HEADS, FOX_DIM = 8, 64
GDN_HEADS, GDN_DIM = 4, 128
MEM_HEADS, MEM_DIM = 4, 128
WIDTH = 512
N_BRANCH = 3
N_IN = 8208
N_AL = 8320
N_CHIPS = 4
NEG = -1e30
LOG2E = 1.4426950408889634
LN2 = 0.6931471805599453

ADAM_LR, ADAM_B1, ADAM_B2, ADAM_EPS, ADAM_WD, ADAM_STEP = 0.001, 0.9, 0.999, 1e-08, 0.01, 10

CB_GATES = 0
CB_BQKV = 2
CB_AQ, CB_AK, CB_AV, CB_AZ, CB_BZ, CB_MQ, CB_MZ = 9, 10, 11, 12, 13, 14, 15
CB_SMALL = 64
LANE_AF, LANE_BA, LANE_BB = 0, 8, 12

NN = ((1,), (0,))
NT = ((1,), (1,))
TN = ((0,), (0,))

VMEM_LIMIT_BYTES = 56 * 1024 * 1024


def _dot(a, b, dims=NN, prec=None):
    return lax.dot_general(a, b, (dims, ((), ())), preferred_element_type=F32, precision=prec)


def _bdot(a, b, ca, cb, prec=None):
    return lax.dot_general(a, b, (((ca,), (cb,)), ((0,), (0,))), preferred_element_type=F32,
                           precision=prec)


def _b16(a):
    return a.astype(BF16)


def _eye(n, dtype=F32):
    r = lax.broadcasted_iota(jnp.int32, (n, n), 0)
    c = lax.broadcasted_iota(jnp.int32, (n, n), 1)
    return jnp.where(r == c, 1.0, 0.0).astype(dtype)


def _transpose_exact(x):
    return _dot(_eye(x.shape[1]), x, NT, HIGHEST)


def _col_to_row(col):
    n = col.shape[0]
    return jnp.sum(jnp.where(_eye(n) > 0.5, col, 0.0), axis=0, keepdims=True)


def _row_to_col(row):
    n = row.shape[1]
    return jnp.sum(jnp.where(_eye(n) > 0.5, row, 0.0), axis=1, keepdims=True)


def _sigmoid(x):
    return 1.0 / (1.0 + jnp.exp(-x))


def _softplus(x):
    return jnp.maximum(x, 0.0) + jnp.log(1.0 + jnp.exp(-jnp.abs(x)))


def _silu_and_grad(x):
    s = _sigmoid(x)
    return x * s, s * (1.0 + x * (1.0 - s))


def _params(semantics):
    return pltpu.CompilerParams(dimension_semantics=semantics, vmem_limit_bytes=VMEM_LIMIT_BYTES)


def _rows(a, ts):
    nd = a.ndim
    return (a, (ts,) + a.shape[1:], lambda i, nd=nd: (i,) + (0,) * (nd - 1))


def _cols(a, ts, width, cb):
    return (a, (ts, width), lambda i, cb=cb: (i, cb))


def _full(a):
    nd = a.ndim
    return (a, a.shape, lambda i, nd=nd: (0,) * nd)


def _orow(S, tail, dtype, ts):
    nd = 1 + len(tail)
    return ((S,) + tuple(tail), dtype, (ts,) + tuple(tail), lambda i, nd=nd: (i,) + (0,) * (nd - 1))


def _oacc(shape, dtype):
    nd = len(shape)
    return (tuple(shape), dtype, tuple(shape), lambda i, nd=nd: (0,) * nd)


def _tiled(name, body, n_steps, ins, outs, scratch=(), reverse=False):
    def rev(imap):
        if not reverse:
            return imap
        return lambda i: imap(n_steps - 1 - i)

    in_specs = [pl.BlockSpec(blk, rev(imap)) for (_, blk, imap) in ins]
    out_specs = [pl.BlockSpec(blk, rev(imap)) for (_, _, blk, imap) in outs]
    out_shape = [jax.ShapeDtypeStruct(shape, dt) for (shape, dt, _, _) in outs]
    n_in, n_out = len(ins), len(outs)

    def kern(*refs):
        step = pl.program_id(0)
        t = (n_steps - 1 - step) if reverse else step
        body(t, step == 0, refs[:n_in], refs[n_in:n_in + n_out], refs[n_in + n_out:])

    res = pl.pallas_call(
        kern, name=name, grid=(n_steps,), in_specs=in_specs, out_specs=out_specs,
        out_shape=out_shape, scratch_shapes=list(scratch),
        compiler_params=_params(("arbitrary",)),
    )(*[a for (a, _, _) in ins])
    return res


def _pick(n, pref):
    if n <= pref:
        return n
    best = None
    for t in range(128, pref + 1, 128):
        if n % t == 0:
            best = t
    assert best is not None, (n, pref)
    return best


def _mm(name, a, b, ta=False, tb=False, out_dtype=F32, tm=1024, tn=1024, tk=1024):
    if ta:
        K, M = a.shape
    else:
        M, K = a.shape
    if tb:
        N, K2 = b.shape
    else:
        K2, N = b.shape
    assert K == K2, (a.shape, b.shape, ta, tb)
    tm, tn, tk = _pick(M, tm), _pick(N, tn), _pick(K, tk)
    nk = K // tk
    a_spec = (pl.BlockSpec((tk, tm), lambda i, j, k: (k, i)) if ta
              else pl.BlockSpec((tm, tk), lambda i, j, k: (i, k)))
    b_spec = (pl.BlockSpec((tn, tk), lambda i, j, k: (j, k)) if tb
              else pl.BlockSpec((tk, tn), lambda i, j, k: (k, j)))
    dims = ((0,) if ta else (1,), (1,) if tb else (0,))

    def kern_single(a_ref, b_ref, o_ref):
        o_ref[...] = _dot(_b16(a_ref[...]), _b16(b_ref[...]), dims).astype(o_ref.dtype)

    def kern_acc(a_ref, b_ref, o_ref, acc_ref):
        k = pl.program_id(2)

        @pl.when(k == 0)
        def _():
            acc_ref[...] = jnp.zeros_like(acc_ref)

        acc_ref[...] += _dot(_b16(a_ref[...]), _b16(b_ref[...]), dims)

        @pl.when(k == nk - 1)
        def _():
            o_ref[...] = acc_ref[...].astype(o_ref.dtype)

    return pl.pallas_call(
        kern_single if nk == 1 else kern_acc, name=name, grid=(M // tm, N // tn, nk),
        in_specs=[a_spec, b_spec],
        out_specs=pl.BlockSpec((tm, tn), lambda i, j, k: (i, j)),
        out_shape=jax.ShapeDtypeStruct((M, N), out_dtype),
        scratch_shapes=[] if nk == 1 else [pltpu.VMEM((tm, tn), F32)],
        compiler_params=_params(("parallel", "parallel", "arbitrary")),
    )(a, b)


def _rms_fwd(name, x, g, ts):
    S, D = x.shape

    def body(t, first, ins, outs, scratch):
        x_ref, g_ref = ins
        h_ref, r_ref = outs
        xv = x_ref[...]
        r = lax.rsqrt(jnp.mean(xv * xv, axis=1, keepdims=True) + EPS)
        h_ref[...] = (xv * r * g_ref[...]).astype(h_ref.dtype)
        r_ref[...] = r

    return _tiled(name, body, S // ts, [_rows(x, ts), _full(g.reshape(1, D))],
                  [_orow(S, (D,), BF16, ts), _orow(S, (1,), F32, ts)])


def _rms_bwd(name, dh, x, rstd, g, dres, ts):
    S, D = x.shape

    def body(t, first, ins, outs, scratch):
        dh_ref, x_ref, r_ref, g_ref, dres_ref = ins
        dx_ref, dg_ref = outs
        r = r_ref[...]
        xh = x_ref[...] * r
        dhv = dh_ref[...]
        dxh = dhv * g_ref[...]
        dx_ref[...] = dres_ref[...] + r * (dxh - xh * jnp.mean(dxh * xh, axis=1, keepdims=True))

        @pl.when(first)
        def _():
            dg_ref[...] = jnp.zeros_like(dg_ref)

        dg_ref[0:1, :] += jnp.sum(dhv * xh, axis=0, keepdims=True)

    dx, dg = _tiled(name, body, S // ts,
                    [_rows(dh, ts), _rows(x, ts), _rows(rstd, ts), _full(g.reshape(1, D)), _rows(dres, ts)],
                    [_orow(S, (D,), F32, ts), _oacc((8, D), F32)])
    return dx, dg[0]


def _loss_head(x, g, target, ts):
    S, D = x.shape

    def body(t, first, ins, outs, scratch):
        x_ref, g_ref, tgt_ref = ins
        dx_ref, dg_ref, loss_ref = outs
        xv = x_ref[...]
        gv = g_ref[...]
        r = lax.rsqrt(jnp.mean(xv * xv, axis=1, keepdims=True) + EPS)
        xh = xv * r
        err = xh * gv - tgt_ref[...]
        dy = err * (1.0 / D)
        dxh = dy * gv
        dx_ref[...] = r * (dxh - xh * jnp.mean(dxh * xh, axis=1, keepdims=True))

        @pl.when(first)
        def _():
            dg_ref[...] = jnp.zeros_like(dg_ref)
            loss_ref[...] = jnp.zeros_like(loss_ref)

        dg_ref[0:1, :] += jnp.sum(dy * xh, axis=0, keepdims=True)
        per_lane = jnp.sum(err * err, axis=0, keepdims=True)
        loss_ref[0:1, :] += per_lane * (0.5 / D)

    dx, dg, loss = _tiled("loss_head", body, S // ts,
                          [_rows(x, ts), _full(g.reshape(1, D)), _rows(target, ts)],
                          [_orow(S, (D,), F32, ts), _oacc((8, D), F32), _oacc((8, D), F32)])
    return dx, dg[0], loss[0]


def _scan_rows(x, length, seg, reverse=False):
    row = lax.broadcasted_iota(jnp.int32, x.shape, 0) % seg
    k = 1
    while k < seg:
        if reverse:
            x = x + jnp.where(row < seg - k, pltpu.roll(x, length - k, 0), 0.0)
        else:
            x = x + jnp.where(row >= k, pltpu.roll(x, k, 0), 0.0)
        k *= 2
    return x


def _fox_decay(z, b_fg128, ts):
    S = z.shape[0]

    def body(t, first, ins, outs, scratch):
        zs_ref, b_ref = ins
        f_ref, hi_ref, mid_ref, lo_ref = outs
        (carry,) = scratch

        @pl.when(first)
        def _():
            carry[...] = jnp.zeros_like(carry)

        logf = -_softplus(-(zs_ref[...] + b_ref[...]))
        run = _scan_rows(logf, ts, ts) + carry[0:1, :]
        carry[0:1, :] = run[ts - 1:ts, :]
        f_ref[...] = _transpose_exact(run)
        f2 = run * LOG2E
        hi = f2.astype(BF16)
        r1 = f2 - hi.astype(F32)
        mid = r1.astype(BF16)
        lo = (r1 - mid.astype(F32)).astype(BF16)
        eye = _eye(128, BF16)
        hi_ref[...] = _dot(eye, hi, NT).astype(BF16)
        mid_ref[...] = _dot(eye, mid, NT).astype(BF16)
        lo_ref[...] = _dot(eye, lo, NT).astype(BF16)

    tcol = lambda dt: ((128, S), dt, (128, ts), lambda i: (0, i))
    return _tiled("fox_decay", body, S // ts,
                  [_cols(z, ts, 128, CB_SMALL), _full(b_fg128)],
                  [tcol(F32), tcol(BF16), tcol(BF16), tcol(BF16)], scratch=[pltpu.VMEM((8, 128), F32)])


def _fox_decay_bwd(dfk_rows, dfq_rows, z, b_fg128, ts):
    S = z.shape[0]
    H = dfk_rows.shape[0]

    def body(t, first, ins, outs, scratch):
        dfk_ref, dfq_ref, zs_ref, b_ref = ins
        daf_ref, db_ref = outs
        (carry,) = scratch

        @pl.when(first)
        def _():
            carry[...] = jnp.zeros_like(carry)
            db_ref[...] = jnp.zeros_like(db_ref)

        r = lax.broadcasted_iota(jnp.int32, (H, 128), 0)
        c = lax.broadcasted_iota(jnp.int32, (H, 128), 1)
        place = jnp.where(r == c, 1.0, 0.0)
        df = _dot(dfk_ref[...] + dfq_ref[...], place, TN, HIGHEST)
        run = _scan_rows(df, ts, ts, reverse=True) + carry[0:1, :]
        carry[0:1, :] = run[0:1, :]
        daf = run * _sigmoid(-(zs_ref[...] + b_ref[...]))
        daf_ref[...] = daf
        db_ref[0:1, :] += jnp.sum(daf, axis=0, keepdims=True)

    rowsin = lambda a: (a, (H, ts), lambda i: (0, i))
    daf, db = _tiled("fox_decay_bwd", body, S // ts,
                     [rowsin(dfk_rows), rowsin(dfq_rows), _cols(z, ts, 128, CB_SMALL), _full(b_fg128)],
                     [_orow(S, (128,), F32, ts), _oacc((8, 128), F32)],
                     scratch=[pltpu.VMEM((8, 128), F32)], reverse=True)
    return daf, db[0]


FOX_AUG = 80


def _fox_fwd(q_aug, kT_aug, v_aug, tq):
    H, S, da = q_aug.shape
    dv = v_aug.shape[2]
    d = FOX_DIM
    tk = tq // 2
    qscale = (d ** -0.5) * LOG2E

    def kern(q_ref, kT_ref, v_ref, o_ref, lse_ref, qs_ref, s_buf, p_buf, m_scr, acc_scr):
        i = pl.program_id(1)
        col = lax.broadcasted_iota(jnp.int32, (1, da), 1)
        qb = _b16(q_ref[...] * jnp.where(col < d, qscale, 1.0))
        qs_ref[...] = qb

        def keys(t):
            return pl.ds(pl.multiple_of(t * tk, tk), tk)

        def stage(t, slot, mask_off, look_ahead):
            if look_ahead:
                s_buf[1 - slot] = _dot(qb, kT_ref[:, keys(t + 1)])
            pv = _dot(p_buf[1 - slot], v_ref[keys(jnp.maximum(t - 1, 0)), :])

            def scores():
                s = s_buf[slot]
                if mask_off is None:
                    return s
                r = lax.broadcasted_iota(jnp.int32, (tq, tk), 0)
                c = lax.broadcasted_iota(jnp.int32, (tq, tk), 1)
                return jnp.where(c + mask_off <= r, s, NEG)

            m = m_scr[...]
            m_new = jnp.maximum(m, jnp.max(scores(), axis=1, keepdims=True))
            alpha = jnp.exp2(m - m_new)
            p_buf[slot] = _b16(jnp.exp2(scores() - m_new))
            m_scr[...] = m_new
            acc_scr[...] = (acc_scr[...] + pv) * alpha

        s_buf[0] = _dot(qb, kT_ref[:, keys(0)])
        p_buf[1] = jnp.zeros((tq, tk), BF16)
        m_scr[...] = jnp.full((tq, 1), NEG, F32)
        acc_scr[...] = jnp.zeros((tq, dv), F32)

        def pair(n):
            stage(2 * n, 0, None, True)
            stage(2 * n + 1, 1, None, True)

        def quad(m, _):
            pair(2 * m)
            pair(2 * m + 1)
            return 0

        lax.fori_loop(0, i // 2, quad, 0)

        @pl.when(i % 2 == 1)
        def _():
            pair(i - 1)

        stage(2 * i, 0, 0, True)
        stage(2 * i + 1, 1, tk, False)
        acc = acc_scr[...] + _dot(p_buf[1], v_ref[keys(2 * i + 1), :])
        l = acc[:, d:d + 1]
        o_ref[...] = acc[:, :d] / l
        lse_ref[...] = _col_to_row(m_scr[...] + jnp.log(l) * LOG2E)

    return pl.pallas_call(
        kern, name="fox_fwd", grid=(H, S // tq),
        in_specs=[pl.BlockSpec((None, tq, da), lambda h, i: (h, i, 0)),
                  pl.BlockSpec((None, da, S), lambda h, i: (h, 0, 0)),
                  pl.BlockSpec((None, S, dv), lambda h, i: (h, 0, 0))],
        out_specs=[pl.BlockSpec((None, tq, d), lambda h, i: (h, i, 0)),
                   pl.BlockSpec((None, 1, tq), lambda h, i: (h, 0, i)),
                   pl.BlockSpec((None, tq, da), lambda h, i: (h, i, 0))],
        out_shape=[jax.ShapeDtypeStruct((H, S, d), F32), jax.ShapeDtypeStruct((H, 1, S), F32),
                   jax.ShapeDtypeStruct((H, S, da), BF16)],
        scratch_shapes=[pltpu.VMEM((2, tq, tk), F32), pltpu.VMEM((2, tq, tk), BF16),
                        pltpu.VMEM((tq, 1), F32), pltpu.VMEM((tq, dv), F32)],
        compiler_params=_params(("parallel", "arbitrary")),
    )(q_aug, kT_aug, v_aug)


def _fox_bwd(qs, k_aug, kT, v, do, lse_row, delta_row, tq):
    H, S, da = qs.shape
    d = FOX_DIM
    tk = tq
    nq = S // tq
    scale = d ** -0.5

    ts2 = tq // 2
    last = 2 * nq - 1

    def kern(q_ref, k_ref, kT_ref, v_ref, do_ref, lse_ref, dl_ref,
             dqT_ref, dk_ref, dv_ref, dfk_ref, dfq_ref,
             kq_buf, dp_buf, pb_buf, ds_buf, dk_scr, dv_scr, dfk_scr):
        j = pl.program_id(1)

        @pl.when(j == 0)
        def _():
            dqT_ref[...] = jnp.zeros_like(dqT_ref)
            dfq_ref[...] = jnp.zeros_like(dfq_ref)

        kb = k_ref[...]
        kTb = kT_ref[...]
        vb = v_ref[...]
        dk_scr[...] = jnp.zeros_like(dk_scr)
        dv_scr[...] = jnp.zeros_like(dv_scr)
        dfk_scr[...] = jnp.zeros_like(dfk_scr)

        def queries(t):
            return pl.ds(pl.multiple_of(t * ts2, ts2), ts2)

        def products(t, slot):
            rows = queries(t)
            kq_buf[slot] = _dot(kb, q_ref[rows, :], NT)
            dp_buf[slot] = _dot(vb, do_ref[rows, :], NT)

        def pointwise(t, slot, mask_off):
            rows = queries(t)
            sT = kq_buf[slot]
            if mask_off is not None:
                r = lax.broadcasted_iota(jnp.int32, (tk, ts2), 0)
                c = lax.broadcasted_iota(jnp.int32, (tk, ts2), 1)
                sT = jnp.where(r <= c + mask_off, sT, NEG)
            pT = jnp.exp2(sT - lse_ref[:, rows])
            dsT = pT * (dp_buf[slot] - dl_ref[:, rows])
            pb_buf[slot] = _b16(pT)
            ds_buf[slot] = _b16(dsT)
            dfk_scr[...] -= jnp.sum(dsT, axis=1, keepdims=True)
            dfq_ref[:, rows] += jnp.sum(dsT, axis=0, keepdims=True)

        def accumulate(t, slot):
            rows = queries(t)
            dsb = ds_buf[slot]
            dv_scr[...] += _dot(pb_buf[slot], do_ref[rows, :])
            dk_scr[...] += _dot(dsb, q_ref[rows, :])
            dqT_ref[:, rows] += _dot(kTb, dsb) * scale

        def stage(t, slot, mask_off, has_prev):
            products(jnp.minimum(t + 1, last), 1 - slot)
            if has_prev:
                accumulate(t - 1, 1 - slot)
            pointwise(t, slot, mask_off)

        products(2 * j, 0)
        stage(2 * j, 0, 0, False)
        stage(2 * j + 1, 1, ts2, True)

        def pair(n):
            stage(2 * n, 0, None, True)
            stage(2 * n + 1, 1, None, True)

        def quad(m, _):
            pair(j + 1 + 2 * m)
            pair(j + 2 + 2 * m)
            return 0

        n_rest = nq - 1 - j
        lax.fori_loop(0, n_rest // 2, quad, 0)

        @pl.when(n_rest % 2 == 1)
        def _():
            pair(nq - 1)

        accumulate(last, 1)
        dk_ref[...] = dk_scr[:, :d] * LN2
        dv_ref[...] = dv_scr[...]
        dfk_ref[...] = _col_to_row(dfk_scr[...])

    tile = lambda h, j: (h, j, 0)
    whole = lambda h, j: (h, 0, 0)
    rowtile = lambda h, j: (h, 0, j)
    return pl.pallas_call(
        kern, name="fox_bwd", grid=(H, S // tk),
        in_specs=[pl.BlockSpec((None, S, da), whole),
                  pl.BlockSpec((None, tk, da), tile),
                  pl.BlockSpec((None, d, tk), lambda h, j: (h, 0, j)),
                  pl.BlockSpec((None, tk, d), tile),
                  pl.BlockSpec((None, S, d), whole),
                  pl.BlockSpec((None, 1, S), whole),
                  pl.BlockSpec((None, 1, S), whole)],
        out_specs=[pl.BlockSpec((None, d, S), whole),
                   pl.BlockSpec((None, tk, d), tile),
                   pl.BlockSpec((None, tk, d), tile),
                   pl.BlockSpec((None, 1, tk), rowtile),
                   pl.BlockSpec((None, 1, S), whole)],
        out_shape=[jax.ShapeDtypeStruct((H, d, S), F32), jax.ShapeDtypeStruct((H, S, d), F32),
                   jax.ShapeDtypeStruct((H, S, d), F32), jax.ShapeDtypeStruct((H, 1, S), F32),
                   jax.ShapeDtypeStruct((H, 1, S), F32)],
        scratch_shapes=[pltpu.VMEM((2, tk, ts2), F32), pltpu.VMEM((2, tk, ts2), F32),
                        pltpu.VMEM((2, tk, ts2), BF16), pltpu.VMEM((2, tk, ts2), BF16),
                        pltpu.VMEM((tk, da), F32), pltpu.VMEM((tk, d), F32), pltpu.VMEM((tk, 1), F32)],
        compiler_params=_params(("parallel", "arbitrary")),
    )(qs, k_aug, kT, v, do, lse_row, delta_row)


def _heads_major(a, H, d):
    S = a.shape[0]
    return a.reshape(S, H, d).transpose(1, 0, 2)


def _heads_minor(a):
    H, S, d = a.shape
    return a.transpose(1, 0, 2).reshape(S, H * d)


def _lane_pick(x128, lane):
    return x128[:, lane:lane + 1]


def _l2_fwd(y):
    return lax.rsqrt(jnp.sum(y * y, axis=1, keepdims=True) + EPS)


def _gdn_prep(z, conv_w, a128, dt128, ts):
    S = z.shape[0]
    C3 = 3 * WIDTH
    hb = ts // 8

    def body(t, first, ins, outs, scratch):
        x_ref, halo_ref, zs_ref, w_ref, a_ref, dt_ref = ins
        qkv_ref, c_ref, gb_ref, gbT_ref = outs
        halo = jnp.where(t > 0, halo_ref[...], 0.0)
        xe = jnp.concatenate([halo, x_ref[...]], axis=0)
        w = w_ref[...]
        c = w[3:4, :] * xe[8:, :]
        for back in (1, 2, 3):
            c = c + w[3 - back:4 - back, :] * pltpu.roll(xe, back, 0)[8:, :]
        c_ref[...] = c
        y = c * _sigmoid(c)
        for h in range(GDN_HEADS):
            lo = h * GDN_DIM
            yq = y[:, lo:lo + GDN_DIM]
            qkv_ref[:, lo:lo + GDN_DIM] = yq * (_l2_fwd(yq) * (GDN_DIM ** -0.5))
            yk = y[:, WIDTH + lo:WIDTH + lo + GDN_DIM]
            qkv_ref[:, WIDTH + lo:WIDTH + lo + GDN_DIM] = yk * _l2_fwd(yk)
        qkv_ref[:, 2 * WIDTH:] = y[:, 2 * WIDTH:]
        zs = zs_ref[...]
        lane = lax.broadcasted_iota(jnp.int32, zs.shape, 1)
        g = -jnp.exp(a_ref[...]) * _softplus(zs + dt_ref[...])
        G = _scan_rows(g, ts, CHUNK)
        beta = _sigmoid(zs)
        out = jnp.where(lane < 8, pltpu.roll(g, 128 - LANE_BA, 1), jnp.where(lane < LANE_BB, G, beta))
        gb_ref[...] = out
        gbT_ref[...] = _transpose_exact(out)

    x_in = (z, (ts, C3), lambda i: (i, CB_BQKV))
    halo_in = (z, (8, C3), lambda i: (jnp.maximum(i * hb - 1, 0), CB_BQKV))
    return _tiled("gdn_prep", body, S // ts,
                  [x_in, halo_in, _cols(z, ts, 128, CB_SMALL), _full(conv_w), _full(a128), _full(dt128)],
                  [_orow(S, (C3,), F32, ts), _orow(S, (C3,), F32, ts), _orow(S, (128,), F32, ts),
                   ((128, S), F32, (128, ts), lambda i: (0, i))])


def _chunk_masks(nc):
    r = lax.broadcasted_iota(jnp.int32, (nc, CHUNK, CHUNK), 1)
    c = lax.broadcasted_iota(jnp.int32, (nc, CHUNK, CHUNK), 2)
    return c <= r, c < r, c == r


def _chunk_local(qh, kh, vh, Gc, Gr, beta):
    nc = qh.shape[0]
    incl, strict, _ = _chunk_masks(nc)
    gamma = jnp.exp(jnp.where(incl, Gc - Gr, NEG))
    kb = kh * beta
    P = _bdot(_b16(kb), _b16(kh), 2, 2)
    Qk = _bdot(_b16(qh), _b16(kh), 2, 2)
    eG = jnp.exp(Gc)
    Gl = Gc[:, CHUNK - 1:CHUNK, :]
    edec = jnp.exp(Gl - Gc)
    return incl, strict, gamma, kb, P, Qk, eG, edec


def _gdn_local_fwd(qkv, gb, grow, ts):
    S = qkv.shape[0]
    nc = ts // CHUNK

    def body(t, first, ins, outs, scratch):
        q_ref, k_ref, v_ref, gb_ref, gr_ref = ins
        u_ref, w_ref, qd_ref, kd_ref, aqk_ref, T_ref = outs
        gbv = gb_ref[...]
        for h in range(GDN_HEADS):
            lo = h * GDN_DIM
            qh = q_ref[:, lo:lo + GDN_DIM].reshape(nc, CHUNK, GDN_DIM)
            kh = k_ref[:, lo:lo + GDN_DIM].reshape(nc, CHUNK, GDN_DIM)
            vh = v_ref[:, lo:lo + GDN_DIM].reshape(nc, CHUNK, GDN_DIM)
            Gc = _lane_pick(gbv, LANE_BA + h).reshape(nc, CHUNK, 1)
            beta = _lane_pick(gbv, LANE_BB + h).reshape(nc, CHUNK, 1)
            Gr = gr_ref[h].reshape(nc, 1, CHUNK)
            incl, strict, gamma, kb, P, Qk, eG, edec = _chunk_local(qh, kh, vh, Gc, Gr, beta)
            A = jnp.where(strict, P * gamma, 0.0)
            _, _, eye = _chunk_masks(nc)
            T = jnp.where(eye, 1.0, 0.0) - A
            X = A
            for _ in range(5):
                X = _bdot(X, X, 2, 1, PREC_UT)
                T = T + _bdot(T, X, 2, 1, PREC_UT)
            u = _bdot(T, vh * beta, 2, 1, PREC_UT)
            w = _bdot(T, kb * eG, 2, 1, PREC_UT)
            u_ref[:, lo:lo + GDN_DIM] = u.reshape(ts, GDN_DIM)
            w_ref[:, lo:lo + GDN_DIM] = w.reshape(ts, GDN_DIM)
            qd_ref[:, lo:lo + GDN_DIM] = (qh * eG).reshape(ts, GDN_DIM)
            kd_ref[:, lo:lo + GDN_DIM] = (kh * edec).reshape(ts, GDN_DIM)
            aqk_ref[h] = jnp.where(incl, Qk * gamma, 0.0).reshape(ts, CHUNK)
            T_ref[h] = T.reshape(ts, CHUNK)

    wide = _orow(S, (WIDTH,), F32, ts)
    perhead = ((GDN_HEADS, S, CHUNK), F32, (GDN_HEADS, ts, CHUNK), lambda i: (0, i, 0))
    return _tiled("gdn_local_fwd", body, S // ts,
                  [_cols(qkv, ts, WIDTH, 0), _cols(qkv, ts, WIDTH, 1), _cols(qkv, ts, WIDTH, 2),
                   _rows(gb, ts), (grow, (GDN_HEADS, nc, CHUNK), lambda i: (0, i, 0))],
                  [wide, wide, wide, wide, perhead, perhead])


def _gdn_scan_fwd(u, w, qd, kd, aqk, gb, ts):
    S = u.shape[0]
    nc = ts // CHUNK
    N = S // CHUNK

    def body(t, first, ins, outs, scratch):
        u_ref, w_ref, qd_ref, kd_ref, aqk_ref, gb_ref = ins
        o_ref, vn_ref, st_ref = outs
        (state,) = scratch

        @pl.when(first)
        def _():
            state[...] = jnp.zeros_like(state)

        def chunk(c, _):
            r0 = pl.multiple_of(c * CHUNK, CHUNK)
            rows = pl.ds(r0, CHUNK)
            glast = gb_ref[pl.ds(r0 + CHUNK - 1, 1), :]
            heads = range(GDN_HEADS)
            cols = [slice(h * GDN_DIM, (h + 1) * GDN_DIM) for h in heads]
            S_old = [state[h] for h in heads]
            u_h = [u_ref[rows, cols[h]] for h in heads]
            w_h = [_b16(w_ref[rows, cols[h]]) for h in heads]
            qd_h = [_b16(qd_ref[rows, cols[h]]) for h in heads]
            kd_h = [_b16(kd_ref[rows, cols[h]]) for h in heads]
            aqk_h = [_b16(aqk_ref[h, rows, :]) for h in heads]
            S_new, o_h, vn_h = [], [], []
            for h in heads:
                Sb = _b16(S_old[h])
                vn = u_h[h] - _dot(w_h[h], Sb)
                vnb = _b16(vn)
                o_h.append(_dot(qd_h[h], Sb) + _dot(aqk_h[h], vnb))
                egl = jnp.exp(glast[:, LANE_BA + h:LANE_BA + h + 1])
                S_new.append(S_old[h] * egl + _dot(kd_h[h], vnb, TN))
                vn_h.append(vn)
            for h in heads:
                st_ref[c, h] = S_old[h]
                state[h] = S_new[h]
                o_ref[rows, cols[h]] = o_h[h]
                vn_ref[rows, cols[h]] = vn_h[h]
            return 0

        lax.fori_loop(0, nc, chunk, 0)

    wide_in = lambda a: _rows(a, ts)
    wide = _orow(S, (WIDTH,), F32, ts)
    states = ((N, GDN_HEADS, GDN_DIM, GDN_DIM), F32, (nc, GDN_HEADS, GDN_DIM, GDN_DIM),
              lambda i: (i, 0, 0, 0))
    return _tiled("gdn_scan_fwd", body, S // ts,
                  [wide_in(u), wide_in(w), wide_in(qd), wide_in(kd),
                   (aqk, (GDN_HEADS, ts, CHUNK), lambda i: (0, i, 0)), _rows(gb, ts)],
                  [wide, wide, states],
                  scratch=[pltpu.VMEM((GDN_HEADS, GDN_DIM, GDN_DIM), F32)])


def _gdn_scan_bwd(do, w, qd, kd, aqk, vn, states, gb, ts):
    S = do.shape[0]
    nc = ts // CHUNK
    N = S // CHUNK

    def body(t, first, ins, outs, scratch):
        do_ref, w_ref, qd_ref, kd_ref, aqk_ref, vn_ref, st_ref, gb_ref = ins
        du_ref, dw_ref, dqd_ref, dkd_ref, daqk_ref, dgl_ref = outs
        (dstate,) = scratch

        @pl.when(first)
        def _():
            dstate[...] = jnp.zeros_like(dstate)

        r = lax.broadcasted_iota(jnp.int32, (CHUNK, CHUNK), 0)
        cc = lax.broadcasted_iota(jnp.int32, (CHUNK, CHUNK), 1)
        incl = cc <= r
        lane = lax.broadcasted_iota(jnp.int32, (1, 128), 1)

        def chunk(k, _):
            c = nc - 1 - k
            r0 = pl.multiple_of(c * CHUNK, CHUNK)
            rows = pl.ds(r0, CHUNK)
            glast = gb_ref[pl.ds(r0 + CHUNK - 1, 1), :]
            dgl_row = jnp.zeros((1, 128), F32)
            heads = range(GDN_HEADS)
            cols = [slice(h * GDN_DIM, (h + 1) * GDN_DIM) for h in heads]
            S_h = [st_ref[c, h] for h in heads]
            dS_h = [dstate[h] for h in heads]
            do_h = [_b16(do_ref[rows, cols[h]]) for h in heads]
            aqk_h = [_b16(aqk_ref[h, rows, :]) for h in heads]
            vn_h = [_b16(vn_ref[rows, cols[h]]) for h in heads]
            kd_h = [_b16(kd_ref[rows, cols[h]]) for h in heads]
            qd_h = [_b16(qd_ref[rows, cols[h]]) for h in heads]
            w_h = [_b16(w_ref[rows, cols[h]]) for h in heads]
            res = []
            for h in heads:
                Sb, dSb, dob, vnb = _b16(S_h[h]), _b16(dS_h[h]), do_h[h], vn_h[h]
                dvn = _dot(aqk_h[h], dob, TN) + _dot(kd_h[h], dSb)
                dvnb = _b16(dvn)
                daqk = jnp.where(incl, _dot(dob, vnb, NT), 0.0)
                dqd = _dot(dob, Sb, NT)
                dkd = _dot(vnb, dSb, NT)
                dw = -_dot(dvnb, Sb, NT)
                egl = jnp.exp(glast[:, LANE_BA + h:LANE_BA + h + 1])
                dgl = egl * jnp.sum(jnp.sum(dS_h[h] * S_h[h], axis=1, keepdims=True), axis=0,
                                    keepdims=True)
                dgl_row = jnp.where(lane == h, dgl, dgl_row)
                dS_new = _dot(qd_h[h], dob, TN) + egl * dS_h[h] - _dot(w_h[h], dvnb, TN)
                res.append((daqk, dqd, dkd, dw, dvn, dS_new))
            for h in heads:
                daqk, dqd, dkd, dw, dvn, dS_new = res[h]
                daqk_ref[h, rows, :] = daqk
                dqd_ref[rows, cols[h]] = dqd
                dkd_ref[rows, cols[h]] = dkd
                dw_ref[rows, cols[h]] = dw
                du_ref[rows, cols[h]] = dvn
                dstate[h] = dS_new
            dgl_ref[pl.ds(c, 1), :] = dgl_row
            return 0

        lax.fori_loop(0, nc, chunk, 0)

    wide_in = lambda a: _rows(a, ts)
    wide = _orow(S, (WIDTH,), F32, ts)
    perhead_in = lambda a: (a, (GDN_HEADS, ts, CHUNK), lambda i: (0, i, 0))
    perhead = ((GDN_HEADS, S, CHUNK), F32, (GDN_HEADS, ts, CHUNK), lambda i: (0, i, 0))
    return _tiled("gdn_scan_bwd", body, S // ts,
                  [wide_in(do), wide_in(w), wide_in(qd), wide_in(kd), perhead_in(aqk), wide_in(vn),
                   (states, (nc, GDN_HEADS, GDN_DIM, GDN_DIM), lambda i: (i, 0, 0, 0)), _rows(gb, ts)],
                  [wide, wide, wide, wide, perhead, ((N, 128), F32, (nc, 128), lambda i: (i, 0))],
                  scratch=[pltpu.VMEM((GDN_HEADS, GDN_DIM, GDN_DIM), F32)], reverse=True)


def _gdn_local_bwd(qkv, gb, grow, T, du, dw, dqd, dkd, daqk, dgl, ts):
    S = qkv.shape[0]
    nc = ts // CHUNK

    def body(t, first, ins, outs, scratch):
        (q_ref, k_ref, v_ref, gb_ref, gr_ref, T_ref, du_ref, dw_ref, dqd_ref, dkd_ref,
         daqk_ref, dgl_ref) = ins
        dqkv_ref, dgb_ref = outs
        gbv = gb_ref[...]
        dglv = dgl_ref[...]
        lane = lax.broadcasted_iota(jnp.int32, (ts, 128), 1)
        dG_all = jnp.zeros((ts, 128), F32)
        dbeta_all = jnp.zeros((ts, 128), F32)
        for h in range(GDN_HEADS):
            lo = h * GDN_DIM
            cols = slice(lo, lo + GDN_DIM)
            r3 = lambda ref: ref[:, cols].reshape(nc, CHUNK, GDN_DIM)
            qh, kh, vh = r3(q_ref), r3(k_ref), r3(v_ref)
            duh, dwh, dqdh, dkdh = r3(du_ref), r3(dw_ref), r3(dqd_ref), r3(dkd_ref)
            Gc = _lane_pick(gbv, LANE_BA + h).reshape(nc, CHUNK, 1)
            beta = _lane_pick(gbv, LANE_BB + h).reshape(nc, CHUNK, 1)
            Gr = gr_ref[h].reshape(nc, 1, CHUNK)
            Th = T_ref[h].reshape(nc, CHUNK, CHUNK)
            daq = daqk_ref[h].reshape(nc, CHUNK, CHUNK)
            incl, strict, gamma, kb, P, Qk, eG, edec = _chunk_local(qh, kh, vh, Gc, Gr, beta)
            _, _, eye = _chunk_masks(nc)
            vb = vh * beta
            kbg = kb * eG
            dvb = _bdot(Th, duh, 1, 1, PREC_UT)
            dkbg = _bdot(Th, dwh, 1, 1, PREC_UT)
            dT = _bdot(duh, vb, 2, 2, PREC_UT) + _bdot(dwh, kbg, 2, 2, PREC_UT)
            M1 = _bdot(Th, dT, 1, 1, PREC_UT)
            dA = jnp.where(strict, -_bdot(M1, Th, 2, 2, PREC_UT), 0.0)
            dP = dA * gamma
            dQ = daq * gamma
            dgam = (dA * P + daq * Qk) * gamma
            dPb, dQb = _b16(dP), _b16(dQ)
            khb, qhb, kbb = _b16(kh), _b16(qh), _b16(kb)
            dq = _bdot(dQb, khb, 2, 1) + dqdh * eG
            dkb = _bdot(dPb, khb, 2, 1) + dkbg * eG
            dk = (_bdot(dQb, qhb, 1, 1) + _bdot(dPb, kbb, 1, 1) + dkdh * edec + dkb * beta)
            dbeta = (jnp.sum(dkb * kh, axis=2, keepdims=True) + jnp.sum(dvb * vh, axis=2, keepdims=True))
            dv = dvb * beta
            col_as_col = jnp.sum(jnp.where(eye, jnp.sum(dgam, axis=1, keepdims=True), 0.0),
                                 axis=2, keepdims=True)
            kd_term = jnp.sum(dkdh * kh * edec, axis=2, keepdims=True)
            dG = (jnp.sum(dgam, axis=2, keepdims=True) - col_as_col
                  + jnp.sum(dqdh * qh * eG, axis=2, keepdims=True)
                  + jnp.sum(dkbg * kbg, axis=2, keepdims=True) - kd_term)
            dgl_h = dglv[:, h:h + 1].reshape(nc, 1, 1) + jnp.sum(kd_term, axis=1, keepdims=True)
            last = lax.broadcasted_iota(jnp.int32, (nc, CHUNK, 1), 1) == CHUNK - 1
            dG = dG + jnp.where(last, dgl_h, 0.0)
            dqkv_ref[:, cols] = dq.reshape(ts, GDN_DIM)
            dqkv_ref[:, WIDTH + lo:WIDTH + lo + GDN_DIM] = dk.reshape(ts, GDN_DIM)
            dqkv_ref[:, 2 * WIDTH + lo:2 * WIDTH + lo + GDN_DIM] = dv.reshape(ts, GDN_DIM)
            dG_all = jnp.where(lane == LANE_BA + h, dG.reshape(ts, 1), dG_all)
            dbeta_all = jnp.where(lane == LANE_BB + h, dbeta.reshape(ts, 1), dbeta_all)
        dg_all = _scan_rows(dG_all, ts, CHUNK, reverse=True)
        dgb_ref[...] = jnp.where(lane < LANE_BB, dg_all, dbeta_all)

    wide_in = lambda a: _rows(a, ts)
    perhead_in = lambda a: (a, (GDN_HEADS, ts, CHUNK), lambda i: (0, i, 0))
    return _tiled("gdn_local_bwd", body, S // ts,
                  [_cols(qkv, ts, WIDTH, 0), _cols(qkv, ts, WIDTH, 1), _cols(qkv, ts, WIDTH, 2),
                   _rows(gb, ts), (grow, (GDN_HEADS, nc, CHUNK), lambda i: (0, i, 0)), perhead_in(T),
                   wide_in(du), wide_in(dw), wide_in(dqd), wide_in(dkd), perhead_in(daqk),
                   (dgl, (nc, 128), lambda i: (i, 0))],
                  [_orow(S, (3 * WIDTH,), F32, ts), _orow(S, (128,), F32, ts)])


def _gdn_prep_bwd(dqkv, dgb, cpre, z, conv_w, a128, dt128, ts):
    S = z.shape[0]
    C3 = 3 * WIDTH
    hb = ts // 8
    n_tiles = S // ts

    def dpre(dq, c):
        y, dsil = _silu_and_grad(c)
        parts = []
        for h in range(GDN_HEADS):
            lo = h * GDN_DIM
            yq = y[:, lo:lo + GDN_DIM]
            rq = _l2_fwd(yq)
            nq = yq * rq
            dn = dq[:, lo:lo + GDN_DIM] * (GDN_DIM ** -0.5)
            parts.append(rq * (dn - nq * jnp.sum(dn * nq, axis=1, keepdims=True)))
        for h in range(GDN_HEADS):
            lo = WIDTH + h * GDN_DIM
            yk = y[:, lo:lo + GDN_DIM]
            rk = _l2_fwd(yk)
            nk = yk * rk
            dn = dq[:, lo:lo + GDN_DIM]
            parts.append(rk * (dn - nk * jnp.sum(dn * nk, axis=1, keepdims=True)))
        parts.append(dq[:, 2 * WIDTH:])
        return jnp.concatenate(parts, axis=1) * dsil

    def body(t, first, ins, outs, scratch):
        (dq_ref, dqn_ref, c_ref, cn_ref, x_ref, xp_ref, zs_ref, dgb_ref, w_ref, a_ref, dt_ref) = ins
        dx_ref, dzs_ref, dw_ref, dad_ref = outs

        @pl.when(first)
        def _():
            dw_ref[...] = jnp.zeros_like(dw_ref)
            dad_ref[...] = jnp.zeros_like(dad_ref)

        dc = dpre(dq_ref[...], c_ref[...])
        dcn = jnp.where(t < n_tiles - 1, dpre(dqn_ref[...], cn_ref[...]), 0.0)
        dce = jnp.concatenate([dc, dcn], axis=0)
        w = w_ref[...]
        dx = w[3:4, :] * dc
        for back in (1, 2, 3):
            dx = dx + w[3 - back:4 - back, :] * pltpu.roll(dce, ts + 8 - back, 0)[:ts, :]
        dx_ref[...] = dx
        halo = jnp.where(t > 0, xp_ref[...], 0.0)
        xe = jnp.concatenate([halo, x_ref[...]], axis=0)
        dw_ref[3:4, :] += jnp.sum(dc * xe[8:, :], axis=0, keepdims=True)
        for back in (1, 2, 3):
            dw_ref[3 - back:4 - back, :] += jnp.sum(dc * pltpu.roll(xe, back, 0)[8:, :], axis=0,
                                                     keepdims=True)
        zs = zs_ref[...]
        dgb = dgb_ref[...]
        lane = lax.broadcasted_iota(jnp.int32, zs.shape, 1)
        arg = zs + dt_ref[...]
        nega = -jnp.exp(a_ref[...])
        dba = dgb * nega * _sigmoid(arg)
        beta = _sigmoid(zs)
        dbb = dgb * beta * (1.0 - beta)
        dzs_ref[...] = jnp.where((lane >= LANE_BA) & (lane < LANE_BB), dba,
                                 jnp.where((lane >= LANE_BB) & (lane < LANE_BB + 4), dbb, 0.0))
        dad_ref[0:1, :] += jnp.sum(dgb * nega * _softplus(arg), axis=0, keepdims=True)
        dad_ref[1:2, :] += jnp.sum(dba, axis=0, keepdims=True)

    nxt = lambda i: (jnp.minimum((i + 1) * hb, S // 8 - 1), 0)
    prv = lambda i: (jnp.maximum(i * hb - 1, 0), CB_BQKV)
    return _tiled("gdn_prep_bwd", body, n_tiles,
                  [_rows(dqkv, ts), (dqkv, (8, C3), nxt), _rows(cpre, ts), (cpre, (8, C3), nxt),
                   (z, (ts, C3), lambda i: (i, CB_BQKV)), (z, (8, C3), prv),
                   _cols(z, ts, 128, CB_SMALL), _rows(dgb, ts), _full(conv_w), _full(a128), _full(dt128)],
                  [_orow(S, (C3,), F32, ts), _orow(S, (128,), F32, ts), _oacc((8, C3), F32),
                   _oacc((8, 128), F32)])


def _mem_attn_fwd(z, mk, mv, ts):
    S = z.shape[0]

    def body(t, first, ins, outs, scratch):
        q_ref, mk_ref, mv_ref = ins
        (o_ref,) = outs
        for h in range(MEM_HEADS):
            cols = slice(h * MEM_DIM, (h + 1) * MEM_DIM)
            s = _dot(_b16(q_ref[:, cols]), _b16(mk_ref[:, cols]), NT) * (MEM_DIM ** -0.5)
            m = jnp.max(s, axis=1, keepdims=True)
            p = jnp.exp(s - m)
            p = p / jnp.sum(p, axis=1, keepdims=True)
            o_ref[:, cols] = _dot(_b16(p), _b16(mv_ref[:, cols]))

    (o,) = _tiled("mem_attn_fwd", body, S // ts, [_cols(z, ts, WIDTH, CB_MQ), _full(mk), _full(mv)],
                  [_orow(S, (WIDTH,), F32, ts)])
    return o


def _mem_attn_bwd(do, z, mk, mv, ts):
    S = z.shape[0]
    M = mk.shape[0]

    def body(t, first, ins, outs, scratch):
        do_ref, q_ref, mk_ref, mv_ref = ins
        dq_ref, dmk_ref, dmv_ref = outs

        @pl.when(first)
        def _():
            dmk_ref[...] = jnp.zeros_like(dmk_ref)
            dmv_ref[...] = jnp.zeros_like(dmv_ref)

        scale = MEM_DIM ** -0.5
        for h in range(MEM_HEADS):
            cols = slice(h * MEM_DIM, (h + 1) * MEM_DIM)
            qb = _b16(q_ref[:, cols])
            kb = _b16(mk_ref[:, cols])
            dob = _b16(do_ref[:, cols])
            s = _dot(qb, kb, NT) * scale
            m = jnp.max(s, axis=1, keepdims=True)
            p = jnp.exp(s - m)
            p = p / jnp.sum(p, axis=1, keepdims=True)
            dmv_ref[:, cols] += _dot(_b16(p), dob, TN)
            dp = _dot(dob, _b16(mv_ref[:, cols]), NT)
            ds = p * (dp - jnp.sum(dp * p, axis=1, keepdims=True)) * scale
            dsb = _b16(ds)
            dq_ref[:, cols] = _dot(dsb, kb)
            dmk_ref[:, cols] += _dot(dsb, qb, TN)

    return _tiled("mem_attn_bwd", body, S // ts,
                  [_rows(do, ts), _cols(z, ts, WIDTH, CB_MQ), _full(mk), _full(mv)],
                  [_orow(S, (WIDTH,), F32, ts), _oacc((M, WIDTH), F32), _oacc((M, WIDTH), F32)])


def _head_norm(ob, g):
    xs, rs = [], []
    for h in range(GDN_HEADS):
        o = ob[:, h * GDN_DIM:(h + 1) * GDN_DIM]
        r = lax.rsqrt(jnp.mean(o * o, axis=1, keepdims=True) + EPS)
        xs.append(o * r)
        rs.append(r)
    return xs, rs


def _merge_fwd(x, z, o_a, o_b, o_m, gdn_g, b_merge, wb, wout, ts):
    S, D = x.shape

    def body(t, first, ins, outs, scratch):
        (x_ref, g_ref, oa_ref, az_ref, ob_ref, bz_ref, om_ref, mz_ref, gg_ref, bm_ref, wb_ref,
         wo_ref) = ins
        xo_ref, ya_ref, yb_ref, ym_ref, mg_ref = outs
        ya = oa_ref[...] * _silu_and_grad(az_ref[...])[0]
        xs, _ = _head_norm(ob_ref[...], None)
        nb = jnp.concatenate([xh * gg_ref[...] for xh in xs], axis=1)
        yb = nb * _silu_and_grad(bz_ref[...])[0]
        ym = om_ref[...] * _silu_and_grad(mz_ref[...])[0]
        merged = jnp.zeros((ts, D), F32)
        for n, (y, y_ref) in enumerate(((ya, ya_ref), (yb, yb_ref), (ym, ym_ref))):
            yb16 = _b16(y)
            y_ref[...] = yb16
            gate = _sigmoid(g_ref[:, n * D:(n + 1) * D] + bm_ref[:, n * D:(n + 1) * D])
            merged = merged + gate * _dot(yb16, wb_ref[n])
        mb = _b16(merged)
        mg_ref[...] = mb
        xo_ref[...] = x_ref[...] + _dot(mb, wo_ref[...])

    half = lambda a: _rows(a, ts)
    return _tiled("merge_fwd", body, S // ts,
                  [_rows(x, ts), _cols(z, ts, 3 * D, CB_GATES), half(o_a), _cols(z, ts, WIDTH, CB_AZ),
                   half(o_b), _cols(z, ts, WIDTH, CB_BZ), half(o_m), _cols(z, ts, WIDTH, CB_MZ),
                   _full(gdn_g.reshape(1, GDN_DIM)), _full(b_merge.reshape(1, 3 * D)), _full(wb), _full(wout)],
                  [_orow(S, (D,), F32, ts), _orow(S, (WIDTH,), BF16, ts), _orow(S, (WIDTH,), BF16, ts),
                   _orow(S, (WIDTH,), BF16, ts), _orow(S, (D,), BF16, ts)])


def _merge_bwd(dout, z, o_a, o_b, o_m, ya, yb, ym, gdn_g, b_merge, wb, wout, hsum, ts):
    S, D = dout.shape

    def body(t, first, ins, outs, scratch):
        (do_ref, g_ref, oa_ref, az_ref, ob_ref, bz_ref, om_ref, mz_ref, ya_ref, yb_ref, ym_ref,
         gg_ref, bm_ref, wb_ref, wo_ref, hs_ref) = ins
        (dg_ref, dpa_ref, dpb_ref, dpm_ref, doa_ref, dob_ref, dom_ref, daz_ref, dbz_ref, dmz_ref,
         dl_ref, dbm_ref, dgg_ref) = outs

        @pl.when(first)
        def _():
            dbm_ref[...] = jnp.zeros_like(dbm_ref)
            dgg_ref[...] = jnp.zeros_like(dgg_ref)

        dmerged = _dot(_b16(do_ref[...]), wo_ref[...], NT)
        dys = []
        for n, (y_ref, dp_ref) in enumerate(((ya_ref, dpa_ref), (yb_ref, dpb_ref), (ym_ref, dpm_ref))):
            sl = slice(n * D, (n + 1) * D)
            gate = _sigmoid(g_ref[:, sl] + bm_ref[:, sl])
            proj = _dot(y_ref[...], wb_ref[n])
            dproj = _b16(gate * dmerged)
            dp_ref[...] = dproj
            dgp = dmerged * proj * gate * (1.0 - gate)
            dg_ref[:, sl] = dgp.astype(dg_ref.dtype)
            dbm_ref[0:1, sl] += jnp.sum(dgp, axis=0, keepdims=True)
            dys.append(_dot(dproj, wb_ref[n], NT))
        dya, dyb, dym = dys
        sa, dsa = _silu_and_grad(az_ref[...])
        oa = oa_ref[...]
        doa = dya * sa
        doa_ref[...] = doa
        daz_ref[...] = dya * oa * dsa
        dl_ref[...] = _dot(hs_ref[...], doa * oa, NT, HIGHEST)
        sm, dsm = _silu_and_grad(mz_ref[...])
        dom_ref[...] = dym * sm
        dmz_ref[...] = dym * om_ref[...] * dsm
        sb, dsb = _silu_and_grad(bz_ref[...])
        xs, rs = _head_norm(ob_ref[...], None)
        gg = gg_ref[...]
        dgg = jnp.zeros((1, GDN_DIM), F32)
        for h in range(GDN_HEADS):
            cols = slice(h * GDN_DIM, (h + 1) * GDN_DIM)
            dn = dyb[:, cols] * sb[:, cols]
            dbz_ref[:, cols] = dyb[:, cols] * (xs[h] * gg) * dsb[:, cols]
            dgg = dgg + jnp.sum(dn * xs[h], axis=0, keepdims=True)
            dxh = dn * gg
            dob_ref[:, cols] = rs[h] * (dxh - xs[h] * jnp.mean(dxh * xs[h], axis=1, keepdims=True))
        dgg_ref[0:1, :] += dgg

    half = lambda a: _rows(a, ts)
    w512 = lambda dt: _orow(S, (WIDTH,), dt, ts)
    return _tiled("merge_bwd", body, S // ts,
                  [_rows(dout, ts), _cols(z, ts, 3 * D, CB_GATES), half(o_a), _cols(z, ts, WIDTH, CB_AZ),
                   half(o_b), _cols(z, ts, WIDTH, CB_BZ), half(o_m), _cols(z, ts, WIDTH, CB_MZ),
                   half(ya), half(yb), half(ym), _full(gdn_g.reshape(1, GDN_DIM)),
                   _full(b_merge.reshape(1, 3 * D)), _full(wb), _full(wout), _full(hsum)],
                  [_orow(S, (3 * D,), BF16, ts), _orow(S, (D,), BF16, ts), _orow(S, (D,), BF16, ts),
                   _orow(S, (D,), BF16, ts), w512(F32), w512(F32), w512(F32), w512(F32), w512(F32),
                   w512(F32), ((128, S), F32, (128, ts), lambda i: (0, i)), _oacc((8, 3 * D), F32),
                   _oacc((8, GDN_DIM), F32)])


def _to_aligned(w):
    sizes = (512, 512, 512, 8, 512, 512, 512, 512, 4, 4, 512, 512, 512, 3072)
    names = ("aq", "ak", "av", "af", "az", "bq", "bk", "bv", "ba", "bb", "bz", "mq", "mz", "gates")
    p, off = {}, 0
    for n, s in zip(names, sizes):
        p[n] = w[..., off:off + s]
        off += s
    pad = jnp.zeros(w.shape[:-1] + (128 - 16,), w.dtype)
    return jnp.concatenate([p["gates"], p["bq"], p["bk"], p["bv"], p["aq"], p["ak"], p["av"], p["az"],
                            p["bz"], p["mq"], p["mz"], p["af"], p["ba"], p["bb"], pad], axis=-1)


def _from_aligned(w):
    c = lambda lo, n: w[..., lo:lo + n]
    gates, bq, bk, bv = c(0, 3072), c(3072, 512), c(3584, 512), c(4096, 512)
    aq, ak, av, az = c(4608, 512), c(5120, 512), c(5632, 512), c(6144, 512)
    bz, mq, mz = c(6656, 512), c(7168, 512), c(7680, 512)
    af, ba, bb = c(8192, 8), c(8200, 4), c(8204, 4)
    return jnp.concatenate([aq, ak, av, af, az, bq, bk, bv, ba, bb, bz, mq, mz, gates], axis=-1)


def _lanes128(v, lane0):
    return jnp.pad(v.astype(F32)[None, :], ((0, 0), (lane0, 128 - lane0 - v.shape[0])))


def _tiles(S):
    ts = min(512, S // 2)
    return dict(ts=ts, ts_small=min(256, S // 2), tq=min(512, S // 4), tq_fwd=min(1024, S // 2))


def _layer_fwd(x, mem, p):
    S = x.shape[0]
    tl = _tiles(S)
    ts, tss, tq = tl["ts"], tl["ts_small"], tl["tq"]
    h, rstd = _rms_fwd("norm_fwd", x, p["norm_g"], ts)
    z = _mm("in_proj", h, p["w_in_al"], tn=1664)

    b_fg128 = _lanes128(p["b_fg"], LANE_AF)
    fT, f_hi, f_mid, f_lo = _fox_decay(z, b_fg128, ts)
    aq = z[:, CB_AQ * WIDTH:(CB_AQ + 1) * WIDTH]
    ak = z[:, CB_AK * WIDTH:(CB_AK + 1) * WIDTH]
    av = z[:, CB_AV * WIDTH:(CB_AV + 1) * WIDTH]
    q32 = _heads_major(aq, FOX_HEADS, FOX_DIM)
    kh = _heads_major(ak, FOX_HEADS, FOX_DIM).astype(BF16)
    vh = _heads_major(av, FOX_HEADS, FOX_DIM).astype(BF16)
    khT = kh.transpose(0, 2, 1)
    piecesT = jnp.stack([f[:FOX_HEADS] for f in (f_hi, f_mid, f_lo)], axis=1)
    pieces = piecesT.transpose(0, 2, 1)
    ones3 = jnp.ones((FOX_HEADS, S, 3), BF16)
    padk = jnp.zeros((FOX_HEADS, S, FOX_AUG - FOX_DIM - 6), BF16)
    q_aug = jnp.concatenate([q32, pieces.astype(F32), ones3.astype(F32), padk.astype(F32)], axis=-1)
    k_aug = jnp.concatenate([kh, ones3, -pieces, padk], axis=-1)
    kT_aug = jnp.concatenate([khT, ones3.transpose(0, 2, 1), -piecesT, padk.transpose(0, 2, 1)], axis=1)
    v_aug = jnp.concatenate([vh, ones3[:, :, :1], jnp.zeros((FOX_HEADS, S, 128 - FOX_DIM - 1), BF16)],
                            axis=-1)
    o_h, lse, qs = _fox_fwd(q_aug, kT_aug, v_aug, tl["tq_fwd"])
    o_a = _heads_minor(o_h)

    a128 = _lanes128(p["a_log"], LANE_BA)
    dt128 = _lanes128(p["dt_bias"], LANE_BA)
    qkv, cpre, gb, gbT = _gdn_prep(z, p["conv_w"], a128, dt128, ts)
    grow = gbT[LANE_BA:LANE_BA + GDN_HEADS].reshape(GDN_HEADS, S // CHUNK, CHUNK)
    u, w, qd, kd, aqk, T = _gdn_local_fwd(qkv, gb, grow, ts)
    o_b, vn, states = _gdn_scan_fwd(u, w, qd, kd, aqk, gb, ts)

    mem_h, mem_r = _rms_fwd("mem_norm_fwd", mem, p["mem_norm_g"], mem.shape[0])
    mkv = _mm("mem_kv", mem_h, p["w_mem_kv"])
    mk, mv = mkv[:, :WIDTH], mkv[:, WIDTH:]
    o_m = _mem_attn_fwd(z, mk, mv, ts)

    x_next, ya, yb, ym, merged = _merge_fwd(x, z, o_a, o_b, o_m, p["gdn_norm_g"], p["b_merge"],
                                            p["w_branch"], p["w_out"], tss)
    saved = dict(x=x, h=h, rstd=rstd, z=z, b_fg128=b_fg128, qs=qs, k_aug=k_aug, khT=khT, vh=vh, lse=lse, o_a=o_a, a128=a128, dt128=dt128, qkv=qkv, cpre=cpre, gb=gb,
                 grow=grow, w=w, qd=qd, kd=kd, aqk=aqk, T=T, o_b=o_b, vn=vn, states=states,
                 mem_h=mem_h, mem_r=mem_r, mk=mk, mv=mv, o_m=o_m, ya=ya, yb=yb, ym=ym, merged=merged)
    return x_next, saved


def _layer_bwd(dout, mem, p, s):
    S = dout.shape[0]
    tl = _tiles(S)
    ts, tss, tq = tl["ts"], tl["ts_small"], tl["tq"]
    z = s["z"]
    hsum = (jnp.arange(128)[:, None] == jnp.arange(WIDTH)[None, :] // FOX_DIM).astype(F32)
    (dgates, dpa, dpb, dpm, do_a, do_b, do_m, daz, dbz, dmz, deltaT, db_merge, dgdn_g) = _merge_bwd(
        dout, z, s["o_a"], s["o_b"], s["o_m"], s["ya"], s["yb"], s["ym"], p["gdn_norm_g"],
        p["b_merge"], p["w_branch"], p["w_out"], hsum, tss)
    g = {}
    g["b_merge"] = db_merge[0]
    g["gdn_norm_g"] = dgdn_g[0]
    g["w_out"] = _mm("dw_out", s["merged"], dout, ta=True)
    g["w_branch"] = jnp.stack([_mm("dw_branch", y, dp, ta=True)
                               for y, dp in ((s["ya"], dpa), (s["yb"], dpb), (s["ym"], dpm))])

    do_h = _heads_major(do_a, FOX_HEADS, FOX_DIM).astype(BF16)
    delta_row = deltaT[:FOX_HEADS, None, :]
    dqT, dk_h, dv_h, dfk, dfq = _fox_bwd(s["qs"], s["k_aug"], s["khT"], s["vh"], do_h, s["lse"],
                                         delta_row, tq)
    daq = _heads_minor(dqT.transpose(0, 2, 1))
    dak = _heads_minor(dk_h)
    dav = _heads_minor(dv_h)
    daf128, db_fg = _fox_decay_bwd(dfk[:, 0, :], dfq[:, 0, :], z, s["b_fg128"], ts)
    g["b_fg"] = db_fg[:FOX_HEADS]

    du, dw, dqd, dkd, daqk, dgl = _gdn_scan_bwd(do_b, s["w"], s["qd"], s["kd"], s["aqk"], s["vn"],
                                                s["states"], s["gb"], ts)
    dqkv, dgb = _gdn_local_bwd(s["qkv"], s["gb"], s["grow"], s["T"], du, dw, dqd, dkd, daqk, dgl, ts)
    dbqkv, dzs_b, dconv, dad = _gdn_prep_bwd(dqkv, dgb, s["cpre"], z, p["conv_w"], s["a128"],
                                             s["dt128"], ts)
    g["conv_w"] = dconv[:4]
    g["a_log"] = dad[0, LANE_BA:LANE_BA + GDN_HEADS]
    g["dt_bias"] = dad[1, LANE_BA:LANE_BA + GDN_HEADS]

    dmq, dmk, dmv = _mem_attn_bwd(do_m, z, s["mk"], s["mv"], ts)
    dmkv = jnp.concatenate([dmk, dmv], axis=1)
    g["w_mem_kv"] = _mm("dw_mem_kv", s["mem_h"], dmkv, ta=True)
    dmem_h = _mm("dmem_h", dmkv, p["w_mem_kv"], tb=True)
    M = mem.shape[0]
    _, g["mem_norm_g"] = _rms_bwd("mem_norm_bwd", dmem_h, mem, s["mem_r"], p["mem_norm_g"],
                                  jnp.zeros_like(mem), M)

    lane = jnp.arange(128)[None, :]
    dsmall = jnp.where(lane < 8, daf128, dzs_b)
    dz = jnp.concatenate([dgates, _b16(dbqkv), _b16(daq), _b16(dak), _b16(dav), _b16(daz), _b16(dbz),
                          _b16(dmq), _b16(dmz), _b16(dsmall)], axis=1)
    g["w_in_al"] = _mm("dw_in", s["h"], dz, ta=True, tn=1664)
    dh = _mm("dh", dz, p["w_in_al"], tb=True, tk=1664)
    dx, g["norm_g"] = _rms_bwd("norm_bwd", dh, s["x"], s["rstd"], p["norm_g"], dout, ts)
    return dx, g


def _local_step(x, mem, layers, final_norm_g, loss_target):
    S = x.shape[0]
    saves = []
    cur = x
    for p in layers:
        cur, sv = _layer_fwd(cur, mem, p)
        saves.append(sv)
    dx, dgf, loss_lanes = _loss_head(cur, final_norm_g, loss_target, _tiles(S)["ts"])
    grads = [None] * len(layers)
    for l in reversed(range(len(layers))):
        dx, grads[l] = _layer_bwd(dx, mem, layers[l], saves[l])
    return loss_lanes, dx, grads, dgf


HBM_SPEC = pl.BlockSpec(memory_space=pltpu.HBM)


def _mesh_pos():
    return lax.axis_index("x"), lax.axis_index("y"), lax.axis_index("c")


def _comm_call(name, body, arrays, out_shapes, n_remote, n_local):
    n = len(arrays)

    def kern(*refs):
        body(refs[:n], refs[n:2 * n], refs[2 * n], refs[2 * n + 1], refs[2 * n + 2])

    return pl.pallas_call(
        kern, name=name, out_shape=out_shapes, in_specs=[HBM_SPEC] * n, out_specs=[HBM_SPEC] * n,
        scratch_shapes=[pltpu.SemaphoreType.DMA((n_remote,)), pltpu.SemaphoreType.DMA((n_remote,)),
                        pltpu.SemaphoreType.DMA((max(n_local, 1),))],
    )(*arrays)


def _remote(src, dst, send_sems, recv_sems, k, to):
    return pltpu.make_async_remote_copy(src_ref=src, dst_ref=dst, send_sem=send_sems.at[k],
                                        recv_sem=recv_sems.at[k], device_id=to, device_id_type=MESH_ID)


def _other_chips(mx, my):
    return [(1 - mx, my), (mx, 1 - my), (1 - mx, 1 - my)]


def _gather_chips(name, shards):
    n = len(shards)

    def body(ins, outs, send_sems, recv_sems, local_sems):
        mx, my, mc = _mesh_pos()
        me = 2 * mx + my
        sibling = (mx, my, 1 - mc)
        chips = _other_chips(mx, my)
        sends = []
        for a in range(n):
            for k, (px, py) in enumerate(chips):
                cp = _remote(ins[a].at[mc], outs[a].at[me, mc], send_sems, recv_sems, 6 * a + k,
                             (px, py, mc))
                cp.start()
                sends.append(cp)
        for a in range(n):
            for k, (px, py) in enumerate(chips):
                j = 2 * px + py
                _remote(ins[a].at[mc], outs[a].at[j, mc], send_sems, recv_sems, 6 * a + k,
                        (px, py, mc)).wait_recv()
                cp = _remote(outs[a].at[j, mc], outs[a].at[j, mc], send_sems, recv_sems, 6 * a + 3 + k,
                             sibling)
                cp.start()
                sends.append(cp)
        for a in range(n):
            for k, (px, py) in enumerate(chips):
                j = 2 * px + py
                _remote(outs[a].at[j, 1 - mc], outs[a].at[j, 1 - mc], send_sems, recv_sems,
                        6 * a + 3 + k, sibling).wait_recv()
        for cp in sends:
            cp.wait_send()

    shapes = [jax.ShapeDtypeStruct((N_CHIPS,) + s.shape, s.dtype) for s in shards]
    outs = _comm_call(name, body, shards, shapes, 6 * n, 0)
    me = 2 * lax.axis_index("x") + lax.axis_index("y")
    return [lax.dynamic_update_index_in_dim(o, s, me, 0) for o, s in zip(outs, shards)]


def _sibling_swap(gs):
    n = len(gs)

    def body(ins, outs, send_sems, recv_sems, local_sems):
        mx, my, mc = _mesh_pos()
        sends = []
        for a in range(n):
            cp = _remote(ins[a].at[:, 1 - mc], outs[a], send_sems, recv_sems, a, (mx, my, 1 - mc))
            cp.start()
            sends.append(cp)
        for cp in sends:
            cp.wait()

    shapes = [jax.ShapeDtypeStruct((g.shape[0],) + g.shape[2:], g.dtype) for g in gs]
    return _comm_call("grad_sibling_swap", body, gs, shapes, n, 0)


def _chip_exchange(ps):
    n = len(ps)

    def body(ins, outs, send_sems, recv_sems, local_sems):
        mx, my, mc = _mesh_pos()
        me = 2 * mx + my
        chips = _other_chips(mx, my)
        sends = []
        for a in range(n):
            for k, (px, py) in enumerate(chips):
                cp = _remote(ins[a].at[2 * px + py], outs[a].at[me], send_sems, recv_sems, 3 * a + k,
                             (px, py, mc))
                cp.start()
                sends.append(cp)
        for a in range(n):
            for k, (px, py) in enumerate(chips):
                _remote(ins[a].at[me], outs[a].at[2 * px + py], send_sems, recv_sems, 3 * a + k,
                        (px, py, mc)).wait_recv()
        for cp in sends:
            cp.wait_send()

    shapes = [jax.ShapeDtypeStruct(p.shape, p.dtype) for p in ps]
    outs = _comm_call("grad_chip_exchange", body, ps, shapes, 3 * n, 0)
    me = 2 * lax.axis_index("x") + lax.axis_index("y")
    return [lax.dynamic_update_index_in_dim(o, lax.dynamic_index_in_dim(p, me, 0, keepdims=False), me, 0)
            for o, p in zip(outs, ps)]


def _sibling_gather(hs):
    n = len(hs)

    def body(ins, outs, send_sems, recv_sems, local_sems):
        mx, my, mc = _mesh_pos()
        sends = []
        for a in range(n):
            cp = _remote(ins[a], outs[a], send_sems, recv_sems, a, (mx, my, 1 - mc))
            cp.start()
            sends.append(cp)
        for cp in sends:
            cp.wait()

    shapes = [jax.ShapeDtypeStruct(h.shape, h.dtype) for h in hs]
    theirs = _comm_call("grad_sibling_gather", body, hs, shapes, n, 0)
    first = lax.axis_index("c") == 0
    return [jnp.stack([jnp.where(first, h, t), jnp.where(first, t, h)]) for h, t in zip(hs, theirs)]


def _add_pairs(a, b, tr, out_dtype):
    n, H, C = a.shape

    def kern(a_ref, b_ref, o_ref):
        o_ref[...] = (a_ref[...] + b_ref[...]).astype(o_ref.dtype)

    spec = pl.BlockSpec((None, tr, C), lambda j, i: (j, i, 0))
    return pl.pallas_call(
        kern, name="grad_pair_sum", grid=(n, H // tr), in_specs=[spec, spec], out_specs=spec,
        out_shape=jax.ShapeDtypeStruct((n, H, C), out_dtype),
        compiler_params=_params(("parallel", "parallel")),
    )(a, b)


def _sum_slots(r4, tr):
    n, H, C = r4.shape

    def kern(r_ref, o_ref):
        f = lambda k: r_ref[k].astype(F32)
        o_ref[...] = ((f(0) + f(1)) + f(2)) + f(3)

    return pl.pallas_call(
        kern, name="grad_chip_sum", grid=(H // tr,),
        in_specs=[pl.BlockSpec((n, tr, C), lambda i: (0, i, 0))],
        out_specs=pl.BlockSpec((tr, C), lambda i: (i, 0)),
        out_shape=jax.ShapeDtypeStruct((H, C), F32),
        compiler_params=_params(("parallel",)),
    )(r4)


def _adamw(w, g, m, v, tr):
    R, C = w.shape
    c1 = 1.0 - ADAM_B1
    c2 = 1.0 - ADAM_B2
    bc1 = 1.0 - ADAM_B1 ** ADAM_STEP
    bc2 = 1.0 - ADAM_B2 ** ADAM_STEP

    def kern(w_ref, g_ref, m_ref, v_ref, d_ref, mo_ref, vo_ref):
        gv = g_ref[...]
        mn = ADAM_B1 * m_ref[...] + c1 * gv
        vn = ADAM_B2 * v_ref[...] + c2 * (gv * gv)
        m_hat = mn / bc1
        v_hat = vn / bc2
        d_ref[...] = -ADAM_LR * (m_hat / (jnp.sqrt(v_hat) + ADAM_EPS) + ADAM_WD * w_ref[...])
        mo_ref[...] = mn
        vo_ref[...] = vn

    spec = pl.BlockSpec((tr, C), lambda i: (i, 0))
    shape = jax.ShapeDtypeStruct((R, C), F32)
    return pl.pallas_call(
        kern, name="adamw", grid=(R // tr,), in_specs=[spec] * 4, out_specs=[spec] * 3,
        out_shape=[shape] * 3, compiler_params=_params(("parallel",)),
    )(w, g, m, v)


PACK_COLS = 1024
PACK_ROWS = 512
W_SHARD = N_IN // N_CHIPS
SLAB = ("conv_w", "w_mem_kv", "w_branch", "w_out")
SMALL =("norm_g", "b_fg", "b_merge", "a_log", "dt_bias", "gdn_norm_g", "mem_norm_g", "final_norm_g")
ALL_WEIGHTS = ("norm_g", "w_in", "b_fg", "b_merge", "conv_w", "a_log", "dt_bias", "gdn_norm_g",
               "mem_norm_g", "w_mem_kv", "w_branch", "w_out", "final_norm_g")
SHARD_AXIS = {"w_in": 2, "conv_w": 2, "w_mem_kv": 1, "w_branch": 3, "w_out": 1}


def _pack(arrays, row_multiple):
    flat = jnp.concatenate([a.reshape(-1) for a in arrays])
    n = flat.shape[0]
    rows = -(-n // PACK_COLS)
    rows = -(-rows // row_multiple) * row_multiple
    flat = jnp.pad(flat, (0, rows * PACK_COLS - n))
    return flat.reshape(rows, PACK_COLS)


def _unpack(slab, shapes):
    out, off = [], 0
    for shp in shapes:
        n = 1
        for d in shp:
            n *= d
        r0, r1 = off // PACK_COLS, -(-(off + n) // PACK_COLS)
        rows = slab[r0:r1].reshape(-1)
        out.append(rows[off - r0 * PACK_COLS:off - r0 * PACK_COLS + n].reshape(shp))
        off += n
    return out


def _shard_of(full, name, j):
    ax = SHARD_AXIS[name]
    n = full.shape[ax] // N_CHIPS
    return lax.slice_in_dim(full, j * n, (j + 1) * n, axis=ax)


W_IN_PIECES = ((0, 512, 4608), (512, 512, 5120), (1024, 512, 5632), (1536, 8, 8192), (1544, 512, 6144),
               (2056, 512, 3072), (2568, 512, 3584), (3080, 512, 4096), (3592, 4, 8200), (3596, 4, 8204),
               (3600, 512, 6656), (4112, 512, 7168), (4624, 512, 7680), (5136, 3072, 0))


def _aligned_from_shards(shards):
    def cols(lo, n):
        parts = []
        while n > 0:
            j, off = divmod(lo, W_SHARD)
            take = min(n, W_SHARD - off)
            parts.append(shards[j][..., off:off + take])
            lo, n = lo + take, n - take
        return parts

    out = []
    for lo, n, _ in sorted(W_IN_PIECES, key=lambda p: p[2]):
        out += cols(lo, n)
    out.append(jnp.zeros(shards[0].shape[:-1] + (N_AL - N_IN,), shards[0].dtype))
    return jnp.concatenate(out, axis=-1)


def _shard_from_aligned(w_al, j):
    lo_j, hi_j = j * W_SHARD, (j + 1) * W_SHARD
    parts = []
    for lo, n, al in W_IN_PIECES:
        a, b = max(lo, lo_j), min(lo + n, hi_j)
        if a < b:
            parts.append(w_al[..., al + a - lo:al + b - lo])
    return jnp.concatenate(parts, axis=-1)


def kernel(x, mem, norm_g, w_in, b_fg, b_merge, conv_w, a_log, dt_bias, gdn_norm_g, mem_norm_g, w_mem_kv, w_branch, w_out, final_norm_g, loss_target, m_norm_g, m_w_in, m_b_fg, m_b_merge, m_conv_w, m_a_log, m_dt_bias, m_gdn_norm_g, m_mem_norm_g, m_w_mem_kv, m_w_branch, m_w_out, m_final_norm_g, v_norm_g, v_w_in, v_b_fg, v_b_merge, v_conv_w, v_a_log, v_dt_bias, v_gdn_norm_g, v_mem_norm_g, v_w_mem_kv, v_w_branch, v_w_out, v_final_norm_g):
    wts = dict(norm_g=norm_g, w_in=w_in, b_fg=b_fg, b_merge=b_merge, conv_w=conv_w, a_log=a_log,
               dt_bias=dt_bias, gdn_norm_g=gdn_norm_g, mem_norm_g=mem_norm_g, w_mem_kv=w_mem_kv,
               w_branch=w_branch, w_out=w_out, final_norm_g=final_norm_g)
    mom = dict(norm_g=m_norm_g, w_in=m_w_in, b_fg=m_b_fg, b_merge=m_b_merge, conv_w=m_conv_w,
               a_log=m_a_log, dt_bias=m_dt_bias, gdn_norm_g=m_gdn_norm_g, mem_norm_g=m_mem_norm_g,
               w_mem_kv=m_w_mem_kv, w_branch=m_w_branch, w_out=m_w_out, final_norm_g=m_final_norm_g)
    vel = dict(norm_g=v_norm_g, w_in=v_w_in, b_fg=v_b_fg, b_merge=v_b_merge, conv_w=v_conv_w,
               a_log=v_a_log, dt_bias=v_dt_bias, gdn_norm_g=v_gdn_norm_g, mem_norm_g=v_mem_norm_g,
               w_mem_kv=v_w_mem_kv, w_branch=v_w_branch, w_out=v_w_out, final_norm_g=v_final_norm_g)

    big = ("w_in", "w_mem_kv", "w_branch", "w_out")
    gathered = _gather_chips("weight_gather", [wts[n].astype(BF16) for n in big] + [conv_w])
    all_w = dict(zip(big + ("conv_w",), gathered))
    w_in_al = _aligned_from_shards([all_w["w_in"][j] for j in range(N_CHIPS)])

    layers = []
    for l in range(DEPTH):
        rows_of = lambda n: all_w[n][:, l].reshape(D_MODEL, D_MODEL)
        last_of = lambda n: jnp.concatenate([all_w[n][j, l] for j in range(N_CHIPS)], axis=-1)
        layers.append(dict(norm_g=norm_g[l], w_in_al=w_in_al[l], b_fg=b_fg[l], b_merge=b_merge[l],
                           conv_w=jnp.pad(last_of("conv_w"), ((0, 4), (0, 0))), a_log=a_log[l],
                           dt_bias=dt_bias[l], gdn_norm_g=gdn_norm_g[l], mem_norm_g=mem_norm_g[l],
                           w_mem_kv=rows_of("w_mem_kv"), w_branch=last_of("w_branch"),
                           w_out=rows_of("w_out")))

    loss_lanes, dx, grads, dgf = _local_step(x[0], mem[0], layers, final_norm_g, loss_target[0])

    gfull = {n: jnp.stack([grads[l][n] for l in range(DEPTH)])
             for n in ("norm_g", "b_fg", "b_merge", "conv_w", "a_log", "dt_bias", "gdn_norm_g",
                       "mem_norm_g", "w_mem_kv", "w_branch", "w_out")}
    gfull["final_norm_g"] = dgf
    loss_local = jnp.sum(loss_lanes).reshape(1)
    small_g = [gfull[n] for n in SMALL] + [loss_local]
    dw_al = jnp.stack([grads[l]["w_in_al"] for l in range(DEPTH)])
    ga = jnp.stack([_shard_from_aligned(dw_al, j) for j in range(N_CHIPS)])
    gb = jnp.stack([_pack([_shard_of(gfull[n], n, j) for n in SLAB] + small_g, PACK_ROWS)
                    for j in range(N_CHIPS)])
    R = gb.shape[1]
    gb = gb.reshape(N_CHIPS, 2, R // 2, PACK_COLS)

    mc = lax.axis_index("c")
    tr = 256
    from_sibling = _sibling_swap([ga, gb])
    mine = [lax.dynamic_index_in_dim(g, mc, axis=1, keepdims=False) for g in (ga, gb)]
    pair = [_add_pairs(a, b, tr, dt) for a, b, dt in zip(mine, from_sibling, (BF16, F32))]
    slots = _chip_exchange(pair)
    half = [_sum_slots(s, tr) for s in slots]
    ga_sum, gb_sum = _sibling_gather(half)
    gb_sum = gb_sum.reshape(R, PACK_COLS)

    zero1 = jnp.zeros((1,), F32)
    slab = lambda d: _pack([d[n] for n in SLAB] + [d[n] for n in SMALL] + [zero1], PACK_ROWS)
    delta_s, m_s, v_s = _adamw(slab(wts), gb_sum, slab(mom), slab(vel), tr)
    flat_in = lambda a: a.reshape(DEPTH * D_MODEL, W_SHARD)
    in_res = _adamw(flat_in(w_in), flat_in(ga_sum), flat_in(m_w_in), flat_in(v_w_in), tr)

    names = list(SLAB) + list(SMALL)
    shapes = [wts[n].shape for n in names] + [(1,)]
    g_un = dict(zip(names + ["loss"], _unpack(gb_sum, shapes)))
    d_un = dict(zip(names, _unpack(delta_s, shapes[:-1])))
    m_un = dict(zip(names, _unpack(m_s, shapes[:-1])))
    v_un = dict(zip(names, _unpack(v_s, shapes[:-1])))
    g_un["w_in"] = ga_sum
    d_un["w_in"], m_un["w_in"], v_un["w_in"] = [r.reshape(w_in.shape) for r in in_res]

    loss = g_un["loss"][0]
    return (loss, dx[None], *[g_un[n] for n in ALL_WEIGHTS], *[d_un[n] for n in ALL_WEIGHTS],
            *[m_un[n] for n in ALL_WEIGHTS], *[v_un[n] for n in ALL_WEIGHTS])
```

```python
import functools

import jax
import jax.numpy as jnp
from jax import lax
from jax.experimental import pallas as pl
from jax.experimental.pallas import tpu as pltpu

F32 = jnp.float32
BF16 = jnp.bfloat16
HIGHEST = lax.Precision.HIGHEST
PREC_UT = lax.Precision.HIGH
MESH_ID = pl.DeviceIdType.MESH

D_MODEL = 1024
DEPTH = 2
CHUNK = 64
EPS = 1e-6
FOX_HEADS, FOX_DIM = 8, 64
GDN_HEADS, GDN_DIM = 4, 128
MEM_HEADS, MEM_DIM = 4, 128
WIDTH = 512
N_BRANCH = 3
N_IN = 8208
N_AL = 8320
N_CHIPS = 4
NEG = -1e30
LOG2E = 1.4426950408889634
LN2 = 0.6931471805599453

ADAM_LR, ADAM_B1, ADAM_B2, ADAM_EPS, ADAM_WD, ADAM_STEP = 0.001, 0.9, 0.999, 1e-08, 0.01, 10

CB_GATES = 0
CB_AZ, CB_BZ, CB_MZ = 6, 7, 8
CB_MERGE = 0
CB_BQKV = 3
CB_AQ, CB_AK, CB_AV = 12, 13, 14
CB_AQKV = 4
CB_MQ = 15
CB_SMALL = 64
W_IN_PIECES = ((0, 512, 6144), (512, 512, 6656), (1024, 512, 7168), (1536, 8, 8192), (1544, 512, 3072),
               (2056, 512, 4608), (2568, 512, 5120), (3080, 512, 5632), (3592, 4, 8200), (3596, 4, 8204),
               (3600, 512, 3584), (4112, 512, 7680), (4624, 512, 4096), (5136, 3072, 0))
LANE_AF, LANE_BA, LANE_BB = 0, 8, 12

NN = ((1,), (0,))
NT = ((1,), (1,))
TN = ((0,), (0,))

VMEM_LIMIT_BYTES = 56 * 1024 * 1024


def _dot(a, b, dims=NN, prec=None):
    return lax.dot_general(a, b, (dims, ((), ())), preferred_element_type=F32, precision=prec)


def _bdot(a, b, ca, cb, prec=None):
    return lax.dot_general(a, b, (((ca,), (cb,)), ((0,), (0,))), preferred_element_type=F32,
                           precision=prec)


def _b16(a):
    return a.astype(BF16)


def _eye(n, dtype=F32):
    r = lax.broadcasted_iota(jnp.int32, (n, n), 0)
    c = lax.broadcasted_iota(jnp.int32, (n, n), 1)
    return jnp.where(r == c, 1.0, 0.0).astype(dtype)


def _transpose_exact(x):
    return _dot(_eye(x.shape[1]), x, NT, HIGHEST)


def _col_to_row(col):
    n = col.shape[0]
    return jnp.sum(jnp.where(_eye(n) > 0.5, col, 0.0), axis=0, keepdims=True)


def _row_to_col(row):
    n = row.shape[1]
    return jnp.sum(jnp.where(_eye(n) > 0.5, row, 0.0), axis=1, keepdims=True)


def _sigmoid(x):
    return 1.0 / (1.0 + jnp.exp(-x))


def _softplus(x):
    return jnp.maximum(x, 0.0) + jnp.log(1.0 + jnp.exp(-jnp.abs(x)))


def _silu_and_grad(x):
    s = _sigmoid(x)
    return x * s, s * (1.0 + x * (1.0 - s))


def _params(semantics):
    return pltpu.CompilerParams(dimension_semantics=semantics, vmem_limit_bytes=VMEM_LIMIT_BYTES)


def _rows(a, ts):
    nd = a.ndim
    return (a, (ts,) + a.shape[1:], lambda i, nd=nd: (i,) + (0,) * (nd - 1))


def _cols(a, ts, width, cb):
    return (a, (ts, width), lambda i, cb=cb: (i, cb))


def _full(a):
    nd = a.ndim
    return (a, a.shape, lambda i, nd=nd: (0,) * nd)


def _orow(S, tail, dtype, ts):
    nd = 1 + len(tail)
    return ((S,) + tuple(tail), dtype, (ts,) + tuple(tail), lambda i, nd=nd: (i,) + (0,) * (nd - 1))


def _oacc(shape, dtype):
    nd = len(shape)
    return (tuple(shape), dtype, tuple(shape), lambda i, nd=nd: (0,) * nd)


def _tiled(name, body, n_steps, ins, outs, scratch=(), reverse=False, fill=None):
    def rev(imap):
        if not reverse:
            return imap
        return lambda i: imap(n_steps - 1 - i)

    in_specs = [pl.BlockSpec(blk, rev(imap)) for (_, blk, imap) in ins]
    out_specs = [pl.BlockSpec(blk, rev(imap)) for (_, _, blk, imap) in outs]
    out_shape = [jax.ShapeDtypeStruct(shape, dt) for (shape, dt, _, _) in outs]
    n_in, n_out = len(ins), len(outs)
    arrays = [a for (a, _, _) in ins]
    aliases = {}
    n_extra = 0
    if fill is not None:
        arrays.append(fill[0])
        in_specs.append(pl.BlockSpec(memory_space=pl.ANY))
        aliases = {n_in: fill[1]}
        n_extra = 1

    def kern(*refs):
        step = pl.program_id(0)
        t = (n_steps - 1 - step) if reverse else step
        lo = n_in + n_extra
        body(t, step == 0, refs[:n_in], refs[lo:lo + n_out], refs[lo + n_out:])

    res = pl.pallas_call(
        kern, name=name, grid=(n_steps,), in_specs=in_specs, out_specs=out_specs,
        out_shape=out_shape, scratch_shapes=list(scratch), input_output_aliases=aliases,
        compiler_params=_params(("arbitrary",)),
    )(*arrays)
    return res


def _pick(n, pref):
    if n <= pref:
        return n
    best = None
    for t in range(128, pref + 1, 128):
        if n % t == 0:
            best = t
    assert best is not None, (n, pref)
    return best


def _mm(name, a, b, ta=False, tb=False, out_dtype=F32, tm=1024, tn=1024, tk=1024):
    if ta:
        K, M = a.shape
    else:
        M, K = a.shape
    if tb:
        N, K2 = b.shape
    else:
        K2, N = b.shape
    assert K == K2, (a.shape, b.shape, ta, tb)
    tm, tn, tk = _pick(M, tm), _pick(N, tn), _pick(K, tk)
    nk = K // tk
    a_spec = (pl.BlockSpec((tk, tm), lambda i, j, k: (k, i)) if ta
              else pl.BlockSpec((tm, tk), lambda i, j, k: (i, k)))
    b_spec = (pl.BlockSpec((tn, tk), lambda i, j, k: (j, k)) if tb
              else pl.BlockSpec((tk, tn), lambda i, j, k: (k, j)))
    dims = ((0,) if ta else (1,), (1,) if tb else (0,))

    def kern_single(a_ref, b_ref, o_ref):
        o_ref[...] = _dot(_b16(a_ref[...]), _b16(b_ref[...]), dims).astype(o_ref.dtype)

    def kern_acc(a_ref, b_ref, o_ref, acc_ref):
        k = pl.program_id(2)

        @pl.when(k == 0)
        def _():
            acc_ref[...] = jnp.zeros_like(acc_ref)

        acc_ref[...] += _dot(_b16(a_ref[...]), _b16(b_ref[...]), dims)

        @pl.when(k == nk - 1)
        def _():
            o_ref[...] = acc_ref[...].astype(o_ref.dtype)

    return pl.pallas_call(
        kern_single if nk == 1 else kern_acc, name=name, grid=(M // tm, N // tn, nk),
        in_specs=[a_spec, b_spec],
        out_specs=pl.BlockSpec((tm, tn), lambda i, j, k: (i, j)),
        out_shape=jax.ShapeDtypeStruct((M, N), out_dtype),
        scratch_shapes=[] if nk == 1 else [pltpu.VMEM((tm, tn), F32)],
        compiler_params=_params(("parallel", "parallel", "arbitrary")),
    )(a, b)


def _rms_fwd(name, x, g, ts):
    S, D = x.shape

    def body(t, first, ins, outs, scratch):
        x_ref, g_ref = ins
        h_ref, r_ref = outs
        xv = x_ref[...]
        r = lax.rsqrt(jnp.mean(xv * xv, axis=1, keepdims=True) + EPS)
        h_ref[...] = (xv * r * g_ref[...]).astype(h_ref.dtype)
        r_ref[...] = r

    return _tiled(name, body, S // ts, [_rows(x, ts), _full(g.reshape(1, D))],
                  [_orow(S, (D,), BF16, ts), _orow(S, (1,), F32, ts)])


def _rms_bwd(name, dh, x, rstd, g, dres, ts):
    S, D = x.shape

    def body(t, first, ins, outs, scratch):
        dh_ref, x_ref, r_ref, g_ref, dres_ref = ins
        dx_ref, dg_ref = outs
        r = r_ref[...]
        xh = x_ref[...] * r
        dhv = dh_ref[...]
        dxh = dhv * g_ref[...]
        dx_ref[...] = dres_ref[...] + r * (dxh - xh * jnp.mean(dxh * xh, axis=1, keepdims=True))

        @pl.when(first)
        def _():
            dg_ref[...] = jnp.zeros_like(dg_ref)

        dg_ref[0:1, :] += jnp.sum(dhv * xh, axis=0, keepdims=True)

    dx, dg = _tiled(name, body, S // ts,
                    [_rows(dh, ts), _rows(x, ts), _rows(rstd, ts), _full(g.reshape(1, D)), _rows(dres, ts)],
                    [_orow(S, (D,), F32, ts), _oacc((8, D), F32)])
    return dx, dg[0]


def _loss_head(x, g, target, ts):
    S, D = x.shape

    def body(t, first, ins, outs, scratch):
        x_ref, g_ref, tgt_ref = ins
        dx_ref, dg_ref, loss_ref = outs
        xv = x_ref[...]
        gv = g_ref[...]
        r = lax.rsqrt(jnp.mean(xv * xv, axis=1, keepdims=True) + EPS)
        xh = xv * r
        err = xh * gv - tgt_ref[...]
        dy = err * (1.0 / D)
        dxh = dy * gv
        dx_ref[...] = r * (dxh - xh * jnp.mean(dxh * xh, axis=1, keepdims=True))

        @pl.when(first)
        def _():
            dg_ref[...] = jnp.zeros_like(dg_ref)
            loss_ref[...] = jnp.zeros_like(loss_ref)

        dg_ref[0:1, :] += jnp.sum(dy * xh, axis=0, keepdims=True)
        per_lane = jnp.sum(err * err, axis=0, keepdims=True)
        loss_ref[0:1, :] += per_lane * (0.5 / D)

    dx, dg, loss = _tiled("loss_head", body, S // ts,
                          [_rows(x, ts), _full(g.reshape(1, D)), _rows(target, ts)],
                          [_orow(S, (D,), F32, ts), _oacc((8, D), F32), _oacc((8, D), F32)])
    return dx, dg[0], loss[0]


def _scan_rows(x, length, seg, reverse=False):
    row = lax.broadcasted_iota(jnp.int32, x.shape, 0) % seg
    k = 1
    while k < seg:
        if reverse:
            x = x + jnp.where(row < seg - k, pltpu.roll(x, length - k, 0), 0.0)
        else:
            x = x + jnp.where(row >= k, pltpu.roll(x, k, 0), 0.0)
        k *= 2
    return x


def _fox_decay(z, b_fg128, ts):
    S = z.shape[0]

    def body(t, first, ins, outs, scratch):
        zs_ref, b_ref = ins
        f_ref, hi_ref, mid_ref, lo_ref = outs
        (carry,) = scratch

        @pl.when(first)
        def _():
            carry[...] = jnp.zeros_like(carry)

        logf = -_softplus(-(zs_ref[...] + b_ref[...]))
        run = _scan_rows(logf, ts, ts) + carry[0:1, :]
        carry[0:1, :] = run[ts - 1:ts, :]
        f_ref[...] = _transpose_exact(run)
        f2 = run * LOG2E
        hi = f2.astype(BF16)
        r1 = f2 - hi.astype(F32)
        mid = r1.astype(BF16)
        lo = (r1 - mid.astype(F32)).astype(BF16)
        eye = _eye(128, BF16)
        hi_ref[...] = _dot(eye, hi, NT).astype(BF16)
        mid_ref[...] = _dot(eye, mid, NT).astype(BF16)
        lo_ref[...] = _dot(eye, lo, NT).astype(BF16)

    tcol = lambda dt: ((128, S), dt, (128, ts), lambda i: (0, i))
    return _tiled("fox_decay", body, S // ts,
                  [_cols(z, ts, 128, CB_SMALL), _full(b_fg128)],
                  [tcol(F32), tcol(BF16), tcol(BF16), tcol(BF16)], scratch=[pltpu.VMEM((8, 128), F32)])


def _fox_decay_bwd(dfk_rows, dfq_rows, z, b_fg128, ts):
    S = z.shape[0]
    H = dfk_rows.shape[0]

    def body(t, first, ins, outs, scratch):
        dfk_ref, dfq_ref, zs_ref, b_ref = ins
        daf_ref, db_ref = outs
        (carry,) = scratch

        @pl.when(first)
        def _():
            carry[...] = jnp.zeros_like(carry)
            db_ref[...] = jnp.zeros_like(db_ref)

        r = lax.broadcasted_iota(jnp.int32, (H, 128), 0)
        c = lax.broadcasted_iota(jnp.int32, (H, 128), 1)
        place = jnp.where(r == c, 1.0, 0.0)
        df = _dot(dfk_ref[...] + dfq_ref[...], place, TN, HIGHEST)
        run = _scan_rows(df, ts, ts, reverse=True) + carry[0:1, :]
        carry[0:1, :] = run[0:1, :]
        daf = run * _sigmoid(-(zs_ref[...] + b_ref[...]))
        daf_ref[...] = daf
        db_ref[0:1, :] += jnp.sum(daf, axis=0, keepdims=True)

    rowsin = lambda a: (a, (H, ts), lambda i: (0, i))
    daf, db = _tiled("fox_decay_bwd", body, S // ts,
                     [rowsin(dfk_rows), rowsin(dfq_rows), _cols(z, ts, 128, CB_SMALL), _full(b_fg128)],
                     [_orow(S, (128,), F32, ts), _oacc((8, 128), F32)],
                     scratch=[pltpu.VMEM((8, 128), F32)], reverse=True)
    return daf, db[0]


FOX_AUG = 80


def _fox_fwd(q_aug, kT_aug, v_aug, tq):
    H, S, da = q_aug.shape
    dv = v_aug.shape[2]
    d = FOX_DIM
    tk = tq // 2
    qscale = (d ** -0.5) * LOG2E

    def kern(q_ref, kT_ref, v_ref, o_ref, lse_ref, qs_ref, s_buf, p_buf, m_scr, acc_scr):
        i = pl.program_id(1)
        col = lax.broadcasted_iota(jnp.int32, (1, da), 1)
        qb = _b16(q_ref[...] * jnp.where(col < d, qscale, 1.0))
        qs_ref[...] = qb

        def keys(t):
            return pl.ds(pl.multiple_of(t * tk, tk), tk)

        def stage(t, slot, mask_off, look_ahead):
            if look_ahead:
                s_buf[1 - slot] = _dot(qb, kT_ref[:, keys(t + 1)])
            pv = _dot(p_buf[1 - slot], v_ref[keys(jnp.maximum(t - 1, 0)), :])

            def scores():
                s = s_buf[slot]
                if mask_off is None:
                    return s
                r = lax.broadcasted_iota(jnp.int32, (tq, tk), 0)
                c = lax.broadcasted_iota(jnp.int32, (tq, tk), 1)
                return jnp.where(c + mask_off <= r, s, NEG)

            m = m_scr[...]
            m_new = jnp.maximum(m, jnp.max(scores(), axis=1, keepdims=True))
            alpha = jnp.exp2(m - m_new)
            p_buf[slot] = _b16(jnp.exp2(scores() - m_new))
            m_scr[...] = m_new
            acc_scr[...] = (acc_scr[...] + pv) * alpha

        s_buf[0] = _dot(qb, kT_ref[:, keys(0)])
        p_buf[1] = jnp.zeros((tq, tk), BF16)
        m_scr[...] = jnp.full((tq, 1), NEG, F32)
        acc_scr[...] = jnp.zeros((tq, dv), F32)

        def pair(n):
            stage(2 * n, 0, None, True)
            stage(2 * n + 1, 1, None, True)

        def quad(m, _):
            pair(2 * m)
            pair(2 * m + 1)
            return 0

        lax.fori_loop(0, i // 2, quad, 0)

        @pl.when(i % 2 == 1)
        def _():
            pair(i - 1)

        stage(2 * i, 0, 0, True)
        stage(2 * i + 1, 1, tk, False)
        acc = acc_scr[...] + _dot(p_buf[1], v_ref[keys(2 * i + 1), :])
        l = acc[:, d:d + 1]
        o_ref[...] = acc[:, :d] / l
        lse_ref[...] = _col_to_row(m_scr[...] + jnp.log(l) * LOG2E)

    return pl.pallas_call(
        kern, name="fox_fwd", grid=(H, S // tq),
        in_specs=[pl.BlockSpec((None, tq, da), lambda h, i: (h, i, 0)),
                  pl.BlockSpec((None, da, S), lambda h, i: (h, 0, 0)),
                  pl.BlockSpec((None, S, dv), lambda h, i: (h, 0, 0))],
        out_specs=[pl.BlockSpec((None, tq, d), lambda h, i: (h, i, 0)),
                   pl.BlockSpec((None, 1, tq), lambda h, i: (h, 0, i)),
                   pl.BlockSpec((None, tq, da), lambda h, i: (h, i, 0))],
        out_shape=[jax.ShapeDtypeStruct((H, S, d), F32), jax.ShapeDtypeStruct((H, 1, S), F32),
                   jax.ShapeDtypeStruct((H, S, da), BF16)],
        scratch_shapes=[pltpu.VMEM((2, tq, tk), F32), pltpu.VMEM((2, tq, tk), BF16),
                        pltpu.VMEM((tq, 1), F32), pltpu.VMEM((tq, dv), F32)],
        compiler_params=_params(("parallel", "arbitrary")),
    )(q_aug, kT_aug, v_aug)


def _fox_bwd(qs, k_aug, kT, v, do, lse_row, delta_row, tq):
    H, S, da = qs.shape
    d = FOX_DIM
    tk = tq
    nq = S // tq
    scale = d ** -0.5

    ts2 = tq // 2
    last = 2 * nq - 1

    def kern(q_ref, k_ref, kT_ref, v_ref, do_ref, lse_ref, dl_ref,
             dqT_ref, dk_ref, dv_ref, dfk_ref, dfq_ref,
             kq_buf, dp_buf, pb_buf, ds_buf, dk_scr, dv_scr, dfk_scr):
        j = pl.program_id(1)

        @pl.when(j == 0)
        def _():
            dqT_ref[...] = jnp.zeros_like(dqT_ref)
            dfq_ref[...] = jnp.zeros_like(dfq_ref)

        kb = k_ref[...]
        kTb = kT_ref[...]
        vb = v_ref[...]
        dk_scr[...] = jnp.zeros_like(dk_scr)
        dv_scr[...] = jnp.zeros_like(dv_scr)
        dfk_scr[...] = jnp.zeros_like(dfk_scr)

        def queries(t):
            return pl.ds(pl.multiple_of(t * ts2, ts2), ts2)

        def products(t, slot):
            rows = queries(t)
            kq_buf[slot] = _dot(kb, q_ref[rows, :], NT)
            dp_buf[slot] = _dot(vb, do_ref[rows, :], NT)

        def pointwise(t, slot, mask_off):
            rows = queries(t)
            sT = kq_buf[slot]
            if mask_off is not None:
                r = lax.broadcasted_iota(jnp.int32, (tk, ts2), 0)
                c = lax.broadcasted_iota(jnp.int32, (tk, ts2), 1)
                sT = jnp.where(r <= c + mask_off, sT, NEG)
            pT = jnp.exp2(sT - lse_ref[:, rows])
            dsT = pT * (dp_buf[slot] - dl_ref[:, rows])
            pb_buf[slot] = _b16(pT)
            ds_buf[slot] = _b16(dsT)
            dfk_scr[...] -= jnp.sum(dsT, axis=1, keepdims=True)
            dfq_ref[:, rows] += jnp.sum(dsT, axis=0, keepdims=True)

        def accumulate(t, slot):
            rows = queries(t)
            dsb = ds_buf[slot]
            dv_scr[...] += _dot(pb_buf[slot], do_ref[rows, :])
            dk_scr[...] += _dot(dsb, q_ref[rows, :])
            dqT_ref[:, rows] += _dot(kTb, dsb) * scale

        def stage(t, slot, mask_off, has_prev):
            products(jnp.minimum(t + 1, last), 1 - slot)
            if has_prev:
                accumulate(t - 1, 1 - slot)
            pointwise(t, slot, mask_off)

        products(2 * j, 0)
        stage(2 * j, 0, 0, False)
        stage(2 * j + 1, 1, ts2, True)

        def pair(n):
            stage(2 * n, 0, None, True)
            stage(2 * n + 1, 1, None, True)

        def quad(m, _):
            pair(j + 1 + 2 * m)
            pair(j + 2 + 2 * m)
            return 0

        n_rest = nq - 1 - j
        lax.fori_loop(0, n_rest // 2, quad, 0)

        @pl.when(n_rest % 2 == 1)
        def _():
            pair(nq - 1)

        accumulate(last, 1)
        dk_ref[...] = dk_scr[:, :d] * LN2
        dv_ref[...] = dv_scr[...]
        dfk_ref[...] = _col_to_row(dfk_scr[...])

    tile = lambda h, j: (h, j, 0)
    whole = lambda h, j: (h, 0, 0)
    rowtile = lambda h, j: (h, 0, j)
    return pl.pallas_call(
        kern, name="fox_bwd", grid=(H, S // tk),
        in_specs=[pl.BlockSpec((None, S, da), whole),
                  pl.BlockSpec((None, tk, da), tile),
                  pl.BlockSpec((None, d, tk), lambda h, j: (h, 0, j)),
                  pl.BlockSpec((None, tk, d), tile),
                  pl.BlockSpec((None, S, d), whole),
                  pl.BlockSpec((None, 1, S), whole),
                  pl.BlockSpec((None, 1, S), whole)],
        out_specs=[pl.BlockSpec((None, d, S), whole),
                   pl.BlockSpec((None, tk, d), tile),
                   pl.BlockSpec((None, tk, d), tile),
                   pl.BlockSpec((None, 1, tk), rowtile),
                   pl.BlockSpec((None, 1, S), whole)],
        out_shape=[jax.ShapeDtypeStruct((H, d, S), F32), jax.ShapeDtypeStruct((H, S, d), F32),
                   jax.ShapeDtypeStruct((H, S, d), F32), jax.ShapeDtypeStruct((H, 1, S), F32),
                   jax.ShapeDtypeStruct((H, 1, S), F32)],
        scratch_shapes=[pltpu.VMEM((2, tk, ts2), F32), pltpu.VMEM((2, tk, ts2), F32),
                        pltpu.VMEM((2, tk, ts2), BF16), pltpu.VMEM((2, tk, ts2), BF16),
                        pltpu.VMEM((tk, da), F32), pltpu.VMEM((tk, d), F32), pltpu.VMEM((tk, 1), F32)],
        compiler_params=_params(("parallel", "arbitrary")),
    )(qs, k_aug, kT, v, do, lse_row, delta_row)


def _heads_major(a, H, d):
    S = a.shape[0]
    return a.reshape(S, H, d).transpose(1, 0, 2)


def _heads_minor(a):
    H, S, d = a.shape
    return a.transpose(1, 0, 2).reshape(S, H * d)


def _lane_pick(x128, lane):
    return x128[:, lane:lane + 1]


def _l2_fwd(y):
    return lax.rsqrt(jnp.sum(y * y, axis=1, keepdims=True) + EPS)


def _gdn_prep(z, conv_w, a128, dt128, ts):
    S = z.shape[0]
    C3 = 3 * WIDTH
    hb = ts // 8

    def body(t, first, ins, outs, scratch):
        x_ref, halo_ref, zs_ref, w_ref, a_ref, dt_ref = ins
        qkv_ref, c_ref, gb_ref, gbT_ref = outs
        halo = jnp.where(t > 0, halo_ref[...], 0.0)
        xe = jnp.concatenate([halo, x_ref[...]], axis=0)
        w = w_ref[...]
        c = w[3:4, :] * xe[8:, :]
        for back in (1, 2, 3):
            c = c + w[3 - back:4 - back, :] * pltpu.roll(xe, back, 0)[8:, :]
        c_ref[...] = c
        y = c * _sigmoid(c)
        for h in range(GDN_HEADS):
            lo = h * GDN_DIM
            yq = y[:, lo:lo + GDN_DIM]
            qkv_ref[:, lo:lo + GDN_DIM] = yq * (_l2_fwd(yq) * (GDN_DIM ** -0.5))
            yk = y[:, WIDTH + lo:WIDTH + lo + GDN_DIM]
            qkv_ref[:, WIDTH + lo:WIDTH + lo + GDN_DIM] = yk * _l2_fwd(yk)
        qkv_ref[:, 2 * WIDTH:] = y[:, 2 * WIDTH:]
        zs = zs_ref[...]
        lane = lax.broadcasted_iota(jnp.int32, zs.shape, 1)
        g = -jnp.exp(a_ref[...]) * _softplus(zs + dt_ref[...])
        G = _scan_rows(g, ts, CHUNK)
        beta = _sigmoid(zs)
        out = jnp.where(lane < 8, pltpu.roll(g, 128 - LANE_BA, 1), jnp.where(lane < LANE_BB, G, beta))
        gb_ref[...] = out
        gbT_ref[...] = _transpose_exact(out)

    x_in = (z, (ts, C3), lambda i: (i, CB_BQKV))
    halo_in = (z, (8, C3), lambda i: (jnp.maximum(i * hb - 1, 0), CB_BQKV))
    return _tiled("gdn_prep", body, S // ts,
                  [x_in, halo_in, _cols(z, ts, 128, CB_SMALL), _full(conv_w), _full(a128), _full(dt128)],
                  [_orow(S, (C3,), F32, ts), _orow(S, (C3,), F32, ts), _orow(S, (128,), F32, ts),
                   ((128, S), F32, (128, ts), lambda i: (0, i))])


def _chunk_masks(nc):
    r = lax.broadcasted_iota(jnp.int32, (nc, CHUNK, CHUNK), 1)
    c = lax.broadcasted_iota(jnp.int32, (nc, CHUNK, CHUNK), 2)
    return c <= r, c < r, c == r


def _chunk_local(qh, kh, vh, Gc, Gr, beta):
    nc = qh.shape[0]
    incl, strict, _ = _chunk_masks(nc)
    gamma = jnp.exp(jnp.where(incl, Gc - Gr, NEG))
    kb = kh * beta
    P = _bdot(_b16(kb), _b16(kh), 2, 2)
    Qk = _bdot(_b16(qh), _b16(kh), 2, 2)
    eG = jnp.exp(Gc)
    Gl = Gc[:, CHUNK - 1:CHUNK, :]
    edec = jnp.exp(Gl - Gc)
    return incl, strict, gamma, kb, P, Qk, eG, edec


def _gdn_local_fwd(qkv, gb, grow, ts):
    S = qkv.shape[0]
    nc = ts // CHUNK

    def body(t, first, ins, outs, scratch):
        q_ref, k_ref, v_ref, gb_ref, gr_ref = ins
        u_ref, w_ref, qd_ref, kd_ref, aqk_ref, T_ref = outs
        gbv = gb_ref[...]
        for h in range(GDN_HEADS):
            lo = h * GDN_DIM
            qh = q_ref[:, lo:lo + GDN_DIM].reshape(nc, CHUNK, GDN_DIM)
            kh = k_ref[:, lo:lo + GDN_DIM].reshape(nc, CHUNK, GDN_DIM)
            vh = v_ref[:, lo:lo + GDN_DIM].reshape(nc, CHUNK, GDN_DIM)
            Gc = _lane_pick(gbv, LANE_BA + h).reshape(nc, CHUNK, 1)
            beta = _lane_pick(gbv, LANE_BB + h).reshape(nc, CHUNK, 1)
            Gr = gr_ref[h].reshape(nc, 1, CHUNK)
            incl, strict, gamma, kb, P, Qk, eG, edec = _chunk_local(qh, kh, vh, Gc, Gr, beta)
            A = jnp.where(strict, P * gamma, 0.0)
            _, _, eye = _chunk_masks(nc)
            T = jnp.where(eye, 1.0, 0.0) - A
            X = A
            for _ in range(5):
                X = _bdot(X, X, 2, 1, PREC_UT)
                T = T + _bdot(T, X, 2, 1, PREC_UT)
            u = _bdot(T, vh * beta, 2, 1, PREC_UT)
            w = _bdot(T, kb * eG, 2, 1, PREC_UT)
            u_ref[:, lo:lo + GDN_DIM] = u.reshape(ts, GDN_DIM)
            w_ref[:, lo:lo + GDN_DIM] = w.reshape(ts, GDN_DIM)
            qd_ref[:, lo:lo + GDN_DIM] = (qh * eG).reshape(ts, GDN_DIM)
            kd_ref[:, lo:lo + GDN_DIM] = (kh * edec).reshape(ts, GDN_DIM)
            aqk_ref[h] = jnp.where(incl, Qk * gamma, 0.0).reshape(ts, CHUNK)
            T_ref[h] = T.reshape(ts, CHUNK)

    wide = _orow(S, (WIDTH,), F32, ts)
    perhead = ((GDN_HEADS, S, CHUNK), F32, (GDN_HEADS, ts, CHUNK), lambda i: (0, i, 0))
    return _tiled("gdn_local_fwd", body, S // ts,
                  [_cols(qkv, ts, WIDTH, 0), _cols(qkv, ts, WIDTH, 1), _cols(qkv, ts, WIDTH, 2),
                   _rows(gb, ts), (grow, (GDN_HEADS, nc, CHUNK), lambda i: (0, i, 0))],
                  [wide, wide, wide, wide, perhead, perhead])


def _gdn_scan_fwd(u, w, qd, kd, aqk, gb, ts):
    S = u.shape[0]
    nc = ts // CHUNK
    N = S // CHUNK

    def body(t, first, ins, outs, scratch):
        u_ref, w_ref, qd_ref, kd_ref, aqk_ref, gb_ref = ins
        o_ref, vn_ref, st_ref = outs
        (state,) = scratch

        @pl.when(first)
        def _():
            state[...] = jnp.zeros_like(state)

        def chunk(c, _):
            r0 = pl.multiple_of(c * CHUNK, CHUNK)
            rows = pl.ds(r0, CHUNK)
            glast = gb_ref[pl.ds(r0 + CHUNK - 1, 1), :]
            heads = range(GDN_HEADS)
            cols = [slice(h * GDN_DIM, (h + 1) * GDN_DIM) for h in heads]
            S_old = [state[h] for h in heads]
            u_h = [u_ref[rows, cols[h]] for h in heads]
            w_h = [_b16(w_ref[rows, cols[h]]) for h in heads]
            qd_h = [_b16(qd_ref[rows, cols[h]]) for h in heads]
            kd_h = [_b16(kd_ref[rows, cols[h]]) for h in heads]
            aqk_h = [_b16(aqk_ref[h, rows, :]) for h in heads]
            S_new, o_h, vn_h = [], [], []
            for h in heads:
                Sb = _b16(S_old[h])
                vn = u_h[h] - _dot(w_h[h], Sb)
                vnb = _b16(vn)
                o_h.append(_dot(qd_h[h], Sb) + _dot(aqk_h[h], vnb))
                egl = jnp.exp(glast[:, LANE_BA + h:LANE_BA + h + 1])
                S_new.append(S_old[h] * egl + _dot(kd_h[h], vnb, TN))
                vn_h.append(vn)
            for h in heads:
                st_ref[c, h] = S_old[h]
                state[h] = S_new[h]
                o_ref[rows, cols[h]] = o_h[h]
                vn_ref[rows, cols[h]] = vn_h[h]
            return 0

        lax.fori_loop(0, nc, chunk, 0)

    wide_in = lambda a: _rows(a, ts)
    wide = _orow(S, (WIDTH,), F32, ts)
    states = ((N, GDN_HEADS, GDN_DIM, GDN_DIM), F32, (nc, GDN_HEADS, GDN_DIM, GDN_DIM),
              lambda i: (i, 0, 0, 0))
    return _tiled("gdn_scan_fwd", body, S // ts,
                  [wide_in(u), wide_in(w), wide_in(qd), wide_in(kd),
                   (aqk, (GDN_HEADS, ts, CHUNK), lambda i: (0, i, 0)), _rows(gb, ts)],
                  [wide, wide, states],
                  scratch=[pltpu.VMEM((GDN_HEADS, GDN_DIM, GDN_DIM), F32)])


def _gdn_scan_bwd(do, w, qd, kd, aqk, vn, states, gb, ts):
    S = do.shape[0]
    nc = ts // CHUNK
    N = S // CHUNK

    def body(t, first, ins, outs, scratch):
        do_ref, w_ref, qd_ref, kd_ref, aqk_ref, vn_ref, st_ref, gb_ref = ins
        du_ref, dw_ref, dqd_ref, dkd_ref, daqk_ref, dgl_ref = outs
        (dstate,) = scratch

        @pl.when(first)
        def _():
            dstate[...] = jnp.zeros_like(dstate)

        r = lax.broadcasted_iota(jnp.int32, (CHUNK, CHUNK), 0)
        cc = lax.broadcasted_iota(jnp.int32, (CHUNK, CHUNK), 1)
        incl = cc <= r
        lane = lax.broadcasted_iota(jnp.int32, (1, 128), 1)

        def chunk(k, _):
            c = nc - 1 - k
            r0 = pl.multiple_of(c * CHUNK, CHUNK)
            rows = pl.ds(r0, CHUNK)
            glast = gb_ref[pl.ds(r0 + CHUNK - 1, 1), :]
            dgl_row = jnp.zeros((1, 128), F32)
            heads = range(GDN_HEADS)
            cols = [slice(h * GDN_DIM, (h + 1) * GDN_DIM) for h in heads]
            S_h = [st_ref[c, h] for h in heads]
            dS_h = [dstate[h] for h in heads]
            do_h = [_b16(do_ref[rows, cols[h]]) for h in heads]
            aqk_h = [_b16(aqk_ref[h, rows, :]) for h in heads]
            vn_h = [_b16(vn_ref[rows, cols[h]]) for h in heads]
            kd_h = [_b16(kd_ref[rows, cols[h]]) for h in heads]
            qd_h = [_b16(qd_ref[rows, cols[h]]) for h in heads]
            w_h = [_b16(w_ref[rows, cols[h]]) for h in heads]
            res = []
            for h in heads:
                Sb, dSb, dob, vnb = _b16(S_h[h]), _b16(dS_h[h]), do_h[h], vn_h[h]
                dvn = _dot(aqk_h[h], dob, TN) + _dot(kd_h[h], dSb)
                dvnb = _b16(dvn)
                daqk = jnp.where(incl, _dot(dob, vnb, NT), 0.0)
                dqd = _dot(dob, Sb, NT)
                dkd = _dot(vnb, dSb, NT)
                dw = -_dot(dvnb, Sb, NT)
                egl = jnp.exp(glast[:, LANE_BA + h:LANE_BA + h + 1])
                dgl = egl * jnp.sum(jnp.sum(dS_h[h] * S_h[h], axis=1, keepdims=True), axis=0,
                                    keepdims=True)
                dgl_row = jnp.where(lane == h, dgl, dgl_row)
                dS_new = _dot(qd_h[h], dob, TN) + egl * dS_h[h] - _dot(w_h[h], dvnb, TN)
                res.append((daqk, dqd, dkd, dw, dvn, dS_new))
            for h in heads:
                daqk, dqd, dkd, dw, dvn, dS_new = res[h]
                daqk_ref[h, rows, :] = daqk
                dqd_ref[rows, cols[h]] = dqd
                dkd_ref[rows, cols[h]] = dkd
                dw_ref[rows, cols[h]] = dw
                du_ref[rows, cols[h]] = dvn
                dstate[h] = dS_new
            dgl_ref[pl.ds(c, 1), :] = dgl_row
            return 0

        lax.fori_loop(0, nc, chunk, 0)

    wide_in = lambda a: _rows(a, ts)
    wide = _orow(S, (WIDTH,), F32, ts)
    perhead_in = lambda a: (a, (GDN_HEADS, ts, CHUNK), lambda i: (0, i, 0))
    perhead = ((GDN_HEADS, S, CHUNK), F32, (GDN_HEADS, ts, CHUNK), lambda i: (0, i, 0))
    return _tiled("gdn_scan_bwd", body, S // ts,
                  [wide_in(do), wide_in(w), wide_in(qd), wide_in(kd), perhead_in(aqk), wide_in(vn),
                   (states, (nc, GDN_HEADS, GDN_DIM, GDN_DIM), lambda i: (i, 0, 0, 0)), _rows(gb, ts)],
                  [wide, wide, wide, wide, perhead, ((N, 128), F32, (nc, 128), lambda i: (i, 0))],
                  scratch=[pltpu.VMEM((GDN_HEADS, GDN_DIM, GDN_DIM), F32)], reverse=True)


def _gdn_local_bwd(qkv, gb, grow, T, du, dw, dqd, dkd, daqk, dgl, ts):
    S = qkv.shape[0]
    nc = ts // CHUNK

    def body(t, first, ins, outs, scratch):
        (q_ref, k_ref, v_ref, gb_ref, gr_ref, T_ref, du_ref, dw_ref, dqd_ref, dkd_ref,
         daqk_ref, dgl_ref) = ins
        dqkv_ref, dgb_ref = outs
        gbv = gb_ref[...]
        dglv = dgl_ref[...]
        lane = lax.broadcasted_iota(jnp.int32, (ts, 128), 1)
        dG_all = jnp.zeros((ts, 128), F32)
        dbeta_all = jnp.zeros((ts, 128), F32)
        for h in range(GDN_HEADS):
            lo = h * GDN_DIM
            cols = slice(lo, lo + GDN_DIM)
            r3 = lambda ref: ref[:, cols].reshape(nc, CHUNK, GDN_DIM)
            qh, kh, vh = r3(q_ref), r3(k_ref), r3(v_ref)
            duh, dwh, dqdh, dkdh = r3(du_ref), r3(dw_ref), r3(dqd_ref), r3(dkd_ref)
            Gc = _lane_pick(gbv, LANE_BA + h).reshape(nc, CHUNK, 1)
            beta = _lane_pick(gbv, LANE_BB + h).reshape(nc, CHUNK, 1)
            Gr = gr_ref[h].reshape(nc, 1, CHUNK)
            Th = T_ref[h].reshape(nc, CHUNK, CHUNK)
            daq = daqk_ref[h].reshape(nc, CHUNK, CHUNK)
            incl, strict, gamma, kb, P, Qk, eG, edec = _chunk_local(qh, kh, vh, Gc, Gr, beta)
            _, _, eye = _chunk_masks(nc)
            vb = vh * beta
            kbg = kb * eG
            dvb = _bdot(Th, duh, 1, 1, PREC_UT)
            dkbg = _bdot(Th, dwh, 1, 1, PREC_UT)
            dT = _bdot(duh, vb, 2, 2, PREC_UT) + _bdot(dwh, kbg, 2, 2, PREC_UT)
            M1 = _bdot(Th, dT, 1, 1, PREC_UT)
            dA = jnp.where(strict, -_bdot(M1, Th, 2, 2, PREC_UT), 0.0)
            dP = dA * gamma
            dQ = daq * gamma
            dgam = (dA * P + daq * Qk) * gamma
            dPb, dQb = _b16(dP), _b16(dQ)
            khb, qhb, kbb = _b16(kh), _b16(qh), _b16(kb)
            dq = _bdot(dQb, khb, 2, 1) + dqdh * eG
            dkb = _bdot(dPb, khb, 2, 1) + dkbg * eG
            dk = (_bdot(dQb, qhb, 1, 1) + _bdot(dPb, kbb, 1, 1) + dkdh * edec + dkb * beta)
            dbeta = (jnp.sum(dkb * kh, axis=2, keepdims=True) + jnp.sum(dvb * vh, axis=2, keepdims=True))
            dv = dvb * beta
            col_as_col = jnp.sum(jnp.where(eye, jnp.sum(dgam, axis=1, keepdims=True), 0.0),
                                 axis=2, keepdims=True)
            kd_term = jnp.sum(dkdh * kh * edec, axis=2, keepdims=True)
            dG = (jnp.sum(dgam, axis=2, keepdims=True) - col_as_col
                  + jnp.sum(dqdh * qh * eG, axis=2, keepdims=True)
                  + jnp.sum(dkbg * kbg, axis=2, keepdims=True) - kd_term)
            dgl_h = dglv[:, h:h + 1].reshape(nc, 1, 1) + jnp.sum(kd_term, axis=1, keepdims=True)
            last = lax.broadcasted_iota(jnp.int32, (nc, CHUNK, 1), 1) == CHUNK - 1
            dG = dG + jnp.where(last, dgl_h, 0.0)
            dqkv_ref[:, cols] = dq.reshape(ts, GDN_DIM)
            dqkv_ref[:, WIDTH + lo:WIDTH + lo + GDN_DIM] = dk.reshape(ts, GDN_DIM)
            dqkv_ref[:, 2 * WIDTH + lo:2 * WIDTH + lo + GDN_DIM] = dv.reshape(ts, GDN_DIM)
            dG_all = jnp.where(lane == LANE_BA + h, dG.reshape(ts, 1), dG_all)
            dbeta_all = jnp.where(lane == LANE_BB + h, dbeta.reshape(ts, 1), dbeta_all)
        dg_all = _scan_rows(dG_all, ts, CHUNK, reverse=True)
        dgb_ref[...] = jnp.where(lane < LANE_BB, dg_all, dbeta_all)

    wide_in = lambda a: _rows(a, ts)
    perhead_in = lambda a: (a, (GDN_HEADS, ts, CHUNK), lambda i: (0, i, 0))
    return _tiled("gdn_local_bwd", body, S // ts,
                  [_cols(qkv, ts, WIDTH, 0), _cols(qkv, ts, WIDTH, 1), _cols(qkv, ts, WIDTH, 2),
                   _rows(gb, ts), (grow, (GDN_HEADS, nc, CHUNK), lambda i: (0, i, 0)), perhead_in(T),
                   wide_in(du), wide_in(dw), wide_in(dqd), wide_in(dkd), perhead_in(daqk),
                   (dgl, (nc, 128), lambda i: (i, 0))],
                  [_orow(S, (3 * WIDTH,), F32, ts), _orow(S, (128,), F32, ts)])


def _gdn_prep_bwd(dqkv, dgb, cpre, z, conv_w, a128, dt128, dz, ts):
    S = z.shape[0]
    C3 = 3 * WIDTH
    hb = ts // 8
    n_tiles = S // ts

    def dpre(dq, c):
        y, dsil = _silu_and_grad(c)
        parts = []
        for h in range(GDN_HEADS):
            lo = h * GDN_DIM
            yq = y[:, lo:lo + GDN_DIM]
            rq = _l2_fwd(yq)
            nq = yq * rq
            dn = dq[:, lo:lo + GDN_DIM] * (GDN_DIM ** -0.5)
            parts.append(rq * (dn - nq * jnp.sum(dn * nq, axis=1, keepdims=True)))
        for h in range(GDN_HEADS):
            lo = WIDTH + h * GDN_DIM
            yk = y[:, lo:lo + GDN_DIM]
            rk = _l2_fwd(yk)
            nk = yk * rk
            dn = dq[:, lo:lo + GDN_DIM]
            parts.append(rk * (dn - nk * jnp.sum(dn * nk, axis=1, keepdims=True)))
        parts.append(dq[:, 2 * WIDTH:])
        return jnp.concatenate(parts, axis=1) * dsil

    def body(t, first, ins, outs, scratch):
        (dq_ref, dqn_ref, c_ref, cn_ref, x_ref, xp_ref, zs_ref, dgb_ref, w_ref, a_ref, dt_ref) = ins
        dx_ref, dzs_ref, dw_ref, dad_ref = outs

        @pl.when(first)
        def _():
            dw_ref[...] = jnp.zeros_like(dw_ref)
            dad_ref[...] = jnp.zeros_like(dad_ref)

        dc = dpre(dq_ref[...], c_ref[...])
        dcn = jnp.where(t < n_tiles - 1, dpre(dqn_ref[...], cn_ref[...]), 0.0)
        dce = jnp.concatenate([dc, dcn], axis=0)
        w = w_ref[...]
        dx = w[3:4, :] * dc
        for back in (1, 2, 3):
            dx = dx + w[3 - back:4 - back, :] * pltpu.roll(dce, ts + 8 - back, 0)[:ts, :]
        dx_ref[...] = _b16(dx)
        halo = jnp.where(t > 0, xp_ref[...], 0.0)
        xe = jnp.concatenate([halo, x_ref[...]], axis=0)
        dw_ref[3:4, :] += jnp.sum(dc * xe[8:, :], axis=0, keepdims=True)
        for back in (1, 2, 3):
            dw_ref[3 - back:4 - back, :] += jnp.sum(dc * pltpu.roll(xe, back, 0)[8:, :], axis=0,
                                                     keepdims=True)
        zs = zs_ref[...]
        dgb = dgb_ref[...]
        lane = lax.broadcasted_iota(jnp.int32, zs.shape, 1)
        arg = zs + dt_ref[...]
        nega = -jnp.exp(a_ref[...])
        dba = dgb * nega * _sigmoid(arg)
        beta = _sigmoid(zs)
        dbb = dgb * beta * (1.0 - beta)
        dzs_ref[...] = jnp.where((lane >= LANE_BA) & (lane < LANE_BB), dba,
                                 jnp.where((lane >= LANE_BB) & (lane < LANE_BB + 4), dbb, 0.0))
        dad_ref[0:1, :] += jnp.sum(dgb * nega * _softplus(arg), axis=0, keepdims=True)
        dad_ref[1:2, :] += jnp.sum(dba, axis=0, keepdims=True)

    nxt = lambda i: (jnp.minimum((i + 1) * hb, S // 8 - 1), 0)
    prv = lambda i: (jnp.maximum(i * hb - 1, 0), CB_BQKV)
    return _tiled("gdn_prep_bwd", body, n_tiles,
                  [_rows(dqkv, ts), (dqkv, (8, C3), nxt), _rows(cpre, ts), (cpre, (8, C3), nxt),
                   (z, (ts, C3), lambda i: (i, CB_BQKV)), (z, (8, C3), prv),
                   _cols(z, ts, 128, CB_SMALL), _rows(dgb, ts), _full(conv_w), _full(a128), _full(dt128)],
                  [((S, N_AL), BF16, (ts, C3), lambda i: (i, CB_BQKV)), _orow(S, (128,), F32, ts),
                   _oacc((8, C3), F32), _oacc((8, 128), F32)],
                  fill=(dz, 0))


def _mem_attn_fwd(z, mk, mv, ts):
    S = z.shape[0]

    def body(t, first, ins, outs, scratch):
        q_ref, mk_ref, mv_ref = ins
        (o_ref,) = outs
        for h in range(MEM_HEADS):
            cols = slice(h * MEM_DIM, (h + 1) * MEM_DIM)
            s = _dot(_b16(q_ref[:, cols]), _b16(mk_ref[:, cols]), NT) * (MEM_DIM ** -0.5)
            m = jnp.max(s, axis=1, keepdims=True)
            p = jnp.exp(s - m)
            p = p / jnp.sum(p, axis=1, keepdims=True)
            o_ref[:, cols] = _dot(_b16(p), _b16(mv_ref[:, cols]))

    (o,) = _tiled("mem_attn_fwd", body, S // ts, [_cols(z, ts, WIDTH, CB_MQ), _full(mk), _full(mv)],
                  [_orow(S, (WIDTH,), F32, ts)])
    return o


def _mem_attn_bwd(do, z, mk, mv, dz, ts):
    S = z.shape[0]
    M = mk.shape[0]

    def body(t, first, ins, outs, scratch):
        do_ref, q_ref, mk_ref, mv_ref = ins
        dq_ref, dmk_ref, dmv_ref = outs

        @pl.when(first)
        def _():
            dmk_ref[...] = jnp.zeros_like(dmk_ref)
            dmv_ref[...] = jnp.zeros_like(dmv_ref)

        scale = MEM_DIM ** -0.5
        for h in range(MEM_HEADS):
            cols = slice(h * MEM_DIM, (h + 1) * MEM_DIM)
            qb = _b16(q_ref[:, cols])
            kb = _b16(mk_ref[:, cols])
            dob = _b16(do_ref[:, cols])
            s = _dot(qb, kb, NT) * scale
            m = jnp.max(s, axis=1, keepdims=True)
            p = jnp.exp(s - m)
            p = p / jnp.sum(p, axis=1, keepdims=True)
            dmv_ref[:, cols] += _dot(_b16(p), dob, TN)
            dp = _dot(dob, _b16(mv_ref[:, cols]), NT)
            ds = p * (dp - jnp.sum(dp * p, axis=1, keepdims=True)) * scale
            dsb = _b16(ds)
            dq_ref[:, cols] = _b16(_dot(dsb, kb))
            dmk_ref[:, cols] += _dot(dsb, qb, TN)

    return _tiled("mem_attn_bwd", body, S // ts,
                  [_rows(do, ts), _cols(z, ts, WIDTH, CB_MQ), _full(mk), _full(mv)],
                  [((S, N_AL), BF16, (ts, WIDTH), lambda i: (i, CB_MQ)), _oacc((M, WIDTH), F32),
                   _oacc((M, WIDTH), F32)],
                  fill=(dz, 0))


def _head_norm(ob, g):
    xs, rs = [], []
    for h in range(GDN_HEADS):
        o = ob[:, h * GDN_DIM:(h + 1) * GDN_DIM]
        r = lax.rsqrt(jnp.mean(o * o, axis=1, keepdims=True) + EPS)
        xs.append(o * r)
        rs.append(r)
    return xs, rs


def _merge_fwd(x, z, o_a, o_b, o_m, gdn_g, b_merge, wb, wout, ts):
    S, D = x.shape

    def body(t, first, ins, outs, scratch):
        (x_ref, g_ref, oa_ref, az_ref, ob_ref, bz_ref, om_ref, mz_ref, gg_ref, bm_ref, wb_ref,
         wo_ref) = ins
        xo_ref, ya_ref, yb_ref, ym_ref, mg_ref = outs
        ya = oa_ref[...] * _silu_and_grad(az_ref[...])[0]
        xs, _ = _head_norm(ob_ref[...], None)
        nb = jnp.concatenate([xh * gg_ref[...] for xh in xs], axis=1)
        yb = nb * _silu_and_grad(bz_ref[...])[0]
        ym = om_ref[...] * _silu_and_grad(mz_ref[...])[0]
        merged = jnp.zeros((ts, D), F32)
        for n, (y, y_ref) in enumerate(((ya, ya_ref), (yb, yb_ref), (ym, ym_ref))):
            yb16 = _b16(y)
            y_ref[...] = yb16
            gate = _sigmoid(g_ref[:, n * D:(n + 1) * D] + bm_ref[:, n * D:(n + 1) * D])
            merged = merged + gate * _dot(yb16, wb_ref[n])
        mb = _b16(merged)
        mg_ref[...] = mb
        xo_ref[...] = x_ref[...] + _dot(mb, wo_ref[...])

    half = lambda a: _rows(a, ts)
    return _tiled("merge_fwd", body, S // ts,
                  [_rows(x, ts), _cols(z, ts, 3 * D, CB_GATES), half(o_a), _cols(z, ts, WIDTH, CB_AZ),
                   half(o_b), _cols(z, ts, WIDTH, CB_BZ), half(o_m), _cols(z, ts, WIDTH, CB_MZ),
                   _full(gdn_g.reshape(1, GDN_DIM)), _full(b_merge.reshape(1, 3 * D)), _full(wb), _full(wout)],
                  [_orow(S, (D,), F32, ts), _orow(S, (WIDTH,), BF16, ts), _orow(S, (WIDTH,), BF16, ts),
                   _orow(S, (WIDTH,), BF16, ts), _orow(S, (D,), BF16, ts)])


def _merge_bwd(dout, z, o_a, o_b, o_m, ya, yb, ym, gdn_g, b_merge, wb, wout, hsum, ts):
    S, D = dout.shape

    def body(t, first, ins, outs, scratch):
        (do_ref, g_ref, oa_ref, az_ref, ob_ref, bz_ref, om_ref, mz_ref, ya_ref, yb_ref, ym_ref,
         gg_ref, bm_ref, wb_ref, wo_ref, hs_ref) = ins
        (dg_ref, dpa_ref, dpb_ref, dpm_ref, doa_ref, dob_ref, dom_ref, dl_ref, dbm_ref, dgg_ref) = outs
        G3 = 3 * D

        @pl.when(first)
        def _():
            dbm_ref[...] = jnp.zeros_like(dbm_ref)
            dgg_ref[...] = jnp.zeros_like(dgg_ref)

        dmerged = _dot(_b16(do_ref[...]), wo_ref[...], NT)
        dys = []
        for n, (y_ref, dp_ref) in enumerate(((ya_ref, dpa_ref), (yb_ref, dpb_ref), (ym_ref, dpm_ref))):
            sl = slice(n * D, (n + 1) * D)
            gate = _sigmoid(g_ref[:, sl] + bm_ref[:, sl])
            proj = _dot(y_ref[...], wb_ref[n])
            dproj = _b16(gate * dmerged)
            dp_ref[...] = dproj
            dgp = dmerged * proj * gate * (1.0 - gate)
            dg_ref[:, sl] = dgp.astype(dg_ref.dtype)
            dbm_ref[0:1, sl] += jnp.sum(dgp, axis=0, keepdims=True)
            dys.append(_dot(dproj, wb_ref[n], NT))
        dya, dyb, dym = dys
        sa, dsa = _silu_and_grad(az_ref[...])
        oa = oa_ref[...]
        doa = dya * sa
        doa_ref[...] = doa
        dg_ref[:, G3:G3 + WIDTH] = _b16(dya * oa * dsa)
        dl_ref[...] = _dot(hs_ref[...], doa * oa, NT, HIGHEST)
        sm, dsm = _silu_and_grad(mz_ref[...])
        dom_ref[...] = dym * sm
        dg_ref[:, G3 + 2 * WIDTH:G3 + 3 * WIDTH] = _b16(dym * om_ref[...] * dsm)
        sb, dsb = _silu_and_grad(bz_ref[...])
        xs, rs = _head_norm(ob_ref[...], None)
        gg = gg_ref[...]
        dgg = jnp.zeros((1, GDN_DIM), F32)
        for h in range(GDN_HEADS):
            cols = slice(h * GDN_DIM, (h + 1) * GDN_DIM)
            dn = dyb[:, cols] * sb[:, cols]
            dg_ref[:, G3 + WIDTH + h * GDN_DIM:G3 + WIDTH + (h + 1) * GDN_DIM] = _b16(
                dyb[:, cols] * (xs[h] * gg) * dsb[:, cols])
            dgg = dgg + jnp.sum(dn * xs[h], axis=0, keepdims=True)
            dxh = dn * gg
            dob_ref[:, cols] = rs[h] * (dxh - xs[h] * jnp.mean(dxh * xs[h], axis=1, keepdims=True))
        dgg_ref[0:1, :] += dgg

    half = lambda a: _rows(a, ts)
    w512 = lambda dt: _orow(S, (WIDTH,), dt, ts)
    return _tiled("merge_bwd", body, S // ts,
                  [_rows(dout, ts), _cols(z, ts, 3 * D, CB_GATES), half(o_a), _cols(z, ts, WIDTH, CB_AZ),
                   half(o_b), _cols(z, ts, WIDTH, CB_BZ), half(o_m), _cols(z, ts, WIDTH, CB_MZ),
                   half(ya), half(yb), half(ym), _full(gdn_g.reshape(1, GDN_DIM)),
                   _full(b_merge.reshape(1, 3 * D)), _full(wb), _full(wout), _full(hsum)],
                  [((S, N_AL), BF16, (ts, 3 * D + 3 * WIDTH), lambda i: (i, CB_MERGE)),
                   _orow(S, (D,), BF16, ts), _orow(S, (D,), BF16, ts),
                   _orow(S, (D,), BF16, ts), w512(F32), w512(F32), w512(F32),
                   ((128, S), F32, (128, ts), lambda i: (0, i)), _oacc((8, 3 * D), F32),
                   _oacc((8, GDN_DIM), F32)])


def _to_aligned(w):
    parts = [w[..., lo:lo + n] for lo, n, _ in sorted(W_IN_PIECES, key=lambda p: p[2])]
    parts.append(jnp.zeros(w.shape[:-1] + (N_AL - N_IN,), w.dtype))
    return jnp.concatenate(parts, axis=-1)


def _from_aligned(w):
    return jnp.concatenate([w[..., al:al + n] for _, n, al in W_IN_PIECES], axis=-1)


def _lanes128(v, lane0):
    return jnp.pad(v.astype(F32)[None, :], ((0, 0), (lane0, 128 - lane0 - v.shape[0])))


def _tiles(S):
    ts = min(512, S // 2)
    return dict(ts=ts, ts_small=min(256, S // 2), tq=min(512, S // 4), tq_fwd=min(1024, S // 2))


def _layer_fwd(x, mem, p):
    S = x.shape[0]
    tl = _tiles(S)
    ts, tss, tq = tl["ts"], tl["ts_small"], tl["tq"]
    h, rstd = _rms_fwd("norm_fwd", x, p["norm_g"], ts)
    z = _mm("in_proj", h, p["w_in_al"], tn=1664)

    b_fg128 = _lanes128(p["b_fg"], LANE_AF)
    fT, f_hi, f_mid, f_lo = _fox_decay(z, b_fg128, ts)
    aq = z[:, CB_AQ * WIDTH:(CB_AQ + 1) * WIDTH]
    ak = z[:, CB_AK * WIDTH:(CB_AK + 1) * WIDTH]
    av = z[:, CB_AV * WIDTH:(CB_AV + 1) * WIDTH]
    q32 = _heads_major(aq, FOX_HEADS, FOX_DIM)
    kh = _heads_major(ak, FOX_HEADS, FOX_DIM).astype(BF16)
    vh = _heads_major(av, FOX_HEADS, FOX_DIM).astype(BF16)
    khT = kh.transpose(0, 2, 1)
    piecesT = jnp.stack([f[:FOX_HEADS] for f in (f_hi, f_mid, f_lo)], axis=1)
    pieces = piecesT.transpose(0, 2, 1)
    ones3 = jnp.ones((FOX_HEADS, S, 3), BF16)
    padk = jnp.zeros((FOX_HEADS, S, FOX_AUG - FOX_DIM - 6), BF16)
    q_aug = jnp.concatenate([q32, pieces.astype(F32), ones3.astype(F32), padk.astype(F32)], axis=-1)
    k_aug = jnp.concatenate([kh, ones3, -pieces, padk], axis=-1)
    kT_aug = jnp.concatenate([khT, ones3.transpose(0, 2, 1), -piecesT, padk.transpose(0, 2, 1)], axis=1)
    v_aug = jnp.concatenate([vh, ones3[:, :, :1], jnp.zeros((FOX_HEADS, S, 128 - FOX_DIM - 1), BF16)],
                            axis=-1)
    o_h, lse, qs = _fox_fwd(q_aug, kT_aug, v_aug, tl["tq_fwd"])
    o_a = _heads_minor(o_h)

    a128 = _lanes128(p["a_log"], LANE_BA)
    dt128 = _lanes128(p["dt_bias"], LANE_BA)
    qkv, cpre, gb, gbT = _gdn_prep(z, p["conv_w"], a128, dt128, ts)
    grow = gbT[LANE_BA:LANE_BA + GDN_HEADS].reshape(GDN_HEADS, S // CHUNK, CHUNK)
    u, w, qd, kd, aqk, T = _gdn_local_fwd(qkv, gb, grow, ts)
    o_b, vn, states = _gdn_scan_fwd(u, w, qd, kd, aqk, gb, ts)

    mem_h, mem_r = _rms_fwd("mem_norm_fwd", mem, p["mem_norm_g"], mem.shape[0])
    mkv = _mm("mem_kv", mem_h, p["w_mem_kv"])
    mk, mv = mkv[:, :WIDTH], mkv[:, WIDTH:]
    o_m = _mem_attn_fwd(z, mk, mv, ts)

    x_next, ya, yb, ym, merged = _merge_fwd(x, z, o_a, o_b, o_m, p["gdn_norm_g"], p["b_merge"],
                                            p["w_branch"], p["w_out"], tss)
    saved = dict(x=x, h=h, rstd=rstd, z=z, b_fg128=b_fg128, qs=qs, k_aug=k_aug, khT=khT, vh=vh, lse=lse, o_a=o_a, a128=a128, dt128=dt128, qkv=qkv, cpre=cpre, gb=gb,
                 grow=grow, w=w, qd=qd, kd=kd, aqk=aqk, T=T, o_b=o_b, vn=vn, states=states,
                 mem_h=mem_h, mem_r=mem_r, mk=mk, mv=mv, o_m=o_m, ya=ya, yb=yb, ym=ym, merged=merged)
    return x_next, saved


def _layer_bwd(dout, mem, p, s):
    S = dout.shape[0]
    tl = _tiles(S)
    ts, tss, tq = tl["ts"], tl["ts_small"], tl["tq"]
    z = s["z"]
    hsum = (jnp.arange(128)[:, None] == jnp.arange(WIDTH)[None, :] // FOX_DIM).astype(F32)
    (dz, dpa, dpb, dpm, do_a, do_b, do_m, deltaT, db_merge, dgdn_g) = _merge_bwd(
        dout, z, s["o_a"], s["o_b"], s["o_m"], s["ya"], s["yb"], s["ym"], p["gdn_norm_g"],
        p["b_merge"], p["w_branch"], p["w_out"], hsum, tss)
    g = {}
    g["b_merge"] = db_merge[0]
    g["gdn_norm_g"] = dgdn_g[0]
    g["w_out"] = _mm("dw_out", s["merged"], dout, ta=True)
    g["w_branch"] = jnp.stack([_mm("dw_branch", y, dp, ta=True)
                               for y, dp in ((s["ya"], dpa), (s["yb"], dpb), (s["ym"], dpm))])

    do_h = _heads_major(do_a, FOX_HEADS, FOX_DIM).astype(BF16)
    delta_row = deltaT[:FOX_HEADS, None, :]
    dqT, dk_h, dv_h, dfk, dfq = _fox_bwd(s["qs"], s["k_aug"], s["khT"], s["vh"], do_h, s["lse"],
                                         delta_row, tq)
    daq = _heads_minor(dqT.transpose(0, 2, 1))
    dak = _heads_minor(dk_h)
    dav = _heads_minor(dv_h)
    daf128, db_fg = _fox_decay_bwd(dfk[:, 0, :], dfq[:, 0, :], z, s["b_fg128"], ts)
    g["b_fg"] = db_fg[:FOX_HEADS]

    du, dw, dqd, dkd, daqk, dgl = _gdn_scan_bwd(do_b, s["w"], s["qd"], s["kd"], s["aqk"], s["vn"],
                                                s["states"], s["gb"], ts)
    dqkv, dgb = _gdn_local_bwd(s["qkv"], s["gb"], s["grow"], s["T"], du, dw, dqd, dkd, daqk, dgl, ts)
    dz, dzs_b, dconv, dad = _gdn_prep_bwd(dqkv, dgb, s["cpre"], z, p["conv_w"], s["a128"],
                                          s["dt128"], dz, ts)
    g["conv_w"] = dconv[:4]
    g["a_log"] = dad[0, LANE_BA:LANE_BA + GDN_HEADS]
    g["dt_bias"] = dad[1, LANE_BA:LANE_BA + GDN_HEADS]

    dz, dmk, dmv = _mem_attn_bwd(do_m, z, s["mk"], s["mv"], dz, ts)
    dmkv = jnp.concatenate([dmk, dmv], axis=1)
    g["w_mem_kv"] = _mm("dw_mem_kv", s["mem_h"], dmkv, ta=True)
    dmem_h = _mm("dmem_h", dmkv, p["w_mem_kv"], tb=True)
    M = mem.shape[0]
    _, g["mem_norm_g"] = _rms_bwd("mem_norm_bwd", dmem_h, mem, s["mem_r"], p["mem_norm_g"],
                                  jnp.zeros_like(mem), M)

    lane = jnp.arange(128)[None, :]
    dsmall = jnp.where(lane < 8, daf128, dzs_b)
    daqkv = jnp.concatenate([_b16(daq), _b16(dak), _b16(dav)], axis=1)
    dz = lax.dynamic_update_slice(dz, daqkv, (0, CB_AQKV * 3 * WIDTH))
    dz = lax.dynamic_update_slice(dz, _b16(dsmall), (0, CB_SMALL * 128))
    g["w_in_al"] = _mm("dw_in", s["h"], dz, ta=True, tn=1664)
    dh = _mm("dh", dz, p["w_in_al"], tb=True, tk=1664)
    dx, g["norm_g"] = _rms_bwd("norm_bwd", dh, s["x"], s["rstd"], p["norm_g"], dout, ts)
    return dx, g


def _local_step(x, mem, layers, final_norm_g, loss_target):
    S = x.shape[0]
    saves = []
    cur = x
    for p in layers:
        cur, sv = _layer_fwd(cur, mem, p)
        saves.append(sv)
    dx, dgf, loss_lanes = _loss_head(cur, final_norm_g, loss_target, _tiles(S)["ts"])
    grads = [None] * len(layers)
    for l in reversed(range(len(layers))):
        dx, grads[l] = _layer_bwd(dx, mem, layers[l], saves[l])
    return loss_lanes, dx, grads, dgf


HBM_SPEC = pl.BlockSpec(memory_space=pltpu.HBM)


def _mesh_pos():
    return lax.axis_index("x"), lax.axis_index("y"), lax.axis_index("c")


def _comm_call(name, body, arrays, out_shapes, n_remote, n_local):
    n = len(arrays)

    def kern(*refs):
        body(refs[:n], refs[n:2 * n], refs[2 * n], refs[2 * n + 1], refs[2 * n + 2])

    return pl.pallas_call(
        kern, name=name, out_shape=out_shapes, in_specs=[HBM_SPEC] * n, out_specs=[HBM_SPEC] * n,
        scratch_shapes=[pltpu.SemaphoreType.DMA((n_remote,)), pltpu.SemaphoreType.DMA((n_remote,)),
                        pltpu.SemaphoreType.DMA((max(n_local, 1),))],
    )(*arrays)


def _remote(src, dst, send_sems, recv_sems, k, to):
    return pltpu.make_async_remote_copy(src_ref=src, dst_ref=dst, send_sem=send_sems.at[k],
                                        recv_sem=recv_sems.at[k], device_id=to, device_id_type=MESH_ID)


def _other_chips(mx, my):
    return [(1 - mx, my), (mx, 1 - my), (1 - mx, 1 - my)]


def _gather_chips(name, shards):
    n = len(shards)

    def body(ins, outs, send_sems, recv_sems, local_sems):
        mx, my, mc = _mesh_pos()
        me = 2 * mx + my
        sibling = (mx, my, 1 - mc)
        chips = _other_chips(mx, my)
        sends = []
        for a in range(n):
            for k, (px, py) in enumerate(chips):
                cp = _remote(ins[a].at[mc], outs[a].at[me, mc], send_sems, recv_sems, 6 * a + k,
                             (px, py, mc))
                cp.start()
                sends.append(cp)
        for a in range(n):
            for k, (px, py) in enumerate(chips):
                j = 2 * px + py
                _remote(ins[a].at[mc], outs[a].at[j, mc], send_sems, recv_sems, 6 * a + k,
                        (px, py, mc)).wait_recv()
                cp = _remote(outs[a].at[j, mc], outs[a].at[j, mc], send_sems, recv_sems, 6 * a + 3 + k,
                             sibling)
                cp.start()
                sends.append(cp)
        for a in range(n):
            for k, (px, py) in enumerate(chips):
                j = 2 * px + py
                _remote(outs[a].at[j, 1 - mc], outs[a].at[j, 1 - mc], send_sems, recv_sems,
                        6 * a + 3 + k, sibling).wait_recv()
        for cp in sends:
            cp.wait_send()

    shapes = [jax.ShapeDtypeStruct((N_CHIPS,) + s.shape, s.dtype) for s in shards]
    outs = _comm_call(name, body, shards, shapes, 6 * n, 0)
    me = 2 * lax.axis_index("x") + lax.axis_index("y")
    return [lax.dynamic_update_index_in_dim(o, s, me, 0) for o, s in zip(outs, shards)]


def _sibling_swap(gs):
    n = len(gs)

    def body(ins, outs, send_sems, recv_sems, local_sems):
        mx, my, mc = _mesh_pos()
        sends = []
        for a in range(n):
            cp = _remote(ins[a].at[:, 1 - mc], outs[a], send_sems, recv_sems, a, (mx, my, 1 - mc))
            cp.start()
            sends.append(cp)
        for cp in sends:
            cp.wait()

    shapes = [jax.ShapeDtypeStruct((g.shape[0],) + g.shape[2:], g.dtype) for g in gs]
    return _comm_call("grad_sibling_swap", body, gs, shapes, n, 0)


def _chip_exchange(ps):
    n = len(ps)

    def body(ins, outs, send_sems, recv_sems, local_sems):
        mx, my, mc = _mesh_pos()
        me = 2 * mx + my
        chips = _other_chips(mx, my)
        sends = []
        for a in range(n):
            for k, (px, py) in enumerate(chips):
                cp = _remote(ins[a].at[2 * px + py], outs[a].at[me], send_sems, recv_sems, 3 * a + k,
                             (px, py, mc))
                cp.start()
                sends.append(cp)
        for a in range(n):
            for k, (px, py) in enumerate(chips):
                _remote(ins[a].at[me], outs[a].at[2 * px + py], send_sems, recv_sems, 3 * a + k,
                        (px, py, mc)).wait_recv()
        for cp in sends:
            cp.wait_send()

    shapes = [jax.ShapeDtypeStruct(p.shape, p.dtype) for p in ps]
    outs = _comm_call("grad_chip_exchange", body, ps, shapes, 3 * n, 0)
    me = 2 * lax.axis_index("x") + lax.axis_index("y")
    return [lax.dynamic_update_index_in_dim(o, lax.dynamic_index_in_dim(p, me, 0, keepdims=False), me, 0)
            for o, p in zip(outs, ps)]


def _sibling_gather(hs):
    n = len(hs)

    def body(ins, outs, send_sems, recv_sems, local_sems):
        mx, my, mc = _mesh_pos()
        sends = []
        for a in range(n):
            cp = _remote(ins[a], outs[a], send_sems, recv_sems, a, (mx, my, 1 - mc))
            cp.start()
            sends.append(cp)
        for cp in sends:
            cp.wait()

    shapes = [jax.ShapeDtypeStruct(h.shape, h.dtype) for h in hs]
    theirs = _comm_call("grad_sibling_gather", body, hs, shapes, n, 0)
    first = lax.axis_index("c") == 0
    return [jnp.stack([jnp.where(first, h, t), jnp.where(first, t, h)]) for h, t in zip(hs, theirs)]


def _add_pairs(a, b, tr, out_dtype):
    n, H, C = a.shape

    def kern(a_ref, b_ref, o_ref):
        o_ref[...] = (a_ref[...] + b_ref[...]).astype(o_ref.dtype)

    spec = pl.BlockSpec((None, tr, C), lambda j, i: (j, i, 0))
    return pl.pallas_call(
        kern, name="grad_pair_sum", grid=(n, H // tr), in_specs=[spec, spec], out_specs=spec,
        out_shape=jax.ShapeDtypeStruct((n, H, C), out_dtype),
        compiler_params=_params(("parallel", "parallel")),
    )(a, b)


def _sum_slots(r4, tr):
    n, H, C = r4.shape

    def kern(r_ref, o_ref):
        f = lambda k: r_ref[k].astype(F32)
        o_ref[...] = ((f(0) + f(1)) + f(2)) + f(3)

    return pl.pallas_call(
        kern, name="grad_chip_sum", grid=(H // tr,),
        in_specs=[pl.BlockSpec((n, tr, C), lambda i: (0, i, 0))],
        out_specs=pl.BlockSpec((tr, C), lambda i: (i, 0)),
        out_shape=jax.ShapeDtypeStruct((H, C), F32),
        compiler_params=_params(("parallel",)),
    )(r4)


def _adamw(w, g, m, v, tr):
    R, C = w.shape
    c1 = 1.0 - ADAM_B1
    c2 = 1.0 - ADAM_B2
    bc1 = 1.0 - ADAM_B1 ** ADAM_STEP
    bc2 = 1.0 - ADAM_B2 ** ADAM_STEP

    def kern(w_ref, g_ref, m_ref, v_ref, d_ref, mo_ref, vo_ref):
        gv = g_ref[...]
        mn = ADAM_B1 * m_ref[...] + c1 * gv
        vn = ADAM_B2 * v_ref[...] + c2 * (gv * gv)
        m_hat = mn / bc1
        v_hat = vn / bc2
        d_ref[...] = -ADAM_LR * (m_hat / (jnp.sqrt(v_hat) + ADAM_EPS) + ADAM_WD * w_ref[...])
        mo_ref[...] = mn
        vo_ref[...] = vn

    spec = pl.BlockSpec((tr, C), lambda i: (i, 0))
    shape = jax.ShapeDtypeStruct((R, C), F32)
    return pl.pallas_call(
        kern, name="adamw", grid=(R // tr,), in_specs=[spec] * 4, out_specs=[spec] * 3,
        out_shape=[shape] * 3, compiler_params=_params(("parallel",)),
    )(w, g, m, v)


PACK_COLS = 1024
PACK_ROWS = 512
W_SHARD = N_IN // N_CHIPS
SLAB = ("conv_w", "w_mem_kv", "w_branch", "w_out")
SMALL =("norm_g", "b_fg", "b_merge", "a_log", "dt_bias", "gdn_norm_g", "mem_norm_g", "final_norm_g")
ALL_WEIGHTS = ("norm_g", "w_in", "b_fg", "b_merge", "conv_w", "a_log", "dt_bias", "gdn_norm_g",
               "mem_norm_g", "w_mem_kv", "w_branch", "w_out", "final_norm_g")
SHARD_AXIS = {"w_in": 2, "conv_w": 2, "w_mem_kv": 1, "w_branch": 3, "w_out": 1}


def _pack(arrays, row_multiple):
    flat = jnp.concatenate([a.reshape(-1) for a in arrays])
    n = flat.shape[0]
    rows = -(-n // PACK_COLS)
    rows = -(-rows // row_multiple) * row_multiple
    flat = jnp.pad(flat, (0, rows * PACK_COLS - n))
    return flat.reshape(rows, PACK_COLS)


def _unpack(slab, shapes):
    out, off = [], 0
    for shp in shapes:
        n = 1
        for d in shp:
            n *= d
        r0, r1 = off // PACK_COLS, -(-(off + n) // PACK_COLS)
        rows = slab[r0:r1].reshape(-1)
        out.append(rows[off - r0 * PACK_COLS:off - r0 * PACK_COLS + n].reshape(shp))
        off += n
    return out


def _shard_of(full, name, j):
    ax = SHARD_AXIS[name]
    n = full.shape[ax] // N_CHIPS
    return lax.slice_in_dim(full, j * n, (j + 1) * n, axis=ax)


def _aligned_from_shards(shards):
    def cols(lo, n):
        parts = []
        while n > 0:
            j, off = divmod(lo, W_SHARD)
            take = min(n, W_SHARD - off)
            parts.append(shards[j][..., off:off + take])
            lo, n = lo + take, n - take
        return parts

    out = []
    for lo, n, _ in sorted(W_IN_PIECES, key=lambda p: p[2]):
        out += cols(lo, n)
    out.append(jnp.zeros(shards[0].shape[:-1] + (N_AL - N_IN,), shards[0].dtype))
    return jnp.concatenate(out, axis=-1)


def _shard_from_aligned(w_al, j):
    lo_j, hi_j = j * W_SHARD, (j + 1) * W_SHARD
    parts = []
    for lo, n, al in W_IN_PIECES:
        a, b = max(lo, lo_j), min(lo + n, hi_j)
        if a < b:
            parts.append(w_al[..., al + a - lo:al + b - lo])
    return jnp.concatenate(parts, axis=-1)


def kernel(x, mem, norm_g, w_in, b_fg, b_merge, conv_w, a_log, dt_bias, gdn_norm_g, mem_norm_g, w_mem_kv, w_branch, w_out, final_norm_g, loss_target, m_norm_g, m_w_in, m_b_fg, m_b_merge, m_conv_w, m_a_log, m_dt_bias, m_gdn_norm_g, m_mem_norm_g, m_w_mem_kv, m_w_branch, m_w_out, m_final_norm_g, v_norm_g, v_w_in, v_b_fg, v_b_merge, v_conv_w, v_a_log, v_dt_bias, v_gdn_norm_g, v_mem_norm_g, v_w_mem_kv, v_w_branch, v_w_out, v_final_norm_g):
    wts = dict(norm_g=norm_g, w_in=w_in, b_fg=b_fg, b_merge=b_merge, conv_w=conv_w, a_log=a_log,
               dt_bias=dt_bias, gdn_norm_g=gdn_norm_g, mem_norm_g=mem_norm_g, w_mem_kv=w_mem_kv,
               w_branch=w_branch, w_out=w_out, final_norm_g=final_norm_g)
    mom = dict(norm_g=m_norm_g, w_in=m_w_in, b_fg=m_b_fg, b_merge=m_b_merge, conv_w=m_conv_w,
               a_log=m_a_log, dt_bias=m_dt_bias, gdn_norm_g=m_gdn_norm_g, mem_norm_g=m_mem_norm_g,
               w_mem_kv=m_w_mem_kv, w_branch=m_w_branch, w_out=m_w_out, final_norm_g=m_final_norm_g)
    vel = dict(norm_g=v_norm_g, w_in=v_w_in, b_fg=v_b_fg, b_merge=v_b_merge, conv_w=v_conv_w,
               a_log=v_a_log, dt_bias=v_dt_bias, gdn_norm_g=v_gdn_norm_g, mem_norm_g=v_mem_norm_g,
               w_mem_kv=v_w_mem_kv, w_branch=v_w_branch, w_out=v_w_out, final_norm_g=v_final_norm_g)

    big = ("w_in", "w_mem_kv", "w_branch", "w_out")
    gathered = _gather_chips("weight_gather", [wts[n].astype(BF16) for n in big] + [conv_w])
    all_w = dict(zip(big + ("conv_w",), gathered))
    w_in_al = _aligned_from_shards([all_w["w_in"][j] for j in range(N_CHIPS)])

    layers = []
    for l in range(DEPTH):
        rows_of = lambda n: all_w[n][:, l].reshape(D_MODEL, D_MODEL)
        last_of = lambda n: jnp.concatenate([all_w[n][j, l] for j in range(N_CHIPS)], axis=-1)
        layers.append(dict(norm_g=norm_g[l], w_in_al=w_in_al[l], b_fg=b_fg[l], b_merge=b_merge[l],
                           conv_w=jnp.pad(last_of("conv_w"), ((0, 4), (0, 0))), a_log=a_log[l],
                           dt_bias=dt_bias[l], gdn_norm_g=gdn_norm_g[l], mem_norm_g=mem_norm_g[l],
                           w_mem_kv=rows_of("w_mem_kv"), w_branch=last_of("w_branch"),
                           w_out=rows_of("w_out")))

    loss_lanes, dx, grads, dgf = _local_step(x[0], mem[0], layers, final_norm_g, loss_target[0])

    gfull = {n: jnp.stack([grads[l][n] for l in range(DEPTH)])
             for n in ("norm_g", "b_fg", "b_merge", "conv_w", "a_log", "dt_bias", "gdn_norm_g",
                       "mem_norm_g", "w_mem_kv", "w_branch", "w_out")}
    gfull["final_norm_g"] = dgf
    loss_local = jnp.sum(loss_lanes).reshape(1)
    small_g = [gfull[n] for n in SMALL] + [loss_local]
    dw_al = jnp.stack([grads[l]["w_in_al"] for l in range(DEPTH)])
    ga = jnp.stack([_shard_from_aligned(dw_al, j) for j in range(N_CHIPS)])
    gb = jnp.stack([_pack([_shard_of(gfull[n], n, j) for n in SLAB] + small_g, PACK_ROWS)
                    for j in range(N_CHIPS)])
    R = gb.shape[1]
    gb = gb.reshape(N_CHIPS, 2, R // 2, PACK_COLS)

    mc = lax.axis_index("c")
    tr = 256
    from_sibling = _sibling_swap([ga, gb])
    mine = [lax.dynamic_index_in_dim(g, mc, axis=1, keepdims=False) for g in (ga, gb)]
    pair = [_add_pairs(a, b, tr, dt) for a, b, dt in zip(mine, from_sibling, (BF16, F32))]
    slots = _chip_exchange(pair)
    half = [_sum_slots(s, tr) for s in slots]
    ga_sum, gb_sum = _sibling_gather(half)
    gb_sum = gb_sum.reshape(R, PACK_COLS)

    names = list(SLAB) + list(SMALL)
    shapes = [wts[n].shape for n in names] + [(1,)]
    g_un = dict(zip(names + ["loss"], _unpack(gb_sum, shapes)))
    g_un["w_in"] = ga_sum
    d_un, m_un, v_un = {}, {}, {}
    rows2d = lambda a: a.reshape(-1, a.shape[-1])
    for n in ("w_in", "w_mem_kv", "w_branch", "w_out"):
        res = _adamw(rows2d(wts[n]), rows2d(g_un[n]), rows2d(mom[n]), rows2d(vel[n]), tr)
        d_un[n], m_un[n], v_un[n] = [r.reshape(wts[n].shape) for r in res]
    little = ("conv_w",) + SMALL
    slab = lambda d: _pack([d[n] for n in little], 8)
    res = _adamw(slab(wts), slab(g_un), slab(mom), slab(vel), 8)
    little_shapes = [wts[n].shape for n in little]
    for out, r in zip((d_un, m_un, v_un), res):
        out.update(zip(little, _unpack(r, little_shapes)))

    loss = g_un["loss"][0]
    return (loss, dx[None], *[g_un[n] for n in ALL_WEIGHTS], *[d_un[n] for n in ALL_WEIGHTS],
            *[m_un[n] for n in ALL_WEIGHTS], *[v_un[n] for n in ALL_WEIGHTS])
```

```python
import functools

import jax
import jax.numpy as jnp
from jax import lax
from jax.experimental import pallas as pl
from jax.experimental.pallas import tpu as pltpu

F32 = jnp.float32
BF16 = jnp.bfloat16
HIGHEST = lax.Precision.HIGHEST
PREC_UT = lax.Precision.HIGH
MESH_ID = pl.DeviceIdType.MESH

D_MODEL = 1024
DEPTH = 2
CHUNK = 64
EPS = 1e-6
FOX_HEADS, FOX_DIM = 8, 64
GDN_HEADS, GDN_DIM = 4, 128
MEM_HEADS, MEM_DIM = 4, 128
WIDTH = 512
N_BRANCH = 3
N_IN = 8208
N_AL = 8320
N_CHIPS = 4
NEG = -1e30
LOG2E = 1.4426950408889634
LN2 = 0.6931471805599453

ADAM_LR, ADAM_B1, ADAM_B2, ADAM_EPS, ADAM_WD, ADAM_STEP = 0.001, 0.9, 0.999, 1e-08, 0.01, 10

CB_GATES = 0
CB_AZ, CB_BZ, CB_MZ = 6, 7, 8
CB_MERGE = 0
CB_BQKV = 3
CB_AQ, CB_AK, CB_AV = 12, 13, 14
CB_AQKV = 4
CB_MQ = 15
CB_SMALL = 64
W_IN_PIECES = ((0, 512, 6144), (512, 512, 6656), (1024, 512, 7168), (1536, 8, 8192), (1544, 512, 3072),
               (2056, 512, 4608), (2568, 512, 5120), (3080, 512, 5632), (3592, 4, 8200), (3596, 4, 8204),
               (3600, 512, 3584), (4112, 512, 7680), (4624, 512, 4096), (5136, 3072, 0))
LANE_AF, LANE_BA, LANE_BB = 0, 8, 12

NN = ((1,), (0,))
NT = ((1,), (1,))
TN = ((0,), (0,))

VMEM_LIMIT_BYTES = 56 * 1024 * 1024


def _dot(a, b, dims=NN, prec=None):
    return lax.dot_general(a, b, (dims, ((), ())), preferred_element_type=F32, precision=prec)


def _bdot(a, b, ca, cb, prec=None):
    return lax.dot_general(a, b, (((ca,), (cb,)), ((0,), (0,))), preferred_element_type=F32,
                           precision=prec)


def _b16(a):
    return a.astype(BF16)


def _eye(n, dtype=F32):
    r = lax.broadcasted_iota(jnp.int32, (n, n), 0)
    c = lax.broadcasted_iota(jnp.int32, (n, n), 1)
    return jnp.where(r == c, 1.0, 0.0).astype(dtype)


def _transpose_exact(x):
    return _dot(_eye(x.shape[1]), x, NT, HIGHEST)


def _col_to_row(col):
    n = col.shape[0]
    return jnp.sum(jnp.where(_eye(n) > 0.5, col, 0.0), axis=0, keepdims=True)


def _row_to_col(row):
    n = row.shape[1]
    return jnp.sum(jnp.where(_eye(n) > 0.5, row, 0.0), axis=1, keepdims=True)


def _sigmoid(x):
    return 1.0 / (1.0 + jnp.exp(-x))


def _softplus(x):
    return jnp.maximum(x, 0.0) + jnp.log(1.0 + jnp.exp(-jnp.abs(x)))


def _silu_and_grad(x):
    s = _sigmoid(x)
    return x * s, s * (1.0 + x * (1.0 - s))


def _params(semantics):
    return pltpu.CompilerParams(dimension_semantics=semantics, vmem_limit_bytes=VMEM_LIMIT_BYTES)


def _rows(a, ts):
    nd = a.ndim
    return (a, (ts,) + a.shape[1:], lambda i, nd=nd: (i,) + (0,) * (nd - 1))


def _cols(a, ts, width, cb):
    return (a, (ts, width), lambda i, cb=cb: (i, cb))


def _full(a):
    nd = a.ndim
    return (a, a.shape, lambda i, nd=nd: (0,) * nd)


def _orow(S, tail, dtype, ts):
    nd = 1 + len(tail)
    return ((S,) + tuple(tail), dtype, (ts,) + tuple(tail), lambda i, nd=nd: (i,) + (0,) * (nd - 1))


def _oacc(shape, dtype):
    nd = len(shape)
    return (tuple(shape), dtype, tuple(shape), lambda i, nd=nd: (0,) * nd)


def _tiled(name, body, n_steps, ins, outs, scratch=(), reverse=False, fill=None):
    def rev(imap):
        if not reverse:
            return imap
        return lambda i: imap(n_steps - 1 - i)

    in_specs = [pl.BlockSpec(blk, rev(imap)) for (_, blk, imap) in ins]
    out_specs = [pl.BlockSpec(blk, rev(imap)) for (_, _, blk, imap) in outs]
    out_shape = [jax.ShapeDtypeStruct(shape, dt) for (shape, dt, _, _) in outs]
    n_in, n_out = len(ins), len(outs)
    arrays = [a for (a, _, _) in ins]
    aliases = {}
    n_extra = 0
    if fill is not None:
        arrays.append(fill[0])
        in_specs.append(pl.BlockSpec(memory_space=pl.ANY))
        aliases = {n_in: fill[1]}
        n_extra = 1

    def kern(*refs):
        step = pl.program_id(0)
        t = (n_steps - 1 - step) if reverse else step
        lo = n_in + n_extra
        body(t, step == 0, refs[:n_in], refs[lo:lo + n_out], refs[lo + n_out:])

    res = pl.pallas_call(
        kern, name=name, grid=(n_steps,), in_specs=in_specs, out_specs=out_specs,
        out_shape=out_shape, scratch_shapes=list(scratch), input_output_aliases=aliases,
        compiler_params=_params(("arbitrary",)),
    )(*arrays)
    return res


def _pick(n, pref):
    if n <= pref:
        return n
    best = None
    for t in range(128, pref + 1, 128):
        if n % t == 0:
            best = t
    assert best is not None, (n, pref)
    return best


def _mm(name, a, b, ta=False, tb=False, out_dtype=F32, tm=1024, tn=1024, tk=1024):
    if ta:
        K, M = a.shape
    else:
        M, K = a.shape
    if tb:
        N, K2 = b.shape
    else:
        K2, N = b.shape
    assert K == K2, (a.shape, b.shape, ta, tb)
    tm, tn, tk = _pick(M, tm), _pick(N, tn), _pick(K, tk)
    nk = K // tk
    a_spec = (pl.BlockSpec((tk, tm), lambda i, j, k: (k, i)) if ta
              else pl.BlockSpec((tm, tk), lambda i, j, k: (i, k)))
    b_spec = (pl.BlockSpec((tn, tk), lambda i, j, k: (j, k)) if tb
              else pl.BlockSpec((tk, tn), lambda i, j, k: (k, j)))
    dims = ((0,) if ta else (1,), (1,) if tb else (0,))

    def kern_single(a_ref, b_ref, o_ref):
        o_ref[...] = _dot(_b16(a_ref[...]), _b16(b_ref[...]), dims).astype(o_ref.dtype)

    def kern_acc(a_ref, b_ref, o_ref, acc_ref):
        k = pl.program_id(2)

        @pl.when(k == 0)
        def _():
            acc_ref[...] = jnp.zeros_like(acc_ref)

        acc_ref[...] += _dot(_b16(a_ref[...]), _b16(b_ref[...]), dims)

        @pl.when(k == nk - 1)
        def _():
            o_ref[...] = acc_ref[...].astype(o_ref.dtype)

    return pl.pallas_call(
        kern_single if nk == 1 else kern_acc, name=name, grid=(M // tm, N // tn, nk),
        in_specs=[a_spec, b_spec],
        out_specs=pl.BlockSpec((tm, tn), lambda i, j, k: (i, j)),
        out_shape=jax.ShapeDtypeStruct((M, N), out_dtype),
        scratch_shapes=[] if nk == 1 else [pltpu.VMEM((tm, tn), F32)],
        compiler_params=_params(("parallel", "parallel", "arbitrary")),
    )(a, b)


def _rms_fwd(name, x, g, ts):
    S, D = x.shape

    def body(t, first, ins, outs, scratch):
        x_ref, g_ref = ins
        h_ref, r_ref = outs
        xv = x_ref[...]
        r = lax.rsqrt(jnp.mean(xv * xv, axis=1, keepdims=True) + EPS)
        h_ref[...] = (xv * r * g_ref[...]).astype(h_ref.dtype)
        r_ref[...] = r

    return _tiled(name, body, S // ts, [_rows(x, ts), _full(g.reshape(1, D))],
                  [_orow(S, (D,), BF16, ts), _orow(S, (1,), F32, ts)])


def _rms_bwd(name, dh, x, rstd, g, dres, ts):
    S, D = x.shape

    def body(t, first, ins, outs, scratch):
        dh_ref, x_ref, r_ref, g_ref, dres_ref = ins
        dx_ref, dg_ref = outs
        r = r_ref[...]
        xh = x_ref[...] * r
        dhv = dh_ref[...]
        dxh = dhv * g_ref[...]
        dx_ref[...] = dres_ref[...] + r * (dxh - xh * jnp.mean(dxh * xh, axis=1, keepdims=True))

        @pl.when(first)
        def _():
            dg_ref[...] = jnp.zeros_like(dg_ref)

        dg_ref[0:1, :] += jnp.sum(dhv * xh, axis=0, keepdims=True)

    dx, dg = _tiled(name, body, S // ts,
                    [_rows(dh, ts), _rows(x, ts), _rows(rstd, ts), _full(g.reshape(1, D)), _rows(dres, ts)],
                    [_orow(S, (D,), F32, ts), _oacc((8, D), F32)])
    return dx, dg[0]


def _loss_head(x, g, target, ts):
    S, D = x.shape

    def body(t, first, ins, outs, scratch):
        x_ref, g_ref, tgt_ref = ins
        dx_ref, dg_ref, loss_ref = outs
        xv = x_ref[...]
        gv = g_ref[...]
        r = lax.rsqrt(jnp.mean(xv * xv, axis=1, keepdims=True) + EPS)
        xh = xv * r
        err = xh * gv - tgt_ref[...]
        dy = err * (1.0 / D)
        dxh = dy * gv
        dx_ref[...] = r * (dxh - xh * jnp.mean(dxh * xh, axis=1, keepdims=True))

        @pl.when(first)
        def _():
            dg_ref[...] = jnp.zeros_like(dg_ref)
            loss_ref[...] = jnp.zeros_like(loss_ref)

        dg_ref[0:1, :] += jnp.sum(dy * xh, axis=0, keepdims=True)
        per_lane = jnp.sum(err * err, axis=0, keepdims=True)
        loss_ref[0:1, :] += per_lane * (0.5 / D)

    dx, dg, loss = _tiled("loss_head", body, S // ts,
                          [_rows(x, ts), _full(g.reshape(1, D)), _rows(target, ts)],
                          [_orow(S, (D,), F32, ts), _oacc((8, D), F32), _oacc((8, D), F32)])
    return dx, dg[0], loss[0]


def _scan_rows(x, length, seg, reverse=False):
    row = lax.broadcasted_iota(jnp.int32, x.shape, 0) % seg
    k = 1
    while k < seg:
        if reverse:
            x = x + jnp.where(row < seg - k, pltpu.roll(x, length - k, 0), 0.0)
        else:
            x = x + jnp.where(row >= k, pltpu.roll(x, k, 0), 0.0)
        k *= 2
    return x


def _fox_decay(z, b_fg128, ts):
    S = z.shape[0]

    def body(t, first, ins, outs, scratch):
        zs_ref, b_ref = ins
        f_ref, hi_ref, mid_ref, lo_ref = outs
        (carry,) = scratch

        @pl.when(first)
        def _():
            carry[...] = jnp.zeros_like(carry)

        logf = -_softplus(-(zs_ref[...] + b_ref[...]))
        run = _scan_rows(logf, ts, ts) + carry[0:1, :]
        carry[0:1, :] = run[ts - 1:ts, :]
        f_ref[...] = _transpose_exact(run)
        f2 = run * LOG2E
        hi = f2.astype(BF16)
        r1 = f2 - hi.astype(F32)
        mid = r1.astype(BF16)
        lo = (r1 - mid.astype(F32)).astype(BF16)
        eye = _eye(128, BF16)
        hi_ref[...] = _dot(eye, hi, NT).astype(BF16)
        mid_ref[...] = _dot(eye, mid, NT).astype(BF16)
        lo_ref[...] = _dot(eye, lo, NT).astype(BF16)

    tcol = lambda dt: ((128, S), dt, (128, ts), lambda i: (0, i))
    return _tiled("fox_decay", body, S // ts,
                  [_cols(z, ts, 128, CB_SMALL), _full(b_fg128)],
                  [tcol(F32), tcol(BF16), tcol(BF16), tcol(BF16)], scratch=[pltpu.VMEM((8, 128), F32)])


def _fox_decay_bwd(dfk_rows, dfq_rows, z, b_fg128, ts):
    S = z.shape[0]
    H = dfk_rows.shape[0]

    def body(t, first, ins, outs, scratch):
        dfk_ref, dfq_ref, zs_ref, b_ref = ins
        daf_ref, db_ref = outs
        (carry,) = scratch

        @pl.when(first)
        def _():
            carry[...] = jnp.zeros_like(carry)
            db_ref[...] = jnp.zeros_like(db_ref)

        r = lax.broadcasted_iota(jnp.int32, (H, 128), 0)
        c = lax.broadcasted_iota(jnp.int32, (H, 128), 1)
        place = jnp.where(r == c, 1.0, 0.0)
        df = _dot(dfk_ref[...] + dfq_ref[...], place, TN, HIGHEST)
        run = _scan_rows(df, ts, ts, reverse=True) + carry[0:1, :]
        carry[0:1, :] = run[0:1, :]
        daf = run * _sigmoid(-(zs_ref[...] + b_ref[...]))
        daf_ref[...] = daf
        db_ref[0:1, :] += jnp.sum(daf, axis=0, keepdims=True)

    rowsin = lambda a: (a, (H, ts), lambda i: (0, i))
    daf, db = _tiled("fox_decay_bwd", body, S // ts,
                     [rowsin(dfk_rows), rowsin(dfq_rows), _cols(z, ts, 128, CB_SMALL), _full(b_fg128)],
                     [_orow(S, (128,), F32, ts), _oacc((8, 128), F32)],
                     scratch=[pltpu.VMEM((8, 128), F32)], reverse=True)
    return daf, db[0]


FOX_AUG = 80


def _fox_fwd(q_aug, kT_aug, v_aug, tq):
    H, S, da = q_aug.shape
    dv = v_aug.shape[2]
    d = FOX_DIM
    tk = tq // 2
    qscale = (d ** -0.5) * LOG2E

    def kern(q_ref, kT_ref, v_ref, o_ref, lse_ref, qs_ref, s_buf, p_buf, m_scr, acc_scr):
        i = pl.program_id(1)
        col = lax.broadcasted_iota(jnp.int32, (1, da), 1)
        qb = _b16(q_ref[...] * jnp.where(col < d, qscale, 1.0))
        qs_ref[...] = qb

        def keys(t):
            return pl.ds(pl.multiple_of(t * tk, tk), tk)

        def stage(t, slot, mask_off, look_ahead):
            if look_ahead:
                s_buf[1 - slot] = _dot(qb, kT_ref[:, keys(t + 1)])
            pv = _dot(p_buf[1 - slot], v_ref[keys(jnp.maximum(t - 1, 0)), :])

            def scores():
                s = s_buf[slot]
                if mask_off is None:
                    return s
                r = lax.broadcasted_iota(jnp.int32, (tq, tk), 0)
                c = lax.broadcasted_iota(jnp.int32, (tq, tk), 1)
                return jnp.where(c + mask_off <= r, s, NEG)

            m = m_scr[...]
            m_new = jnp.maximum(m, jnp.max(scores(), axis=1, keepdims=True))
            alpha = jnp.exp2(m - m_new)
            p_buf[slot] = _b16(jnp.exp2(scores() - m_new))
            m_scr[...] = m_new
            acc_scr[...] = (acc_scr[...] + pv) * alpha

        s_buf[0] = _dot(qb, kT_ref[:, keys(0)])
        p_buf[1] = jnp.zeros((tq, tk), BF16)
        m_scr[...] = jnp.full((tq, 1), NEG, F32)
        acc_scr[...] = jnp.zeros((tq, dv), F32)

        def pair(n):
            stage(2 * n, 0, None, True)
            stage(2 * n + 1, 1, None, True)

        def quad(m, _):
            pair(2 * m)
            pair(2 * m + 1)
            return 0

        lax.fori_loop(0, i // 2, quad, 0)

        @pl.when(i % 2 == 1)
        def _():
            pair(i - 1)

        stage(2 * i, 0, 0, True)
        stage(2 * i + 1, 1, tk, False)
        acc = acc_scr[...] + _dot(p_buf[1], v_ref[keys(2 * i + 1), :])
        l = acc[:, d:d + 1]
        o_ref[...] = acc[:, :d] / l
        lse_ref[...] = _col_to_row(m_scr[...] + jnp.log(l) * LOG2E)

    return pl.pallas_call(
        kern, name="fox_fwd", grid=(H, S // tq),
        in_specs=[pl.BlockSpec((None, tq, da), lambda h, i: (h, i, 0)),
                  pl.BlockSpec((None, da, S), lambda h, i: (h, 0, 0)),
                  pl.BlockSpec((None, S, dv), lambda h, i: (h, 0, 0))],
        out_specs=[pl.BlockSpec((None, tq, d), lambda h, i: (h, i, 0)),
                   pl.BlockSpec((None, 1, tq), lambda h, i: (h, 0, i)),
                   pl.BlockSpec((None, tq, da), lambda h, i: (h, i, 0))],
        out_shape=[jax.ShapeDtypeStruct((H, S, d), F32), jax.ShapeDtypeStruct((H, 1, S), F32),
                   jax.ShapeDtypeStruct((H, S, da), BF16)],
        scratch_shapes=[pltpu.VMEM((2, tq, tk), F32), pltpu.VMEM((2, tq, tk), BF16),
                        pltpu.VMEM((tq, 1), F32), pltpu.VMEM((tq, dv), F32)],
        compiler_params=_params(("parallel", "arbitrary")),
    )(q_aug, kT_aug, v_aug)


def _fox_bwd(qs, k_aug, kT, v, do, lse_row, delta_row, tq):
    H, S, da = qs.shape
    d = FOX_DIM
    tk = tq
    nq = S // tq
    scale = d ** -0.5

    ts2 = tq // 2
    last = 2 * nq - 1

    def kern(q_ref, k_ref, kT_ref, v_ref, do_ref, lse_ref, dl_ref,
             dqT_ref, dk_ref, dv_ref, dfk_ref, dfq_ref,
             kq_buf, dp_buf, pb_buf, ds_buf, dk_scr, dv_scr, dfk_scr):
        j = pl.program_id(1)

        @pl.when(j == 0)
        def _():
            dqT_ref[...] = jnp.zeros_like(dqT_ref)
            dfq_ref[...] = jnp.zeros_like(dfq_ref)

        kb = k_ref[...]
        kTb = kT_ref[...]
        vb = v_ref[...]
        dk_scr[...] = jnp.zeros_like(dk_scr)
        dv_scr[...] = jnp.zeros_like(dv_scr)
        dfk_scr[...] = jnp.zeros_like(dfk_scr)

        def queries(t):
            return pl.ds(pl.multiple_of(t * ts2, ts2), ts2)

        def products(t, slot):
            rows = queries(t)
            kq_buf[slot] = _dot(kb, q_ref[rows, :], NT)
            dp_buf[slot] = _dot(vb, do_ref[rows, :], NT)

        def pointwise(t, slot, mask_off):
            rows = queries(t)
            sT = kq_buf[slot]
            if mask_off is not None:
                r = lax.broadcasted_iota(jnp.int32, (tk, ts2), 0)
                c = lax.broadcasted_iota(jnp.int32, (tk, ts2), 1)
                sT = jnp.where(r <= c + mask_off, sT, NEG)
            pT = jnp.exp2(sT - lse_ref[:, rows])
            dsT = pT * (dp_buf[slot] - dl_ref[:, rows])
            pb_buf[slot] = _b16(pT)
            ds_buf[slot] = _b16(dsT)
            dfk_scr[...] -= jnp.sum(dsT, axis=1, keepdims=True)
            dfq_ref[:, rows] += jnp.sum(dsT, axis=0, keepdims=True)

        def accumulate(t, slot):
            rows = queries(t)
            dsb = ds_buf[slot]
            dv_scr[...] += _dot(pb_buf[slot], do_ref[rows, :])
            dk_scr[...] += _dot(dsb, q_ref[rows, :])
            dqT_ref[:, rows] += _dot(kTb, dsb) * scale

        def stage(t, slot, mask_off, has_prev):
            products(jnp.minimum(t + 1, last), 1 - slot)
            if has_prev:
                accumulate(t - 1, 1 - slot)
            pointwise(t, slot, mask_off)

        products(2 * j, 0)
        stage(2 * j, 0, 0, False)
        stage(2 * j + 1, 1, ts2, True)

        def pair(n):
            stage(2 * n, 0, None, True)
            stage(2 * n + 1, 1, None, True)

        def quad(m, _):
            pair(j + 1 + 2 * m)
            pair(j + 2 + 2 * m)
            return 0

        n_rest = nq - 1 - j
        lax.fori_loop(0, n_rest // 2, quad, 0)

        @pl.when(n_rest % 2 == 1)
        def _():
            pair(nq - 1)

        accumulate(last, 1)
        dk_ref[...] = dk_scr[:, :d] * LN2
        dv_ref[...] = dv_scr[...]
        dfk_ref[...] = _col_to_row(dfk_scr[...])

    tile = lambda h, j: (h, j, 0)
    whole = lambda h, j: (h, 0, 0)
    rowtile = lambda h, j: (h, 0, j)
    return pl.pallas_call(
        kern, name="fox_bwd", grid=(H, S // tk),
        in_specs=[pl.BlockSpec((None, S, da), whole),
                  pl.BlockSpec((None, tk, da), tile),
                  pl.BlockSpec((None, d, tk), lambda h, j: (h, 0, j)),
                  pl.BlockSpec((None, tk, d), tile),
                  pl.BlockSpec((None, S, d), whole),
                  pl.BlockSpec((None, 1, S), whole),
                  pl.BlockSpec((None, 1, S), whole)],
        out_specs=[pl.BlockSpec((None, d, S), whole),
                   pl.BlockSpec((None, tk, d), tile),
                   pl.BlockSpec((None, tk, d), tile),
                   pl.BlockSpec((None, 1, tk), rowtile),
                   pl.BlockSpec((None, 1, S), whole)],
        out_shape=[jax.ShapeDtypeStruct((H, d, S), F32), jax.ShapeDtypeStruct((H, S, d), F32),
                   jax.ShapeDtypeStruct((H, S, d), F32), jax.ShapeDtypeStruct((H, 1, S), F32),
                   jax.ShapeDtypeStruct((H, 1, S), F32)],
        scratch_shapes=[pltpu.VMEM((2, tk, ts2), F32), pltpu.VMEM((2, tk, ts2), F32),
                        pltpu.VMEM((2, tk, ts2), BF16), pltpu.VMEM((2, tk, ts2), BF16),
                        pltpu.VMEM((tk, da), F32), pltpu.VMEM((tk, d), F32), pltpu.VMEM((tk, 1), F32)],
        compiler_params=_params(("parallel", "arbitrary")),
    )(qs, k_aug, kT, v, do, lse_row, delta_row)


def _heads_major(a, H, d):
    S = a.shape[0]
    return a.reshape(S, H, d).transpose(1, 0, 2)


def _heads_minor(a):
    H, S, d = a.shape
    return a.transpose(1, 0, 2).reshape(S, H * d)


def _lane_pick(x128, lane):
    return x128[:, lane:lane + 1]


def _l2_fwd(y):
    return lax.rsqrt(jnp.sum(y * y, axis=1, keepdims=True) + EPS)


def _gdn_prep(z, conv_w, a128, dt128, ts):
    S = z.shape[0]
    C3 = 3 * WIDTH
    hb = ts // 8

    def body(t, first, ins, outs, scratch):
        x_ref, halo_ref, zs_ref, w_ref, a_ref, dt_ref = ins
        qkv_ref, c_ref, gb_ref, gbT_ref = outs
        halo = jnp.where(t > 0, halo_ref[...], 0.0)
        xe = jnp.concatenate([halo, x_ref[...]], axis=0)
        w = w_ref[...]
        c = w[3:4, :] * xe[8:, :]
        for back in (1, 2, 3):
            c = c + w[3 - back:4 - back, :] * pltpu.roll(xe, back, 0)[8:, :]
        c_ref[...] = c
        y = c * _sigmoid(c)
        for h in range(GDN_HEADS):
            lo = h * GDN_DIM
            yq = y[:, lo:lo + GDN_DIM]
            qkv_ref[:, lo:lo + GDN_DIM] = yq * (_l2_fwd(yq) * (GDN_DIM ** -0.5))
            yk = y[:, WIDTH + lo:WIDTH + lo + GDN_DIM]
            qkv_ref[:, WIDTH + lo:WIDTH + lo + GDN_DIM] = yk * _l2_fwd(yk)
        qkv_ref[:, 2 * WIDTH:] = y[:, 2 * WIDTH:]
        zs = zs_ref[...]
        lane = lax.broadcasted_iota(jnp.int32, zs.shape, 1)
        g = -jnp.exp(a_ref[...]) * _softplus(zs + dt_ref[...])
        G = _scan_rows(g, ts, CHUNK)
        beta = _sigmoid(zs)
        out = jnp.where(lane < 8, pltpu.roll(g, 128 - LANE_BA, 1), jnp.where(lane < LANE_BB, G, beta))
        gb_ref[...] = out
        gbT_ref[...] = _transpose_exact(out)

    x_in = (z, (ts, C3), lambda i: (i, CB_BQKV))
    halo_in = (z, (8, C3), lambda i: (jnp.maximum(i * hb - 1, 0), CB_BQKV))
    return _tiled("gdn_prep", body, S // ts,
                  [x_in, halo_in, _cols(z, ts, 128, CB_SMALL), _full(conv_w), _full(a128), _full(dt128)],
                  [_orow(S, (C3,), F32, ts), _orow(S, (C3,), F32, ts), _orow(S, (128,), F32, ts),
                   ((128, S), F32, (128, ts), lambda i: (0, i))])


def _chunk_masks(nc):
    r = lax.broadcasted_iota(jnp.int32, (nc, CHUNK, CHUNK), 1)
    c = lax.broadcasted_iota(jnp.int32, (nc, CHUNK, CHUNK), 2)
    return c <= r, c < r, c == r


def _chunk_local(qh, kh, vh, Gc, Gr, beta):
    nc = qh.shape[0]
    incl, strict, _ = _chunk_masks(nc)
    gamma = jnp.exp(jnp.where(incl, Gc - Gr, NEG))
    kb = kh * beta
    P = _bdot(_b16(kb), _b16(kh), 2, 2)
    Qk = _bdot(_b16(qh), _b16(kh), 2, 2)
    eG = jnp.exp(Gc)
    Gl = Gc[:, CHUNK - 1:CHUNK, :]
    edec = jnp.exp(Gl - Gc)
    return incl, strict, gamma, kb, P, Qk, eG, edec


def _gdn_local_fwd(qkv, gb, grow, ts):
    S = qkv.shape[0]
    nc = ts // CHUNK

    def body(t, first, ins, outs, scratch):
        q_ref, k_ref, v_ref, gb_ref, gr_ref = ins
        u_ref, w_ref, qd_ref, kd_ref, aqk_ref, T_ref = outs
        gbv = gb_ref[...]
        for h in range(GDN_HEADS):
            lo = h * GDN_DIM
            qh = q_ref[:, lo:lo + GDN_DIM].reshape(nc, CHUNK, GDN_DIM)
            kh = k_ref[:, lo:lo + GDN_DIM].reshape(nc, CHUNK, GDN_DIM)
            vh = v_ref[:, lo:lo + GDN_DIM].reshape(nc, CHUNK, GDN_DIM)
            Gc = _lane_pick(gbv, LANE_BA + h).reshape(nc, CHUNK, 1)
            beta = _lane_pick(gbv, LANE_BB + h).reshape(nc, CHUNK, 1)
            Gr = gr_ref[h].reshape(nc, 1, CHUNK)
            incl, strict, gamma, kb, P, Qk, eG, edec = _chunk_local(qh, kh, vh, Gc, Gr, beta)
            A = jnp.where(strict, P * gamma, 0.0)
            _, _, eye = _chunk_masks(nc)
            T = jnp.where(eye, 1.0, 0.0) - A
            X = A
            for _ in range(5):
                X = _bdot(X, X, 2, 1, PREC_UT)
                T = T + _bdot(T, X, 2, 1, PREC_UT)
            u = _bdot(T, vh * beta, 2, 1, PREC_UT)
            w = _bdot(T, kb * eG, 2, 1, PREC_UT)
            u_ref[:, lo:lo + GDN_DIM] = u.reshape(ts, GDN_DIM)
            w_ref[:, lo:lo + GDN_DIM] = w.reshape(ts, GDN_DIM)
            qd_ref[:, lo:lo + GDN_DIM] = (qh * eG).reshape(ts, GDN_DIM)
            kd_ref[:, lo:lo + GDN_DIM] = (kh * edec).reshape(ts, GDN_DIM)
            aqk_ref[h] = jnp.where(incl, Qk * gamma, 0.0).reshape(ts, CHUNK)
            T_ref[h] = T.reshape(ts, CHUNK)

    wide = _orow(S, (WIDTH,), F32, ts)
    perhead = ((GDN_HEADS, S, CHUNK), F32, (GDN_HEADS, ts, CHUNK), lambda i: (0, i, 0))
    return _tiled("gdn_local_fwd", body, S // ts,
                  [_cols(qkv, ts, WIDTH, 0), _cols(qkv, ts, WIDTH, 1), _cols(qkv, ts, WIDTH, 2),
                   _rows(gb, ts), (grow, (GDN_HEADS, nc, CHUNK), lambda i: (0, i, 0))],
                  [wide, wide, wide, wide, perhead, perhead])


def _gdn_scan_fwd(u, w, qd, kd, aqk, gb, ts):
    S = u.shape[0]
    nc = ts // CHUNK
    N = S // CHUNK

    def body(t, first, ins, outs, scratch):
        u_ref, w_ref, qd_ref, kd_ref, aqk_ref, gb_ref = ins
        o_ref, vn_ref, st_ref = outs
        (state,) = scratch

        @pl.when(first)
        def _():
            state[...] = jnp.zeros_like(state)

        def chunk(c, _):
            r0 = pl.multiple_of(c * CHUNK, CHUNK)
            rows = pl.ds(r0, CHUNK)
            glast = gb_ref[pl.ds(r0 + CHUNK - 1, 1), :]
            heads = range(GDN_HEADS)
            cols = [slice(h * GDN_DIM, (h + 1) * GDN_DIM) for h in heads]
            S_old = [state[h] for h in heads]
            u_h = [u_ref[rows, cols[h]] for h in heads]
            w_h = [_b16(w_ref[rows, cols[h]]) for h in heads]
            qd_h = [_b16(qd_ref[rows, cols[h]]) for h in heads]
            kd_h = [_b16(kd_ref[rows, cols[h]]) for h in heads]
            aqk_h = [_b16(aqk_ref[h, rows, :]) for h in heads]
            S_new, o_h, vn_h = [], [], []
            for h in heads:
                Sb = _b16(S_old[h])
                both = _dot(jnp.concatenate([w_h[h], qd_h[h]], axis=0), Sb)
                vn = u_h[h] - both[:CHUNK]
                vnb = _b16(vn)
                o_h.append(both[CHUNK:] + _dot(aqk_h[h], vnb))
                egl = jnp.exp(glast[:, LANE_BA + h:LANE_BA + h + 1])
                S_new.append(S_old[h] * egl + _dot(kd_h[h], vnb, TN))
                vn_h.append(vn)
            for h in heads:
                st_ref[c, h] = S_old[h]
                state[h] = S_new[h]
                o_ref[rows, cols[h]] = o_h[h]
                vn_ref[rows, cols[h]] = vn_h[h]
            return 0

        lax.fori_loop(0, nc, chunk, 0)

    wide_in = lambda a: _rows(a, ts)
    wide = _orow(S, (WIDTH,), F32, ts)
    states = ((N, GDN_HEADS, GDN_DIM, GDN_DIM), F32, (nc, GDN_HEADS, GDN_DIM, GDN_DIM),
              lambda i: (i, 0, 0, 0))
    return _tiled("gdn_scan_fwd", body, S // ts,
                  [wide_in(u), wide_in(w), wide_in(qd), wide_in(kd),
                   (aqk, (GDN_HEADS, ts, CHUNK), lambda i: (0, i, 0)), _rows(gb, ts)],
                  [wide, wide, states],
                  scratch=[pltpu.VMEM((GDN_HEADS, GDN_DIM, GDN_DIM), F32)])


def _gdn_scan_bwd(do, w, qd, kd, aqk, vn, states, gb, ts):
    S = do.shape[0]
    nc = ts // CHUNK
    N = S // CHUNK

    def body(t, first, ins, outs, scratch):
        do_ref, w_ref, qd_ref, kd_ref, aqk_ref, vn_ref, st_ref, gb_ref = ins
        du_ref, dw_ref, dqd_ref, dkd_ref, daqk_ref, dgl_ref = outs
        (dstate,) = scratch

        @pl.when(first)
        def _():
            dstate[...] = jnp.zeros_like(dstate)

        r = lax.broadcasted_iota(jnp.int32, (CHUNK, CHUNK), 0)
        cc = lax.broadcasted_iota(jnp.int32, (CHUNK, CHUNK), 1)
        incl = cc <= r
        lane = lax.broadcasted_iota(jnp.int32, (1, 128), 1)

        def chunk(k, _):
            c = nc - 1 - k
            r0 = pl.multiple_of(c * CHUNK, CHUNK)
            rows = pl.ds(r0, CHUNK)
            glast = gb_ref[pl.ds(r0 + CHUNK - 1, 1), :]
            dgl_row = jnp.zeros((1, 128), F32)
            heads = range(GDN_HEADS)
            cols = [slice(h * GDN_DIM, (h + 1) * GDN_DIM) for h in heads]
            S_h = [st_ref[c, h] for h in heads]
            dS_h = [dstate[h] for h in heads]
            do_h = [_b16(do_ref[rows, cols[h]]) for h in heads]
            aqk_h = [_b16(aqk_ref[h, rows, :]) for h in heads]
            vn_h = [_b16(vn_ref[rows, cols[h]]) for h in heads]
            kd_h = [_b16(kd_ref[rows, cols[h]]) for h in heads]
            qd_h = [_b16(qd_ref[rows, cols[h]]) for h in heads]
            w_h = [_b16(w_ref[rows, cols[h]]) for h in heads]
            res = []
            for h in heads:
                Sb, dSb, dob, vnb = _b16(S_h[h]), _b16(dS_h[h]), do_h[h], vn_h[h]
                dvn = _dot(aqk_h[h], dob, TN) + _dot(kd_h[h], dSb)
                dvnb = _b16(dvn)
                daqk = jnp.where(incl, _dot(dob, vnb, NT), 0.0)
                both = jnp.concatenate([dob, dvnb], axis=0)
                by_state = _dot(both, Sb, NT)
                dqd = by_state[:CHUNK]
                dw = -by_state[CHUNK:]
                dkd = _dot(vnb, dSb, NT)
                egl = jnp.exp(glast[:, LANE_BA + h:LANE_BA + h + 1])
                dgl = egl * jnp.sum(jnp.sum(dS_h[h] * S_h[h], axis=1, keepdims=True), axis=0,
                                    keepdims=True)
                dgl_row = jnp.where(lane == h, dgl, dgl_row)
                dS_new = _dot(jnp.concatenate([qd_h[h], -w_h[h]], axis=0), both, TN) + egl * dS_h[h]
                res.append((daqk, dqd, dkd, dw, dvn, dS_new))
            for h in heads:
                daqk, dqd, dkd, dw, dvn, dS_new = res[h]
                daqk_ref[h, rows, :] = daqk
                dqd_ref[rows, cols[h]] = dqd
                dkd_ref[rows, cols[h]] = dkd
                dw_ref[rows, cols[h]] = dw
                du_ref[rows, cols[h]] = dvn
                dstate[h] = dS_new
            dgl_ref[pl.ds(c, 1), :] = dgl_row
            return 0

        lax.fori_loop(0, nc, chunk, 0)

    wide_in = lambda a: _rows(a, ts)
    wide = _orow(S, (WIDTH,), F32, ts)
    perhead_in = lambda a: (a, (GDN_HEADS, ts, CHUNK), lambda i: (0, i, 0))
    perhead = ((GDN_HEADS, S, CHUNK), F32, (GDN_HEADS, ts, CHUNK), lambda i: (0, i, 0))
    return _tiled("gdn_scan_bwd", body, S // ts,
                  [wide_in(do), wide_in(w), wide_in(qd), wide_in(kd), perhead_in(aqk), wide_in(vn),
                   (states, (nc, GDN_HEADS, GDN_DIM, GDN_DIM), lambda i: (i, 0, 0, 0)), _rows(gb, ts)],
                  [wide, wide, wide, wide, perhead, ((N, 128), F32, (nc, 128), lambda i: (i, 0))],
                  scratch=[pltpu.VMEM((GDN_HEADS, GDN_DIM, GDN_DIM), F32)], reverse=True)


def _gdn_local_bwd(qkv, gb, grow, T, du, dw, dqd, dkd, daqk, dgl, ts):
    S = qkv.shape[0]
    nc = ts // CHUNK

    def body(t, first, ins, outs, scratch):
        (q_ref, k_ref, v_ref, gb_ref, gr_ref, T_ref, du_ref, dw_ref, dqd_ref, dkd_ref,
         daqk_ref, dgl_ref) = ins
        dqkv_ref, dgb_ref = outs
        gbv = gb_ref[...]
        dglv = dgl_ref[...]
        lane = lax.broadcasted_iota(jnp.int32, (ts, 128), 1)
        dG_all = jnp.zeros((ts, 128), F32)
        dbeta_all = jnp.zeros((ts, 128), F32)
        for h in range(GDN_HEADS):
            lo = h * GDN_DIM
            cols = slice(lo, lo + GDN_DIM)
            r3 = lambda ref: ref[:, cols].reshape(nc, CHUNK, GDN_DIM)
            qh, kh, vh = r3(q_ref), r3(k_ref), r3(v_ref)
            duh, dwh, dqdh, dkdh = r3(du_ref), r3(dw_ref), r3(dqd_ref), r3(dkd_ref)
            Gc = _lane_pick(gbv, LANE_BA + h).reshape(nc, CHUNK, 1)
            beta = _lane_pick(gbv, LANE_BB + h).reshape(nc, CHUNK, 1)
            Gr = gr_ref[h].reshape(nc, 1, CHUNK)
            Th = T_ref[h].reshape(nc, CHUNK, CHUNK)
            daq = daqk_ref[h].reshape(nc, CHUNK, CHUNK)
            incl, strict, gamma, kb, P, Qk, eG, edec = _chunk_local(qh, kh, vh, Gc, Gr, beta)
            _, _, eye = _chunk_masks(nc)
            vb = vh * beta
            kbg = kb * eG
            dvb = _bdot(Th, duh, 1, 1, PREC_UT)
            dkbg = _bdot(Th, dwh, 1, 1, PREC_UT)
            dT = _bdot(duh, vb, 2, 2, PREC_UT) + _bdot(dwh, kbg, 2, 2, PREC_UT)
            M1 = _bdot(Th, dT, 1, 1, PREC_UT)
            dA = jnp.where(strict, -_bdot(M1, Th, 2, 2, PREC_UT), 0.0)
            dP = dA * gamma
            dQ = daq * gamma
            dgam = (dA * P + daq * Qk) * gamma
            dPb, dQb = _b16(dP), _b16(dQ)
            khb, qhb, kbb = _b16(kh), _b16(qh), _b16(kb)
            dq = _bdot(dQb, khb, 2, 1) + dqdh * eG
            dkb = _bdot(dPb, khb, 2, 1) + dkbg * eG
            dk = (_bdot(dQb, qhb, 1, 1) + _bdot(dPb, kbb, 1, 1) + dkdh * edec + dkb * beta)
            dbeta = (jnp.sum(dkb * kh, axis=2, keepdims=True) + jnp.sum(dvb * vh, axis=2, keepdims=True))
            dv = dvb * beta
            col_as_col = jnp.sum(jnp.where(eye, jnp.sum(dgam, axis=1, keepdims=True), 0.0),
                                 axis=2, keepdims=True)
            kd_term = jnp.sum(dkdh * kh * edec, axis=2, keepdims=True)
            dG = (jnp.sum(dgam, axis=2, keepdims=True) - col_as_col
                  + jnp.sum(dqdh * qh * eG, axis=2, keepdims=True)
                  + jnp.sum(dkbg * kbg, axis=2, keepdims=True) - kd_term)
            dgl_h = dglv[:, h:h + 1].reshape(nc, 1, 1) + jnp.sum(kd_term, axis=1, keepdims=True)
            last = lax.broadcasted_iota(jnp.int32, (nc, CHUNK, 1), 1) == CHUNK - 1
            dG = dG + jnp.where(last, dgl_h, 0.0)
            dqkv_ref[:, cols] = dq.reshape(ts, GDN_DIM)
            dqkv_ref[:, WIDTH + lo:WIDTH + lo + GDN_DIM] = dk.reshape(ts, GDN_DIM)
            dqkv_ref[:, 2 * WIDTH + lo:2 * WIDTH + lo + GDN_DIM] = dv.reshape(ts, GDN_DIM)
            dG_all = jnp.where(lane == LANE_BA + h, dG.reshape(ts, 1), dG_all)
            dbeta_all = jnp.where(lane == LANE_BB + h, dbeta.reshape(ts, 1), dbeta_all)
        dg_all = _scan_rows(dG_all, ts, CHUNK, reverse=True)
        dgb_ref[...] = jnp.where(lane < LANE_BB, dg_all, dbeta_all)

    wide_in = lambda a: _rows(a, ts)
    perhead_in = lambda a: (a, (GDN_HEADS, ts, CHUNK), lambda i: (0, i, 0))
    return _tiled("gdn_local_bwd", body, S // ts,
                  [_cols(qkv, ts, WIDTH, 0), _cols(qkv, ts, WIDTH, 1), _cols(qkv, ts, WIDTH, 2),
                   _rows(gb, ts), (grow, (GDN_HEADS, nc, CHUNK), lambda i: (0, i, 0)), perhead_in(T),
                   wide_in(du), wide_in(dw), wide_in(dqd), wide_in(dkd), perhead_in(daqk),
                   (dgl, (nc, 128), lambda i: (i, 0))],
                  [_orow(S, (3 * WIDTH,), F32, ts), _orow(S, (128,), F32, ts)])


def _gdn_prep_bwd(dqkv, dgb, cpre, z, conv_w, a128, dt128, dz, ts):
    S = z.shape[0]
    C3 = 3 * WIDTH
    hb = ts // 8
    n_tiles = S // ts

    def dpre(dq, c):
        y, dsil = _silu_and_grad(c)
        parts = []
        for h in range(GDN_HEADS):
            lo = h * GDN_DIM
            yq = y[:, lo:lo + GDN_DIM]
            rq = _l2_fwd(yq)
            nq = yq * rq
            dn = dq[:, lo:lo + GDN_DIM] * (GDN_DIM ** -0.5)
            parts.append(rq * (dn - nq * jnp.sum(dn * nq, axis=1, keepdims=True)))
        for h in range(GDN_HEADS):
            lo = WIDTH + h * GDN_DIM
            yk = y[:, lo:lo + GDN_DIM]
            rk = _l2_fwd(yk)
            nk = yk * rk
            dn = dq[:, lo:lo + GDN_DIM]
            parts.append(rk * (dn - nk * jnp.sum(dn * nk, axis=1, keepdims=True)))
        parts.append(dq[:, 2 * WIDTH:])
        return jnp.concatenate(parts, axis=1) * dsil

    def body(t, first, ins, outs, scratch):
        (dq_ref, dqn_ref, c_ref, cn_ref, x_ref, xp_ref, zs_ref, dgb_ref, w_ref, a_ref, dt_ref) = ins
        dx_ref, dzs_ref, dw_ref, dad_ref = outs

        @pl.when(first)
        def _():
            dw_ref[...] = jnp.zeros_like(dw_ref)
            dad_ref[...] = jnp.zeros_like(dad_ref)

        dc = dpre(dq_ref[...], c_ref[...])
        dcn = jnp.where(t < n_tiles - 1, dpre(dqn_ref[...], cn_ref[...]), 0.0)
        dce = jnp.concatenate([dc, dcn], axis=0)
        w = w_ref[...]
        dx = w[3:4, :] * dc
        for back in (1, 2, 3):
            dx = dx + w[3 - back:4 - back, :] * pltpu.roll(dce, ts + 8 - back, 0)[:ts, :]
        dx_ref[...] = _b16(dx)
        halo = jnp.where(t > 0, xp_ref[...], 0.0)
        xe = jnp.concatenate([halo, x_ref[...]], axis=0)
        dw_ref[3:4, :] += jnp.sum(dc * xe[8:, :], axis=0, keepdims=True)
        for back in (1, 2, 3):
            dw_ref[3 - back:4 - back, :] += jnp.sum(dc * pltpu.roll(xe, back, 0)[8:, :], axis=0,
                                                     keepdims=True)
        zs = zs_ref[...]
        dgb = dgb_ref[...]
        lane = lax.broadcasted_iota(jnp.int32, zs.shape, 1)
        arg = zs + dt_ref[...]
        nega = -jnp.exp(a_ref[...])
        dba = dgb * nega * _sigmoid(arg)
        beta = _sigmoid(zs)
        dbb = dgb * beta * (1.0 - beta)
        dzs_ref[...] = jnp.where((lane >= LANE_BA) & (lane < LANE_BB), dba,
                                 jnp.where((lane >= LANE_BB) & (lane < LANE_BB + 4), dbb, 0.0))
        dad_ref[0:1, :] += jnp.sum(dgb * nega * _softplus(arg), axis=0, keepdims=True)
        dad_ref[1:2, :] += jnp.sum(dba, axis=0, keepdims=True)

    nxt = lambda i: (jnp.minimum((i + 1) * hb, S // 8 - 1), 0)
    prv = lambda i: (jnp.maximum(i * hb - 1, 0), CB_BQKV)
    return _tiled("gdn_prep_bwd", body, n_tiles,
                  [_rows(dqkv, ts), (dqkv, (8, C3), nxt), _rows(cpre, ts), (cpre, (8, C3), nxt),
                   (z, (ts, C3), lambda i: (i, CB_BQKV)), (z, (8, C3), prv),
                   _cols(z, ts, 128, CB_SMALL), _rows(dgb, ts), _full(conv_w), _full(a128), _full(dt128)],
                  [((S, N_AL), BF16, (ts, C3), lambda i: (i, CB_BQKV)), _orow(S, (128,), F32, ts),
                   _oacc((8, C3), F32), _oacc((8, 128), F32)],
                  fill=(dz, 0))


def _mem_attn_fwd(z, mk, mv, ts):
    S = z.shape[0]

    def body(t, first, ins, outs, scratch):
        q_ref, mk_ref, mv_ref = ins
        (o_ref,) = outs
        for h in range(MEM_HEADS):
            cols = slice(h * MEM_DIM, (h + 1) * MEM_DIM)
            s = _dot(_b16(q_ref[:, cols]), _b16(mk_ref[:, cols]), NT) * (MEM_DIM ** -0.5)
            m = jnp.max(s, axis=1, keepdims=True)
            p = jnp.exp(s - m)
            p = p / jnp.sum(p, axis=1, keepdims=True)
            o_ref[:, cols] = _dot(_b16(p), _b16(mv_ref[:, cols]))

    (o,) = _tiled("mem_attn_fwd", body, S // ts, [_cols(z, ts, WIDTH, CB_MQ), _full(mk), _full(mv)],
                  [_orow(S, (WIDTH,), F32, ts)])
    return o


def _mem_attn_bwd(do, z, mk, mv, dz, ts):
    S = z.shape[0]
    M = mk.shape[0]

    def body(t, first, ins, outs, scratch):
        do_ref, q_ref, mk_ref, mv_ref = ins
        dq_ref, dmk_ref, dmv_ref = outs

        @pl.when(first)
        def _():
            dmk_ref[...] = jnp.zeros_like(dmk_ref)
            dmv_ref[...] = jnp.zeros_like(dmv_ref)

        scale = MEM_DIM ** -0.5
        for h in range(MEM_HEADS):
            cols = slice(h * MEM_DIM, (h + 1) * MEM_DIM)
            qb = _b16(q_ref[:, cols])
            kb = _b16(mk_ref[:, cols])
            dob = _b16(do_ref[:, cols])
            s = _dot(qb, kb, NT) * scale
            m = jnp.max(s, axis=1, keepdims=True)
            p = jnp.exp(s - m)
            p = p / jnp.sum(p, axis=1, keepdims=True)
            dmv_ref[:, cols] += _dot(_b16(p), dob, TN)
            dp = _dot(dob, _b16(mv_ref[:, cols]), NT)
            ds = p * (dp - jnp.sum(dp * p, axis=1, keepdims=True)) * scale
            dsb = _b16(ds)
            dq_ref[:, cols] = _b16(_dot(dsb, kb))
            dmk_ref[:, cols] += _dot(dsb, qb, TN)

    return _tiled("mem_attn_bwd", body, S // ts,
                  [_rows(do, ts), _cols(z, ts, WIDTH, CB_MQ), _full(mk), _full(mv)],
                  [((S, N_AL), BF16, (ts, WIDTH), lambda i: (i, CB_MQ)), _oacc((M, WIDTH), F32),
                   _oacc((M, WIDTH), F32)],
                  fill=(dz, 0))


def _head_norm(ob, g):
    xs, rs = [], []
    for h in range(GDN_HEADS):
        o = ob[:, h * GDN_DIM:(h + 1) * GDN_DIM]
        r = lax.rsqrt(jnp.mean(o * o, axis=1, keepdims=True) + EPS)
        xs.append(o * r)
        rs.append(r)
    return xs, rs


def _merge_fwd(x, z, o_a, o_b, o_m, gdn_g, b_merge, wb, wout, ts):
    S, D = x.shape

    def body(t, first, ins, outs, scratch):
        (x_ref, g_ref, oa_ref, az_ref, ob_ref, bz_ref, om_ref, mz_ref, gg_ref, bm_ref, wb_ref,
         wo_ref) = ins
        xo_ref, ya_ref, yb_ref, ym_ref, mg_ref = outs
        ya = oa_ref[...] * _silu_and_grad(az_ref[...])[0]
        xs, _ = _head_norm(ob_ref[...], None)
        nb = jnp.concatenate([xh * gg_ref[...] for xh in xs], axis=1)
        yb = nb * _silu_and_grad(bz_ref[...])[0]
        ym = om_ref[...] * _silu_and_grad(mz_ref[...])[0]
        merged = jnp.zeros((ts, D), F32)
        for n, (y, y_ref) in enumerate(((ya, ya_ref), (yb, yb_ref), (ym, ym_ref))):
            yb16 = _b16(y)
            y_ref[...] = yb16
            gate = _sigmoid(g_ref[:, n * D:(n + 1) * D] + bm_ref[:, n * D:(n + 1) * D])
            merged = merged + gate * _dot(yb16, wb_ref[n])
        mb = _b16(merged)
        mg_ref[...] = mb
        xo_ref[...] = x_ref[...] + _dot(mb, wo_ref[...])

    half = lambda a: _rows(a, ts)
    return _tiled("merge_fwd", body, S // ts,
                  [_rows(x, ts), _cols(z, ts, 3 * D, CB_GATES), half(o_a), _cols(z, ts, WIDTH, CB_AZ),
                   half(o_b), _cols(z, ts, WIDTH, CB_BZ), half(o_m), _cols(z, ts, WIDTH, CB_MZ),
                   _full(gdn_g.reshape(1, GDN_DIM)), _full(b_merge.reshape(1, 3 * D)), _full(wb), _full(wout)],
                  [_orow(S, (D,), F32, ts), _orow(S, (WIDTH,), BF16, ts), _orow(S, (WIDTH,), BF16, ts),
                   _orow(S, (WIDTH,), BF16, ts), _orow(S, (D,), BF16, ts)])


def _merge_bwd(dout, z, o_a, o_b, o_m, ya, yb, ym, gdn_g, b_merge, wb, wout, hsum, ts):
    S, D = dout.shape

    def body(t, first, ins, outs, scratch):
        (do_ref, g_ref, oa_ref, az_ref, ob_ref, bz_ref, om_ref, mz_ref, ya_ref, yb_ref, ym_ref,
         gg_ref, bm_ref, wb_ref, wo_ref, hs_ref) = ins
        (dg_ref, dpa_ref, dpb_ref, dpm_ref, doa_ref, dob_ref, dom_ref, dl_ref, dbm_ref, dgg_ref) = outs
        G3 = 3 * D

        @pl.when(first)
        def _():
            dbm_ref[...] = jnp.zeros_like(dbm_ref)
            dgg_ref[...] = jnp.zeros_like(dgg_ref)

        dmerged = _dot(_b16(do_ref[...]), wo_ref[...], NT)
        dys = []
        for n, (y_ref, dp_ref) in enumerate(((ya_ref, dpa_ref), (yb_ref, dpb_ref), (ym_ref, dpm_ref))):
            sl = slice(n * D, (n + 1) * D)
            gate = _sigmoid(g_ref[:, sl] + bm_ref[:, sl])
            proj = _dot(y_ref[...], wb_ref[n])
            dproj = _b16(gate * dmerged)
            dp_ref[...] = dproj
            dgp = dmerged * proj * gate * (1.0 - gate)
            dg_ref[:, sl] = dgp.astype(dg_ref.dtype)
            dbm_ref[0:1, sl] += jnp.sum(dgp, axis=0, keepdims=True)
            dys.append(_dot(dproj, wb_ref[n], NT))
        dya, dyb, dym = dys
        sa, dsa = _silu_and_grad(az_ref[...])
        oa = oa_ref[...]
        doa = dya * sa
        doa_ref[...] = doa
        dg_ref[:, G3:G3 + WIDTH] = _b16(dya * oa * dsa)
        dl_ref[...] = _dot(hs_ref[...], doa * oa, NT, HIGHEST)
        sm, dsm = _silu_and_grad(mz_ref[...])
        dom_ref[...] = dym * sm
        dg_ref[:, G3 + 2 * WIDTH:G3 + 3 * WIDTH] = _b16(dym * om_ref[...] * dsm)
        sb, dsb = _silu_and_grad(bz_ref[...])
        xs, rs = _head_norm(ob_ref[...], None)
        gg = gg_ref[...]
        dgg = jnp.zeros((1, GDN_DIM), F32)
        for h in range(GDN_HEADS):
            cols = slice(h * GDN_DIM, (h + 1) * GDN_DIM)
            dn = dyb[:, cols] * sb[:, cols]
            dg_ref[:, G3 + WIDTH + h * GDN_DIM:G3 + WIDTH + (h + 1) * GDN_DIM] = _b16(
                dyb[:, cols] * (xs[h] * gg) * dsb[:, cols])
            dgg = dgg + jnp.sum(dn * xs[h], axis=0, keepdims=True)
            dxh = dn * gg
            dob_ref[:, cols] = rs[h] * (dxh - xs[h] * jnp.mean(dxh * xs[h], axis=1, keepdims=True))
        dgg_ref[0:1, :] += dgg

    half = lambda a: _rows(a, ts)
    w512 = lambda dt: _orow(S, (WIDTH,), dt, ts)
    return _tiled("merge_bwd", body, S // ts,
                  [_rows(dout, ts), _cols(z, ts, 3 * D, CB_GATES), half(o_a), _cols(z, ts, WIDTH, CB_AZ),
                   half(o_b), _cols(z, ts, WIDTH, CB_BZ), half(o_m), _cols(z, ts, WIDTH, CB_MZ),
                   half(ya), half(yb), half(ym), _full(gdn_g.reshape(1, GDN_DIM)),
                   _full(b_merge.reshape(1, 3 * D)), _full(wb), _full(wout), _full(hsum)],
                  [((S, N_AL), BF16, (ts, 3 * D + 3 * WIDTH), lambda i: (i, CB_MERGE)),
                   _orow(S, (D,), BF16, ts), _orow(S, (D,), BF16, ts),
                   _orow(S, (D,), BF16, ts), w512(F32), w512(F32), w512(F32),
                   ((128, S), F32, (128, ts), lambda i: (0, i)), _oacc((8, 3 * D), F32),
                   _oacc((8, GDN_DIM), F32)])


def _to_aligned(w):
    parts = [w[..., lo:lo + n] for lo, n, _ in sorted(W_IN_PIECES, key=lambda p: p[2])]
    parts.append(jnp.zeros(w.shape[:-1] + (N_AL - N_IN,), w.dtype))
    return jnp.concatenate(parts, axis=-1)


def _from_aligned(w):
    return jnp.concatenate([w[..., al:al + n] for _, n, al in W_IN_PIECES], axis=-1)


def _lanes128(v, lane0):
    return jnp.pad(v.astype(F32)[None, :], ((0, 0), (lane0, 128 - lane0 - v.shape[0])))


def _tiles(S):
    ts = min(512, S // 2)
    return dict(ts=ts, ts_small=min(256, S // 2), tq=min(512, S // 4), tq_fwd=min(1024, S // 2))


def _layer_fwd(x, mem, p):
    S = x.shape[0]
    tl = _tiles(S)
    ts, tss, tq = tl["ts"], tl["ts_small"], tl["tq"]
    h, rstd = _rms_fwd("norm_fwd", x, p["norm_g"], ts)
    z = _mm("in_proj", h, p["w_in_al"], tn=1664)

    b_fg128 = _lanes128(p["b_fg"], LANE_AF)
    fT, f_hi, f_mid, f_lo = _fox_decay(z, b_fg128, ts)
    aq = z[:, CB_AQ * WIDTH:(CB_AQ + 1) * WIDTH]
    ak = z[:, CB_AK * WIDTH:(CB_AK + 1) * WIDTH]
    av = z[:, CB_AV * WIDTH:(CB_AV + 1) * WIDTH]
    q32 = _heads_major(aq, FOX_HEADS, FOX_DIM)
    kh = _heads_major(ak, FOX_HEADS, FOX_DIM).astype(BF16)
    vh = _heads_major(av, FOX_HEADS, FOX_DIM).astype(BF16)
    khT = kh.transpose(0, 2, 1)
    piecesT = jnp.stack([f[:FOX_HEADS] for f in (f_hi, f_mid, f_lo)], axis=1)
    pieces = piecesT.transpose(0, 2, 1)
    ones3 = jnp.ones((FOX_HEADS, S, 3), BF16)
    padk = jnp.zeros((FOX_HEADS, S, FOX_AUG - FOX_DIM - 6), BF16)
    q_aug = jnp.concatenate([q32, pieces.astype(F32), ones3.astype(F32), padk.astype(F32)], axis=-1)
    k_aug = jnp.concatenate([kh, ones3, -pieces, padk], axis=-1)
    kT_aug = jnp.concatenate([khT, ones3.transpose(0, 2, 1), -piecesT, padk.transpose(0, 2, 1)], axis=1)
    v_aug = jnp.concatenate([vh, ones3[:, :, :1], jnp.zeros((FOX_HEADS, S, 128 - FOX_DIM - 1), BF16)],
                            axis=-1)
    o_h, lse, qs = _fox_fwd(q_aug, kT_aug, v_aug, tl["tq_fwd"])
    o_a = _heads_minor(o_h)

    a128 = _lanes128(p["a_log"], LANE_BA)
    dt128 = _lanes128(p["dt_bias"], LANE_BA)
    qkv, cpre, gb, gbT = _gdn_prep(z, p["conv_w"], a128, dt128, ts)
    grow = gbT[LANE_BA:LANE_BA + GDN_HEADS].reshape(GDN_HEADS, S // CHUNK, CHUNK)
    u, w, qd, kd, aqk, T = _gdn_local_fwd(qkv, gb, grow, ts)
    o_b, vn, states = _gdn_scan_fwd(u, w, qd, kd, aqk, gb, ts)

    mem_h, mem_r = _rms_fwd("mem_norm_fwd", mem, p["mem_norm_g"], mem.shape[0])
    mkv = _mm("mem_kv", mem_h, p["w_mem_kv"])
    mk, mv = mkv[:, :WIDTH], mkv[:, WIDTH:]
    o_m = _mem_attn_fwd(z, mk, mv, ts)

    x_next, ya, yb, ym, merged = _merge_fwd(x, z, o_a, o_b, o_m, p["gdn_norm_g"], p["b_merge"],
                                            p["w_branch"], p["w_out"], ts)
    saved = dict(x=x, h=h, rstd=rstd, z=z, b_fg128=b_fg128, qs=qs, k_aug=k_aug, khT=khT, vh=vh, lse=lse, o_a=o_a, a128=a128, dt128=dt128, qkv=qkv, cpre=cpre, gb=gb,
                 grow=grow, w=w, qd=qd, kd=kd, aqk=aqk, T=T, o_b=o_b, vn=vn, states=states,
                 mem_h=mem_h, mem_r=mem_r, mk=mk, mv=mv, o_m=o_m, ya=ya, yb=yb, ym=ym, merged=merged)
    return x_next, saved


def _layer_bwd(dout, mem, p, s):
    S = dout.shape[0]
    tl = _tiles(S)
    ts, tss, tq = tl["ts"], tl["ts_small"], tl["tq"]
    z = s["z"]
    hsum = (jnp.arange(128)[:, None] == jnp.arange(WIDTH)[None, :] // FOX_DIM).astype(F32)
    (dz, dpa, dpb, dpm, do_a, do_b, do_m, deltaT, db_merge, dgdn_g) = _merge_bwd(
        dout, z, s["o_a"], s["o_b"], s["o_m"], s["ya"], s["yb"], s["ym"], p["gdn_norm_g"],
        p["b_merge"], p["w_branch"], p["w_out"], hsum, tss)
    g = {}
    g["b_merge"] = db_merge[0]
    g["gdn_norm_g"] = dgdn_g[0]
    g["w_out"] = _mm("dw_out", s["merged"], dout, ta=True)
    g["w_branch"] = jnp.stack([_mm("dw_branch", y, dp, ta=True)
                               for y, dp in ((s["ya"], dpa), (s["yb"], dpb), (s["ym"], dpm))])

    do_h = _heads_major(do_a, FOX_HEADS, FOX_DIM).astype(BF16)
    delta_row = deltaT[:FOX_HEADS, None, :]
    dqT, dk_h, dv_h, dfk, dfq = _fox_bwd(s["qs"], s["k_aug"], s["khT"], s["vh"], do_h, s["lse"],
                                         delta_row, tq)
    daq = _heads_minor(dqT.transpose(0, 2, 1))
    dak = _heads_minor(dk_h)
    dav = _heads_minor(dv_h)
    daf128, db_fg = _fox_decay_bwd(dfk[:, 0, :], dfq[:, 0, :], z, s["b_fg128"], ts)
    g["b_fg"] = db_fg[:FOX_HEADS]

    du, dw, dqd, dkd, daqk, dgl = _gdn_scan_bwd(do_b, s["w"], s["qd"], s["kd"], s["aqk"], s["vn"],
                                                s["states"], s["gb"], ts)
    dqkv, dgb = _gdn_local_bwd(s["qkv"], s["gb"], s["grow"], s["T"], du, dw, dqd, dkd, daqk, dgl, ts)
    dz, dzs_b, dconv, dad = _gdn_prep_bwd(dqkv, dgb, s["cpre"], z, p["conv_w"], s["a128"],
                                          s["dt128"], dz, ts)
    g["conv_w"] = dconv[:4]
    g["a_log"] = dad[0, LANE_BA:LANE_BA + GDN_HEADS]
    g["dt_bias"] = dad[1, LANE_BA:LANE_BA + GDN_HEADS]

    dz, dmk, dmv = _mem_attn_bwd(do_m, z, s["mk"], s["mv"], dz, ts)
    dmkv = jnp.concatenate([dmk, dmv], axis=1)
    g["w_mem_kv"] = _mm("dw_mem_kv", s["mem_h"], dmkv, ta=True)
    dmem_h = _mm("dmem_h", dmkv, p["w_mem_kv"], tb=True)
    M = mem.shape[0]
    _, g["mem_norm_g"] = _rms_bwd("mem_norm_bwd", dmem_h, mem, s["mem_r"], p["mem_norm_g"],
                                  jnp.zeros_like(mem), M)

    lane = jnp.arange(128)[None, :]
    dsmall = jnp.where(lane < 8, daf128, dzs_b)
    daqkv = jnp.concatenate([_b16(daq), _b16(dak), _b16(dav)], axis=1)
    dz = lax.dynamic_update_slice(dz, daqkv, (0, CB_AQKV * 3 * WIDTH))
    dz = lax.dynamic_update_slice(dz, _b16(dsmall), (0, CB_SMALL * 128))
    g["w_in_al"] = _mm("dw_in", s["h"], dz, ta=True, tn=1664)
    dh = _mm("dh", dz, p["w_in_al"], tb=True, tk=1664)
    dx, g["norm_g"] = _rms_bwd("norm_bwd", dh, s["x"], s["rstd"], p["norm_g"], dout, ts)
    return dx, g


def _local_step(x, mem, layers, final_norm_g, loss_target):
    S = x.shape[0]
    saves = []
    cur = x
    for p in layers:
        cur, sv = _layer_fwd(cur, mem, p)
        saves.append(sv)
    dx, dgf, loss_lanes = _loss_head(cur, final_norm_g, loss_target, _tiles(S)["ts"])
    grads = [None] * len(layers)
    for l in reversed(range(len(layers))):
        dx, grads[l] = _layer_bwd(dx, mem, layers[l], saves[l])
    return loss_lanes, dx, grads, dgf


HBM_SPEC = pl.BlockSpec(memory_space=pltpu.HBM)


def _mesh_pos():
    return lax.axis_index("x"), lax.axis_index("y"), lax.axis_index("c")


def _comm_call(name, body, arrays, out_shapes, n_remote, n_local):
    n = len(arrays)

    def kern(*refs):
        body(refs[:n], refs[n:2 * n], refs[2 * n], refs[2 * n + 1], refs[2 * n + 2])

    return pl.pallas_call(
        kern, name=name, out_shape=out_shapes, in_specs=[HBM_SPEC] * n, out_specs=[HBM_SPEC] * n,
        scratch_shapes=[pltpu.SemaphoreType.DMA((n_remote,)), pltpu.SemaphoreType.DMA((n_remote,)),
                        pltpu.SemaphoreType.DMA((max(n_local, 1),))],
    )(*arrays)


def _remote(src, dst, send_sems, recv_sems, k, to):
    return pltpu.make_async_remote_copy(src_ref=src, dst_ref=dst, send_sem=send_sems.at[k],
                                        recv_sem=recv_sems.at[k], device_id=to, device_id_type=MESH_ID)


def _other_chips(mx, my):
    return [(1 - mx, my), (mx, 1 - my), (1 - mx, 1 - my)]


def _gather_chips(name, shards):
    n = len(shards)

    def body(ins, outs, send_sems, recv_sems, local_sems):
        mx, my, mc = _mesh_pos()
        me = 2 * mx + my
        sibling = (mx, my, 1 - mc)
        chips = _other_chips(mx, my)
        sends = []
        for a in range(n):
            for k, (px, py) in enumerate(chips):
                cp = _remote(ins[a].at[mc], outs[a].at[me, mc], send_sems, recv_sems, 6 * a + k,
                             (px, py, mc))
                cp.start()
                sends.append(cp)
        for a in range(n):
            for k, (px, py) in enumerate(chips):
                j = 2 * px + py
                _remote(ins[a].at[mc], outs[a].at[j, mc], send_sems, recv_sems, 6 * a + k,
                        (px, py, mc)).wait_recv()
                cp = _remote(outs[a].at[j, mc], outs[a].at[j, mc], send_sems, recv_sems, 6 * a + 3 + k,
                             sibling)
                cp.start()
                sends.append(cp)
        for a in range(n):
            for k, (px, py) in enumerate(chips):
                j = 2 * px + py
                _remote(outs[a].at[j, 1 - mc], outs[a].at[j, 1 - mc], send_sems, recv_sems,
                        6 * a + 3 + k, sibling).wait_recv()
        for cp in sends:
            cp.wait_send()

    shapes = [jax.ShapeDtypeStruct((N_CHIPS,) + s.shape, s.dtype) for s in shards]
    outs = _comm_call(name, body, shards, shapes, 6 * n, 0)
    me = 2 * lax.axis_index("x") + lax.axis_index("y")
    return [lax.dynamic_update_index_in_dim(o, s, me, 0) for o, s in zip(outs, shards)]


def _sibling_swap(gs):
    n = len(gs)

    def body(ins, outs, send_sems, recv_sems, local_sems):
        mx, my, mc = _mesh_pos()
        sends = []
        for a in range(n):
            cp = _remote(ins[a].at[:, 1 - mc], outs[a], send_sems, recv_sems, a, (mx, my, 1 - mc))
            cp.start()
            sends.append(cp)
        for cp in sends:
            cp.wait()

    shapes = [jax.ShapeDtypeStruct((g.shape[0],) + g.shape[2:], g.dtype) for g in gs]
    return _comm_call("grad_sibling_swap", body, gs, shapes, n, 0)


def _chip_exchange(ps):
    n = len(ps)

    def body(ins, outs, send_sems, recv_sems, local_sems):
        mx, my, mc = _mesh_pos()
        me = 2 * mx + my
        chips = _other_chips(mx, my)
        sends = []
        for a in range(n):
            for k, (px, py) in enumerate(chips):
                cp = _remote(ins[a].at[2 * px + py], outs[a].at[me], send_sems, recv_sems, 3 * a + k,
                             (px, py, mc))
                cp.start()
                sends.append(cp)
        for a in range(n):
            for k, (px, py) in enumerate(chips):
                _remote(ins[a].at[me], outs[a].at[2 * px + py], send_sems, recv_sems, 3 * a + k,
                        (px, py, mc)).wait_recv()
        for cp in sends:
            cp.wait_send()

    shapes = [jax.ShapeDtypeStruct(p.shape, p.dtype) for p in ps]
    outs = _comm_call("grad_chip_exchange", body, ps, shapes, 3 * n, 0)
    me = 2 * lax.axis_index("x") + lax.axis_index("y")
    return [lax.dynamic_update_index_in_dim(o, lax.dynamic_index_in_dim(p, me, 0, keepdims=False), me, 0)
            for o, p in zip(outs, ps)]


def _sibling_gather(hs):
    n = len(hs)

    def body(ins, outs, send_sems, recv_sems, local_sems):
        mx, my, mc = _mesh_pos()
        sends = []
        for a in range(n):
            cp = _remote(ins[a], outs[a], send_sems, recv_sems, a, (mx, my, 1 - mc))
            cp.start()
            sends.append(cp)
        for cp in sends:
            cp.wait()

    shapes = [jax.ShapeDtypeStruct(h.shape, h.dtype) for h in hs]
    theirs = _comm_call("grad_sibling_gather", body, hs, shapes, n, 0)
    first = lax.axis_index("c") == 0
    return [jnp.stack([jnp.where(first, h, t), jnp.where(first, t, h)]) for h, t in zip(hs, theirs)]


def _add_pairs(a, b, tr, out_dtype):
    n, H, C = a.shape

    def kern(a_ref, b_ref, o_ref):
        o_ref[...] = (a_ref[...] + b_ref[...]).astype(o_ref.dtype)

    spec = pl.BlockSpec((None, tr, C), lambda j, i: (j, i, 0))
    return pl.pallas_call(
        kern, name="grad_pair_sum", grid=(n, H // tr), in_specs=[spec, spec], out_specs=spec,
        out_shape=jax.ShapeDtypeStruct((n, H, C), out_dtype),
        compiler_params=_params(("parallel", "parallel")),
    )(a, b)


def _sum_slots(r4, tr):
    n, H, C = r4.shape

    def kern(r_ref, o_ref):
        f = lambda k: r_ref[k].astype(F32)
        o_ref[...] = ((f(0) + f(1)) + f(2)) + f(3)

    return pl.pallas_call(
        kern, name="grad_chip_sum", grid=(H // tr,),
        in_specs=[pl.BlockSpec((n, tr, C), lambda i: (0, i, 0))],
        out_specs=pl.BlockSpec((tr, C), lambda i: (i, 0)),
        out_shape=jax.ShapeDtypeStruct((H, C), F32),
        compiler_params=_params(("parallel",)),
    )(r4)


def _adamw(w, g, m, v, tr):
    R, C = w.shape
    c1 = 1.0 - ADAM_B1
    c2 = 1.0 - ADAM_B2
    bc1 = 1.0 - ADAM_B1 ** ADAM_STEP
    bc2 = 1.0 - ADAM_B2 ** ADAM_STEP

    def kern(w_ref, g_ref, m_ref, v_ref, d_ref, mo_ref, vo_ref):
        gv = g_ref[...]
        mn = ADAM_B1 * m_ref[...] + c1 * gv
        vn = ADAM_B2 * v_ref[...] + c2 * (gv * gv)
        m_hat = mn / bc1
        v_hat = vn / bc2
        d_ref[...] = -ADAM_LR * (m_hat / (jnp.sqrt(v_hat) + ADAM_EPS) + ADAM_WD * w_ref[...])
        mo_ref[...] = mn
        vo_ref[...] = vn

    spec = pl.BlockSpec((tr, C), lambda i: (i, 0))
    shape = jax.ShapeDtypeStruct((R, C), F32)
    return pl.pallas_call(
        kern, name="adamw", grid=(R // tr,), in_specs=[spec] * 4, out_specs=[spec] * 3,
        out_shape=[shape] * 3, compiler_params=_params(("parallel",)),
    )(w, g, m, v)


PACK_COLS = 1024
PACK_ROWS = 512
W_SHARD = N_IN // N_CHIPS
SLAB = ("conv_w", "w_mem_kv", "w_branch", "w_out")
SMALL =("norm_g", "b_fg", "b_merge", "a_log", "dt_bias", "gdn_norm_g", "mem_norm_g", "final_norm_g")
ALL_WEIGHTS = ("norm_g", "w_in", "b_fg", "b_merge", "conv_w", "a_log", "dt_bias", "gdn_norm_g",
               "mem_norm_g", "w_mem_kv", "w_branch", "w_out", "final_norm_g")
SHARD_AXIS = {"w_in": 2, "conv_w": 2, "w_mem_kv": 1, "w_branch": 3, "w_out": 1}


def _pack(arrays, row_multiple):
    flat = jnp.concatenate([a.reshape(-1) for a in arrays])
    n = flat.shape[0]
    rows = -(-n // PACK_COLS)
    rows = -(-rows // row_multiple) * row_multiple
    flat = jnp.pad(flat, (0, rows * PACK_COLS - n))
    return flat.reshape(rows, PACK_COLS)


def _unpack(slab, shapes):
    out, off = [], 0
    for shp in shapes:
        n = 1
        for d in shp:
            n *= d
        r0, r1 = off // PACK_COLS, -(-(off + n) // PACK_COLS)
        rows = slab[r0:r1].reshape(-1)
        out.append(rows[off - r0 * PACK_COLS:off - r0 * PACK_COLS + n].reshape(shp))
        off += n
    return out


def _shard_of(full, name, j):
    ax = SHARD_AXIS[name]
    n = full.shape[ax] // N_CHIPS
    return lax.slice_in_dim(full, j * n, (j + 1) * n, axis=ax)


def _aligned_from_shards(shards):
    def cols(lo, n):
        parts = []
        while n > 0:
            j, off = divmod(lo, W_SHARD)
            take = min(n, W_SHARD - off)
            parts.append(shards[j][..., off:off + take])
            lo, n = lo + take, n - take
        return parts

    out = []
    for lo, n, _ in sorted(W_IN_PIECES, key=lambda p: p[2]):
        out += cols(lo, n)
    out.append(jnp.zeros(shards[0].shape[:-1] + (N_AL - N_IN,), shards[0].dtype))
    return jnp.concatenate(out, axis=-1)


def _shard_from_aligned(w_al, j):
    lo_j, hi_j = j * W_SHARD, (j + 1) * W_SHARD
    parts = []
    for lo, n, al in W_IN_PIECES:
        a, b = max(lo, lo_j), min(lo + n, hi_j)
        if a < b:
            parts.append(w_al[..., al + a - lo:al + b - lo])
    return jnp.concatenate(parts, axis=-1)


def kernel(x, mem, norm_g, w_in, b_fg, b_merge, conv_w, a_log, dt_bias, gdn_norm_g, mem_norm_g, w_mem_kv, w_branch, w_out, final_norm_g, loss_target, m_norm_g, m_w_in, m_b_fg, m_b_merge, m_conv_w, m_a_log, m_dt_bias, m_gdn_norm_g, m_mem_norm_g, m_w_mem_kv, m_w_branch, m_w_out, m_final_norm_g, v_norm_g, v_w_in, v_b_fg, v_b_merge, v_conv_w, v_a_log, v_dt_bias, v_gdn_norm_g, v_mem_norm_g, v_w_mem_kv, v_w_branch, v_w_out, v_final_norm_g):
    wts = dict(norm_g=norm_g, w_in=w_in, b_fg=b_fg, b_merge=b_merge, conv_w=conv_w, a_log=a_log,
               dt_bias=dt_bias, gdn_norm_g=gdn_norm_g, mem_norm_g=mem_norm_g, w_mem_kv=w_mem_kv,
               w_branch=w_branch, w_out=w_out, final_norm_g=final_norm_g)
    mom = dict(norm_g=m_norm_g, w_in=m_w_in, b_fg=m_b_fg, b_merge=m_b_merge, conv_w=m_conv_w,
               a_log=m_a_log, dt_bias=m_dt_bias, gdn_norm_g=m_gdn_norm_g, mem_norm_g=m_mem_norm_g,
               w_mem_kv=m_w_mem_kv, w_branch=m_w_branch, w_out=m_w_out, final_norm_g=m_final_norm_g)
    vel = dict(norm_g=v_norm_g, w_in=v_w_in, b_fg=v_b_fg, b_merge=v_b_merge, conv_w=v_conv_w,
               a_log=v_a_log, dt_bias=v_dt_bias, gdn_norm_g=v_gdn_norm_g, mem_norm_g=v_mem_norm_g,
               w_mem_kv=v_w_mem_kv, w_branch=v_w_branch, w_out=v_w_out, final_norm_g=v_final_norm_g)

    big = ("w_in", "w_mem_kv", "w_branch", "w_out")
    gathered = _gather_chips("weight_gather", [wts[n].astype(BF16) for n in big] + [conv_w])
    all_w = dict(zip(big + ("conv_w",), gathered))
    w_in_al = _aligned_from_shards([all_w["w_in"][j] for j in range(N_CHIPS)])

    layers = []
    for l in range(DEPTH):
        rows_of = lambda n: all_w[n][:, l].reshape(D_MODEL, D_MODEL)
        last_of = lambda n: jnp.concatenate([all_w[n][j, l] for j in range(N_CHIPS)], axis=-1)
        layers.append(dict(norm_g=norm_g[l], w_in_al=w_in_al[l], b_fg=b_fg[l], b_merge=b_merge[l],
                           conv_w=jnp.pad(last_of("conv_w"), ((0, 4), (0, 0))), a_log=a_log[l],
                           dt_bias=dt_bias[l], gdn_norm_g=gdn_norm_g[l], mem_norm_g=mem_norm_g[l],
                           w_mem_kv=rows_of("w_mem_kv"), w_branch=last_of("w_branch"),
                           w_out=rows_of("w_out")))

    loss_lanes, dx, grads, dgf = _local_step(x[0], mem[0], layers, final_norm_g, loss_target[0])

    gfull = {n: jnp.stack([grads[l][n] for l in range(DEPTH)])
             for n in ("norm_g", "b_fg", "b_merge", "conv_w", "a_log", "dt_bias", "gdn_norm_g",
                       "mem_norm_g", "w_mem_kv", "w_branch", "w_out")}
    gfull["final_norm_g"] = dgf
    loss_local = jnp.sum(loss_lanes).reshape(1)
    small_g = [gfull[n] for n in SMALL] + [loss_local]
    dw_al = jnp.stack([grads[l]["w_in_al"] for l in range(DEPTH)])
    ga = jnp.stack([_shard_from_aligned(dw_al, j) for j in range(N_CHIPS)])
    gb = jnp.stack([_pack([_shard_of(gfull[n], n, j) for n in SLAB] + small_g, PACK_ROWS)
                    for j in range(N_CHIPS)])
    R = gb.shape[1]
    gb = gb.reshape(N_CHIPS, 2, R // 2, PACK_COLS)

    mc = lax.axis_index("c")
    tr = 256
    from_sibling = _sibling_swap([ga, gb])
    mine = [lax.dynamic_index_in_dim(g, mc, axis=1, keepdims=False) for g in (ga, gb)]
    pair = [_add_pairs(a, b, tr, dt) for a, b, dt in zip(mine, from_sibling, (BF16, F32))]
    slots = _chip_exchange(pair)
    half = [_sum_slots(s, tr) for s in slots]
    ga_sum, gb_sum = _sibling_gather(half)
    gb_sum = gb_sum.reshape(R, PACK_COLS)

    names = list(SLAB) + list(SMALL)
    shapes = [wts[n].shape for n in names] + [(1,)]
    g_un = dict(zip(names + ["loss"], _unpack(gb_sum, shapes)))
    g_un["w_in"] = ga_sum
    d_un, m_un, v_un = {}, {}, {}
    rows2d = lambda a: a.reshape(-1, a.shape[-1])
    for n in ("w_in", "w_mem_kv", "w_branch", "w_out"):
        res = _adamw(rows2d(wts[n]), rows2d(g_un[n]), rows2d(mom[n]), rows2d(vel[n]), tr)
        d_un[n], m_un[n], v_un[n] = [r.reshape(wts[n].shape) for r in res]
    little = ("conv_w",) + SMALL
    slab = lambda d: _pack([d[n] for n in little], 8)
    res = _adamw(slab(wts), slab(g_un), slab(mom), slab(vel), 8)
    little_shapes = [wts[n].shape for n in little]
    for out, r in zip((d_un, m_un, v_un), res):
        out.update(zip(little, _unpack(r, little_shapes)))

    loss = g_un["loss"][0]
    return (loss, dx[None], *[g_un[n] for n in ALL_WEIGHTS], *[d_un[n] for n in ALL_WEIGHTS],
            *[m_un[n] for n in ALL_WEIGHTS], *[v_un[n] for n in ALL_WEIGHTS])
```

```python
import functools

import jax
import jax.numpy as jnp
from jax import lax
from jax.experimental import pallas as pl
from jax.experimental.pallas import tpu as pltpu

F32 = jnp.float32
BF16 = jnp.bfloat16
HIGHEST = lax.Precision.HIGHEST
PREC_UT = lax.Precision.HIGH
MESH_ID = pl.DeviceIdType.MESH

D_MODEL = 1024
DEPTH = 2
CHUNK = 64
EPS = 1e-6
FOX_HEADS, FOX_DIM = 8, 64
GDN_HEADS, GDN_DIM = 4, 128
MEM_HEADS, MEM_DIM = 4, 128
WIDTH = 512
N_BRANCH = 3
N_IN = 8208
N_AL = 8320
N_CHIPS = 4
NEG = -1e30
LOG2E = 1.4426950408889634
LN2 = 0.6931471805599453

ADAM_LR, ADAM_B1, ADAM_B2, ADAM_EPS, ADAM_WD, ADAM_STEP = 0.001, 0.9, 0.999, 1e-08, 0.01, 10

CB_GATES = 0
CB_AZ, CB_BZ, CB_MZ = 6, 7, 8
CB_MERGE = 0
CB_BQKV = 3
CB_AQ, CB_AK, CB_AV = 12, 13, 14
CB_AQKV = 4
CB_MQ = 15
CB_SMALL = 64
W_IN_PIECES = ((0, 512, 6144), (512, 512, 6656), (1024, 512, 7168), (1536, 8, 8192), (1544, 512, 3072),
               (2056, 512, 4608), (2568, 512, 5120), (3080, 512, 5632), (3592, 4, 8200), (3596, 4, 8204),
               (3600, 512, 3584), (4112, 512, 7680), (4624, 512, 4096), (5136, 3072, 0))
LANE_AF, LANE_BA, LANE_BB = 0, 8, 12

NN = ((1,), (0,))
NT = ((1,), (1,))
TN = ((0,), (0,))

VMEM_LIMIT_BYTES = 56 * 1024 * 1024


def _dot(a, b, dims=NN, prec=None):
    return lax.dot_general(a, b, (dims, ((), ())), preferred_element_type=F32, precision=prec)


def _bdot(a, b, ca, cb, prec=None):
    return lax.dot_general(a, b, (((ca,), (cb,)), ((0,), (0,))), preferred_element_type=F32,
                           precision=prec)


def _b16(a):
    return a.astype(BF16)


def _eye(n, dtype=F32):
    r = lax.broadcasted_iota(jnp.int32, (n, n), 0)
    c = lax.broadcasted_iota(jnp.int32, (n, n), 1)
    return jnp.where(r == c, 1.0, 0.0).astype(dtype)


def _transpose_exact(x):
    return _dot(_eye(x.shape[1]), x, NT, HIGHEST)


def _col_to_row(col):
    n = col.shape[0]
    return jnp.sum(jnp.where(_eye(n) > 0.5, col, 0.0), axis=0, keepdims=True)


def _row_to_col(row):
    n = row.shape[1]
    return jnp.sum(jnp.where(_eye(n) > 0.5, row, 0.0), axis=1, keepdims=True)


def _sigmoid(x):
    return 1.0 / (1.0 + jnp.exp(-x))


def _softplus(x):
    return jnp.maximum(x, 0.0) + jnp.log(1.0 + jnp.exp(-jnp.abs(x)))


def _silu_and_grad(x):
    s = _sigmoid(x)
    return x * s, s * (1.0 + x * (1.0 - s))


def _params(semantics):
    return pltpu.CompilerParams(dimension_semantics=semantics, vmem_limit_bytes=VMEM_LIMIT_BYTES)


def _rows(a, ts):
    nd = a.ndim
    return (a, (ts,) + a.shape[1:], lambda i, nd=nd: (i,) + (0,) * (nd - 1))


def _cols(a, ts, width, cb):
    return (a, (ts, width), lambda i, cb=cb: (i, cb))


def _full(a):
    nd = a.ndim
    return (a, a.shape, lambda i, nd=nd: (0,) * nd)


def _orow(S, tail, dtype, ts):
    nd = 1 + len(tail)
    return ((S,) + tuple(tail), dtype, (ts,) + tuple(tail), lambda i, nd=nd: (i,) + (0,) * (nd - 1))


def _oacc(shape, dtype):
    nd = len(shape)
    return (tuple(shape), dtype, tuple(shape), lambda i, nd=nd: (0,) * nd)


def _tiled(name, body, n_steps, ins, outs, scratch=(), reverse=False, fill=None):
    def rev(imap):
        if not reverse:
            return imap
        return lambda i: imap(n_steps - 1 - i)

    in_specs = [pl.BlockSpec(blk, rev(imap)) for (_, blk, imap) in ins]
    out_specs = [pl.BlockSpec(blk, rev(imap)) for (_, _, blk, imap) in outs]
    out_shape = [jax.ShapeDtypeStruct(shape, dt) for (shape, dt, _, _) in outs]
    n_in, n_out = len(ins), len(outs)
    arrays = [a for (a, _, _) in ins]
    aliases = {}
    n_extra = 0
    if fill is not None:
        arrays.append(fill[0])
        in_specs.append(pl.BlockSpec(memory_space=pl.ANY))
        aliases = {n_in: fill[1]}
        n_extra = 1

    def kern(*refs):
        step = pl.program_id(0)
        t = (n_steps - 1 - step) if reverse else step
        lo = n_in + n_extra
        body(t, step == 0, refs[:n_in], refs[lo:lo + n_out], refs[lo + n_out:])

    res = pl.pallas_call(
        kern, name=name, grid=(n_steps,), in_specs=in_specs, out_specs=out_specs,
        out_shape=out_shape, scratch_shapes=list(scratch), input_output_aliases=aliases,
        compiler_params=_params(("arbitrary",)),
    )(*arrays)
    return res


def _pick(n, pref):
    if n <= pref:
        return n
    best = None
    for t in range(128, pref + 1, 128):
        if n % t == 0:
            best = t
    assert best is not None, (n, pref)
    return best


def _mm(name, a, b, ta=False, tb=False, out_dtype=F32, tm=1024, tn=1024, tk=1024):
    if ta:
        K, M = a.shape
    else:
        M, K = a.shape
    if tb:
        N, K2 = b.shape
    else:
        K2, N = b.shape
    assert K == K2, (a.shape, b.shape, ta, tb)
    tm, tn, tk = _pick(M, tm), _pick(N, tn), _pick(K, tk)
    nk = K // tk
    a_spec = (pl.BlockSpec((tk, tm), lambda i, j, k: (k, i)) if ta
              else pl.BlockSpec((tm, tk), lambda i, j, k: (i, k)))
    b_spec = (pl.BlockSpec((tn, tk), lambda i, j, k: (j, k)) if tb
              else pl.BlockSpec((tk, tn), lambda i, j, k: (k, j)))
    dims = ((0,) if ta else (1,), (1,) if tb else (0,))

    def kern_single(a_ref, b_ref, o_ref):
        o_ref[...] = _dot(_b16(a_ref[...]), _b16(b_ref[...]), dims).astype(o_ref.dtype)

    def kern_acc(a_ref, b_ref, o_ref, acc_ref):
        k = pl.program_id(2)

        @pl.when(k == 0)
        def _():
            acc_ref[...] = jnp.zeros_like(acc_ref)

        acc_ref[...] += _dot(_b16(a_ref[...]), _b16(b_ref[...]), dims)

        @pl.when(k == nk - 1)
        def _():
            o_ref[...] = acc_ref[...].astype(o_ref.dtype)

    return pl.pallas_call(
        kern_single if nk == 1 else kern_acc, name=name, grid=(M // tm, N // tn, nk),
        in_specs=[a_spec, b_spec],
        out_specs=pl.BlockSpec((tm, tn), lambda i, j, k: (i, j)),
        out_shape=jax.ShapeDtypeStruct((M, N), out_dtype),
        scratch_shapes=[] if nk == 1 else [pltpu.VMEM((tm, tn), F32)],
        compiler_params=_params(("parallel", "parallel", "arbitrary")),
    )(a, b)


def _rms_fwd(name, x, g, ts):
    S, D = x.shape

    def body(t, first, ins, outs, scratch):
        x_ref, g_ref = ins
        h_ref, r_ref = outs
        xv = x_ref[...]
        r = lax.rsqrt(jnp.mean(xv * xv, axis=1, keepdims=True) + EPS)
        h_ref[...] = (xv * r * g_ref[...]).astype(h_ref.dtype)
        r_ref[...] = r

    return _tiled(name, body, S // ts, [_rows(x, ts), _full(g.reshape(1, D))],
                  [_orow(S, (D,), BF16, ts), _orow(S, (1,), F32, ts)])


def _rms_bwd(name, dh, x, rstd, g, dres, ts):
    S, D = x.shape

    def body(t, first, ins, outs, scratch):
        dh_ref, x_ref, r_ref, g_ref, dres_ref = ins
        dx_ref, dg_ref = outs
        r = r_ref[...]
        xh = x_ref[...] * r
        dhv = dh_ref[...]
        dxh = dhv * g_ref[...]
        dx_ref[...] = dres_ref[...] + r * (dxh - xh * jnp.mean(dxh * xh, axis=1, keepdims=True))

        @pl.when(first)
        def _():
            dg_ref[...] = jnp.zeros_like(dg_ref)

        dg_ref[0:1, :] += jnp.sum(dhv * xh, axis=0, keepdims=True)

    dx, dg = _tiled(name, body, S // ts,
                    [_rows(dh, ts), _rows(x, ts), _rows(rstd, ts), _full(g.reshape(1, D)), _rows(dres, ts)],
                    [_orow(S, (D,), F32, ts), _oacc((8, D), F32)])
    return dx, dg[0]


def _loss_head(x, g, target, ts):
    S, D = x.shape

    def body(t, first, ins, outs, scratch):
        x_ref, g_ref, tgt_ref = ins
        dx_ref, dg_ref, loss_ref = outs
        xv = x_ref[...]
        gv = g_ref[...]
        r = lax.rsqrt(jnp.mean(xv * xv, axis=1, keepdims=True) + EPS)
        xh = xv * r
        err = xh * gv - tgt_ref[...]
        dy = err * (1.0 / D)
        dxh = dy * gv
        dx_ref[...] = r * (dxh - xh * jnp.mean(dxh * xh, axis=1, keepdims=True))

        @pl.when(first)
        def _():
            dg_ref[...] = jnp.zeros_like(dg_ref)
            loss_ref[...] = jnp.zeros_like(loss_ref)

        dg_ref[0:1, :] += jnp.sum(dy * xh, axis=0, keepdims=True)
        per_lane = jnp.sum(err * err, axis=0, keepdims=True)
        loss_ref[0:1, :] += per_lane * (0.5 / D)

    dx, dg, loss = _tiled("loss_head", body, S // ts,
                          [_rows(x, ts), _full(g.reshape(1, D)), _rows(target, ts)],
                          [_orow(S, (D,), F32, ts), _oacc((8, D), F32), _oacc((8, D), F32)])
    return dx, dg[0], loss[0]


def _scan_rows(x, length, seg, reverse=False):
    row = lax.broadcasted_iota(jnp.int32, x.shape, 0) % seg
    k = 1
    while k < seg:
        if reverse:
            x = x + jnp.where(row < seg - k, pltpu.roll(x, length - k, 0), 0.0)
        else:
            x = x + jnp.where(row >= k, pltpu.roll(x, k, 0), 0.0)
        k *= 2
    return x


def _fox_decay(z, b_fg128, ts):
    S = z.shape[0]

    def body(t, first, ins, outs, scratch):
        zs_ref, b_ref = ins
        hi_ref, mid_ref, lo_ref = outs
        (carry,) = scratch

        @pl.when(first)
        def _():
            carry[...] = jnp.zeros_like(carry)

        logf = -_softplus(-(zs_ref[...] + b_ref[...]))
        run = _scan_rows(logf, ts, ts) + carry[0:1, :]
        carry[0:1, :] = run[ts - 1:ts, :]
        f2 = run * LOG2E
        hi = f2.astype(BF16)
        r1 = f2 - hi.astype(F32)
        mid = r1.astype(BF16)
        lo = (r1 - mid.astype(F32)).astype(BF16)
        eye = _eye(128, BF16)
        hi_ref[...] = _dot(eye, hi, NT).astype(BF16)
        mid_ref[...] = _dot(eye, mid, NT).astype(BF16)
        lo_ref[...] = _dot(eye, lo, NT).astype(BF16)

    tcol = lambda dt: ((128, S), dt, (128, ts), lambda i: (0, i))
    return _tiled("fox_decay", body, S // ts,
                  [_cols(z, ts, 128, CB_SMALL), _full(b_fg128)],
                  [tcol(BF16), tcol(BF16), tcol(BF16)], scratch=[pltpu.VMEM((8, 128), F32)])


def _fox_decay_bwd(dfk_rows, dfq_rows, z, b_fg128, ts):
    S = z.shape[0]
    H = dfk_rows.shape[0]

    def body(t, first, ins, outs, scratch):
        dfk_ref, dfq_ref, zs_ref, b_ref = ins
        daf_ref, db_ref = outs
        (carry,) = scratch

        @pl.when(first)
        def _():
            carry[...] = jnp.zeros_like(carry)
            db_ref[...] = jnp.zeros_like(db_ref)

        r = lax.broadcasted_iota(jnp.int32, (H, 128), 0)
        c = lax.broadcasted_iota(jnp.int32, (H, 128), 1)
        place = jnp.where(r == c, 1.0, 0.0)
        df = _dot(dfk_ref[...] + dfq_ref[...], place, TN, HIGHEST)
        run = _scan_rows(df, ts, ts, reverse=True) + carry[0:1, :]
        carry[0:1, :] = run[0:1, :]
        daf = run * _sigmoid(-(zs_ref[...] + b_ref[...]))
        daf_ref[...] = daf
        db_ref[0:1, :] += jnp.sum(daf, axis=0, keepdims=True)

    rowsin = lambda a: (a, (H, ts), lambda i: (0, i))
    daf, db = _tiled("fox_decay_bwd", body, S // ts,
                     [rowsin(dfk_rows), rowsin(dfq_rows), _cols(z, ts, 128, CB_SMALL), _full(b_fg128)],
                     [_orow(S, (128,), F32, ts), _oacc((8, 128), F32)],
                     scratch=[pltpu.VMEM((8, 128), F32)], reverse=True)
    return daf, db[0]


FOX_AUG = 80


def _fox_fwd(q_aug, kT_aug, v_aug, tq):
    H, S, da = q_aug.shape
    dv = v_aug.shape[2]
    d = FOX_DIM
    tk = tq // 2
    qscale = (d ** -0.5) * LOG2E

    def kern(q_ref, kT_ref, v_ref, o_ref, lse_ref, qs_ref, s_buf, p_buf, m_scr, acc_scr):
        i = pl.program_id(1)
        col = lax.broadcasted_iota(jnp.int32, (1, da), 1)
        qb = _b16(q_ref[...] * jnp.where(col < d, qscale, 1.0))
        qs_ref[...] = qb

        def keys(t):
            return pl.ds(pl.multiple_of(t * tk, tk), tk)

        def stage(t, slot, mask_off, look_ahead):
            if look_ahead:
                s_buf[1 - slot] = _dot(qb, kT_ref[:, keys(t + 1)])
            pv = _dot(p_buf[1 - slot], v_ref[keys(jnp.maximum(t - 1, 0)), :])

            def scores():
                s = s_buf[slot]
                if mask_off is None:
                    return s
                r = lax.broadcasted_iota(jnp.int32, (tq, tk), 0)
                c = lax.broadcasted_iota(jnp.int32, (tq, tk), 1)
                return jnp.where(c + mask_off <= r, s, NEG)

            m = m_scr[...]
            m_new = jnp.maximum(m, jnp.max(scores(), axis=1, keepdims=True))
            alpha = jnp.exp2(m - m_new)
            p_buf[slot] = _b16(jnp.exp2(scores() - m_new))
            m_scr[...] = m_new
            acc_scr[...] = (acc_scr[...] + pv) * alpha

        s_buf[0] = _dot(qb, kT_ref[:, keys(0)])
        p_buf[1] = jnp.zeros((tq, tk), BF16)
        m_scr[...] = jnp.full((tq, 1), NEG, F32)
        acc_scr[...] = jnp.zeros((tq, dv), F32)

        def pair(n):
            stage(2 * n, 0, None, True)
            stage(2 * n + 1, 1, None, True)

        def quad(m, _):
            pair(2 * m)
            pair(2 * m + 1)
            return 0

        lax.fori_loop(0, i // 2, quad, 0)

        @pl.when(i % 2 == 1)
        def _():
            pair(i - 1)

        stage(2 * i, 0, 0, True)
        stage(2 * i + 1, 1, tk, False)
        acc = acc_scr[...] + _dot(p_buf[1], v_ref[keys(2 * i + 1), :])
        l = acc[:, d:d + 1]
        o_ref[...] = acc[:, :d] / l
        lse_ref[...] = _col_to_row(m_scr[...] + jnp.log(l) * LOG2E)

    return pl.pallas_call(
        kern, name="fox_fwd", grid=(H, S // tq),
        in_specs=[pl.BlockSpec((None, tq, da), lambda h, i: (h, i, 0)),
                  pl.BlockSpec((None, da, S), lambda h, i: (h, 0, 0)),
                  pl.BlockSpec((None, S, dv), lambda h, i: (h, 0, 0))],
        out_specs=[pl.BlockSpec((None, tq, d), lambda h, i: (h, i, 0)),
                   pl.BlockSpec((None, 1, tq), lambda h, i: (h, 0, i)),
                   pl.BlockSpec((None, tq, da), lambda h, i: (h, i, 0))],
        out_shape=[jax.ShapeDtypeStruct((H, S, d), F32), jax.ShapeDtypeStruct((H, 1, S), F32),
                   jax.ShapeDtypeStruct((H, S, da), BF16)],
        scratch_shapes=[pltpu.VMEM((2, tq, tk), F32), pltpu.VMEM((2, tq, tk), BF16),
                        pltpu.VMEM((tq, 1), F32), pltpu.VMEM((tq, dv), F32)],
        compiler_params=_params(("parallel", "arbitrary")),
    )(q_aug, kT_aug, v_aug)


def _fox_bwd(qs, k_aug, kT, v, do, lse_row, delta_row, tq):
    H, S, da = qs.shape
    d = FOX_DIM
    tk = tq
    nq = S // tq
    scale = d ** -0.5

    ts2 = tq // 2
    last = 2 * nq - 1

    def kern(q_ref, k_ref, kT_ref, v_ref, do_ref, lse_ref, dl_ref,
             dqT_ref, dk_ref, dv_ref, dfk_ref, dfq_ref,
             kq_buf, dp_buf, pb_buf, ds_buf, dk_scr, dv_scr, dfk_scr):
        j = pl.program_id(1)

        @pl.when(j == 0)
        def _():
            dqT_ref[...] = jnp.zeros_like(dqT_ref)
            dfq_ref[...] = jnp.zeros_like(dfq_ref)

        kb = k_ref[...]
        kTb = kT_ref[...]
        vb = v_ref[:, :d]
        dk_scr[...] = jnp.zeros_like(dk_scr)
        dv_scr[...] = jnp.zeros_like(dv_scr)
        dfk_scr[...] = jnp.zeros_like(dfk_scr)

        def queries(t):
            return pl.ds(pl.multiple_of(t * ts2, ts2), ts2)

        def products(t, slot):
            rows = queries(t)
            kq_buf[slot] = _dot(kb, q_ref[rows, :], NT)
            dp_buf[slot] = _dot(vb, do_ref[rows, :], NT)

        def pointwise(t, slot, mask_off):
            rows = queries(t)
            sT = kq_buf[slot]
            if mask_off is not None:
                r = lax.broadcasted_iota(jnp.int32, (tk, ts2), 0)
                c = lax.broadcasted_iota(jnp.int32, (tk, ts2), 1)
                sT = jnp.where(r <= c + mask_off, sT, NEG)
            pT = jnp.exp2(sT - lse_ref[:, rows])
            dsT = pT * (dp_buf[slot] - dl_ref[:, rows])
            pb_buf[slot] = _b16(pT)
            ds_buf[slot] = _b16(dsT)
            dfk_scr[...] -= jnp.sum(dsT, axis=1, keepdims=True)
            dfq_ref[:, rows] += jnp.sum(dsT, axis=0, keepdims=True)

        def accumulate(t, slot):
            rows = queries(t)
            dsb = ds_buf[slot]
            dv_scr[...] += _dot(pb_buf[slot], do_ref[rows, :])
            dk_scr[...] += _dot(dsb, q_ref[rows, :])
            dqT_ref[:, rows] += _dot(kTb, dsb) * scale

        def stage(t, slot, mask_off, has_prev):
            products(jnp.minimum(t + 1, last), 1 - slot)
            if has_prev:
                accumulate(t - 1, 1 - slot)
            pointwise(t, slot, mask_off)

        products(2 * j, 0)
        stage(2 * j, 0, 0, False)
        stage(2 * j + 1, 1, ts2, True)

        def pair(n):
            stage(2 * n, 0, None, True)
            stage(2 * n + 1, 1, None, True)

        def quad(m, _):
            pair(j + 1 + 2 * m)
            pair(j + 2 + 2 * m)
            return 0

        n_rest = nq - 1 - j
        lax.fori_loop(0, n_rest // 2, quad, 0)

        @pl.when(n_rest % 2 == 1)
        def _():
            pair(nq - 1)

        accumulate(last, 1)
        dk_ref[...] = dk_scr[:, :d] * LN2
        dv_ref[...] = dv_scr[...]
        dfk_ref[...] = _col_to_row(dfk_scr[...])

    tile = lambda h, j: (h, j, 0)
    whole = lambda h, j: (h, 0, 0)
    rowtile = lambda h, j: (h, 0, j)
    return pl.pallas_call(
        kern, name="fox_bwd", grid=(H, S // tk),
        in_specs=[pl.BlockSpec((None, S, da), whole),
                  pl.BlockSpec((None, tk, da), tile),
                  pl.BlockSpec((None, d, tk), lambda h, j: (h, 0, j)),
                  pl.BlockSpec((None, tk, 128), tile),
                  pl.BlockSpec((None, S, d), whole),
                  pl.BlockSpec((None, 1, S), whole),
                  pl.BlockSpec((None, 1, S), whole)],
        out_specs=[pl.BlockSpec((None, d, S), whole),
                   pl.BlockSpec((None, tk, d), tile),
                   pl.BlockSpec((None, tk, d), tile),
                   pl.BlockSpec((None, 1, tk), rowtile),
                   pl.BlockSpec((None, 1, S), whole)],
        out_shape=[jax.ShapeDtypeStruct((H, d, S), F32), jax.ShapeDtypeStruct((H, S, d), F32),
                   jax.ShapeDtypeStruct((H, S, d), F32), jax.ShapeDtypeStruct((H, 1, S), F32),
                   jax.ShapeDtypeStruct((H, 1, S), F32)],
        scratch_shapes=[pltpu.VMEM((2, tk, ts2), F32), pltpu.VMEM((2, tk, ts2), F32),
                        pltpu.VMEM((2, tk, ts2), BF16), pltpu.VMEM((2, tk, ts2), BF16),
                        pltpu.VMEM((tk, da), F32), pltpu.VMEM((tk, d), F32), pltpu.VMEM((tk, 1), F32)],
        compiler_params=_params(("parallel", "arbitrary")),
    )(qs, k_aug, kT, v, do, lse_row, delta_row)


def _heads_major(a, H, d):
    S = a.shape[0]
    return a.reshape(S, H, d).transpose(1, 0, 2)


def _heads_minor(a):
    H, S, d = a.shape
    return a.transpose(1, 0, 2).reshape(S, H * d)


def _lane_pick(x128, lane):
    return x128[:, lane:lane + 1]


def _l2_fwd(y):
    return lax.rsqrt(jnp.sum(y * y, axis=1, keepdims=True) + EPS)


def _gdn_prep(z, conv_w, a128, dt128, ts):
    S = z.shape[0]
    C3 = 3 * WIDTH
    hb = ts // 8

    def body(t, first, ins, outs, scratch):
        x_ref, halo_ref, zs_ref, w_ref, a_ref, dt_ref = ins
        qkv_ref, c_ref, gb_ref, gbT_ref = outs
        halo = jnp.where(t > 0, halo_ref[...], 0.0)
        xe = jnp.concatenate([halo, x_ref[...]], axis=0)
        w = w_ref[...]
        c = w[3:4, :] * xe[8:, :]
        for back in (1, 2, 3):
            c = c + w[3 - back:4 - back, :] * pltpu.roll(xe, back, 0)[8:, :]
        c_ref[...] = c
        y = c * _sigmoid(c)
        for h in range(GDN_HEADS):
            lo = h * GDN_DIM
            yq = y[:, lo:lo + GDN_DIM]
            qkv_ref[:, lo:lo + GDN_DIM] = yq * (_l2_fwd(yq) * (GDN_DIM ** -0.5))
            yk = y[:, WIDTH + lo:WIDTH + lo + GDN_DIM]
            qkv_ref[:, WIDTH + lo:WIDTH + lo + GDN_DIM] = yk * _l2_fwd(yk)
        qkv_ref[:, 2 * WIDTH:] = y[:, 2 * WIDTH:]
        zs = zs_ref[...]
        lane = lax.broadcasted_iota(jnp.int32, zs.shape, 1)
        g = -jnp.exp(a_ref[...]) * _softplus(zs + dt_ref[...])
        G = _scan_rows(g, ts, CHUNK)
        beta = _sigmoid(zs)
        out = jnp.where(lane < 8, pltpu.roll(g, 128 - LANE_BA, 1), jnp.where(lane < LANE_BB, G, beta))
        gb_ref[...] = out
        gbT_ref[...] = _transpose_exact(out)

    x_in = (z, (ts, C3), lambda i: (i, CB_BQKV))
    halo_in = (z, (8, C3), lambda i: (jnp.maximum(i * hb - 1, 0), CB_BQKV))
    return _tiled("gdn_prep", body, S // ts,
                  [x_in, halo_in, _cols(z, ts, 128, CB_SMALL), _full(conv_w), _full(a128), _full(dt128)],
                  [_orow(S, (C3,), F32, ts), _orow(S, (C3,), F32, ts), _orow(S, (128,), F32, ts),
                   ((128, S), F32, (128, ts), lambda i: (0, i))])


def _chunk_masks(nc):
    r = lax.broadcasted_iota(jnp.int32, (nc, CHUNK, CHUNK), 1)
    c = lax.broadcasted_iota(jnp.int32, (nc, CHUNK, CHUNK), 2)
    return c <= r, c < r, c == r


def _chunk_local(qh, kh, vh, Gc, Gr, beta):
    nc = qh.shape[0]
    incl, strict, _ = _chunk_masks(nc)
    gamma = jnp.exp(jnp.where(incl, Gc - Gr, NEG))
    kb = kh * beta
    P = _bdot(_b16(kb), _b16(kh), 2, 2)
    Qk = _bdot(_b16(qh), _b16(kh), 2, 2)
    eG = jnp.exp(Gc)
    Gl = Gc[:, CHUNK - 1:CHUNK, :]
    edec = jnp.exp(Gl - Gc)
    return incl, strict, gamma, kb, P, Qk, eG, edec


def _gdn_local_fwd(qkv, gb, grow, ts):
    S = qkv.shape[0]
    nc = ts // CHUNK

    def body(t, first, ins, outs, scratch):
        q_ref, k_ref, v_ref, gb_ref, gr_ref = ins
        u_ref, w_ref, qd_ref, kd_ref, aqk_ref, T_ref = outs
        gbv = gb_ref[...]
        for h in range(GDN_HEADS):
            lo = h * GDN_DIM
            qh = q_ref[:, lo:lo + GDN_DIM].reshape(nc, CHUNK, GDN_DIM)
            kh = k_ref[:, lo:lo + GDN_DIM].reshape(nc, CHUNK, GDN_DIM)
            vh = v_ref[:, lo:lo + GDN_DIM].reshape(nc, CHUNK, GDN_DIM)
            Gc = _lane_pick(gbv, LANE_BA + h).reshape(nc, CHUNK, 1)
            beta = _lane_pick(gbv, LANE_BB + h).reshape(nc, CHUNK, 1)
            Gr = gr_ref[h].reshape(nc, 1, CHUNK)
            incl, strict, gamma, kb, P, Qk, eG, edec = _chunk_local(qh, kh, vh, Gc, Gr, beta)
            A = jnp.where(strict, P * gamma, 0.0)
            _, _, eye = _chunk_masks(nc)
            T = jnp.where(eye, 1.0, 0.0) - A
            X = A
            for _ in range(5):
                X = _bdot(X, X, 2, 1, PREC_UT)
                T = T + _bdot(T, X, 2, 1, PREC_UT)
            u = _bdot(T, vh * beta, 2, 1, PREC_UT)
            w = _bdot(T, kb * eG, 2, 1, PREC_UT)
            u_ref[:, lo:lo + GDN_DIM] = u.reshape(ts, GDN_DIM)
            w_ref[:, lo:lo + GDN_DIM] = w.reshape(ts, GDN_DIM)
            qd_ref[:, lo:lo + GDN_DIM] = (qh * eG).reshape(ts, GDN_DIM)
            kd_ref[:, lo:lo + GDN_DIM] = (kh * edec).reshape(ts, GDN_DIM)
            aqk_ref[h] = jnp.where(incl, Qk * gamma, 0.0).reshape(ts, CHUNK)
            T_ref[h] = T.reshape(ts, CHUNK)

    wide = _orow(S, (WIDTH,), F32, ts)
    perhead = ((GDN_HEADS, S, CHUNK), F32, (GDN_HEADS, ts, CHUNK), lambda i: (0, i, 0))
    return _tiled("gdn_local_fwd", body, S // ts,
                  [_cols(qkv, ts, WIDTH, 0), _cols(qkv, ts, WIDTH, 1), _cols(qkv, ts, WIDTH, 2),
                   _rows(gb, ts), (grow, (GDN_HEADS, nc, CHUNK), lambda i: (0, i, 0))],
                  [wide, wide, wide, wide, perhead, perhead])


def _gdn_scan_fwd(u, w, qd, kd, aqk, gb, ts):
    S = u.shape[0]
    nc = ts // CHUNK
    N = S // CHUNK

    def body(t, first, ins, outs, scratch):
        u_ref, w_ref, qd_ref, kd_ref, aqk_ref, gb_ref = ins
        o_ref, vn_ref, st_ref = outs
        (state,) = scratch

        @pl.when(first)
        def _():
            state[...] = jnp.zeros_like(state)

        def chunk(c, _):
            r0 = pl.multiple_of(c * CHUNK, CHUNK)
            rows = pl.ds(r0, CHUNK)
            glast = gb_ref[pl.ds(r0 + CHUNK - 1, 1), :]
            heads = range(GDN_HEADS)
            cols = [slice(h * GDN_DIM, (h + 1) * GDN_DIM) for h in heads]
            S_old = [state[h] for h in heads]
            u_h = [u_ref[rows, cols[h]] for h in heads]
            w_h = [_b16(w_ref[rows, cols[h]]) for h in heads]
            qd_h = [_b16(qd_ref[rows, cols[h]]) for h in heads]
            kd_h = [_b16(kd_ref[rows, cols[h]]) for h in heads]
            aqk_h = [_b16(aqk_ref[h, rows, :]) for h in heads]
            S_new, o_h, vn_h = [], [], []
            for h in heads:
                Sb = _b16(S_old[h])
                both = _dot(jnp.concatenate([w_h[h], qd_h[h]], axis=0), Sb)
                vn = u_h[h] - both[:CHUNK]
                vnb = _b16(vn)
                o_h.append(both[CHUNK:] + _dot(aqk_h[h], vnb))
                egl = jnp.exp(glast[:, LANE_BA + h:LANE_BA + h + 1])
                S_new.append(S_old[h] * egl + _dot(kd_h[h], vnb, TN))
                vn_h.append(vn)
            for h in heads:
                st_ref[c, h] = S_old[h]
                state[h] = S_new[h]
                o_ref[rows, cols[h]] = o_h[h]
                vn_ref[rows, cols[h]] = vn_h[h]
            return 0

        lax.fori_loop(0, nc, chunk, 0)

    wide_in = lambda a: _rows(a, ts)
    wide = _orow(S, (WIDTH,), F32, ts)
    states = ((N, GDN_HEADS, GDN_DIM, GDN_DIM), F32, (nc, GDN_HEADS, GDN_DIM, GDN_DIM),
              lambda i: (i, 0, 0, 0))
    return _tiled("gdn_scan_fwd", body, S // ts,
                  [wide_in(u), wide_in(w), wide_in(qd), wide_in(kd),
                   (aqk, (GDN_HEADS, ts, CHUNK), lambda i: (0, i, 0)), _rows(gb, ts)],
                  [wide, wide, states],
                  scratch=[pltpu.VMEM((GDN_HEADS, GDN_DIM, GDN_DIM), F32)])


def _gdn_scan_bwd(do, w, qd, kd, aqk, vn, states, gb, ts):
    S = do.shape[0]
    nc = ts // CHUNK
    N = S // CHUNK

    def body(t, first, ins, outs, scratch):
        do_ref, w_ref, qd_ref, kd_ref, aqk_ref, vn_ref, st_ref, gb_ref = ins
        du_ref, dw_ref, dqd_ref, dkd_ref, daqk_ref, dgl_ref = outs
        (dstate,) = scratch

        @pl.when(first)
        def _():
            dstate[...] = jnp.zeros_like(dstate)

        r = lax.broadcasted_iota(jnp.int32, (CHUNK, CHUNK), 0)
        cc = lax.broadcasted_iota(jnp.int32, (CHUNK, CHUNK), 1)
        incl = cc <= r
        lane = lax.broadcasted_iota(jnp.int32, (1, 128), 1)

        def chunk(k, _):
            c = nc - 1 - k
            r0 = pl.multiple_of(c * CHUNK, CHUNK)
            rows = pl.ds(r0, CHUNK)
            glast = gb_ref[pl.ds(r0 + CHUNK - 1, 1), :]
            dgl_row = jnp.zeros((1, 128), F32)
            heads = range(GDN_HEADS)
            cols = [slice(h * GDN_DIM, (h + 1) * GDN_DIM) for h in heads]
            S_h = [st_ref[c, h] for h in heads]
            dS_h = [dstate[h] for h in heads]
            do_h = [_b16(do_ref[rows, cols[h]]) for h in heads]
            aqk_h = [_b16(aqk_ref[h, rows, :]) for h in heads]
            vn_h = [_b16(vn_ref[rows, cols[h]]) for h in heads]
            kd_h = [_b16(kd_ref[rows, cols[h]]) for h in heads]
            qd_h = [_b16(qd_ref[rows, cols[h]]) for h in heads]
            w_h = [_b16(w_ref[rows, cols[h]]) for h in heads]
            res = []
            for h in heads:
                Sb, dSb, dob, vnb = _b16(S_h[h]), _b16(dS_h[h]), do_h[h], vn_h[h]
                dvn = _dot(aqk_h[h], dob, TN) + _dot(kd_h[h], dSb)
                dvnb = _b16(dvn)
                daqk = jnp.where(incl, _dot(dob, vnb, NT), 0.0)
                both = jnp.concatenate([dob, dvnb], axis=0)
                by_state = _dot(both, Sb, NT)
                dqd = by_state[:CHUNK]
                dw = -by_state[CHUNK:]
                dkd = _dot(vnb, dSb, NT)
                egl = jnp.exp(glast[:, LANE_BA + h:LANE_BA + h + 1])
                dgl = egl * jnp.sum(jnp.sum(dS_h[h] * S_h[h], axis=1, keepdims=True), axis=0,
                                    keepdims=True)
                dgl_row = jnp.where(lane == h, dgl, dgl_row)
                dS_new = _dot(jnp.concatenate([qd_h[h], -w_h[h]], axis=0), both, TN) + egl * dS_h[h]
                res.append((daqk, dqd, dkd, dw, dvn, dS_new))
            for h in heads:
                daqk, dqd, dkd, dw, dvn, dS_new = res[h]
                daqk_ref[h, rows, :] = daqk
                dqd_ref[rows, cols[h]] = dqd
                dkd_ref[rows, cols[h]] = dkd
                dw_ref[rows, cols[h]] = dw
                du_ref[rows, cols[h]] = dvn
                dstate[h] = dS_new
            dgl_ref[pl.ds(c, 1), :] = dgl_row
            return 0

        lax.fori_loop(0, nc, chunk, 0)

    wide_in = lambda a: _rows(a, ts)
    wide = _orow(S, (WIDTH,), F32, ts)
    perhead_in = lambda a: (a, (GDN_HEADS, ts, CHUNK), lambda i: (0, i, 0))
    perhead = ((GDN_HEADS, S, CHUNK), F32, (GDN_HEADS, ts, CHUNK), lambda i: (0, i, 0))
    return _tiled("gdn_scan_bwd", body, S // ts,
                  [wide_in(do), wide_in(w), wide_in(qd), wide_in(kd), perhead_in(aqk), wide_in(vn),
                   (states, (nc, GDN_HEADS, GDN_DIM, GDN_DIM), lambda i: (i, 0, 0, 0)), _rows(gb, ts)],
                  [wide, wide, wide, wide, perhead, ((N, 128), F32, (nc, 128), lambda i: (i, 0))],
                  scratch=[pltpu.VMEM((GDN_HEADS, GDN_DIM, GDN_DIM), F32)], reverse=True)


def _gdn_local_bwd(qkv, gb, grow, T, du, dw, dqd, dkd, daqk, dgl, ts):
    S = qkv.shape[0]
    nc = ts // CHUNK

    def body(t, first, ins, outs, scratch):
        (q_ref, k_ref, v_ref, gb_ref, gr_ref, T_ref, du_ref, dw_ref, dqd_ref, dkd_ref,
         daqk_ref, dgl_ref) = ins
        dqkv_ref, dgb_ref = outs
        gbv = gb_ref[...]
        dglv = dgl_ref[...]
        lane = lax.broadcasted_iota(jnp.int32, (ts, 128), 1)
        dG_all = jnp.zeros((ts, 128), F32)
        dbeta_all = jnp.zeros((ts, 128), F32)
        for h in range(GDN_HEADS):
            lo = h * GDN_DIM
            cols = slice(lo, lo + GDN_DIM)
            r3 = lambda ref: ref[:, cols].reshape(nc, CHUNK, GDN_DIM)
            qh, kh, vh = r3(q_ref), r3(k_ref), r3(v_ref)
            duh, dwh, dqdh, dkdh = r3(du_ref), r3(dw_ref), r3(dqd_ref), r3(dkd_ref)
            Gc = _lane_pick(gbv, LANE_BA + h).reshape(nc, CHUNK, 1)
            beta = _lane_pick(gbv, LANE_BB + h).reshape(nc, CHUNK, 1)
            Gr = gr_ref[h].reshape(nc, 1, CHUNK)
            Th = T_ref[h].reshape(nc, CHUNK, CHUNK)
            daq = daqk_ref[h].reshape(nc, CHUNK, CHUNK)
            incl, strict, gamma, kb, P, Qk, eG, edec = _chunk_local(qh, kh, vh, Gc, Gr, beta)
            _, _, eye = _chunk_masks(nc)
            vb = vh * beta
            kbg = kb * eG
            dvb = _bdot(Th, duh, 1, 1, PREC_UT)
            dkbg = _bdot(Th, dwh, 1, 1, PREC_UT)
            dT = _bdot(duh, vb, 2, 2, PREC_UT) + _bdot(dwh, kbg, 2, 2, PREC_UT)
            M1 = _bdot(Th, dT, 1, 1, PREC_UT)
            dA = jnp.where(strict, -_bdot(M1, Th, 2, 2, PREC_UT), 0.0)
            dP = dA * gamma
            dQ = daq * gamma
            dgam = (dA * P + daq * Qk) * gamma
            dPb, dQb = _b16(dP), _b16(dQ)
            khb, qhb, kbb = _b16(kh), _b16(qh), _b16(kb)
            dq = _bdot(dQb, khb, 2, 1) + dqdh * eG
            dkb = _bdot(dPb, khb, 2, 1) + dkbg * eG
            dk = (_bdot(dQb, qhb, 1, 1) + _bdot(dPb, kbb, 1, 1) + dkdh * edec + dkb * beta)
            dbeta = (jnp.sum(dkb * kh, axis=2, keepdims=True) + jnp.sum(dvb * vh, axis=2, keepdims=True))
            dv = dvb * beta
            col_as_col = jnp.sum(jnp.where(eye, jnp.sum(dgam, axis=1, keepdims=True), 0.0),
                                 axis=2, keepdims=True)
            kd_term = jnp.sum(dkdh * kh * edec, axis=2, keepdims=True)
            dG = (jnp.sum(dgam, axis=2, keepdims=True) - col_as_col
                  + jnp.sum(dqdh * qh * eG, axis=2, keepdims=True)
                  + jnp.sum(dkbg * kbg, axis=2, keepdims=True) - kd_term)
            dgl_h = dglv[:, h:h + 1].reshape(nc, 1, 1) + jnp.sum(kd_term, axis=1, keepdims=True)
            last = lax.broadcasted_iota(jnp.int32, (nc, CHUNK, 1), 1) == CHUNK - 1
            dG = dG + jnp.where(last, dgl_h, 0.0)
            dqkv_ref[:, cols] = dq.reshape(ts, GDN_DIM)
            dqkv_ref[:, WIDTH + lo:WIDTH + lo + GDN_DIM] = dk.reshape(ts, GDN_DIM)
            dqkv_ref[:, 2 * WIDTH + lo:2 * WIDTH + lo + GDN_DIM] = dv.reshape(ts, GDN_DIM)
            dG_all = jnp.where(lane == LANE_BA + h, dG.reshape(ts, 1), dG_all)
            dbeta_all = jnp.where(lane == LANE_BB + h, dbeta.reshape(ts, 1), dbeta_all)
        dg_all = _scan_rows(dG_all, ts, CHUNK, reverse=True)
        dgb_ref[...] = jnp.where(lane < LANE_BB, dg_all, dbeta_all)

    wide_in = lambda a: _rows(a, ts)
    perhead_in = lambda a: (a, (GDN_HEADS, ts, CHUNK), lambda i: (0, i, 0))
    return _tiled("gdn_local_bwd", body, S // ts,
                  [_cols(qkv, ts, WIDTH, 0), _cols(qkv, ts, WIDTH, 1), _cols(qkv, ts, WIDTH, 2),
                   _rows(gb, ts), (grow, (GDN_HEADS, nc, CHUNK), lambda i: (0, i, 0)), perhead_in(T),
                   wide_in(du), wide_in(dw), wide_in(dqd), wide_in(dkd), perhead_in(daqk),
                   (dgl, (nc, 128), lambda i: (i, 0))],
                  [_orow(S, (3 * WIDTH,), F32, ts), _orow(S, (128,), F32, ts)])


def _gdn_prep_bwd(dqkv, dgb, cpre, z, conv_w, a128, dt128, dz, ts):
    S = z.shape[0]
    C3 = 3 * WIDTH
    hb = ts // 8
    n_tiles = S // ts

    def dpre(dq, c):
        y, dsil = _silu_and_grad(c)
        parts = []
        for h in range(GDN_HEADS):
            lo = h * GDN_DIM
            yq = y[:, lo:lo + GDN_DIM]
            rq = _l2_fwd(yq)
            nq = yq * rq
            dn = dq[:, lo:lo + GDN_DIM] * (GDN_DIM ** -0.5)
            parts.append(rq * (dn - nq * jnp.sum(dn * nq, axis=1, keepdims=True)))
        for h in range(GDN_HEADS):
            lo = WIDTH + h * GDN_DIM
            yk = y[:, lo:lo + GDN_DIM]
            rk = _l2_fwd(yk)
            nk = yk * rk
            dn = dq[:, lo:lo + GDN_DIM]
            parts.append(rk * (dn - nk * jnp.sum(dn * nk, axis=1, keepdims=True)))
        parts.append(dq[:, 2 * WIDTH:])
        return jnp.concatenate(parts, axis=1) * dsil

    def body(t, first, ins, outs, scratch):
        (dq_ref, dqn_ref, c_ref, cn_ref, x_ref, xp_ref, zs_ref, dgb_ref, w_ref, a_ref, dt_ref) = ins
        dx_ref, dzs_ref, dw_ref, dad_ref = outs

        @pl.when(first)
        def _():
            dw_ref[...] = jnp.zeros_like(dw_ref)
            dad_ref[...] = jnp.zeros_like(dad_ref)

        dc = dpre(dq_ref[...], c_ref[...])
        dcn = jnp.where(t < n_tiles - 1, dpre(dqn_ref[...], cn_ref[...]), 0.0)
        dce = jnp.concatenate([dc, dcn], axis=0)
        w = w_ref[...]
        dx = w[3:4, :] * dc
        for back in (1, 2, 3):
            dx = dx + w[3 - back:4 - back, :] * pltpu.roll(dce, ts + 8 - back, 0)[:ts, :]
        dx_ref[...] = _b16(dx)
        halo = jnp.where(t > 0, xp_ref[...], 0.0)
        xe = jnp.concatenate([halo, x_ref[...]], axis=0)
        dw_ref[3:4, :] += jnp.sum(dc * xe[8:, :], axis=0, keepdims=True)
        for back in (1, 2, 3):
            dw_ref[3 - back:4 - back, :] += jnp.sum(dc * pltpu.roll(xe, back, 0)[8:, :], axis=0,
                                                     keepdims=True)
        zs = zs_ref[...]
        dgb = dgb_ref[...]
        lane = lax.broadcasted_iota(jnp.int32, zs.shape, 1)
        arg = zs + dt_ref[...]
        nega = -jnp.exp(a_ref[...])
        dba = dgb * nega * _sigmoid(arg)
        beta = _sigmoid(zs)
        dbb = dgb * beta * (1.0 - beta)
        dzs_ref[...] = jnp.where((lane >= LANE_BA) & (lane < LANE_BB), dba,
                                 jnp.where((lane >= LANE_BB) & (lane < LANE_BB + 4), dbb, 0.0))
        dad_ref[0:1, :] += jnp.sum(dgb * nega * _softplus(arg), axis=0, keepdims=True)
        dad_ref[1:2, :] += jnp.sum(dba, axis=0, keepdims=True)

    nxt = lambda i: (jnp.minimum((i + 1) * hb, S // 8 - 1), 0)
    prv = lambda i: (jnp.maximum(i * hb - 1, 0), CB_BQKV)
    return _tiled("gdn_prep_bwd", body, n_tiles,
                  [_rows(dqkv, ts), (dqkv, (8, C3), nxt), _rows(cpre, ts), (cpre, (8, C3), nxt),
                   (z, (ts, C3), lambda i: (i, CB_BQKV)), (z, (8, C3), prv),
                   _cols(z, ts, 128, CB_SMALL), _rows(dgb, ts), _full(conv_w), _full(a128), _full(dt128)],
                  [((S, N_AL), BF16, (ts, C3), lambda i: (i, CB_BQKV)), _orow(S, (128,), F32, ts),
                   _oacc((8, C3), F32), _oacc((8, 128), F32)],
                  fill=(dz, 0))


def _mem_attn_fwd(z, mk, mv, ts):
    S = z.shape[0]

    def body(t, first, ins, outs, scratch):
        q_ref, mk_ref, mv_ref = ins
        (o_ref,) = outs
        for h in range(MEM_HEADS):
            cols = slice(h * MEM_DIM, (h + 1) * MEM_DIM)
            s = _dot(_b16(q_ref[:, cols]), _b16(mk_ref[:, cols]), NT) * (MEM_DIM ** -0.5)
            m = jnp.max(s, axis=1, keepdims=True)
            p = jnp.exp(s - m)
            p = p / jnp.sum(p, axis=1, keepdims=True)
            o_ref[:, cols] = _dot(_b16(p), _b16(mv_ref[:, cols]))

    (o,) = _tiled("mem_attn_fwd", body, S // ts, [_cols(z, ts, WIDTH, CB_MQ), _full(mk), _full(mv)],
                  [_orow(S, (WIDTH,), F32, ts)])
    return o


def _mem_attn_bwd(do, z, mk, mv, dz, ts):
    S = z.shape[0]
    M = mk.shape[0]

    def body(t, first, ins, outs, scratch):
        do_ref, q_ref, mk_ref, mv_ref = ins
        dq_ref, dmk_ref, dmv_ref = outs

        @pl.when(first)
        def _():
            dmk_ref[...] = jnp.zeros_like(dmk_ref)
            dmv_ref[...] = jnp.zeros_like(dmv_ref)

        scale = MEM_DIM ** -0.5
        for h in range(MEM_HEADS):
            cols = slice(h * MEM_DIM, (h + 1) * MEM_DIM)
            qb = _b16(q_ref[:, cols])
            kb = _b16(mk_ref[:, cols])
            dob = _b16(do_ref[:, cols])
            s = _dot(qb, kb, NT) * scale
            m = jnp.max(s, axis=1, keepdims=True)
            p = jnp.exp(s - m)
            p = p / jnp.sum(p, axis=1, keepdims=True)
            dmv_ref[:, cols] += _dot(_b16(p), dob, TN)
            dp = _dot(dob, _b16(mv_ref[:, cols]), NT)
            ds = p * (dp - jnp.sum(dp * p, axis=1, keepdims=True)) * scale
            dsb = _b16(ds)
            dq_ref[:, cols] = _b16(_dot(dsb, kb))
            dmk_ref[:, cols] += _dot(dsb, qb, TN)

    return _tiled("mem_attn_bwd", body, S // ts,
                  [_rows(do, ts), _cols(z, ts, WIDTH, CB_MQ), _full(mk), _full(mv)],
                  [((S, N_AL), BF16, (ts, WIDTH), lambda i: (i, CB_MQ)), _oacc((M, WIDTH), F32),
                   _oacc((M, WIDTH), F32)],
                  fill=(dz, 0))


def _head_norm(ob, g):
    xs, rs = [], []
    for h in range(GDN_HEADS):
        o = ob[:, h * GDN_DIM:(h + 1) * GDN_DIM]
        r = lax.rsqrt(jnp.mean(o * o, axis=1, keepdims=True) + EPS)
        xs.append(o * r)
        rs.append(r)
    return xs, rs


def _merge_fwd(x, z, o_a, o_b, o_m, gdn_g, b_merge, wb, wout, ts):
    S, D = x.shape

    def body(t, first, ins, outs, scratch):
        (x_ref, g_ref, oa_ref, az_ref, ob_ref, bz_ref, om_ref, mz_ref, gg_ref, bm_ref, wb_ref,
         wo_ref) = ins
        xo_ref, ya_ref, yb_ref, ym_ref, mg_ref = outs
        ya = oa_ref[...] * _silu_and_grad(az_ref[...])[0]
        xs, _ = _head_norm(ob_ref[...], None)
        nb = jnp.concatenate([xh * gg_ref[...] for xh in xs], axis=1)
        yb = nb * _silu_and_grad(bz_ref[...])[0]
        ym = om_ref[...] * _silu_and_grad(mz_ref[...])[0]
        merged = jnp.zeros((ts, D), F32)
        for n, (y, y_ref) in enumerate(((ya, ya_ref), (yb, yb_ref), (ym, ym_ref))):
            yb16 = _b16(y)
            y_ref[...] = yb16
            gate = _sigmoid(g_ref[:, n * D:(n + 1) * D] + bm_ref[:, n * D:(n + 1) * D])
            merged = merged + gate * _dot(yb16, wb_ref[n])
        mb = _b16(merged)
        mg_ref[...] = mb
        xo_ref[...] = x_ref[...] + _dot(mb, wo_ref[...])

    half = lambda a: _rows(a, ts)
    return _tiled("merge_fwd", body, S // ts,
                  [_rows(x, ts), _cols(z, ts, 3 * D, CB_GATES), half(o_a), _cols(z, ts, WIDTH, CB_AZ),
                   half(o_b), _cols(z, ts, WIDTH, CB_BZ), half(o_m), _cols(z, ts, WIDTH, CB_MZ),
                   _full(gdn_g.reshape(1, GDN_DIM)), _full(b_merge.reshape(1, 3 * D)), _full(wb), _full(wout)],
                  [_orow(S, (D,), F32, ts), _orow(S, (WIDTH,), BF16, ts), _orow(S, (WIDTH,), BF16, ts),
                   _orow(S, (WIDTH,), BF16, ts), _orow(S, (D,), BF16, ts)])


def _merge_bwd(dout, z, o_a, o_b, o_m, ya, yb, ym, gdn_g, b_merge, wb, wout, hsum, ts):
    S, D = dout.shape

    def body(t, first, ins, outs, scratch):
        (do_ref, g_ref, oa_ref, az_ref, ob_ref, bz_ref, om_ref, mz_ref, ya_ref, yb_ref, ym_ref,
         gg_ref, bm_ref, wb_ref, wo_ref, hs_ref) = ins
        (dg_ref, dpa_ref, dpb_ref, dpm_ref, doa_ref, dob_ref, dom_ref, dl_ref, dbm_ref, dgg_ref) = outs
        G3 = 3 * D

        @pl.when(first)
        def _():
            dbm_ref[...] = jnp.zeros_like(dbm_ref)
            dgg_ref[...] = jnp.zeros_like(dgg_ref)

        dmerged = _dot(_b16(do_ref[...]), wo_ref[...], NT)
        dys = []
        for n, (y_ref, dp_ref) in enumerate(((ya_ref, dpa_ref), (yb_ref, dpb_ref), (ym_ref, dpm_ref))):
            sl = slice(n * D, (n + 1) * D)
            gate = _sigmoid(g_ref[:, sl] + bm_ref[:, sl])
            proj = _dot(y_ref[...], wb_ref[n])
            dproj = _b16(gate * dmerged)
            dp_ref[...] = dproj
            dgp = dmerged * proj * gate * (1.0 - gate)
            dg_ref[:, sl] = dgp.astype(dg_ref.dtype)
            dbm_ref[0:1, sl] += jnp.sum(dgp, axis=0, keepdims=True)
            dys.append(_dot(dproj, wb_ref[n], NT))
        dya, dyb, dym = dys
        sa, dsa = _silu_and_grad(az_ref[...])
        oa = oa_ref[...]
        doa = dya * sa
        doa_ref[...] = doa
        dg_ref[:, G3:G3 + WIDTH] = _b16(dya * oa * dsa)
        dl_ref[...] = _dot(hs_ref[...], doa * oa, NT, HIGHEST)
        sm, dsm = _silu_and_grad(mz_ref[...])
        dom_ref[...] = dym * sm
        dg_ref[:, G3 + 2 * WIDTH:G3 + 3 * WIDTH] = _b16(dym * om_ref[...] * dsm)
        sb, dsb = _silu_and_grad(bz_ref[...])
        xs, rs = _head_norm(ob_ref[...], None)
        gg = gg_ref[...]
        dgg = jnp.zeros((1, GDN_DIM), F32)
        for h in range(GDN_HEADS):
            cols = slice(h * GDN_DIM, (h + 1) * GDN_DIM)
            dn = dyb[:, cols] * sb[:, cols]
            dg_ref[:, G3 + WIDTH + h * GDN_DIM:G3 + WIDTH + (h + 1) * GDN_DIM] = _b16(
                dyb[:, cols] * (xs[h] * gg) * dsb[:, cols])
            dgg = dgg + jnp.sum(dn * xs[h], axis=0, keepdims=True)
            dxh = dn * gg
            dob_ref[:, cols] = rs[h] * (dxh - xs[h] * jnp.mean(dxh * xs[h], axis=1, keepdims=True))
        dgg_ref[0:1, :] += dgg

    half = lambda a: _rows(a, ts)
    w512 = lambda dt: _orow(S, (WIDTH,), dt, ts)
    return _tiled("merge_bwd", body, S // ts,
                  [_rows(dout, ts), _cols(z, ts, 3 * D, CB_GATES), half(o_a), _cols(z, ts, WIDTH, CB_AZ),
                   half(o_b), _cols(z, ts, WIDTH, CB_BZ), half(o_m), _cols(z, ts, WIDTH, CB_MZ),
                   half(ya), half(yb), half(ym), _full(gdn_g.reshape(1, GDN_DIM)),
                   _full(b_merge.reshape(1, 3 * D)), _full(wb), _full(wout), _full(hsum)],
                  [((S, N_AL), BF16, (ts, 3 * D + 3 * WIDTH), lambda i: (i, CB_MERGE)),
                   _orow(S, (D,), BF16, ts), _orow(S, (D,), BF16, ts),
                   _orow(S, (D,), BF16, ts), w512(F32), w512(F32), w512(F32),
                   ((128, S), F32, (128, ts), lambda i: (0, i)), _oacc((8, 3 * D), F32),
                   _oacc((8, GDN_DIM), F32)])


def _to_aligned(w):
    parts = [w[..., lo:lo + n] for lo, n, _ in sorted(W_IN_PIECES, key=lambda p: p[2])]
    parts.append(jnp.zeros(w.shape[:-1] + (N_AL - N_IN,), w.dtype))
    return jnp.concatenate(parts, axis=-1)


def _from_aligned(w):
    return jnp.concatenate([w[..., al:al + n] for _, n, al in W_IN_PIECES], axis=-1)


def _lanes128(v, lane0):
    return jnp.pad(v.astype(F32)[None, :], ((0, 0), (lane0, 128 - lane0 - v.shape[0])))


def _tiles(S):
    ts = min(512, S // 2)
    return dict(ts=ts, ts_small=min(256, S // 2), tq=min(512, S // 4), tq_fwd=min(1024, S // 2))


def _layer_fwd(x, mem, p):
    S = x.shape[0]
    tl = _tiles(S)
    ts, tss, tq = tl["ts"], tl["ts_small"], tl["tq"]
    h, rstd = _rms_fwd("norm_fwd", x, p["norm_g"], ts)
    z = _mm("in_proj", h, p["w_in_al"], tn=1664)

    b_fg128 = _lanes128(p["b_fg"], LANE_AF)
    f_hi, f_mid, f_lo = _fox_decay(z, b_fg128, ts)
    aq = z[:, CB_AQ * WIDTH:(CB_AQ + 1) * WIDTH]
    ak = z[:, CB_AK * WIDTH:(CB_AK + 1) * WIDTH]
    av = z[:, CB_AV * WIDTH:(CB_AV + 1) * WIDTH]
    q32 = _heads_major(aq, FOX_HEADS, FOX_DIM)
    kh = _heads_major(ak, FOX_HEADS, FOX_DIM).astype(BF16)
    vh = _heads_major(av, FOX_HEADS, FOX_DIM).astype(BF16)
    piecesT = jnp.stack([f[:FOX_HEADS] for f in (f_hi, f_mid, f_lo)], axis=1)
    pieces = piecesT.transpose(0, 2, 1)
    ones3 = jnp.ones((FOX_HEADS, S, 3), BF16)
    padk = jnp.zeros((FOX_HEADS, S, FOX_AUG - FOX_DIM - 6), BF16)
    q_aug = jnp.concatenate([q32, pieces.astype(F32), ones3.astype(F32), padk.astype(F32)], axis=-1)
    k_aug = jnp.concatenate([kh, ones3, -pieces, padk], axis=-1)
    kT_aug = k_aug.transpose(0, 2, 1)
    v_aug = jnp.concatenate([vh, ones3[:, :, :1], jnp.zeros((FOX_HEADS, S, 128 - FOX_DIM - 1), BF16)],
                            axis=-1)
    o_h, lse, qs = _fox_fwd(q_aug, kT_aug, v_aug, tl["tq_fwd"])
    o_a = _heads_minor(o_h)

    a128 = _lanes128(p["a_log"], LANE_BA)
    dt128 = _lanes128(p["dt_bias"], LANE_BA)
    qkv, cpre, gb, gbT = _gdn_prep(z, p["conv_w"], a128, dt128, ts)
    grow = gbT[LANE_BA:LANE_BA + GDN_HEADS].reshape(GDN_HEADS, S // CHUNK, CHUNK)
    u, w, qd, kd, aqk, T = _gdn_local_fwd(qkv, gb, grow, ts)
    o_b, vn, states = _gdn_scan_fwd(u, w, qd, kd, aqk, gb, ts)

    mem_h, mem_r = _rms_fwd("mem_norm_fwd", mem, p["mem_norm_g"], mem.shape[0])
    mkv = _mm("mem_kv", mem_h, p["w_mem_kv"])
    mk, mv = mkv[:, :WIDTH], mkv[:, WIDTH:]
    o_m = _mem_attn_fwd(z, mk, mv, ts)

    x_next, ya, yb, ym, merged = _merge_fwd(x, z, o_a, o_b, o_m, p["gdn_norm_g"], p["b_merge"],
                                            p["w_branch"], p["w_out"], ts)
    saved = dict(x=x, h=h, rstd=rstd, z=z, b_fg128=b_fg128, qs=qs, k_aug=k_aug, kT_aug=kT_aug, v_aug=v_aug, lse=lse, o_a=o_a, a128=a128, dt128=dt128, qkv=qkv, cpre=cpre, gb=gb,
                 grow=grow, w=w, qd=qd, kd=kd, aqk=aqk, T=T, o_b=o_b, vn=vn, states=states,
                 mem_h=mem_h, mem_r=mem_r, mk=mk, mv=mv, o_m=o_m, ya=ya, yb=yb, ym=ym, merged=merged)
    return x_next, saved


def _layer_bwd(dout, mem, p, s):
    S = dout.shape[0]
    tl = _tiles(S)
    ts, tss, tq = tl["ts"], tl["ts_small"], tl["tq"]
    z = s["z"]
    hsum = (jnp.arange(128)[:, None] == jnp.arange(WIDTH)[None, :] // FOX_DIM).astype(F32)
    (dz, dpa, dpb, dpm, do_a, do_b, do_m, deltaT, db_merge, dgdn_g) = _merge_bwd(
        dout, z, s["o_a"], s["o_b"], s["o_m"], s["ya"], s["yb"], s["ym"], p["gdn_norm_g"],
        p["b_merge"], p["w_branch"], p["w_out"], hsum, tss)
    g = {}
    g["b_merge"] = db_merge[0]
    g["gdn_norm_g"] = dgdn_g[0]
    g["w_out"] = _mm("dw_out", s["merged"], dout, ta=True)
    g["w_branch"] = jnp.stack([_mm("dw_branch", y, dp, ta=True)
                               for y, dp in ((s["ya"], dpa), (s["yb"], dpb), (s["ym"], dpm))])

    do_h = _heads_major(do_a, FOX_HEADS, FOX_DIM).astype(BF16)
    delta_row = deltaT[:FOX_HEADS, None, :]
    dqT, dk_h, dv_h, dfk, dfq = _fox_bwd(s["qs"], s["k_aug"], s["kT_aug"], s["v_aug"], do_h, s["lse"],
                                         delta_row, tq)
    daq = _heads_minor(dqT.transpose(0, 2, 1))
    dak = _heads_minor(dk_h)
    dav = _heads_minor(dv_h)
    daf128, db_fg = _fox_decay_bwd(dfk[:, 0, :], dfq[:, 0, :], z, s["b_fg128"], ts)
    g["b_fg"] = db_fg[:FOX_HEADS]

    du, dw, dqd, dkd, daqk, dgl = _gdn_scan_bwd(do_b, s["w"], s["qd"], s["kd"], s["aqk"], s["vn"],
                                                s["states"], s["gb"], ts)
    dqkv, dgb = _gdn_local_bwd(s["qkv"], s["gb"], s["grow"], s["T"], du, dw, dqd, dkd, daqk, dgl, ts)
    dz, dzs_b, dconv, dad = _gdn_prep_bwd(dqkv, dgb, s["cpre"], z, p["conv_w"], s["a128"],
                                          s["dt128"], dz, ts)
    g["conv_w"] = dconv[:4]
    g["a_log"] = dad[0, LANE_BA:LANE_BA + GDN_HEADS]
    g["dt_bias"] = dad[1, LANE_BA:LANE_BA + GDN_HEADS]

    dz, dmk, dmv = _mem_attn_bwd(do_m, z, s["mk"], s["mv"], dz, ts)
    dmkv = jnp.concatenate([dmk, dmv], axis=1)
    g["w_mem_kv"] = _mm("dw_mem_kv", s["mem_h"], dmkv, ta=True)
    dmem_h = _mm("dmem_h", dmkv, p["w_mem_kv"], tb=True)
    M = mem.shape[0]
    _, g["mem_norm_g"] = _rms_bwd("mem_norm_bwd", dmem_h, mem, s["mem_r"], p["mem_norm_g"],
                                  jnp.zeros_like(mem), M)

    lane = jnp.arange(128)[None, :]
    dsmall = jnp.where(lane < 8, daf128, dzs_b)
    daqkv = jnp.concatenate([_b16(daq), _b16(dak), _b16(dav)], axis=1)
    dz = lax.dynamic_update_slice(dz, daqkv, (0, CB_AQKV * 3 * WIDTH))
    dz = lax.dynamic_update_slice(dz, _b16(dsmall), (0, CB_SMALL * 128))
    g["w_in_al"] = _mm("dw_in", s["h"], dz, ta=True, tn=1664)
    dh = _mm("dh", dz, p["w_in_al"], tb=True, tk=1664)
    dx, g["norm_g"] = _rms_bwd("norm_bwd", dh, s["x"], s["rstd"], p["norm_g"], dout, ts)
    return dx, g


def _local_step(x, mem, layers, final_norm_g, loss_target):
    S = x.shape[0]
    saves = []
    cur = x
    for p in layers:
        cur, sv = _layer_fwd(cur, mem, p)
        saves.append(sv)
    dx, dgf, loss_lanes = _loss_head(cur, final_norm_g, loss_target, _tiles(S)["ts"])
    grads = [None] * len(layers)
    for l in reversed(range(len(layers))):
        dx, grads[l] = _layer_bwd(dx, mem, layers[l], saves[l])
    return loss_lanes, dx, grads, dgf


HBM_SPEC = pl.BlockSpec(memory_space=pltpu.HBM)


def _mesh_pos():
    return lax.axis_index("x"), lax.axis_index("y"), lax.axis_index("c")


def _comm_call(name, body, arrays, out_shapes, n_remote, n_local):
    n = len(arrays)

    def kern(*refs):
        body(refs[:n], refs[n:2 * n], refs[2 * n], refs[2 * n + 1], refs[2 * n + 2])

    return pl.pallas_call(
        kern, name=name, out_shape=out_shapes, in_specs=[HBM_SPEC] * n, out_specs=[HBM_SPEC] * n,
        scratch_shapes=[pltpu.SemaphoreType.DMA((n_remote,)), pltpu.SemaphoreType.DMA((n_remote,)),
                        pltpu.SemaphoreType.DMA((max(n_local, 1),))],
    )(*arrays)


def _remote(src, dst, send_sems, recv_sems, k, to):
    return pltpu.make_async_remote_copy(src_ref=src, dst_ref=dst, send_sem=send_sems.at[k],
                                        recv_sem=recv_sems.at[k], device_id=to, device_id_type=MESH_ID)


def _other_chips(mx, my):
    return [(1 - mx, my), (mx, 1 - my), (1 - mx, 1 - my)]


def _gather_chips(name, shards):
    n = len(shards)

    def body(ins, outs, send_sems, recv_sems, local_sems):
        mx, my, mc = _mesh_pos()
        me = 2 * mx + my
        sibling = (mx, my, 1 - mc)
        chips = _other_chips(mx, my)
        sends = []
        for a in range(n):
            for k, (px, py) in enumerate(chips):
                cp = _remote(ins[a].at[mc], outs[a].at[me, mc], send_sems, recv_sems, 6 * a + k,
                             (px, py, mc))
                cp.start()
                sends.append(cp)
        for a in range(n):
            for k, (px, py) in enumerate(chips):
                j = 2 * px + py
                _remote(ins[a].at[mc], outs[a].at[j, mc], send_sems, recv_sems, 6 * a + k,
                        (px, py, mc)).wait_recv()
                cp = _remote(outs[a].at[j, mc], outs[a].at[j, mc], send_sems, recv_sems, 6 * a + 3 + k,
                             sibling)
                cp.start()
                sends.append(cp)
        for a in range(n):
            for k, (px, py) in enumerate(chips):
                j = 2 * px + py
                _remote(outs[a].at[j, 1 - mc], outs[a].at[j, 1 - mc], send_sems, recv_sems,
                        6 * a + 3 + k, sibling).wait_recv()
        for cp in sends:
            cp.wait_send()

    shapes = [jax.ShapeDtypeStruct((N_CHIPS,) + s.shape, s.dtype) for s in shards]
    outs = _comm_call(name, body, shards, shapes, 6 * n, 0)
    me = 2 * lax.axis_index("x") + lax.axis_index("y")
    return [lax.dynamic_update_index_in_dim(o, s, me, 0) for o, s in zip(outs, shards)]


def _sibling_swap(gs):
    n = len(gs)

    def body(ins, outs, send_sems, recv_sems, local_sems):
        mx, my, mc = _mesh_pos()
        sends = []
        for a in range(n):
            cp = _remote(ins[a].at[:, 1 - mc], outs[a], send_sems, recv_sems, a, (mx, my, 1 - mc))
            cp.start()
            sends.append(cp)
        for cp in sends:
            cp.wait()

    shapes = [jax.ShapeDtypeStruct((g.shape[0],) + g.shape[2:], g.dtype) for g in gs]
    return _comm_call("grad_sibling_swap", body, gs, shapes, n, 0)


def _chip_exchange(ps):
    n = len(ps)

    def body(ins, outs, send_sems, recv_sems, local_sems):
        mx, my, mc = _mesh_pos()
        me = 2 * mx + my
        chips = _other_chips(mx, my)
        sends = []
        for a in range(n):
            for k, (px, py) in enumerate(chips):
                cp = _remote(ins[a].at[2 * px + py], outs[a].at[me], send_sems, recv_sems, 3 * a + k,
                             (px, py, mc))
                cp.start()
                sends.append(cp)
        for a in range(n):
            for k, (px, py) in enumerate(chips):
                _remote(ins[a].at[me], outs[a].at[2 * px + py], send_sems, recv_sems, 3 * a + k,
                        (px, py, mc)).wait_recv()
        for cp in sends:
            cp.wait_send()

    shapes = [jax.ShapeDtypeStruct(p.shape, p.dtype) for p in ps]
    outs = _comm_call("grad_chip_exchange", body, ps, shapes, 3 * n, 0)
    me = 2 * lax.axis_index("x") + lax.axis_index("y")
    return [lax.dynamic_update_index_in_dim(o, lax.dynamic_index_in_dim(p, me, 0, keepdims=False), me, 0)
            for o, p in zip(outs, ps)]


def _sibling_gather(hs):
    n = len(hs)

    def body(ins, outs, send_sems, recv_sems, local_sems):
        mx, my, mc = _mesh_pos()
        sends = []
        for a in range(n):
            cp = _remote(ins[a], outs[a], send_sems, recv_sems, a, (mx, my, 1 - mc))
            cp.start()
            sends.append(cp)
        for cp in sends:
            cp.wait()

    shapes = [jax.ShapeDtypeStruct(h.shape, h.dtype) for h in hs]
    theirs = _comm_call("grad_sibling_gather", body, hs, shapes, n, 0)
    first = lax.axis_index("c") == 0
    return [jnp.stack([jnp.where(first, h, t), jnp.where(first, t, h)]) for h, t in zip(hs, theirs)]


def _add_pairs(a, b, tr, out_dtype):
    n, H, C = a.shape

    def kern(a_ref, b_ref, o_ref):
        o_ref[...] = (a_ref[...] + b_ref[...]).astype(o_ref.dtype)

    spec = pl.BlockSpec((None, tr, C), lambda j, i: (j, i, 0))
    return pl.pallas_call(
        kern, name="grad_pair_sum", grid=(n, H // tr), in_specs=[spec, spec], out_specs=spec,
        out_shape=jax.ShapeDtypeStruct((n, H, C), out_dtype),
        compiler_params=_params(("parallel", "parallel")),
    )(a, b)


def _sum_slots(r4, tr):
    n, H, C = r4.shape

    def kern(r_ref, o_ref):
        f = lambda k: r_ref[k].astype(F32)
        o_ref[...] = ((f(0) + f(1)) + f(2)) + f(3)

    return pl.pallas_call(
        kern, name="grad_chip_sum", grid=(H // tr,),
        in_specs=[pl.BlockSpec((n, tr, C), lambda i: (0, i, 0))],
        out_specs=pl.BlockSpec((tr, C), lambda i: (i, 0)),
        out_shape=jax.ShapeDtypeStruct((H, C), F32),
        compiler_params=_params(("parallel",)),
    )(r4)


def _adamw(w, g, m, v, tr):
    R, C = w.shape
    c1 = 1.0 - ADAM_B1
    c2 = 1.0 - ADAM_B2
    bc1 = 1.0 - ADAM_B1 ** ADAM_STEP
    bc2 = 1.0 - ADAM_B2 ** ADAM_STEP

    def kern(w_ref, g_ref, m_ref, v_ref, d_ref, mo_ref, vo_ref):
        gv = g_ref[...]
        mn = ADAM_B1 * m_ref[...] + c1 * gv
        vn = ADAM_B2 * v_ref[...] + c2 * (gv * gv)
        m_hat = mn / bc1
        v_hat = vn / bc2
        d_ref[...] = -ADAM_LR * (m_hat / (jnp.sqrt(v_hat) + ADAM_EPS) + ADAM_WD * w_ref[...])
        mo_ref[...] = mn
        vo_ref[...] = vn

    spec = pl.BlockSpec((tr, C), lambda i: (i, 0))
    shape = jax.ShapeDtypeStruct((R, C), F32)
    return pl.pallas_call(
        kern, name="adamw", grid=(R // tr,), in_specs=[spec] * 4, out_specs=[spec] * 3,
        out_shape=[shape] * 3, compiler_params=_params(("parallel",)),
    )(w, g, m, v)


PACK_COLS = 1024
PACK_ROWS = 512
W_SHARD = N_IN // N_CHIPS
SLAB = ("conv_w", "w_mem_kv", "w_branch", "w_out")
SMALL =("norm_g", "b_fg", "b_merge", "a_log", "dt_bias", "gdn_norm_g", "mem_norm_g", "final_norm_g")
ALL_WEIGHTS = ("norm_g", "w_in", "b_fg", "b_merge", "conv_w", "a_log", "dt_bias", "gdn_norm_g",
               "mem_norm_g", "w_mem_kv", "w_branch", "w_out", "final_norm_g")
SHARD_AXIS = {"w_in": 2, "conv_w": 2, "w_mem_kv": 1, "w_branch": 3, "w_out": 1}


def _pack(arrays, row_multiple):
    flat = jnp.concatenate([a.reshape(-1) for a in arrays])
    n = flat.shape[0]
    rows = -(-n // PACK_COLS)
    rows = -(-rows // row_multiple) * row_multiple
    flat = jnp.pad(flat, (0, rows * PACK_COLS - n))
    return flat.reshape(rows, PACK_COLS)


def _unpack(slab, shapes):
    out, off = [], 0
    for shp in shapes:
        n = 1
        for d in shp:
            n *= d
        r0, r1 = off // PACK_COLS, -(-(off + n) // PACK_COLS)
        rows = slab[r0:r1].reshape(-1)
        out.append(rows[off - r0 * PACK_COLS:off - r0 * PACK_COLS + n].reshape(shp))
        off += n
    return out


def _shard_of(full, name, j):
    ax = SHARD_AXIS[name]
    n = full.shape[ax] // N_CHIPS
    return lax.slice_in_dim(full, j * n, (j + 1) * n, axis=ax)


def _aligned_from_shards(shards):
    def cols(lo, n):
        parts = []
        while n > 0:
            j, off = divmod(lo, W_SHARD)
            take = min(n, W_SHARD - off)
            parts.append(shards[j][..., off:off + take])
            lo, n = lo + take, n - take
        return parts

    out = []
    for lo, n, _ in sorted(W_IN_PIECES, key=lambda p: p[2]):
        out += cols(lo, n)
    out.append(jnp.zeros(shards[0].shape[:-1] + (N_AL - N_IN,), shards[0].dtype))
    return jnp.concatenate(out, axis=-1)


def _shard_from_aligned(w_al, j):
    lo_j, hi_j = j * W_SHARD, (j + 1) * W_SHARD
    parts = []
    for lo, n, al in W_IN_PIECES:
        a, b = max(lo, lo_j), min(lo + n, hi_j)
        if a < b:
            parts.append(w_al[..., al + a - lo:al + b - lo])
    return jnp.concatenate(parts, axis=-1)


def kernel(x, mem, norm_g, w_in, b_fg, b_merge, conv_w, a_log, dt_bias, gdn_norm_g, mem_norm_g, w_mem_kv, w_branch, w_out, final_norm_g, loss_target, m_norm_g, m_w_in, m_b_fg, m_b_merge, m_conv_w, m_a_log, m_dt_bias, m_gdn_norm_g, m_mem_norm_g, m_w_mem_kv, m_w_branch, m_w_out, m_final_norm_g, v_norm_g, v_w_in, v_b_fg, v_b_merge, v_conv_w, v_a_log, v_dt_bias, v_gdn_norm_g, v_mem_norm_g, v_w_mem_kv, v_w_branch, v_w_out, v_final_norm_g):
    wts = dict(norm_g=norm_g, w_in=w_in, b_fg=b_fg, b_merge=b_merge, conv_w=conv_w, a_log=a_log,
               dt_bias=dt_bias, gdn_norm_g=gdn_norm_g, mem_norm_g=mem_norm_g, w_mem_kv=w_mem_kv,
               w_branch=w_branch, w_out=w_out, final_norm_g=final_norm_g)
    mom = dict(norm_g=m_norm_g, w_in=m_w_in, b_fg=m_b_fg, b_merge=m_b_merge, conv_w=m_conv_w,
               a_log=m_a_log, dt_bias=m_dt_bias, gdn_norm_g=m_gdn_norm_g, mem_norm_g=m_mem_norm_g,
               w_mem_kv=m_w_mem_kv, w_branch=m_w_branch, w_out=m_w_out, final_norm_g=m_final_norm_g)
    vel = dict(norm_g=v_norm_g, w_in=v_w_in, b_fg=v_b_fg, b_merge=v_b_merge, conv_w=v_conv_w,
               a_log=v_a_log, dt_bias=v_dt_bias, gdn_norm_g=v_gdn_norm_g, mem_norm_g=v_mem_norm_g,
               w_mem_kv=v_w_mem_kv, w_branch=v_w_branch, w_out=v_w_out, final_norm_g=v_final_norm_g)

    big = ("w_in", "w_mem_kv", "w_branch", "w_out")
    gathered = _gather_chips("weight_gather", [wts[n].astype(BF16) for n in big] + [conv_w])
    all_w = dict(zip(big + ("conv_w",), gathered))
    w_in_al = _aligned_from_shards([all_w["w_in"][j] for j in range(N_CHIPS)])

    layers = []
    for l in range(DEPTH):
        rows_of = lambda n: all_w[n][:, l].reshape(D_MODEL, D_MODEL)
        last_of = lambda n: jnp.concatenate([all_w[n][j, l] for j in range(N_CHIPS)], axis=-1)
        layers.append(dict(norm_g=norm_g[l], w_in_al=w_in_al[l], b_fg=b_fg[l], b_merge=b_merge[l],
                           conv_w=jnp.pad(last_of("conv_w"), ((0, 4), (0, 0))), a_log=a_log[l],
                           dt_bias=dt_bias[l], gdn_norm_g=gdn_norm_g[l], mem_norm_g=mem_norm_g[l],
                           w_mem_kv=rows_of("w_mem_kv"), w_branch=last_of("w_branch"),
                           w_out=rows_of("w_out")))

    loss_lanes, dx, grads, dgf = _local_step(x[0], mem[0], layers, final_norm_g, loss_target[0])

    gfull = {n: jnp.stack([grads[l][n] for l in range(DEPTH)])
             for n in ("norm_g", "b_fg", "b_merge", "conv_w", "a_log", "dt_bias", "gdn_norm_g",
                       "mem_norm_g", "w_mem_kv", "w_branch", "w_out")}
    gfull["final_norm_g"] = dgf
    loss_local = jnp.sum(loss_lanes).reshape(1)
    small_g = [gfull[n] for n in SMALL] + [loss_local]
    dw_al = jnp.stack([grads[l]["w_in_al"] for l in range(DEPTH)])
    ga = jnp.stack([_shard_from_aligned(dw_al, j) for j in range(N_CHIPS)])
    mats = ("w_mem_kv", "w_branch", "w_out")
    rest = ("conv_w",) + SMALL
    gb = jnp.stack([_pack([_shard_of(gfull[n], n, j) for n in mats], PACK_ROWS)
                    for j in range(N_CHIPS)])
    gc = jnp.stack([_pack([_shard_of(gfull["conv_w"], "conv_w", j)] + small_g, 16)
                    for j in range(N_CHIPS)])
    halves = lambda g: g.reshape(N_CHIPS, 2, g.shape[1] // 2, PACK_COLS)
    gb, gc = halves(gb), halves(gc)

    mc = lax.axis_index("c")
    tr = 256
    trs = (tr, tr, 8)
    from_sibling = _sibling_swap([ga, gb, gc])
    mine = [lax.dynamic_index_in_dim(g, mc, axis=1, keepdims=False) for g in (ga, gb, gc)]
    pair = [_add_pairs(a, b, t, dt) for a, b, t, dt in zip(mine, from_sibling, trs, (BF16, BF16, F32))]
    slots = _chip_exchange(pair)
    half = [_sum_slots(s, t) for s, t in zip(slots, trs)]
    ga_sum, gb_sum, gc_sum = _sibling_gather(half)
    flat = lambda g: g.reshape(-1, PACK_COLS)

    g_un = dict(zip(mats, _unpack(flat(gb_sum), [wts[n].shape for n in mats])))
    g_un.update(zip(rest + ("loss",), _unpack(flat(gc_sum), [wts[n].shape for n in rest] + [(1,)])))
    g_un["w_in"] = ga_sum
    d_un, m_un, v_un = {}, {}, {}
    rows2d = lambda a: a.reshape(-1, a.shape[-1])
    for n in ("w_in", "w_mem_kv", "w_branch", "w_out"):
        res = _adamw(rows2d(wts[n]), rows2d(g_un[n]), rows2d(mom[n]), rows2d(vel[n]), tr)
        d_un[n], m_un[n], v_un[n] = [r.reshape(wts[n].shape) for r in res]
    little = ("conv_w",) + SMALL
    slab = lambda d: _pack([d[n] for n in little], 8)
    res = _adamw(slab(wts), slab(g_un), slab(mom), slab(vel), 8)
    little_shapes = [wts[n].shape for n in little]
    for out, r in zip((d_un, m_un, v_un), res):
        out.update(zip(little, _unpack(r, little_shapes)))

    loss = g_un["loss"][0]
    return (loss, dx[None], *[g_un[n] for n in ALL_WEIGHTS], *[d_un[n] for n in ALL_WEIGHTS],
            *[m_un[n] for n in ALL_WEIGHTS], *[v_un[n] for n in ALL_WEIGHTS])
```

```python
import functools

import jax
import jax.numpy as jnp
from jax import lax
from jax.experimental import pallas as pl
from jax.experimental.pallas import tpu as pltpu

F32 = jnp.float32
BF16 = jnp.bfloat16
HIGHEST = lax.Precision.HIGHEST
PREC_UT = lax.Precision.HIGH
MESH_ID = pl.DeviceIdType.MESH

D_MODEL = 1024
DEPTH = 2
CHUNK = 64
EPS = 1e-6
FOX_HEADS, FOX_DIM = 8, 64
GDN_HEADS, GDN_DIM = 4, 128
MEM_HEADS, MEM_DIM = 4, 128
WIDTH = 512
N_BRANCH = 3
N_IN = 8208
N_AL = 8320
N_CHIPS = 4
NEG = -1e30
LOG2E = 1.4426950408889634
LN2 = 0.6931471805599453

ADAM_LR, ADAM_B1, ADAM_B2, ADAM_EPS, ADAM_WD, ADAM_STEP = 0.001, 0.9, 0.999, 1e-08, 0.01, 10

CB_GATES = 0
CB_AZ, CB_BZ, CB_MZ = 6, 7, 8
CB_MERGE = 0
CB_BQKV = 3
CB_AQ, CB_AK, CB_AV = 12, 13, 14
CB_AQKV = 4
CB_MQ = 15
CB_SMALL = 64
W_IN_PIECES = ((0, 512, 6144), (512, 512, 6656), (1024, 512, 7168), (1536, 8, 8192), (1544, 512, 3072),
               (2056, 512, 4608), (2568, 512, 5120), (3080, 512, 5632), (3592, 4, 8200), (3596, 4, 8204),
               (3600, 512, 3584), (4112, 512, 7680), (4624, 512, 4096), (5136, 3072, 0))
LANE_AF, LANE_BA, LANE_BB = 0, 8, 12

NN = ((1,), (0,))
NT = ((1,), (1,))
TN = ((0,), (0,))

VMEM_LIMIT_BYTES = 56 * 1024 * 1024


def _dot(a, b, dims=NN, prec=None):
    return lax.dot_general(a, b, (dims, ((), ())), preferred_element_type=F32, precision=prec)


def _bdot(a, b, ca, cb, prec=None):
    return lax.dot_general(a, b, (((ca,), (cb,)), ((0,), (0,))), preferred_element_type=F32,
                           precision=prec)


def _b16(a):
    return a.astype(BF16)


def _eye(n, dtype=F32):
    r = lax.broadcasted_iota(jnp.int32, (n, n), 0)
    c = lax.broadcasted_iota(jnp.int32, (n, n), 1)
    return jnp.where(r == c, 1.0, 0.0).astype(dtype)


def _transpose_exact(x):
    return _dot(_eye(x.shape[1]), x, NT, HIGHEST)


def _col_to_row(col):
    n = col.shape[0]
    return jnp.sum(jnp.where(_eye(n) > 0.5, col, 0.0), axis=0, keepdims=True)


def _row_to_col(row):
    n = row.shape[1]
    return jnp.sum(jnp.where(_eye(n) > 0.5, row, 0.0), axis=1, keepdims=True)


def _sigmoid(x):
    return 1.0 / (1.0 + jnp.exp(-x))


def _softplus(x):
    return jnp.maximum(x, 0.0) + jnp.log(1.0 + jnp.exp(-jnp.abs(x)))


def _silu_and_grad(x):
    s = _sigmoid(x)
    return x * s, s * (1.0 + x * (1.0 - s))


def _params(semantics):
    return pltpu.CompilerParams(dimension_semantics=semantics, vmem_limit_bytes=VMEM_LIMIT_BYTES)


def _rows(a, ts):
    nd = a.ndim
    return (a, (ts,) + a.shape[1:], lambda i, nd=nd: (i,) + (0,) * (nd - 1))


def _cols(a, ts, width, cb):
    return (a, (ts, width), lambda i, cb=cb: (i, cb))


def _full(a):
    nd = a.ndim
    return (a, a.shape, lambda i, nd=nd: (0,) * nd)


def _orow(S, tail, dtype, ts):
    nd = 1 + len(tail)
    return ((S,) + tuple(tail), dtype, (ts,) + tuple(tail), lambda i, nd=nd: (i,) + (0,) * (nd - 1))


def _oacc(shape, dtype):
    nd = len(shape)
    return (tuple(shape), dtype, tuple(shape), lambda i, nd=nd: (0,) * nd)


def _tiled(name, body, n_steps, ins, outs, scratch=(), reverse=False, fill=None):
    def rev(imap):
        if not reverse:
            return imap
        return lambda i: imap(n_steps - 1 - i)

    in_specs = [pl.BlockSpec(blk, rev(imap)) for (_, blk, imap) in ins]
    out_specs = [pl.BlockSpec(blk, rev(imap)) for (_, _, blk, imap) in outs]
    out_shape = [jax.ShapeDtypeStruct(shape, dt) for (shape, dt, _, _) in outs]
    n_in, n_out = len(ins), len(outs)
    arrays = [a for (a, _, _) in ins]
    aliases = {}
    n_extra = 0
    if fill is not None:
        arrays.append(fill[0])
        in_specs.append(pl.BlockSpec(memory_space=pl.ANY))
        aliases = {n_in: fill[1]}
        n_extra = 1

    def kern(*refs):
        step = pl.program_id(0)
        t = (n_steps - 1 - step) if reverse else step
        lo = n_in + n_extra
        body(t, step == 0, refs[:n_in], refs[lo:lo + n_out], refs[lo + n_out:])

    res = pl.pallas_call(
        kern, name=name, grid=(n_steps,), in_specs=in_specs, out_specs=out_specs,
        out_shape=out_shape, scratch_shapes=list(scratch), input_output_aliases=aliases,
        compiler_params=_params(("arbitrary",)),
    )(*arrays)
    return res


def _pick(n, pref):
    if n <= pref:
        return n
    best = None
    for t in range(128, pref + 1, 128):
        if n % t == 0:
            best = t
    assert best is not None, (n, pref)
    return best


def _mm(name, a, b, ta=False, tb=False, out_dtype=F32, tm=1024, tn=1024, tk=1024):
    if ta:
        K, M = a.shape
    else:
        M, K = a.shape
    if tb:
        N, K2 = b.shape
    else:
        K2, N = b.shape
    assert K == K2, (a.shape, b.shape, ta, tb)
    tm, tn, tk = _pick(M, tm), _pick(N, tn), _pick(K, tk)
    nk = K // tk
    a_spec = (pl.BlockSpec((tk, tm), lambda i, j, k: (k, i)) if ta
              else pl.BlockSpec((tm, tk), lambda i, j, k: (i, k)))
    b_spec = (pl.BlockSpec((tn, tk), lambda i, j, k: (j, k)) if tb
              else pl.BlockSpec((tk, tn), lambda i, j, k: (k, j)))
    dims = ((0,) if ta else (1,), (1,) if tb else (0,))

    def kern_single(a_ref, b_ref, o_ref):
        o_ref[...] = _dot(_b16(a_ref[...]), _b16(b_ref[...]), dims).astype(o_ref.dtype)

    def kern_acc(a_ref, b_ref, o_ref, acc_ref):
        k = pl.program_id(2)

        @pl.when(k == 0)
        def _():
            acc_ref[...] = jnp.zeros_like(acc_ref)

        acc_ref[...] += _dot(_b16(a_ref[...]), _b16(b_ref[...]), dims)

        @pl.when(k == nk - 1)
        def _():
            o_ref[...] = acc_ref[...].astype(o_ref.dtype)

    return pl.pallas_call(
        kern_single if nk == 1 else kern_acc, name=name, grid=(M // tm, N // tn, nk),
        in_specs=[a_spec, b_spec],
        out_specs=pl.BlockSpec((tm, tn), lambda i, j, k: (i, j)),
        out_shape=jax.ShapeDtypeStruct((M, N), out_dtype),
        scratch_shapes=[] if nk == 1 else [pltpu.VMEM((tm, tn), F32)],
        compiler_params=_params(("parallel", "parallel", "arbitrary")),
    )(a, b)


def _rms_fwd(name, x, g, ts):
    S, D = x.shape

    def body(t, first, ins, outs, scratch):
        x_ref, g_ref = ins
        h_ref, r_ref = outs
        xv = x_ref[...]
        r = lax.rsqrt(jnp.mean(xv * xv, axis=1, keepdims=True) + EPS)
        h_ref[...] = (xv * r * g_ref[...]).astype(h_ref.dtype)
        r_ref[...] = r

    return _tiled(name, body, S // ts, [_rows(x, ts), _full(g.reshape(1, D))],
                  [_orow(S, (D,), BF16, ts), _orow(S, (1,), F32, ts)])


def _rms_bwd(name, dh, x, rstd, g, dres, ts):
    S, D = x.shape

    def body(t, first, ins, outs, scratch):
        dh_ref, x_ref, r_ref, g_ref, dres_ref = ins
        dx_ref, dg_ref = outs
        r = r_ref[...]
        xh = x_ref[...] * r
        dhv = dh_ref[...]
        dxh = dhv * g_ref[...]
        dx_ref[...] = dres_ref[...] + r * (dxh - xh * jnp.mean(dxh * xh, axis=1, keepdims=True))

        @pl.when(first)
        def _():
            dg_ref[...] = jnp.zeros_like(dg_ref)

        dg_ref[0:1, :] += jnp.sum(dhv * xh, axis=0, keepdims=True)

    dx, dg = _tiled(name, body, S // ts,
                    [_rows(dh, ts), _rows(x, ts), _rows(rstd, ts), _full(g.reshape(1, D)), _rows(dres, ts)],
                    [_orow(S, (D,), F32, ts), _oacc((8, D), F32)])
    return dx, dg[0]


def _loss_head(x, g, target, ts):
    S, D = x.shape

    def body(t, first, ins, outs, scratch):
        x_ref, g_ref, tgt_ref = ins
        dx_ref, dg_ref, loss_ref = outs
        xv = x_ref[...]
        gv = g_ref[...]
        r = lax.rsqrt(jnp.mean(xv * xv, axis=1, keepdims=True) + EPS)
        xh = xv * r
        err = xh * gv - tgt_ref[...]
        dy = err * (1.0 / D)
        dxh = dy * gv
        dx_ref[...] = r * (dxh - xh * jnp.mean(dxh * xh, axis=1, keepdims=True))

        @pl.when(first)
        def _():
            dg_ref[...] = jnp.zeros_like(dg_ref)
            loss_ref[...] = jnp.zeros_like(loss_ref)

        dg_ref[0:1, :] += jnp.sum(dy * xh, axis=0, keepdims=True)
        per_lane = jnp.sum(err * err, axis=0, keepdims=True)
        loss_ref[0:1, :] += per_lane * (0.5 / D)

    dx, dg, loss = _tiled("loss_head", body, S // ts,
                          [_rows(x, ts), _full(g.reshape(1, D)), _rows(target, ts)],
                          [_orow(S, (D,), F32, ts), _oacc((8, D), F32), _oacc((8, D), F32)])
    return dx, dg[0], loss[0]


def _scan_rows(x, length, seg, reverse=False):
    row = lax.broadcasted_iota(jnp.int32, x.shape, 0) % seg
    k = 1
    while k < seg:
        if reverse:
            x = x + jnp.where(row < seg - k, pltpu.roll(x, length - k, 0), 0.0)
        else:
            x = x + jnp.where(row >= k, pltpu.roll(x, k, 0), 0.0)
        k *= 2
    return x


def _fox_decay(z, b_fg128, ts):
    S = z.shape[0]

    def body(t, first, ins, outs, scratch):
        zs_ref, b_ref = ins
        hi_ref, mid_ref, lo_ref = outs
        (carry,) = scratch

        @pl.when(first)
        def _():
            carry[...] = jnp.zeros_like(carry)

        logf = -_softplus(-(zs_ref[...] + b_ref[...]))
        run = _scan_rows(logf, ts, ts) + carry[0:1, :]
        carry[0:1, :] = run[ts - 1:ts, :]
        f2 = run * LOG2E
        hi = f2.astype(BF16)
        r1 = f2 - hi.astype(F32)
        mid = r1.astype(BF16)
        lo = (r1 - mid.astype(F32)).astype(BF16)
        eye = _eye(128, BF16)
        hi_ref[...] = _dot(eye, hi, NT).astype(BF16)
        mid_ref[...] = _dot(eye, mid, NT).astype(BF16)
        lo_ref[...] = _dot(eye, lo, NT).astype(BF16)

    tcol = lambda dt: ((128, S), dt, (128, ts), lambda i: (0, i))
    return _tiled("fox_decay", body, S // ts,
                  [_cols(z, ts, 128, CB_SMALL), _full(b_fg128)],
                  [tcol(BF16), tcol(BF16), tcol(BF16)], scratch=[pltpu.VMEM((8, 128), F32)])


def _fox_decay_bwd(dfk_rows, dfq_rows, z, b_fg128, ts):
    S = z.shape[0]
    H = dfk_rows.shape[0]

    def body(t, first, ins, outs, scratch):
        dfk_ref, dfq_ref, zs_ref, b_ref = ins
        daf_ref, db_ref = outs
        (carry,) = scratch

        @pl.when(first)
        def _():
            carry[...] = jnp.zeros_like(carry)
            db_ref[...] = jnp.zeros_like(db_ref)

        r = lax.broadcasted_iota(jnp.int32, (H, 128), 0)
        c = lax.broadcasted_iota(jnp.int32, (H, 128), 1)
        place = jnp.where(r == c, 1.0, 0.0)
        df = _dot(dfk_ref[...] + dfq_ref[...], place, TN, HIGHEST)
        run = _scan_rows(df, ts, ts, reverse=True) + carry[0:1, :]
        carry[0:1, :] = run[0:1, :]
        daf = run * _sigmoid(-(zs_ref[...] + b_ref[...]))
        daf_ref[...] = daf
        db_ref[0:1, :] += jnp.sum(daf, axis=0, keepdims=True)

    rowsin = lambda a: (a, (H, ts), lambda i: (0, i))
    daf, db = _tiled("fox_decay_bwd", body, S // ts,
                     [rowsin(dfk_rows), rowsin(dfq_rows), _cols(z, ts, 128, CB_SMALL), _full(b_fg128)],
                     [_orow(S, (128,), F32, ts), _oacc((8, 128), F32)],
                     scratch=[pltpu.VMEM((8, 128), F32)], reverse=True)
    return daf, db[0]


FOX_AUG = 80


def _fox_fwd(q_aug, kT_aug, v_aug, tq):
    H, S, da = q_aug.shape
    dv = v_aug.shape[2]
    d = FOX_DIM
    tk = tq // 2
    qscale = (d ** -0.5) * LOG2E

    def kern(q_ref, kT_ref, v_ref, o_ref, lse_ref, qs_ref, s_buf, p_buf, m_scr, acc_scr):
        i = pl.program_id(1)
        col = lax.broadcasted_iota(jnp.int32, (1, da), 1)
        qb = _b16(q_ref[...] * jnp.where(col < d, qscale, 1.0))
        qs_ref[...] = qb

        def keys(t):
            return pl.ds(pl.multiple_of(t * tk, tk), tk)

        def stage(t, slot, mask_off, look_ahead):
            if look_ahead:
                s_buf[1 - slot] = _dot(qb, kT_ref[:, keys(t + 1)])
            pv = _dot(p_buf[1 - slot], v_ref[keys(jnp.maximum(t - 1, 0)), :])

            def scores():
                s = s_buf[slot]
                if mask_off is None:
                    return s
                r = lax.broadcasted_iota(jnp.int32, (tq, tk), 0)
                c = lax.broadcasted_iota(jnp.int32, (tq, tk), 1)
                return jnp.where(c + mask_off <= r, s, NEG)

            m = m_scr[...]
            m_new = jnp.maximum(m, jnp.max(scores(), axis=1, keepdims=True))
            alpha = jnp.exp2(m - m_new)
            p_buf[slot] = _b16(jnp.exp2(scores() - m_new))
            m_scr[...] = m_new
            acc_scr[...] = (acc_scr[...] + pv) * alpha

        s_buf[0] = _dot(qb, kT_ref[:, keys(0)])
        p_buf[1] = jnp.zeros((tq, tk), BF16)
        m_scr[...] = jnp.full((tq, 1), NEG, F32)
        acc_scr[...] = jnp.zeros((tq, dv), F32)

        def pair(n):
            stage(2 * n, 0, None, True)
            stage(2 * n + 1, 1, None, True)

        def quad(m, _):
            pair(2 * m)
            pair(2 * m + 1)
            return 0

        lax.fori_loop(0, i // 2, quad, 0)

        @pl.when(i % 2 == 1)
        def _():
            pair(i - 1)

        stage(2 * i, 0, 0, True)
        stage(2 * i + 1, 1, tk, False)
        acc = acc_scr[...] + _dot(p_buf[1], v_ref[keys(2 * i + 1), :])
        l = acc[:, d:d + 1]
        o_ref[...] = acc[:, :d] / l
        lse_ref[...] = _col_to_row(m_scr[...] + jnp.log(l) * LOG2E)

    return pl.pallas_call(
        kern, name="fox_fwd", grid=(H, S // tq),
        in_specs=[pl.BlockSpec((None, tq, da), lambda h, i: (h, i, 0)),
                  pl.BlockSpec((None, da, S), lambda h, i: (h, 0, 0)),
                  pl.BlockSpec((None, S, dv), lambda h, i: (h, 0, 0))],
        out_specs=[pl.BlockSpec((None, tq, d), lambda h, i: (h, i, 0)),
                   pl.BlockSpec((None, 1, tq), lambda h, i: (h, 0, i)),
                   pl.BlockSpec((None, tq, da), lambda h, i: (h, i, 0))],
        out_shape=[jax.ShapeDtypeStruct((H, S, d), F32), jax.ShapeDtypeStruct((H, 1, S), F32),
                   jax.ShapeDtypeStruct((H, S, da), BF16)],
        scratch_shapes=[pltpu.VMEM((2, tq, tk), F32), pltpu.VMEM((2, tq, tk), BF16),
                        pltpu.VMEM((tq, 1), F32), pltpu.VMEM((tq, dv), F32)],
        compiler_params=_params(("parallel", "arbitrary")),
    )(q_aug, kT_aug, v_aug)


def _fox_bwd(qs, k_aug, kT, v, do, lse_row, delta_row, tq):
    H, S, da = qs.shape
    d = FOX_DIM
    tk = tq
    nq = S // tq
    scale = d ** -0.5

    ts2 = tq // 2
    last = 2 * nq - 1

    def kern(q_ref, k_ref, kT_ref, v_ref, do_ref, lse_ref, dl_ref,
             dqT_ref, dk_ref, dv_ref, dfk_ref, dfq_ref,
             kq_buf, dp_buf, pb_buf, ds_buf, dk_scr, dv_scr, dfk_scr):
        j = pl.program_id(1)

        @pl.when(j == 0)
        def _():
            dqT_ref[...] = jnp.zeros_like(dqT_ref)
            dfq_ref[...] = jnp.zeros_like(dfq_ref)

        kb = k_ref[...]
        kTb = kT_ref[...]
        vb = v_ref[:, :d]
        dk_scr[...] = jnp.zeros_like(dk_scr)
        dv_scr[...] = jnp.zeros_like(dv_scr)
        dfk_scr[...] = jnp.zeros_like(dfk_scr)

        def queries(t):
            return pl.ds(pl.multiple_of(t * ts2, ts2), ts2)

        def products(t, slot):
            rows = queries(t)
            kq_buf[slot] = _dot(kb, q_ref[rows, :], NT)
            dp_buf[slot] = _dot(vb, do_ref[rows, :], NT)

        def pointwise(t, slot, mask_off):
            rows = queries(t)
            sT = kq_buf[slot]
            if mask_off is not None:
                r = lax.broadcasted_iota(jnp.int32, (tk, ts2), 0)
                c = lax.broadcasted_iota(jnp.int32, (tk, ts2), 1)
                sT = jnp.where(r <= c + mask_off, sT, NEG)
            pT = jnp.exp2(sT - lse_ref[:, rows])
            dsT = pT * (dp_buf[slot] - dl_ref[:, rows])
            pb_buf[slot] = _b16(pT)
            ds_buf[slot] = _b16(dsT)
            dfk_scr[...] -= jnp.sum(dsT, axis=1, keepdims=True)
            dfq_ref[:, rows] += jnp.sum(dsT, axis=0, keepdims=True)

        def accumulate(t, slot):
            rows = queries(t)
            dsb = ds_buf[slot]
            dv_scr[...] += _dot(pb_buf[slot], do_ref[rows, :])
            dk_scr[...] += _dot(dsb, q_ref[rows, :])
            dqT_ref[:, rows] += _dot(kTb, dsb) * scale

        def stage(t, slot, mask_off, has_prev):
            products(jnp.minimum(t + 1, last), 1 - slot)
            if has_prev:
                accumulate(t - 1, 1 - slot)
            pointwise(t, slot, mask_off)

        products(2 * j, 0)
        stage(2 * j, 0, 0, False)
        stage(2 * j + 1, 1, ts2, True)

        def pair(n):
            stage(2 * n, 0, None, True)
            stage(2 * n + 1, 1, None, True)

        def quad(m, _):
            pair(j + 1 + 2 * m)
            pair(j + 2 + 2 * m)
            return 0

        n_rest = nq - 1 - j
        lax.fori_loop(0, n_rest // 2, quad, 0)

        @pl.when(n_rest % 2 == 1)
        def _():
            pair(nq - 1)

        accumulate(last, 1)
        dk_ref[...] = dk_scr[:, :d] * LN2
        dv_ref[...] = dv_scr[...]
        dfk_ref[...] = _col_to_row(dfk_scr[...])

    tile = lambda h, j: (h, j, 0)
    whole = lambda h, j: (h, 0, 0)
    rowtile = lambda h, j: (h, 0, j)
    return pl.pallas_call(
        kern, name="fox_bwd", grid=(H, S // tk),
        in_specs=[pl.BlockSpec((None, S, da), whole),
                  pl.BlockSpec((None, tk, da), tile),
                  pl.BlockSpec((None, d, tk), lambda h, j: (h, 0, j)),
                  pl.BlockSpec((None, tk, 128), tile),
                  pl.BlockSpec((None, S, d), whole),
                  pl.BlockSpec((None, 1, S), whole),
                  pl.BlockSpec((None, 1, S), whole)],
        out_specs=[pl.BlockSpec((None, d, S), whole),
                   pl.BlockSpec((None, tk, d), tile),
                   pl.BlockSpec((None, tk, d), tile),
                   pl.BlockSpec((None, 1, tk), rowtile),
                   pl.BlockSpec((None, 1, S), whole)],
        out_shape=[jax.ShapeDtypeStruct((H, d, S), F32), jax.ShapeDtypeStruct((H, S, d), F32),
                   jax.ShapeDtypeStruct((H, S, d), F32), jax.ShapeDtypeStruct((H, 1, S), F32),
                   jax.ShapeDtypeStruct((H, 1, S), F32)],
        scratch_shapes=[pltpu.VMEM((2, tk, ts2), F32), pltpu.VMEM((2, tk, ts2), F32),
                        pltpu.VMEM((2, tk, ts2), BF16), pltpu.VMEM((2, tk, ts2), BF16),
                        pltpu.VMEM((tk, da), F32), pltpu.VMEM((tk, d), F32), pltpu.VMEM((tk, 1), F32)],
        compiler_params=_params(("parallel", "arbitrary")),
    )(qs, k_aug, kT, v, do, lse_row, delta_row)


def _heads_major(a, H, d):
    S = a.shape[0]
    return a.reshape(S, H, d).transpose(1, 0, 2)


def _heads_minor(a):
    H, S, d = a.shape
    return a.transpose(1, 0, 2).reshape(S, H * d)


def _lane_pick(x128, lane):
    return x128[:, lane:lane + 1]


def _l2_fwd(y):
    return lax.rsqrt(jnp.sum(y * y, axis=1, keepdims=True) + EPS)


def _gdn_prep(z, conv_w, a128, dt128, ts):
    S = z.shape[0]
    C3 = 3 * WIDTH
    hb = ts // 8

    def body(t, first, ins, outs, scratch):
        x_ref, halo_ref, zs_ref, w_ref, a_ref, dt_ref = ins
        qkv_ref, c_ref, gb_ref, gbT_ref = outs
        halo = jnp.where(t > 0, halo_ref[...], 0.0)
        xe = jnp.concatenate([halo, x_ref[...]], axis=0)
        w = w_ref[...]
        c = w[3:4, :] * xe[8:, :]
        for back in (1, 2, 3):
            c = c + w[3 - back:4 - back, :] * pltpu.roll(xe, back, 0)[8:, :]
        c_ref[...] = c
        y = c * _sigmoid(c)
        for h in range(GDN_HEADS):
            lo = h * GDN_DIM
            yq = y[:, lo:lo + GDN_DIM]
            qkv_ref[:, lo:lo + GDN_DIM] = yq * (_l2_fwd(yq) * (GDN_DIM ** -0.5))
            yk = y[:, WIDTH + lo:WIDTH + lo + GDN_DIM]
            qkv_ref[:, WIDTH + lo:WIDTH + lo + GDN_DIM] = yk * _l2_fwd(yk)
        qkv_ref[:, 2 * WIDTH:] = y[:, 2 * WIDTH:]
        zs = zs_ref[...]
        lane = lax.broadcasted_iota(jnp.int32, zs.shape, 1)
        g = -jnp.exp(a_ref[...]) * _softplus(zs + dt_ref[...])
        G = _scan_rows(g, ts, CHUNK)
        beta = _sigmoid(zs)
        out = jnp.where(lane < 8, pltpu.roll(g, 128 - LANE_BA, 1), jnp.where(lane < LANE_BB, G, beta))
        gb_ref[...] = out
        gbT_ref[...] = _transpose_exact(out)

    x_in = (z, (ts, C3), lambda i: (i, CB_BQKV))
    halo_in = (z, (8, C3), lambda i: (jnp.maximum(i * hb - 1, 0), CB_BQKV))
    return _tiled("gdn_prep", body, S // ts,
                  [x_in, halo_in, _cols(z, ts, 128, CB_SMALL), _full(conv_w), _full(a128), _full(dt128)],
                  [_orow(S, (C3,), F32, ts), _orow(S, (C3,), F32, ts), _orow(S, (128,), F32, ts),
                   ((128, S), F32, (128, ts), lambda i: (0, i))])


def _chunk_masks(nc):
    r = lax.broadcasted_iota(jnp.int32, (nc, CHUNK, CHUNK), 1)
    c = lax.broadcasted_iota(jnp.int32, (nc, CHUNK, CHUNK), 2)
    return c <= r, c < r, c == r


def _chunk_local(qh, kh, vh, Gc, Gr, beta):
    nc = qh.shape[0]
    incl, strict, _ = _chunk_masks(nc)
    gamma = jnp.exp(jnp.where(incl, Gc - Gr, NEG))
    kb = kh * beta
    P = _bdot(_b16(kb), _b16(kh), 2, 2)
    Qk = _bdot(_b16(qh), _b16(kh), 2, 2)
    eG = jnp.exp(Gc)
    Gl = Gc[:, CHUNK - 1:CHUNK, :]
    edec = jnp.exp(Gl - Gc)
    return incl, strict, gamma, kb, P, Qk, eG, edec


def _gdn_local_fwd(qkv, gb, grow, ts):
    S = qkv.shape[0]
    nc = ts // CHUNK

    def body(t, first, ins, outs, scratch):
        q_ref, k_ref, v_ref, gb_ref, gr_ref = ins
        u_ref, w_ref, qd_ref, kd_ref, aqk_ref, T_ref = outs
        gbv = gb_ref[...]
        for h in range(GDN_HEADS):
            lo = h * GDN_DIM
            qh = q_ref[:, lo:lo + GDN_DIM].reshape(nc, CHUNK, GDN_DIM)
            kh = k_ref[:, lo:lo + GDN_DIM].reshape(nc, CHUNK, GDN_DIM)
            vh = v_ref[:, lo:lo + GDN_DIM].reshape(nc, CHUNK, GDN_DIM)
            Gc = _lane_pick(gbv, LANE_BA + h).reshape(nc, CHUNK, 1)
            beta = _lane_pick(gbv, LANE_BB + h).reshape(nc, CHUNK, 1)
            Gr = gr_ref[h].reshape(nc, 1, CHUNK)
            incl, strict, gamma, kb, P, Qk, eG, edec = _chunk_local(qh, kh, vh, Gc, Gr, beta)
            A = jnp.where(strict, P * gamma, 0.0)
            _, _, eye = _chunk_masks(nc)
            T = jnp.where(eye, 1.0, 0.0) - A
            X = A
            for _ in range(5):
                X = _bdot(X, X, 2, 1, PREC_UT)
                T = T + _bdot(T, X, 2, 1, PREC_UT)
            u = _bdot(T, vh * beta, 2, 1, PREC_UT)
            w = _bdot(T, kb * eG, 2, 1, PREC_UT)
            u_ref[:, lo:lo + GDN_DIM] = u.reshape(ts, GDN_DIM)
            w_ref[:, lo:lo + GDN_DIM] = w.reshape(ts, GDN_DIM)
            qd_ref[:, lo:lo + GDN_DIM] = (qh * eG).reshape(ts, GDN_DIM)
            kd_ref[:, lo:lo + GDN_DIM] = (kh * edec).reshape(ts, GDN_DIM)
            aqk_ref[h] = jnp.where(incl, Qk * gamma, 0.0).reshape(ts, CHUNK)
            T_ref[h] = T.reshape(ts, CHUNK)

    wide = _orow(S, (WIDTH,), F32, ts)
    perhead = ((GDN_HEADS, S, CHUNK), F32, (GDN_HEADS, ts, CHUNK), lambda i: (0, i, 0))
    return _tiled("gdn_local_fwd", body, S // ts,
                  [_cols(qkv, ts, WIDTH, 0), _cols(qkv, ts, WIDTH, 1), _cols(qkv, ts, WIDTH, 2),
                   _rows(gb, ts), (grow, (GDN_HEADS, nc, CHUNK), lambda i: (0, i, 0))],
                  [wide, wide, wide, wide, perhead, perhead])


def _gdn_scan_fwd(u, w, qd, kd, aqk, gb, ts):
    S = u.shape[0]
    nc = ts // CHUNK
    N = S // CHUNK

    def body(t, first, ins, outs, scratch):
        u_ref, w_ref, qd_ref, kd_ref, aqk_ref, gb_ref = ins
        o_ref, vn_ref, st_ref = outs
        (state,) = scratch

        @pl.when(first)
        def _():
            state[...] = jnp.zeros_like(state)

        def chunk(c, _):
            r0 = pl.multiple_of(c * CHUNK, CHUNK)
            rows = pl.ds(r0, CHUNK)
            glast = gb_ref[pl.ds(r0 + CHUNK - 1, 1), :]
            heads = range(GDN_HEADS)
            cols = [slice(h * GDN_DIM, (h + 1) * GDN_DIM) for h in heads]
            S_old = [state[h] for h in heads]
            u_h = [u_ref[rows, cols[h]] for h in heads]
            w_h = [_b16(w_ref[rows, cols[h]]) for h in heads]
            qd_h = [_b16(qd_ref[rows, cols[h]]) for h in heads]
            kd_h = [_b16(kd_ref[rows, cols[h]]) for h in heads]
            aqk_h = [_b16(aqk_ref[h, rows, :]) for h in heads]
            S_new, o_h, vn_h = [], [], []
            for h in heads:
                Sb = _b16(S_old[h])
                both = _dot(jnp.concatenate([w_h[h], qd_h[h]], axis=0), Sb)
                vn = u_h[h] - both[:CHUNK]
                vnb = _b16(vn)
                o_h.append(both[CHUNK:] + _dot(aqk_h[h], vnb))
                egl = jnp.exp(glast[:, LANE_BA + h:LANE_BA + h + 1])
                S_new.append(S_old[h] * egl + _dot(kd_h[h], vnb, TN))
                vn_h.append(vn)
            for h in heads:
                st_ref[c, h] = S_old[h]
                state[h] = S_new[h]
                o_ref[rows, cols[h]] = o_h[h]
                vn_ref[rows, cols[h]] = vn_h[h]
            return 0

        lax.fori_loop(0, nc, chunk, 0)

    wide_in = lambda a: _rows(a, ts)
    wide = _orow(S, (WIDTH,), F32, ts)
    states = ((N, GDN_HEADS, GDN_DIM, GDN_DIM), F32, (nc, GDN_HEADS, GDN_DIM, GDN_DIM),
              lambda i: (i, 0, 0, 0))
    return _tiled("gdn_scan_fwd", body, S // ts,
                  [wide_in(u), wide_in(w), wide_in(qd), wide_in(kd),
                   (aqk, (GDN_HEADS, ts, CHUNK), lambda i: (0, i, 0)), _rows(gb, ts)],
                  [wide, wide, states],
                  scratch=[pltpu.VMEM((GDN_HEADS, GDN_DIM, GDN_DIM), F32)])


def _gdn_scan_bwd(do, w, qd, kd, aqk, vn, states, gb, ts):
    S = do.shape[0]
    nc = ts // CHUNK
    N = S // CHUNK

    def body(t, first, ins, outs, scratch):
        do_ref, w_ref, qd_ref, kd_ref, aqk_ref, vn_ref, st_ref, gb_ref = ins
        du_ref, dw_ref, dqd_ref, dkd_ref, daqk_ref, dgl_ref = outs
        (dstate,) = scratch

        @pl.when(first)
        def _():
            dstate[...] = jnp.zeros_like(dstate)

        r = lax.broadcasted_iota(jnp.int32, (CHUNK, CHUNK), 0)
        cc = lax.broadcasted_iota(jnp.int32, (CHUNK, CHUNK), 1)
        incl = cc <= r
        lane = lax.broadcasted_iota(jnp.int32, (1, 128), 1)

        def chunk(k, _):
            c = nc - 1 - k
            r0 = pl.multiple_of(c * CHUNK, CHUNK)
            rows = pl.ds(r0, CHUNK)
            glast = gb_ref[pl.ds(r0 + CHUNK - 1, 1), :]
            dgl_row = jnp.zeros((1, 128), F32)
            heads = range(GDN_HEADS)
            cols = [slice(h * GDN_DIM, (h + 1) * GDN_DIM) for h in heads]
            S_h = [st_ref[c, h] for h in heads]
            dS_h = [dstate[h] for h in heads]
            do_h = [_b16(do_ref[rows, cols[h]]) for h in heads]
            aqk_h = [_b16(aqk_ref[h, rows, :]) for h in heads]
            vn_h = [_b16(vn_ref[rows, cols[h]]) for h in heads]
            kd_h = [_b16(kd_ref[rows, cols[h]]) for h in heads]
            qd_h = [_b16(qd_ref[rows, cols[h]]) for h in heads]
            w_h = [_b16(w_ref[rows, cols[h]]) for h in heads]
            res = []
            for h in heads:
                Sb, dSb, dob, vnb = _b16(S_h[h]), _b16(dS_h[h]), do_h[h], vn_h[h]
                dvn = _dot(aqk_h[h], dob, TN) + _dot(kd_h[h], dSb)
                dvnb = _b16(dvn)
                daqk = jnp.where(incl, _dot(dob, vnb, NT), 0.0)
                both = jnp.concatenate([dob, dvnb], axis=0)
                by_state = _dot(both, Sb, NT)
                dqd = by_state[:CHUNK]
                dw = -by_state[CHUNK:]
                dkd = _dot(vnb, dSb, NT)
                egl = jnp.exp(glast[:, LANE_BA + h:LANE_BA + h + 1])
                dgl = egl * jnp.sum(jnp.sum(dS_h[h] * S_h[h], axis=1, keepdims=True), axis=0,
                                    keepdims=True)
                dgl_row = jnp.where(lane == h, dgl, dgl_row)
                dS_new = _dot(jnp.concatenate([qd_h[h], -w_h[h]], axis=0), both, TN) + egl * dS_h[h]
                res.append((daqk, dqd, dkd, dw, dvn, dS_new))
            for h in heads:
                daqk, dqd, dkd, dw, dvn, dS_new = res[h]
                daqk_ref[h, rows, :] = daqk
                dqd_ref[rows, cols[h]] = dqd
                dkd_ref[rows, cols[h]] = dkd
                dw_ref[rows, cols[h]] = dw
                du_ref[rows, cols[h]] = dvn
                dstate[h] = dS_new
            dgl_ref[pl.ds(c, 1), :] = dgl_row
            return 0

        lax.fori_loop(0, nc, chunk, 0)

    wide_in = lambda a: _rows(a, ts)
    wide = _orow(S, (WIDTH,), F32, ts)
    perhead_in = lambda a: (a, (GDN_HEADS, ts, CHUNK), lambda i: (0, i, 0))
    perhead = ((GDN_HEADS, S, CHUNK), F32, (GDN_HEADS, ts, CHUNK), lambda i: (0, i, 0))
    return _tiled("gdn_scan_bwd", body, S // ts,
                  [wide_in(do), wide_in(w), wide_in(qd), wide_in(kd), perhead_in(aqk), wide_in(vn),
                   (states, (nc, GDN_HEADS, GDN_DIM, GDN_DIM), lambda i: (i, 0, 0, 0)), _rows(gb, ts)],
                  [wide, wide, wide, wide, perhead, ((N, 128), F32, (nc, 128), lambda i: (i, 0))],
                  scratch=[pltpu.VMEM((GDN_HEADS, GDN_DIM, GDN_DIM), F32)], reverse=True)


def _gdn_local_bwd(qkv, gb, grow, T, du, dw, dqd, dkd, daqk, dgl, ts):
    S = qkv.shape[0]
    nc = ts // CHUNK

    def body(t, first, ins, outs, scratch):
        (q_ref, k_ref, v_ref, gb_ref, gr_ref, T_ref, du_ref, dw_ref, dqd_ref, dkd_ref,
         daqk_ref, dgl_ref) = ins
        dqkv_ref, dgb_ref = outs
        gbv = gb_ref[...]
        dglv = dgl_ref[...]
        lane = lax.broadcasted_iota(jnp.int32, (ts, 128), 1)
        dG_all = jnp.zeros((ts, 128), F32)
        dbeta_all = jnp.zeros((ts, 128), F32)
        for h in range(GDN_HEADS):
            lo = h * GDN_DIM
            cols = slice(lo, lo + GDN_DIM)
            r3 = lambda ref: ref[:, cols].reshape(nc, CHUNK, GDN_DIM)
            qh, kh, vh = r3(q_ref), r3(k_ref), r3(v_ref)
            duh, dwh, dqdh, dkdh = r3(du_ref), r3(dw_ref), r3(dqd_ref), r3(dkd_ref)
            Gc = _lane_pick(gbv, LANE_BA + h).reshape(nc, CHUNK, 1)
            beta = _lane_pick(gbv, LANE_BB + h).reshape(nc, CHUNK, 1)
            Gr = gr_ref[h].reshape(nc, 1, CHUNK)
            Th = T_ref[h].reshape(nc, CHUNK, CHUNK)
            daq = daqk_ref[h].reshape(nc, CHUNK, CHUNK)
            incl, strict, gamma, kb, P, Qk, eG, edec = _chunk_local(qh, kh, vh, Gc, Gr, beta)
            _, _, eye = _chunk_masks(nc)
            vb = vh * beta
            kbg = kb * eG
            dvb = _bdot(Th, duh, 1, 1, PREC_UT)
            dkbg = _bdot(Th, dwh, 1, 1, PREC_UT)
            dT = _bdot(duh, vb, 2, 2, PREC_UT) + _bdot(dwh, kbg, 2, 2, PREC_UT)
            M1 = _bdot(Th, dT, 1, 1, PREC_UT)
            dA = jnp.where(strict, -_bdot(M1, Th, 2, 2, PREC_UT), 0.0)
            dP = dA * gamma
            dQ = daq * gamma
            dgam = (dA * P + daq * Qk) * gamma
            dPb, dQb = _b16(dP), _b16(dQ)
            khb, qhb, kbb = _b16(kh), _b16(qh), _b16(kb)
            dq = _bdot(dQb, khb, 2, 1) + dqdh * eG
            dkb = _bdot(dPb, khb, 2, 1) + dkbg * eG
            dk = (_bdot(dQb, qhb, 1, 1) + _bdot(dPb, kbb, 1, 1) + dkdh * edec + dkb * beta)
            dbeta = (jnp.sum(dkb * kh, axis=2, keepdims=True) + jnp.sum(dvb * vh, axis=2, keepdims=True))
            dv = dvb * beta
            col_as_col = jnp.sum(jnp.where(eye, jnp.sum(dgam, axis=1, keepdims=True), 0.0),
                                 axis=2, keepdims=True)
            kd_term = jnp.sum(dkdh * kh * edec, axis=2, keepdims=True)
            dG = (jnp.sum(dgam, axis=2, keepdims=True) - col_as_col
                  + jnp.sum(dqdh * qh * eG, axis=2, keepdims=True)
                  + jnp.sum(dkbg * kbg, axis=2, keepdims=True) - kd_term)
            dgl_h = dglv[:, h:h + 1].reshape(nc, 1, 1) + jnp.sum(kd_term, axis=1, keepdims=True)
            last = lax.broadcasted_iota(jnp.int32, (nc, CHUNK, 1), 1) == CHUNK - 1
            dG = dG + jnp.where(last, dgl_h, 0.0)
            dqkv_ref[:, cols] = dq.reshape(ts, GDN_DIM)
            dqkv_ref[:, WIDTH + lo:WIDTH + lo + GDN_DIM] = dk.reshape(ts, GDN_DIM)
            dqkv_ref[:, 2 * WIDTH + lo:2 * WIDTH + lo + GDN_DIM] = dv.reshape(ts, GDN_DIM)
            dG_all = jnp.where(lane == LANE_BA + h, dG.reshape(ts, 1), dG_all)
            dbeta_all = jnp.where(lane == LANE_BB + h, dbeta.reshape(ts, 1), dbeta_all)
        dg_all = _scan_rows(dG_all, ts, CHUNK, reverse=True)
        dgb_ref[...] = jnp.where(lane < LANE_BB, dg_all, dbeta_all)

    wide_in = lambda a: _rows(a, ts)
    perhead_in = lambda a: (a, (GDN_HEADS, ts, CHUNK), lambda i: (0, i, 0))
    return _tiled("gdn_local_bwd", body, S // ts,
                  [_cols(qkv, ts, WIDTH, 0), _cols(qkv, ts, WIDTH, 1), _cols(qkv, ts, WIDTH, 2),
                   _rows(gb, ts), (grow, (GDN_HEADS, nc, CHUNK), lambda i: (0, i, 0)), perhead_in(T),
                   wide_in(du), wide_in(dw), wide_in(dqd), wide_in(dkd), perhead_in(daqk),
                   (dgl, (nc, 128), lambda i: (i, 0))],
                  [_orow(S, (3 * WIDTH,), F32, ts), _orow(S, (128,), F32, ts)])


def _gdn_prep_bwd(dqkv, dgb, cpre, z, conv_w, a128, dt128, dz, ts):
    S = z.shape[0]
    C3 = 3 * WIDTH
    hb = ts // 8
    n_tiles = S // ts

    def dpre(dq, c):
        y, dsil = _silu_and_grad(c)
        parts = []
        for h in range(GDN_HEADS):
            lo = h * GDN_DIM
            yq = y[:, lo:lo + GDN_DIM]
            rq = _l2_fwd(yq)
            nq = yq * rq
            dn = dq[:, lo:lo + GDN_DIM] * (GDN_DIM ** -0.5)
            parts.append(rq * (dn - nq * jnp.sum(dn * nq, axis=1, keepdims=True)))
        for h in range(GDN_HEADS):
            lo = WIDTH + h * GDN_DIM
            yk = y[:, lo:lo + GDN_DIM]
            rk = _l2_fwd(yk)
            nk = yk * rk
            dn = dq[:, lo:lo + GDN_DIM]
            parts.append(rk * (dn - nk * jnp.sum(dn * nk, axis=1, keepdims=True)))
        parts.append(dq[:, 2 * WIDTH:])
        return jnp.concatenate(parts, axis=1) * dsil

    def body(t, first, ins, outs, scratch):
        (dq_ref, dqn_ref, c_ref, cn_ref, x_ref, xp_ref, zs_ref, dgb_ref, w_ref, a_ref, dt_ref) = ins
        dx_ref, dzs_ref, dw_ref, dad_ref = outs

        @pl.when(first)
        def _():
            dw_ref[...] = jnp.zeros_like(dw_ref)
            dad_ref[...] = jnp.zeros_like(dad_ref)

        dc = dpre(dq_ref[...], c_ref[...])
        dcn = jnp.where(t < n_tiles - 1, dpre(dqn_ref[...], cn_ref[...]), 0.0)
        dce = jnp.concatenate([dc, dcn], axis=0)
        w = w_ref[...]
        dx = w[3:4, :] * dc
        for back in (1, 2, 3):
            dx = dx + w[3 - back:4 - back, :] * pltpu.roll(dce, ts + 8 - back, 0)[:ts, :]
        dx_ref[...] = _b16(dx)
        halo = jnp.where(t > 0, xp_ref[...], 0.0)
        xe = jnp.concatenate([halo, x_ref[...]], axis=0)
        dw_ref[3:4, :] += jnp.sum(dc * xe[8:, :], axis=0, keepdims=True)
        for back in (1, 2, 3):
            dw_ref[3 - back:4 - back, :] += jnp.sum(dc * pltpu.roll(xe, back, 0)[8:, :], axis=0,
                                                     keepdims=True)
        zs = zs_ref[...]
        dgb = dgb_ref[...]
        lane = lax.broadcasted_iota(jnp.int32, zs.shape, 1)
        arg = zs + dt_ref[...]
        nega = -jnp.exp(a_ref[...])
        dba = dgb * nega * _sigmoid(arg)
        beta = _sigmoid(zs)
        dbb = dgb * beta * (1.0 - beta)
        dzs_ref[...] = jnp.where((lane >= LANE_BA) & (lane < LANE_BB), dba,
                                 jnp.where((lane >= LANE_BB) & (lane < LANE_BB + 4), dbb, 0.0))
        dad_ref[0:1, :] += jnp.sum(dgb * nega * _softplus(arg), axis=0, keepdims=True)
        dad_ref[1:2, :] += jnp.sum(dba, axis=0, keepdims=True)

    nxt = lambda i: (jnp.minimum((i + 1) * hb, S // 8 - 1), 0)
    prv = lambda i: (jnp.maximum(i * hb - 1, 0), CB_BQKV)
    return _tiled("gdn_prep_bwd", body, n_tiles,
                  [_rows(dqkv, ts), (dqkv, (8, C3), nxt), _rows(cpre, ts), (cpre, (8, C3), nxt),
                   (z, (ts, C3), lambda i: (i, CB_BQKV)), (z, (8, C3), prv),
                   _cols(z, ts, 128, CB_SMALL), _rows(dgb, ts), _full(conv_w), _full(a128), _full(dt128)],
                  [((S, N_AL), BF16, (ts, C3), lambda i: (i, CB_BQKV)), _orow(S, (128,), F32, ts),
                   _oacc((8, C3), F32), _oacc((8, 128), F32)],
                  fill=(dz, 0))


def _mem_attn_fwd(z, mk, mv, ts):
    S = z.shape[0]

    def body(t, first, ins, outs, scratch):
        q_ref, mk_ref, mv_ref = ins
        (o_ref,) = outs
        for h in range(MEM_HEADS):
            cols = slice(h * MEM_DIM, (h + 1) * MEM_DIM)
            s = _dot(_b16(q_ref[:, cols]), _b16(mk_ref[:, cols]), NT) * (MEM_DIM ** -0.5)
            m = jnp.max(s, axis=1, keepdims=True)
            p = jnp.exp(s - m)
            p = p / jnp.sum(p, axis=1, keepdims=True)
            o_ref[:, cols] = _dot(_b16(p), _b16(mv_ref[:, cols]))

    (o,) = _tiled("mem_attn_fwd", body, S // ts, [_cols(z, ts, WIDTH, CB_MQ), _full(mk), _full(mv)],
                  [_orow(S, (WIDTH,), F32, ts)])
    return o


def _mem_attn_bwd(do, z, mk, mv, dz, ts):
    S = z.shape[0]
    M = mk.shape[0]

    def body(t, first, ins, outs, scratch):
        do_ref, q_ref, mk_ref, mv_ref = ins
        dq_ref, dmk_ref, dmv_ref = outs

        @pl.when(first)
        def _():
            dmk_ref[...] = jnp.zeros_like(dmk_ref)
            dmv_ref[...] = jnp.zeros_like(dmv_ref)

        scale = MEM_DIM ** -0.5
        for h in range(MEM_HEADS):
            cols = slice(h * MEM_DIM, (h + 1) * MEM_DIM)
            qb = _b16(q_ref[:, cols])
            kb = _b16(mk_ref[:, cols])
            dob = _b16(do_ref[:, cols])
            s = _dot(qb, kb, NT) * scale
            m = jnp.max(s, axis=1, keepdims=True)
            p = jnp.exp(s - m)
            p = p / jnp.sum(p, axis=1, keepdims=True)
            dmv_ref[:, cols] += _dot(_b16(p), dob, TN)
            dp = _dot(dob, _b16(mv_ref[:, cols]), NT)
            ds = p * (dp - jnp.sum(dp * p, axis=1, keepdims=True)) * scale
            dsb = _b16(ds)
            dq_ref[:, cols] = _b16(_dot(dsb, kb))
            dmk_ref[:, cols] += _dot(dsb, qb, TN)

    return _tiled("mem_attn_bwd", body, S // ts,
                  [_rows(do, ts), _cols(z, ts, WIDTH, CB_MQ), _full(mk), _full(mv)],
                  [((S, N_AL), BF16, (ts, WIDTH), lambda i: (i, CB_MQ)), _oacc((M, WIDTH), F32),
                   _oacc((M, WIDTH), F32)],
                  fill=(dz, 0))


def _head_norm(ob, g):
    xs, rs = [], []
    for h in range(GDN_HEADS):
        o = ob[:, h * GDN_DIM:(h + 1) * GDN_DIM]
        r = lax.rsqrt(jnp.mean(o * o, axis=1, keepdims=True) + EPS)
        xs.append(o * r)
        rs.append(r)
    return xs, rs


def _merge_fwd(x, z, o_a, o_b, o_m, gdn_g, b_merge, wb, wout, ts):
    S, D = x.shape

    def body(t, first, ins, outs, scratch):
        (x_ref, g_ref, oa_ref, az_ref, ob_ref, bz_ref, om_ref, mz_ref, gg_ref, bm_ref, wb_ref,
         wo_ref) = ins
        xo_ref, ya_ref, yb_ref, ym_ref, mg_ref = outs
        ya = oa_ref[...] * _silu_and_grad(az_ref[...])[0]
        xs, _ = _head_norm(ob_ref[...], None)
        nb = jnp.concatenate([xh * gg_ref[...] for xh in xs], axis=1)
        yb = nb * _silu_and_grad(bz_ref[...])[0]
        ym = om_ref[...] * _silu_and_grad(mz_ref[...])[0]
        merged = jnp.zeros((ts, D), F32)
        for n, (y, y_ref) in enumerate(((ya, ya_ref), (yb, yb_ref), (ym, ym_ref))):
            yb16 = _b16(y)
            y_ref[...] = yb16
            gate = _sigmoid(g_ref[:, n * D:(n + 1) * D] + bm_ref[:, n * D:(n + 1) * D])
            merged = merged + gate * _dot(yb16, wb_ref[n])
        mb = _b16(merged)
        mg_ref[...] = mb
        xo_ref[...] = x_ref[...] + _dot(mb, wo_ref[...])

    half = lambda a: _rows(a, ts)
    return _tiled("merge_fwd", body, S // ts,
                  [_rows(x, ts), _cols(z, ts, 3 * D, CB_GATES), half(o_a), _cols(z, ts, WIDTH, CB_AZ),
                   half(o_b), _cols(z, ts, WIDTH, CB_BZ), half(o_m), _cols(z, ts, WIDTH, CB_MZ),
                   _full(gdn_g.reshape(1, GDN_DIM)), _full(b_merge.reshape(1, 3 * D)), _full(wb), _full(wout)],
                  [_orow(S, (D,), F32, ts), _orow(S, (WIDTH,), BF16, ts), _orow(S, (WIDTH,), BF16, ts),
                   _orow(S, (WIDTH,), BF16, ts), _orow(S, (D,), BF16, ts)])


def _merge_bwd(dout, z, o_a, o_b, o_m, ya, yb, ym, gdn_g, b_merge, wb, wout, hsum, ts):
    S, D = dout.shape

    def body(t, first, ins, outs, scratch):
        (do_ref, g_ref, oa_ref, az_ref, ob_ref, bz_ref, om_ref, mz_ref, ya_ref, yb_ref, ym_ref,
         gg_ref, bm_ref, wb_ref, wo_ref, hs_ref) = ins
        (dg_ref, dpa_ref, dpb_ref, dpm_ref, doa_ref, dob_ref, dom_ref, dl_ref, dbm_ref, dgg_ref) = outs
        G3 = 3 * D

        @pl.when(first)
        def _():
            dbm_ref[...] = jnp.zeros_like(dbm_ref)
            dgg_ref[...] = jnp.zeros_like(dgg_ref)

        dmerged = _dot(_b16(do_ref[...]), wo_ref[...], NT)
        dys = []
        for n, (y_ref, dp_ref) in enumerate(((ya_ref, dpa_ref), (yb_ref, dpb_ref), (ym_ref, dpm_ref))):
            sl = slice(n * D, (n + 1) * D)
            gate = _sigmoid(g_ref[:, sl] + bm_ref[:, sl])
            proj = _dot(y_ref[...], wb_ref[n])
            dproj = _b16(gate * dmerged)
            dp_ref[...] = dproj
            dgp = dmerged * proj * gate * (1.0 - gate)
            dg_ref[:, sl] = dgp.astype(dg_ref.dtype)
            dbm_ref[0:1, sl] += jnp.sum(dgp, axis=0, keepdims=True)
            dys.append(_dot(dproj, wb_ref[n], NT))
        dya, dyb, dym = dys
        sa, dsa = _silu_and_grad(az_ref[...])
        oa = oa_ref[...]
        doa = dya * sa
        doa_ref[...] = doa
        dg_ref[:, G3:G3 + WIDTH] = _b16(dya * oa * dsa)
        dl_ref[...] = _dot(hs_ref[...], doa * oa, NT, HIGHEST)
        sm, dsm = _silu_and_grad(mz_ref[...])
        dom_ref[...] = dym * sm
        dg_ref[:, G3 + 2 * WIDTH:G3 + 3 * WIDTH] = _b16(dym * om_ref[...] * dsm)
        sb, dsb = _silu_and_grad(bz_ref[...])
        xs, rs = _head_norm(ob_ref[...], None)
        gg = gg_ref[...]
        dgg = jnp.zeros((1, GDN_DIM), F32)
        for h in range(GDN_HEADS):
            cols = slice(h * GDN_DIM, (h + 1) * GDN_DIM)
            dn = dyb[:, cols] * sb[:, cols]
            dg_ref[:, G3 + WIDTH + h * GDN_DIM:G3 + WIDTH + (h + 1) * GDN_DIM] = _b16(
                dyb[:, cols] * (xs[h] * gg) * dsb[:, cols])
            dgg = dgg + jnp.sum(dn * xs[h], axis=0, keepdims=True)
            dxh = dn * gg
            dob_ref[:, cols] = rs[h] * (dxh - xs[h] * jnp.mean(dxh * xs[h], axis=1, keepdims=True))
        dgg_ref[0:1, :] += dgg

    half = lambda a: _rows(a, ts)
    w512 = lambda dt: _orow(S, (WIDTH,), dt, ts)
    return _tiled("merge_bwd", body, S // ts,
                  [_rows(dout, ts), _cols(z, ts, 3 * D, CB_GATES), half(o_a), _cols(z, ts, WIDTH, CB_AZ),
                   half(o_b), _cols(z, ts, WIDTH, CB_BZ), half(o_m), _cols(z, ts, WIDTH, CB_MZ),
                   half(ya), half(yb), half(ym), _full(gdn_g.reshape(1, GDN_DIM)),
                   _full(b_merge.reshape(1, 3 * D)), _full(wb), _full(wout), _full(hsum)],
                  [((S, N_AL), BF16, (ts, 3 * D + 3 * WIDTH), lambda i: (i, CB_MERGE)),
                   _orow(S, (D,), BF16, ts), _orow(S, (D,), BF16, ts),
                   _orow(S, (D,), BF16, ts), w512(F32), w512(F32), w512(F32),
                   ((128, S), F32, (128, ts), lambda i: (0, i)), _oacc((8, 3 * D), F32),
                   _oacc((8, GDN_DIM), F32)])


def _to_aligned(w):
    parts = [w[..., lo:lo + n] for lo, n, _ in sorted(W_IN_PIECES, key=lambda p: p[2])]
    parts.append(jnp.zeros(w.shape[:-1] + (N_AL - N_IN,), w.dtype))
    return jnp.concatenate(parts, axis=-1)


def _from_aligned(w):
    return jnp.concatenate([w[..., al:al + n] for _, n, al in W_IN_PIECES], axis=-1)


def _lanes128(v, lane0):
    return jnp.pad(v.astype(F32)[None, :], ((0, 0), (lane0, 128 - lane0 - v.shape[0])))


def _tiles(S):
    ts = min(512, S // 2)
    return dict(ts=ts, ts_small=min(256, S // 2), tq=min(512, S // 4), tq_fwd=min(1024, S // 2))


def _layer_fwd(x, mem, p):
    S = x.shape[0]
    tl = _tiles(S)
    ts, tss, tq = tl["ts"], tl["ts_small"], tl["tq"]
    h, rstd = _rms_fwd("norm_fwd", x, p["norm_g"], ts)
    z = _mm("in_proj", h, p["w_in_al"], tn=1664)

    b_fg128 = _lanes128(p["b_fg"], LANE_AF)
    f_hi, f_mid, f_lo = _fox_decay(z, b_fg128, ts)
    aq = z[:, CB_AQ * WIDTH:(CB_AQ + 1) * WIDTH]
    ak = z[:, CB_AK * WIDTH:(CB_AK + 1) * WIDTH]
    av = z[:, CB_AV * WIDTH:(CB_AV + 1) * WIDTH]
    q32 = _heads_major(aq, FOX_HEADS, FOX_DIM)
    kh = _heads_major(ak, FOX_HEADS, FOX_DIM).astype(BF16)
    vh = _heads_major(av, FOX_HEADS, FOX_DIM).astype(BF16)
    piecesT = jnp.stack([f[:FOX_HEADS] for f in (f_hi, f_mid, f_lo)], axis=1)
    pieces = piecesT.transpose(0, 2, 1)
    ones3 = jnp.ones((FOX_HEADS, S, 3), BF16)
    padk = jnp.zeros((FOX_HEADS, S, FOX_AUG - FOX_DIM - 6), BF16)
    q_aug = jnp.concatenate([q32, pieces.astype(F32), ones3.astype(F32), padk.astype(F32)], axis=-1)
    k_aug = jnp.concatenate([kh, ones3, -pieces, padk], axis=-1)
    kT_aug = jnp.concatenate([kh.transpose(0, 2, 1), ones3.transpose(0, 2, 1), -piecesT,
                              padk.transpose(0, 2, 1)], axis=1)
    v_aug = jnp.concatenate([vh, ones3[:, :, :1], jnp.zeros((FOX_HEADS, S, 128 - FOX_DIM - 1), BF16)],
                            axis=-1)
    o_h, lse, qs = _fox_fwd(q_aug, kT_aug, v_aug, tl["tq_fwd"])
    o_a = _heads_minor(o_h)

    a128 = _lanes128(p["a_log"], LANE_BA)
    dt128 = _lanes128(p["dt_bias"], LANE_BA)
    qkv, cpre, gb, gbT = _gdn_prep(z, p["conv_w"], a128, dt128, ts)
    grow = gbT[LANE_BA:LANE_BA + GDN_HEADS].reshape(GDN_HEADS, S // CHUNK, CHUNK)
    u, w, qd, kd, aqk, T = _gdn_local_fwd(qkv, gb, grow, ts)
    o_b, vn, states = _gdn_scan_fwd(u, w, qd, kd, aqk, gb, ts)

    mem_h, mem_r = _rms_fwd("mem_norm_fwd", mem, p["mem_norm_g"], mem.shape[0])
    mkv = _mm("mem_kv", mem_h, p["w_mem_kv"])
    mk, mv = mkv[:, :WIDTH], mkv[:, WIDTH:]
    o_m = _mem_attn_fwd(z, mk, mv, ts)

    x_next, ya, yb, ym, merged = _merge_fwd(x, z, o_a, o_b, o_m, p["gdn_norm_g"], p["b_merge"],
                                            p["w_branch"], p["w_out"], ts)
    saved = dict(x=x, h=h, rstd=rstd, z=z, b_fg128=b_fg128, qs=qs, k_aug=k_aug, kT_aug=kT_aug, v_aug=v_aug, lse=lse, o_a=o_a, a128=a128, dt128=dt128, qkv=qkv, cpre=cpre, gb=gb,
                 grow=grow, w=w, qd=qd, kd=kd, aqk=aqk, T=T, o_b=o_b, vn=vn, states=states,
                 mem_h=mem_h, mem_r=mem_r, mk=mk, mv=mv, o_m=o_m, ya=ya, yb=yb, ym=ym, merged=merged)
    return x_next, saved


def _layer_bwd(dout, mem, p, s):
    S = dout.shape[0]
    tl = _tiles(S)
    ts, tss, tq = tl["ts"], tl["ts_small"], tl["tq"]
    z = s["z"]
    hsum = (jnp.arange(128)[:, None] == jnp.arange(WIDTH)[None, :] // FOX_DIM).astype(F32)
    (dz, dpa, dpb, dpm, do_a, do_b, do_m, deltaT, db_merge, dgdn_g) = _merge_bwd(
        dout, z, s["o_a"], s["o_b"], s["o_m"], s["ya"], s["yb"], s["ym"], p["gdn_norm_g"],
        p["b_merge"], p["w_branch"], p["w_out"], hsum, tss)
    g = {}
    g["b_merge"] = db_merge[0]
    g["gdn_norm_g"] = dgdn_g[0]
    g["w_out"] = _mm("dw_out", s["merged"], dout, ta=True)
    g["w_branch"] = jnp.stack([_mm("dw_branch", y, dp, ta=True)
                               for y, dp in ((s["ya"], dpa), (s["yb"], dpb), (s["ym"], dpm))])

    do_h = _heads_major(do_a, FOX_HEADS, FOX_DIM).astype(BF16)
    delta_row = deltaT[:FOX_HEADS, None, :]
    dqT, dk_h, dv_h, dfk, dfq = _fox_bwd(s["qs"], s["k_aug"], s["kT_aug"], s["v_aug"], do_h, s["lse"],
                                         delta_row, tq)
    daq = _heads_minor(dqT.transpose(0, 2, 1))
    dak = _heads_minor(dk_h)
    dav = _heads_minor(dv_h)
    daf128, db_fg = _fox_decay_bwd(dfk[:, 0, :], dfq[:, 0, :], z, s["b_fg128"], ts)
    g["b_fg"] = db_fg[:FOX_HEADS]

    du, dw, dqd, dkd, daqk, dgl = _gdn_scan_bwd(do_b, s["w"], s["qd"], s["kd"], s["aqk"], s["vn"],
                                                s["states"], s["gb"], ts)
    dqkv, dgb = _gdn_local_bwd(s["qkv"], s["gb"], s["grow"], s["T"], du, dw, dqd, dkd, daqk, dgl, ts)
    dz, dzs_b, dconv, dad = _gdn_prep_bwd(dqkv, dgb, s["cpre"], z, p["conv_w"], s["a128"],
                                          s["dt128"], dz, ts)
    g["conv_w"] = dconv[:4]
    g["a_log"] = dad[0, LANE_BA:LANE_BA + GDN_HEADS]
    g["dt_bias"] = dad[1, LANE_BA:LANE_BA + GDN_HEADS]

    dz, dmk, dmv = _mem_attn_bwd(do_m, z, s["mk"], s["mv"], dz, ts)
    dmkv = jnp.concatenate([dmk, dmv], axis=1)
    g["w_mem_kv"] = _mm("dw_mem_kv", s["mem_h"], dmkv, ta=True)
    dmem_h = _mm("dmem_h", dmkv, p["w_mem_kv"], tb=True)
    M = mem.shape[0]
    _, g["mem_norm_g"] = _rms_bwd("mem_norm_bwd", dmem_h, mem, s["mem_r"], p["mem_norm_g"],
                                  jnp.zeros_like(mem), M)

    lane = jnp.arange(128)[None, :]
    dsmall = jnp.where(lane < 8, daf128, dzs_b)
    daqkv = jnp.concatenate([_b16(daq), _b16(dak), _b16(dav)], axis=1)
    dz = lax.dynamic_update_slice(dz, daqkv, (0, CB_AQKV * 3 * WIDTH))
    dz = lax.dynamic_update_slice(dz, _b16(dsmall), (0, CB_SMALL * 128))
    g["w_in_al"] = _mm("dw_in", s["h"], dz, ta=True, tn=1664)
    dh = _mm("dh", dz, p["w_in_al"], tb=True, tk=1664)
    dx, g["norm_g"] = _rms_bwd("norm_bwd", dh, s["x"], s["rstd"], p["norm_g"], dout, ts)
    return dx, g


def _local_step(x, mem, layers, final_norm_g, loss_target):
    S = x.shape[0]
    saves = []
    cur = x
    for p in layers:
        cur, sv = _layer_fwd(cur, mem, p)
        saves.append(sv)
    dx, dgf, loss_lanes = _loss_head(cur, final_norm_g, loss_target, _tiles(S)["ts"])
    grads = [None] * len(layers)
    for l in reversed(range(len(layers))):
        dx, grads[l] = _layer_bwd(dx, mem, layers[l], saves[l])
    return loss_lanes, dx, grads, dgf


HBM_SPEC = pl.BlockSpec(memory_space=pltpu.HBM)


def _mesh_pos():
    return lax.axis_index("x"), lax.axis_index("y"), lax.axis_index("c")


def _comm_call(name, body, arrays, out_shapes, n_remote, n_local):
    n = len(arrays)

    def kern(*refs):
        body(refs[:n], refs[n:2 * n], refs[2 * n], refs[2 * n + 1], refs[2 * n + 2])

    return pl.pallas_call(
        kern, name=name, out_shape=out_shapes, in_specs=[HBM_SPEC] * n, out_specs=[HBM_SPEC] * n,
        scratch_shapes=[pltpu.SemaphoreType.DMA((n_remote,)), pltpu.SemaphoreType.DMA((n_remote,)),
                        pltpu.SemaphoreType.DMA((max(n_local, 1),))],
    )(*arrays)


def _remote(src, dst, send_sems, recv_sems, k, to):
    return pltpu.make_async_remote_copy(src_ref=src, dst_ref=dst, send_sem=send_sems.at[k],
                                        recv_sem=recv_sems.at[k], device_id=to, device_id_type=MESH_ID)


def _other_chips(mx, my):
    return [(1 - mx, my), (mx, 1 - my), (1 - mx, 1 - my)]


def _gather_chips(name, shards):
    n = len(shards)

    def body(ins, outs, send_sems, recv_sems, local_sems):
        mx, my, mc = _mesh_pos()
        me = 2 * mx + my
        sibling = (mx, my, 1 - mc)
        chips = _other_chips(mx, my)
        sends = []
        for a in range(n):
            for k, (px, py) in enumerate(chips):
                cp = _remote(ins[a].at[mc], outs[a].at[me, mc], send_sems, recv_sems, 6 * a + k,
                             (px, py, mc))
                cp.start()
                sends.append(cp)
        for a in range(n):
            for k, (px, py) in enumerate(chips):
                j = 2 * px + py
                _remote(ins[a].at[mc], outs[a].at[j, mc], send_sems, recv_sems, 6 * a + k,
                        (px, py, mc)).wait_recv()
                cp = _remote(outs[a].at[j, mc], outs[a].at[j, mc], send_sems, recv_sems, 6 * a + 3 + k,
                             sibling)
                cp.start()
                sends.append(cp)
        for a in range(n):
            for k, (px, py) in enumerate(chips):
                j = 2 * px + py
                _remote(outs[a].at[j, 1 - mc], outs[a].at[j, 1 - mc], send_sems, recv_sems,
                        6 * a + 3 + k, sibling).wait_recv()
        for cp in sends:
            cp.wait_send()

    shapes = [jax.ShapeDtypeStruct((N_CHIPS,) + s.shape, s.dtype) for s in shards]
    outs = _comm_call(name, body, shards, shapes, 6 * n, 0)
    me = 2 * lax.axis_index("x") + lax.axis_index("y")
    return [lax.dynamic_update_index_in_dim(o, s, me, 0) for o, s in zip(outs, shards)]


def _sibling_swap(gs):
    n = len(gs)

    def body(ins, outs, send_sems, recv_sems, local_sems):
        mx, my, mc = _mesh_pos()
        sends = []
        for a in range(n):
            cp = _remote(ins[a].at[:, 1 - mc], outs[a], send_sems, recv_sems, a, (mx, my, 1 - mc))
            cp.start()
            sends.append(cp)
        for cp in sends:
            cp.wait()

    shapes = [jax.ShapeDtypeStruct((g.shape[0],) + g.shape[2:], g.dtype) for g in gs]
    return _comm_call("grad_sibling_swap", body, gs, shapes, n, 0)


def _chip_exchange(ps):
    n = len(ps)

    def body(ins, outs, send_sems, recv_sems, local_sems):
        mx, my, mc = _mesh_pos()
        me = 2 * mx + my
        chips = _other_chips(mx, my)
        sends = []
        for a in range(n):
            for k, (px, py) in enumerate(chips):
                cp = _remote(ins[a].at[2 * px + py], outs[a].at[me], send_sems, recv_sems, 3 * a + k,
                             (px, py, mc))
                cp.start()
                sends.append(cp)
        for a in range(n):
            for k, (px, py) in enumerate(chips):
                _remote(ins[a].at[me], outs[a].at[2 * px + py], send_sems, recv_sems, 3 * a + k,
                        (px, py, mc)).wait_recv()
        for cp in sends:
            cp.wait_send()

    shapes = [jax.ShapeDtypeStruct(p.shape, p.dtype) for p in ps]
    outs = _comm_call("grad_chip_exchange", body, ps, shapes, 3 * n, 0)
    me = 2 * lax.axis_index("x") + lax.axis_index("y")
    return [lax.dynamic_update_index_in_dim(o, lax.dynamic_index_in_dim(p, me, 0, keepdims=False), me, 0)
            for o, p in zip(outs, ps)]


def _sibling_gather(hs):
    n = len(hs)

    def body(ins, outs, send_sems, recv_sems, local_sems):
        mx, my, mc = _mesh_pos()
        sends = []
        for a in range(n):
            cp = _remote(ins[a], outs[a], send_sems, recv_sems, a, (mx, my, 1 - mc))
            cp.start()
            sends.append(cp)
        for cp in sends:
            cp.wait()

    shapes = [jax.ShapeDtypeStruct(h.shape, h.dtype) for h in hs]
    theirs = _comm_call("grad_sibling_gather", body, hs, shapes, n, 0)
    first = lax.axis_index("c") == 0
    return [jnp.stack([jnp.where(first, h, t), jnp.where(first, t, h)]) for h, t in zip(hs, theirs)]


def _add_pairs(a, b, tr, out_dtype):
    n, H, C = a.shape

    def kern(a_ref, b_ref, o_ref):
        o_ref[...] = (a_ref[...] + b_ref[...]).astype(o_ref.dtype)

    spec = pl.BlockSpec((None, tr, C), lambda j, i: (j, i, 0))
    return pl.pallas_call(
        kern, name="grad_pair_sum", grid=(n, H // tr), in_specs=[spec, spec], out_specs=spec,
        out_shape=jax.ShapeDtypeStruct((n, H, C), out_dtype),
        compiler_params=_params(("parallel", "parallel")),
    )(a, b)


def _sum_slots(r4, tr):
    n, H, C = r4.shape

    def kern(r_ref, o_ref):
        f = lambda k: r_ref[k].astype(F32)
        o_ref[...] = ((f(0) + f(1)) + f(2)) + f(3)

    return pl.pallas_call(
        kern, name="grad_chip_sum", grid=(H // tr,),
        in_specs=[pl.BlockSpec((n, tr, C), lambda i: (0, i, 0))],
        out_specs=pl.BlockSpec((tr, C), lambda i: (i, 0)),
        out_shape=jax.ShapeDtypeStruct((H, C), F32),
        compiler_params=_params(("parallel",)),
    )(r4)


def _adamw(w, g, m, v, tr):
    R, C = w.shape
    c1 = 1.0 - ADAM_B1
    c2 = 1.0 - ADAM_B2
    bc1 = 1.0 - ADAM_B1 ** ADAM_STEP
    bc2 = 1.0 - ADAM_B2 ** ADAM_STEP

    def kern(w_ref, g_ref, m_ref, v_ref, d_ref, mo_ref, vo_ref):
        gv = g_ref[...]
        mn = ADAM_B1 * m_ref[...] + c1 * gv
        vn = ADAM_B2 * v_ref[...] + c2 * (gv * gv)
        m_hat = mn / bc1
        v_hat = vn / bc2
        d_ref[...] = -ADAM_LR * (m_hat / (jnp.sqrt(v_hat) + ADAM_EPS) + ADAM_WD * w_ref[...])
        mo_ref[...] = mn
        vo_ref[...] = vn

    spec = pl.BlockSpec((tr, C), lambda i: (i, 0))
    shape = jax.ShapeDtypeStruct((R, C), F32)
    return pl.pallas_call(
        kern, name="adamw", grid=(R // tr,), in_specs=[spec] * 4, out_specs=[spec] * 3,
        out_shape=[shape] * 3, compiler_params=_params(("parallel",)),
    )(w, g, m, v)


PACK_COLS = 1024
PACK_ROWS = 512
W_SHARD = N_IN // N_CHIPS
SLAB = ("conv_w", "w_mem_kv", "w_branch", "w_out")
SMALL =("norm_g", "b_fg", "b_merge", "a_log", "dt_bias", "gdn_norm_g", "mem_norm_g", "final_norm_g")
ALL_WEIGHTS = ("norm_g", "w_in", "b_fg", "b_merge", "conv_w", "a_log", "dt_bias", "gdn_norm_g",
               "mem_norm_g", "w_mem_kv", "w_branch", "w_out", "final_norm_g")
SHARD_AXIS = {"w_in": 2, "conv_w": 2, "w_mem_kv": 1, "w_branch": 3, "w_out": 1}


def _pack(arrays, row_multiple):
    flat = jnp.concatenate([a.reshape(-1) for a in arrays])
    n = flat.shape[0]
    rows = -(-n // PACK_COLS)
    rows = -(-rows // row_multiple) * row_multiple
    flat = jnp.pad(flat, (0, rows * PACK_COLS - n))
    return flat.reshape(rows, PACK_COLS)


def _unpack(slab, shapes):
    out, off = [], 0
    for shp in shapes:
        n = 1
        for d in shp:
            n *= d
        r0, r1 = off // PACK_COLS, -(-(off + n) // PACK_COLS)
        rows = slab[r0:r1].reshape(-1)
        out.append(rows[off - r0 * PACK_COLS:off - r0 * PACK_COLS + n].reshape(shp))
        off += n
    return out


def _shard_of(full, name, j):
    ax = SHARD_AXIS[name]
    n = full.shape[ax] // N_CHIPS
    return lax.slice_in_dim(full, j * n, (j + 1) * n, axis=ax)


def _aligned_from_shards(shards):
    def cols(lo, n):
        parts = []
        while n > 0:
            j, off = divmod(lo, W_SHARD)
            take = min(n, W_SHARD - off)
            parts.append(shards[j][..., off:off + take])
            lo, n = lo + take, n - take
        return parts

    out = []
    for lo, n, _ in sorted(W_IN_PIECES, key=lambda p: p[2]):
        out += cols(lo, n)
    out.append(jnp.zeros(shards[0].shape[:-1] + (N_AL - N_IN,), shards[0].dtype))
    return jnp.concatenate(out, axis=-1)


def _shard_from_aligned(w_al, j):
    lo_j, hi_j = j * W_SHARD, (j + 1) * W_SHARD
    parts = []
    for lo, n, al in W_IN_PIECES:
        a, b = max(lo, lo_j), min(lo + n, hi_j)
        if a < b:
            parts.append(w_al[..., al + a - lo:al + b - lo])
    return jnp.concatenate(parts, axis=-1)


def kernel(x, mem, norm_g, w_in, b_fg, b_merge, conv_w, a_log, dt_bias, gdn_norm_g, mem_norm_g, w_mem_kv, w_branch, w_out, final_norm_g, loss_target, m_norm_g, m_w_in, m_b_fg, m_b_merge, m_conv_w, m_a_log, m_dt_bias, m_gdn_norm_g, m_mem_norm_g, m_w_mem_kv, m_w_branch, m_w_out, m_final_norm_g, v_norm_g, v_w_in, v_b_fg, v_b_merge, v_conv_w, v_a_log, v_dt_bias, v_gdn_norm_g, v_mem_norm_g, v_w_mem_kv, v_w_branch, v_w_out, v_final_norm_g):
    wts = dict(norm_g=norm_g, w_in=w_in, b_fg=b_fg, b_merge=b_merge, conv_w=conv_w, a_log=a_log,
               dt_bias=dt_bias, gdn_norm_g=gdn_norm_g, mem_norm_g=mem_norm_g, w_mem_kv=w_mem_kv,
               w_branch=w_branch, w_out=w_out, final_norm_g=final_norm_g)
    mom = dict(norm_g=m_norm_g, w_in=m_w_in, b_fg=m_b_fg, b_merge=m_b_merge, conv_w=m_conv_w,
               a_log=m_a_log, dt_bias=m_dt_bias, gdn_norm_g=m_gdn_norm_g, mem_norm_g=m_mem_norm_g,
               w_mem_kv=m_w_mem_kv, w_branch=m_w_branch, w_out=m_w_out, final_norm_g=m_final_norm_g)
    vel = dict(norm_g=v_norm_g, w_in=v_w_in, b_fg=v_b_fg, b_merge=v_b_merge, conv_w=v_conv_w,
               a_log=v_a_log, dt_bias=v_dt_bias, gdn_norm_g=v_gdn_norm_g, mem_norm_g=v_mem_norm_g,
               w_mem_kv=v_w_mem_kv, w_branch=v_w_branch, w_out=v_w_out, final_norm_g=v_final_norm_g)

    big = ("w_in", "w_mem_kv", "w_branch", "w_out")
    gathered = _gather_chips("weight_gather", [wts[n].astype(BF16) for n in big] + [conv_w])
    all_w = dict(zip(big + ("conv_w",), gathered))
    w_in_al = _aligned_from_shards([all_w["w_in"][j] for j in range(N_CHIPS)])

    layers = []
    for l in range(DEPTH):
        rows_of = lambda n: all_w[n][:, l].reshape(D_MODEL, D_MODEL)
        last_of = lambda n: jnp.concatenate([all_w[n][j, l] for j in range(N_CHIPS)], axis=-1)
        layers.append(dict(norm_g=norm_g[l], w_in_al=w_in_al[l], b_fg=b_fg[l], b_merge=b_merge[l],
                           conv_w=jnp.pad(last_of("conv_w"), ((0, 4), (0, 0))), a_log=a_log[l],
                           dt_bias=dt_bias[l], gdn_norm_g=gdn_norm_g[l], mem_norm_g=mem_norm_g[l],
                           w_mem_kv=rows_of("w_mem_kv"), w_branch=last_of("w_branch"),
                           w_out=rows_of("w_out")))

    loss_lanes, dx, grads, dgf = _local_step(x[0], mem[0], layers, final_norm_g, loss_target[0])

    gfull = {n: jnp.stack([grads[l][n] for l in range(DEPTH)])
             for n in ("norm_g", "b_fg", "b_merge", "conv_w", "a_log", "dt_bias", "gdn_norm_g",
                       "mem_norm_g", "w_mem_kv", "w_branch", "w_out")}
    gfull["final_norm_g"] = dgf
    loss_local = jnp.sum(loss_lanes).reshape(1)
    small_g = [gfull[n] for n in SMALL] + [loss_local]
    dw_al = jnp.stack([grads[l]["w_in_al"] for l in range(DEPTH)])
    ga = jnp.stack([_shard_from_aligned(dw_al, j) for j in range(N_CHIPS)])
    mats = ("w_mem_kv", "w_branch", "w_out")
    rest = ("conv_w",) + SMALL
    gb = jnp.stack([_pack([_shard_of(gfull[n], n, j) for n in mats], PACK_ROWS)
                    for j in range(N_CHIPS)])
    gc = jnp.stack([_pack([_shard_of(gfull["conv_w"], "conv_w", j)] + small_g, 16)
                    for j in range(N_CHIPS)])
    halves = lambda g: g.reshape(N_CHIPS, 2, g.shape[1] // 2, PACK_COLS)
    gb, gc = halves(gb), halves(gc)

    mc = lax.axis_index("c")
    tr = 256
    trs = (tr, tr, 8)
    from_sibling = _sibling_swap([ga, gb, gc])
    mine = [lax.dynamic_index_in_dim(g, mc, axis=1, keepdims=False) for g in (ga, gb, gc)]
    pair = [_add_pairs(a, b, t, dt) for a, b, t, dt in zip(mine, from_sibling, trs, (BF16, BF16, F32))]
    slots = _chip_exchange(pair)
    half = [_sum_slots(s, t) for s, t in zip(slots, trs)]
    ga_sum, gb_sum, gc_sum = _sibling_gather(half)
    flat = lambda g: g.reshape(-1, PACK_COLS)

    g_un = dict(zip(mats, _unpack(flat(gb_sum), [wts[n].shape for n in mats])))
    g_un.update(zip(rest + ("loss",), _unpack(flat(gc_sum), [wts[n].shape for n in rest] + [(1,)])))
    g_un["w_in"] = ga_sum
    d_un, m_un, v_un = {}, {}, {}
    rows2d = lambda a: a.reshape(-1, a.shape[-1])
    for n in ("w_in", "w_mem_kv", "w_branch", "w_out"):
        res = _adamw(rows2d(wts[n]), rows2d(g_un[n]), rows2d(mom[n]), rows2d(vel[n]), tr)
        d_un[n], m_un[n], v_un[n] = [r.reshape(wts[n].shape) for r in res]
    little = ("conv_w",) + SMALL
    slab = lambda d: _pack([d[n] for n in little], 8)
    res = _adamw(slab(wts), slab(g_un), slab(mom), slab(vel), 8)
    little_shapes = [wts[n].shape for n in little]
    for out, r in zip((d_un, m_un, v_un), res):
        out.update(zip(little, _unpack(r, little_shapes)))

    loss = g_un["loss"][0]
    return (loss, dx[None], *[g_un[n] for n in ALL_WEIGHTS], *[d_un[n] for n in ALL_WEIGHTS],
            *[m_un[n] for n in ALL_WEIGHTS], *[v_un[n] for n in ALL_WEIGHTS])
```

```python
import functools

import jax
import jax.numpy as jnp
from jax import lax
from jax.experimental import pallas as pl
from jax.experimental.pallas import tpu as pltpu

F32 = jnp.float32
BF16 = jnp.bfloat16
HIGHEST = lax.Precision.HIGHEST
PREC_UT = lax.Precision.HIGH
MESH_ID = pl.DeviceIdType.MESH

D_MODEL = 1024
DEPTH = 2
CHUNK = 64
EPS = 1e-6
FOX_HEADS, FOX_DIM = 8, 64
GDN_HEADS, GDN_DIM = 4, 128
MEM_HEADS, MEM_DIM = 4, 128
WIDTH = 512
N_BRANCH = 3
N_IN = 8208
N_AL = 8320
N_CHIPS = 4
NEG = -1e30
LOG2E = 1.4426950408889634
LN2 = 0.6931471805599453

ADAM_LR, ADAM_B1, ADAM_B2, ADAM_EPS, ADAM_WD, ADAM_STEP = 0.001, 0.9, 0.999, 1e-08, 0.01, 10

CB_GATES = 0
CB_AZ, CB_BZ, CB_MZ = 6, 7, 8
CB_MERGE = 0
CB_BQKV = 3
CB_AQ, CB_AK, CB_AV = 12, 13, 14
CB_AQKV = 4
CB_MQ = 15
CB_SMALL = 64
W_IN_PIECES = ((0, 512, 6144), (512, 512, 6656), (1024, 512, 7168), (1536, 8, 8192), (1544, 512, 3072),
               (2056, 512, 4608), (2568, 512, 5120), (3080, 512, 5632), (3592, 4, 8200), (3596, 4, 8204),
               (3600, 512, 3584), (4112, 512, 7680), (4624, 512, 4096), (5136, 3072, 0))
LANE_AF, LANE_BA, LANE_BB = 0, 8, 12

NN = ((1,), (0,))
NT = ((1,), (1,))
TN = ((0,), (0,))

VMEM_LIMIT_BYTES = 56 * 1024 * 1024


def _dot(a, b, dims=NN, prec=None):
    return lax.dot_general(a, b, (dims, ((), ())), preferred_element_type=F32, precision=prec)


def _bdot(a, b, ca, cb, prec=None):
    return lax.dot_general(a, b, (((ca,), (cb,)), ((0,), (0,))), preferred_element_type=F32,
                           precision=prec)


def _b16(a):
    return a.astype(BF16)


def _eye(n, dtype=F32):
    r = lax.broadcasted_iota(jnp.int32, (n, n), 0)
    c = lax.broadcasted_iota(jnp.int32, (n, n), 1)
    return jnp.where(r == c, 1.0, 0.0).astype(dtype)


def _transpose_exact(x):
    return _dot(_eye(x.shape[1]), x, NT, HIGHEST)


def _col_to_row(col):
    n = col.shape[0]
    return jnp.sum(jnp.where(_eye(n) > 0.5, col, 0.0), axis=0, keepdims=True)


def _row_to_col(row):
    n = row.shape[1]
    return jnp.sum(jnp.where(_eye(n) > 0.5, row, 0.0), axis=1, keepdims=True)


def _sigmoid(x):
    return 0.5 * jnp.tanh(0.5 * x) + 0.5


def _softplus(x):
    return jnp.maximum(x, 0.0) + jnp.log(1.0 + jnp.exp(-jnp.abs(x)))


def _silu_and_grad(x):
    s = _sigmoid(x)
    return x * s, s * (1.0 + x * (1.0 - s))


def _params(semantics):
    return pltpu.CompilerParams(dimension_semantics=semantics, vmem_limit_bytes=VMEM_LIMIT_BYTES)


def _rows(a, ts):
    nd = a.ndim
    return (a, (ts,) + a.shape[1:], lambda i, nd=nd: (i,) + (0,) * (nd - 1))


def _cols(a, ts, width, cb):
    return (a, (ts, width), lambda i, cb=cb: (i, cb))


def _full(a):
    nd = a.ndim
    return (a, a.shape, lambda i, nd=nd: (0,) * nd)


def _orow(S, tail, dtype, ts):
    nd = 1 + len(tail)
    return ((S,) + tuple(tail), dtype, (ts,) + tuple(tail), lambda i, nd=nd: (i,) + (0,) * (nd - 1))


def _oacc(shape, dtype):
    nd = len(shape)
    return (tuple(shape), dtype, tuple(shape), lambda i, nd=nd: (0,) * nd)


def _tiled(name, body, n_steps, ins, outs, scratch=(), reverse=False, fill=None):
    def rev(imap):
        if not reverse:
            return imap
        return lambda i: imap(n_steps - 1 - i)

    in_specs = [pl.BlockSpec(blk, rev(imap)) for (_, blk, imap) in ins]
    out_specs = [pl.BlockSpec(blk, rev(imap)) for (_, _, blk, imap) in outs]
    out_shape = [jax.ShapeDtypeStruct(shape, dt) for (shape, dt, _, _) in outs]
    n_in, n_out = len(ins), len(outs)
    arrays = [a for (a, _, _) in ins]
    aliases = {}
    n_extra = 0
    if fill is not None:
        arrays.append(fill[0])
        in_specs.append(pl.BlockSpec(memory_space=pl.ANY))
        aliases = {n_in: fill[1]}
        n_extra = 1

    def kern(*refs):
        step = pl.program_id(0)
        t = (n_steps - 1 - step) if reverse else step
        lo = n_in + n_extra
        body(t, step == 0, refs[:n_in], refs[lo:lo + n_out], refs[lo + n_out:])

    res = pl.pallas_call(
        kern, name=name, grid=(n_steps,), in_specs=in_specs, out_specs=out_specs,
        out_shape=out_shape, scratch_shapes=list(scratch), input_output_aliases=aliases,
        compiler_params=_params(("arbitrary",)),
    )(*arrays)
    return res


def _pick(n, pref):
    if n <= pref:
        return n
    best = None
    for t in range(128, pref + 1, 128):
        if n % t == 0:
            best = t
    assert best is not None, (n, pref)
    return best


def _mm(name, a, b, ta=False, tb=False, out_dtype=F32, tm=1024, tn=1024, tk=1024):
    if ta:
        K, M = a.shape
    else:
        M, K = a.shape
    if tb:
        N, K2 = b.shape
    else:
        K2, N = b.shape
    assert K == K2, (a.shape, b.shape, ta, tb)
    tm, tn, tk = _pick(M, tm), _pick(N, tn), _pick(K, tk)
    nk = K // tk
    a_spec = (pl.BlockSpec((tk, tm), lambda i, j, k: (k, i)) if ta
              else pl.BlockSpec((tm, tk), lambda i, j, k: (i, k)))
    b_spec = (pl.BlockSpec((tn, tk), lambda i, j, k: (j, k)) if tb
              else pl.BlockSpec((tk, tn), lambda i, j, k: (k, j)))
    dims = ((0,) if ta else (1,), (1,) if tb else (0,))

    def kern_single(a_ref, b_ref, o_ref):
        o_ref[...] = _dot(_b16(a_ref[...]), _b16(b_ref[...]), dims).astype(o_ref.dtype)

    def kern_acc(a_ref, b_ref, o_ref, acc_ref):
        k = pl.program_id(2)

        @pl.when(k == 0)
        def _():
            acc_ref[...] = jnp.zeros_like(acc_ref)

        acc_ref[...] += _dot(_b16(a_ref[...]), _b16(b_ref[...]), dims)

        @pl.when(k == nk - 1)
        def _():
            o_ref[...] = acc_ref[...].astype(o_ref.dtype)

    return pl.pallas_call(
        kern_single if nk == 1 else kern_acc, name=name, grid=(M // tm, N // tn, nk),
        in_specs=[a_spec, b_spec],
        out_specs=pl.BlockSpec((tm, tn), lambda i, j, k: (i, j)),
        out_shape=jax.ShapeDtypeStruct((M, N), out_dtype),
        scratch_shapes=[] if nk == 1 else [pltpu.VMEM((tm, tn), F32)],
        compiler_params=_params(("parallel", "parallel", "arbitrary")),
    )(a, b)


def _rms_fwd(name, x, g, ts):
    S, D = x.shape

    def body(t, first, ins, outs, scratch):
        x_ref, g_ref = ins
        h_ref, r_ref = outs
        xv = x_ref[...]
        r = lax.rsqrt(jnp.mean(xv * xv, axis=1, keepdims=True) + EPS)
        h_ref[...] = (xv * r * g_ref[...]).astype(h_ref.dtype)
        r_ref[...] = r

    return _tiled(name, body, S // ts, [_rows(x, ts), _full(g.reshape(1, D))],
                  [_orow(S, (D,), BF16, ts), _orow(S, (1,), F32, ts)])


def _rms_bwd(name, dh, x, rstd, g, dres, ts):
    S, D = x.shape

    def body(t, first, ins, outs, scratch):
        dh_ref, x_ref, r_ref, g_ref, dres_ref = ins
        dx_ref, dg_ref = outs
        r = r_ref[...]
        xh = x_ref[...] * r
        dhv = dh_ref[...]
        dxh = dhv * g_ref[...]
        dx_ref[...] = dres_ref[...] + r * (dxh - xh * jnp.mean(dxh * xh, axis=1, keepdims=True))

        @pl.when(first)
        def _():
            dg_ref[...] = jnp.zeros_like(dg_ref)

        dg_ref[0:1, :] += jnp.sum(dhv * xh, axis=0, keepdims=True)

    dx, dg = _tiled(name, body, S // ts,
                    [_rows(dh, ts), _rows(x, ts), _rows(rstd, ts), _full(g.reshape(1, D)), _rows(dres, ts)],
                    [_orow(S, (D,), F32, ts), _oacc((8, D), F32)])
    return dx, dg[0]


def _loss_head(x, g, target, ts):
    S, D = x.shape

    def body(t, first, ins, outs, scratch):
        x_ref, g_ref, tgt_ref = ins
        dx_ref, dg_ref, loss_ref = outs
        xv = x_ref[...]
        gv = g_ref[...]
        r = lax.rsqrt(jnp.mean(xv * xv, axis=1, keepdims=True) + EPS)
        xh = xv * r
        err = xh * gv - tgt_ref[...]
        dy = err * (1.0 / D)
        dxh = dy * gv
        dx_ref[...] = r * (dxh - xh * jnp.mean(dxh * xh, axis=1, keepdims=True))

        @pl.when(first)
        def _():
            dg_ref[...] = jnp.zeros_like(dg_ref)
            loss_ref[...] = jnp.zeros_like(loss_ref)

        dg_ref[0:1, :] += jnp.sum(dy * xh, axis=0, keepdims=True)
        per_lane = jnp.sum(err * err, axis=0, keepdims=True)
        loss_ref[0:1, :] += per_lane * (0.5 / D)

    dx, dg, loss = _tiled("loss_head", body, S // ts,
                          [_rows(x, ts), _full(g.reshape(1, D)), _rows(target, ts)],
                          [_orow(S, (D,), F32, ts), _oacc((8, D), F32), _oacc((8, D), F32)])
    return dx, dg[0], loss[0]


def _scan_rows(x, length, seg, reverse=False):
    row = lax.broadcasted_iota(jnp.int32, x.shape, 0) % seg
    k = 1
    while k < seg:
        if reverse:
            x = x + jnp.where(row < seg - k, pltpu.roll(x, length - k, 0), 0.0)
        else:
            x = x + jnp.where(row >= k, pltpu.roll(x, k, 0), 0.0)
        k *= 2
    return x


def _fox_decay(z, b_fg128, ts):
    S = z.shape[0]

    def body(t, first, ins, outs, scratch):
        zs_ref, b_ref = ins
        hi_ref, mid_ref, lo_ref = outs
        (carry,) = scratch

        @pl.when(first)
        def _():
            carry[...] = jnp.zeros_like(carry)

        logf = -_softplus(-(zs_ref[...] + b_ref[...]))
        run = _scan_rows(logf, ts, ts) + carry[0:1, :]
        carry[0:1, :] = run[ts - 1:ts, :]
        f2 = run * LOG2E
        hi = f2.astype(BF16)
        r1 = f2 - hi.astype(F32)
        mid = r1.astype(BF16)
        lo = (r1 - mid.astype(F32)).astype(BF16)
        eye = _eye(128, BF16)
        hi_ref[...] = _dot(eye, hi, NT).astype(BF16)
        mid_ref[...] = _dot(eye, mid, NT).astype(BF16)
        lo_ref[...] = _dot(eye, lo, NT).astype(BF16)

    tcol = lambda dt: ((128, S), dt, (128, ts), lambda i: (0, i))
    return _tiled("fox_decay", body, S // ts,
                  [_cols(z, ts, 128, CB_SMALL), _full(b_fg128)],
                  [tcol(BF16), tcol(BF16), tcol(BF16)], scratch=[pltpu.VMEM((8, 128), F32)])


def _fox_decay_bwd(dfk_rows, dfq_rows, z, b_fg128, ts):
    S = z.shape[0]
    H = dfk_rows.shape[0]

    def body(t, first, ins, outs, scratch):
        dfk_ref, dfq_ref, zs_ref, b_ref = ins
        daf_ref, db_ref = outs
        (carry,) = scratch

        @pl.when(first)
        def _():
            carry[...] = jnp.zeros_like(carry)
            db_ref[...] = jnp.zeros_like(db_ref)

        r = lax.broadcasted_iota(jnp.int32, (H, 128), 0)
        c = lax.broadcasted_iota(jnp.int32, (H, 128), 1)
        place = jnp.where(r == c, 1.0, 0.0)
        df = _dot(dfk_ref[...] + dfq_ref[...], place, TN, HIGHEST)
        run = _scan_rows(df, ts, ts, reverse=True) + carry[0:1, :]
        carry[0:1, :] = run[0:1, :]
        daf = run * _sigmoid(-(zs_ref[...] + b_ref[...]))
        daf_ref[...] = daf
        db_ref[0:1, :] += jnp.sum(daf, axis=0, keepdims=True)

    rowsin = lambda a: (a, (H, ts), lambda i: (0, i))
    daf, db = _tiled("fox_decay_bwd", body, S // ts,
                     [rowsin(dfk_rows), rowsin(dfq_rows), _cols(z, ts, 128, CB_SMALL), _full(b_fg128)],
                     [_orow(S, (128,), F32, ts), _oacc((8, 128), F32)],
                     scratch=[pltpu.VMEM((8, 128), F32)], reverse=True)
    return daf, db[0]


FOX_AUG = 80


def _fox_fwd(q_aug, kT_aug, v_aug, tq):
    H, S, da = q_aug.shape
    dv = v_aug.shape[2]
    d = FOX_DIM
    tk = tq // 2
    qscale = (d ** -0.5) * LOG2E

    def kern(q_ref, kT_ref, v_ref, o_ref, lse_ref, qs_ref, s_buf, p_buf, m_scr, acc_scr):
        i = pl.program_id(1)
        col = lax.broadcasted_iota(jnp.int32, (1, da), 1)
        qb = _b16(q_ref[...] * jnp.where(col < d, qscale, 1.0))
        qs_ref[...] = qb

        def keys(t):
            return pl.ds(pl.multiple_of(t * tk, tk), tk)

        def stage(t, slot, mask_off, look_ahead):
            if look_ahead:
                s_buf[1 - slot] = _dot(qb, kT_ref[:, keys(t + 1)])
            pv = _dot(p_buf[1 - slot], v_ref[keys(jnp.maximum(t - 1, 0)), :])

            def scores():
                s = s_buf[slot]
                if mask_off is None:
                    return s
                r = lax.broadcasted_iota(jnp.int32, (tq, tk), 0)
                c = lax.broadcasted_iota(jnp.int32, (tq, tk), 1)
                return jnp.where(c + mask_off <= r, s, NEG)

            m = m_scr[...]
            m_new = jnp.maximum(m, jnp.max(scores(), axis=1, keepdims=True))
            alpha = jnp.exp2(m - m_new)
            p_buf[slot] = _b16(jnp.exp2(scores() - m_new))
            m_scr[...] = m_new
            acc_scr[...] = (acc_scr[...] + pv) * alpha

        s_buf[0] = _dot(qb, kT_ref[:, keys(0)])
        p_buf[1] = jnp.zeros((tq, tk), BF16)
        m_scr[...] = jnp.full((tq, 1), NEG, F32)
        acc_scr[...] = jnp.zeros((tq, dv), F32)

        def pair(n):
            stage(2 * n, 0, None, True)
            stage(2 * n + 1, 1, None, True)

        def quad(m, _):
            pair(2 * m)
            pair(2 * m + 1)
            return 0

        lax.fori_loop(0, i // 2, quad, 0)

        @pl.when(i % 2 == 1)
        def _():
            pair(i - 1)

        stage(2 * i, 0, 0, True)
        stage(2 * i + 1, 1, tk, False)
        acc = acc_scr[...] + _dot(p_buf[1], v_ref[keys(2 * i + 1), :])
        l = acc[:, d:d + 1]
        o_ref[...] = acc[:, :d] / l
        lse_ref[...] = _col_to_row(m_scr[...] + jnp.log(l) * LOG2E)

    return pl.pallas_call(
        kern, name="fox_fwd", grid=(H, S // tq),
        in_specs=[pl.BlockSpec((None, tq, da), lambda h, i: (h, i, 0)),
                  pl.BlockSpec((None, da, S), lambda h, i: (h, 0, 0)),
                  pl.BlockSpec((None, S, dv), lambda h, i: (h, 0, 0))],
        out_specs=[pl.BlockSpec((None, tq, d), lambda h, i: (h, i, 0)),
                   pl.BlockSpec((None, 1, tq), lambda h, i: (h, 0, i)),
                   pl.BlockSpec((None, tq, da), lambda h, i: (h, i, 0))],
        out_shape=[jax.ShapeDtypeStruct((H, S, d), F32), jax.ShapeDtypeStruct((H, 1, S), F32),
                   jax.ShapeDtypeStruct((H, S, da), BF16)],
        scratch_shapes=[pltpu.VMEM((2, tq, tk), F32), pltpu.VMEM((2, tq, tk), BF16),
                        pltpu.VMEM((tq, 1), F32), pltpu.VMEM((tq, dv), F32)],
        compiler_params=_params(("parallel", "arbitrary")),
    )(q_aug, kT_aug, v_aug)


def _fox_bwd(qs, k_aug, kT, v, do, lse_row, delta_row, tq):
    H, S, da = qs.shape
    d = FOX_DIM
    tk = tq
    nq = S // tq
    scale = d ** -0.5

    ts2 = tq // 2
    last = 2 * nq - 1

    def kern(q_ref, k_ref, kT_ref, v_ref, do_ref, lse_ref, dl_ref,
             dqT_ref, dk_ref, dv_ref, dfk_ref, dfq_ref,
             kq_buf, dp_buf, pb_buf, ds_buf, dk_scr, dv_scr, dfk_scr):
        j = pl.program_id(1)

        @pl.when(j == 0)
        def _():
            dqT_ref[...] = jnp.zeros_like(dqT_ref)
            dfq_ref[...] = jnp.zeros_like(dfq_ref)

        kb = k_ref[...]
        kTb = kT_ref[...]
        vb = v_ref[:, :d]
        dk_scr[...] = jnp.zeros_like(dk_scr)
        dv_scr[...] = jnp.zeros_like(dv_scr)
        dfk_scr[...] = jnp.zeros_like(dfk_scr)

        def queries(t):
            return pl.ds(pl.multiple_of(t * ts2, ts2), ts2)

        def products(t, slot):
            rows = queries(t)
            kq_buf[slot] = _dot(kb, q_ref[rows, :], NT)
            dp_buf[slot] = _dot(vb, do_ref[rows, :], NT)

        def pointwise(t, slot, mask_off):
            rows = queries(t)
            sT = kq_buf[slot]
            if mask_off is not None:
                r = lax.broadcasted_iota(jnp.int32, (tk, ts2), 0)
                c = lax.broadcasted_iota(jnp.int32, (tk, ts2), 1)
                sT = jnp.where(r <= c + mask_off, sT, NEG)
            pT = jnp.exp2(sT - lse_ref[:, rows])
            dsT = pT * (dp_buf[slot] - dl_ref[:, rows])
            pb_buf[slot] = _b16(pT)
            ds_buf[slot] = _b16(dsT)
            dfk_scr[...] -= jnp.sum(dsT, axis=1, keepdims=True)
            dfq_ref[:, rows] += jnp.sum(dsT, axis=0, keepdims=True)

        def accumulate(t, slot):
            rows = queries(t)
            dsb = ds_buf[slot]
            dv_scr[...] += _dot(pb_buf[slot], do_ref[rows, :])
            dk_scr[...] += _dot(dsb, q_ref[rows, :])
            dqT_ref[:, rows] += _dot(kTb, dsb) * scale

        def stage(t, slot, mask_off, has_prev):
            products(jnp.minimum(t + 1, last), 1 - slot)
            if has_prev:
                accumulate(t - 1, 1 - slot)
            pointwise(t, slot, mask_off)

        products(2 * j, 0)
        stage(2 * j, 0, 0, False)
        stage(2 * j + 1, 1, ts2, True)

        def pair(n):
            stage(2 * n, 0, None, True)
            stage(2 * n + 1, 1, None, True)

        def quad(m, _):
            pair(j + 1 + 2 * m)
            pair(j + 2 + 2 * m)
            return 0

        n_rest = nq - 1 - j
        lax.fori_loop(0, n_rest // 2, quad, 0)

        @pl.when(n_rest % 2 == 1)
        def _():
            pair(nq - 1)

        accumulate(last, 1)
        dk_ref[...] = dk_scr[:, :d] * LN2
        dv_ref[...] = dv_scr[...]
        dfk_ref[...] = _col_to_row(dfk_scr[...])

    tile = lambda h, j: (h, j, 0)
    whole = lambda h, j: (h, 0, 0)
    rowtile = lambda h, j: (h, 0, j)
    return pl.pallas_call(
        kern, name="fox_bwd", grid=(H, S // tk),
        in_specs=[pl.BlockSpec((None, S, da), whole),
                  pl.BlockSpec((None, tk, da), tile),
                  pl.BlockSpec((None, d, tk), lambda h, j: (h, 0, j)),
                  pl.BlockSpec((None, tk, 128), tile),
                  pl.BlockSpec((None, S, d), whole),
                  pl.BlockSpec((None, 1, S), whole),
                  pl.BlockSpec((None, 1, S), whole)],
        out_specs=[pl.BlockSpec((None, d, S), whole),
                   pl.BlockSpec((None, tk, d), tile),
                   pl.BlockSpec((None, tk, d), tile),
                   pl.BlockSpec((None, 1, tk), rowtile),
                   pl.BlockSpec((None, 1, S), whole)],
        out_shape=[jax.ShapeDtypeStruct((H, d, S), F32), jax.ShapeDtypeStruct((H, S, d), F32),
                   jax.ShapeDtypeStruct((H, S, d), F32), jax.ShapeDtypeStruct((H, 1, S), F32),
                   jax.ShapeDtypeStruct((H, 1, S), F32)],
        scratch_shapes=[pltpu.VMEM((2, tk, ts2), F32), pltpu.VMEM((2, tk, ts2), F32),
                        pltpu.VMEM((2, tk, ts2), BF16), pltpu.VMEM((2, tk, ts2), BF16),
                        pltpu.VMEM((tk, da), F32), pltpu.VMEM((tk, d), F32), pltpu.VMEM((tk, 1), F32)],
        compiler_params=_params(("parallel", "arbitrary")),
    )(qs, k_aug, kT, v, do, lse_row, delta_row)


def _heads_major(a, H, d):
    S = a.shape[0]
    return a.reshape(S, H, d).transpose(1, 0, 2)


def _heads_minor(a):
    H, S, d = a.shape
    return a.transpose(1, 0, 2).reshape(S, H * d)


def _lane_pick(x128, lane):
    return x128[:, lane:lane + 1]


def _l2_fwd(y):
    return lax.rsqrt(jnp.sum(y * y, axis=1, keepdims=True) + EPS)


def _gdn_prep(z, conv_w, a128, dt128, ts):
    S = z.shape[0]
    C3 = 3 * WIDTH
    hb = ts // 8

    def body(t, first, ins, outs, scratch):
        x_ref, halo_ref, zs_ref, w_ref, a_ref, dt_ref = ins
        qkv_ref, c_ref, gb_ref, gbT_ref = outs
        halo = jnp.where(t > 0, halo_ref[...], 0.0)
        xe = jnp.concatenate([halo, x_ref[...]], axis=0)
        w = w_ref[...]
        c = w[3:4, :] * xe[8:, :]
        for back in (1, 2, 3):
            c = c + w[3 - back:4 - back, :] * pltpu.roll(xe, back, 0)[8:, :]
        c_ref[...] = c
        y = c * _sigmoid(c)
        for h in range(GDN_HEADS):
            lo = h * GDN_DIM
            yq = y[:, lo:lo + GDN_DIM]
            qkv_ref[:, lo:lo + GDN_DIM] = yq * (_l2_fwd(yq) * (GDN_DIM ** -0.5))
            yk = y[:, WIDTH + lo:WIDTH + lo + GDN_DIM]
            qkv_ref[:, WIDTH + lo:WIDTH + lo + GDN_DIM] = yk * _l2_fwd(yk)
        qkv_ref[:, 2 * WIDTH:] = y[:, 2 * WIDTH:]
        zs = zs_ref[...]
        lane = lax.broadcasted_iota(jnp.int32, zs.shape, 1)
        g = -jnp.exp(a_ref[...]) * _softplus(zs + dt_ref[...])
        G = _scan_rows(g, ts, CHUNK)
        beta = _sigmoid(zs)
        out = jnp.where(lane < 8, pltpu.roll(g, 128 - LANE_BA, 1), jnp.where(lane < LANE_BB, G, beta))
        gb_ref[...] = out
        gbT_ref[...] = _transpose_exact(out)

    x_in = (z, (ts, C3), lambda i: (i, CB_BQKV))
    halo_in = (z, (8, C3), lambda i: (jnp.maximum(i * hb - 1, 0), CB_BQKV))
    return _tiled("gdn_prep", body, S // ts,
                  [x_in, halo_in, _cols(z, ts, 128, CB_SMALL), _full(conv_w), _full(a128), _full(dt128)],
                  [_orow(S, (C3,), F32, ts), _orow(S, (C3,), F32, ts), _orow(S, (128,), F32, ts),
                   ((128, S), F32, (128, ts), lambda i: (0, i))])


def _chunk_masks(nc):
    r = lax.broadcasted_iota(jnp.int32, (nc, CHUNK, CHUNK), 1)
    c = lax.broadcasted_iota(jnp.int32, (nc, CHUNK, CHUNK), 2)
    return c <= r, c < r, c == r


def _chunk_local(qh, kh, vh, Gc, Gr, beta):
    nc = qh.shape[0]
    incl, strict, _ = _chunk_masks(nc)
    gamma = jnp.exp(jnp.where(incl, Gc - Gr, NEG))
    kb = kh * beta
    P = _bdot(_b16(kb), _b16(kh), 2, 2)
    Qk = _bdot(_b16(qh), _b16(kh), 2, 2)
    eG = jnp.exp(Gc)
    Gl = Gc[:, CHUNK - 1:CHUNK, :]
    edec = jnp.exp(Gl - Gc)
    return incl, strict, gamma, kb, P, Qk, eG, edec


def _gdn_local_fwd(qkv, gb, grow, ts):
    S = qkv.shape[0]
    nc = ts // CHUNK

    def body(t, first, ins, outs, scratch):
        q_ref, k_ref, v_ref, gb_ref, gr_ref = ins
        u_ref, w_ref, qd_ref, kd_ref, aqk_ref, T_ref = outs
        gbv = gb_ref[...]
        for h in range(GDN_HEADS):
            lo = h * GDN_DIM
            qh = q_ref[:, lo:lo + GDN_DIM].reshape(nc, CHUNK, GDN_DIM)
            kh = k_ref[:, lo:lo + GDN_DIM].reshape(nc, CHUNK, GDN_DIM)
            vh = v_ref[:, lo:lo + GDN_DIM].reshape(nc, CHUNK, GDN_DIM)
            Gc = _lane_pick(gbv, LANE_BA + h).reshape(nc, CHUNK, 1)
            beta = _lane_pick(gbv, LANE_BB + h).reshape(nc, CHUNK, 1)
            Gr = gr_ref[h].reshape(nc, 1, CHUNK)
            incl, strict, gamma, kb, P, Qk, eG, edec = _chunk_local(qh, kh, vh, Gc, Gr, beta)
            A = jnp.where(strict, P * gamma, 0.0)
            _, _, eye = _chunk_masks(nc)
            T = jnp.where(eye, 1.0, 0.0) - A
            X = A
            for _ in range(5):
                X = _bdot(X, X, 2, 1, PREC_UT)
                T = T + _bdot(T, X, 2, 1, PREC_UT)
            u = _bdot(T, vh * beta, 2, 1, PREC_UT)
            w = _bdot(T, kb * eG, 2, 1, PREC_UT)
            u_ref[:, lo:lo + GDN_DIM] = u.reshape(ts, GDN_DIM)
            w_ref[:, lo:lo + GDN_DIM] = w.reshape(ts, GDN_DIM)
            qd_ref[:, lo:lo + GDN_DIM] = (qh * eG).reshape(ts, GDN_DIM)
            kd_ref[:, lo:lo + GDN_DIM] = (kh * edec).reshape(ts, GDN_DIM)
            aqk_ref[h] = jnp.where(incl, Qk * gamma, 0.0).reshape(ts, CHUNK)
            T_ref[h] = T.reshape(ts, CHUNK)

    wide = _orow(S, (WIDTH,), F32, ts)
    perhead = ((GDN_HEADS, S, CHUNK), F32, (GDN_HEADS, ts, CHUNK), lambda i: (0, i, 0))
    return _tiled("gdn_local_fwd", body, S // ts,
                  [_cols(qkv, ts, WIDTH, 0), _cols(qkv, ts, WIDTH, 1), _cols(qkv, ts, WIDTH, 2),
                   _rows(gb, ts), (grow, (GDN_HEADS, nc, CHUNK), lambda i: (0, i, 0))],
                  [wide, wide, wide, wide, perhead, perhead])


def _gdn_scan_fwd(u, w, qd, kd, aqk, gb, ts):
    S = u.shape[0]
    nc = ts // CHUNK
    N = S // CHUNK

    def body(t, first, ins, outs, scratch):
        u_ref, w_ref, qd_ref, kd_ref, aqk_ref, gb_ref = ins
        o_ref, vn_ref, st_ref = outs
        (state,) = scratch

        @pl.when(first)
        def _():
            state[...] = jnp.zeros_like(state)

        def chunk(c, _):
            r0 = pl.multiple_of(c * CHUNK, CHUNK)
            rows = pl.ds(r0, CHUNK)
            glast = gb_ref[pl.ds(r0 + CHUNK - 1, 1), :]
            heads = range(GDN_HEADS)
            cols = [slice(h * GDN_DIM, (h + 1) * GDN_DIM) for h in heads]
            S_old = [state[h] for h in heads]
            u_h = [u_ref[rows, cols[h]] for h in heads]
            w_h = [_b16(w_ref[rows, cols[h]]) for h in heads]
            qd_h = [_b16(qd_ref[rows, cols[h]]) for h in heads]
            kd_h = [_b16(kd_ref[rows, cols[h]]) for h in heads]
            aqk_h = [_b16(aqk_ref[h, rows, :]) for h in heads]
            S_new, o_h, vn_h = [], [], []
            for h in heads:
                Sb = _b16(S_old[h])
                both = _dot(jnp.concatenate([w_h[h], qd_h[h]], axis=0), Sb)
                vn = u_h[h] - both[:CHUNK]
                vnb = _b16(vn)
                o_h.append(both[CHUNK:] + _dot(aqk_h[h], vnb))
                egl = jnp.exp(glast[:, LANE_BA + h:LANE_BA + h + 1])
                S_new.append(S_old[h] * egl + _dot(kd_h[h], vnb, TN))
                vn_h.append(vn)
            for h in heads:
                st_ref[c, h] = S_old[h]
                state[h] = S_new[h]
                o_ref[rows, cols[h]] = o_h[h]
                vn_ref[rows, cols[h]] = vn_h[h]
            return 0

        lax.fori_loop(0, nc, chunk, 0)

    wide_in = lambda a: _rows(a, ts)
    wide = _orow(S, (WIDTH,), F32, ts)
    states = ((N, GDN_HEADS, GDN_DIM, GDN_DIM), F32, (nc, GDN_HEADS, GDN_DIM, GDN_DIM),
              lambda i: (i, 0, 0, 0))
    return _tiled("gdn_scan_fwd", body, S // ts,
                  [wide_in(u), wide_in(w), wide_in(qd), wide_in(kd),
                   (aqk, (GDN_HEADS, ts, CHUNK), lambda i: (0, i, 0)), _rows(gb, ts)],
                  [wide, wide, states],
                  scratch=[pltpu.VMEM((GDN_HEADS, GDN_DIM, GDN_DIM), F32)])


def _gdn_scan_bwd(do, w, qd, kd, aqk, vn, states, gb, ts):
    S = do.shape[0]
    nc = ts // CHUNK
    N = S // CHUNK

    def body(t, first, ins, outs, scratch):
        do_ref, w_ref, qd_ref, kd_ref, aqk_ref, vn_ref, st_ref, gb_ref = ins
        du_ref, dw_ref, dqd_ref, dkd_ref, daqk_ref, dgl_ref = outs
        (dstate,) = scratch

        @pl.when(first)
        def _():
            dstate[...] = jnp.zeros_like(dstate)

        r = lax.broadcasted_iota(jnp.int32, (CHUNK, CHUNK), 0)
        cc = lax.broadcasted_iota(jnp.int32, (CHUNK, CHUNK), 1)
        incl = cc <= r
        lane = lax.broadcasted_iota(jnp.int32, (1, 128), 1)

        def chunk(k, _):
            c = nc - 1 - k
            r0 = pl.multiple_of(c * CHUNK, CHUNK)
            rows = pl.ds(r0, CHUNK)
            glast = gb_ref[pl.ds(r0 + CHUNK - 1, 1), :]
            dgl_row = jnp.zeros((1, 128), F32)
            heads = range(GDN_HEADS)
            cols = [slice(h * GDN_DIM, (h + 1) * GDN_DIM) for h in heads]
            S_h = [st_ref[c, h] for h in heads]
            dS_h = [dstate[h] for h in heads]
            do_h = [_b16(do_ref[rows, cols[h]]) for h in heads]
            aqk_h = [_b16(aqk_ref[h, rows, :]) for h in heads]
            vn_h = [_b16(vn_ref[rows, cols[h]]) for h in heads]
            kd_h = [_b16(kd_ref[rows, cols[h]]) for h in heads]
            qd_h = [_b16(qd_ref[rows, cols[h]]) for h in heads]
            w_h = [_b16(w_ref[rows, cols[h]]) for h in heads]
            res = []
            for h in heads:
                Sb, dSb, dob, vnb = _b16(S_h[h]), _b16(dS_h[h]), do_h[h], vn_h[h]
                dvn = _dot(aqk_h[h], dob, TN) + _dot(kd_h[h], dSb)
                dvnb = _b16(dvn)
                daqk = jnp.where(incl, _dot(dob, vnb, NT), 0.0)
                both = jnp.concatenate([dob, dvnb], axis=0)
                by_state = _dot(both, Sb, NT)
                dqd = by_state[:CHUNK]
                dw = -by_state[CHUNK:]
                dkd = _dot(vnb, dSb, NT)
                egl = jnp.exp(glast[:, LANE_BA + h:LANE_BA + h + 1])
                dgl = egl * jnp.sum(jnp.sum(dS_h[h] * S_h[h], axis=1, keepdims=True), axis=0,
                                    keepdims=True)
                dgl_row = jnp.where(lane == h, dgl, dgl_row)
                dS_new = _dot(jnp.concatenate([qd_h[h], -w_h[h]], axis=0), both, TN) + egl * dS_h[h]
                res.append((daqk, dqd, dkd, dw, dvn, dS_new))
            for h in heads:
                daqk, dqd, dkd, dw, dvn, dS_new = res[h]
                daqk_ref[h, rows, :] = daqk
                dqd_ref[rows, cols[h]] = dqd
                dkd_ref[rows, cols[h]] = dkd
                dw_ref[rows, cols[h]] = dw
                du_ref[rows, cols[h]] = dvn
                dstate[h] = dS_new
            dgl_ref[pl.ds(c, 1), :] = dgl_row
            return 0

        lax.fori_loop(0, nc, chunk, 0)

    wide_in = lambda a: _rows(a, ts)
    wide = _orow(S, (WIDTH,), F32, ts)
    perhead_in = lambda a: (a, (GDN_HEADS, ts, CHUNK), lambda i: (0, i, 0))
    perhead = ((GDN_HEADS, S, CHUNK), F32, (GDN_HEADS, ts, CHUNK), lambda i: (0, i, 0))
    return _tiled("gdn_scan_bwd", body, S // ts,
                  [wide_in(do), wide_in(w), wide_in(qd), wide_in(kd), perhead_in(aqk), wide_in(vn),
                   (states, (nc, GDN_HEADS, GDN_DIM, GDN_DIM), lambda i: (i, 0, 0, 0)), _rows(gb, ts)],
                  [wide, wide, wide, wide, perhead, ((N, 128), F32, (nc, 128), lambda i: (i, 0))],
                  scratch=[pltpu.VMEM((GDN_HEADS, GDN_DIM, GDN_DIM), F32)], reverse=True)


def _gdn_local_bwd(qkv, gb, grow, T, du, dw, dqd, dkd, daqk, dgl, ts):
    S = qkv.shape[0]
    nc = ts // CHUNK

    def body(t, first, ins, outs, scratch):
        (q_ref, k_ref, v_ref, gb_ref, gr_ref, T_ref, du_ref, dw_ref, dqd_ref, dkd_ref,
         daqk_ref, dgl_ref) = ins
        dqkv_ref, dgb_ref = outs
        gbv = gb_ref[...]
        dglv = dgl_ref[...]
        lane = lax.broadcasted_iota(jnp.int32, (ts, 128), 1)
        dG_all = jnp.zeros((ts, 128), F32)
        dbeta_all = jnp.zeros((ts, 128), F32)
        for h in range(GDN_HEADS):
            lo = h * GDN_DIM
            cols = slice(lo, lo + GDN_DIM)
            r3 = lambda ref: ref[:, cols].reshape(nc, CHUNK, GDN_DIM)
            qh, kh, vh = r3(q_ref), r3(k_ref), r3(v_ref)
            duh, dwh, dqdh, dkdh = r3(du_ref), r3(dw_ref), r3(dqd_ref), r3(dkd_ref)
            Gc = _lane_pick(gbv, LANE_BA + h).reshape(nc, CHUNK, 1)
            beta = _lane_pick(gbv, LANE_BB + h).reshape(nc, CHUNK, 1)
            Gr = gr_ref[h].reshape(nc, 1, CHUNK)
            Th = T_ref[h].reshape(nc, CHUNK, CHUNK)
            daq = daqk_ref[h].reshape(nc, CHUNK, CHUNK)
            incl, strict, gamma, kb, P, Qk, eG, edec = _chunk_local(qh, kh, vh, Gc, Gr, beta)
            _, _, eye = _chunk_masks(nc)
            vb = vh * beta
            kbg = kb * eG
            dvb = _bdot(Th, duh, 1, 1, PREC_UT)
            dkbg = _bdot(Th, dwh, 1, 1, PREC_UT)
            dT = _bdot(duh, vb, 2, 2, PREC_UT) + _bdot(dwh, kbg, 2, 2, PREC_UT)
            M1 = _bdot(Th, dT, 1, 1, PREC_UT)
            dA = jnp.where(strict, -_bdot(M1, Th, 2, 2, PREC_UT), 0.0)
            dP = dA * gamma
            dQ = daq * gamma
            dgam = (dA * P + daq * Qk) * gamma
            dPb, dQb = _b16(dP), _b16(dQ)
            khb, qhb, kbb = _b16(kh), _b16(qh), _b16(kb)
            dq = _bdot(dQb, khb, 2, 1) + dqdh * eG
            dkb = _bdot(dPb, khb, 2, 1) + dkbg * eG
            dk = (_bdot(dQb, qhb, 1, 1) + _bdot(dPb, kbb, 1, 1) + dkdh * edec + dkb * beta)
            dbeta = (jnp.sum(dkb * kh, axis=2, keepdims=True) + jnp.sum(dvb * vh, axis=2, keepdims=True))
            dv = dvb * beta
            col_as_col = jnp.sum(jnp.where(eye, jnp.sum(dgam, axis=1, keepdims=True), 0.0),
                                 axis=2, keepdims=True)
            kd_term = jnp.sum(dkdh * kh * edec, axis=2, keepdims=True)
            dG = (jnp.sum(dgam, axis=2, keepdims=True) - col_as_col
                  + jnp.sum(dqdh * qh * eG, axis=2, keepdims=True)
                  + jnp.sum(dkbg * kbg, axis=2, keepdims=True) - kd_term)
            dgl_h = dglv[:, h:h + 1].reshape(nc, 1, 1) + jnp.sum(kd_term, axis=1, keepdims=True)
            last = lax.broadcasted_iota(jnp.int32, (nc, CHUNK, 1), 1) == CHUNK - 1
            dG = dG + jnp.where(last, dgl_h, 0.0)
            dqkv_ref[:, cols] = dq.reshape(ts, GDN_DIM)
            dqkv_ref[:, WIDTH + lo:WIDTH + lo + GDN_DIM] = dk.reshape(ts, GDN_DIM)
            dqkv_ref[:, 2 * WIDTH + lo:2 * WIDTH + lo + GDN_DIM] = dv.reshape(ts, GDN_DIM)
            dG_all = jnp.where(lane == LANE_BA + h, dG.reshape(ts, 1), dG_all)
            dbeta_all = jnp.where(lane == LANE_BB + h, dbeta.reshape(ts, 1), dbeta_all)
        dg_all = _scan_rows(dG_all, ts, CHUNK, reverse=True)
        dgb_ref[...] = jnp.where(lane < LANE_BB, dg_all, dbeta_all)

    wide_in = lambda a: _rows(a, ts)
    perhead_in = lambda a: (a, (GDN_HEADS, ts, CHUNK), lambda i: (0, i, 0))
    return _tiled("gdn_local_bwd", body, S // ts,
                  [_cols(qkv, ts, WIDTH, 0), _cols(qkv, ts, WIDTH, 1), _cols(qkv, ts, WIDTH, 2),
                   _rows(gb, ts), (grow, (GDN_HEADS, nc, CHUNK), lambda i: (0, i, 0)), perhead_in(T),
                   wide_in(du), wide_in(dw), wide_in(dqd), wide_in(dkd), perhead_in(daqk),
                   (dgl, (nc, 128), lambda i: (i, 0))],
                  [_orow(S, (3 * WIDTH,), F32, ts), _orow(S, (128,), F32, ts)])


def _gdn_prep_bwd(dqkv, dgb, cpre, z, conv_w, a128, dt128, dz, ts):
    S = z.shape[0]
    C3 = 3 * WIDTH
    hb = ts // 8
    n_tiles = S // ts

    def dpre(dq, c):
        y, dsil = _silu_and_grad(c)
        parts = []
        for h in range(GDN_HEADS):
            lo = h * GDN_DIM
            yq = y[:, lo:lo + GDN_DIM]
            rq = _l2_fwd(yq)
            nq = yq * rq
            dn = dq[:, lo:lo + GDN_DIM] * (GDN_DIM ** -0.5)
            parts.append(rq * (dn - nq * jnp.sum(dn * nq, axis=1, keepdims=True)))
        for h in range(GDN_HEADS):
            lo = WIDTH + h * GDN_DIM
            yk = y[:, lo:lo + GDN_DIM]
            rk = _l2_fwd(yk)
            nk = yk * rk
            dn = dq[:, lo:lo + GDN_DIM]
            parts.append(rk * (dn - nk * jnp.sum(dn * nk, axis=1, keepdims=True)))
        parts.append(dq[:, 2 * WIDTH:])
        return jnp.concatenate(parts, axis=1) * dsil

    def body(t, first, ins, outs, scratch):
        (dq_ref, dqn_ref, c_ref, cn_ref, x_ref, xp_ref, zs_ref, dgb_ref, w_ref, a_ref, dt_ref) = ins
        dx_ref, dzs_ref, dw_ref, dad_ref = outs

        @pl.when(first)
        def _():
            dw_ref[...] = jnp.zeros_like(dw_ref)
            dad_ref[...] = jnp.zeros_like(dad_ref)

        dc = dpre(dq_ref[...], c_ref[...])
        dcn = jnp.where(t < n_tiles - 1, dpre(dqn_ref[...], cn_ref[...]), 0.0)
        dce = jnp.concatenate([dc, dcn], axis=0)
        w = w_ref[...]
        dx = w[3:4, :] * dc
        for back in (1, 2, 3):
            dx = dx + w[3 - back:4 - back, :] * pltpu.roll(dce, ts + 8 - back, 0)[:ts, :]
        dx_ref[...] = _b16(dx)
        halo = jnp.where(t > 0, xp_ref[...], 0.0)
        xe = jnp.concatenate([halo, x_ref[...]], axis=0)
        dw_ref[3:4, :] += jnp.sum(dc * xe[8:, :], axis=0, keepdims=True)
        for back in (1, 2, 3):
            dw_ref[3 - back:4 - back, :] += jnp.sum(dc * pltpu.roll(xe, back, 0)[8:, :], axis=0,
                                                     keepdims=True)
        zs = zs_ref[...]
        dgb = dgb_ref[...]
        lane = lax.broadcasted_iota(jnp.int32, zs.shape, 1)
        arg = zs + dt_ref[...]
        nega = -jnp.exp(a_ref[...])
        dba = dgb * nega * _sigmoid(arg)
        beta = _sigmoid(zs)
        dbb = dgb * beta * (1.0 - beta)
        dzs_ref[...] = jnp.where((lane >= LANE_BA) & (lane < LANE_BB), dba,
                                 jnp.where((lane >= LANE_BB) & (lane < LANE_BB + 4), dbb, 0.0))
        dad_ref[0:1, :] += jnp.sum(dgb * nega * _softplus(arg), axis=0, keepdims=True)
        dad_ref[1:2, :] += jnp.sum(dba, axis=0, keepdims=True)

    nxt = lambda i: (jnp.minimum((i + 1) * hb, S // 8 - 1), 0)
    prv = lambda i: (jnp.maximum(i * hb - 1, 0), CB_BQKV)
    return _tiled("gdn_prep_bwd", body, n_tiles,
                  [_rows(dqkv, ts), (dqkv, (8, C3), nxt), _rows(cpre, ts), (cpre, (8, C3), nxt),
                   (z, (ts, C3), lambda i: (i, CB_BQKV)), (z, (8, C3), prv),
                   _cols(z, ts, 128, CB_SMALL), _rows(dgb, ts), _full(conv_w), _full(a128), _full(dt128)],
                  [((S, N_AL), BF16, (ts, C3), lambda i: (i, CB_BQKV)), _orow(S, (128,), F32, ts),
                   _oacc((8, C3), F32), _oacc((8, 128), F32)],
                  fill=(dz, 0))


def _mem_attn_fwd(z, mk, mv, ts):
    S = z.shape[0]

    def body(t, first, ins, outs, scratch):
        q_ref, mk_ref, mv_ref = ins
        (o_ref,) = outs
        for h in range(MEM_HEADS):
            cols = slice(h * MEM_DIM, (h + 1) * MEM_DIM)
            s = _dot(_b16(q_ref[:, cols]), _b16(mk_ref[:, cols]), NT) * (MEM_DIM ** -0.5)
            m = jnp.max(s, axis=1, keepdims=True)
            p = jnp.exp(s - m)
            p = p / jnp.sum(p, axis=1, keepdims=True)
            o_ref[:, cols] = _dot(_b16(p), _b16(mv_ref[:, cols]))

    (o,) = _tiled("mem_attn_fwd", body, S // ts, [_cols(z, ts, WIDTH, CB_MQ), _full(mk), _full(mv)],
                  [_orow(S, (WIDTH,), F32, ts)])
    return o


def _mem_attn_bwd(do, z, mk, mv, dz, ts):
    S = z.shape[0]
    M = mk.shape[0]

    def body(t, first, ins, outs, scratch):
        do_ref, q_ref, mk_ref, mv_ref = ins
        dq_ref, dmk_ref, dmv_ref = outs

        @pl.when(first)
        def _():
            dmk_ref[...] = jnp.zeros_like(dmk_ref)
            dmv_ref[...] = jnp.zeros_like(dmv_ref)

        scale = MEM_DIM ** -0.5
        for h in range(MEM_HEADS):
            cols = slice(h * MEM_DIM, (h + 1) * MEM_DIM)
            qb = _b16(q_ref[:, cols])
            kb = _b16(mk_ref[:, cols])
            dob = _b16(do_ref[:, cols])
            s = _dot(qb, kb, NT) * scale
            m = jnp.max(s, axis=1, keepdims=True)
            p = jnp.exp(s - m)
            p = p / jnp.sum(p, axis=1, keepdims=True)
            dmv_ref[:, cols] += _dot(_b16(p), dob, TN)
            dp = _dot(dob, _b16(mv_ref[:, cols]), NT)
            ds = p * (dp - jnp.sum(dp * p, axis=1, keepdims=True)) * scale
            dsb = _b16(ds)
            dq_ref[:, cols] = _b16(_dot(dsb, kb))
            dmk_ref[:, cols] += _dot(dsb, qb, TN)

    return _tiled("mem_attn_bwd", body, S // ts,
                  [_rows(do, ts), _cols(z, ts, WIDTH, CB_MQ), _full(mk), _full(mv)],
                  [((S, N_AL), BF16, (ts, WIDTH), lambda i: (i, CB_MQ)), _oacc((M, WIDTH), F32),
                   _oacc((M, WIDTH), F32)],
                  fill=(dz, 0))


def _head_norm(ob, g):
    xs, rs = [], []
    for h in range(GDN_HEADS):
        o = ob[:, h * GDN_DIM:(h + 1) * GDN_DIM]
        r = lax.rsqrt(jnp.mean(o * o, axis=1, keepdims=True) + EPS)
        xs.append(o * r)
        rs.append(r)
    return xs, rs


def _merge_fwd(x, z, o_a, o_b, o_m, gdn_g, b_merge, wb, wout, ts):
    S, D = x.shape

    def body(t, first, ins, outs, scratch):
        (x_ref, g_ref, oa_ref, az_ref, ob_ref, bz_ref, om_ref, mz_ref, gg_ref, bm_ref, wb_ref,
         wo_ref) = ins
        xo_ref, ya_ref, yb_ref, ym_ref, mg_ref = outs
        ya = oa_ref[...] * _silu_and_grad(az_ref[...])[0]
        xs, _ = _head_norm(ob_ref[...], None)
        nb = jnp.concatenate([xh * gg_ref[...] for xh in xs], axis=1)
        yb = nb * _silu_and_grad(bz_ref[...])[0]
        ym = om_ref[...] * _silu_and_grad(mz_ref[...])[0]
        merged = jnp.zeros((ts, D), F32)
        for n, (y, y_ref) in enumerate(((ya, ya_ref), (yb, yb_ref), (ym, ym_ref))):
            yb16 = _b16(y)
            y_ref[...] = yb16
            gate = _sigmoid(g_ref[:, n * D:(n + 1) * D] + bm_ref[:, n * D:(n + 1) * D])
            merged = merged + gate * _dot(yb16, wb_ref[n])
        mb = _b16(merged)
        mg_ref[...] = mb
        xo_ref[...] = x_ref[...] + _dot(mb, wo_ref[...])

    half = lambda a: _rows(a, ts)
    return _tiled("merge_fwd", body, S // ts,
                  [_rows(x, ts), _cols(z, ts, 3 * D, CB_GATES), half(o_a), _cols(z, ts, WIDTH, CB_AZ),
                   half(o_b), _cols(z, ts, WIDTH, CB_BZ), half(o_m), _cols(z, ts, WIDTH, CB_MZ),
                   _full(gdn_g.reshape(1, GDN_DIM)), _full(b_merge.reshape(1, 3 * D)), _full(wb), _full(wout)],
                  [_orow(S, (D,), F32, ts), _orow(S, (WIDTH,), BF16, ts), _orow(S, (WIDTH,), BF16, ts),
                   _orow(S, (WIDTH,), BF16, ts), _orow(S, (D,), BF16, ts)])


def _merge_bwd(dout, z, o_a, o_b, o_m, ya, yb, ym, gdn_g, b_merge, wb, wout, hsum, ts):
    S, D = dout.shape

    def body(t, first, ins, outs, scratch):
        (do_ref, g_ref, oa_ref, az_ref, ob_ref, bz_ref, om_ref, mz_ref, ya_ref, yb_ref, ym_ref,
         gg_ref, bm_ref, wb_ref, wo_ref, hs_ref) = ins
        (dg_ref, dpa_ref, dpb_ref, dpm_ref, doa_ref, dob_ref, dom_ref, dl_ref, dbm_ref, dgg_ref) = outs
        G3 = 3 * D

        @pl.when(first)
        def _():
            dbm_ref[...] = jnp.zeros_like(dbm_ref)
            dgg_ref[...] = jnp.zeros_like(dgg_ref)

        dmerged = _dot(_b16(do_ref[...]), wo_ref[...], NT)
        dys = []
        for n, (y_ref, dp_ref) in enumerate(((ya_ref, dpa_ref), (yb_ref, dpb_ref), (ym_ref, dpm_ref))):
            sl = slice(n * D, (n + 1) * D)
            gate = _sigmoid(g_ref[:, sl] + bm_ref[:, sl])
            proj = _dot(y_ref[...], wb_ref[n])
            dproj = _b16(gate * dmerged)
            dp_ref[...] = dproj
            dgp = dmerged * proj * gate * (1.0 - gate)
            dg_ref[:, sl] = dgp.astype(dg_ref.dtype)
            dbm_ref[0:1, sl] += jnp.sum(dgp, axis=0, keepdims=True)
            dys.append(_dot(dproj, wb_ref[n], NT))
        dya, dyb, dym = dys
        sa, dsa = _silu_and_grad(az_ref[...])
        oa = oa_ref[...]
        doa = dya * sa
        doa_ref[...] = doa
        dg_ref[:, G3:G3 + WIDTH] = _b16(dya * oa * dsa)
        dl_ref[...] = _dot(hs_ref[...], doa * oa, NT, HIGHEST)
        sm, dsm = _silu_and_grad(mz_ref[...])
        dom_ref[...] = dym * sm
        dg_ref[:, G3 + 2 * WIDTH:G3 + 3 * WIDTH] = _b16(dym * om_ref[...] * dsm)
        sb, dsb = _silu_and_grad(bz_ref[...])
        xs, rs = _head_norm(ob_ref[...], None)
        gg = gg_ref[...]
        dgg = jnp.zeros((1, GDN_DIM), F32)
        for h in range(GDN_HEADS):
            cols = slice(h * GDN_DIM, (h + 1) * GDN_DIM)
            dn = dyb[:, cols] * sb[:, cols]
            dg_ref[:, G3 + WIDTH + h * GDN_DIM:G3 + WIDTH + (h + 1) * GDN_DIM] = _b16(
                dyb[:, cols] * (xs[h] * gg) * dsb[:, cols])
            dgg = dgg + jnp.sum(dn * xs[h], axis=0, keepdims=True)
            dxh = dn * gg
            dob_ref[:, cols] = rs[h] * (dxh - xs[h] * jnp.mean(dxh * xs[h], axis=1, keepdims=True))
        dgg_ref[0:1, :] += dgg

    half = lambda a: _rows(a, ts)
    w512 = lambda dt: _orow(S, (WIDTH,), dt, ts)
    return _tiled("merge_bwd", body, S // ts,
                  [_rows(dout, ts), _cols(z, ts, 3 * D, CB_GATES), half(o_a), _cols(z, ts, WIDTH, CB_AZ),
                   half(o_b), _cols(z, ts, WIDTH, CB_BZ), half(o_m), _cols(z, ts, WIDTH, CB_MZ),
                   half(ya), half(yb), half(ym), _full(gdn_g.reshape(1, GDN_DIM)),
                   _full(b_merge.reshape(1, 3 * D)), _full(wb), _full(wout), _full(hsum)],
                  [((S, N_AL), BF16, (ts, 3 * D + 3 * WIDTH), lambda i: (i, CB_MERGE)),
                   _orow(S, (D,), BF16, ts), _orow(S, (D,), BF16, ts),
                   _orow(S, (D,), BF16, ts), w512(F32), w512(F32), w512(F32),
                   ((128, S), F32, (128, ts), lambda i: (0, i)), _oacc((8, 3 * D), F32),
                   _oacc((8, GDN_DIM), F32)])


def _to_aligned(w):
    parts = [w[..., lo:lo + n] for lo, n, _ in sorted(W_IN_PIECES, key=lambda p: p[2])]
    parts.append(jnp.zeros(w.shape[:-1] + (N_AL - N_IN,), w.dtype))
    return jnp.concatenate(parts, axis=-1)


def _from_aligned(w):
    return jnp.concatenate([w[..., al:al + n] for _, n, al in W_IN_PIECES], axis=-1)


def _lanes128(v, lane0):
    return jnp.pad(v.astype(F32)[None, :], ((0, 0), (lane0, 128 - lane0 - v.shape[0])))


def _tiles(S):
    ts = min(512, S // 2)
    return dict(ts=ts, ts_small=min(256, S // 2), tq=min(512, S // 4), tq_fwd=min(1024, S // 2))


def _layer_fwd(x, mem, p):
    S = x.shape[0]
    tl = _tiles(S)
    ts, tss, tq = tl["ts"], tl["ts_small"], tl["tq"]
    h, rstd = _rms_fwd("norm_fwd", x, p["norm_g"], ts)
    z = _mm("in_proj", h, p["w_in_al"], tn=1664)

    b_fg128 = _lanes128(p["b_fg"], LANE_AF)
    f_hi, f_mid, f_lo = _fox_decay(z, b_fg128, ts)
    aq = z[:, CB_AQ * WIDTH:(CB_AQ + 1) * WIDTH]
    ak = z[:, CB_AK * WIDTH:(CB_AK + 1) * WIDTH]
    av = z[:, CB_AV * WIDTH:(CB_AV + 1) * WIDTH]
    q32 = _heads_major(aq, FOX_HEADS, FOX_DIM)
    kh = _heads_major(ak, FOX_HEADS, FOX_DIM).astype(BF16)
    vh = _heads_major(av, FOX_HEADS, FOX_DIM).astype(BF16)
    piecesT = jnp.stack([f[:FOX_HEADS] for f in (f_hi, f_mid, f_lo)], axis=1)
    pieces = piecesT.transpose(0, 2, 1)
    ones3 = jnp.ones((FOX_HEADS, S, 3), BF16)
    padk = jnp.zeros((FOX_HEADS, S, FOX_AUG - FOX_DIM - 6), BF16)
    q_aug = jnp.concatenate([q32, pieces.astype(F32), ones3.astype(F32), padk.astype(F32)], axis=-1)
    k_aug = jnp.concatenate([kh, ones3, -pieces, padk], axis=-1)
    kT_aug = jnp.concatenate([kh.transpose(0, 2, 1), ones3.transpose(0, 2, 1), -piecesT,
                              padk.transpose(0, 2, 1)], axis=1)
    v_aug = jnp.concatenate([vh, ones3[:, :, :1], jnp.zeros((FOX_HEADS, S, 128 - FOX_DIM - 1), BF16)],
                            axis=-1)
    o_h, lse, qs = _fox_fwd(q_aug, kT_aug, v_aug, tl["tq_fwd"])
    o_a = _heads_minor(o_h)

    a128 = _lanes128(p["a_log"], LANE_BA)
    dt128 = _lanes128(p["dt_bias"], LANE_BA)
    qkv, cpre, gb, gbT = _gdn_prep(z, p["conv_w"], a128, dt128, ts)
    grow = gbT[LANE_BA:LANE_BA + GDN_HEADS].reshape(GDN_HEADS, S // CHUNK, CHUNK)
    u, w, qd, kd, aqk, T = _gdn_local_fwd(qkv, gb, grow, ts)
    o_b, vn, states = _gdn_scan_fwd(u, w, qd, kd, aqk, gb, ts)

    mem_h, mem_r = _rms_fwd("mem_norm_fwd", mem, p["mem_norm_g"], mem.shape[0])
    mkv = _mm("mem_kv", mem_h, p["w_mem_kv"])
    mk, mv = mkv[:, :WIDTH], mkv[:, WIDTH:]
    o_m = _mem_attn_fwd(z, mk, mv, ts)

    x_next, ya, yb, ym, merged = _merge_fwd(x, z, o_a, o_b, o_m, p["gdn_norm_g"], p["b_merge"],
                                            p["w_branch"], p["w_out"], ts)
    saved = dict(x=x, h=h, rstd=rstd, z=z, b_fg128=b_fg128, qs=qs, k_aug=k_aug, kT_aug=kT_aug, v_aug=v_aug, lse=lse, o_a=o_a, a128=a128, dt128=dt128, qkv=qkv, cpre=cpre, gb=gb,
                 grow=grow, w=w, qd=qd, kd=kd, aqk=aqk, T=T, o_b=o_b, vn=vn, states=states,
                 mem_h=mem_h, mem_r=mem_r, mk=mk, mv=mv, o_m=o_m, ya=ya, yb=yb, ym=ym, merged=merged)
    return x_next, saved


def _layer_bwd(dout, mem, p, s):
    S = dout.shape[0]
    tl = _tiles(S)
    ts, tss, tq = tl["ts"], tl["ts_small"], tl["tq"]
    z = s["z"]
    hsum = (jnp.arange(128)[:, None] == jnp.arange(WIDTH)[None, :] // FOX_DIM).astype(F32)
    (dz, dpa, dpb, dpm, do_a, do_b, do_m, deltaT, db_merge, dgdn_g) = _merge_bwd(
        dout, z, s["o_a"], s["o_b"], s["o_m"], s["ya"], s["yb"], s["ym"], p["gdn_norm_g"],
        p["b_merge"], p["w_branch"], p["w_out"], hsum, tss)
    g = {}
    g["b_merge"] = db_merge[0]
    g["gdn_norm_g"] = dgdn_g[0]
    g["w_out"] = _mm("dw_out", s["merged"], dout, ta=True)
    g["w_branch"] = jnp.stack([_mm("dw_branch", y, dp, ta=True)
                               for y, dp in ((s["ya"], dpa), (s["yb"], dpb), (s["ym"], dpm))])

    do_h = _heads_major(do_a, FOX_HEADS, FOX_DIM).astype(BF16)
    delta_row = deltaT[:FOX_HEADS, None, :]
    dqT, dk_h, dv_h, dfk, dfq = _fox_bwd(s["qs"], s["k_aug"], s["kT_aug"], s["v_aug"], do_h, s["lse"],
                                         delta_row, tq)
    daq = _heads_minor(dqT.transpose(0, 2, 1))
    dak = _heads_minor(dk_h)
    dav = _heads_minor(dv_h)
    daf128, db_fg = _fox_decay_bwd(dfk[:, 0, :], dfq[:, 0, :], z, s["b_fg128"], ts)
    g["b_fg"] = db_fg[:FOX_HEADS]

    du, dw, dqd, dkd, daqk, dgl = _gdn_scan_bwd(do_b, s["w"], s["qd"], s["kd"], s["aqk"], s["vn"],
                                                s["states"], s["gb"], ts)
    dqkv, dgb = _gdn_local_bwd(s["qkv"], s["gb"], s["grow"], s["T"], du, dw, dqd, dkd, daqk, dgl, ts)
    dz, dzs_b, dconv, dad = _gdn_prep_bwd(dqkv, dgb, s["cpre"], z, p["conv_w"], s["a128"],
                                          s["dt128"], dz, ts)
    g["conv_w"] = dconv[:4]
    g["a_log"] = dad[0, LANE_BA:LANE_BA + GDN_HEADS]
    g["dt_bias"] = dad[1, LANE_BA:LANE_BA + GDN_HEADS]

    dz, dmk, dmv = _mem_attn_bwd(do_m, z, s["mk"], s["mv"], dz, ts)
    dmkv = jnp.concatenate([dmk, dmv], axis=1)
    g["w_mem_kv"] = _mm("dw_mem_kv", s["mem_h"], dmkv, ta=True)
    dmem_h = _mm("dmem_h", dmkv, p["w_mem_kv"], tb=True)
    M = mem.shape[0]
    _, g["mem_norm_g"] = _rms_bwd("mem_norm_bwd", dmem_h, mem, s["mem_r"], p["mem_norm_g"],
                                  jnp.zeros_like(mem), M)

    lane = jnp.arange(128)[None, :]
    dsmall = jnp.where(lane < 8, daf128, dzs_b)
    daqkv = jnp.concatenate([_b16(daq), _b16(dak), _b16(dav)], axis=1)
    dz = lax.dynamic_update_slice(dz, daqkv, (0, CB_AQKV * 3 * WIDTH))
    dz = lax.dynamic_update_slice(dz, _b16(dsmall), (0, CB_SMALL * 128))
    g["w_in_al"] = _mm("dw_in", s["h"], dz, ta=True, tn=1664)
    dh = _mm("dh", dz, p["w_in_al"], tb=True, tk=1664)
    dx, g["norm_g"] = _rms_bwd("norm_bwd", dh, s["x"], s["rstd"], p["norm_g"], dout, ts)
    return dx, g


def _local_step(x, mem, layers, final_norm_g, loss_target):
    S = x.shape[0]
    saves = []
    cur = x
    for p in layers:
        cur, sv = _layer_fwd(cur, mem, p)
        saves.append(sv)
    dx, dgf, loss_lanes = _loss_head(cur, final_norm_g, loss_target, _tiles(S)["ts"])
    grads = [None] * len(layers)
    for l in reversed(range(len(layers))):
        dx, grads[l] = _layer_bwd(dx, mem, layers[l], saves[l])
    return loss_lanes, dx, grads, dgf


HBM_SPEC = pl.BlockSpec(memory_space=pltpu.HBM)


def _mesh_pos():
    return lax.axis_index("x"), lax.axis_index("y"), lax.axis_index("c")


def _comm_call(name, body, arrays, out_shapes, n_remote, n_local):
    n = len(arrays)

    def kern(*refs):
        body(refs[:n], refs[n:2 * n], refs[2 * n], refs[2 * n + 1], refs[2 * n + 2])

    return pl.pallas_call(
        kern, name=name, out_shape=out_shapes, in_specs=[HBM_SPEC] * n, out_specs=[HBM_SPEC] * n,
        scratch_shapes=[pltpu.SemaphoreType.DMA((n_remote,)), pltpu.SemaphoreType.DMA((n_remote,)),
                        pltpu.SemaphoreType.DMA((max(n_local, 1),))],
    )(*arrays)


def _remote(src, dst, send_sems, recv_sems, k, to):
    return pltpu.make_async_remote_copy(src_ref=src, dst_ref=dst, send_sem=send_sems.at[k],
                                        recv_sem=recv_sems.at[k], device_id=to, device_id_type=MESH_ID)


def _other_chips(mx, my):
    return [(1 - mx, my), (mx, 1 - my), (1 - mx, 1 - my)]


def _gather_chips(name, shards):
    n = len(shards)

    def body(ins, outs, send_sems, recv_sems, local_sems):
        mx, my, mc = _mesh_pos()
        me = 2 * mx + my
        sibling = (mx, my, 1 - mc)
        chips = _other_chips(mx, my)
        sends = []
        for a in range(n):
            for k, (px, py) in enumerate(chips):
                cp = _remote(ins[a].at[mc], outs[a].at[me, mc], send_sems, recv_sems, 6 * a + k,
                             (px, py, mc))
                cp.start()
                sends.append(cp)
        for a in range(n):
            for k, (px, py) in enumerate(chips):
                j = 2 * px + py
                _remote(ins[a].at[mc], outs[a].at[j, mc], send_sems, recv_sems, 6 * a + k,
                        (px, py, mc)).wait_recv()
                cp = _remote(outs[a].at[j, mc], outs[a].at[j, mc], send_sems, recv_sems, 6 * a + 3 + k,
                             sibling)
                cp.start()
                sends.append(cp)
        for a in range(n):
            for k, (px, py) in enumerate(chips):
                j = 2 * px + py
                _remote(outs[a].at[j, 1 - mc], outs[a].at[j, 1 - mc], send_sems, recv_sems,
                        6 * a + 3 + k, sibling).wait_recv()
        for cp in sends:
            cp.wait_send()

    shapes = [jax.ShapeDtypeStruct((N_CHIPS,) + s.shape, s.dtype) for s in shards]
    outs = _comm_call(name, body, shards, shapes, 6 * n, 0)
    me = 2 * lax.axis_index("x") + lax.axis_index("y")
    return [lax.dynamic_update_index_in_dim(o, s, me, 0) for o, s in zip(outs, shards)]


def _sibling_swap(gs):
    n = len(gs)

    def body(ins, outs, send_sems, recv_sems, local_sems):
        mx, my, mc = _mesh_pos()
        sends = []
        for a in range(n):
            cp = _remote(ins[a].at[:, 1 - mc], outs[a], send_sems, recv_sems, a, (mx, my, 1 - mc))
            cp.start()
            sends.append(cp)
        for cp in sends:
            cp.wait()

    shapes = [jax.ShapeDtypeStruct((g.shape[0],) + g.shape[2:], g.dtype) for g in gs]
    return _comm_call("grad_sibling_swap", body, gs, shapes, n, 0)


def _chip_exchange(ps):
    n = len(ps)

    def body(ins, outs, send_sems, recv_sems, local_sems):
        mx, my, mc = _mesh_pos()
        me = 2 * mx + my
        chips = _other_chips(mx, my)
        sends = []
        for a in range(n):
            for k, (px, py) in enumerate(chips):
                cp = _remote(ins[a].at[2 * px + py], outs[a].at[me], send_sems, recv_sems, 3 * a + k,
                             (px, py, mc))
                cp.start()
                sends.append(cp)
        for a in range(n):
            for k, (px, py) in enumerate(chips):
                _remote(ins[a].at[me], outs[a].at[2 * px + py], send_sems, recv_sems, 3 * a + k,
                        (px, py, mc)).wait_recv()
        for cp in sends:
            cp.wait_send()

    shapes = [jax.ShapeDtypeStruct(p.shape, p.dtype) for p in ps]
    outs = _comm_call("grad_chip_exchange", body, ps, shapes, 3 * n, 0)
    me = 2 * lax.axis_index("x") + lax.axis_index("y")
    return [lax.dynamic_update_index_in_dim(o, lax.dynamic_index_in_dim(p, me, 0, keepdims=False), me, 0)
            for o, p in zip(outs, ps)]


def _sibling_gather(hs):
    n = len(hs)

    def body(ins, outs, send_sems, recv_sems, local_sems):
        mx, my, mc = _mesh_pos()
        sends = []
        for a in range(n):
            cp = _remote(ins[a], outs[a], send_sems, recv_sems, a, (mx, my, 1 - mc))
            cp.start()
            sends.append(cp)
        for cp in sends:
            cp.wait()

    shapes = [jax.ShapeDtypeStruct(h.shape, h.dtype) for h in hs]
    theirs = _comm_call("grad_sibling_gather", body, hs, shapes, n, 0)
    first = lax.axis_index("c") == 0
    return [jnp.stack([jnp.where(first, h, t), jnp.where(first, t, h)]) for h, t in zip(hs, theirs)]


def _add_pairs(a, b, tr, out_dtype):
    n, H, C = a.shape

    def kern(a_ref, b_ref, o_ref):
        o_ref[...] = (a_ref[...] + b_ref[...]).astype(o_ref.dtype)

    spec = pl.BlockSpec((None, tr, C), lambda j, i: (j, i, 0))
    return pl.pallas_call(
        kern, name="grad_pair_sum", grid=(n, H // tr), in_specs=[spec, spec], out_specs=spec,
        out_shape=jax.ShapeDtypeStruct((n, H, C), out_dtype),
        compiler_params=_params(("parallel", "parallel")),
    )(a, b)


def _sum_slots(r4, tr):
    n, H, C = r4.shape

    def kern(r_ref, o_ref):
        f = lambda k: r_ref[k].astype(F32)
        o_ref[...] = ((f(0) + f(1)) + f(2)) + f(3)

    return pl.pallas_call(
        kern, name="grad_chip_sum", grid=(H // tr,),
        in_specs=[pl.BlockSpec((n, tr, C), lambda i: (0, i, 0))],
        out_specs=pl.BlockSpec((tr, C), lambda i: (i, 0)),
        out_shape=jax.ShapeDtypeStruct((H, C), F32),
        compiler_params=_params(("parallel",)),
    )(r4)


def _adamw(w, g, m, v, tr):
    R, C = w.shape
    c1 = 1.0 - ADAM_B1
    c2 = 1.0 - ADAM_B2
    bc1 = 1.0 - ADAM_B1 ** ADAM_STEP
    bc2 = 1.0 - ADAM_B2 ** ADAM_STEP

    def kern(w_ref, g_ref, m_ref, v_ref, d_ref, mo_ref, vo_ref):
        gv = g_ref[...]
        mn = ADAM_B1 * m_ref[...] + c1 * gv
        vn = ADAM_B2 * v_ref[...] + c2 * (gv * gv)
        m_hat = mn / bc1
        v_hat = vn / bc2
        d_ref[...] = -ADAM_LR * (m_hat / (jnp.sqrt(v_hat) + ADAM_EPS) + ADAM_WD * w_ref[...])
        mo_ref[...] = mn
        vo_ref[...] = vn

    spec = pl.BlockSpec((tr, C), lambda i: (i, 0))
    shape = jax.ShapeDtypeStruct((R, C), F32)
    return pl.pallas_call(
        kern, name="adamw", grid=(R // tr,), in_specs=[spec] * 4, out_specs=[spec] * 3,
        out_shape=[shape] * 3, compiler_params=_params(("parallel",)),
    )(w, g, m, v)


PACK_COLS = 1024
PACK_ROWS = 512
W_SHARD = N_IN // N_CHIPS
SLAB = ("conv_w", "w_mem_kv", "w_branch", "w_out")
SMALL =("norm_g", "b_fg", "b_merge", "a_log", "dt_bias", "gdn_norm_g", "mem_norm_g", "final_norm_g")
ALL_WEIGHTS = ("norm_g", "w_in", "b_fg", "b_merge", "conv_w", "a_log", "dt_bias", "gdn_norm_g",
               "mem_norm_g", "w_mem_kv", "w_branch", "w_out", "final_norm_g")
SHARD_AXIS = {"w_in": 2, "conv_w": 2, "w_mem_kv": 1, "w_branch": 3, "w_out": 1}


def _pack(arrays, row_multiple):
    flat = jnp.concatenate([a.reshape(-1) for a in arrays])
    n = flat.shape[0]
    rows = -(-n // PACK_COLS)
    rows = -(-rows // row_multiple) * row_multiple
    flat = jnp.pad(flat, (0, rows * PACK_COLS - n))
    return flat.reshape(rows, PACK_COLS)


def _unpack(slab, shapes):
    out, off = [], 0
    for shp in shapes:
        n = 1
        for d in shp:
            n *= d
        r0, r1 = off // PACK_COLS, -(-(off + n) // PACK_COLS)
        rows = slab[r0:r1].reshape(-1)
        out.append(rows[off - r0 * PACK_COLS:off - r0 * PACK_COLS + n].reshape(shp))
        off += n
    return out


def _shard_of(full, name, j):
    ax = SHARD_AXIS[name]
    n = full.shape[ax] // N_CHIPS
    return lax.slice_in_dim(full, j * n, (j + 1) * n, axis=ax)


def _aligned_from_shards(shards):
    def cols(lo, n):
        parts = []
        while n > 0:
            j, off = divmod(lo, W_SHARD)
            take = min(n, W_SHARD - off)
            parts.append(shards[j][..., off:off + take])
            lo, n = lo + take, n - take
        return parts

    out = []
    for lo, n, _ in sorted(W_IN_PIECES, key=lambda p: p[2]):
        out += cols(lo, n)
    out.append(jnp.zeros(shards[0].shape[:-1] + (N_AL - N_IN,), shards[0].dtype))
    return jnp.concatenate(out, axis=-1)


def _shard_from_aligned(w_al, j):
    lo_j, hi_j = j * W_SHARD, (j + 1) * W_SHARD
    parts = []
    for lo, n, al in W_IN_PIECES:
        a, b = max(lo, lo_j), min(lo + n, hi_j)
        if a < b:
            parts.append(w_al[..., al + a - lo:al + b - lo])
    return jnp.concatenate(parts, axis=-1)


def kernel(x, mem, norm_g, w_in, b_fg, b_merge, conv_w, a_log, dt_bias, gdn_norm_g, mem_norm_g, w_mem_kv, w_branch, w_out, final_norm_g, loss_target, m_norm_g, m_w_in, m_b_fg, m_b_merge, m_conv_w, m_a_log, m_dt_bias, m_gdn_norm_g, m_mem_norm_g, m_w_mem_kv, m_w_branch, m_w_out, m_final_norm_g, v_norm_g, v_w_in, v_b_fg, v_b_merge, v_conv_w, v_a_log, v_dt_bias, v_gdn_norm_g, v_mem_norm_g, v_w_mem_kv, v_w_branch, v_w_out, v_final_norm_g):
    wts = dict(norm_g=norm_g, w_in=w_in, b_fg=b_fg, b_merge=b_merge, conv_w=conv_w, a_log=a_log,
               dt_bias=dt_bias, gdn_norm_g=gdn_norm_g, mem_norm_g=mem_norm_g, w_mem_kv=w_mem_kv,
               w_branch=w_branch, w_out=w_out, final_norm_g=final_norm_g)
    mom = dict(norm_g=m_norm_g, w_in=m_w_in, b_fg=m_b_fg, b_merge=m_b_merge, conv_w=m_conv_w,
               a_log=m_a_log, dt_bias=m_dt_bias, gdn_norm_g=m_gdn_norm_g, mem_norm_g=m_mem_norm_g,
               w_mem_kv=m_w_mem_kv, w_branch=m_w_branch, w_out=m_w_out, final_norm_g=m_final_norm_g)
    vel = dict(norm_g=v_norm_g, w_in=v_w_in, b_fg=v_b_fg, b_merge=v_b_merge, conv_w=v_conv_w,
               a_log=v_a_log, dt_bias=v_dt_bias, gdn_norm_g=v_gdn_norm_g, mem_norm_g=v_mem_norm_g,
               w_mem_kv=v_w_mem_kv, w_branch=v_w_branch, w_out=v_w_out, final_norm_g=v_final_norm_g)

    big = ("w_in", "w_mem_kv", "w_branch", "w_out")
    gathered = _gather_chips("weight_gather", [wts[n].astype(BF16) for n in big] + [conv_w])
    all_w = dict(zip(big + ("conv_w",), gathered))
    w_in_al = _aligned_from_shards([all_w["w_in"][j] for j in range(N_CHIPS)])

    layers = []
    for l in range(DEPTH):
        rows_of = lambda n: all_w[n][:, l].reshape(D_MODEL, D_MODEL)
        last_of = lambda n: jnp.concatenate([all_w[n][j, l] for j in range(N_CHIPS)], axis=-1)
        layers.append(dict(norm_g=norm_g[l], w_in_al=w_in_al[l], b_fg=b_fg[l], b_merge=b_merge[l],
                           conv_w=jnp.pad(last_of("conv_w"), ((0, 4), (0, 0))), a_log=a_log[l],
                           dt_bias=dt_bias[l], gdn_norm_g=gdn_norm_g[l], mem_norm_g=mem_norm_g[l],
                           w_mem_kv=rows_of("w_mem_kv"), w_branch=last_of("w_branch"),
                           w_out=rows_of("w_out")))

    loss_lanes, dx, grads, dgf = _local_step(x[0], mem[0], layers, final_norm_g, loss_target[0])

    gfull = {n: jnp.stack([grads[l][n] for l in range(DEPTH)])
             for n in ("norm_g", "b_fg", "b_merge", "conv_w", "a_log", "dt_bias", "gdn_norm_g",
                       "mem_norm_g", "w_mem_kv", "w_branch", "w_out")}
    gfull["final_norm_g"] = dgf
    loss_local = jnp.sum(loss_lanes).reshape(1)
    small_g = [gfull[n] for n in SMALL] + [loss_local]
    dw_al = jnp.stack([grads[l]["w_in_al"] for l in range(DEPTH)])
    ga = jnp.stack([_shard_from_aligned(dw_al, j) for j in range(N_CHIPS)])
    mats = ("w_mem_kv", "w_branch", "w_out")
    rest = ("conv_w",) + SMALL
    gb = jnp.stack([_pack([_shard_of(gfull[n], n, j) for n in mats], PACK_ROWS)
                    for j in range(N_CHIPS)])
    gc = jnp.stack([_pack([_shard_of(gfull["conv_w"], "conv_w", j)] + small_g, 16)
                    for j in range(N_CHIPS)])
    halves = lambda g: g.reshape(N_CHIPS, 2, g.shape[1] // 2, PACK_COLS)
    gb, gc = halves(gb), halves(gc)

    mc = lax.axis_index("c")
    tr = 256
    trs = (tr, tr, 8)
    from_sibling = _sibling_swap([ga, gb, gc])
    mine = [lax.dynamic_index_in_dim(g, mc, axis=1, keepdims=False) for g in (ga, gb, gc)]
    pair = [_add_pairs(a, b, t, dt) for a, b, t, dt in zip(mine, from_sibling, trs, (BF16, BF16, F32))]
    slots = _chip_exchange(pair)
    half = [_sum_slots(s, t) for s, t in zip(slots, trs)]
    ga_sum, gb_sum, gc_sum = _sibling_gather(half)
    flat = lambda g: g.reshape(-1, PACK_COLS)

    g_un = dict(zip(mats, _unpack(flat(gb_sum), [wts[n].shape for n in mats])))
    g_un.update(zip(rest + ("loss",), _unpack(flat(gc_sum), [wts[n].shape for n in rest] + [(1,)])))
    g_un["w_in"] = ga_sum
    d_un, m_un, v_un = {}, {}, {}
    rows2d = lambda a: a.reshape(-1, a.shape[-1])
    for n in ("w_in", "w_mem_kv", "w_branch", "w_out"):
        res = _adamw(rows2d(wts[n]), rows2d(g_un[n]), rows2d(mom[n]), rows2d(vel[n]), tr)
        d_un[n], m_un[n], v_un[n] = [r.reshape(wts[n].shape) for r in res]
    little = ("conv_w",) + SMALL
    slab = lambda d: _pack([d[n] for n in little], 8)
    res = _adamw(slab(wts), slab(g_un), slab(mom), slab(vel), 8)
    little_shapes = [wts[n].shape for n in little]
    for out, r in zip((d_un, m_un, v_un), res):
        out.update(zip(little, _unpack(r, little_shapes)))

    loss = g_un["loss"][0]
    return (loss, dx[None], *[g_un[n] for n in ALL_WEIGHTS], *[d_un[n] for n in ALL_WEIGHTS],
            *[m_un[n] for n in ALL_WEIGHTS], *[v_un[n] for n in ALL_WEIGHTS])
```

```python
import functools

import jax
import jax.numpy as jnp
from jax import lax
from jax.experimental import pallas as pl
from jax.experimental.pallas import tpu as pltpu

F32 = jnp.float32
BF16 = jnp.bfloat16
HIGHEST = lax.Precision.HIGHEST
PREC_UT = lax.Precision.HIGH
MESH_ID = pl.DeviceIdType.MESH

D_MODEL = 1024
DEPTH = 2
CHUNK = 64
EPS = 1e-6
FOX_HEADS, FOX_DIM = 8, 64
GDN_HEADS, GDN_DIM = 4, 128
MEM_HEADS, MEM_DIM = 4, 128
WIDTH = 512
N_BRANCH = 3
N_IN = 8208
N_AL = 8320
N_CHIPS = 4
NEG = -1e30
LOG2E = 1.4426950408889634
LN2 = 0.6931471805599453

ADAM_LR, ADAM_B1, ADAM_B2, ADAM_EPS, ADAM_WD, ADAM_STEP = 0.001, 0.9, 0.999, 1e-08, 0.01, 10

CB_GATES = 0
CB_AZ, CB_BZ, CB_MZ = 6, 7, 8
CB_MERGE = 0
CB_BQKV = 3
CB_AQ, CB_AK, CB_AV = 12, 13, 14
CB_AQKV = 4
CB_MQ = 15
CB_SMALL = 64
W_IN_PIECES = ((0, 512, 6144), (512, 512, 6656), (1024, 512, 7168), (1536, 8, 8192), (1544, 512, 3072),
               (2056, 512, 4608), (2568, 512, 5120), (3080, 512, 5632), (3592, 4, 8200), (3596, 4, 8204),
               (3600, 512, 3584), (4112, 512, 7680), (4624, 512, 4096), (5136, 3072, 0))
LANE_AF, LANE_BA, LANE_BB = 0, 8, 12

NN = ((1,), (0,))
NT = ((1,), (1,))
TN = ((0,), (0,))

VMEM_LIMIT_BYTES = 56 * 1024 * 1024


def _dot(a, b, dims=NN, prec=None):
    return lax.dot_general(a, b, (dims, ((), ())), preferred_element_type=F32, precision=prec)


def _bdot(a, b, ca, cb, prec=None):
    return lax.dot_general(a, b, (((ca,), (cb,)), ((0,), (0,))), preferred_element_type=F32,
                           precision=prec)


def _b16(a):
    return a.astype(BF16)


def _eye(n, dtype=F32):
    r = lax.broadcasted_iota(jnp.int32, (n, n), 0)
    c = lax.broadcasted_iota(jnp.int32, (n, n), 1)
    return jnp.where(r == c, 1.0, 0.0).astype(dtype)


def _transpose_exact(x):
    return _dot(_eye(x.shape[1]), x, NT, HIGHEST)


def _col_to_row(col):
    n = col.shape[0]
    return jnp.sum(jnp.where(_eye(n) > 0.5, col, 0.0), axis=0, keepdims=True)


def _row_to_col(row):
    n = row.shape[1]
    return jnp.sum(jnp.where(_eye(n) > 0.5, row, 0.0), axis=1, keepdims=True)


def _sigmoid(x):
    return 1.0 / (1.0 + jnp.exp(-x))


def _softplus(x):
    return jnp.maximum(x, 0.0) + jnp.log(1.0 + jnp.exp(-jnp.abs(x)))


def _silu_and_grad(x):
    s = _sigmoid(x)
    return x * s, s * (1.0 + x * (1.0 - s))


def _params(semantics):
    return pltpu.CompilerParams(dimension_semantics=semantics, vmem_limit_bytes=VMEM_LIMIT_BYTES)


def _rows(a, ts):
    nd = a.ndim
    return (a, (ts,) + a.shape[1:], lambda i, nd=nd: (i,) + (0,) * (nd - 1))


def _cols(a, ts, width, cb):
    return (a, (ts, width), lambda i, cb=cb: (i, cb))


def _full(a):
    nd = a.ndim
    return (a, a.shape, lambda i, nd=nd: (0,) * nd)


def _orow(S, tail, dtype, ts):
    nd = 1 + len(tail)
    return ((S,) + tuple(tail), dtype, (ts,) + tuple(tail), lambda i, nd=nd: (i,) + (0,) * (nd - 1))


def _oacc(shape, dtype):
    nd = len(shape)
    return (tuple(shape), dtype, tuple(shape), lambda i, nd=nd: (0,) * nd)


def _tiled(name, body, n_steps, ins, outs, scratch=(), reverse=False, fill=None):
    def rev(imap):
        if not reverse:
            return imap
        return lambda i: imap(n_steps - 1 - i)

    in_specs = [pl.BlockSpec(blk, rev(imap)) for (_, blk, imap) in ins]
    out_specs = [pl.BlockSpec(blk, rev(imap)) for (_, _, blk, imap) in outs]
    out_shape = [jax.ShapeDtypeStruct(shape, dt) for (shape, dt, _, _) in outs]
    n_in, n_out = len(ins), len(outs)
    arrays = [a for (a, _, _) in ins]
    aliases = {}
    n_extra = 0
    if fill is not None:
        arrays.append(fill[0])
        in_specs.append(pl.BlockSpec(memory_space=pl.ANY))
        aliases = {n_in: fill[1]}
        n_extra = 1

    def kern(*refs):
        step = pl.program_id(0)
        t = (n_steps - 1 - step) if reverse else step
        lo = n_in + n_extra
        body(t, step == 0, refs[:n_in], refs[lo:lo + n_out], refs[lo + n_out:])

    res = pl.pallas_call(
        kern, name=name, grid=(n_steps,), in_specs=in_specs, out_specs=out_specs,
        out_shape=out_shape, scratch_shapes=list(scratch), input_output_aliases=aliases,
        compiler_params=_params(("arbitrary",)),
    )(*arrays)
    return res


def _pick(n, pref):
    if n <= pref:
        return n
    best = None
    for t in range(128, pref + 1, 128):
        if n % t == 0:
            best = t
    assert best is not None, (n, pref)
    return best


def _mm(name, a, b, ta=False, tb=False, out_dtype=F32, tm=1024, tn=1024, tk=1024):
    if ta:
        K, M = a.shape
    else:
        M, K = a.shape
    if tb:
        N, K2 = b.shape
    else:
        K2, N = b.shape
    assert K == K2, (a.shape, b.shape, ta, tb)
    tm, tn, tk = _pick(M, tm), _pick(N, tn), _pick(K, tk)
    nk = K // tk
    a_spec = (pl.BlockSpec((tk, tm), lambda i, j, k: (k, i)) if ta
              else pl.BlockSpec((tm, tk), lambda i, j, k: (i, k)))
    b_spec = (pl.BlockSpec((tn, tk), lambda i, j, k: (j, k)) if tb
              else pl.BlockSpec((tk, tn), lambda i, j, k: (k, j)))
    dims = ((0,) if ta else (1,), (1,) if tb else (0,))

    def kern_single(a_ref, b_ref, o_ref):
        o_ref[...] = _dot(_b16(a_ref[...]), _b16(b_ref[...]), dims).astype(o_ref.dtype)

    def kern_acc(a_ref, b_ref, o_ref, acc_ref):
        k = pl.program_id(2)

        @pl.when(k == 0)
        def _():
            acc_ref[...] = jnp.zeros_like(acc_ref)

        acc_ref[...] += _dot(_b16(a_ref[...]), _b16(b_ref[...]), dims)

        @pl.when(k == nk - 1)
        def _():
            o_ref[...] = acc_ref[...].astype(o_ref.dtype)

    return pl.pallas_call(
        kern_single if nk == 1 else kern_acc, name=name, grid=(M // tm, N // tn, nk),
        in_specs=[a_spec, b_spec],
        out_specs=pl.BlockSpec((tm, tn), lambda i, j, k: (i, j)),
        out_shape=jax.ShapeDtypeStruct((M, N), out_dtype),
        scratch_shapes=[] if nk == 1 else [pltpu.VMEM((tm, tn), F32)],
        compiler_params=_params(("parallel", "parallel", "arbitrary")),
    )(a, b)


def _rms_fwd(name, x, g, ts):
    S, D = x.shape

    def body(t, first, ins, outs, scratch):
        x_ref, g_ref = ins
        h_ref, r_ref = outs
        xv = x_ref[...]
        r = lax.rsqrt(jnp.mean(xv * xv, axis=1, keepdims=True) + EPS)
        h_ref[...] = (xv * r * g_ref[...]).astype(h_ref.dtype)
        r_ref[...] = r

    return _tiled(name, body, S // ts, [_rows(x, ts), _full(g.reshape(1, D))],
                  [_orow(S, (D,), BF16, ts), _orow(S, (1,), F32, ts)])


def _rms_bwd(name, dh, x, rstd, g, dres, ts):
    S, D = x.shape

    def body(t, first, ins, outs, scratch):
        dh_ref, x_ref, r_ref, g_ref, dres_ref = ins
        dx_ref, dg_ref = outs
        r = r_ref[...]
        xh = x_ref[...] * r
        dhv = dh_ref[...]
        dxh = dhv * g_ref[...]
        dx_ref[...] = dres_ref[...] + r * (dxh - xh * jnp.mean(dxh * xh, axis=1, keepdims=True))

        @pl.when(first)
        def _():
            dg_ref[...] = jnp.zeros_like(dg_ref)

        dg_ref[0:1, :] += jnp.sum(dhv * xh, axis=0, keepdims=True)

    dx, dg = _tiled(name, body, S // ts,
                    [_rows(dh, ts), _rows(x, ts), _rows(rstd, ts), _full(g.reshape(1, D)), _rows(dres, ts)],
                    [_orow(S, (D,), F32, ts), _oacc((8, D), F32)])
    return dx, dg[0]


def _loss_head(x, g, target, ts):
    S, D = x.shape

    def body(t, first, ins, outs, scratch):
        x_ref, g_ref, tgt_ref = ins
        dx_ref, dg_ref, loss_ref = outs
        xv = x_ref[...]
        gv = g_ref[...]
        r = lax.rsqrt(jnp.mean(xv * xv, axis=1, keepdims=True) + EPS)
        xh = xv * r
        err = xh * gv - tgt_ref[...]
        dy = err * (1.0 / D)
        dxh = dy * gv
        dx_ref[...] = r * (dxh - xh * jnp.mean(dxh * xh, axis=1, keepdims=True))

        @pl.when(first)
        def _():
            dg_ref[...] = jnp.zeros_like(dg_ref)
            loss_ref[...] = jnp.zeros_like(loss_ref)

        dg_ref[0:1, :] += jnp.sum(dy * xh, axis=0, keepdims=True)
        per_lane = jnp.sum(err * err, axis=0, keepdims=True)
        loss_ref[0:1, :] += per_lane * (0.5 / D)

    dx, dg, loss = _tiled("loss_head", body, S // ts,
                          [_rows(x, ts), _full(g.reshape(1, D)), _rows(target, ts)],
                          [_orow(S, (D,), F32, ts), _oacc((8, D), F32), _oacc((8, D), F32)])
    return dx, dg[0], loss[0]


def _scan_rows(x, length, seg, reverse=False):
    row = lax.broadcasted_iota(jnp.int32, x.shape, 0) % seg
    k = 1
    while k < seg:
        if reverse:
            x = x + jnp.where(row < seg - k, pltpu.roll(x, length - k, 0), 0.0)
        else:
            x = x + jnp.where(row >= k, pltpu.roll(x, k, 0), 0.0)
        k *= 2
    return x


def _fox_decay(z, b_fg128, ts):
    S = z.shape[0]

    def body(t, first, ins, outs, scratch):
        zs_ref, b_ref = ins
        hi_ref, mid_ref, lo_ref = outs
        (carry,) = scratch

        @pl.when(first)
        def _():
            carry[...] = jnp.zeros_like(carry)

        logf = -_softplus(-(zs_ref[...] + b_ref[...]))
        run = _scan_rows(logf, ts, ts) + carry[0:1, :]
        carry[0:1, :] = run[ts - 1:ts, :]
        f2 = run * LOG2E
        hi = f2.astype(BF16)
        r1 = f2 - hi.astype(F32)
        mid = r1.astype(BF16)
        lo = (r1 - mid.astype(F32)).astype(BF16)
        eye = _eye(128, BF16)
        hi_ref[...] = _dot(eye, hi, NT).astype(BF16)
        mid_ref[...] = _dot(eye, mid, NT).astype(BF16)
        lo_ref[...] = _dot(eye, lo, NT).astype(BF16)

    tcol = lambda dt: ((128, S), dt, (128, ts), lambda i: (0, i))
    return _tiled("fox_decay", body, S // ts,
                  [_cols(z, ts, 128, CB_SMALL), _full(b_fg128)],
                  [tcol(BF16), tcol(BF16), tcol(BF16)], scratch=[pltpu.VMEM((8, 128), F32)])


def _fox_decay_bwd(dfk_rows, dfq_rows, z, b_fg128, ts):
    S = z.shape[0]
    H = dfk_rows.shape[0]

    def body(t, first, ins, outs, scratch):
        dfk_ref, dfq_ref, zs_ref, b_ref = ins
        daf_ref, db_ref = outs
        (carry,) = scratch

        @pl.when(first)
        def _():
            carry[...] = jnp.zeros_like(carry)
            db_ref[...] = jnp.zeros_like(db_ref)

        r = lax.broadcasted_iota(jnp.int32, (H, 128), 0)
        c = lax.broadcasted_iota(jnp.int32, (H, 128), 1)
        place = jnp.where(r == c, 1.0, 0.0)
        df = _dot(dfk_ref[...] + dfq_ref[...], place, TN, HIGHEST)
        run = _scan_rows(df, ts, ts, reverse=True) + carry[0:1, :]
        carry[0:1, :] = run[0:1, :]
        daf = run * _sigmoid(-(zs_ref[...] + b_ref[...]))
        daf_ref[...] = daf
        db_ref[0:1, :] += jnp.sum(daf, axis=0, keepdims=True)

    rowsin = lambda a: (a, (H, ts), lambda i: (0, i))
    daf, db = _tiled("fox_decay_bwd", body, S // ts,
                     [rowsin(dfk_rows), rowsin(dfq_rows), _cols(z, ts, 128, CB_SMALL), _full(b_fg128)],
                     [_orow(S, (128,), F32, ts), _oacc((8, 128), F32)],
                     scratch=[pltpu.VMEM((8, 128), F32)], reverse=True)
    return daf, db[0]


FOX_AUG = 80


def _fox_fwd(q_aug, kT_aug, v_aug, tq):
    H, S, da = q_aug.shape
    dv = v_aug.shape[2]
    d = FOX_DIM
    tk = tq // 2
    qscale = (d ** -0.5) * LOG2E

    def kern(q_ref, kT_ref, v_ref, o_ref, lse_ref, qs_ref, s_buf, p_buf, m_scr, acc_scr):
        i = pl.program_id(1)
        col = lax.broadcasted_iota(jnp.int32, (1, da), 1)
        qb = _b16(q_ref[...] * jnp.where(col < d, qscale, 1.0))
        qs_ref[...] = qb

        def keys(t):
            return pl.ds(pl.multiple_of(t * tk, tk), tk)

        def stage(t, slot, mask_off, look_ahead):
            if look_ahead:
                s_buf[1 - slot] = _dot(qb, kT_ref[:, keys(t + 1)])
            pv = _dot(p_buf[1 - slot], v_ref[keys(jnp.maximum(t - 1, 0)), :])

            def scores():
                s = s_buf[slot]
                if mask_off is None:
                    return s
                r = lax.broadcasted_iota(jnp.int32, (tq, tk), 0)
                c = lax.broadcasted_iota(jnp.int32, (tq, tk), 1)
                return jnp.where(c + mask_off <= r, s, NEG)

            m = m_scr[...]
            m_new = jnp.maximum(m, jnp.max(scores(), axis=1, keepdims=True))
            alpha = jnp.exp2(m - m_new)
            p_buf[slot] = _b16(jnp.exp2(scores() - m_new))
            m_scr[...] = m_new
            acc_scr[...] = (acc_scr[...] + pv) * alpha

        s_buf[0] = _dot(qb, kT_ref[:, keys(0)])
        p_buf[1] = jnp.zeros((tq, tk), BF16)
        m_scr[...] = jnp.full((tq, 1), NEG, F32)
        acc_scr[...] = jnp.zeros((tq, dv), F32)

        def pair(n):
            stage(2 * n, 0, None, True)
            stage(2 * n + 1, 1, None, True)

        def quad(m, _):
            pair(2 * m)
            pair(2 * m + 1)
            return 0

        lax.fori_loop(0, i // 2, quad, 0)

        @pl.when(i % 2 == 1)
        def _():
            pair(i - 1)

        stage(2 * i, 0, 0, True)
        stage(2 * i + 1, 1, tk, False)
        acc = acc_scr[...] + _dot(p_buf[1], v_ref[keys(2 * i + 1), :])
        l = acc[:, d:d + 1]
        o_ref[...] = acc[:, :d] / l
        lse_ref[...] = _col_to_row(m_scr[...] + jnp.log(l) * LOG2E)

    return pl.pallas_call(
        kern, name="fox_fwd", grid=(H, S // tq),
        in_specs=[pl.BlockSpec((None, tq, da), lambda h, i: (h, i, 0)),
                  pl.BlockSpec((None, da, S), lambda h, i: (h, 0, 0)),
                  pl.BlockSpec((None, S, dv), lambda h, i: (h, 0, 0))],
        out_specs=[pl.BlockSpec((None, tq, d), lambda h, i: (h, i, 0)),
                   pl.BlockSpec((None, 1, tq), lambda h, i: (h, 0, i)),
                   pl.BlockSpec((None, tq, da), lambda h, i: (h, i, 0))],
        out_shape=[jax.ShapeDtypeStruct((H, S, d), F32), jax.ShapeDtypeStruct((H, 1, S), F32),
                   jax.ShapeDtypeStruct((H, S, da), BF16)],
        scratch_shapes=[pltpu.VMEM((2, tq, tk), F32), pltpu.VMEM((2, tq, tk), BF16),
                        pltpu.VMEM((tq, 1), F32), pltpu.VMEM((tq, dv), F32)],
        compiler_params=_params(("parallel", "arbitrary")),
    )(q_aug, kT_aug, v_aug)


def _fox_bwd(qs, k_aug, kT, v, do, lse_row, delta_row, tq):
    H, S, da = qs.shape
    d = FOX_DIM
    tk = tq
    nq = S // tq
    scale = d ** -0.5

    ts2 = tq // 2
    last = 2 * nq - 1

    def kern(q_ref, k_ref, kT_ref, v_ref, do_ref, lse_ref, dl_ref,
             dqT_ref, dk_ref, dv_ref, dfk_ref, dfq_ref,
             kq_buf, dp_buf, pb_buf, ds_buf, dk_scr, dv_scr, dfk_scr):
        j = pl.program_id(1)

        @pl.when(j == 0)
        def _():
            dqT_ref[...] = jnp.zeros_like(dqT_ref)
            dfq_ref[...] = jnp.zeros_like(dfq_ref)

        kb = k_ref[...]
        kTb = kT_ref[...]
        vb = v_ref[:, :d]
        dk_scr[...] = jnp.zeros_like(dk_scr)
        dv_scr[...] = jnp.zeros_like(dv_scr)
        dfk_scr[...] = jnp.zeros_like(dfk_scr)

        def queries(t):
            return pl.ds(pl.multiple_of(t * ts2, ts2), ts2)

        def products(t, slot):
            rows = queries(t)
            kq_buf[slot] = _dot(kb, q_ref[rows, :], NT)
            dp_buf[slot] = _dot(vb, do_ref[rows, :], NT)

        def pointwise(t, slot, mask_off):
            rows = queries(t)
            sT = kq_buf[slot]
            if mask_off is not None:
                r = lax.broadcasted_iota(jnp.int32, (tk, ts2), 0)
                c = lax.broadcasted_iota(jnp.int32, (tk, ts2), 1)
                sT = jnp.where(r <= c + mask_off, sT, NEG)
            pT = jnp.exp2(sT - lse_ref[:, rows])
            dsT = pT * (dp_buf[slot] - dl_ref[:, rows])
            pb_buf[slot] = _b16(pT)
            ds_buf[slot] = _b16(dsT)
            dfk_scr[...] -= jnp.sum(dsT, axis=1, keepdims=True)
            dfq_ref[:, rows] += jnp.sum(dsT, axis=0, keepdims=True)

        def accumulate(t, slot):
            rows = queries(t)
            dsb = ds_buf[slot]
            dv_scr[...] += _dot(pb_buf[slot], do_ref[rows, :])
            dk_scr[...] += _dot(dsb, q_ref[rows, :])
            dqT_ref[:, rows] += _dot(kTb, dsb) * scale

        def stage(t, slot, mask_off, has_prev):
            products(jnp.minimum(t + 1, last), 1 - slot)
            if has_prev:
                accumulate(t - 1, 1 - slot)
            pointwise(t, slot, mask_off)

        products(2 * j, 0)
        stage(2 * j, 0, 0, False)
        stage(2 * j + 1, 1, ts2, True)

        def pair(n):
            stage(2 * n, 0, None, True)
            stage(2 * n + 1, 1, None, True)

        def quad(m, _):
            pair(j + 1 + 2 * m)
            pair(j + 2 + 2 * m)
            return 0

        n_rest = nq - 1 - j
        lax.fori_loop(0, n_rest // 2, quad, 0)

        @pl.when(n_rest % 2 == 1)
        def _():
            pair(nq - 1)

        accumulate(last, 1)
        dk_ref[...] = dk_scr[:, :d] * LN2
        dv_ref[...] = dv_scr[...]
        dfk_ref[...] = _col_to_row(dfk_scr[...])

    tile = lambda h, j: (h, j, 0)
    whole = lambda h, j: (h, 0, 0)
    rowtile = lambda h, j: (h, 0, j)
    return pl.pallas_call(
        kern, name="fox_bwd", grid=(H, S // tk),
        in_specs=[pl.BlockSpec((None, S, da), whole),
                  pl.BlockSpec((None, tk, da), tile),
                  pl.BlockSpec((None, d, tk), lambda h, j: (h, 0, j)),
                  pl.BlockSpec((None, tk, 128), tile),
                  pl.BlockSpec((None, S, d), whole),
                  pl.BlockSpec((None, 1, S), whole),
                  pl.BlockSpec((None, 1, S), whole)],
        out_specs=[pl.BlockSpec((None, d, S), whole),
                   pl.BlockSpec((None, tk, d), tile),
                   pl.BlockSpec((None, tk, d), tile),
                   pl.BlockSpec((None, 1, tk), rowtile),
                   pl.BlockSpec((None, 1, S), whole)],
        out_shape=[jax.ShapeDtypeStruct((H, d, S), F32), jax.ShapeDtypeStruct((H, S, d), F32),
                   jax.ShapeDtypeStruct((H, S, d), F32), jax.ShapeDtypeStruct((H, 1, S), F32),
                   jax.ShapeDtypeStruct((H, 1, S), F32)],
        scratch_shapes=[pltpu.VMEM((2, tk, ts2), F32), pltpu.VMEM((2, tk, ts2), F32),
                        pltpu.VMEM((2, tk, ts2), BF16), pltpu.VMEM((2, tk, ts2), BF16),
                        pltpu.VMEM((tk, da), F32), pltpu.VMEM((tk, d), F32), pltpu.VMEM((tk, 1), F32)],
        compiler_params=_params(("parallel", "arbitrary")),
    )(qs, k_aug, kT, v, do, lse_row, delta_row)


def _heads_major(a, H, d):
    S = a.shape[0]
    return a.reshape(S, H, d).transpose(1, 0, 2)


def _heads_minor(a):
    H, S, d = a.shape
    return a.transpose(1, 0, 2).reshape(S, H * d)


def _lane_pick(x128, lane):
    return x128[:, lane:lane + 1]


def _l2_fwd(y):
    return lax.rsqrt(jnp.sum(y * y, axis=1, keepdims=True) + EPS)


def _gdn_prep(z, conv_w, a128, dt128, ts):
    S = z.shape[0]
    C3 = 3 * WIDTH
    hb = ts // 8

    def body(t, first, ins, outs, scratch):
        x_ref, halo_ref, zs_ref, w_ref, a_ref, dt_ref = ins
        qkv_ref, c_ref, gb_ref, gbT_ref = outs
        halo = jnp.where(t > 0, halo_ref[...], 0.0)
        xe = jnp.concatenate([halo, x_ref[...]], axis=0)
        w = w_ref[...]
        c = w[3:4, :] * xe[8:, :]
        for back in (1, 2, 3):
            c = c + w[3 - back:4 - back, :] * pltpu.roll(xe, back, 0)[8:, :]
        c_ref[...] = c
        y = c * _sigmoid(c)
        for h in range(GDN_HEADS):
            lo = h * GDN_DIM
            yq = y[:, lo:lo + GDN_DIM]
            qkv_ref[:, lo:lo + GDN_DIM] = yq * (_l2_fwd(yq) * (GDN_DIM ** -0.5))
            yk = y[:, WIDTH + lo:WIDTH + lo + GDN_DIM]
            qkv_ref[:, WIDTH + lo:WIDTH + lo + GDN_DIM] = yk * _l2_fwd(yk)
        qkv_ref[:, 2 * WIDTH:] = y[:, 2 * WIDTH:]
        zs = zs_ref[...]
        lane = lax.broadcasted_iota(jnp.int32, zs.shape, 1)
        g = -jnp.exp(a_ref[...]) * _softplus(zs + dt_ref[...])
        G = _scan_rows(g, ts, CHUNK)
        beta = _sigmoid(zs)
        out = jnp.where(lane < 8, pltpu.roll(g, 128 - LANE_BA, 1), jnp.where(lane < LANE_BB, G, beta))
        gb_ref[...] = out
        gbT_ref[...] = _transpose_exact(out)

    x_in = (z, (ts, C3), lambda i: (i, CB_BQKV))
    halo_in = (z, (8, C3), lambda i: (jnp.maximum(i * hb - 1, 0), CB_BQKV))
    return _tiled("gdn_prep", body, S // ts,
                  [x_in, halo_in, _cols(z, ts, 128, CB_SMALL), _full(conv_w), _full(a128), _full(dt128)],
                  [_orow(S, (C3,), F32, ts), _orow(S, (C3,), F32, ts), _orow(S, (128,), F32, ts),
                   ((128, S), F32, (128, ts), lambda i: (0, i))])


def _chunk_masks(nc):
    r = lax.broadcasted_iota(jnp.int32, (nc, CHUNK, CHUNK), 1)
    c = lax.broadcasted_iota(jnp.int32, (nc, CHUNK, CHUNK), 2)
    return c <= r, c < r, c == r


def _chunk_local(qh, kh, vh, Gc, Gr, beta):
    nc = qh.shape[0]
    incl, strict, _ = _chunk_masks(nc)
    gamma = jnp.exp(jnp.where(incl, Gc - Gr, NEG))
    kb = kh * beta
    P = _bdot(_b16(kb), _b16(kh), 2, 2)
    Qk = _bdot(_b16(qh), _b16(kh), 2, 2)
    eG = jnp.exp(Gc)
    Gl = Gc[:, CHUNK - 1:CHUNK, :]
    edec = jnp.exp(Gl - Gc)
    return incl, strict, gamma, kb, P, Qk, eG, edec


def _gdn_local_fwd(qkv, gb, grow, ts):
    S = qkv.shape[0]
    nc = ts // CHUNK

    def body(t, first, ins, outs, scratch):
        q_ref, k_ref, v_ref, gb_ref, gr_ref = ins
        u_ref, w_ref, qd_ref, kd_ref, aqk_ref, T_ref = outs
        gbv = gb_ref[...]
        heads = range(GDN_HEADS)
        _, _, eye = _chunk_masks(nc)
        T, X, rhs_u, rhs_w = [], [], [], []
        for h in heads:
            lo = h * GDN_DIM
            qh = q_ref[:, lo:lo + GDN_DIM].reshape(nc, CHUNK, GDN_DIM)
            kh = k_ref[:, lo:lo + GDN_DIM].reshape(nc, CHUNK, GDN_DIM)
            vh = v_ref[:, lo:lo + GDN_DIM].reshape(nc, CHUNK, GDN_DIM)
            Gc = _lane_pick(gbv, LANE_BA + h).reshape(nc, CHUNK, 1)
            beta = _lane_pick(gbv, LANE_BB + h).reshape(nc, CHUNK, 1)
            Gr = gr_ref[h].reshape(nc, 1, CHUNK)
            incl, strict, gamma, kb, P, Qk, eG, edec = _chunk_local(qh, kh, vh, Gc, Gr, beta)
            A = jnp.where(strict, P * gamma, 0.0)
            T.append(jnp.where(eye, 1.0, 0.0) - A)
            X.append(A)
            rhs_u.append(vh * beta)
            rhs_w.append(kb * eG)
            qd_ref[:, lo:lo + GDN_DIM] = (qh * eG).reshape(ts, GDN_DIM)
            kd_ref[:, lo:lo + GDN_DIM] = (kh * edec).reshape(ts, GDN_DIM)
            aqk_ref[h] = jnp.where(incl, Qk * gamma, 0.0).reshape(ts, CHUNK)
        for _ in range(5):
            X = [_bdot(X[h], X[h], 2, 1, PREC_UT) for h in heads]
            T = [T[h] + _bdot(T[h], X[h], 2, 1, PREC_UT) for h in heads]
        u = [_bdot(T[h], rhs_u[h], 2, 1, PREC_UT) for h in heads]
        w = [_bdot(T[h], rhs_w[h], 2, 1, PREC_UT) for h in heads]
        for h in heads:
            lo = h * GDN_DIM
            u_ref[:, lo:lo + GDN_DIM] = u[h].reshape(ts, GDN_DIM)
            w_ref[:, lo:lo + GDN_DIM] = w[h].reshape(ts, GDN_DIM)
            T_ref[h] = T[h].reshape(ts, CHUNK)

    wide = _orow(S, (WIDTH,), F32, ts)
    perhead = ((GDN_HEADS, S, CHUNK), F32, (GDN_HEADS, ts, CHUNK), lambda i: (0, i, 0))
    return _tiled("gdn_local_fwd", body, S // ts,
                  [_cols(qkv, ts, WIDTH, 0), _cols(qkv, ts, WIDTH, 1), _cols(qkv, ts, WIDTH, 2),
                   _rows(gb, ts), (grow, (GDN_HEADS, nc, CHUNK), lambda i: (0, i, 0))],
                  [wide, wide, wide, wide, perhead, perhead])


def _gdn_scan_fwd(u, w, qd, kd, aqk, gb, ts):
    S = u.shape[0]
    nc = ts // CHUNK
    N = S // CHUNK

    def body(t, first, ins, outs, scratch):
        u_ref, w_ref, qd_ref, kd_ref, aqk_ref, gb_ref = ins
        o_ref, vn_ref, st_ref = outs
        (state,) = scratch

        @pl.when(first)
        def _():
            state[...] = jnp.zeros_like(state)

        def chunk(c, _):
            r0 = pl.multiple_of(c * CHUNK, CHUNK)
            rows = pl.ds(r0, CHUNK)
            glast = gb_ref[pl.ds(r0 + CHUNK - 1, 1), :]
            heads = range(GDN_HEADS)
            cols = [slice(h * GDN_DIM, (h + 1) * GDN_DIM) for h in heads]
            S_old = [state[h] for h in heads]
            u_h = [u_ref[rows, cols[h]] for h in heads]
            w_h = [_b16(w_ref[rows, cols[h]]) for h in heads]
            qd_h = [_b16(qd_ref[rows, cols[h]]) for h in heads]
            kd_h = [_b16(kd_ref[rows, cols[h]]) for h in heads]
            aqk_h = [_b16(aqk_ref[h, rows, :]) for h in heads]
            both = [_dot(jnp.concatenate([w_h[h], qd_h[h]], axis=0), _b16(S_old[h]))
                    for h in heads]
            vn_h = [u_h[h] - both[h][:CHUNK] for h in heads]
            vnb = [_b16(vn_h[h]) for h in heads]
            intra = [_dot(aqk_h[h], vnb[h]) for h in heads]
            outer = [_dot(kd_h[h], vnb[h], TN) for h in heads]
            o_h = [both[h][CHUNK:] + intra[h] for h in heads]
            S_new = [S_old[h] * jnp.exp(glast[:, LANE_BA + h:LANE_BA + h + 1]) + outer[h] for h in heads]
            for h in heads:
                st_ref[c, h] = S_old[h]
                state[h] = S_new[h]
                o_ref[rows, cols[h]] = o_h[h]
                vn_ref[rows, cols[h]] = vn_h[h]
            return 0

        lax.fori_loop(0, nc, chunk, 0)

    wide_in = lambda a: _rows(a, ts)
    wide = _orow(S, (WIDTH,), F32, ts)
    states = ((N, GDN_HEADS, GDN_DIM, GDN_DIM), F32, (nc, GDN_HEADS, GDN_DIM, GDN_DIM),
              lambda i: (i, 0, 0, 0))
    return _tiled("gdn_scan_fwd", body, S // ts,
                  [wide_in(u), wide_in(w), wide_in(qd), wide_in(kd),
                   (aqk, (GDN_HEADS, ts, CHUNK), lambda i: (0, i, 0)), _rows(gb, ts)],
                  [wide, wide, states],
                  scratch=[pltpu.VMEM((GDN_HEADS, GDN_DIM, GDN_DIM), F32)])


def _gdn_scan_bwd(do, w, qd, kd, aqk, vn, states, gb, ts):
    S = do.shape[0]
    nc = ts // CHUNK
    N = S // CHUNK

    def body(t, first, ins, outs, scratch):
        do_ref, w_ref, qd_ref, kd_ref, aqk_ref, vn_ref, st_ref, gb_ref = ins
        du_ref, dw_ref, dqd_ref, dkd_ref, daqk_ref, dgl_ref = outs
        (dstate,) = scratch

        @pl.when(first)
        def _():
            dstate[...] = jnp.zeros_like(dstate)

        r = lax.broadcasted_iota(jnp.int32, (CHUNK, CHUNK), 0)
        cc = lax.broadcasted_iota(jnp.int32, (CHUNK, CHUNK), 1)
        incl = cc <= r
        lane = lax.broadcasted_iota(jnp.int32, (1, 128), 1)

        def chunk(k, _):
            c = nc - 1 - k
            r0 = pl.multiple_of(c * CHUNK, CHUNK)
            rows = pl.ds(r0, CHUNK)
            glast = gb_ref[pl.ds(r0 + CHUNK - 1, 1), :]
            dgl_row = jnp.zeros((1, 128), F32)
            heads = range(GDN_HEADS)
            cols = [slice(h * GDN_DIM, (h + 1) * GDN_DIM) for h in heads]
            S_h = [st_ref[c, h] for h in heads]
            dS_h = [dstate[h] for h in heads]
            do_h = [_b16(do_ref[rows, cols[h]]) for h in heads]
            aqk_h = [_b16(aqk_ref[h, rows, :]) for h in heads]
            vn_h = [_b16(vn_ref[rows, cols[h]]) for h in heads]
            kd_h = [_b16(kd_ref[rows, cols[h]]) for h in heads]
            qd_h = [_b16(qd_ref[rows, cols[h]]) for h in heads]
            w_h = [_b16(w_ref[rows, cols[h]]) for h in heads]
            Sb = [_b16(S_h[h]) for h in heads]
            dSb = [_b16(dS_h[h]) for h in heads]
            dvn_a = [_dot(aqk_h[h], do_h[h], TN) for h in heads]
            dvn_b = [_dot(kd_h[h], dSb[h]) for h in heads]
            daqk_h = [jnp.where(incl, _dot(do_h[h], vn_h[h], NT), 0.0) for h in heads]
            dkd_h = [_dot(vn_h[h], dSb[h], NT) for h in heads]
            dvn_h = [dvn_a[h] + dvn_b[h] for h in heads]
            both = [jnp.concatenate([do_h[h], _b16(dvn_h[h])], axis=0) for h in heads]
            by_state = [_dot(both[h], Sb[h], NT) for h in heads]
            dS_dot = [_dot(jnp.concatenate([qd_h[h], -w_h[h]], axis=0), both[h], TN) for h in heads]
            res = []
            for h in heads:
                egl = jnp.exp(glast[:, LANE_BA + h:LANE_BA + h + 1])
                dgl = egl * jnp.sum(jnp.sum(dS_h[h] * S_h[h], axis=1, keepdims=True), axis=0,
                                    keepdims=True)
                dgl_row = jnp.where(lane == h, dgl, dgl_row)
                res.append((daqk_h[h], by_state[h][:CHUNK], dkd_h[h], -by_state[h][CHUNK:], dvn_h[h],
                            dS_dot[h] + egl * dS_h[h]))
            for h in heads:
                daqk, dqd, dkd, dw, dvn, dS_new = res[h]
                daqk_ref[h, rows, :] = daqk
                dqd_ref[rows, cols[h]] = dqd
                dkd_ref[rows, cols[h]] = dkd
                dw_ref[rows, cols[h]] = dw
                du_ref[rows, cols[h]] = dvn
                dstate[h] = dS_new
            dgl_ref[pl.ds(c, 1), :] = dgl_row
            return 0

        lax.fori_loop(0, nc, chunk, 0)

    wide_in = lambda a: _rows(a, ts)
    wide = _orow(S, (WIDTH,), F32, ts)
    perhead_in = lambda a: (a, (GDN_HEADS, ts, CHUNK), lambda i: (0, i, 0))
    perhead = ((GDN_HEADS, S, CHUNK), F32, (GDN_HEADS, ts, CHUNK), lambda i: (0, i, 0))
    return _tiled("gdn_scan_bwd", body, S // ts,
                  [wide_in(do), wide_in(w), wide_in(qd), wide_in(kd), perhead_in(aqk), wide_in(vn),
                   (states, (nc, GDN_HEADS, GDN_DIM, GDN_DIM), lambda i: (i, 0, 0, 0)), _rows(gb, ts)],
                  [wide, wide, wide, wide, perhead, ((N, 128), F32, (nc, 128), lambda i: (i, 0))],
                  scratch=[pltpu.VMEM((GDN_HEADS, GDN_DIM, GDN_DIM), F32)], reverse=True)


def _gdn_local_bwd(qkv, gb, grow, T, du, dw, dqd, dkd, daqk, dgl, ts):
    S = qkv.shape[0]
    nc = ts // CHUNK

    def body(t, first, ins, outs, scratch):
        (q_ref, k_ref, v_ref, gb_ref, gr_ref, T_ref, du_ref, dw_ref, dqd_ref, dkd_ref,
         daqk_ref, dgl_ref) = ins
        dqkv_ref, dgb_ref = outs
        gbv = gb_ref[...]
        dglv = dgl_ref[...]
        lane = lax.broadcasted_iota(jnp.int32, (ts, 128), 1)
        dG_all = jnp.zeros((ts, 128), F32)
        dbeta_all = jnp.zeros((ts, 128), F32)
        heads = range(GDN_HEADS)
        _, _, eye = _chunk_masks(nc)
        pre = []
        for h in heads:
            lo = h * GDN_DIM
            cols = slice(lo, lo + GDN_DIM)
            r3 = lambda ref: ref[:, cols].reshape(nc, CHUNK, GDN_DIM)
            qh, kh, vh = r3(q_ref), r3(k_ref), r3(v_ref)
            duh, dwh, dqdh, dkdh = r3(du_ref), r3(dw_ref), r3(dqd_ref), r3(dkd_ref)
            Gc = _lane_pick(gbv, LANE_BA + h).reshape(nc, CHUNK, 1)
            beta = _lane_pick(gbv, LANE_BB + h).reshape(nc, CHUNK, 1)
            Gr = gr_ref[h].reshape(nc, 1, CHUNK)
            Th = T_ref[h].reshape(nc, CHUNK, CHUNK)
            daq = daqk_ref[h].reshape(nc, CHUNK, CHUNK)
            local = _chunk_local(qh, kh, vh, Gc, Gr, beta)
            kb, eG = local[3], local[6]
            vb = vh * beta
            kbg = kb * eG
            dvb = _bdot(Th, duh, 1, 1, PREC_UT)
            dkbg = _bdot(Th, dwh, 1, 1, PREC_UT)
            dT = _bdot(duh, vb, 2, 2, PREC_UT) + _bdot(dwh, kbg, 2, 2, PREC_UT)
            pre.append((qh, kh, vh, dqdh, dkdh, beta, Th, daq, local, kbg, dvb, dkbg, dT))
        M1s = [_bdot(pre[h][6], pre[h][12], 1, 1, PREC_UT) for h in heads]
        dAs = [_bdot(M1s[h], pre[h][6], 2, 2, PREC_UT) for h in heads]
        for h in heads:
            lo = h * GDN_DIM
            cols = slice(lo, lo + GDN_DIM)
            qh, kh, vh, dqdh, dkdh, beta, Th, daq, local, kbg, dvb, dkbg, dT = pre[h]
            incl, strict, gamma, kb, P, Qk, eG, edec = local
            dA = jnp.where(strict, -dAs[h], 0.0)
            dP = dA * gamma
            dQ = daq * gamma
            dgam = (dA * P + daq * Qk) * gamma
            dPb, dQb = _b16(dP), _b16(dQ)
            khb, qhb, kbb = _b16(kh), _b16(qh), _b16(kb)
            dq = _bdot(dQb, khb, 2, 1) + dqdh * eG
            dkb = _bdot(dPb, khb, 2, 1) + dkbg * eG
            dk = (_bdot(dQb, qhb, 1, 1) + _bdot(dPb, kbb, 1, 1) + dkdh * edec + dkb * beta)
            dbeta = (jnp.sum(dkb * kh, axis=2, keepdims=True) + jnp.sum(dvb * vh, axis=2, keepdims=True))
            dv = dvb * beta
            col_as_col = jnp.sum(jnp.where(eye, jnp.sum(dgam, axis=1, keepdims=True), 0.0),
                                 axis=2, keepdims=True)
            kd_term = jnp.sum(dkdh * kh * edec, axis=2, keepdims=True)
            dG = (jnp.sum(dgam, axis=2, keepdims=True) - col_as_col
                  + jnp.sum(dqdh * qh * eG, axis=2, keepdims=True)
                  + jnp.sum(dkbg * kbg, axis=2, keepdims=True) - kd_term)
            dgl_h = dglv[:, h:h + 1].reshape(nc, 1, 1) + jnp.sum(kd_term, axis=1, keepdims=True)
            last = lax.broadcasted_iota(jnp.int32, (nc, CHUNK, 1), 1) == CHUNK - 1
            dG = dG + jnp.where(last, dgl_h, 0.0)
            dqkv_ref[:, cols] = dq.reshape(ts, GDN_DIM)
            dqkv_ref[:, WIDTH + lo:WIDTH + lo + GDN_DIM] = dk.reshape(ts, GDN_DIM)
            dqkv_ref[:, 2 * WIDTH + lo:2 * WIDTH + lo + GDN_DIM] = dv.reshape(ts, GDN_DIM)
            dG_all = jnp.where(lane == LANE_BA + h, dG.reshape(ts, 1), dG_all)
            dbeta_all = jnp.where(lane == LANE_BB + h, dbeta.reshape(ts, 1), dbeta_all)
        dg_all = _scan_rows(dG_all, ts, CHUNK, reverse=True)
        dgb_ref[...] = jnp.where(lane < LANE_BB, dg_all, dbeta_all)

    wide_in = lambda a: _rows(a, ts)
    perhead_in = lambda a: (a, (GDN_HEADS, ts, CHUNK), lambda i: (0, i, 0))
    return _tiled("gdn_local_bwd", body, S // ts,
                  [_cols(qkv, ts, WIDTH, 0), _cols(qkv, ts, WIDTH, 1), _cols(qkv, ts, WIDTH, 2),
                   _rows(gb, ts), (grow, (GDN_HEADS, nc, CHUNK), lambda i: (0, i, 0)), perhead_in(T),
                   wide_in(du), wide_in(dw), wide_in(dqd), wide_in(dkd), perhead_in(daqk),
                   (dgl, (nc, 128), lambda i: (i, 0))],
                  [_orow(S, (3 * WIDTH,), F32, ts), _orow(S, (128,), F32, ts)])


def _gdn_prep_bwd(dqkv, dgb, cpre, z, conv_w, a128, dt128, dz, ts):
    S = z.shape[0]
    C3 = 3 * WIDTH
    hb = ts // 8
    n_tiles = S // ts

    def dpre(dq, c):
        y, dsil = _silu_and_grad(c)
        parts = []
        for h in range(GDN_HEADS):
            lo = h * GDN_DIM
            yq = y[:, lo:lo + GDN_DIM]
            rq = _l2_fwd(yq)
            nq = yq * rq
            dn = dq[:, lo:lo + GDN_DIM] * (GDN_DIM ** -0.5)
            parts.append(rq * (dn - nq * jnp.sum(dn * nq, axis=1, keepdims=True)))
        for h in range(GDN_HEADS):
            lo = WIDTH + h * GDN_DIM
            yk = y[:, lo:lo + GDN_DIM]
            rk = _l2_fwd(yk)
            nk = yk * rk
            dn = dq[:, lo:lo + GDN_DIM]
            parts.append(rk * (dn - nk * jnp.sum(dn * nk, axis=1, keepdims=True)))
        parts.append(dq[:, 2 * WIDTH:])
        return jnp.concatenate(parts, axis=1) * dsil

    def body(t, first, ins, outs, scratch):
        (dq_ref, dqn_ref, c_ref, cn_ref, x_ref, xp_ref, zs_ref, dgb_ref, w_ref, a_ref, dt_ref) = ins
        dx_ref, dzs_ref, dw_ref, dad_ref = outs

        @pl.when(first)
        def _():
            dw_ref[...] = jnp.zeros_like(dw_ref)
            dad_ref[...] = jnp.zeros_like(dad_ref)

        dc = dpre(dq_ref[...], c_ref[...])
        dcn = jnp.where(t < n_tiles - 1, dpre(dqn_ref[...], cn_ref[...]), 0.0)
        dce = jnp.concatenate([dc, dcn], axis=0)
        w = w_ref[...]
        dx = w[3:4, :] * dc
        for back in (1, 2, 3):
            dx = dx + w[3 - back:4 - back, :] * pltpu.roll(dce, ts + 8 - back, 0)[:ts, :]
        dx_ref[...] = _b16(dx)
        halo = jnp.where(t > 0, xp_ref[...], 0.0)
        xe = jnp.concatenate([halo, x_ref[...]], axis=0)
        dw_ref[3:4, :] += jnp.sum(dc * xe[8:, :], axis=0, keepdims=True)
        for back in (1, 2, 3):
            dw_ref[3 - back:4 - back, :] += jnp.sum(dc * pltpu.roll(xe, back, 0)[8:, :], axis=0,
                                                     keepdims=True)
        zs = zs_ref[...]
        dgb = dgb_ref[...]
        lane = lax.broadcasted_iota(jnp.int32, zs.shape, 1)
        arg = zs + dt_ref[...]
        nega = -jnp.exp(a_ref[...])
        dba = dgb * nega * _sigmoid(arg)
        beta = _sigmoid(zs)
        dbb = dgb * beta * (1.0 - beta)
        dzs_ref[...] = jnp.where((lane >= LANE_BA) & (lane < LANE_BB), dba,
                                 jnp.where((lane >= LANE_BB) & (lane < LANE_BB + 4), dbb, 0.0))
        dad_ref[0:1, :] += jnp.sum(dgb * nega * _softplus(arg), axis=0, keepdims=True)
        dad_ref[1:2, :] += jnp.sum(dba, axis=0, keepdims=True)

    nxt = lambda i: (jnp.minimum((i + 1) * hb, S // 8 - 1), 0)
    prv = lambda i: (jnp.maximum(i * hb - 1, 0), CB_BQKV)
    return _tiled("gdn_prep_bwd", body, n_tiles,
                  [_rows(dqkv, ts), (dqkv, (8, C3), nxt), _rows(cpre, ts), (cpre, (8, C3), nxt),
                   (z, (ts, C3), lambda i: (i, CB_BQKV)), (z, (8, C3), prv),
                   _cols(z, ts, 128, CB_SMALL), _rows(dgb, ts), _full(conv_w), _full(a128), _full(dt128)],
                  [((S, N_AL), BF16, (ts, C3), lambda i: (i, CB_BQKV)), _orow(S, (128,), F32, ts),
                   _oacc((8, C3), F32), _oacc((8, 128), F32)],
                  fill=(dz, 0))


def _mem_attn_fwd(z, mk, mv, ts):
    S = z.shape[0]

    def body(t, first, ins, outs, scratch):
        q_ref, mk_ref, mv_ref = ins
        (o_ref,) = outs
        heads = range(MEM_HEADS)
        cols = [slice(h * MEM_DIM, (h + 1) * MEM_DIM) for h in heads]
        s = [_dot(_b16(q_ref[:, cols[h]]), _b16(mk_ref[:, cols[h]]), NT) * (MEM_DIM ** -0.5)
             for h in heads]
        p = []
        for h in heads:
            e = jnp.exp(s[h] - jnp.max(s[h], axis=1, keepdims=True))
            p.append(_b16(e / jnp.sum(e, axis=1, keepdims=True)))
        o = [_dot(p[h], _b16(mv_ref[:, cols[h]])) for h in heads]
        for h in heads:
            o_ref[:, cols[h]] = o[h]

    (o,) = _tiled("mem_attn_fwd", body, S // ts, [_cols(z, ts, WIDTH, CB_MQ), _full(mk), _full(mv)],
                  [_orow(S, (WIDTH,), F32, ts)])
    return o


def _mem_attn_bwd(do, z, mk, mv, dz, ts):
    S = z.shape[0]
    M = mk.shape[0]

    def body(t, first, ins, outs, scratch):
        do_ref, q_ref, mk_ref, mv_ref = ins
        dq_ref, dmk_ref, dmv_ref = outs

        @pl.when(first)
        def _():
            dmk_ref[...] = jnp.zeros_like(dmk_ref)
            dmv_ref[...] = jnp.zeros_like(dmv_ref)

        scale = MEM_DIM ** -0.5
        heads = range(MEM_HEADS)
        cols = [slice(h * MEM_DIM, (h + 1) * MEM_DIM) for h in heads]
        qb = [_b16(q_ref[:, cols[h]]) for h in heads]
        kb = [_b16(mk_ref[:, cols[h]]) for h in heads]
        dob = [_b16(do_ref[:, cols[h]]) for h in heads]
        s = [_dot(qb[h], kb[h], NT) * scale for h in heads]
        dp = [_dot(dob[h], _b16(mv_ref[:, cols[h]]), NT) for h in heads]
        p = []
        for h in heads:
            e = jnp.exp(s[h] - jnp.max(s[h], axis=1, keepdims=True))
            p.append(e / jnp.sum(e, axis=1, keepdims=True))
        dsb = [_b16(p[h] * (dp[h] - jnp.sum(dp[h] * p[h], axis=1, keepdims=True)) * scale) for h in heads]
        dmv = [_dot(_b16(p[h]), dob[h], TN) for h in heads]
        dq = [_dot(dsb[h], kb[h]) for h in heads]
        dmk = [_dot(dsb[h], qb[h], TN) for h in heads]
        for h in heads:
            dmv_ref[:, cols[h]] += dmv[h]
            dq_ref[:, cols[h]] = _b16(dq[h])
            dmk_ref[:, cols[h]] += dmk[h]

    return _tiled("mem_attn_bwd", body, S // ts,
                  [_rows(do, ts), _cols(z, ts, WIDTH, CB_MQ), _full(mk), _full(mv)],
                  [((S, N_AL), BF16, (ts, WIDTH), lambda i: (i, CB_MQ)), _oacc((M, WIDTH), F32),
                   _oacc((M, WIDTH), F32)],
                  fill=(dz, 0))


def _head_norm(ob, g):
    xs, rs = [], []
    for h in range(GDN_HEADS):
        o = ob[:, h * GDN_DIM:(h + 1) * GDN_DIM]
        r = lax.rsqrt(jnp.mean(o * o, axis=1, keepdims=True) + EPS)
        xs.append(o * r)
        rs.append(r)
    return xs, rs


def _merge_fwd(x, z, o_a, o_b, o_m, gdn_g, b_merge, wb, wout, ts):
    S, D = x.shape

    def body(t, first, ins, outs, scratch):
        (x_ref, g_ref, oa_ref, az_ref, ob_ref, bz_ref, om_ref, mz_ref, gg_ref, bm_ref, wb_ref,
         wo_ref) = ins
        xo_ref, ya_ref, yb_ref, ym_ref, mg_ref = outs
        ya = oa_ref[...] * _silu_and_grad(az_ref[...])[0]
        xs, _ = _head_norm(ob_ref[...], None)
        nb = jnp.concatenate([xh * gg_ref[...] for xh in xs], axis=1)
        yb = nb * _silu_and_grad(bz_ref[...])[0]
        ym = om_ref[...] * _silu_and_grad(mz_ref[...])[0]
        merged = jnp.zeros((ts, D), F32)
        for n, (y, y_ref) in enumerate(((ya, ya_ref), (yb, yb_ref), (ym, ym_ref))):
            yb16 = _b16(y)
            y_ref[...] = yb16
            gate = _sigmoid(g_ref[:, n * D:(n + 1) * D] + bm_ref[:, n * D:(n + 1) * D])
            merged = merged + gate * _dot(yb16, wb_ref[n])
        mb = _b16(merged)
        mg_ref[...] = mb
        xo_ref[...] = x_ref[...] + _dot(mb, wo_ref[...])

    half = lambda a: _rows(a, ts)
    return _tiled("merge_fwd", body, S // ts,
                  [_rows(x, ts), _cols(z, ts, 3 * D, CB_GATES), half(o_a), _cols(z, ts, WIDTH, CB_AZ),
                   half(o_b), _cols(z, ts, WIDTH, CB_BZ), half(o_m), _cols(z, ts, WIDTH, CB_MZ),
                   _full(gdn_g.reshape(1, GDN_DIM)), _full(b_merge.reshape(1, 3 * D)), _full(wb), _full(wout)],
                  [_orow(S, (D,), F32, ts), _orow(S, (WIDTH,), BF16, ts), _orow(S, (WIDTH,), BF16, ts),
                   _orow(S, (WIDTH,), BF16, ts), _orow(S, (D,), BF16, ts)])


def _merge_bwd(dout, z, o_a, o_b, o_m, ya, yb, ym, gdn_g, b_merge, wb, wout, hsum, ts):
    S, D = dout.shape

    def body(t, first, ins, outs, scratch):
        (do_ref, g_ref, oa_ref, az_ref, ob_ref, bz_ref, om_ref, mz_ref, ya_ref, yb_ref, ym_ref,
         gg_ref, bm_ref, wb_ref, wo_ref, hs_ref) = ins
        (dg_ref, dpa_ref, dpb_ref, dpm_ref, doa_ref, dob_ref, dom_ref, dl_ref, dbm_ref, dgg_ref) = outs
        G3 = 3 * D

        @pl.when(first)
        def _():
            dbm_ref[...] = jnp.zeros_like(dbm_ref)
            dgg_ref[...] = jnp.zeros_like(dgg_ref)

        dmerged = _dot(_b16(do_ref[...]), wo_ref[...], NT)
        dys = []
        for n, (y_ref, dp_ref) in enumerate(((ya_ref, dpa_ref), (yb_ref, dpb_ref), (ym_ref, dpm_ref))):
            sl = slice(n * D, (n + 1) * D)
            gate = _sigmoid(g_ref[:, sl] + bm_ref[:, sl])
            proj = _dot(y_ref[...], wb_ref[n])
            dproj = _b16(gate * dmerged)
            dp_ref[...] = dproj
            dgp = dmerged * proj * gate * (1.0 - gate)
            dg_ref[:, sl] = dgp.astype(dg_ref.dtype)
            dbm_ref[0:1, sl] += jnp.sum(dgp, axis=0, keepdims=True)
            dys.append(_dot(dproj, wb_ref[n], NT))
        dya, dyb, dym = dys
        sa, dsa = _silu_and_grad(az_ref[...])
        oa = oa_ref[...]
        doa = dya * sa
        doa_ref[...] = doa
        dg_ref[:, G3:G3 + WIDTH] = _b16(dya * oa * dsa)
        dl_ref[...] = _dot(hs_ref[...], doa * oa, NT, HIGHEST)
        sm, dsm = _silu_and_grad(mz_ref[...])
        dom_ref[...] = dym * sm
        dg_ref[:, G3 + 2 * WIDTH:G3 + 3 * WIDTH] = _b16(dym * om_ref[...] * dsm)
        sb, dsb = _silu_and_grad(bz_ref[...])
        xs, rs = _head_norm(ob_ref[...], None)
        gg = gg_ref[...]
        dgg = jnp.zeros((1, GDN_DIM), F32)
        for h in range(GDN_HEADS):
            cols = slice(h * GDN_DIM, (h + 1) * GDN_DIM)
            dn = dyb[:, cols] * sb[:, cols]
            dg_ref[:, G3 + WIDTH + h * GDN_DIM:G3 + WIDTH + (h + 1) * GDN_DIM] = _b16(
                dyb[:, cols] * (xs[h] * gg) * dsb[:, cols])
            dgg = dgg + jnp.sum(dn * xs[h], axis=0, keepdims=True)
            dxh = dn * gg
            dob_ref[:, cols] = rs[h] * (dxh - xs[h] * jnp.mean(dxh * xs[h], axis=1, keepdims=True))
        dgg_ref[0:1, :] += dgg

    half = lambda a: _rows(a, ts)
    w512 = lambda dt: _orow(S, (WIDTH,), dt, ts)
    return _tiled("merge_bwd", body, S // ts,
                  [_rows(dout, ts), _cols(z, ts, 3 * D, CB_GATES), half(o_a), _cols(z, ts, WIDTH, CB_AZ),
                   half(o_b), _cols(z, ts, WIDTH, CB_BZ), half(o_m), _cols(z, ts, WIDTH, CB_MZ),
                   half(ya), half(yb), half(ym), _full(gdn_g.reshape(1, GDN_DIM)),
                   _full(b_merge.reshape(1, 3 * D)), _full(wb), _full(wout), _full(hsum)],
                  [((S, N_AL), BF16, (ts, 3 * D + 3 * WIDTH), lambda i: (i, CB_MERGE)),
                   _orow(S, (D,), BF16, ts), _orow(S, (D,), BF16, ts),
                   _orow(S, (D,), BF16, ts), w512(F32), w512(F32), w512(F32),
                   ((128, S), F32, (128, ts), lambda i: (0, i)), _oacc((8, 3 * D), F32),
                   _oacc((8, GDN_DIM), F32)])


def _to_aligned(w):
    parts = [w[..., lo:lo + n] for lo, n, _ in sorted(W_IN_PIECES, key=lambda p: p[2])]
    parts.append(jnp.zeros(w.shape[:-1] + (N_AL - N_IN,), w.dtype))
    return jnp.concatenate(parts, axis=-1)


def _from_aligned(w):
    return jnp.concatenate([w[..., al:al + n] for _, n, al in W_IN_PIECES], axis=-1)


def _lanes128(v, lane0):
    return jnp.pad(v.astype(F32)[None, :], ((0, 0), (lane0, 128 - lane0 - v.shape[0])))


def _tiles(S):
    ts = min(512, S // 2)
    return dict(ts=ts, ts_small=min(256, S // 2), tq=min(512, S // 4), tq_fwd=min(1024, S // 2))


def _layer_fwd(x, mem, p):
    S = x.shape[0]
    tl = _tiles(S)
    ts, tss, tq = tl["ts"], tl["ts_small"], tl["tq"]
    h, rstd = _rms_fwd("norm_fwd", x, p["norm_g"], ts)
    z = _mm("in_proj", h, p["w_in_al"], tn=1664)

    b_fg128 = _lanes128(p["b_fg"], LANE_AF)
    f_hi, f_mid, f_lo = _fox_decay(z, b_fg128, ts)
    aq = z[:, CB_AQ * WIDTH:(CB_AQ + 1) * WIDTH]
    ak = z[:, CB_AK * WIDTH:(CB_AK + 1) * WIDTH]
    av = z[:, CB_AV * WIDTH:(CB_AV + 1) * WIDTH]
    q32 = _heads_major(aq, FOX_HEADS, FOX_DIM)
    kh = _heads_major(ak, FOX_HEADS, FOX_DIM).astype(BF16)
    vh = _heads_major(av, FOX_HEADS, FOX_DIM).astype(BF16)
    piecesT = jnp.stack([f[:FOX_HEADS] for f in (f_hi, f_mid, f_lo)], axis=1)
    pieces = piecesT.transpose(0, 2, 1)
    ones3 = jnp.ones((FOX_HEADS, S, 3), BF16)
    padk = jnp.zeros((FOX_HEADS, S, FOX_AUG - FOX_DIM - 6), BF16)
    q_aug = jnp.concatenate([q32, pieces.astype(F32), ones3.astype(F32), padk.astype(F32)], axis=-1)
    k_aug = jnp.concatenate([kh, ones3, -pieces, padk], axis=-1)
    kT_aug = jnp.concatenate([kh.transpose(0, 2, 1), ones3.transpose(0, 2, 1), -piecesT,
                              padk.transpose(0, 2, 1)], axis=1)
    v_aug = jnp.concatenate([vh, ones3[:, :, :1], jnp.zeros((FOX_HEADS, S, 128 - FOX_DIM - 1), BF16)],
                            axis=-1)
    o_h, lse, qs = _fox_fwd(q_aug, kT_aug, v_aug, tl["tq_fwd"])
    o_a = _heads_minor(o_h)

    a128 = _lanes128(p["a_log"], LANE_BA)
    dt128 = _lanes128(p["dt_bias"], LANE_BA)
    qkv, cpre, gb, gbT = _gdn_prep(z, p["conv_w"], a128, dt128, ts)
    grow = gbT[LANE_BA:LANE_BA + GDN_HEADS].reshape(GDN_HEADS, S // CHUNK, CHUNK)
    u, w, qd, kd, aqk, T = _gdn_local_fwd(qkv, gb, grow, ts)
    o_b, vn, states = _gdn_scan_fwd(u, w, qd, kd, aqk, gb, ts)

    mem_h, mem_r = _rms_fwd("mem_norm_fwd", mem, p["mem_norm_g"], mem.shape[0])
    mkv = _mm("mem_kv", mem_h, p["w_mem_kv"])
    mk, mv = mkv[:, :WIDTH], mkv[:, WIDTH:]
    o_m = _mem_attn_fwd(z, mk, mv, ts)

    x_next, ya, yb, ym, merged = _merge_fwd(x, z, o_a, o_b, o_m, p["gdn_norm_g"], p["b_merge"],
                                            p["w_branch"], p["w_out"], ts)
    saved = dict(x=x, h=h, rstd=rstd, z=z, b_fg128=b_fg128, qs=qs, k_aug=k_aug, kT_aug=kT_aug, v_aug=v_aug, lse=lse, o_a=o_a, a128=a128, dt128=dt128, qkv=qkv, cpre=cpre, gb=gb,
                 grow=grow, w=w, qd=qd, kd=kd, aqk=aqk, T=T, o_b=o_b, vn=vn, states=states,
                 mem_h=mem_h, mem_r=mem_r, mk=mk, mv=mv, o_m=o_m, ya=ya, yb=yb, ym=ym, merged=merged)
    return x_next, saved


def _layer_bwd(dout, mem, p, s):
    S = dout.shape[0]
    tl = _tiles(S)
    ts, tss, tq = tl["ts"], tl["ts_small"], tl["tq"]
    z = s["z"]
    hsum = (jnp.arange(128)[:, None] == jnp.arange(WIDTH)[None, :] // FOX_DIM).astype(F32)
    (dz, dpa, dpb, dpm, do_a, do_b, do_m, deltaT, db_merge, dgdn_g) = _merge_bwd(
        dout, z, s["o_a"], s["o_b"], s["o_m"], s["ya"], s["yb"], s["ym"], p["gdn_norm_g"],
        p["b_merge"], p["w_branch"], p["w_out"], hsum, tss)
    g = {}
    g["b_merge"] = db_merge[0]
    g["gdn_norm_g"] = dgdn_g[0]
    g["w_out"] = _mm("dw_out", s["merged"], dout, ta=True)
    g["w_branch"] = jnp.stack([_mm("dw_branch", y, dp, ta=True)
                               for y, dp in ((s["ya"], dpa), (s["yb"], dpb), (s["ym"], dpm))])

    do_h = _heads_major(do_a, FOX_HEADS, FOX_DIM).astype(BF16)
    delta_row = deltaT[:FOX_HEADS, None, :]
    dqT, dk_h, dv_h, dfk, dfq = _fox_bwd(s["qs"], s["k_aug"], s["kT_aug"], s["v_aug"], do_h, s["lse"],
                                         delta_row, tq)
    daq = _heads_minor(dqT.transpose(0, 2, 1))
    dak = _heads_minor(dk_h)
    dav = _heads_minor(dv_h)
    daf128, db_fg = _fox_decay_bwd(dfk[:, 0, :], dfq[:, 0, :], z, s["b_fg128"], ts)
    g["b_fg"] = db_fg[:FOX_HEADS]

    du, dw, dqd, dkd, daqk, dgl = _gdn_scan_bwd(do_b, s["w"], s["qd"], s["kd"], s["aqk"], s["vn"],
                                                s["states"], s["gb"], ts)
    dqkv, dgb = _gdn_local_bwd(s["qkv"], s["gb"], s["grow"], s["T"], du, dw, dqd, dkd, daqk, dgl, ts)
    dz, dzs_b, dconv, dad = _gdn_prep_bwd(dqkv, dgb, s["cpre"], z, p["conv_w"], s["a128"],
                                          s["dt128"], dz, ts)
    g["conv_w"] = dconv[:4]
    g["a_log"] = dad[0, LANE_BA:LANE_BA + GDN_HEADS]
    g["dt_bias"] = dad[1, LANE_BA:LANE_BA + GDN_HEADS]

    dz, dmk, dmv = _mem_attn_bwd(do_m, z, s["mk"], s["mv"], dz, ts)
    dmkv = jnp.concatenate([dmk, dmv], axis=1)
    g["w_mem_kv"] = _mm("dw_mem_kv", s["mem_h"], dmkv, ta=True)
    dmem_h = _mm("dmem_h", dmkv, p["w_mem_kv"], tb=True)
    M = mem.shape[0]
    _, g["mem_norm_g"] = _rms_bwd("mem_norm_bwd", dmem_h, mem, s["mem_r"], p["mem_norm_g"],
                                  jnp.zeros_like(mem), M)

    lane = jnp.arange(128)[None, :]
    dsmall = jnp.where(lane < 8, daf128, dzs_b)
    daqkv = jnp.concatenate([_b16(daq), _b16(dak), _b16(dav)], axis=1)
    dz = lax.dynamic_update_slice(dz, daqkv, (0, CB_AQKV * 3 * WIDTH))
    dz = lax.dynamic_update_slice(dz, _b16(dsmall), (0, CB_SMALL * 128))
    g["w_in_al"] = _mm("dw_in", s["h"], dz, ta=True, tn=1664)
    dh = _mm("dh", dz, p["w_in_al"], tb=True, tk=1664)
    dx, g["norm_g"] = _rms_bwd("norm_bwd", dh, s["x"], s["rstd"], p["norm_g"], dout, ts)
    return dx, g


def _local_step(x, mem, layers, final_norm_g, loss_target):
    S = x.shape[0]
    saves = []
    cur = x
    for p in layers:
        cur, sv = _layer_fwd(cur, mem, p)
        saves.append(sv)
    dx, dgf, loss_lanes = _loss_head(cur, final_norm_g, loss_target, _tiles(S)["ts"])
    grads = [None] * len(layers)
    for l in reversed(range(len(layers))):
        dx, grads[l] = _layer_bwd(dx, mem, layers[l], saves[l])
    return loss_lanes, dx, grads, dgf


HBM_SPEC = pl.BlockSpec(memory_space=pltpu.HBM)


def _mesh_pos():
    return lax.axis_index("x"), lax.axis_index("y"), lax.axis_index("c")


def _comm_call(name, body, arrays, out_shapes, n_remote, n_local):
    n = len(arrays)

    def kern(*refs):
        body(refs[:n], refs[n:2 * n], refs[2 * n], refs[2 * n + 1], refs[2 * n + 2])

    return pl.pallas_call(
        kern, name=name, out_shape=out_shapes, in_specs=[HBM_SPEC] * n, out_specs=[HBM_SPEC] * n,
        scratch_shapes=[pltpu.SemaphoreType.DMA((n_remote,)), pltpu.SemaphoreType.DMA((n_remote,)),
                        pltpu.SemaphoreType.DMA((max(n_local, 1),))],
    )(*arrays)


def _remote(src, dst, send_sems, recv_sems, k, to):
    return pltpu.make_async_remote_copy(src_ref=src, dst_ref=dst, send_sem=send_sems.at[k],
                                        recv_sem=recv_sems.at[k], device_id=to, device_id_type=MESH_ID)


def _other_chips(mx, my):
    return [(1 - mx, my), (mx, 1 - my), (1 - mx, 1 - my)]


def _gather_chips(name, shards):
    n = len(shards)

    def body(ins, outs, send_sems, recv_sems, local_sems):
        mx, my, mc = _mesh_pos()
        me = 2 * mx + my
        sibling = (mx, my, 1 - mc)
        chips = _other_chips(mx, my)
        sends = []
        for a in range(n):
            for k, (px, py) in enumerate(chips):
                cp = _remote(ins[a].at[mc], outs[a].at[me, mc], send_sems, recv_sems, 6 * a + k,
                             (px, py, mc))
                cp.start()
                sends.append(cp)
        for a in range(n):
            for k, (px, py) in enumerate(chips):
                j = 2 * px + py
                _remote(ins[a].at[mc], outs[a].at[j, mc], send_sems, recv_sems, 6 * a + k,
                        (px, py, mc)).wait_recv()
                cp = _remote(outs[a].at[j, mc], outs[a].at[j, mc], send_sems, recv_sems, 6 * a + 3 + k,
                             sibling)
                cp.start()
                sends.append(cp)
        for a in range(n):
            for k, (px, py) in enumerate(chips):
                j = 2 * px + py
                _remote(outs[a].at[j, 1 - mc], outs[a].at[j, 1 - mc], send_sems, recv_sems,
                        6 * a + 3 + k, sibling).wait_recv()
        for cp in sends:
            cp.wait_send()

    shapes = [jax.ShapeDtypeStruct((N_CHIPS,) + s.shape, s.dtype) for s in shards]
    outs = _comm_call(name, body, shards, shapes, 6 * n, 0)
    me = 2 * lax.axis_index("x") + lax.axis_index("y")
    return [lax.dynamic_update_index_in_dim(o, s, me, 0) for o, s in zip(outs, shards)]


def _sibling_swap(gs):
    n = len(gs)

    def body(ins, outs, send_sems, recv_sems, local_sems):
        mx, my, mc = _mesh_pos()
        sends = []
        for a in range(n):
            cp = _remote(ins[a].at[:, 1 - mc], outs[a], send_sems, recv_sems, a, (mx, my, 1 - mc))
            cp.start()
            sends.append(cp)
        for cp in sends:
            cp.wait()

    shapes = [jax.ShapeDtypeStruct((g.shape[0],) + g.shape[2:], g.dtype) for g in gs]
    return _comm_call("grad_sibling_swap", body, gs, shapes, n, 0)


def _chip_exchange(ps):
    n = len(ps)

    def body(ins, outs, send_sems, recv_sems, local_sems):
        mx, my, mc = _mesh_pos()
        me = 2 * mx + my
        chips = _other_chips(mx, my)
        sends = []
        for a in range(n):
            for k, (px, py) in enumerate(chips):
                cp = _remote(ins[a].at[2 * px + py], outs[a].at[me], send_sems, recv_sems, 3 * a + k,
                             (px, py, mc))
                cp.start()
                sends.append(cp)
        for a in range(n):
            for k, (px, py) in enumerate(chips):
                _remote(ins[a].at[me], outs[a].at[2 * px + py], send_sems, recv_sems, 3 * a + k,
                        (px, py, mc)).wait_recv()
        for cp in sends:
            cp.wait_send()

    shapes = [jax.ShapeDtypeStruct(p.shape, p.dtype) for p in ps]
    outs = _comm_call("grad_chip_exchange", body, ps, shapes, 3 * n, 0)
    me = 2 * lax.axis_index("x") + lax.axis_index("y")
    return [lax.dynamic_update_index_in_dim(o, lax.dynamic_index_in_dim(p, me, 0, keepdims=False), me, 0)
            for o, p in zip(outs, ps)]


def _sibling_gather(hs):
    n = len(hs)

    def body(ins, outs, send_sems, recv_sems, local_sems):
        mx, my, mc = _mesh_pos()
        sends = []
        for a in range(n):
            cp = _remote(ins[a], outs[a], send_sems, recv_sems, a, (mx, my, 1 - mc))
            cp.start()
            sends.append(cp)
        for cp in sends:
            cp.wait()

    shapes = [jax.ShapeDtypeStruct(h.shape, h.dtype) for h in hs]
    theirs = _comm_call("grad_sibling_gather", body, hs, shapes, n, 0)
    first = lax.axis_index("c") == 0
    return [jnp.stack([jnp.where(first, h, t), jnp.where(first, t, h)]) for h, t in zip(hs, theirs)]


def _add_pairs(a, b, tr, out_dtype):
    n, H, C = a.shape

    def kern(a_ref, b_ref, o_ref):
        o_ref[...] = (a_ref[...] + b_ref[...]).astype(o_ref.dtype)

    spec = pl.BlockSpec((None, tr, C), lambda j, i: (j, i, 0))
    return pl.pallas_call(
        kern, name="grad_pair_sum", grid=(n, H // tr), in_specs=[spec, spec], out_specs=spec,
        out_shape=jax.ShapeDtypeStruct((n, H, C), out_dtype),
        compiler_params=_params(("parallel", "parallel")),
    )(a, b)


def _sum_slots(r4, tr):
    n, H, C = r4.shape

    def kern(r_ref, o_ref):
        f = lambda k: r_ref[k].astype(F32)
        o_ref[...] = ((f(0) + f(1)) + f(2)) + f(3)

    return pl.pallas_call(
        kern, name="grad_chip_sum", grid=(H // tr,),
        in_specs=[pl.BlockSpec((n, tr, C), lambda i: (0, i, 0))],
        out_specs=pl.BlockSpec((tr, C), lambda i: (i, 0)),
        out_shape=jax.ShapeDtypeStruct((H, C), F32),
        compiler_params=_params(("parallel",)),
    )(r4)


def _adamw(w, g, m, v, tr):
    R, C = w.shape
    c1 = 1.0 - ADAM_B1
    c2 = 1.0 - ADAM_B2
    bc1 = 1.0 - ADAM_B1 ** ADAM_STEP
    bc2 = 1.0 - ADAM_B2 ** ADAM_STEP

    def kern(w_ref, g_ref, m_ref, v_ref, d_ref, mo_ref, vo_ref):
        gv = g_ref[...]
        mn = ADAM_B1 * m_ref[...] + c1 * gv
        vn = ADAM_B2 * v_ref[...] + c2 * (gv * gv)
        m_hat = mn / bc1
        v_hat = vn / bc2
        d_ref[...] = -ADAM_LR * (m_hat / (jnp.sqrt(v_hat) + ADAM_EPS) + ADAM_WD * w_ref[...])
        mo_ref[...] = mn
        vo_ref[...] = vn

    spec = pl.BlockSpec((tr, C), lambda i: (i, 0))
    shape = jax.ShapeDtypeStruct((R, C), F32)
    return pl.pallas_call(
        kern, name="adamw", grid=(R // tr,), in_specs=[spec] * 4, out_specs=[spec] * 3,
        out_shape=[shape] * 3, compiler_params=_params(("parallel",)),
    )(w, g, m, v)


PACK_COLS = 1024
PACK_ROWS = 512
W_SHARD = N_IN // N_CHIPS
SLAB = ("conv_w", "w_mem_kv", "w_branch", "w_out")
SMALL =("norm_g", "b_fg", "b_merge", "a_log", "dt_bias", "gdn_norm_g", "mem_norm_g", "final_norm_g")
ALL_WEIGHTS = ("norm_g", "w_in", "b_fg", "b_merge", "conv_w", "a_log", "dt_bias", "gdn_norm_g",
               "mem_norm_g", "w_mem_kv", "w_branch", "w_out", "final_norm_g")
SHARD_AXIS = {"w_in": 2, "conv_w": 2, "w_mem_kv": 1, "w_branch": 3, "w_out": 1}


def _pack(arrays, row_multiple):
    flat = jnp.concatenate([a.reshape(-1) for a in arrays])
    n = flat.shape[0]
    rows = -(-n // PACK_COLS)
    rows = -(-rows // row_multiple) * row_multiple
    flat = jnp.pad(flat, (0, rows * PACK_COLS - n))
    return flat.reshape(rows, PACK_COLS)


def _unpack(slab, shapes):
    out, off = [], 0
    for shp in shapes:
        n = 1
        for d in shp:
            n *= d
        r0, r1 = off // PACK_COLS, -(-(off + n) // PACK_COLS)
        rows = slab[r0:r1].reshape(-1)
        out.append(rows[off - r0 * PACK_COLS:off - r0 * PACK_COLS + n].reshape(shp))
        off += n
    return out


def _shard_of(full, name, j):
    ax = SHARD_AXIS[name]
    n = full.shape[ax] // N_CHIPS
    return lax.slice_in_dim(full, j * n, (j + 1) * n, axis=ax)


def _aligned_from_shards(shards):
    def cols(lo, n):
        parts = []
        while n > 0:
            j, off = divmod(lo, W_SHARD)
            take = min(n, W_SHARD - off)
            parts.append(shards[j][..., off:off + take])
            lo, n = lo + take, n - take
        return parts

    out = []
    for lo, n, _ in sorted(W_IN_PIECES, key=lambda p: p[2]):
        out += cols(lo, n)
    out.append(jnp.zeros(shards[0].shape[:-1] + (N_AL - N_IN,), shards[0].dtype))
    return jnp.concatenate(out, axis=-1)


def _shard_from_aligned(w_al, j):
    lo_j, hi_j = j * W_SHARD, (j + 1) * W_SHARD
    parts = []
    for lo, n, al in W_IN_PIECES:
        a, b = max(lo, lo_j), min(lo + n, hi_j)
        if a < b:
            parts.append(w_al[..., al + a - lo:al + b - lo])
    return jnp.concatenate(parts, axis=-1)


def kernel(x, mem, norm_g, w_in, b_fg, b_merge, conv_w, a_log, dt_bias, gdn_norm_g, mem_norm_g, w_mem_kv, w_branch, w_out, final_norm_g, loss_target, m_norm_g, m_w_in, m_b_fg, m_b_merge, m_conv_w, m_a_log, m_dt_bias, m_gdn_norm_g, m_mem_norm_g, m_w_mem_kv, m_w_branch, m_w_out, m_final_norm_g, v_norm_g, v_w_in, v_b_fg, v_b_merge, v_conv_w, v_a_log, v_dt_bias, v_gdn_norm_g, v_mem_norm_g, v_w_mem_kv, v_w_branch, v_w_out, v_final_norm_g):
    wts = dict(norm_g=norm_g, w_in=w_in, b_fg=b_fg, b_merge=b_merge, conv_w=conv_w, a_log=a_log,
               dt_bias=dt_bias, gdn_norm_g=gdn_norm_g, mem_norm_g=mem_norm_g, w_mem_kv=w_mem_kv,
               w_branch=w_branch, w_out=w_out, final_norm_g=final_norm_g)
    mom = dict(norm_g=m_norm_g, w_in=m_w_in, b_fg=m_b_fg, b_merge=m_b_merge, conv_w=m_conv_w,
               a_log=m_a_log, dt_bias=m_dt_bias, gdn_norm_g=m_gdn_norm_g, mem_norm_g=m_mem_norm_g,
               w_mem_kv=m_w_mem_kv, w_branch=m_w_branch, w_out=m_w_out, final_norm_g=m_final_norm_g)
    vel = dict(norm_g=v_norm_g, w_in=v_w_in, b_fg=v_b_fg, b_merge=v_b_merge, conv_w=v_conv_w,
               a_log=v_a_log, dt_bias=v_dt_bias, gdn_norm_g=v_gdn_norm_g, mem_norm_g=v_mem_norm_g,
               w_mem_kv=v_w_mem_kv, w_branch=v_w_branch, w_out=v_w_out, final_norm_g=v_final_norm_g)

    big = ("w_in", "w_mem_kv", "w_branch", "w_out")
    gathered = _gather_chips("weight_gather", [wts[n].astype(BF16) for n in big] + [conv_w])
    all_w = dict(zip(big + ("conv_w",), gathered))
    w_in_al = _aligned_from_shards([all_w["w_in"][j] for j in range(N_CHIPS)])

    layers = []
    for l in range(DEPTH):
        rows_of = lambda n: all_w[n][:, l].reshape(D_MODEL, D_MODEL)
        last_of = lambda n: jnp.concatenate([all_w[n][j, l] for j in range(N_CHIPS)], axis=-1)
        layers.append(dict(norm_g=norm_g[l], w_in_al=w_in_al[l], b_fg=b_fg[l], b_merge=b_merge[l],
                           conv_w=jnp.pad(last_of("conv_w"), ((0, 4), (0, 0))), a_log=a_log[l],
                           dt_bias=dt_bias[l], gdn_norm_g=gdn_norm_g[l], mem_norm_g=mem_norm_g[l],
                           w_mem_kv=rows_of("w_mem_kv"), w_branch=last_of("w_branch"),
                           w_out=rows_of("w_out")))

    loss_lanes, dx, grads, dgf = _local_step(x[0], mem[0], layers, final_norm_g, loss_target[0])

    gfull = {n: jnp.stack([grads[l][n] for l in range(DEPTH)])
             for n in ("norm_g", "b_fg", "b_merge", "conv_w", "a_log", "dt_bias", "gdn_norm_g",
                       "mem_norm_g", "w_mem_kv", "w_branch", "w_out")}
    gfull["final_norm_g"] = dgf
    loss_local = jnp.sum(loss_lanes).reshape(1)
    small_g = [gfull[n] for n in SMALL] + [loss_local]
    dw_al = jnp.stack([grads[l]["w_in_al"] for l in range(DEPTH)])
    ga = jnp.stack([_shard_from_aligned(dw_al, j) for j in range(N_CHIPS)])
    mats = ("w_mem_kv", "w_branch", "w_out")
    rest = ("conv_w",) + SMALL
    gb = jnp.stack([_pack([_shard_of(gfull[n], n, j) for n in mats], PACK_ROWS)
                    for j in range(N_CHIPS)])
    gc = jnp.stack([_pack([_shard_of(gfull["conv_w"], "conv_w", j)] + small_g, 16)
                    for j in range(N_CHIPS)])
    halves = lambda g: g.reshape(N_CHIPS, 2, g.shape[1] // 2, PACK_COLS)
    gb, gc = halves(gb), halves(gc)

    mc = lax.axis_index("c")
    tr = 256
    trs = (tr, tr, 8)
    from_sibling = _sibling_swap([ga, gb, gc])
    mine = [lax.dynamic_index_in_dim(g, mc, axis=1, keepdims=False) for g in (ga, gb, gc)]
    pair = [_add_pairs(a, b, t, dt) for a, b, t, dt in zip(mine, from_sibling, trs, (BF16, BF16, F32))]
    slots = _chip_exchange(pair)
    half = [_sum_slots(s, t) for s, t in zip(slots, trs)]
    ga_sum, gb_sum, gc_sum = _sibling_gather(half)
    flat = lambda g: g.reshape(-1, PACK_COLS)

    g_un = dict(zip(mats, _unpack(flat(gb_sum), [wts[n].shape for n in mats])))
    g_un.update(zip(rest + ("loss",), _unpack(flat(gc_sum), [wts[n].shape for n in rest] + [(1,)])))
    g_un["w_in"] = ga_sum
    d_un, m_un, v_un = {}, {}, {}
    rows2d = lambda a: a.reshape(-1, a.shape[-1])
    for n in ("w_in", "w_mem_kv", "w_branch", "w_out"):
        res = _adamw(rows2d(wts[n]), rows2d(g_un[n]), rows2d(mom[n]), rows2d(vel[n]), tr)
        d_un[n], m_un[n], v_un[n] = [r.reshape(wts[n].shape) for r in res]
    little = ("conv_w",) + SMALL
    slab = lambda d: _pack([d[n] for n in little], 8)
    res = _adamw(slab(wts), slab(g_un), slab(mom), slab(vel), 8)
    little_shapes = [wts[n].shape for n in little]
    for out, r in zip((d_un, m_un, v_un), res):
        out.update(zip(little, _unpack(r, little_shapes)))

    loss = g_un["loss"][0]
    return (loss, dx[None], *[g_un[n] for n in ALL_WEIGHTS], *[d_un[n] for n in ALL_WEIGHTS],
            *[m_un[n] for n in ALL_WEIGHTS], *[v_un[n] for n in ALL_WEIGHTS])
```

```python
import functools

import jax
import jax.numpy as jnp
from jax import lax
from jax.experimental import pallas as pl
from jax.experimental.pallas import tpu as pltpu

F32 = jnp.float32
BF16 = jnp.bfloat16
HIGHEST = lax.Precision.HIGHEST
PREC_UT = lax.Precision.HIGH
MESH_ID = pl.DeviceIdType.MESH

D_MODEL = 1024
DEPTH = 2
CHUNK = 64
EPS = 1e-6
FOX_HEADS, FOX_DIM = 8, 64
GDN_HEADS, GDN_DIM = 4, 128
MEM_HEADS, MEM_DIM = 4, 128
WIDTH = 512
N_BRANCH = 3
N_IN = 8208
N_AL = 8320
N_CHIPS = 4
NEG = -1e30
LOG2E = 1.4426950408889634
LN2 = 0.6931471805599453

ADAM_LR, ADAM_B1, ADAM_B2, ADAM_EPS, ADAM_WD, ADAM_STEP = 0.001, 0.9, 0.999, 1e-08, 0.01, 10

CB_GATES = 0
CB_AZ, CB_BZ, CB_MZ = 6, 7, 8
CB_MERGE = 0
CB_BQKV = 3
CB_AQ, CB_AK, CB_AV = 12, 13, 14
CB_AQKV = 4
CB_MQ = 15
CB_SMALL = 64
W_IN_PIECES = ((0, 512, 6144), (512, 512, 6656), (1024, 512, 7168), (1536, 8, 8192), (1544, 512, 3072),
               (2056, 512, 4608), (2568, 512, 5120), (3080, 512, 5632), (3592, 4, 8200), (3596, 4, 8204),
               (3600, 512, 3584), (4112, 512, 7680), (4624, 512, 4096), (5136, 3072, 0))
LANE_AF, LANE_BA, LANE_BB = 0, 8, 12

NN = ((1,), (0,))
NT = ((1,), (1,))
TN = ((0,), (0,))

VMEM_LIMIT_BYTES = 56 * 1024 * 1024


def _dot(a, b, dims=NN, prec=None):
    return lax.dot_general(a, b, (dims, ((), ())), preferred_element_type=F32, precision=prec)


def _bdot(a, b, ca, cb, prec=None):
    return lax.dot_general(a, b, (((ca,), (cb,)), ((0,), (0,))), preferred_element_type=F32,
                           precision=prec)


def _b16(a):
    return a.astype(BF16)


def _eye(n, dtype=F32):
    r = lax.broadcasted_iota(jnp.int32, (n, n), 0)
    c = lax.broadcasted_iota(jnp.int32, (n, n), 1)
    return jnp.where(r == c, 1.0, 0.0).astype(dtype)


def _transpose_exact(x):
    return _dot(_eye(x.shape[1]), x, NT, HIGHEST)


def _col_to_row(col):
    n = col.shape[0]
    return jnp.sum(jnp.where(_eye(n) > 0.5, col, 0.0), axis=0, keepdims=True)


def _row_to_col(row):
    n = row.shape[1]
    return jnp.sum(jnp.where(_eye(n) > 0.5, row, 0.0), axis=1, keepdims=True)


def _sigmoid(x):
    return 1.0 / (1.0 + jnp.exp(-x))


def _softplus(x):
    return jnp.maximum(x, 0.0) + jnp.log(1.0 + jnp.exp(-jnp.abs(x)))


def _silu_and_grad(x):
    s = _sigmoid(x)
    return x * s, s * (1.0 + x * (1.0 - s))


def _params(semantics):
    return pltpu.CompilerParams(dimension_semantics=semantics, vmem_limit_bytes=VMEM_LIMIT_BYTES)


def _rows(a, ts):
    nd = a.ndim
    return (a, (ts,) + a.shape[1:], lambda i, nd=nd: (i,) + (0,) * (nd - 1))


def _cols(a, ts, width, cb):
    return (a, (ts, width), lambda i, cb=cb: (i, cb))


def _full(a):
    nd = a.ndim
    return (a, a.shape, lambda i, nd=nd: (0,) * nd)


def _orow(S, tail, dtype, ts):
    nd = 1 + len(tail)
    return ((S,) + tuple(tail), dtype, (ts,) + tuple(tail), lambda i, nd=nd: (i,) + (0,) * (nd - 1))


def _oacc(shape, dtype):
    nd = len(shape)
    return (tuple(shape), dtype, tuple(shape), lambda i, nd=nd: (0,) * nd)


def _tiled(name, body, n_steps, ins, outs, scratch=(), reverse=False, fill=None):
    def rev(imap):
        if not reverse:
            return imap
        return lambda i: imap(n_steps - 1 - i)

    in_specs = [pl.BlockSpec(blk, rev(imap)) for (_, blk, imap) in ins]
    out_specs = [pl.BlockSpec(blk, rev(imap)) for (_, _, blk, imap) in outs]
    out_shape = [jax.ShapeDtypeStruct(shape, dt) for (shape, dt, _, _) in outs]
    n_in, n_out = len(ins), len(outs)
    arrays = [a for (a, _, _) in ins]
    aliases = {}
    n_extra = 0
    if fill is not None:
        arrays.append(fill[0])
        in_specs.append(pl.BlockSpec(memory_space=pl.ANY))
        aliases = {n_in: fill[1]}
        n_extra = 1

    def kern(*refs):
        step = pl.program_id(0)
        t = (n_steps - 1 - step) if reverse else step
        lo = n_in + n_extra
        body(t, step == 0, refs[:n_in], refs[lo:lo + n_out], refs[lo + n_out:])

    res = pl.pallas_call(
        kern, name=name, grid=(n_steps,), in_specs=in_specs, out_specs=out_specs,
        out_shape=out_shape, scratch_shapes=list(scratch), input_output_aliases=aliases,
        compiler_params=_params(("arbitrary",)),
    )(*arrays)
    return res


def _pick(n, pref):
    if n <= pref:
        return n
    best = None
    for t in range(128, pref + 1, 128):
        if n % t == 0:
            best = t
    assert best is not None, (n, pref)
    return best


def _mm(name, a, b, ta=False, tb=False, out_dtype=F32, tm=1024, tn=1024, tk=1024):
    if ta:
        K, M = a.shape
    else:
        M, K = a.shape
    if tb:
        N, K2 = b.shape
    else:
        K2, N = b.shape
    assert K == K2, (a.shape, b.shape, ta, tb)
    tm, tn, tk = _pick(M, tm), _pick(N, tn), _pick(K, tk)
    nk = K // tk
    a_spec = (pl.BlockSpec((tk, tm), lambda i, j, k: (k, i)) if ta
              else pl.BlockSpec((tm, tk), lambda i, j, k: (i, k)))
    b_spec = (pl.BlockSpec((tn, tk), lambda i, j, k: (j, k)) if tb
              else pl.BlockSpec((tk, tn), lambda i, j, k: (k, j)))
    dims = ((0,) if ta else (1,), (1,) if tb else (0,))

    def kern_single(a_ref, b_ref, o_ref):
        o_ref[...] = _dot(_b16(a_ref[...]), _b16(b_ref[...]), dims).astype(o_ref.dtype)

    def kern_acc(a_ref, b_ref, o_ref, acc_ref):
        k = pl.program_id(2)

        @pl.when(k == 0)
        def _():
            acc_ref[...] = jnp.zeros_like(acc_ref)

        acc_ref[...] += _dot(_b16(a_ref[...]), _b16(b_ref[...]), dims)

        @pl.when(k == nk - 1)
        def _():
            o_ref[...] = acc_ref[...].astype(o_ref.dtype)

    return pl.pallas_call(
        kern_single if nk == 1 else kern_acc, name=name, grid=(M // tm, N // tn, nk),
        in_specs=[a_spec, b_spec],
        out_specs=pl.BlockSpec((tm, tn), lambda i, j, k: (i, j)),
        out_shape=jax.ShapeDtypeStruct((M, N), out_dtype),
        scratch_shapes=[] if nk == 1 else [pltpu.VMEM((tm, tn), F32)],
        compiler_params=_params(("parallel", "parallel", "arbitrary")),
    )(a, b)


def _rms_fwd(name, x, g, ts):
    S, D = x.shape

    def body(t, first, ins, outs, scratch):
        x_ref, g_ref = ins
        h_ref, r_ref = outs
        xv = x_ref[...]
        r = lax.rsqrt(jnp.mean(xv * xv, axis=1, keepdims=True) + EPS)
        h_ref[...] = (xv * r * g_ref[...]).astype(h_ref.dtype)
        r_ref[...] = r

    return _tiled(name, body, S // ts, [_rows(x, ts), _full(g.reshape(1, D))],
                  [_orow(S, (D,), BF16, ts), _orow(S, (1,), F32, ts)])


def _rms_bwd(name, dh, x, rstd, g, dres, ts):
    S, D = x.shape

    def body(t, first, ins, outs, scratch):
        dh_ref, x_ref, r_ref, g_ref, dres_ref = ins
        dx_ref, dg_ref = outs
        r = r_ref[...]
        xh = x_ref[...] * r
        dhv = dh_ref[...]
        dxh = dhv * g_ref[...]
        dx_ref[...] = dres_ref[...] + r * (dxh - xh * jnp.mean(dxh * xh, axis=1, keepdims=True))

        @pl.when(first)
        def _():
            dg_ref[...] = jnp.zeros_like(dg_ref)

        dg_ref[0:1, :] += jnp.sum(dhv * xh, axis=0, keepdims=True)

    dx, dg = _tiled(name, body, S // ts,
                    [_rows(dh, ts), _rows(x, ts), _rows(rstd, ts), _full(g.reshape(1, D)), _rows(dres, ts)],
                    [_orow(S, (D,), F32, ts), _oacc((8, D), F32)])
    return dx, dg[0]


def _loss_head(x, g, target, ts):
    S, D = x.shape

    def body(t, first, ins, outs, scratch):
        x_ref, g_ref, tgt_ref = ins
        dx_ref, dg_ref, loss_ref = outs
        xv = x_ref[...]
        gv = g_ref[...]
        r = lax.rsqrt(jnp.mean(xv * xv, axis=1, keepdims=True) + EPS)
        xh = xv * r
        err = xh * gv - tgt_ref[...]
        dy = err * (1.0 / D)
        dxh = dy * gv
        dx_ref[...] = r * (dxh - xh * jnp.mean(dxh * xh, axis=1, keepdims=True))

        @pl.when(first)
        def _():
            dg_ref[...] = jnp.zeros_like(dg_ref)
            loss_ref[...] = jnp.zeros_like(loss_ref)

        dg_ref[0:1, :] += jnp.sum(dy * xh, axis=0, keepdims=True)
        per_lane = jnp.sum(err * err, axis=0, keepdims=True)
        loss_ref[0:1, :] += per_lane * (0.5 / D)

    dx, dg, loss = _tiled("loss_head", body, S // ts,
                          [_rows(x, ts), _full(g.reshape(1, D)), _rows(target, ts)],
                          [_orow(S, (D,), F32, ts), _oacc((8, D), F32), _oacc((8, D), F32)])
    return dx, dg[0], loss[0]


def _scan_rows(x, length, seg, reverse=False):
    row = lax.broadcasted_iota(jnp.int32, x.shape, 0) % seg
    k = 1
    while k < seg:
        if reverse:
            x = x + jnp.where(row < seg - k, pltpu.roll(x, length - k, 0), 0.0)
        else:
            x = x + jnp.where(row >= k, pltpu.roll(x, k, 0), 0.0)
        k *= 2
    return x


def _fox_decay(z, b_fg128, ts):
    S = z.shape[0]

    def body(t, first, ins, outs, scratch):
        zs_ref, b_ref = ins
        hi_ref, mid_ref, lo_ref = outs
        (carry,) = scratch

        @pl.when(first)
        def _():
            carry[...] = jnp.zeros_like(carry)

        logf = -_softplus(-(zs_ref[...] + b_ref[...]))
        run = _scan_rows(logf, ts, ts) + carry[0:1, :]
        carry[0:1, :] = run[ts - 1:ts, :]
        f2 = run * LOG2E
        hi = f2.astype(BF16)
        r1 = f2 - hi.astype(F32)
        mid = r1.astype(BF16)
        lo = (r1 - mid.astype(F32)).astype(BF16)
        eye = _eye(128, BF16)
        hi_ref[...] = _dot(eye, hi, NT).astype(BF16)
        mid_ref[...] = _dot(eye, mid, NT).astype(BF16)
        lo_ref[...] = _dot(eye, lo, NT).astype(BF16)

    tcol = lambda dt: ((128, S), dt, (128, ts), lambda i: (0, i))
    return _tiled("fox_decay", body, S // ts,
                  [_cols(z, ts, 128, CB_SMALL), _full(b_fg128)],
                  [tcol(BF16), tcol(BF16), tcol(BF16)], scratch=[pltpu.VMEM((8, 128), F32)])


def _fox_decay_bwd(dfk_rows, dfq_rows, z, b_fg128, ts):
    S = z.shape[0]
    H = dfk_rows.shape[0]

    def body(t, first, ins, outs, scratch):
        dfk_ref, dfq_ref, zs_ref, b_ref = ins
        daf_ref, db_ref = outs
        (carry,) = scratch

        @pl.when(first)
        def _():
            carry[...] = jnp.zeros_like(carry)
            db_ref[...] = jnp.zeros_like(db_ref)

        r = lax.broadcasted_iota(jnp.int32, (H, 128), 0)
        c = lax.broadcasted_iota(jnp.int32, (H, 128), 1)
        place = jnp.where(r == c, 1.0, 0.0)
        df = _dot(dfk_ref[...] + dfq_ref[...], place, TN, HIGHEST)
        run = _scan_rows(df, ts, ts, reverse=True) + carry[0:1, :]
        carry[0:1, :] = run[0:1, :]
        daf = run * _sigmoid(-(zs_ref[...] + b_ref[...]))
        daf_ref[...] = daf
        db_ref[0:1, :] += jnp.sum(daf, axis=0, keepdims=True)

    rowsin = lambda a: (a, (H, ts), lambda i: (0, i))
    daf, db = _tiled("fox_decay_bwd", body, S // ts,
                     [rowsin(dfk_rows), rowsin(dfq_rows), _cols(z, ts, 128, CB_SMALL), _full(b_fg128)],
                     [_orow(S, (128,), F32, ts), _oacc((8, 128), F32)],
                     scratch=[pltpu.VMEM((8, 128), F32)], reverse=True)
    return daf, db[0]


FOX_AUG = 80


def _fox_fwd(q_aug, kT_aug, v_aug, tq):
    H, S, da = q_aug.shape
    dv = v_aug.shape[2]
    d = FOX_DIM
    tk = tq // 2
    qscale = (d ** -0.5) * LOG2E

    def kern(q_ref, kT_ref, v_ref, o_ref, lse_ref, qs_ref, s_buf, p_buf, m_scr, acc_scr):
        i = pl.program_id(1)
        col = lax.broadcasted_iota(jnp.int32, (1, da), 1)
        qb = _b16(q_ref[...] * jnp.where(col < d, qscale, 1.0))
        qs_ref[...] = qb

        def keys(t):
            return pl.ds(pl.multiple_of(t * tk, tk), tk)

        def stage(t, slot, mask_off, look_ahead):
            if look_ahead:
                s_buf[1 - slot] = _dot(qb, kT_ref[:, keys(t + 1)])
            pv = _dot(p_buf[1 - slot], v_ref[keys(jnp.maximum(t - 1, 0)), :])

            def scores():
                s = s_buf[slot]
                if mask_off is None:
                    return s
                r = lax.broadcasted_iota(jnp.int32, (tq, tk), 0)
                c = lax.broadcasted_iota(jnp.int32, (tq, tk), 1)
                return jnp.where(c + mask_off <= r, s, NEG)

            m = m_scr[...]
            m_new = jnp.maximum(m, jnp.max(scores(), axis=1, keepdims=True))
            alpha = jnp.exp2(m - m_new)
            p_buf[slot] = _b16(jnp.exp2(scores() - m_new))
            m_scr[...] = m_new
            acc_scr[...] = (acc_scr[...] + pv) * alpha

        s_buf[0] = _dot(qb, kT_ref[:, keys(0)])
        p_buf[1] = jnp.zeros((tq, tk), BF16)
        m_scr[...] = jnp.full((tq, 1), NEG, F32)
        acc_scr[...] = jnp.zeros((tq, dv), F32)

        def pair(n):
            stage(2 * n, 0, None, True)
            stage(2 * n + 1, 1, None, True)

        def quad(m, _):
            pair(2 * m)
            pair(2 * m + 1)
            return 0

        lax.fori_loop(0, i // 2, quad, 0)

        @pl.when(i % 2 == 1)
        def _():
            pair(i - 1)

        stage(2 * i, 0, 0, True)
        stage(2 * i + 1, 1, tk, False)
        acc = acc_scr[...] + _dot(p_buf[1], v_ref[keys(2 * i + 1), :])
        l = acc[:, d:d + 1]
        o_ref[...] = acc[:, :d] / l
        lse_ref[...] = _col_to_row(m_scr[...] + jnp.log(l) * LOG2E)

    return pl.pallas_call(
        kern, name="fox_fwd", grid=(H, S // tq),
        in_specs=[pl.BlockSpec((None, tq, da), lambda h, i: (h, i, 0)),
                  pl.BlockSpec((None, da, S), lambda h, i: (h, 0, 0)),
                  pl.BlockSpec((None, S, dv), lambda h, i: (h, 0, 0))],
        out_specs=[pl.BlockSpec((None, tq, d), lambda h, i: (h, i, 0)),
                   pl.BlockSpec((None, 1, tq), lambda h, i: (h, 0, i)),
                   pl.BlockSpec((None, tq, da), lambda h, i: (h, i, 0))],
        out_shape=[jax.ShapeDtypeStruct((H, S, d), F32), jax.ShapeDtypeStruct((H, 1, S), F32),
                   jax.ShapeDtypeStruct((H, S, da), BF16)],
        scratch_shapes=[pltpu.VMEM((2, tq, tk), F32), pltpu.VMEM((2, tq, tk), BF16),
                        pltpu.VMEM((tq, 1), F32), pltpu.VMEM((tq, dv), F32)],
        compiler_params=_params(("parallel", "arbitrary")),
    )(q_aug, kT_aug, v_aug)


def _fox_bwd(qs, k_aug, kT, v, do, lse_row, delta_row, tq):
    H, S, da = qs.shape
    d = FOX_DIM
    tk = tq
    nq = S // tq
    scale = d ** -0.5

    ts2 = tq // 2
    last = 2 * nq - 1

    def kern(q_ref, k_ref, kT_ref, v_ref, do_ref, lse_ref, dl_ref,
             dqT_ref, dk_ref, dv_ref, dfk_ref, dfq_ref,
             kq_buf, dp_buf, pb_buf, ds_buf, dk_scr, dv_scr, dfk_scr):
        j = pl.program_id(1)

        @pl.when(j == 0)
        def _():
            dqT_ref[...] = jnp.zeros_like(dqT_ref)
            dfq_ref[...] = jnp.zeros_like(dfq_ref)

        kb = k_ref[...]
        kTb = kT_ref[...]
        vb = v_ref[:, :d]
        dk_scr[...] = jnp.zeros_like(dk_scr)
        dv_scr[...] = jnp.zeros_like(dv_scr)
        dfk_scr[...] = jnp.zeros_like(dfk_scr)

        def queries(t):
            return pl.ds(pl.multiple_of(t * ts2, ts2), ts2)

        def products(t, slot):
            rows = queries(t)
            kq_buf[slot] = _dot(kb, q_ref[rows, :], NT)
            dp_buf[slot] = _dot(vb, do_ref[rows, :], NT)

        def pointwise(t, slot, mask_off):
            rows = queries(t)
            sT = kq_buf[slot]
            if mask_off is not None:
                r = lax.broadcasted_iota(jnp.int32, (tk, ts2), 0)
                c = lax.broadcasted_iota(jnp.int32, (tk, ts2), 1)
                sT = jnp.where(r <= c + mask_off, sT, NEG)
            pT = jnp.exp2(sT - lse_ref[:, rows])
            dsT = pT * (dp_buf[slot] - dl_ref[:, rows])
            pb_buf[slot] = _b16(pT)
            ds_buf[slot] = _b16(dsT)
            dfk_scr[...] -= jnp.sum(dsT, axis=1, keepdims=True)
            dfq_ref[:, rows] += jnp.sum(dsT, axis=0, keepdims=True)

        def accumulate(t, slot):
            rows = queries(t)
            dsb = ds_buf[slot]
            dv_scr[...] += _dot(pb_buf[slot], do_ref[rows, :])
            dk_scr[...] += _dot(dsb, q_ref[rows, :])
            dqT_ref[:, rows] += _dot(kTb, dsb) * scale

        def stage(t, slot, mask_off, has_prev):
            products(jnp.minimum(t + 1, last), 1 - slot)
            if has_prev:
                accumulate(t - 1, 1 - slot)
            pointwise(t, slot, mask_off)

        products(2 * j, 0)
        stage(2 * j, 0, 0, False)
        stage(2 * j + 1, 1, ts2, True)

        def pair(n):
            stage(2 * n, 0, None, True)
            stage(2 * n + 1, 1, None, True)

        def quad(m, _):
            pair(j + 1 + 2 * m)
            pair(j + 2 + 2 * m)
            return 0

        n_rest = nq - 1 - j
        lax.fori_loop(0, n_rest // 2, quad, 0)

        @pl.when(n_rest % 2 == 1)
        def _():
            pair(nq - 1)

        accumulate(last, 1)
        dk_ref[...] = dk_scr[:, :d] * LN2
        dv_ref[...] = dv_scr[...]
        dfk_ref[...] = _col_to_row(dfk_scr[...])

    tile = lambda h, j: (h, j, 0)
    whole = lambda h, j: (h, 0, 0)
    rowtile = lambda h, j: (h, 0, j)
    return pl.pallas_call(
        kern, name="fox_bwd", grid=(H, S // tk),
        in_specs=[pl.BlockSpec((None, S, da), whole),
                  pl.BlockSpec((None, tk, da), tile),
                  pl.BlockSpec((None, d, tk), lambda h, j: (h, 0, j)),
                  pl.BlockSpec((None, tk, 128), tile),
                  pl.BlockSpec((None, S, d), whole),
                  pl.BlockSpec((None, 1, S), whole),
                  pl.BlockSpec((None, 1, S), whole)],
        out_specs=[pl.BlockSpec((None, d, S), whole),
                   pl.BlockSpec((None, tk, d), tile),
                   pl.BlockSpec((None, tk, d), tile),
                   pl.BlockSpec((None, 1, tk), rowtile),
                   pl.BlockSpec((None, 1, S), whole)],
        out_shape=[jax.ShapeDtypeStruct((H, d, S), F32), jax.ShapeDtypeStruct((H, S, d), F32),
                   jax.ShapeDtypeStruct((H, S, d), F32), jax.ShapeDtypeStruct((H, 1, S), F32),
                   jax.ShapeDtypeStruct((H, 1, S), F32)],
        scratch_shapes=[pltpu.VMEM((2, tk, ts2), F32), pltpu.VMEM((2, tk, ts2), F32),
                        pltpu.VMEM((2, tk, ts2), BF16), pltpu.VMEM((2, tk, ts2), BF16),
                        pltpu.VMEM((tk, da), F32), pltpu.VMEM((tk, d), F32), pltpu.VMEM((tk, 1), F32)],
        compiler_params=_params(("parallel", "arbitrary")),
    )(qs, k_aug, kT, v, do, lse_row, delta_row)


def _heads_major(a, H, d):
    S = a.shape[0]
    return a.reshape(S, H, d).transpose(1, 0, 2)


def _heads_minor(a):
    H, S, d = a.shape
    return a.transpose(1, 0, 2).reshape(S, H * d)


def _lane_pick(x128, lane):
    return x128[:, lane:lane + 1]


def _l2_fwd(y):
    return lax.rsqrt(jnp.sum(y * y, axis=1, keepdims=True) + EPS)


def _gdn_prep(z, conv_w, a128, dt128, ts):
    S = z.shape[0]
    C3 = 3 * WIDTH
    hb = ts // 8

    def body(t, first, ins, outs, scratch):
        x_ref, halo_ref, zs_ref, w_ref, a_ref, dt_ref = ins
        qkv_ref, c_ref, gb_ref, gbT_ref = outs
        halo = jnp.where(t > 0, halo_ref[...], 0.0)
        xe = jnp.concatenate([halo, x_ref[...]], axis=0)
        w = w_ref[...]
        c = w[3:4, :] * xe[8:, :]
        for back in (1, 2, 3):
            c = c + w[3 - back:4 - back, :] * pltpu.roll(xe, back, 0)[8:, :]
        c_ref[...] = c
        y = c * _sigmoid(c)
        for h in range(GDN_HEADS):
            lo = h * GDN_DIM
            yq = y[:, lo:lo + GDN_DIM]
            qkv_ref[:, lo:lo + GDN_DIM] = yq * (_l2_fwd(yq) * (GDN_DIM ** -0.5))
            yk = y[:, WIDTH + lo:WIDTH + lo + GDN_DIM]
            qkv_ref[:, WIDTH + lo:WIDTH + lo + GDN_DIM] = yk * _l2_fwd(yk)
        qkv_ref[:, 2 * WIDTH:] = y[:, 2 * WIDTH:]
        zs = zs_ref[...]
        lane = lax.broadcasted_iota(jnp.int32, zs.shape, 1)
        g = -jnp.exp(a_ref[...]) * _softplus(zs + dt_ref[...])
        G = _scan_rows(g, ts, CHUNK)
        beta = _sigmoid(zs)
        out = jnp.where(lane < 8, pltpu.roll(g, 128 - LANE_BA, 1), jnp.where(lane < LANE_BB, G, beta))
        gb_ref[...] = out
        gbT_ref[...] = _transpose_exact(out)

    x_in = (z, (ts, C3), lambda i: (i, CB_BQKV))
    halo_in = (z, (8, C3), lambda i: (jnp.maximum(i * hb - 1, 0), CB_BQKV))
    return _tiled("gdn_prep", body, S // ts,
                  [x_in, halo_in, _cols(z, ts, 128, CB_SMALL), _full(conv_w), _full(a128), _full(dt128)],
                  [_orow(S, (C3,), F32, ts), _orow(S, (C3,), F32, ts), _orow(S, (128,), F32, ts),
                   ((128, S), F32, (128, ts), lambda i: (0, i))])


def _chunk_masks(nc):
    r = lax.broadcasted_iota(jnp.int32, (nc, CHUNK, CHUNK), 1)
    c = lax.broadcasted_iota(jnp.int32, (nc, CHUNK, CHUNK), 2)
    return c <= r, c < r, c == r


def _chunk_local(qh, kh, vh, Gc, Gr, beta):
    nc = qh.shape[0]
    incl, strict, _ = _chunk_masks(nc)
    gamma = jnp.exp(jnp.where(incl, Gc - Gr, NEG))
    kb = kh * beta
    P = _bdot(_b16(kb), _b16(kh), 2, 2)
    Qk = _bdot(_b16(qh), _b16(kh), 2, 2)
    eG = jnp.exp(Gc)
    Gl = Gc[:, CHUNK - 1:CHUNK, :]
    edec = jnp.exp(Gl - Gc)
    return incl, strict, gamma, kb, P, Qk, eG, edec


def _gdn_local_fwd(qkv, gb, grow, ts):
    S = qkv.shape[0]
    nc = ts // CHUNK

    def body(t, first, ins, outs, scratch):
        q_ref, k_ref, v_ref, gb_ref, gr_ref = ins
        u_ref, w_ref, qd_ref, kd_ref, aqk_ref, T_ref = outs
        gbv = gb_ref[...]
        heads = range(GDN_HEADS)
        _, _, eye = _chunk_masks(nc)
        T, X, rhs_u, rhs_w = [], [], [], []
        for h in heads:
            lo = h * GDN_DIM
            qh = q_ref[:, lo:lo + GDN_DIM].reshape(nc, CHUNK, GDN_DIM)
            kh = k_ref[:, lo:lo + GDN_DIM].reshape(nc, CHUNK, GDN_DIM)
            vh = v_ref[:, lo:lo + GDN_DIM].reshape(nc, CHUNK, GDN_DIM)
            Gc = _lane_pick(gbv, LANE_BA + h).reshape(nc, CHUNK, 1)
            beta = _lane_pick(gbv, LANE_BB + h).reshape(nc, CHUNK, 1)
            Gr = gr_ref[h].reshape(nc, 1, CHUNK)
            incl, strict, gamma, kb, P, Qk, eG, edec = _chunk_local(qh, kh, vh, Gc, Gr, beta)
            A = jnp.where(strict, P * gamma, 0.0)
            T.append(jnp.where(eye, 1.0, 0.0) - A)
            X.append(A)
            rhs_u.append(vh * beta)
            rhs_w.append(kb * eG)
            qd_ref[:, lo:lo + GDN_DIM] = (qh * eG).reshape(ts, GDN_DIM)
            kd_ref[:, lo:lo + GDN_DIM] = (kh * edec).reshape(ts, GDN_DIM)
            aqk_ref[h] = jnp.where(incl, Qk * gamma, 0.0).reshape(ts, CHUNK)
        for _ in range(5):
            X = [_bdot(X[h], X[h], 2, 1, PREC_UT) for h in heads]
            T = [T[h] + _bdot(T[h], X[h], 2, 1, PREC_UT) for h in heads]
        u = [_bdot(T[h], rhs_u[h], 2, 1, PREC_UT) for h in heads]
        w = [_bdot(T[h], rhs_w[h], 2, 1, PREC_UT) for h in heads]
        for h in heads:
            lo = h * GDN_DIM
            u_ref[:, lo:lo + GDN_DIM] = u[h].reshape(ts, GDN_DIM)
            w_ref[:, lo:lo + GDN_DIM] = w[h].reshape(ts, GDN_DIM)
            T_ref[h] = T[h].reshape(ts, CHUNK)

    wide = _orow(S, (WIDTH,), F32, ts)
    perhead = ((GDN_HEADS, S, CHUNK), F32, (GDN_HEADS, ts, CHUNK), lambda i: (0, i, 0))
    return _tiled("gdn_local_fwd", body, S // ts,
                  [_cols(qkv, ts, WIDTH, 0), _cols(qkv, ts, WIDTH, 1), _cols(qkv, ts, WIDTH, 2),
                   _rows(gb, ts), (grow, (GDN_HEADS, nc, CHUNK), lambda i: (0, i, 0))],
                  [wide, wide, wide, wide, perhead, perhead])


def _gdn_scan_fwd(u, w, qd, kd, aqk, gb, ts):
    S = u.shape[0]
    nc = ts // CHUNK
    N = S // CHUNK

    def body(t, first, ins, outs, scratch):
        u_ref, w_ref, qd_ref, kd_ref, aqk_ref, gb_ref = ins
        o_ref, vn_ref, st_ref = outs
        (state,) = scratch

        @pl.when(first)
        def _():
            state[...] = jnp.zeros_like(state)

        def chunk(c, _):
            r0 = pl.multiple_of(c * CHUNK, CHUNK)
            rows = pl.ds(r0, CHUNK)
            glast = gb_ref[pl.ds(r0 + CHUNK - 1, 1), :]
            heads = range(GDN_HEADS)
            cols = [slice(h * GDN_DIM, (h + 1) * GDN_DIM) for h in heads]
            S_old = [state[h] for h in heads]
            u_h = [u_ref[rows, cols[h]] for h in heads]
            w_h = [_b16(w_ref[rows, cols[h]]) for h in heads]
            qd_h = [_b16(qd_ref[rows, cols[h]]) for h in heads]
            kd_h = [_b16(kd_ref[rows, cols[h]]) for h in heads]
            aqk_h = [_b16(aqk_ref[h, rows, :]) for h in heads]
            both = [_dot(jnp.concatenate([w_h[h], qd_h[h]], axis=0), _b16(S_old[h]))
                    for h in heads]
            vn_h = [u_h[h] - both[h][:CHUNK] for h in heads]
            vnb = [_b16(vn_h[h]) for h in heads]
            intra = [_dot(aqk_h[h], vnb[h]) for h in heads]
            outer = [_dot(kd_h[h], vnb[h], TN) for h in heads]
            o_h = [both[h][CHUNK:] + intra[h] for h in heads]
            S_new = [S_old[h] * jnp.exp(glast[:, LANE_BA + h:LANE_BA + h + 1]) + outer[h] for h in heads]
            for h in heads:
                st_ref[c, h] = S_old[h]
                state[h] = S_new[h]
                o_ref[rows, cols[h]] = o_h[h]
                vn_ref[rows, cols[h]] = vn_h[h]
            return 0

        lax.fori_loop(0, nc, chunk, 0)

    wide_in = lambda a: _rows(a, ts)
    wide = _orow(S, (WIDTH,), F32, ts)
    states = ((N, GDN_HEADS, GDN_DIM, GDN_DIM), F32, (nc, GDN_HEADS, GDN_DIM, GDN_DIM),
              lambda i: (i, 0, 0, 0))
    return _tiled("gdn_scan_fwd", body, S // ts,
                  [wide_in(u), wide_in(w), wide_in(qd), wide_in(kd),
                   (aqk, (GDN_HEADS, ts, CHUNK), lambda i: (0, i, 0)), _rows(gb, ts)],
                  [wide, wide, states],
                  scratch=[pltpu.VMEM((GDN_HEADS, GDN_DIM, GDN_DIM), F32)])


def _gdn_scan_bwd(do, w, qd, kd, aqk, vn, states, gb, ts):
    S = do.shape[0]
    nc = ts // CHUNK
    N = S // CHUNK

    def body(t, first, ins, outs, scratch):
        do_ref, w_ref, qd_ref, kd_ref, aqk_ref, vn_ref, st_ref, gb_ref = ins
        du_ref, dw_ref, dqd_ref, dkd_ref, daqk_ref, dgl_ref = outs
        (dstate,) = scratch

        @pl.when(first)
        def _():
            dstate[...] = jnp.zeros_like(dstate)

        r = lax.broadcasted_iota(jnp.int32, (CHUNK, CHUNK), 0)
        cc = lax.broadcasted_iota(jnp.int32, (CHUNK, CHUNK), 1)
        incl = cc <= r
        lane = lax.broadcasted_iota(jnp.int32, (1, 128), 1)

        def chunk(k, _):
            c = nc - 1 - k
            r0 = pl.multiple_of(c * CHUNK, CHUNK)
            rows = pl.ds(r0, CHUNK)
            glast = gb_ref[pl.ds(r0 + CHUNK - 1, 1), :]
            dgl_row = jnp.zeros((1, 128), F32)
            heads = range(GDN_HEADS)
            cols = [slice(h * GDN_DIM, (h + 1) * GDN_DIM) for h in heads]
            S_h = [st_ref[c, h] for h in heads]
            dS_h = [dstate[h] for h in heads]
            do_h = [_b16(do_ref[rows, cols[h]]) for h in heads]
            aqk_h = [_b16(aqk_ref[h, rows, :]) for h in heads]
            vn_h = [_b16(vn_ref[rows, cols[h]]) for h in heads]
            kd_h = [_b16(kd_ref[rows, cols[h]]) for h in heads]
            qd_h = [_b16(qd_ref[rows, cols[h]]) for h in heads]
            w_h = [_b16(w_ref[rows, cols[h]]) for h in heads]
            Sb = [_b16(S_h[h]) for h in heads]
            dSb = [_b16(dS_h[h]) for h in heads]
            dvn_a = [_dot(aqk_h[h], do_h[h], TN) for h in heads]
            dvn_b = [_dot(kd_h[h], dSb[h]) for h in heads]
            daqk_h = [jnp.where(incl, _dot(do_h[h], vn_h[h], NT), 0.0) for h in heads]
            dkd_h = [_dot(vn_h[h], dSb[h], NT) for h in heads]
            dvn_h = [dvn_a[h] + dvn_b[h] for h in heads]
            both = [jnp.concatenate([do_h[h], _b16(dvn_h[h])], axis=0) for h in heads]
            by_state = [_dot(both[h], Sb[h], NT) for h in heads]
            dS_dot = [_dot(jnp.concatenate([qd_h[h], -w_h[h]], axis=0), both[h], TN) for h in heads]
            res = []
            for h in heads:
                egl = jnp.exp(glast[:, LANE_BA + h:LANE_BA + h + 1])
                dgl = egl * jnp.sum(jnp.sum(dS_h[h] * S_h[h], axis=1, keepdims=True), axis=0,
                                    keepdims=True)
                dgl_row = jnp.where(lane == h, dgl, dgl_row)
                res.append((daqk_h[h], by_state[h][:CHUNK], dkd_h[h], -by_state[h][CHUNK:], dvn_h[h],
                            dS_dot[h] + egl * dS_h[h]))
            for h in heads:
                daqk, dqd, dkd, dw, dvn, dS_new = res[h]
                daqk_ref[h, rows, :] = daqk
                dqd_ref[rows, cols[h]] = dqd
                dkd_ref[rows, cols[h]] = dkd
                dw_ref[rows, cols[h]] = dw
                du_ref[rows, cols[h]] = dvn
                dstate[h] = dS_new
            dgl_ref[pl.ds(c, 1), :] = dgl_row
            return 0

        lax.fori_loop(0, nc, chunk, 0)

    wide_in = lambda a: _rows(a, ts)
    wide = _orow(S, (WIDTH,), F32, ts)
    perhead_in = lambda a: (a, (GDN_HEADS, ts, CHUNK), lambda i: (0, i, 0))
    perhead = ((GDN_HEADS, S, CHUNK), F32, (GDN_HEADS, ts, CHUNK), lambda i: (0, i, 0))
    return _tiled("gdn_scan_bwd", body, S // ts,
                  [wide_in(do), wide_in(w), wide_in(qd), wide_in(kd), perhead_in(aqk), wide_in(vn),
                   (states, (nc, GDN_HEADS, GDN_DIM, GDN_DIM), lambda i: (i, 0, 0, 0)), _rows(gb, ts)],
                  [wide, wide, wide, wide, perhead, ((N, 128), F32, (nc, 128), lambda i: (i, 0))],
                  scratch=[pltpu.VMEM((GDN_HEADS, GDN_DIM, GDN_DIM), F32)], reverse=True)


def _gdn_local_bwd(qkv, gb, grow, T, du, dw, dqd, dkd, daqk, dgl, ts):
    S = qkv.shape[0]
    nc = ts // CHUNK

    def body(t, first, ins, outs, scratch):
        (q_ref, k_ref, v_ref, gb_ref, gr_ref, T_ref, du_ref, dw_ref, dqd_ref, dkd_ref,
         daqk_ref, dgl_ref) = ins
        dqkv_ref, dgb_ref = outs
        gbv = gb_ref[...]
        dglv = dgl_ref[...]
        lane = lax.broadcasted_iota(jnp.int32, (ts, 128), 1)
        dG_all = jnp.zeros((ts, 128), F32)
        dbeta_all = jnp.zeros((ts, 128), F32)
        heads = range(GDN_HEADS)
        _, _, eye = _chunk_masks(nc)
        pre = []
        for h in heads:
            lo = h * GDN_DIM
            cols = slice(lo, lo + GDN_DIM)
            r3 = lambda ref: ref[:, cols].reshape(nc, CHUNK, GDN_DIM)
            qh, kh, vh = r3(q_ref), r3(k_ref), r3(v_ref)
            duh, dwh, dqdh, dkdh = r3(du_ref), r3(dw_ref), r3(dqd_ref), r3(dkd_ref)
            Gc = _lane_pick(gbv, LANE_BA + h).reshape(nc, CHUNK, 1)
            beta = _lane_pick(gbv, LANE_BB + h).reshape(nc, CHUNK, 1)
            Gr = gr_ref[h].reshape(nc, 1, CHUNK)
            Th = T_ref[h].reshape(nc, CHUNK, CHUNK)
            daq = daqk_ref[h].reshape(nc, CHUNK, CHUNK)
            local = _chunk_local(qh, kh, vh, Gc, Gr, beta)
            kb, eG = local[3], local[6]
            vb = vh * beta
            kbg = kb * eG
            dvb = _bdot(Th, duh, 1, 1, PREC_UT)
            dkbg = _bdot(Th, dwh, 1, 1, PREC_UT)
            dT = _bdot(duh, vb, 2, 2, PREC_UT) + _bdot(dwh, kbg, 2, 2, PREC_UT)
            pre.append((qh, kh, vh, dqdh, dkdh, beta, Th, daq, local, kbg, dvb, dkbg, dT))
        M1s = [_bdot(pre[h][6], pre[h][12], 1, 1, PREC_UT) for h in heads]
        dAs = [_bdot(M1s[h], pre[h][6], 2, 2, PREC_UT) for h in heads]
        for h in heads:
            lo = h * GDN_DIM
            cols = slice(lo, lo + GDN_DIM)
            qh, kh, vh, dqdh, dkdh, beta, Th, daq, local, kbg, dvb, dkbg, dT = pre[h]
            incl, strict, gamma, kb, P, Qk, eG, edec = local
            dA = jnp.where(strict, -dAs[h], 0.0)
            dP = dA * gamma
            dQ = daq * gamma
            dgam = (dA * P + daq * Qk) * gamma
            dPb, dQb = _b16(dP), _b16(dQ)
            khb, qhb, kbb = _b16(kh), _b16(qh), _b16(kb)
            dq = _bdot(dQb, khb, 2, 1) + dqdh * eG
            dkb = _bdot(dPb, khb, 2, 1) + dkbg * eG
            dk = (_bdot(dQb, qhb, 1, 1) + _bdot(dPb, kbb, 1, 1) + dkdh * edec + dkb * beta)
            dbeta = (jnp.sum(dkb * kh, axis=2, keepdims=True) + jnp.sum(dvb * vh, axis=2, keepdims=True))
            dv = dvb * beta
            col_as_col = jnp.sum(jnp.where(eye, jnp.sum(dgam, axis=1, keepdims=True), 0.0),
                                 axis=2, keepdims=True)
            kd_term = jnp.sum(dkdh * kh * edec, axis=2, keepdims=True)
            dG = (jnp.sum(dgam, axis=2, keepdims=True) - col_as_col
                  + jnp.sum(dqdh * qh * eG, axis=2, keepdims=True)
                  + jnp.sum(dkbg * kbg, axis=2, keepdims=True) - kd_term)
            dgl_h = dglv[:, h:h + 1].reshape(nc, 1, 1) + jnp.sum(kd_term, axis=1, keepdims=True)
            last = lax.broadcasted_iota(jnp.int32, (nc, CHUNK, 1), 1) == CHUNK - 1
            dG = dG + jnp.where(last, dgl_h, 0.0)
            dqkv_ref[:, cols] = dq.reshape(ts, GDN_DIM)
            dqkv_ref[:, WIDTH + lo:WIDTH + lo + GDN_DIM] = dk.reshape(ts, GDN_DIM)
            dqkv_ref[:, 2 * WIDTH + lo:2 * WIDTH + lo + GDN_DIM] = dv.reshape(ts, GDN_DIM)
            dG_all = jnp.where(lane == LANE_BA + h, dG.reshape(ts, 1), dG_all)
            dbeta_all = jnp.where(lane == LANE_BB + h, dbeta.reshape(ts, 1), dbeta_all)
        dg_all = _scan_rows(dG_all, ts, CHUNK, reverse=True)
        dgb_ref[...] = jnp.where(lane < LANE_BB, dg_all, dbeta_all)

    wide_in = lambda a: _rows(a, ts)
    perhead_in = lambda a: (a, (GDN_HEADS, ts, CHUNK), lambda i: (0, i, 0))
    return _tiled("gdn_local_bwd", body, S // ts,
                  [_cols(qkv, ts, WIDTH, 0), _cols(qkv, ts, WIDTH, 1), _cols(qkv, ts, WIDTH, 2),
                   _rows(gb, ts), (grow, (GDN_HEADS, nc, CHUNK), lambda i: (0, i, 0)), perhead_in(T),
                   wide_in(du), wide_in(dw), wide_in(dqd), wide_in(dkd), perhead_in(daqk),
                   (dgl, (nc, 128), lambda i: (i, 0))],
                  [_orow(S, (3 * WIDTH,), F32, ts), _orow(S, (128,), F32, ts)])


def _gdn_prep_bwd(dqkv, dgb, cpre, z, conv_w, a128, dt128, dz, ts):
    S = z.shape[0]
    C3 = 3 * WIDTH
    hb = ts // 8
    n_tiles = S // ts

    def dpre(dq, c):
        y, dsil = _silu_and_grad(c)
        parts = []
        for h in range(GDN_HEADS):
            lo = h * GDN_DIM
            yq = y[:, lo:lo + GDN_DIM]
            rq = _l2_fwd(yq)
            nq = yq * rq
            dn = dq[:, lo:lo + GDN_DIM] * (GDN_DIM ** -0.5)
            parts.append(rq * (dn - nq * jnp.sum(dn * nq, axis=1, keepdims=True)))
        for h in range(GDN_HEADS):
            lo = WIDTH + h * GDN_DIM
            yk = y[:, lo:lo + GDN_DIM]
            rk = _l2_fwd(yk)
            nk = yk * rk
            dn = dq[:, lo:lo + GDN_DIM]
            parts.append(rk * (dn - nk * jnp.sum(dn * nk, axis=1, keepdims=True)))
        parts.append(dq[:, 2 * WIDTH:])
        return jnp.concatenate(parts, axis=1) * dsil

    def body(t, first, ins, outs, scratch):
        (dq_ref, dqn_ref, c_ref, cn_ref, x_ref, xp_ref, zs_ref, dgb_ref, w_ref, a_ref, dt_ref) = ins
        dx_ref, dzs_ref, dw_ref, dad_ref = outs

        @pl.when(first)
        def _():
            dw_ref[...] = jnp.zeros_like(dw_ref)
            dad_ref[...] = jnp.zeros_like(dad_ref)

        dc = dpre(dq_ref[...], c_ref[...])
        dcn = jnp.where(t < n_tiles - 1, dpre(dqn_ref[...], cn_ref[...]), 0.0)
        dce = jnp.concatenate([dc, dcn], axis=0)
        w = w_ref[...]
        dx = w[3:4, :] * dc
        for back in (1, 2, 3):
            dx = dx + w[3 - back:4 - back, :] * pltpu.roll(dce, ts + 8 - back, 0)[:ts, :]
        dx_ref[...] = _b16(dx)
        halo = jnp.where(t > 0, xp_ref[...], 0.0)
        xe = jnp.concatenate([halo, x_ref[...]], axis=0)
        dw_ref[3:4, :] += jnp.sum(dc * xe[8:, :], axis=0, keepdims=True)
        for back in (1, 2, 3):
            dw_ref[3 - back:4 - back, :] += jnp.sum(dc * pltpu.roll(xe, back, 0)[8:, :], axis=0,
                                                     keepdims=True)
        zs = zs_ref[...]
        dgb = dgb_ref[...]
        lane = lax.broadcasted_iota(jnp.int32, zs.shape, 1)
        arg = zs + dt_ref[...]
        nega = -jnp.exp(a_ref[...])
        dba = dgb * nega * _sigmoid(arg)
        beta = _sigmoid(zs)
        dbb = dgb * beta * (1.0 - beta)
        dzs_ref[...] = jnp.where((lane >= LANE_BA) & (lane < LANE_BB), dba,
                                 jnp.where((lane >= LANE_BB) & (lane < LANE_BB + 4), dbb, 0.0))
        dad_ref[0:1, :] += jnp.sum(dgb * nega * _softplus(arg), axis=0, keepdims=True)
        dad_ref[1:2, :] += jnp.sum(dba, axis=0, keepdims=True)

    nxt = lambda i: (jnp.minimum((i + 1) * hb, S // 8 - 1), 0)
    prv = lambda i: (jnp.maximum(i * hb - 1, 0), CB_BQKV)
    return _tiled("gdn_prep_bwd", body, n_tiles,
                  [_rows(dqkv, ts), (dqkv, (8, C3), nxt), _rows(cpre, ts), (cpre, (8, C3), nxt),
                   (z, (ts, C3), lambda i: (i, CB_BQKV)), (z, (8, C3), prv),
                   _cols(z, ts, 128, CB_SMALL), _rows(dgb, ts), _full(conv_w), _full(a128), _full(dt128)],
                  [((S, N_AL), BF16, (ts, C3), lambda i: (i, CB_BQKV)), _orow(S, (128,), F32, ts),
                   _oacc((8, C3), F32), _oacc((8, 128), F32)],
                  fill=(dz, 0))


def _mem_attn_fwd(z, mk, mv, ts):
    S = z.shape[0]

    def body(t, first, ins, outs, scratch):
        q_ref, mk_ref, mv_ref = ins
        (o_ref,) = outs
        heads = range(MEM_HEADS)
        cols = [slice(h * MEM_DIM, (h + 1) * MEM_DIM) for h in heads]
        s = [_dot(_b16(q_ref[:, cols[h]]), _b16(mk_ref[:, cols[h]]), NT) * (MEM_DIM ** -0.5)
             for h in heads]
        p = []
        for h in heads:
            e = jnp.exp(s[h] - jnp.max(s[h], axis=1, keepdims=True))
            p.append(_b16(e / jnp.sum(e, axis=1, keepdims=True)))
        o = [_dot(p[h], _b16(mv_ref[:, cols[h]])) for h in heads]
        for h in heads:
            o_ref[:, cols[h]] = o[h]

    (o,) = _tiled("mem_attn_fwd", body, S // ts, [_cols(z, ts, WIDTH, CB_MQ), _full(mk), _full(mv)],
                  [_orow(S, (WIDTH,), F32, ts)])
    return o


def _mem_attn_bwd(do, z, mk, mv, dz, ts):
    S = z.shape[0]
    M = mk.shape[0]

    def body(t, first, ins, outs, scratch):
        do_ref, q_ref, mk_ref, mv_ref = ins
        dq_ref, dmk_ref, dmv_ref = outs

        @pl.when(first)
        def _():
            dmk_ref[...] = jnp.zeros_like(dmk_ref)
            dmv_ref[...] = jnp.zeros_like(dmv_ref)

        scale = MEM_DIM ** -0.5
        heads = range(MEM_HEADS)
        cols = [slice(h * MEM_DIM, (h + 1) * MEM_DIM) for h in heads]
        qb = [_b16(q_ref[:, cols[h]]) for h in heads]
        kb = [_b16(mk_ref[:, cols[h]]) for h in heads]
        dob = [_b16(do_ref[:, cols[h]]) for h in heads]
        s = [_dot(qb[h], kb[h], NT) * scale for h in heads]
        dp = [_dot(dob[h], _b16(mv_ref[:, cols[h]]), NT) for h in heads]
        p = []
        for h in heads:
            e = jnp.exp(s[h] - jnp.max(s[h], axis=1, keepdims=True))
            p.append(e / jnp.sum(e, axis=1, keepdims=True))
        dsb = [_b16(p[h] * (dp[h] - jnp.sum(dp[h] * p[h], axis=1, keepdims=True)) * scale) for h in heads]
        dmv = [_dot(_b16(p[h]), dob[h], TN) for h in heads]
        dq = [_dot(dsb[h], kb[h]) for h in heads]
        dmk = [_dot(dsb[h], qb[h], TN) for h in heads]
        for h in heads:
            dmv_ref[:, cols[h]] += dmv[h]
            dq_ref[:, cols[h]] = _b16(dq[h])
            dmk_ref[:, cols[h]] += dmk[h]

    return _tiled("mem_attn_bwd", body, S // ts,
                  [_rows(do, ts), _cols(z, ts, WIDTH, CB_MQ), _full(mk), _full(mv)],
                  [((S, N_AL), BF16, (ts, WIDTH), lambda i: (i, CB_MQ)), _oacc((M, WIDTH), F32),
                   _oacc((M, WIDTH), F32)],
                  fill=(dz, 0))


def _head_norm(ob, g):
    xs, rs = [], []
    for h in range(GDN_HEADS):
        o = ob[:, h * GDN_DIM:(h + 1) * GDN_DIM]
        r = lax.rsqrt(jnp.mean(o * o, axis=1, keepdims=True) + EPS)
        xs.append(o * r)
        rs.append(r)
    return xs, rs


def _merge_fwd(x, z, o_a, o_b, o_m, gdn_g, b_merge, wb, wout, ts):
    S, D = x.shape

    def body(t, first, ins, outs, scratch):
        (x_ref, g_ref, oa_ref, az_ref, ob_ref, bz_ref, om_ref, mz_ref, gg_ref, bm_ref, wb_ref,
         wo_ref) = ins
        xo_ref, ya_ref, yb_ref, ym_ref, mg_ref = outs
        ya = oa_ref[...] * _silu_and_grad(az_ref[...])[0]
        xs, _ = _head_norm(ob_ref[...], None)
        nb = jnp.concatenate([xh * gg_ref[...] for xh in xs], axis=1)
        yb = nb * _silu_and_grad(bz_ref[...])[0]
        ym = om_ref[...] * _silu_and_grad(mz_ref[...])[0]
        merged = jnp.zeros((ts, D), F32)
        for n, (y, y_ref) in enumerate(((ya, ya_ref), (yb, yb_ref), (ym, ym_ref))):
            yb16 = _b16(y)
            y_ref[...] = yb16
            gate = _sigmoid(g_ref[:, n * D:(n + 1) * D] + bm_ref[:, n * D:(n + 1) * D])
            merged = merged + gate * _dot(yb16, wb_ref[n])
        mb = _b16(merged)
        mg_ref[...] = mb
        xo_ref[...] = x_ref[...] + _dot(mb, wo_ref[...])

    half = lambda a: _rows(a, ts)
    return _tiled("merge_fwd", body, S // ts,
                  [_rows(x, ts), _cols(z, ts, 3 * D, CB_GATES), half(o_a), _cols(z, ts, WIDTH, CB_AZ),
                   half(o_b), _cols(z, ts, WIDTH, CB_BZ), half(o_m), _cols(z, ts, WIDTH, CB_MZ),
                   _full(gdn_g.reshape(1, GDN_DIM)), _full(b_merge.reshape(1, 3 * D)), _full(wb), _full(wout)],
                  [_orow(S, (D,), F32, ts), _orow(S, (WIDTH,), BF16, ts), _orow(S, (WIDTH,), BF16, ts),
                   _orow(S, (WIDTH,), BF16, ts), _orow(S, (D,), BF16, ts)])


def _merge_bwd(dout, z, o_a, o_b, o_m, ya, yb, ym, gdn_g, b_merge, wb, wout, hsum, ts):
    S, D = dout.shape

    def body(t, first, ins, outs, scratch):
        (do_ref, g_ref, oa_ref, az_ref, ob_ref, bz_ref, om_ref, mz_ref, ya_ref, yb_ref, ym_ref,
         gg_ref, bm_ref, wb_ref, wo_ref, hs_ref) = ins
        (dg_ref, dpa_ref, dpb_ref, dpm_ref, doa_ref, dob_ref, dom_ref, dl_ref, dbm_ref, dgg_ref) = outs
        G3 = 3 * D

        @pl.when(first)
        def _():
            dbm_ref[...] = jnp.zeros_like(dbm_ref)
            dgg_ref[...] = jnp.zeros_like(dgg_ref)

        dmerged = _dot(_b16(do_ref[...]), wo_ref[...], NT)
        dys = []
        for n, (y_ref, dp_ref) in enumerate(((ya_ref, dpa_ref), (yb_ref, dpb_ref), (ym_ref, dpm_ref))):
            sl = slice(n * D, (n + 1) * D)
            gate = _sigmoid(g_ref[:, sl] + bm_ref[:, sl])
            proj = _dot(y_ref[...], wb_ref[n])
            dproj = _b16(gate * dmerged)
            dp_ref[...] = dproj
            dgp = dmerged * proj * gate * (1.0 - gate)
            dg_ref[:, sl] = dgp.astype(dg_ref.dtype)
            dbm_ref[0:1, sl] += jnp.sum(dgp, axis=0, keepdims=True)
            dys.append(_dot(dproj, wb_ref[n], NT))
        dya, dyb, dym = dys
        sa, dsa = _silu_and_grad(az_ref[...])
        oa = oa_ref[...]
        doa = dya * sa
        doa_ref[...] = doa
        dg_ref[:, G3:G3 + WIDTH] = _b16(dya * oa * dsa)
        dl_ref[...] = _dot(hs_ref[...], doa * oa, NT, HIGHEST)
        sm, dsm = _silu_and_grad(mz_ref[...])
        dom_ref[...] = dym * sm
        dg_ref[:, G3 + 2 * WIDTH:G3 + 3 * WIDTH] = _b16(dym * om_ref[...] * dsm)
        sb, dsb = _silu_and_grad(bz_ref[...])
        xs, rs = _head_norm(ob_ref[...], None)
        gg = gg_ref[...]
        dgg = jnp.zeros((1, GDN_DIM), F32)
        for h in range(GDN_HEADS):
            cols = slice(h * GDN_DIM, (h + 1) * GDN_DIM)
            dn = dyb[:, cols] * sb[:, cols]
            dg_ref[:, G3 + WIDTH + h * GDN_DIM:G3 + WIDTH + (h + 1) * GDN_DIM] = _b16(
                dyb[:, cols] * (xs[h] * gg) * dsb[:, cols])
            dgg = dgg + jnp.sum(dn * xs[h], axis=0, keepdims=True)
            dxh = dn * gg
            dob_ref[:, cols] = rs[h] * (dxh - xs[h] * jnp.mean(dxh * xs[h], axis=1, keepdims=True))
        dgg_ref[0:1, :] += dgg

    half = lambda a: _rows(a, ts)
    w512 = lambda dt: _orow(S, (WIDTH,), dt, ts)
    return _tiled("merge_bwd", body, S // ts,
                  [_rows(dout, ts), _cols(z, ts, 3 * D, CB_GATES), half(o_a), _cols(z, ts, WIDTH, CB_AZ),
                   half(o_b), _cols(z, ts, WIDTH, CB_BZ), half(o_m), _cols(z, ts, WIDTH, CB_MZ),
                   half(ya), half(yb), half(ym), _full(gdn_g.reshape(1, GDN_DIM)),
                   _full(b_merge.reshape(1, 3 * D)), _full(wb), _full(wout), _full(hsum)],
                  [((S, N_AL), BF16, (ts, 3 * D + 3 * WIDTH), lambda i: (i, CB_MERGE)),
                   _orow(S, (D,), BF16, ts), _orow(S, (D,), BF16, ts),
                   _orow(S, (D,), BF16, ts), w512(F32), w512(F32), w512(F32),
                   ((128, S), F32, (128, ts), lambda i: (0, i)), _oacc((8, 3 * D), F32),
                   _oacc((8, GDN_DIM), F32)])


def _to_aligned(w):
    parts = [w[..., lo:lo + n] for lo, n, _ in sorted(W_IN_PIECES, key=lambda p: p[2])]
    parts.append(jnp.zeros(w.shape[:-1] + (N_AL - N_IN,), w.dtype))
    return jnp.concatenate(parts, axis=-1)


def _from_aligned(w):
    return jnp.concatenate([w[..., al:al + n] for _, n, al in W_IN_PIECES], axis=-1)


def _lanes128(v, lane0):
    return jnp.pad(v.astype(F32)[None, :], ((0, 0), (lane0, 128 - lane0 - v.shape[0])))


def _tiles(S):
    ts = min(512, S // 2)
    return dict(ts=ts, ts_small=min(256, S // 2), tq=min(512, S // 4), tq_fwd=min(1024, S // 2))


def _layer_fwd(x, mem, p):
    S = x.shape[0]
    tl = _tiles(S)
    ts, tss, tq = tl["ts"], tl["ts_small"], tl["tq"]
    h, rstd = _rms_fwd("norm_fwd", x, p["norm_g"], ts)
    z = _mm("in_proj", h, p["w_in_al"], tm=2048, tn=1664)

    b_fg128 = _lanes128(p["b_fg"], LANE_AF)
    f_hi, f_mid, f_lo = _fox_decay(z, b_fg128, ts)
    aq = z[:, CB_AQ * WIDTH:(CB_AQ + 1) * WIDTH]
    ak = z[:, CB_AK * WIDTH:(CB_AK + 1) * WIDTH]
    av = z[:, CB_AV * WIDTH:(CB_AV + 1) * WIDTH]
    q32 = _heads_major(aq, FOX_HEADS, FOX_DIM)
    kh = _heads_major(ak, FOX_HEADS, FOX_DIM).astype(BF16)
    vh = _heads_major(av, FOX_HEADS, FOX_DIM).astype(BF16)
    piecesT = jnp.stack([f[:FOX_HEADS] for f in (f_hi, f_mid, f_lo)], axis=1)
    pieces = piecesT.transpose(0, 2, 1)
    ones3 = jnp.ones((FOX_HEADS, S, 3), BF16)
    padk = jnp.zeros((FOX_HEADS, S, FOX_AUG - FOX_DIM - 6), BF16)
    q_aug = jnp.concatenate([q32, pieces.astype(F32), ones3.astype(F32), padk.astype(F32)], axis=-1)
    k_aug = jnp.concatenate([kh, ones3, -pieces, padk], axis=-1)
    kT_aug = jnp.concatenate([kh.transpose(0, 2, 1), ones3.transpose(0, 2, 1), -piecesT,
                              padk.transpose(0, 2, 1)], axis=1)
    v_aug = jnp.concatenate([vh, ones3[:, :, :1], jnp.zeros((FOX_HEADS, S, 128 - FOX_DIM - 1), BF16)],
                            axis=-1)
    o_h, lse, qs = _fox_fwd(q_aug, kT_aug, v_aug, tl["tq_fwd"])
    o_a = _heads_minor(o_h)

    a128 = _lanes128(p["a_log"], LANE_BA)
    dt128 = _lanes128(p["dt_bias"], LANE_BA)
    qkv, cpre, gb, gbT = _gdn_prep(z, p["conv_w"], a128, dt128, ts)
    grow = gbT[LANE_BA:LANE_BA + GDN_HEADS].reshape(GDN_HEADS, S // CHUNK, CHUNK)
    u, w, qd, kd, aqk, T = _gdn_local_fwd(qkv, gb, grow, ts)
    o_b, vn, states = _gdn_scan_fwd(u, w, qd, kd, aqk, gb, ts)

    mem_h, mem_r = _rms_fwd("mem_norm_fwd", mem, p["mem_norm_g"], mem.shape[0])
    mkv = _mm("mem_kv", mem_h, p["w_mem_kv"])
    mk, mv = mkv[:, :WIDTH], mkv[:, WIDTH:]
    o_m = _mem_attn_fwd(z, mk, mv, ts)

    x_next, ya, yb, ym, merged = _merge_fwd(x, z, o_a, o_b, o_m, p["gdn_norm_g"], p["b_merge"],
                                            p["w_branch"], p["w_out"], ts)
    saved = dict(x=x, h=h, rstd=rstd, z=z, b_fg128=b_fg128, qs=qs, k_aug=k_aug, kT_aug=kT_aug, v_aug=v_aug, lse=lse, o_a=o_a, a128=a128, dt128=dt128, qkv=qkv, cpre=cpre, gb=gb,
                 grow=grow, w=w, qd=qd, kd=kd, aqk=aqk, T=T, o_b=o_b, vn=vn, states=states,
                 mem_h=mem_h, mem_r=mem_r, mk=mk, mv=mv, o_m=o_m, ya=ya, yb=yb, ym=ym, merged=merged)
    return x_next, saved


def _layer_bwd(dout, mem, p, s):
    S = dout.shape[0]
    tl = _tiles(S)
    ts, tss, tq = tl["ts"], tl["ts_small"], tl["tq"]
    z = s["z"]
    hsum = (jnp.arange(128)[:, None] == jnp.arange(WIDTH)[None, :] // FOX_DIM).astype(F32)
    (dz, dpa, dpb, dpm, do_a, do_b, do_m, deltaT, db_merge, dgdn_g) = _merge_bwd(
        dout, z, s["o_a"], s["o_b"], s["o_m"], s["ya"], s["yb"], s["ym"], p["gdn_norm_g"],
        p["b_merge"], p["w_branch"], p["w_out"], hsum, tss)
    g = {}
    g["b_merge"] = db_merge[0]
    g["gdn_norm_g"] = dgdn_g[0]
    g["w_out"] = _mm("dw_out", s["merged"], dout, ta=True)
    g["w_branch"] = jnp.stack([_mm("dw_branch", y, dp, ta=True)
                               for y, dp in ((s["ya"], dpa), (s["yb"], dpb), (s["ym"], dpm))])

    do_h = _heads_major(do_a, FOX_HEADS, FOX_DIM).astype(BF16)
    delta_row = deltaT[:FOX_HEADS, None, :]
    dqT, dk_h, dv_h, dfk, dfq = _fox_bwd(s["qs"], s["k_aug"], s["kT_aug"], s["v_aug"], do_h, s["lse"],
                                         delta_row, tq)
    daq = _heads_minor(dqT.transpose(0, 2, 1))
    dak = _heads_minor(dk_h)
    dav = _heads_minor(dv_h)
    daf128, db_fg = _fox_decay_bwd(dfk[:, 0, :], dfq[:, 0, :], z, s["b_fg128"], ts)
    g["b_fg"] = db_fg[:FOX_HEADS]

    du, dw, dqd, dkd, daqk, dgl = _gdn_scan_bwd(do_b, s["w"], s["qd"], s["kd"], s["aqk"], s["vn"],
                                                s["states"], s["gb"], ts)
    dqkv, dgb = _gdn_local_bwd(s["qkv"], s["gb"], s["grow"], s["T"], du, dw, dqd, dkd, daqk, dgl, ts)
    dz, dzs_b, dconv, dad = _gdn_prep_bwd(dqkv, dgb, s["cpre"], z, p["conv_w"], s["a128"],
                                          s["dt128"], dz, ts)
    g["conv_w"] = dconv[:4]
    g["a_log"] = dad[0, LANE_BA:LANE_BA + GDN_HEADS]
    g["dt_bias"] = dad[1, LANE_BA:LANE_BA + GDN_HEADS]

    dz, dmk, dmv = _mem_attn_bwd(do_m, z, s["mk"], s["mv"], dz, ts)
    dmkv = jnp.concatenate([dmk, dmv], axis=1)
    g["w_mem_kv"] = _mm("dw_mem_kv", s["mem_h"], dmkv, ta=True)
    dmem_h = _mm("dmem_h", dmkv, p["w_mem_kv"], tb=True)
    M = mem.shape[0]
    _, g["mem_norm_g"] = _rms_bwd("mem_norm_bwd", dmem_h, mem, s["mem_r"], p["mem_norm_g"],
                                  jnp.zeros_like(mem), M)

    lane = jnp.arange(128)[None, :]
    dsmall = jnp.where(lane < 8, daf128, dzs_b)
    daqkv = jnp.concatenate([_b16(daq), _b16(dak), _b16(dav)], axis=1)
    dz = lax.dynamic_update_slice(dz, daqkv, (0, CB_AQKV * 3 * WIDTH))
    dz = lax.dynamic_update_slice(dz, _b16(dsmall), (0, CB_SMALL * 128))
    g["w_in_al"] = _mm("dw_in", s["h"], dz, ta=True, tn=1664, tk=2048)
    dh = _mm("dh", dz, p["w_in_al"], tb=True, tm=2048, tk=1664)
    dx, g["norm_g"] = _rms_bwd("norm_bwd", dh, s["x"], s["rstd"], p["norm_g"], dout, ts)
    return dx, g


def _local_step(x, mem, layers, final_norm_g, loss_target):
    S = x.shape[0]
    saves = []
    cur = x
    for p in layers:
        cur, sv = _layer_fwd(cur, mem, p)
        saves.append(sv)
    dx, dgf, loss_lanes = _loss_head(cur, final_norm_g, loss_target, _tiles(S)["ts"])
    grads = [None] * len(layers)
    for l in reversed(range(len(layers))):
        dx, grads[l] = _layer_bwd(dx, mem, layers[l], saves[l])
    return loss_lanes, dx, grads, dgf


HBM_SPEC = pl.BlockSpec(memory_space=pltpu.HBM)


def _mesh_pos():
    return lax.axis_index("x"), lax.axis_index("y"), lax.axis_index("c")


def _comm_call(name, body, arrays, out_shapes, n_remote, n_local):
    n = len(arrays)

    def kern(*refs):
        body(refs[:n], refs[n:2 * n], refs[2 * n], refs[2 * n + 1], refs[2 * n + 2])

    return pl.pallas_call(
        kern, name=name, out_shape=out_shapes, in_specs=[HBM_SPEC] * n, out_specs=[HBM_SPEC] * n,
        scratch_shapes=[pltpu.SemaphoreType.DMA((n_remote,)), pltpu.SemaphoreType.DMA((n_remote,)),
                        pltpu.SemaphoreType.DMA((max(n_local, 1),))],
    )(*arrays)


def _remote(src, dst, send_sems, recv_sems, k, to):
    return pltpu.make_async_remote_copy(src_ref=src, dst_ref=dst, send_sem=send_sems.at[k],
                                        recv_sem=recv_sems.at[k], device_id=to, device_id_type=MESH_ID)


def _other_chips(mx, my):
    return [(1 - mx, my), (mx, 1 - my), (1 - mx, 1 - my)]


def _gather_chips(name, shards):
    n = len(shards)

    def body(ins, outs, send_sems, recv_sems, local_sems):
        mx, my, mc = _mesh_pos()
        me = 2 * mx + my
        sibling = (mx, my, 1 - mc)
        chips = _other_chips(mx, my)
        sends = []
        for a in range(n):
            for k, (px, py) in enumerate(chips):
                cp = _remote(ins[a].at[mc], outs[a].at[me, mc], send_sems, recv_sems, 6 * a + k,
                             (px, py, mc))
                cp.start()
                sends.append(cp)
        for a in range(n):
            for k, (px, py) in enumerate(chips):
                j = 2 * px + py
                _remote(ins[a].at[mc], outs[a].at[j, mc], send_sems, recv_sems, 6 * a + k,
                        (px, py, mc)).wait_recv()
                cp = _remote(outs[a].at[j, mc], outs[a].at[j, mc], send_sems, recv_sems, 6 * a + 3 + k,
                             sibling)
                cp.start()
                sends.append(cp)
        for a in range(n):
            for k, (px, py) in enumerate(chips):
                j = 2 * px + py
                _remote(outs[a].at[j, 1 - mc], outs[a].at[j, 1 - mc], send_sems, recv_sems,
                        6 * a + 3 + k, sibling).wait_recv()
        for cp in sends:
            cp.wait_send()

    shapes = [jax.ShapeDtypeStruct((N_CHIPS,) + s.shape, s.dtype) for s in shards]
    outs = _comm_call(name, body, shards, shapes, 6 * n, 0)
    me = 2 * lax.axis_index("x") + lax.axis_index("y")
    return [lax.dynamic_update_index_in_dim(o, s, me, 0) for o, s in zip(outs, shards)]


def _sibling_swap(gs):
    n = len(gs)

    def body(ins, outs, send_sems, recv_sems, local_sems):
        mx, my, mc = _mesh_pos()
        sends = []
        for a in range(n):
            cp = _remote(ins[a].at[:, 1 - mc], outs[a], send_sems, recv_sems, a, (mx, my, 1 - mc))
            cp.start()
            sends.append(cp)
        for cp in sends:
            cp.wait()

    shapes = [jax.ShapeDtypeStruct((g.shape[0],) + g.shape[2:], g.dtype) for g in gs]
    return _comm_call("grad_sibling_swap", body, gs, shapes, n, 0)


def _chip_exchange(ps):
    n = len(ps)

    def body(ins, outs, send_sems, recv_sems, local_sems):
        mx, my, mc = _mesh_pos()
        me = 2 * mx + my
        chips = _other_chips(mx, my)
        sends = []
        for a in range(n):
            for k, (px, py) in enumerate(chips):
                cp = _remote(ins[a].at[2 * px + py], outs[a].at[me], send_sems, recv_sems, 3 * a + k,
                             (px, py, mc))
                cp.start()
                sends.append(cp)
        for a in range(n):
            for k, (px, py) in enumerate(chips):
                _remote(ins[a].at[me], outs[a].at[2 * px + py], send_sems, recv_sems, 3 * a + k,
                        (px, py, mc)).wait_recv()
        for cp in sends:
            cp.wait_send()

    shapes = [jax.ShapeDtypeStruct(p.shape, p.dtype) for p in ps]
    outs = _comm_call("grad_chip_exchange", body, ps, shapes, 3 * n, 0)
    me = 2 * lax.axis_index("x") + lax.axis_index("y")
    return [lax.dynamic_update_index_in_dim(o, lax.dynamic_index_in_dim(p, me, 0, keepdims=False), me, 0)
            for o, p in zip(outs, ps)]


def _sibling_gather(hs):
    n = len(hs)

    def body(ins, outs, send_sems, recv_sems, local_sems):
        mx, my, mc = _mesh_pos()
        sends = []
        for a in range(n):
            cp = _remote(ins[a], outs[a], send_sems, recv_sems, a, (mx, my, 1 - mc))
            cp.start()
            sends.append(cp)
        for cp in sends:
            cp.wait()

    shapes = [jax.ShapeDtypeStruct(h.shape, h.dtype) for h in hs]
    theirs = _comm_call("grad_sibling_gather", body, hs, shapes, n, 0)
    first = lax.axis_index("c") == 0
    return [jnp.stack([jnp.where(first, h, t), jnp.where(first, t, h)]) for h, t in zip(hs, theirs)]


def _add_pairs(a, b, tr, out_dtype):
    n, H, C = a.shape

    def kern(a_ref, b_ref, o_ref):
        o_ref[...] = (a_ref[...] + b_ref[...]).astype(o_ref.dtype)

    spec = pl.BlockSpec((None, tr, C), lambda j, i: (j, i, 0))
    return pl.pallas_call(
        kern, name="grad_pair_sum", grid=(n, H // tr), in_specs=[spec, spec], out_specs=spec,
        out_shape=jax.ShapeDtypeStruct((n, H, C), out_dtype),
        compiler_params=_params(("parallel", "parallel")),
    )(a, b)


def _sum_slots(r4, tr):
    n, H, C = r4.shape

    def kern(r_ref, o_ref):
        f = lambda k: r_ref[k].astype(F32)
        o_ref[...] = ((f(0) + f(1)) + f(2)) + f(3)

    return pl.pallas_call(
        kern, name="grad_chip_sum", grid=(H // tr,),
        in_specs=[pl.BlockSpec((n, tr, C), lambda i: (0, i, 0))],
        out_specs=pl.BlockSpec((tr, C), lambda i: (i, 0)),
        out_shape=jax.ShapeDtypeStruct((H, C), F32),
        compiler_params=_params(("parallel",)),
    )(r4)


def _adamw(w, g, m, v, tr):
    R, C = w.shape
    c1 = 1.0 - ADAM_B1
    c2 = 1.0 - ADAM_B2
    bc1 = 1.0 - ADAM_B1 ** ADAM_STEP
    bc2 = 1.0 - ADAM_B2 ** ADAM_STEP

    def kern(w_ref, g_ref, m_ref, v_ref, d_ref, mo_ref, vo_ref):
        gv = g_ref[...]
        mn = ADAM_B1 * m_ref[...] + c1 * gv
        vn = ADAM_B2 * v_ref[...] + c2 * (gv * gv)
        m_hat = mn / bc1
        v_hat = vn / bc2
        d_ref[...] = -ADAM_LR * (m_hat / (jnp.sqrt(v_hat) + ADAM_EPS) + ADAM_WD * w_ref[...])
        mo_ref[...] = mn
        vo_ref[...] = vn

    spec = pl.BlockSpec((tr, C), lambda i: (i, 0))
    shape = jax.ShapeDtypeStruct((R, C), F32)
    return pl.pallas_call(
        kern, name="adamw", grid=(R // tr,), in_specs=[spec] * 4, out_specs=[spec] * 3,
        out_shape=[shape] * 3, compiler_params=_params(("parallel",)),
    )(w, g, m, v)


PACK_COLS = 1024
PACK_ROWS = 512
W_SHARD = N_IN // N_CHIPS
SLAB = ("conv_w", "w_mem_kv", "w_branch", "w_out")
SMALL =("norm_g", "b_fg", "b_merge", "a_log", "dt_bias", "gdn_norm_g", "mem_norm_g", "final_norm_g")
ALL_WEIGHTS = ("norm_g", "w_in", "b_fg", "b_merge", "conv_w", "a_log", "dt_bias", "gdn_norm_g",
               "mem_norm_g", "w_mem_kv", "w_branch", "w_out", "final_norm_g")
SHARD_AXIS = {"w_in": 2, "conv_w": 2, "w_mem_kv": 1, "w_branch": 3, "w_out": 1}


def _pack(arrays, row_multiple):
    flat = jnp.concatenate([a.reshape(-1) for a in arrays])
    n = flat.shape[0]
    rows = -(-n // PACK_COLS)
    rows = -(-rows // row_multiple) * row_multiple
    flat = jnp.pad(flat, (0, rows * PACK_COLS - n))
    return flat.reshape(rows, PACK_COLS)


def _unpack(slab, shapes):
    out, off = [], 0
    for shp in shapes:
        n = 1
        for d in shp:
            n *= d
        r0, r1 = off // PACK_COLS, -(-(off + n) // PACK_COLS)
        rows = slab[r0:r1].reshape(-1)
        out.append(rows[off - r0 * PACK_COLS:off - r0 * PACK_COLS + n].reshape(shp))
        off += n
    return out


def _shard_of(full, name, j):
    ax = SHARD_AXIS[name]
    n = full.shape[ax] // N_CHIPS
    return lax.slice_in_dim(full, j * n, (j + 1) * n, axis=ax)


def _aligned_from_shards(shards):
    def cols(lo, n):
        parts = []
        while n > 0:
            j, off = divmod(lo, W_SHARD)
            take = min(n, W_SHARD - off)
            parts.append(shards[j][..., off:off + take])
            lo, n = lo + take, n - take
        return parts

    out = []
    for lo, n, _ in sorted(W_IN_PIECES, key=lambda p: p[2]):
        out += cols(lo, n)
    out.append(jnp.zeros(shards[0].shape[:-1] + (N_AL - N_IN,), shards[0].dtype))
    return jnp.concatenate(out, axis=-1)


def _shard_from_aligned(w_al, j):
    lo_j, hi_j = j * W_SHARD, (j + 1) * W_SHARD
    parts = []
    for lo, n, al in W_IN_PIECES:
        a, b = max(lo, lo_j), min(lo + n, hi_j)
        if a < b:
            parts.append(w_al[..., al + a - lo:al + b - lo])
    return jnp.concatenate(parts, axis=-1)


def kernel(x, mem, norm_g, w_in, b_fg, b_merge, conv_w, a_log, dt_bias, gdn_norm_g, mem_norm_g, w_mem_kv, w_branch, w_out, final_norm_g, loss_target, m_norm_g, m_w_in, m_b_fg, m_b_merge, m_conv_w, m_a_log, m_dt_bias, m_gdn_norm_g, m_mem_norm_g, m_w_mem_kv, m_w_branch, m_w_out, m_final_norm_g, v_norm_g, v_w_in, v_b_fg, v_b_merge, v_conv_w, v_a_log, v_dt_bias, v_gdn_norm_g, v_mem_norm_g, v_w_mem_kv, v_w_branch, v_w_out, v_final_norm_g):
    wts = dict(norm_g=norm_g, w_in=w_in, b_fg=b_fg, b_merge=b_merge, conv_w=conv_w, a_log=a_log,
               dt_bias=dt_bias, gdn_norm_g=gdn_norm_g, mem_norm_g=mem_norm_g, w_mem_kv=w_mem_kv,
               w_branch=w_branch, w_out=w_out, final_norm_g=final_norm_g)
    mom = dict(norm_g=m_norm_g, w_in=m_w_in, b_fg=m_b_fg, b_merge=m_b_merge, conv_w=m_conv_w,
               a_log=m_a_log, dt_bias=m_dt_bias, gdn_norm_g=m_gdn_norm_g, mem_norm_g=m_mem_norm_g,
               w_mem_kv=m_w_mem_kv, w_branch=m_w_branch, w_out=m_w_out, final_norm_g=m_final_norm_g)
    vel = dict(norm_g=v_norm_g, w_in=v_w_in, b_fg=v_b_fg, b_merge=v_b_merge, conv_w=v_conv_w,
               a_log=v_a_log, dt_bias=v_dt_bias, gdn_norm_g=v_gdn_norm_g, mem_norm_g=v_mem_norm_g,
               w_mem_kv=v_w_mem_kv, w_branch=v_w_branch, w_out=v_w_out, final_norm_g=v_final_norm_g)

    big = ("w_in", "w_mem_kv", "w_branch", "w_out")
    gathered = _gather_chips("weight_gather", [wts[n].astype(BF16) for n in big] + [conv_w])
    all_w = dict(zip(big + ("conv_w",), gathered))
    w_in_al = _aligned_from_shards([all_w["w_in"][j] for j in range(N_CHIPS)])

    layers = []
    for l in range(DEPTH):
        rows_of = lambda n: all_w[n][:, l].reshape(D_MODEL, D_MODEL)
        last_of = lambda n: jnp.concatenate([all_w[n][j, l] for j in range(N_CHIPS)], axis=-1)
        layers.append(dict(norm_g=norm_g[l], w_in_al=w_in_al[l], b_fg=b_fg[l], b_merge=b_merge[l],
                           conv_w=jnp.pad(last_of("conv_w"), ((0, 4), (0, 0))), a_log=a_log[l],
                           dt_bias=dt_bias[l], gdn_norm_g=gdn_norm_g[l], mem_norm_g=mem_norm_g[l],
                           w_mem_kv=rows_of("w_mem_kv"), w_branch=last_of("w_branch"),
                           w_out=rows_of("w_out")))

    loss_lanes, dx, grads, dgf = _local_step(x[0], mem[0], layers, final_norm_g, loss_target[0])

    gfull = {n: jnp.stack([grads[l][n] for l in range(DEPTH)])
             for n in ("norm_g", "b_fg", "b_merge", "conv_w", "a_log", "dt_bias", "gdn_norm_g",
                       "mem_norm_g", "w_mem_kv", "w_branch", "w_out")}
    gfull["final_norm_g"] = dgf
    loss_local = jnp.sum(loss_lanes).reshape(1)
    small_g = [gfull[n] for n in SMALL] + [loss_local]
    dw_al = jnp.stack([grads[l]["w_in_al"] for l in range(DEPTH)])
    ga = jnp.stack([_shard_from_aligned(dw_al, j) for j in range(N_CHIPS)])
    mats = ("w_mem_kv", "w_branch", "w_out")
    rest = ("conv_w",) + SMALL
    gb = jnp.stack([_pack([_shard_of(gfull[n], n, j) for n in mats], PACK_ROWS)
                    for j in range(N_CHIPS)])
    gc = jnp.stack([_pack([_shard_of(gfull["conv_w"], "conv_w", j)] + small_g, 16)
                    for j in range(N_CHIPS)])
    halves = lambda g: g.reshape(N_CHIPS, 2, g.shape[1] // 2, PACK_COLS)
    gb, gc = halves(gb), halves(gc)

    mc = lax.axis_index("c")
    tr = 256
    trs = (tr, tr, 8)
    from_sibling = _sibling_swap([ga, gb, gc])
    mine = [lax.dynamic_index_in_dim(g, mc, axis=1, keepdims=False) for g in (ga, gb, gc)]
    pair = [_add_pairs(a, b, t, dt) for a, b, t, dt in zip(mine, from_sibling, trs, (BF16, BF16, F32))]
    slots = _chip_exchange(pair)
    half = [_sum_slots(s, t) for s, t in zip(slots, trs)]
    ga_sum, gb_sum, gc_sum = _sibling_gather(half)
    flat = lambda g: g.reshape(-1, PACK_COLS)

    g_un = dict(zip(mats, _unpack(flat(gb_sum), [wts[n].shape for n in mats])))
    g_un.update(zip(rest + ("loss",), _unpack(flat(gc_sum), [wts[n].shape for n in rest] + [(1,)])))
    g_un["w_in"] = ga_sum
    d_un, m_un, v_un = {}, {}, {}
    rows2d = lambda a: a.reshape(-1, a.shape[-1])
    for n in ("w_in", "w_mem_kv", "w_branch", "w_out"):
        res = _adamw(rows2d(wts[n]), rows2d(g_un[n]), rows2d(mom[n]), rows2d(vel[n]), tr)
        d_un[n], m_un[n], v_un[n] = [r.reshape(wts[n].shape) for r in res]
    little = ("conv_w",) + SMALL
    slab = lambda d: _pack([d[n] for n in little], 8)
    res = _adamw(slab(wts), slab(g_un), slab(mom), slab(vel), 8)
    little_shapes = [wts[n].shape for n in little]
    for out, r in zip((d_un, m_un, v_un), res):
        out.update(zip(little, _unpack(r, little_shapes)))

    loss = g_un["loss"][0]
    return (loss, dx[None], *[g_un[n] for n in ALL_WEIGHTS], *[d_un[n] for n in ALL_WEIGHTS],
            *[m_un[n] for n in ALL_WEIGHTS], *[v_un[n] for n in ALL_WEIGHTS])
```

```python
import functools

import jax
import jax.numpy as jnp
from jax import lax
from jax.experimental import pallas as pl
from jax.experimental.pallas import tpu as pltpu

F32 = jnp.float32
BF16 = jnp.bfloat16
HIGHEST = lax.Precision.HIGHEST
PREC_UT = lax.Precision.HIGH
MESH_ID = pl.DeviceIdType.MESH

D_MODEL = 1024
DEPTH = 2
CHUNK = 64
EPS = 1e-6
FOX_HEADS, FOX_DIM = 8, 64
GDN_HEADS, GDN_DIM = 4, 128
MEM_HEADS, MEM_DIM = 4, 128
WIDTH = 512
N_BRANCH = 3
N_IN = 8208
N_AL = 8320
N_CHIPS = 4
NEG = -1e30
LOG2E = 1.4426950408889634
LN2 = 0.6931471805599453

ADAM_LR, ADAM_B1, ADAM_B2, ADAM_EPS, ADAM_WD, ADAM_STEP = 0.001, 0.9, 0.999, 1e-08, 0.01, 10

CB_GATES = 0
CB_AZ, CB_BZ, CB_MZ = 6, 7, 8
CB_MERGE = 0
CB_BQKV = 3
CB_AQ, CB_AK, CB_AV = 12, 13, 14
CB_AQKV = 4
CB_MQ = 15
CB_SMALL = 64
W_IN_PIECES = ((0, 512, 6144), (512, 512, 6656), (1024, 512, 7168), (1536, 8, 8192), (1544, 512, 3072),
               (2056, 512, 4608), (2568, 512, 5120), (3080, 512, 5632), (3592, 4, 8200), (3596, 4, 8204),
               (3600, 512, 3584), (4112, 512, 7680), (4624, 512, 4096), (5136, 3072, 0))
LANE_AF, LANE_BA, LANE_BB = 0, 8, 12

NN = ((1,), (0,))
NT = ((1,), (1,))
TN = ((0,), (0,))

VMEM_LIMIT_BYTES = 56 * 1024 * 1024


def _dot(a, b, dims=NN, prec=None):
    return lax.dot_general(a, b, (dims, ((), ())), preferred_element_type=F32, precision=prec)


def _bdot(a, b, ca, cb, prec=None):
    return lax.dot_general(a, b, (((ca,), (cb,)), ((0,), (0,))), preferred_element_type=F32,
                           precision=prec)


def _b16(a):
    return a.astype(BF16)


def _eye(n, dtype=F32):
    r = lax.broadcasted_iota(jnp.int32, (n, n), 0)
    c = lax.broadcasted_iota(jnp.int32, (n, n), 1)
    return jnp.where(r == c, 1.0, 0.0).astype(dtype)


def _transpose_exact(x):
    return _dot(_eye(x.shape[1]), x, NT, HIGHEST)


def _col_to_row(col):
    n = col.shape[0]
    return jnp.sum(jnp.where(_eye(n) > 0.5, col, 0.0), axis=0, keepdims=True)


def _row_to_col(row):
    n = row.shape[1]
    return jnp.sum(jnp.where(_eye(n) > 0.5, row, 0.0), axis=1, keepdims=True)


def _sigmoid(x):
    return 1.0 / (1.0 + jnp.exp(-x))


def _softplus(x):
    return jnp.maximum(x, 0.0) + jnp.log(1.0 + jnp.exp(-jnp.abs(x)))


def _silu_and_grad(x):
    s = _sigmoid(x)
    return x * s, s * (1.0 + x * (1.0 - s))


def _params(semantics):
    return pltpu.CompilerParams(dimension_semantics=semantics, vmem_limit_bytes=VMEM_LIMIT_BYTES)


def _rows(a, ts):
    nd = a.ndim
    return (a, (ts,) + a.shape[1:], lambda i, nd=nd: (i,) + (0,) * (nd - 1))


def _cols(a, ts, width, cb):
    return (a, (ts, width), lambda i, cb=cb: (i, cb))


def _full(a):
    nd = a.ndim
    return (a, a.shape, lambda i, nd=nd: (0,) * nd)


def _orow(S, tail, dtype, ts):
    nd = 1 + len(tail)
    return ((S,) + tuple(tail), dtype, (ts,) + tuple(tail), lambda i, nd=nd: (i,) + (0,) * (nd - 1))


def _oacc(shape, dtype):
    nd = len(shape)
    return (tuple(shape), dtype, tuple(shape), lambda i, nd=nd: (0,) * nd)


def _tiled(name, body, n_steps, ins, outs, scratch=(), reverse=False, fill=None):
    def rev(imap):
        if not reverse:
            return imap
        return lambda i: imap(n_steps - 1 - i)

    in_specs = [pl.BlockSpec(blk, rev(imap)) for (_, blk, imap) in ins]
    out_specs = [pl.BlockSpec(blk, rev(imap)) for (_, _, blk, imap) in outs]
    out_shape = [jax.ShapeDtypeStruct(shape, dt) for (shape, dt, _, _) in outs]
    n_in, n_out = len(ins), len(outs)
    arrays = [a for (a, _, _) in ins]
    aliases = {}
    n_extra = 0
    if fill is not None:
        arrays.append(fill[0])
        in_specs.append(pl.BlockSpec(memory_space=pl.ANY))
        aliases = {n_in: fill[1]}
        n_extra = 1

    def kern(*refs):
        step = pl.program_id(0)
        t = (n_steps - 1 - step) if reverse else step
        lo = n_in + n_extra
        body(t, step == 0, refs[:n_in], refs[lo:lo + n_out], refs[lo + n_out:])

    res = pl.pallas_call(
        kern, name=name, grid=(n_steps,), in_specs=in_specs, out_specs=out_specs,
        out_shape=out_shape, scratch_shapes=list(scratch), input_output_aliases=aliases,
        compiler_params=_params(("arbitrary",)),
    )(*arrays)
    return res


def _pick(n, pref):
    if n <= pref:
        return n
    best = None
    for t in range(128, pref + 1, 128):
        if n % t == 0:
            best = t
    assert best is not None, (n, pref)
    return best


def _mm(name, a, b, ta=False, tb=False, out_dtype=F32, tm=1024, tn=1024, tk=1024):
    if ta:
        K, M = a.shape
    else:
        M, K = a.shape
    if tb:
        N, K2 = b.shape
    else:
        K2, N = b.shape
    assert K == K2, (a.shape, b.shape, ta, tb)
    tm, tn, tk = _pick(M, tm), _pick(N, tn), _pick(K, tk)
    nk = K // tk
    a_spec = (pl.BlockSpec((tk, tm), lambda i, j, k: (k, i)) if ta
              else pl.BlockSpec((tm, tk), lambda i, j, k: (i, k)))
    b_spec = (pl.BlockSpec((tn, tk), lambda i, j, k: (j, k)) if tb
              else pl.BlockSpec((tk, tn), lambda i, j, k: (k, j)))
    dims = ((0,) if ta else (1,), (1,) if tb else (0,))

    def kern_single(a_ref, b_ref, o_ref):
        o_ref[...] = _dot(_b16(a_ref[...]), _b16(b_ref[...]), dims).astype(o_ref.dtype)

    def kern_acc(a_ref, b_ref, o_ref, acc_ref):
        k = pl.program_id(2)

        @pl.when(k == 0)
        def _():
            acc_ref[...] = jnp.zeros_like(acc_ref)

        acc_ref[...] += _dot(_b16(a_ref[...]), _b16(b_ref[...]), dims)

        @pl.when(k == nk - 1)
        def _():
            o_ref[...] = acc_ref[...].astype(o_ref.dtype)

    return pl.pallas_call(
        kern_single if nk == 1 else kern_acc, name=name, grid=(M // tm, N // tn, nk),
        in_specs=[a_spec, b_spec],
        out_specs=pl.BlockSpec((tm, tn), lambda i, j, k: (i, j)),
        out_shape=jax.ShapeDtypeStruct((M, N), out_dtype),
        scratch_shapes=[] if nk == 1 else [pltpu.VMEM((tm, tn), F32)],
        compiler_params=_params(("parallel", "parallel", "arbitrary")),
    )(a, b)


def _rms_fwd(name, x, g, ts):
    S, D = x.shape

    def body(t, first, ins, outs, scratch):
        x_ref, g_ref = ins
        h_ref, r_ref = outs
        xv = x_ref[...]
        r = lax.rsqrt(jnp.mean(xv * xv, axis=1, keepdims=True) + EPS)
        h_ref[...] = (xv * r * g_ref[...]).astype(h_ref.dtype)
        r_ref[...] = r

    return _tiled(name, body, S // ts, [_rows(x, ts), _full(g.reshape(1, D))],
                  [_orow(S, (D,), BF16, ts), _orow(S, (1,), F32, ts)])


def _rms_bwd(name, dh, x, rstd, g, dres, ts):
    S, D = x.shape

    def body(t, first, ins, outs, scratch):
        dh_ref, x_ref, r_ref, g_ref, dres_ref = ins
        dx_ref, dg_ref = outs
        r = r_ref[...]
        xh = x_ref[...] * r
        dhv = dh_ref[...]
        dxh = dhv * g_ref[...]
        dx_ref[...] = dres_ref[...] + r * (dxh - xh * jnp.mean(dxh * xh, axis=1, keepdims=True))

        @pl.when(first)
        def _():
            dg_ref[...] = jnp.zeros_like(dg_ref)

        dg_ref[0:1, :] += jnp.sum(dhv * xh, axis=0, keepdims=True)

    dx, dg = _tiled(name, body, S // ts,
                    [_rows(dh, ts), _rows(x, ts), _rows(rstd, ts), _full(g.reshape(1, D)), _rows(dres, ts)],
                    [_orow(S, (D,), F32, ts), _oacc((8, D), F32)])
    return dx, dg[0]


def _loss_head(x, g, target, ts):
    S, D = x.shape

    def body(t, first, ins, outs, scratch):
        x_ref, g_ref, tgt_ref = ins
        dx_ref, dg_ref, loss_ref = outs
        xv = x_ref[...]
        gv = g_ref[...]
        r = lax.rsqrt(jnp.mean(xv * xv, axis=1, keepdims=True) + EPS)
        xh = xv * r
        err = xh * gv - tgt_ref[...]
        dy = err * (1.0 / D)
        dxh = dy * gv
        dx_ref[...] = r * (dxh - xh * jnp.mean(dxh * xh, axis=1, keepdims=True))

        @pl.when(first)
        def _():
            dg_ref[...] = jnp.zeros_like(dg_ref)
            loss_ref[...] = jnp.zeros_like(loss_ref)

        dg_ref[0:1, :] += jnp.sum(dy * xh, axis=0, keepdims=True)
        per_lane = jnp.sum(err * err, axis=0, keepdims=True)
        loss_ref[0:1, :] += per_lane * (0.5 / D)

    dx, dg, loss = _tiled("loss_head", body, S // ts,
                          [_rows(x, ts), _full(g.reshape(1, D)), _rows(target, ts)],
                          [_orow(S, (D,), F32, ts), _oacc((8, D), F32), _oacc((8, D), F32)])
    return dx, dg[0], loss[0]


def _scan_rows(x, length, seg, reverse=False):
    row = lax.broadcasted_iota(jnp.int32, x.shape, 0) % seg
    k = 1
    while k < seg:
        if reverse:
            x = x + jnp.where(row < seg - k, pltpu.roll(x, length - k, 0), 0.0)
        else:
            x = x + jnp.where(row >= k, pltpu.roll(x, k, 0), 0.0)
        k *= 2
    return x


def _fox_decay(z, b_fg128, ts):
    S = z.shape[0]

    def body(t, first, ins, outs, scratch):
        zs_ref, b_ref = ins
        hi_ref, mid_ref, lo_ref = outs
        (carry,) = scratch

        @pl.when(first)
        def _():
            carry[...] = jnp.zeros_like(carry)

        logf = -_softplus(-(zs_ref[...] + b_ref[...]))
        run = _scan_rows(logf, ts, ts) + carry[0:1, :]
        carry[0:1, :] = run[ts - 1:ts, :]
        f2 = run * LOG2E
        hi = f2.astype(BF16)
        r1 = f2 - hi.astype(F32)
        mid = r1.astype(BF16)
        lo = (r1 - mid.astype(F32)).astype(BF16)
        eye = _eye(128, BF16)
        hi_ref[...] = _dot(eye, hi, NT).astype(BF16)
        mid_ref[...] = _dot(eye, mid, NT).astype(BF16)
        lo_ref[...] = _dot(eye, lo, NT).astype(BF16)

    tcol = lambda dt: ((128, S), dt, (128, ts), lambda i: (0, i))
    return _tiled("fox_decay", body, S // ts,
                  [_cols(z, ts, 128, CB_SMALL), _full(b_fg128)],
                  [tcol(BF16), tcol(BF16), tcol(BF16)], scratch=[pltpu.VMEM((8, 128), F32)])


def _fox_decay_bwd(dfk_rows, dfq_rows, z, b_fg128, ts):
    S = z.shape[0]
    H = dfk_rows.shape[0]

    def body(t, first, ins, outs, scratch):
        dfk_ref, dfq_ref, zs_ref, b_ref = ins
        daf_ref, db_ref = outs
        (carry,) = scratch

        @pl.when(first)
        def _():
            carry[...] = jnp.zeros_like(carry)
            db_ref[...] = jnp.zeros_like(db_ref)

        r = lax.broadcasted_iota(jnp.int32, (H, 128), 0)
        c = lax.broadcasted_iota(jnp.int32, (H, 128), 1)
        place = jnp.where(r == c, 1.0, 0.0)
        df = _dot(dfk_ref[...] + dfq_ref[...], place, TN, HIGHEST)
        run = _scan_rows(df, ts, ts, reverse=True) + carry[0:1, :]
        carry[0:1, :] = run[0:1, :]
        daf = run * _sigmoid(-(zs_ref[...] + b_ref[...]))
        daf_ref[...] = daf
        db_ref[0:1, :] += jnp.sum(daf, axis=0, keepdims=True)

    rowsin = lambda a: (a, (H, ts), lambda i: (0, i))
    daf, db = _tiled("fox_decay_bwd", body, S // ts,
                     [rowsin(dfk_rows), rowsin(dfq_rows), _cols(z, ts, 128, CB_SMALL), _full(b_fg128)],
                     [_orow(S, (128,), F32, ts), _oacc((8, 128), F32)],
                     scratch=[pltpu.VMEM((8, 128), F32)], reverse=True)
    return daf, db[0]


FOX_AUG = 80


def _fox_fwd(q_aug, kT_aug, v_aug, tq):
    H, S, da = q_aug.shape
    dv = v_aug.shape[2]
    d = FOX_DIM
    tk = tq // 2
    qscale = (d ** -0.5) * LOG2E

    def kern(q_ref, kT_ref, v_ref, o_ref, lse_ref, qs_ref, s_buf, p_buf, m_scr, acc_scr):
        i = pl.program_id(1)
        col = lax.broadcasted_iota(jnp.int32, (1, da), 1)
        qb = _b16(q_ref[...] * jnp.where(col < d, qscale, 1.0))
        qs_ref[...] = qb

        def keys(t):
            return pl.ds(pl.multiple_of(t * tk, tk), tk)

        def stage(t, slot, mask_off, look_ahead):
            if look_ahead:
                s_buf[1 - slot] = _dot(qb, kT_ref[:, keys(t + 1)])
            pv = _dot(p_buf[1 - slot], v_ref[keys(jnp.maximum(t - 1, 0)), :])

            def scores():
                s = s_buf[slot]
                if mask_off is None:
                    return s
                r = lax.broadcasted_iota(jnp.int32, (tq, tk), 0)
                c = lax.broadcasted_iota(jnp.int32, (tq, tk), 1)
                return jnp.where(c + mask_off <= r, s, NEG)

            m = m_scr[...]
            m_new = jnp.maximum(m, jnp.max(scores(), axis=1, keepdims=True))
            alpha = jnp.exp2(m - m_new)
            p_buf[slot] = _b16(jnp.exp2(scores() - m_new))
            m_scr[...] = m_new
            acc_scr[...] = (acc_scr[...] + pv) * alpha

        s_buf[0] = _dot(qb, kT_ref[:, keys(0)])
        p_buf[1] = jnp.zeros((tq, tk), BF16)
        m_scr[...] = jnp.full((tq, 1), NEG, F32)
        acc_scr[...] = jnp.zeros((tq, dv), F32)

        def pair(n):
            stage(2 * n, 0, None, True)
            stage(2 * n + 1, 1, None, True)

        def quad(m, _):
            pair(2 * m)
            pair(2 * m + 1)
            return 0

        lax.fori_loop(0, i // 2, quad, 0)

        @pl.when(i % 2 == 1)
        def _():
            pair(i - 1)

        stage(2 * i, 0, 0, True)
        stage(2 * i + 1, 1, tk, False)
        acc = acc_scr[...] + _dot(p_buf[1], v_ref[keys(2 * i + 1), :])
        l = acc[:, d:d + 1]
        o_ref[...] = acc[:, :d] / l
        lse_ref[...] = _col_to_row(m_scr[...] + jnp.log(l) * LOG2E)

    return pl.pallas_call(
        kern, name="fox_fwd", grid=(H, S // tq),
        in_specs=[pl.BlockSpec((None, tq, da), lambda h, i: (h, i, 0)),
                  pl.BlockSpec((None, da, S), lambda h, i: (h, 0, 0)),
                  pl.BlockSpec((None, S, dv), lambda h, i: (h, 0, 0))],
        out_specs=[pl.BlockSpec((None, tq, d), lambda h, i: (h, i, 0)),
                   pl.BlockSpec((None, 1, tq), lambda h, i: (h, 0, i)),
                   pl.BlockSpec((None, tq, da), lambda h, i: (h, i, 0))],
        out_shape=[jax.ShapeDtypeStruct((H, S, d), F32), jax.ShapeDtypeStruct((H, 1, S), F32),
                   jax.ShapeDtypeStruct((H, S, da), BF16)],
        scratch_shapes=[pltpu.VMEM((2, tq, tk), F32), pltpu.VMEM((2, tq, tk), BF16),
                        pltpu.VMEM((tq, 1), F32), pltpu.VMEM((tq, dv), F32)],
        compiler_params=_params(("parallel", "arbitrary")),
    )(q_aug, kT_aug, v_aug)


def _fox_bwd(qs, k_aug, kT, v, do, lse_row, delta_row, tq):
    H, S, da = qs.shape
    d = FOX_DIM
    tk = tq
    nq = S // tq
    scale = d ** -0.5

    ts2 = tq // 2
    last = 2 * nq - 1

    def kern(q_ref, k_ref, kT_ref, v_ref, do_ref, lse_ref, dl_ref,
             dqT_ref, dk_ref, dv_ref, dfk_ref, dfq_ref,
             kq_buf, dp_buf, pb_buf, ds_buf, dk_scr, dv_scr, dfk_scr):
        j = pl.program_id(1)

        @pl.when(j == 0)
        def _():
            dqT_ref[...] = jnp.zeros_like(dqT_ref)
            dfq_ref[...] = jnp.zeros_like(dfq_ref)

        kb = k_ref[...]
        kTb = kT_ref[...]
        vb = v_ref[:, :d]
        dk_scr[...] = jnp.zeros_like(dk_scr)
        dv_scr[...] = jnp.zeros_like(dv_scr)
        dfk_scr[...] = jnp.zeros_like(dfk_scr)

        def queries(t):
            return pl.ds(pl.multiple_of(t * ts2, ts2), ts2)

        def products(t, slot):
            rows = queries(t)
            kq_buf[slot] = _dot(kb, q_ref[rows, :], NT)
            dp_buf[slot] = _dot(vb, do_ref[rows, :], NT)

        def pointwise(t, slot, mask_off):
            rows = queries(t)
            sT = kq_buf[slot]
            if mask_off is not None:
                r = lax.broadcasted_iota(jnp.int32, (tk, ts2), 0)
                c = lax.broadcasted_iota(jnp.int32, (tk, ts2), 1)
                sT = jnp.where(r <= c + mask_off, sT, NEG)
            pT = jnp.exp2(sT - lse_ref[:, rows])
            dsT = pT * (dp_buf[slot] - dl_ref[:, rows])
            pb_buf[slot] = _b16(pT)
            ds_buf[slot] = _b16(dsT)
            dfk_scr[...] -= jnp.sum(dsT, axis=1, keepdims=True)
            dfq_ref[:, rows] += jnp.sum(dsT, axis=0, keepdims=True)

        def accumulate(t, slot):
            rows = queries(t)
            dsb = ds_buf[slot]
            dv_scr[...] += _dot(pb_buf[slot], do_ref[rows, :])
            dk_scr[...] += _dot(dsb, q_ref[rows, :])
            dqT_ref[:, rows] += _dot(kTb, dsb) * scale

        def stage(t, slot, mask_off, has_prev):
            products(jnp.minimum(t + 1, last), 1 - slot)
            if has_prev:
                accumulate(t - 1, 1 - slot)
            pointwise(t, slot, mask_off)

        products(2 * j, 0)
        stage(2 * j, 0, 0, False)
        stage(2 * j + 1, 1, ts2, True)

        def pair(n):
            stage(2 * n, 0, None, True)
            stage(2 * n + 1, 1, None, True)

        def quad(m, _):
            pair(j + 1 + 2 * m)
            pair(j + 2 + 2 * m)
            return 0

        n_rest = nq - 1 - j
        lax.fori_loop(0, n_rest // 2, quad, 0)

        @pl.when(n_rest % 2 == 1)
        def _():
            pair(nq - 1)

        accumulate(last, 1)
        dk_ref[...] = dk_scr[:, :d] * LN2
        dv_ref[...] = dv_scr[...]
        dfk_ref[...] = _col_to_row(dfk_scr[...])

    tile = lambda h, j: (h, j, 0)
    whole = lambda h, j: (h, 0, 0)
    rowtile = lambda h, j: (h, 0, j)
    return pl.pallas_call(
        kern, name="fox_bwd", grid=(H, S // tk),
        in_specs=[pl.BlockSpec((None, S, da), whole),
                  pl.BlockSpec((None, tk, da), tile),
                  pl.BlockSpec((None, d, tk), lambda h, j: (h, 0, j)),
                  pl.BlockSpec((None, tk, 128), tile),
                  pl.BlockSpec((None, S, d), whole),
                  pl.BlockSpec((None, 1, S), whole),
                  pl.BlockSpec((None, 1, S), whole)],
        out_specs=[pl.BlockSpec((None, d, S), whole),
                   pl.BlockSpec((None, tk, d), tile),
                   pl.BlockSpec((None, tk, d), tile),
                   pl.BlockSpec((None, 1, tk), rowtile),
                   pl.BlockSpec((None, 1, S), whole)],
        out_shape=[jax.ShapeDtypeStruct((H, d, S), F32), jax.ShapeDtypeStruct((H, S, d), F32),
                   jax.ShapeDtypeStruct((H, S, d), F32), jax.ShapeDtypeStruct((H, 1, S), F32),
                   jax.ShapeDtypeStruct((H, 1, S), F32)],
        scratch_shapes=[pltpu.VMEM((2, tk, ts2), F32), pltpu.VMEM((2, tk, ts2), F32),
                        pltpu.VMEM((2, tk, ts2), BF16), pltpu.VMEM((2, tk, ts2), BF16),
                        pltpu.VMEM((tk, da), F32), pltpu.VMEM((tk, d), F32), pltpu.VMEM((tk, 1), F32)],
        compiler_params=_params(("parallel", "arbitrary")),
    )(qs, k_aug, kT, v, do, lse_row, delta_row)


def _heads_major(a, H, d):
    S = a.shape[0]
    return a.reshape(S, H, d).transpose(1, 0, 2)


def _heads_minor(a):
    H, S, d = a.shape
    return a.transpose(1, 0, 2).reshape(S, H * d)


def _lane_pick(x128, lane):
    return x128[:, lane:lane + 1]


def _l2_fwd(y):
    return lax.rsqrt(jnp.sum(y * y, axis=1, keepdims=True) + EPS)


def _gdn_prep(z, conv_w, a128, dt128, ts):
    S = z.shape[0]
    C3 = 3 * WIDTH
    hb = ts // 8

    def body(t, first, ins, outs, scratch):
        x_ref, halo_ref, zs_ref, w_ref, a_ref, dt_ref = ins
        qkv_ref, c_ref, gb_ref, gbT_ref = outs
        halo = jnp.where(t > 0, halo_ref[...], 0.0)
        xe = jnp.concatenate([halo, x_ref[...]], axis=0)
        w = w_ref[...]
        c = w[3:4, :] * xe[8:, :]
        for back in (1, 2, 3):
            c = c + w[3 - back:4 - back, :] * pltpu.roll(xe, back, 0)[8:, :]
        c_ref[...] = c
        y = c * _sigmoid(c)
        for h in range(GDN_HEADS):
            lo = h * GDN_DIM
            yq = y[:, lo:lo + GDN_DIM]
            qkv_ref[:, lo:lo + GDN_DIM] = yq * (_l2_fwd(yq) * (GDN_DIM ** -0.5))
            yk = y[:, WIDTH + lo:WIDTH + lo + GDN_DIM]
            qkv_ref[:, WIDTH + lo:WIDTH + lo + GDN_DIM] = yk * _l2_fwd(yk)
        qkv_ref[:, 2 * WIDTH:] = y[:, 2 * WIDTH:]
        zs = zs_ref[...]
        lane = lax.broadcasted_iota(jnp.int32, zs.shape, 1)
        g = -jnp.exp(a_ref[...]) * _softplus(zs + dt_ref[...])
        G = _scan_rows(g, ts, CHUNK)
        beta = _sigmoid(zs)
        out = jnp.where(lane < 8, pltpu.roll(g, 128 - LANE_BA, 1), jnp.where(lane < LANE_BB, G, beta))
        gb_ref[...] = out
        gbT_ref[...] = _transpose_exact(out)

    x_in = (z, (ts, C3), lambda i: (i, CB_BQKV))
    halo_in = (z, (8, C3), lambda i: (jnp.maximum(i * hb - 1, 0), CB_BQKV))
    return _tiled("gdn_prep", body, S // ts,
                  [x_in, halo_in, _cols(z, ts, 128, CB_SMALL), _full(conv_w), _full(a128), _full(dt128)],
                  [_orow(S, (C3,), F32, ts), _orow(S, (C3,), F32, ts), _orow(S, (128,), F32, ts),
                   ((128, S), F32, (128, ts), lambda i: (0, i))])


def _chunk_masks(nc):
    r = lax.broadcasted_iota(jnp.int32, (nc, CHUNK, CHUNK), 1)
    c = lax.broadcasted_iota(jnp.int32, (nc, CHUNK, CHUNK), 2)
    return c <= r, c < r, c == r


def _chunk_local(qh, kh, vh, Gc, Gr, beta):
    nc = qh.shape[0]
    incl, strict, _ = _chunk_masks(nc)
    gamma = jnp.exp(jnp.where(incl, Gc - Gr, NEG))
    kb = kh * beta
    P = _bdot(_b16(kb), _b16(kh), 2, 2)
    Qk = _bdot(_b16(qh), _b16(kh), 2, 2)
    eG = jnp.exp(Gc)
    Gl = Gc[:, CHUNK - 1:CHUNK, :]
    edec = jnp.exp(Gl - Gc)
    return incl, strict, gamma, kb, P, Qk, eG, edec


def _gdn_local_fwd(qkv, gb, grow, ts):
    S = qkv.shape[0]
    nc = ts // CHUNK

    def body(t, first, ins, outs, scratch):
        q_ref, k_ref, v_ref, gb_ref, gr_ref = ins
        u_ref, w_ref, qd_ref, kd_ref, aqk_ref, T_ref = outs
        gbv = gb_ref[...]
        heads = range(GDN_HEADS)
        _, _, eye = _chunk_masks(nc)
        T, X, rhs_u, rhs_w = [], [], [], []
        for h in heads:
            lo = h * GDN_DIM
            qh = q_ref[:, lo:lo + GDN_DIM].reshape(nc, CHUNK, GDN_DIM)
            kh = k_ref[:, lo:lo + GDN_DIM].reshape(nc, CHUNK, GDN_DIM)
            vh = v_ref[:, lo:lo + GDN_DIM].reshape(nc, CHUNK, GDN_DIM)
            Gc = _lane_pick(gbv, LANE_BA + h).reshape(nc, CHUNK, 1)
            beta = _lane_pick(gbv, LANE_BB + h).reshape(nc, CHUNK, 1)
            Gr = gr_ref[h].reshape(nc, 1, CHUNK)
            incl, strict, gamma, kb, P, Qk, eG, edec = _chunk_local(qh, kh, vh, Gc, Gr, beta)
            A = jnp.where(strict, P * gamma, 0.0)
            T.append(jnp.where(eye, 1.0, 0.0) - A)
            X.append(A)
            rhs_u.append(vh * beta)
            rhs_w.append(kb * eG)
            qd_ref[:, lo:lo + GDN_DIM] = _b16(qh * eG).reshape(ts, GDN_DIM)
            kd_ref[:, lo:lo + GDN_DIM] = _b16(kh * edec).reshape(ts, GDN_DIM)
            aqk_ref[h] = _b16(jnp.where(incl, Qk * gamma, 0.0)).reshape(ts, CHUNK)
        for _ in range(5):
            X = [_bdot(X[h], X[h], 2, 1, PREC_UT) for h in heads]
            T = [T[h] + _bdot(T[h], X[h], 2, 1, PREC_UT) for h in heads]
        u = [_bdot(T[h], rhs_u[h], 2, 1, PREC_UT) for h in heads]
        w = [_bdot(T[h], rhs_w[h], 2, 1, PREC_UT) for h in heads]
        for h in heads:
            lo = h * GDN_DIM
            u_ref[:, lo:lo + GDN_DIM] = u[h].reshape(ts, GDN_DIM)
            w_ref[:, lo:lo + GDN_DIM] = _b16(w[h]).reshape(ts, GDN_DIM)
            T_ref[h] = T[h].reshape(ts, CHUNK)

    wide = lambda dt: _orow(S, (WIDTH,), dt, ts)
    perhead = lambda dt: ((GDN_HEADS, S, CHUNK), dt, (GDN_HEADS, ts, CHUNK), lambda i: (0, i, 0))
    return _tiled("gdn_local_fwd", body, S // ts,
                  [_cols(qkv, ts, WIDTH, 0), _cols(qkv, ts, WIDTH, 1), _cols(qkv, ts, WIDTH, 2),
                   _rows(gb, ts), (grow, (GDN_HEADS, nc, CHUNK), lambda i: (0, i, 0))],
                  [wide(F32), wide(BF16), wide(BF16), wide(BF16), perhead(BF16), perhead(F32)])


def _gdn_scan_fwd(u, w, qd, kd, aqk, gb, ts):
    S = u.shape[0]
    nc = ts // CHUNK
    N = S // CHUNK

    def body(t, first, ins, outs, scratch):
        u_ref, w_ref, qd_ref, kd_ref, aqk_ref, gb_ref = ins
        o_ref, vn_ref, st_ref = outs
        (state,) = scratch

        @pl.when(first)
        def _():
            state[...] = jnp.zeros_like(state)

        def chunk(c, _):
            r0 = pl.multiple_of(c * CHUNK, CHUNK)
            rows = pl.ds(r0, CHUNK)
            glast = gb_ref[pl.ds(r0 + CHUNK - 1, 1), :]
            heads = range(GDN_HEADS)
            cols = [slice(h * GDN_DIM, (h + 1) * GDN_DIM) for h in heads]
            S_old = [state[h] for h in heads]
            u_h = [u_ref[rows, cols[h]] for h in heads]
            w_h = [_b16(w_ref[rows, cols[h]]) for h in heads]
            qd_h = [_b16(qd_ref[rows, cols[h]]) for h in heads]
            kd_h = [_b16(kd_ref[rows, cols[h]]) for h in heads]
            aqk_h = [_b16(aqk_ref[h, rows, :]) for h in heads]
            both = [_dot(jnp.concatenate([w_h[h], qd_h[h]], axis=0), _b16(S_old[h]))
                    for h in heads]
            vn_h = [u_h[h] - both[h][:CHUNK] for h in heads]
            vnb = [_b16(vn_h[h]) for h in heads]
            intra = [_dot(aqk_h[h], vnb[h]) for h in heads]
            outer = [_dot(kd_h[h], vnb[h], TN) for h in heads]
            o_h = [both[h][CHUNK:] + intra[h] for h in heads]
            S_new = [S_old[h] * jnp.exp(glast[:, LANE_BA + h:LANE_BA + h + 1]) + outer[h] for h in heads]
            for h in heads:
                st_ref[c, h] = S_old[h]
                state[h] = S_new[h]
                o_ref[rows, cols[h]] = o_h[h]
                vn_ref[rows, cols[h]] = vnb[h]
            return 0

        lax.fori_loop(0, nc, chunk, 0)

    wide_in = lambda a: _rows(a, ts)
    wide = lambda dt: _orow(S, (WIDTH,), dt, ts)
    states = ((N, GDN_HEADS, GDN_DIM, GDN_DIM), F32, (nc, GDN_HEADS, GDN_DIM, GDN_DIM),
              lambda i: (i, 0, 0, 0))
    return _tiled("gdn_scan_fwd", body, S // ts,
                  [wide_in(u), wide_in(w), wide_in(qd), wide_in(kd),
                   (aqk, (GDN_HEADS, ts, CHUNK), lambda i: (0, i, 0)), _rows(gb, ts)],
                  [wide(F32), wide(BF16), states],
                  scratch=[pltpu.VMEM((GDN_HEADS, GDN_DIM, GDN_DIM), F32)])


def _gdn_scan_bwd(do, w, qd, kd, aqk, vn, states, gb, ts):
    S = do.shape[0]
    nc = ts // CHUNK
    N = S // CHUNK

    def body(t, first, ins, outs, scratch):
        do_ref, w_ref, qd_ref, kd_ref, aqk_ref, vn_ref, st_ref, gb_ref = ins
        du_ref, dw_ref, dqd_ref, dkd_ref, daqk_ref, dgl_ref = outs
        (dstate,) = scratch

        @pl.when(first)
        def _():
            dstate[...] = jnp.zeros_like(dstate)

        r = lax.broadcasted_iota(jnp.int32, (CHUNK, CHUNK), 0)
        cc = lax.broadcasted_iota(jnp.int32, (CHUNK, CHUNK), 1)
        incl = cc <= r
        lane = lax.broadcasted_iota(jnp.int32, (1, 128), 1)

        def chunk(k, _):
            c = nc - 1 - k
            r0 = pl.multiple_of(c * CHUNK, CHUNK)
            rows = pl.ds(r0, CHUNK)
            glast = gb_ref[pl.ds(r0 + CHUNK - 1, 1), :]
            dgl_row = jnp.zeros((1, 128), F32)
            heads = range(GDN_HEADS)
            cols = [slice(h * GDN_DIM, (h + 1) * GDN_DIM) for h in heads]
            S_h = [st_ref[c, h] for h in heads]
            dS_h = [dstate[h] for h in heads]
            do_h = [_b16(do_ref[rows, cols[h]]) for h in heads]
            aqk_h = [_b16(aqk_ref[h, rows, :]) for h in heads]
            vn_h = [_b16(vn_ref[rows, cols[h]]) for h in heads]
            kd_h = [_b16(kd_ref[rows, cols[h]]) for h in heads]
            qd_h = [_b16(qd_ref[rows, cols[h]]) for h in heads]
            w_h = [_b16(w_ref[rows, cols[h]]) for h in heads]
            Sb = [_b16(S_h[h]) for h in heads]
            dSb = [_b16(dS_h[h]) for h in heads]
            dvn_a = [_dot(aqk_h[h], do_h[h], TN) for h in heads]
            dvn_b = [_dot(kd_h[h], dSb[h]) for h in heads]
            daqk_h = [jnp.where(incl, _dot(do_h[h], vn_h[h], NT), 0.0) for h in heads]
            dkd_h = [_dot(vn_h[h], dSb[h], NT) for h in heads]
            dvn_h = [dvn_a[h] + dvn_b[h] for h in heads]
            both = [jnp.concatenate([do_h[h], _b16(dvn_h[h])], axis=0) for h in heads]
            by_state = [_dot(both[h], Sb[h], NT) for h in heads]
            dS_dot = [_dot(jnp.concatenate([qd_h[h], -w_h[h]], axis=0), both[h], TN) for h in heads]
            res = []
            for h in heads:
                egl = jnp.exp(glast[:, LANE_BA + h:LANE_BA + h + 1])
                dgl = egl * jnp.sum(jnp.sum(dS_h[h] * S_h[h], axis=1, keepdims=True), axis=0,
                                    keepdims=True)
                dgl_row = jnp.where(lane == h, dgl, dgl_row)
                res.append((daqk_h[h], by_state[h][:CHUNK], dkd_h[h], -by_state[h][CHUNK:], dvn_h[h],
                            dS_dot[h] + egl * dS_h[h]))
            for h in heads:
                daqk, dqd, dkd, dw, dvn, dS_new = res[h]
                daqk_ref[h, rows, :] = daqk
                dqd_ref[rows, cols[h]] = dqd
                dkd_ref[rows, cols[h]] = dkd
                dw_ref[rows, cols[h]] = dw
                du_ref[rows, cols[h]] = dvn
                dstate[h] = dS_new
            dgl_ref[pl.ds(c, 1), :] = dgl_row
            return 0

        lax.fori_loop(0, nc, chunk, 0)

    wide_in = lambda a: _rows(a, ts)
    wide = _orow(S, (WIDTH,), F32, ts)
    perhead_in = lambda a: (a, (GDN_HEADS, ts, CHUNK), lambda i: (0, i, 0))
    perhead = ((GDN_HEADS, S, CHUNK), F32, (GDN_HEADS, ts, CHUNK), lambda i: (0, i, 0))
    return _tiled("gdn_scan_bwd", body, S // ts,
                  [wide_in(do), wide_in(w), wide_in(qd), wide_in(kd), perhead_in(aqk), wide_in(vn),
                   (states, (nc, GDN_HEADS, GDN_DIM, GDN_DIM), lambda i: (i, 0, 0, 0)), _rows(gb, ts)],
                  [wide, wide, wide, wide, perhead, ((N, 128), F32, (nc, 128), lambda i: (i, 0))],
                  scratch=[pltpu.VMEM((GDN_HEADS, GDN_DIM, GDN_DIM), F32)], reverse=True)


def _gdn_local_bwd(qkv, gb, grow, T, du, dw, dqd, dkd, daqk, dgl, ts):
    S = qkv.shape[0]
    nc = ts // CHUNK

    def body(t, first, ins, outs, scratch):
        (q_ref, k_ref, v_ref, gb_ref, gr_ref, T_ref, du_ref, dw_ref, dqd_ref, dkd_ref,
         daqk_ref, dgl_ref) = ins
        dqkv_ref, dgb_ref = outs
        gbv = gb_ref[...]
        dglv = dgl_ref[...]
        lane = lax.broadcasted_iota(jnp.int32, (ts, 128), 1)
        dG_all = jnp.zeros((ts, 128), F32)
        dbeta_all = jnp.zeros((ts, 128), F32)
        heads = range(GDN_HEADS)
        _, _, eye = _chunk_masks(nc)
        pre = []
        for h in heads:
            lo = h * GDN_DIM
            cols = slice(lo, lo + GDN_DIM)
            r3 = lambda ref: ref[:, cols].reshape(nc, CHUNK, GDN_DIM)
            qh, kh, vh = r3(q_ref), r3(k_ref), r3(v_ref)
            duh, dwh, dqdh, dkdh = r3(du_ref), r3(dw_ref), r3(dqd_ref), r3(dkd_ref)
            Gc = _lane_pick(gbv, LANE_BA + h).reshape(nc, CHUNK, 1)
            beta = _lane_pick(gbv, LANE_BB + h).reshape(nc, CHUNK, 1)
            Gr = gr_ref[h].reshape(nc, 1, CHUNK)
            Th = T_ref[h].reshape(nc, CHUNK, CHUNK)
            daq = daqk_ref[h].reshape(nc, CHUNK, CHUNK)
            local = _chunk_local(qh, kh, vh, Gc, Gr, beta)
            kb, eG = local[3], local[6]
            vb = vh * beta
            kbg = kb * eG
            dvb = _bdot(Th, duh, 1, 1, PREC_UT)
            dkbg = _bdot(Th, dwh, 1, 1, PREC_UT)
            dT = _bdot(duh, vb, 2, 2, PREC_UT) + _bdot(dwh, kbg, 2, 2, PREC_UT)
            pre.append((qh, kh, vh, dqdh, dkdh, beta, Th, daq, local, kbg, dvb, dkbg, dT))
        M1s = [_bdot(pre[h][6], pre[h][12], 1, 1, PREC_UT) for h in heads]
        dAs = [_bdot(M1s[h], pre[h][6], 2, 2, PREC_UT) for h in heads]
        for h in heads:
            lo = h * GDN_DIM
            cols = slice(lo, lo + GDN_DIM)
            qh, kh, vh, dqdh, dkdh, beta, Th, daq, local, kbg, dvb, dkbg, dT = pre[h]
            incl, strict, gamma, kb, P, Qk, eG, edec = local
            dA = jnp.where(strict, -dAs[h], 0.0)
            dP = dA * gamma
            dQ = daq * gamma
            dgam = (dA * P + daq * Qk) * gamma
            dPb, dQb = _b16(dP), _b16(dQ)
            khb, qhb, kbb = _b16(kh), _b16(qh), _b16(kb)
            dq = _bdot(dQb, khb, 2, 1) + dqdh * eG
            dkb = _bdot(dPb, khb, 2, 1) + dkbg * eG
            dk = (_bdot(dQb, qhb, 1, 1) + _bdot(dPb, kbb, 1, 1) + dkdh * edec + dkb * beta)
            dbeta = (jnp.sum(dkb * kh, axis=2, keepdims=True) + jnp.sum(dvb * vh, axis=2, keepdims=True))
            dv = dvb * beta
            col_as_col = jnp.sum(jnp.where(eye, jnp.sum(dgam, axis=1, keepdims=True), 0.0),
                                 axis=2, keepdims=True)
            kd_term = jnp.sum(dkdh * kh * edec, axis=2, keepdims=True)
            dG = (jnp.sum(dgam, axis=2, keepdims=True) - col_as_col
                  + jnp.sum(dqdh * qh * eG, axis=2, keepdims=True)
                  + jnp.sum(dkbg * kbg, axis=2, keepdims=True) - kd_term)
            dgl_h = dglv[:, h:h + 1].reshape(nc, 1, 1) + jnp.sum(kd_term, axis=1, keepdims=True)
            last = lax.broadcasted_iota(jnp.int32, (nc, CHUNK, 1), 1) == CHUNK - 1
            dG = dG + jnp.where(last, dgl_h, 0.0)
            dqkv_ref[:, cols] = dq.reshape(ts, GDN_DIM)
            dqkv_ref[:, WIDTH + lo:WIDTH + lo + GDN_DIM] = dk.reshape(ts, GDN_DIM)
            dqkv_ref[:, 2 * WIDTH + lo:2 * WIDTH + lo + GDN_DIM] = dv.reshape(ts, GDN_DIM)
            dG_all = jnp.where(lane == LANE_BA + h, dG.reshape(ts, 1), dG_all)
            dbeta_all = jnp.where(lane == LANE_BB + h, dbeta.reshape(ts, 1), dbeta_all)
        dg_all = _scan_rows(dG_all, ts, CHUNK, reverse=True)
        dgb_ref[...] = jnp.where(lane < LANE_BB, dg_all, dbeta_all)

    wide_in = lambda a: _rows(a, ts)
    perhead_in = lambda a: (a, (GDN_HEADS, ts, CHUNK), lambda i: (0, i, 0))
    return _tiled("gdn_local_bwd", body, S // ts,
                  [_cols(qkv, ts, WIDTH, 0), _cols(qkv, ts, WIDTH, 1), _cols(qkv, ts, WIDTH, 2),
                   _rows(gb, ts), (grow, (GDN_HEADS, nc, CHUNK), lambda i: (0, i, 0)), perhead_in(T),
                   wide_in(du), wide_in(dw), wide_in(dqd), wide_in(dkd), perhead_in(daqk),
                   (dgl, (nc, 128), lambda i: (i, 0))],
                  [_orow(S, (3 * WIDTH,), F32, ts), _orow(S, (128,), F32, ts)])


def _gdn_prep_bwd(dqkv, dgb, cpre, z, conv_w, a128, dt128, dz, ts):
    S = z.shape[0]
    C3 = 3 * WIDTH
    hb = ts // 8
    n_tiles = S // ts

    def dpre(dq, c):
        y, dsil = _silu_and_grad(c)
        parts = []
        for h in range(GDN_HEADS):
            lo = h * GDN_DIM
            yq = y[:, lo:lo + GDN_DIM]
            rq = _l2_fwd(yq)
            nq = yq * rq
            dn = dq[:, lo:lo + GDN_DIM] * (GDN_DIM ** -0.5)
            parts.append(rq * (dn - nq * jnp.sum(dn * nq, axis=1, keepdims=True)))
        for h in range(GDN_HEADS):
            lo = WIDTH + h * GDN_DIM
            yk = y[:, lo:lo + GDN_DIM]
            rk = _l2_fwd(yk)
            nk = yk * rk
            dn = dq[:, lo:lo + GDN_DIM]
            parts.append(rk * (dn - nk * jnp.sum(dn * nk, axis=1, keepdims=True)))
        parts.append(dq[:, 2 * WIDTH:])
        return jnp.concatenate(parts, axis=1) * dsil

    def body(t, first, ins, outs, scratch):
        (dq_ref, dqn_ref, c_ref, cn_ref, x_ref, xp_ref, zs_ref, dgb_ref, w_ref, a_ref, dt_ref) = ins
        dx_ref, dzs_ref, dw_ref, dad_ref = outs

        @pl.when(first)
        def _():
            dw_ref[...] = jnp.zeros_like(dw_ref)
            dad_ref[...] = jnp.zeros_like(dad_ref)

        dc = dpre(dq_ref[...], c_ref[...])
        dcn = jnp.where(t < n_tiles - 1, dpre(dqn_ref[...], cn_ref[...]), 0.0)
        dce = jnp.concatenate([dc, dcn], axis=0)
        w = w_ref[...]
        dx = w[3:4, :] * dc
        for back in (1, 2, 3):
            dx = dx + w[3 - back:4 - back, :] * pltpu.roll(dce, ts + 8 - back, 0)[:ts, :]
        dx_ref[...] = _b16(dx)
        halo = jnp.where(t > 0, xp_ref[...], 0.0)
        xe = jnp.concatenate([halo, x_ref[...]], axis=0)
        dw_ref[3:4, :] += jnp.sum(dc * xe[8:, :], axis=0, keepdims=True)
        for back in (1, 2, 3):
            dw_ref[3 - back:4 - back, :] += jnp.sum(dc * pltpu.roll(xe, back, 0)[8:, :], axis=0,
                                                     keepdims=True)
        zs = zs_ref[...]
        dgb = dgb_ref[...]
        lane = lax.broadcasted_iota(jnp.int32, zs.shape, 1)
        arg = zs + dt_ref[...]
        nega = -jnp.exp(a_ref[...])
        dba = dgb * nega * _sigmoid(arg)
        beta = _sigmoid(zs)
        dbb = dgb * beta * (1.0 - beta)
        dzs_ref[...] = jnp.where((lane >= LANE_BA) & (lane < LANE_BB), dba,
                                 jnp.where((lane >= LANE_BB) & (lane < LANE_BB + 4), dbb, 0.0))
        dad_ref[0:1, :] += jnp.sum(dgb * nega * _softplus(arg), axis=0, keepdims=True)
        dad_ref[1:2, :] += jnp.sum(dba, axis=0, keepdims=True)

    nxt = lambda i: (jnp.minimum((i + 1) * hb, S // 8 - 1), 0)
    prv = lambda i: (jnp.maximum(i * hb - 1, 0), CB_BQKV)
    return _tiled("gdn_prep_bwd", body, n_tiles,
                  [_rows(dqkv, ts), (dqkv, (8, C3), nxt), _rows(cpre, ts), (cpre, (8, C3), nxt),
                   (z, (ts, C3), lambda i: (i, CB_BQKV)), (z, (8, C3), prv),
                   _cols(z, ts, 128, CB_SMALL), _rows(dgb, ts), _full(conv_w), _full(a128), _full(dt128)],
                  [((S, N_AL), BF16, (ts, C3), lambda i: (i, CB_BQKV)), _orow(S, (128,), F32, ts),
                   _oacc((8, C3), F32), _oacc((8, 128), F32)],
                  fill=(dz, 0))


def _mem_attn_fwd(z, mk, mv, ts):
    S = z.shape[0]

    def body(t, first, ins, outs, scratch):
        q_ref, mk_ref, mv_ref = ins
        (o_ref,) = outs
        heads = range(MEM_HEADS)
        cols = [slice(h * MEM_DIM, (h + 1) * MEM_DIM) for h in heads]
        s = [_dot(_b16(q_ref[:, cols[h]]), _b16(mk_ref[:, cols[h]]), NT) * (MEM_DIM ** -0.5)
             for h in heads]
        p = []
        for h in heads:
            e = jnp.exp(s[h] - jnp.max(s[h], axis=1, keepdims=True))
            p.append(_b16(e / jnp.sum(e, axis=1, keepdims=True)))
        o = [_dot(p[h], _b16(mv_ref[:, cols[h]])) for h in heads]
        for h in heads:
            o_ref[:, cols[h]] = o[h]

    (o,) = _tiled("mem_attn_fwd", body, S // ts, [_cols(z, ts, WIDTH, CB_MQ), _full(mk), _full(mv)],
                  [_orow(S, (WIDTH,), F32, ts)])
    return o


def _mem_attn_bwd(do, z, mk, mv, dz, ts):
    S = z.shape[0]
    M = mk.shape[0]

    def body(t, first, ins, outs, scratch):
        do_ref, q_ref, mk_ref, mv_ref = ins
        dq_ref, dmk_ref, dmv_ref = outs

        @pl.when(first)
        def _():
            dmk_ref[...] = jnp.zeros_like(dmk_ref)
            dmv_ref[...] = jnp.zeros_like(dmv_ref)

        scale = MEM_DIM ** -0.5
        heads = range(MEM_HEADS)
        cols = [slice(h * MEM_DIM, (h + 1) * MEM_DIM) for h in heads]
        qb = [_b16(q_ref[:, cols[h]]) for h in heads]
        kb = [_b16(mk_ref[:, cols[h]]) for h in heads]
        dob = [_b16(do_ref[:, cols[h]]) for h in heads]
        s = [_dot(qb[h], kb[h], NT) * scale for h in heads]
        dp = [_dot(dob[h], _b16(mv_ref[:, cols[h]]), NT) for h in heads]
        p = []
        for h in heads:
            e = jnp.exp(s[h] - jnp.max(s[h], axis=1, keepdims=True))
            p.append(e / jnp.sum(e, axis=1, keepdims=True))
        dsb = [_b16(p[h] * (dp[h] - jnp.sum(dp[h] * p[h], axis=1, keepdims=True)) * scale) for h in heads]
        dmv = [_dot(_b16(p[h]), dob[h], TN) for h in heads]
        dq = [_dot(dsb[h], kb[h]) for h in heads]
        dmk = [_dot(dsb[h], qb[h], TN) for h in heads]
        for h in heads:
            dmv_ref[:, cols[h]] += dmv[h]
            dq_ref[:, cols[h]] = _b16(dq[h])
            dmk_ref[:, cols[h]] += dmk[h]

    return _tiled("mem_attn_bwd", body, S // ts,
                  [_rows(do, ts), _cols(z, ts, WIDTH, CB_MQ), _full(mk), _full(mv)],
                  [((S, N_AL), BF16, (ts, WIDTH), lambda i: (i, CB_MQ)), _oacc((M, WIDTH), F32),
                   _oacc((M, WIDTH), F32)],
                  fill=(dz, 0))


def _head_norm(ob, g):
    xs, rs = [], []
    for h in range(GDN_HEADS):
        o = ob[:, h * GDN_DIM:(h + 1) * GDN_DIM]
        r = lax.rsqrt(jnp.mean(o * o, axis=1, keepdims=True) + EPS)
        xs.append(o * r)
        rs.append(r)
    return xs, rs


def _merge_fwd(x, z, o_a, o_b, o_m, gdn_g, b_merge, wb, wout, ts):
    S, D = x.shape

    def body(t, first, ins, outs, scratch):
        (x_ref, g_ref, oa_ref, az_ref, ob_ref, bz_ref, om_ref, mz_ref, gg_ref, bm_ref, wb_ref,
         wo_ref) = ins
        xo_ref, ya_ref, yb_ref, ym_ref, mg_ref = outs
        ya = oa_ref[...] * _silu_and_grad(az_ref[...])[0]
        xs, _ = _head_norm(ob_ref[...], None)
        nb = jnp.concatenate([xh * gg_ref[...] for xh in xs], axis=1)
        yb = nb * _silu_and_grad(bz_ref[...])[0]
        ym = om_ref[...] * _silu_and_grad(mz_ref[...])[0]
        merged = jnp.zeros((ts, D), F32)
        for n, (y, y_ref) in enumerate(((ya, ya_ref), (yb, yb_ref), (ym, ym_ref))):
            yb16 = _b16(y)
            y_ref[...] = yb16
            gate = _sigmoid(g_ref[:, n * D:(n + 1) * D] + bm_ref[:, n * D:(n + 1) * D])
            merged = merged + gate * _dot(yb16, wb_ref[n])
        mb = _b16(merged)
        mg_ref[...] = mb
        xo_ref[...] = x_ref[...] + _dot(mb, wo_ref[...])

    half = lambda a: _rows(a, ts)
    return _tiled("merge_fwd", body, S // ts,
                  [_rows(x, ts), _cols(z, ts, 3 * D, CB_GATES), half(o_a), _cols(z, ts, WIDTH, CB_AZ),
                   half(o_b), _cols(z, ts, WIDTH, CB_BZ), half(o_m), _cols(z, ts, WIDTH, CB_MZ),
                   _full(gdn_g.reshape(1, GDN_DIM)), _full(b_merge.reshape(1, 3 * D)), _full(wb), _full(wout)],
                  [_orow(S, (D,), F32, ts), _orow(S, (WIDTH,), BF16, ts), _orow(S, (WIDTH,), BF16, ts),
                   _orow(S, (WIDTH,), BF16, ts), _orow(S, (D,), BF16, ts)])


def _merge_bwd(dout, z, o_a, o_b, o_m, ya, yb, ym, gdn_g, b_merge, wb, wout, hsum, ts):
    S, D = dout.shape

    def body(t, first, ins, outs, scratch):
        (do_ref, g_ref, oa_ref, az_ref, ob_ref, bz_ref, om_ref, mz_ref, ya_ref, yb_ref, ym_ref,
         gg_ref, bm_ref, wb_ref, wo_ref, hs_ref) = ins
        (dg_ref, dpa_ref, dpb_ref, dpm_ref, doa_ref, dob_ref, dom_ref, dl_ref, dbm_ref, dgg_ref) = outs
        G3 = 3 * D

        @pl.when(first)
        def _():
            dbm_ref[...] = jnp.zeros_like(dbm_ref)
            dgg_ref[...] = jnp.zeros_like(dgg_ref)

        dmerged = _dot(_b16(do_ref[...]), wo_ref[...], NT)
        dys = []
        for n, (y_ref, dp_ref) in enumerate(((ya_ref, dpa_ref), (yb_ref, dpb_ref), (ym_ref, dpm_ref))):
            sl = slice(n * D, (n + 1) * D)
            gate = _sigmoid(g_ref[:, sl] + bm_ref[:, sl])
            proj = _dot(y_ref[...], wb_ref[n])
            dproj = _b16(gate * dmerged)
            dp_ref[...] = dproj
            dgp = dmerged * proj * gate * (1.0 - gate)
            dg_ref[:, sl] = dgp.astype(dg_ref.dtype)
            dbm_ref[0:1, sl] += jnp.sum(dgp, axis=0, keepdims=True)
            dys.append(_dot(dproj, wb_ref[n], NT))
        dya, dyb, dym = dys
        sa, dsa = _silu_and_grad(az_ref[...])
        oa = oa_ref[...]
        doa = dya * sa
        doa_ref[...] = doa
        dg_ref[:, G3:G3 + WIDTH] = _b16(dya * oa * dsa)
        dl_ref[...] = _dot(hs_ref[...], doa * oa, NT, HIGHEST)
        sm, dsm = _silu_and_grad(mz_ref[...])
        dom_ref[...] = dym * sm
        dg_ref[:, G3 + 2 * WIDTH:G3 + 3 * WIDTH] = _b16(dym * om_ref[...] * dsm)
        sb, dsb = _silu_and_grad(bz_ref[...])
        xs, rs = _head_norm(ob_ref[...], None)
        gg = gg_ref[...]
        dgg = jnp.zeros((1, GDN_DIM), F32)
        for h in range(GDN_HEADS):
            cols = slice(h * GDN_DIM, (h + 1) * GDN_DIM)
            dn = dyb[:, cols] * sb[:, cols]
            dg_ref[:, G3 + WIDTH + h * GDN_DIM:G3 + WIDTH + (h + 1) * GDN_DIM] = _b16(
                dyb[:, cols] * (xs[h] * gg) * dsb[:, cols])
            dgg = dgg + jnp.sum(dn * xs[h], axis=0, keepdims=True)
            dxh = dn * gg
            dob_ref[:, cols] = rs[h] * (dxh - xs[h] * jnp.mean(dxh * xs[h], axis=1, keepdims=True))
        dgg_ref[0:1, :] += dgg

    half = lambda a: _rows(a, ts)
    w512 = lambda dt: _orow(S, (WIDTH,), dt, ts)
    return _tiled("merge_bwd", body, S // ts,
                  [_rows(dout, ts), _cols(z, ts, 3 * D, CB_GATES), half(o_a), _cols(z, ts, WIDTH, CB_AZ),
                   half(o_b), _cols(z, ts, WIDTH, CB_BZ), half(o_m), _cols(z, ts, WIDTH, CB_MZ),
                   half(ya), half(yb), half(ym), _full(gdn_g.reshape(1, GDN_DIM)),
                   _full(b_merge.reshape(1, 3 * D)), _full(wb), _full(wout), _full(hsum)],
                  [((S, N_AL), BF16, (ts, 3 * D + 3 * WIDTH), lambda i: (i, CB_MERGE)),
                   _orow(S, (D,), BF16, ts), _orow(S, (D,), BF16, ts),
                   _orow(S, (D,), BF16, ts), w512(F32), w512(F32), w512(F32),
                   ((128, S), F32, (128, ts), lambda i: (0, i)), _oacc((8, 3 * D), F32),
                   _oacc((8, GDN_DIM), F32)])


def _to_aligned(w):
    parts = [w[..., lo:lo + n] for lo, n, _ in sorted(W_IN_PIECES, key=lambda p: p[2])]
    parts.append(jnp.zeros(w.shape[:-1] + (N_AL - N_IN,), w.dtype))
    return jnp.concatenate(parts, axis=-1)


def _from_aligned(w):
    return jnp.concatenate([w[..., al:al + n] for _, n, al in W_IN_PIECES], axis=-1)


def _lanes128(v, lane0):
    return jnp.pad(v.astype(F32)[None, :], ((0, 0), (lane0, 128 - lane0 - v.shape[0])))


def _tiles(S):
    ts = min(512, S // 2)
    return dict(ts=ts, ts_small=min(256, S // 2), tq=min(512, S // 4), tq_fwd=min(1024, S // 2))


def _layer_fwd(x, mem, p):
    S = x.shape[0]
    tl = _tiles(S)
    ts, tss, tq = tl["ts"], tl["ts_small"], tl["tq"]
    h, rstd = _rms_fwd("norm_fwd", x, p["norm_g"], ts)
    z = _mm("in_proj", h, p["w_in_al"], tm=2048, tn=1664)

    b_fg128 = _lanes128(p["b_fg"], LANE_AF)
    f_hi, f_mid, f_lo = _fox_decay(z, b_fg128, ts)
    aq = z[:, CB_AQ * WIDTH:(CB_AQ + 1) * WIDTH]
    ak = z[:, CB_AK * WIDTH:(CB_AK + 1) * WIDTH]
    av = z[:, CB_AV * WIDTH:(CB_AV + 1) * WIDTH]
    q32 = _heads_major(aq, FOX_HEADS, FOX_DIM)
    kh = _heads_major(ak, FOX_HEADS, FOX_DIM).astype(BF16)
    vh = _heads_major(av, FOX_HEADS, FOX_DIM).astype(BF16)
    piecesT = jnp.stack([f[:FOX_HEADS] for f in (f_hi, f_mid, f_lo)], axis=1)
    pieces = piecesT.transpose(0, 2, 1)
    ones3 = jnp.ones((FOX_HEADS, S, 3), BF16)
    padk = jnp.zeros((FOX_HEADS, S, FOX_AUG - FOX_DIM - 6), BF16)
    q_aug = jnp.concatenate([q32, pieces.astype(F32), ones3.astype(F32), padk.astype(F32)], axis=-1)
    k_aug = jnp.concatenate([kh, ones3, -pieces, padk], axis=-1)
    kT_aug = jnp.concatenate([kh.transpose(0, 2, 1), ones3.transpose(0, 2, 1), -piecesT,
                              padk.transpose(0, 2, 1)], axis=1)
    v_aug = jnp.concatenate([vh, ones3[:, :, :1], jnp.zeros((FOX_HEADS, S, 128 - FOX_DIM - 1), BF16)],
                            axis=-1)
    o_h, lse, qs = _fox_fwd(q_aug, kT_aug, v_aug, tl["tq_fwd"])
    o_a = _heads_minor(o_h)

    a128 = _lanes128(p["a_log"], LANE_BA)
    dt128 = _lanes128(p["dt_bias"], LANE_BA)
    qkv, cpre, gb, gbT = _gdn_prep(z, p["conv_w"], a128, dt128, ts)
    grow = gbT[LANE_BA:LANE_BA + GDN_HEADS].reshape(GDN_HEADS, S // CHUNK, CHUNK)
    u, w, qd, kd, aqk, T = _gdn_local_fwd(qkv, gb, grow, ts)
    o_b, vn, states = _gdn_scan_fwd(u, w, qd, kd, aqk, gb, ts)

    mem_h, mem_r = _rms_fwd("mem_norm_fwd", mem, p["mem_norm_g"], mem.shape[0])
    mkv = _mm("mem_kv", mem_h, p["w_mem_kv"])
    mk, mv = mkv[:, :WIDTH], mkv[:, WIDTH:]
    o_m = _mem_attn_fwd(z, mk, mv, ts)

    x_next, ya, yb, ym, merged = _merge_fwd(x, z, o_a, o_b, o_m, p["gdn_norm_g"], p["b_merge"],
                                            p["w_branch"], p["w_out"], ts)
    saved = dict(x=x, h=h, rstd=rstd, z=z, b_fg128=b_fg128, qs=qs, k_aug=k_aug, kT_aug=kT_aug, v_aug=v_aug, lse=lse, o_a=o_a, a128=a128, dt128=dt128, qkv=qkv, cpre=cpre, gb=gb,
                 grow=grow, w=w, qd=qd, kd=kd, aqk=aqk, T=T, o_b=o_b, vn=vn, states=states,
                 mem_h=mem_h, mem_r=mem_r, mk=mk, mv=mv, o_m=o_m, ya=ya, yb=yb, ym=ym, merged=merged)
    return x_next, saved


def _layer_bwd(dout, mem, p, s):
    S = dout.shape[0]
    tl = _tiles(S)
    ts, tss, tq = tl["ts"], tl["ts_small"], tl["tq"]
    z = s["z"]
    hsum = (jnp.arange(128)[:, None] == jnp.arange(WIDTH)[None, :] // FOX_DIM).astype(F32)
    (dz, dpa, dpb, dpm, do_a, do_b, do_m, deltaT, db_merge, dgdn_g) = _merge_bwd(
        dout, z, s["o_a"], s["o_b"], s["o_m"], s["ya"], s["yb"], s["ym"], p["gdn_norm_g"],
        p["b_merge"], p["w_branch"], p["w_out"], hsum, tss)
    g = {}
    g["b_merge"] = db_merge[0]
    g["gdn_norm_g"] = dgdn_g[0]
    g["w_out"] = _mm("dw_out", s["merged"], dout, ta=True)
    g["w_branch"] = jnp.stack([_mm("dw_branch", y, dp, ta=True)
                               for y, dp in ((s["ya"], dpa), (s["yb"], dpb), (s["ym"], dpm))])

    do_h = _heads_major(do_a, FOX_HEADS, FOX_DIM).astype(BF16)
    delta_row = deltaT[:FOX_HEADS, None, :]
    dqT, dk_h, dv_h, dfk, dfq = _fox_bwd(s["qs"], s["k_aug"], s["kT_aug"], s["v_aug"], do_h, s["lse"],
                                         delta_row, tq)
    daq = _heads_minor(dqT.transpose(0, 2, 1))
    dak = _heads_minor(dk_h)
    dav = _heads_minor(dv_h)
    daf128, db_fg = _fox_decay_bwd(dfk[:, 0, :], dfq[:, 0, :], z, s["b_fg128"], ts)
    g["b_fg"] = db_fg[:FOX_HEADS]

    du, dw, dqd, dkd, daqk, dgl = _gdn_scan_bwd(do_b, s["w"], s["qd"], s["kd"], s["aqk"], s["vn"],
                                                s["states"], s["gb"], ts)
    dqkv, dgb = _gdn_local_bwd(s["qkv"], s["gb"], s["grow"], s["T"], du, dw, dqd, dkd, daqk, dgl, ts)
    dz, dzs_b, dconv, dad = _gdn_prep_bwd(dqkv, dgb, s["cpre"], z, p["conv_w"], s["a128"],
                                          s["dt128"], dz, ts)
    g["conv_w"] = dconv[:4]
    g["a_log"] = dad[0, LANE_BA:LANE_BA + GDN_HEADS]
    g["dt_bias"] = dad[1, LANE_BA:LANE_BA + GDN_HEADS]

    dz, dmk, dmv = _mem_attn_bwd(do_m, z, s["mk"], s["mv"], dz, ts)
    dmkv = jnp.concatenate([dmk, dmv], axis=1)
    g["w_mem_kv"] = _mm("dw_mem_kv", s["mem_h"], dmkv, ta=True)
    dmem_h = _mm("dmem_h", dmkv, p["w_mem_kv"], tb=True)
    M = mem.shape[0]
    _, g["mem_norm_g"] = _rms_bwd("mem_norm_bwd", dmem_h, mem, s["mem_r"], p["mem_norm_g"],
                                  jnp.zeros_like(mem), M)

    lane = jnp.arange(128)[None, :]
    dsmall = jnp.where(lane < 8, daf128, dzs_b)
    daqkv = jnp.concatenate([_b16(daq), _b16(dak), _b16(dav)], axis=1)
    dz = lax.dynamic_update_slice(dz, daqkv, (0, CB_AQKV * 3 * WIDTH))
    dz = lax.dynamic_update_slice(dz, _b16(dsmall), (0, CB_SMALL * 128))
    g["w_in_al"] = _mm("dw_in", s["h"], dz, ta=True, tn=1664, tk=2048)
    dh = _mm("dh", dz, p["w_in_al"], tb=True, tm=2048, tk=1664)
    dx, g["norm_g"] = _rms_bwd("norm_bwd", dh, s["x"], s["rstd"], p["norm_g"], dout, ts)
    return dx, g


def _local_step(x, mem, layers, final_norm_g, loss_target):
    S = x.shape[0]
    saves = []
    cur = x
    for p in layers:
        cur, sv = _layer_fwd(cur, mem, p)
        saves.append(sv)
    dx, dgf, loss_lanes = _loss_head(cur, final_norm_g, loss_target, _tiles(S)["ts"])
    grads = [None] * len(layers)
    for l in reversed(range(len(layers))):
        dx, grads[l] = _layer_bwd(dx, mem, layers[l], saves[l])
    return loss_lanes, dx, grads, dgf


HBM_SPEC = pl.BlockSpec(memory_space=pltpu.HBM)


def _mesh_pos():
    return lax.axis_index("x"), lax.axis_index("y"), lax.axis_index("c")


def _comm_call(name, body, arrays, out_shapes, n_remote, n_local):
    n = len(arrays)

    def kern(*refs):
        body(refs[:n], refs[n:2 * n], refs[2 * n], refs[2 * n + 1], refs[2 * n + 2])

    return pl.pallas_call(
        kern, name=name, out_shape=out_shapes, in_specs=[HBM_SPEC] * n, out_specs=[HBM_SPEC] * n,
        scratch_shapes=[pltpu.SemaphoreType.DMA((n_remote,)), pltpu.SemaphoreType.DMA((n_remote,)),
                        pltpu.SemaphoreType.DMA((max(n_local, 1),))],
    )(*arrays)


def _remote(src, dst, send_sems, recv_sems, k, to):
    return pltpu.make_async_remote_copy(src_ref=src, dst_ref=dst, send_sem=send_sems.at[k],
                                        recv_sem=recv_sems.at[k], device_id=to, device_id_type=MESH_ID)


def _other_chips(mx, my):
    return [(1 - mx, my), (mx, 1 - my), (1 - mx, 1 - my)]


def _gather_chips(name, shards):
    n = len(shards)

    def body(ins, outs, send_sems, recv_sems, local_sems):
        mx, my, mc = _mesh_pos()
        me = 2 * mx + my
        sibling = (mx, my, 1 - mc)
        chips = _other_chips(mx, my)
        sends = []
        for a in range(n):
            for k, (px, py) in enumerate(chips):
                cp = _remote(ins[a].at[mc], outs[a].at[me, mc], send_sems, recv_sems, 6 * a + k,
                             (px, py, mc))
                cp.start()
                sends.append(cp)
        for a in range(n):
            for k, (px, py) in enumerate(chips):
                j = 2 * px + py
                _remote(ins[a].at[mc], outs[a].at[j, mc], send_sems, recv_sems, 6 * a + k,
                        (px, py, mc)).wait_recv()
                cp = _remote(outs[a].at[j, mc], outs[a].at[j, mc], send_sems, recv_sems, 6 * a + 3 + k,
                             sibling)
                cp.start()
                sends.append(cp)
        for a in range(n):
            for k, (px, py) in enumerate(chips):
                j = 2 * px + py
                _remote(outs[a].at[j, 1 - mc], outs[a].at[j, 1 - mc], send_sems, recv_sems,
                        6 * a + 3 + k, sibling).wait_recv()
        for cp in sends:
            cp.wait_send()

    shapes = [jax.ShapeDtypeStruct((N_CHIPS,) + s.shape, s.dtype) for s in shards]
    outs = _comm_call(name, body, shards, shapes, 6 * n, 0)
    me = 2 * lax.axis_index("x") + lax.axis_index("y")
    return [lax.dynamic_update_index_in_dim(o, s, me, 0) for o, s in zip(outs, shards)]


def _sibling_swap(gs):
    n = len(gs)

    def body(ins, outs, send_sems, recv_sems, local_sems):
        mx, my, mc = _mesh_pos()
        sends = []
        for a in range(n):
            cp = _remote(ins[a].at[:, 1 - mc], outs[a], send_sems, recv_sems, a, (mx, my, 1 - mc))
            cp.start()
            sends.append(cp)
        for cp in sends:
            cp.wait()

    shapes = [jax.ShapeDtypeStruct((g.shape[0],) + g.shape[2:], g.dtype) for g in gs]
    return _comm_call("grad_sibling_swap", body, gs, shapes, n, 0)


def _chip_exchange(ps):
    n = len(ps)

    def body(ins, outs, send_sems, recv_sems, local_sems):
        mx, my, mc = _mesh_pos()
        me = 2 * mx + my
        chips = _other_chips(mx, my)
        sends = []
        for a in range(n):
            for k, (px, py) in enumerate(chips):
                cp = _remote(ins[a].at[2 * px + py], outs[a].at[me], send_sems, recv_sems, 3 * a + k,
                             (px, py, mc))
                cp.start()
                sends.append(cp)
        for a in range(n):
            for k, (px, py) in enumerate(chips):
                _remote(ins[a].at[me], outs[a].at[2 * px + py], send_sems, recv_sems, 3 * a + k,
                        (px, py, mc)).wait_recv()
        for cp in sends:
            cp.wait_send()

    shapes = [jax.ShapeDtypeStruct(p.shape, p.dtype) for p in ps]
    outs = _comm_call("grad_chip_exchange", body, ps, shapes, 3 * n, 0)
    me = 2 * lax.axis_index("x") + lax.axis_index("y")
    return [lax.dynamic_update_index_in_dim(o, lax.dynamic_index_in_dim(p, me, 0, keepdims=False), me, 0)
            for o, p in zip(outs, ps)]


def _sibling_gather(hs):
    n = len(hs)

    def body(ins, outs, send_sems, recv_sems, local_sems):
        mx, my, mc = _mesh_pos()
        sends = []
        for a in range(n):
            cp = _remote(ins[a], outs[a], send_sems, recv_sems, a, (mx, my, 1 - mc))
            cp.start()
            sends.append(cp)
        for cp in sends:
            cp.wait()

    shapes = [jax.ShapeDtypeStruct(h.shape, h.dtype) for h in hs]
    theirs = _comm_call("grad_sibling_gather", body, hs, shapes, n, 0)
    first = lax.axis_index("c") == 0
    return [jnp.stack([jnp.where(first, h, t), jnp.where(first, t, h)]) for h, t in zip(hs, theirs)]


def _add_pairs(a, b, tr, out_dtype):
    n, H, C = a.shape

    def kern(a_ref, b_ref, o_ref):
        o_ref[...] = (a_ref[...] + b_ref[...]).astype(o_ref.dtype)

    spec = pl.BlockSpec((None, tr, C), lambda j, i: (j, i, 0))
    return pl.pallas_call(
        kern, name="grad_pair_sum", grid=(n, H // tr), in_specs=[spec, spec], out_specs=spec,
        out_shape=jax.ShapeDtypeStruct((n, H, C), out_dtype),
        compiler_params=_params(("parallel", "parallel")),
    )(a, b)


def _sum_slots(r4, tr):
    n, H, C = r4.shape

    def kern(r_ref, o_ref):
        f = lambda k: r_ref[k].astype(F32)
        o_ref[...] = ((f(0) + f(1)) + f(2)) + f(3)

    return pl.pallas_call(
        kern, name="grad_chip_sum", grid=(H // tr,),
        in_specs=[pl.BlockSpec((n, tr, C), lambda i: (0, i, 0))],
        out_specs=pl.BlockSpec((tr, C), lambda i: (i, 0)),
        out_shape=jax.ShapeDtypeStruct((H, C), F32),
        compiler_params=_params(("parallel",)),
    )(r4)


def _adamw(w, g, m, v, tr):
    R, C = w.shape
    c1 = 1.0 - ADAM_B1
    c2 = 1.0 - ADAM_B2
    bc1 = 1.0 - ADAM_B1 ** ADAM_STEP
    bc2 = 1.0 - ADAM_B2 ** ADAM_STEP

    def kern(w_ref, g_ref, m_ref, v_ref, d_ref, mo_ref, vo_ref):
        gv = g_ref[...]
        mn = ADAM_B1 * m_ref[...] + c1 * gv
        vn = ADAM_B2 * v_ref[...] + c2 * (gv * gv)
        m_hat = mn / bc1
        v_hat = vn / bc2
        d_ref[...] = -ADAM_LR * (m_hat / (jnp.sqrt(v_hat) + ADAM_EPS) + ADAM_WD * w_ref[...])
        mo_ref[...] = mn
        vo_ref[...] = vn

    spec = pl.BlockSpec((tr, C), lambda i: (i, 0))
    shape = jax.ShapeDtypeStruct((R, C), F32)
    return pl.pallas_call(
        kern, name="adamw", grid=(R // tr,), in_specs=[spec] * 4, out_specs=[spec] * 3,
        out_shape=[shape] * 3, compiler_params=_params(("parallel",)),
    )(w, g, m, v)


PACK_COLS = 1024
PACK_ROWS = 512
W_SHARD = N_IN // N_CHIPS
SLAB = ("conv_w", "w_mem_kv", "w_branch", "w_out")
SMALL =("norm_g", "b_fg", "b_merge", "a_log", "dt_bias", "gdn_norm_g", "mem_norm_g", "final_norm_g")
ALL_WEIGHTS = ("norm_g", "w_in", "b_fg", "b_merge", "conv_w", "a_log", "dt_bias", "gdn_norm_g",
               "mem_norm_g", "w_mem_kv", "w_branch", "w_out", "final_norm_g")
SHARD_AXIS = {"w_in": 2, "conv_w": 2, "w_mem_kv": 1, "w_branch": 3, "w_out": 1}


def _pack(arrays, row_multiple):
    flat = jnp.concatenate([a.reshape(-1) for a in arrays])
    n = flat.shape[0]
    rows = -(-n // PACK_COLS)
    rows = -(-rows // row_multiple) * row_multiple
    flat = jnp.pad(flat, (0, rows * PACK_COLS - n))
    return flat.reshape(rows, PACK_COLS)


def _unpack(slab, shapes):
    out, off = [], 0
    for shp in shapes:
        n = 1
        for d in shp:
            n *= d
        r0, r1 = off // PACK_COLS, -(-(off + n) // PACK_COLS)
        rows = slab[r0:r1].reshape(-1)
        out.append(rows[off - r0 * PACK_COLS:off - r0 * PACK_COLS + n].reshape(shp))
        off += n
    return out


def _shard_of(full, name, j):
    ax = SHARD_AXIS[name]
    n = full.shape[ax] // N_CHIPS
    return lax.slice_in_dim(full, j * n, (j + 1) * n, axis=ax)


def _aligned_from_shards(shards):
    def cols(lo, n):
        parts = []
        while n > 0:
            j, off = divmod(lo, W_SHARD)
            take = min(n, W_SHARD - off)
            parts.append(shards[j][..., off:off + take])
            lo, n = lo + take, n - take
        return parts

    out = []
    for lo, n, _ in sorted(W_IN_PIECES, key=lambda p: p[2]):
        out += cols(lo, n)
    out.append(jnp.zeros(shards[0].shape[:-1] + (N_AL - N_IN,), shards[0].dtype))
    return jnp.concatenate(out, axis=-1)


def _shard_from_aligned(w_al, j):
    lo_j, hi_j = j * W_SHARD, (j + 1) * W_SHARD
    parts = []
    for lo, n, al in W_IN_PIECES:
        a, b = max(lo, lo_j), min(lo + n, hi_j)
        if a < b:
            parts.append(w_al[..., al + a - lo:al + b - lo])
    return jnp.concatenate(parts, axis=-1)


def kernel(x, mem, norm_g, w_in, b_fg, b_merge, conv_w, a_log, dt_bias, gdn_norm_g, mem_norm_g, w_mem_kv, w_branch, w_out, final_norm_g, loss_target, m_norm_g, m_w_in, m_b_fg, m_b_merge, m_conv_w, m_a_log, m_dt_bias, m_gdn_norm_g, m_mem_norm_g, m_w_mem_kv, m_w_branch, m_w_out, m_final_norm_g, v_norm_g, v_w_in, v_b_fg, v_b_merge, v_conv_w, v_a_log, v_dt_bias, v_gdn_norm_g, v_mem_norm_g, v_w_mem_kv, v_w_branch, v_w_out, v_final_norm_g):
    wts = dict(norm_g=norm_g, w_in=w_in, b_fg=b_fg, b_merge=b_merge, conv_w=conv_w, a_log=a_log,
               dt_bias=dt_bias, gdn_norm_g=gdn_norm_g, mem_norm_g=mem_norm_g, w_mem_kv=w_mem_kv,
               w_branch=w_branch, w_out=w_out, final_norm_g=final_norm_g)
    mom = dict(norm_g=m_norm_g, w_in=m_w_in, b_fg=m_b_fg, b_merge=m_b_merge, conv_w=m_conv_w,
               a_log=m_a_log, dt_bias=m_dt_bias, gdn_norm_g=m_gdn_norm_g, mem_norm_g=m_mem_norm_g,
               w_mem_kv=m_w_mem_kv, w_branch=m_w_branch, w_out=m_w_out, final_norm_g=m_final_norm_g)
    vel = dict(norm_g=v_norm_g, w_in=v_w_in, b_fg=v_b_fg, b_merge=v_b_merge, conv_w=v_conv_w,
               a_log=v_a_log, dt_bias=v_dt_bias, gdn_norm_g=v_gdn_norm_g, mem_norm_g=v_mem_norm_g,
               w_mem_kv=v_w_mem_kv, w_branch=v_w_branch, w_out=v_w_out, final_norm_g=v_final_norm_g)

    big = ("w_in", "w_mem_kv", "w_branch", "w_out")
    gathered = _gather_chips("weight_gather", [wts[n].astype(BF16) for n in big] + [conv_w])
    all_w = dict(zip(big + ("conv_w",), gathered))
    w_in_al = _aligned_from_shards([all_w["w_in"][j] for j in range(N_CHIPS)])

    layers = []
    for l in range(DEPTH):
        rows_of = lambda n: all_w[n][:, l].reshape(D_MODEL, D_MODEL)
        last_of = lambda n: jnp.concatenate([all_w[n][j, l] for j in range(N_CHIPS)], axis=-1)
        layers.append(dict(norm_g=norm_g[l], w_in_al=w_in_al[l], b_fg=b_fg[l], b_merge=b_merge[l],
                           conv_w=jnp.pad(last_of("conv_w"), ((0, 4), (0, 0))), a_log=a_log[l],
                           dt_bias=dt_bias[l], gdn_norm_g=gdn_norm_g[l], mem_norm_g=mem_norm_g[l],
                           w_mem_kv=rows_of("w_mem_kv"), w_branch=last_of("w_branch"),
                           w_out=rows_of("w_out")))

    loss_lanes, dx, grads, dgf = _local_step(x[0], mem[0], layers, final_norm_g, loss_target[0])

    gfull = {n: jnp.stack([grads[l][n] for l in range(DEPTH)])
             for n in ("norm_g", "b_fg", "b_merge", "conv_w", "a_log", "dt_bias", "gdn_norm_g",
                       "mem_norm_g", "w_mem_kv", "w_branch", "w_out")}
    gfull["final_norm_g"] = dgf
    loss_local = jnp.sum(loss_lanes).reshape(1)
    small_g = [gfull[n] for n in SMALL] + [loss_local]
    dw_al = jnp.stack([grads[l]["w_in_al"] for l in range(DEPTH)])
    ga = jnp.stack([_shard_from_aligned(dw_al, j) for j in range(N_CHIPS)])
    mats = ("w_mem_kv", "w_branch", "w_out")
    rest = ("conv_w",) + SMALL
    gb = jnp.stack([_pack([_shard_of(gfull[n], n, j) for n in mats], PACK_ROWS)
                    for j in range(N_CHIPS)])
    gc = jnp.stack([_pack([_shard_of(gfull["conv_w"], "conv_w", j)] + small_g, 16)
                    for j in range(N_CHIPS)])
    halves = lambda g: g.reshape(N_CHIPS, 2, g.shape[1] // 2, PACK_COLS)
    gb, gc = halves(gb), halves(gc)

    mc = lax.axis_index("c")
    tr = 256
    trs = (tr, tr, 8)
    from_sibling = _sibling_swap([ga, gb, gc])
    mine = [lax.dynamic_index_in_dim(g, mc, axis=1, keepdims=False) for g in (ga, gb, gc)]
    pair = [_add_pairs(a, b, t, dt) for a, b, t, dt in zip(mine, from_sibling, trs, (BF16, BF16, F32))]
    slots = _chip_exchange(pair)
    half = [_sum_slots(s, t) for s, t in zip(slots, trs)]
    ga_sum, gb_sum, gc_sum = _sibling_gather(half)
    flat = lambda g: g.reshape(-1, PACK_COLS)

    g_un = dict(zip(mats, _unpack(flat(gb_sum), [wts[n].shape for n in mats])))
    g_un.update(zip(rest + ("loss",), _unpack(flat(gc_sum), [wts[n].shape for n in rest] + [(1,)])))
    g_un["w_in"] = ga_sum
    d_un, m_un, v_un = {}, {}, {}
    rows2d = lambda a: a.reshape(-1, a.shape[-1])
    for n in ("w_in", "w_mem_kv", "w_branch", "w_out"):
        res = _adamw(rows2d(wts[n]), rows2d(g_un[n]), rows2d(mom[n]), rows2d(vel[n]), tr)
        d_un[n], m_un[n], v_un[n] = [r.reshape(wts[n].shape) for r in res]
    little = ("conv_w",) + SMALL
    slab = lambda d: _pack([d[n] for n in little], 8)
    res = _adamw(slab(wts), slab(g_un), slab(mom), slab(vel), 8)
    little_shapes = [wts[n].shape for n in little]
    for out, r in zip((d_un, m_un, v_un), res):
        out.update(zip(little, _unpack(r, little_shapes)))

    loss = g_un["loss"][0]
    return (loss, dx[None], *[g_un[n] for n in ALL_WEIGHTS], *[d_un[n] for n in ALL_WEIGHTS],
            *[m_un[n] for n in ALL_WEIGHTS], *[v_un[n] for n in ALL_WEIGHTS])
```

```python
import functools

import jax
import jax.numpy as jnp
from jax import lax
from jax.experimental import pallas as pl
from jax.experimental.pallas import tpu as pltpu

F32 = jnp.float32
BF16 = jnp.bfloat16
HIGHEST = lax.Precision.HIGHEST
PREC_UT = lax.Precision.HIGH
MESH_ID = pl.DeviceIdType.MESH

D_MODEL = 1024
DEPTH = 2
CHUNK = 64
EPS = 1e-6
FOX_HEADS, FOX_DIM = 8, 64
GDN_HEADS, GDN_DIM = 4, 128
MEM_HEADS, MEM_DIM = 4, 128
WIDTH = 512
N_BRANCH = 3
N_IN = 8208
N_AL = 8320
N_CHIPS = 4
NEG = -1e30
LOG2E = 1.4426950408889634
LN2 = 0.6931471805599453

ADAM_LR, ADAM_B1, ADAM_B2, ADAM_EPS, ADAM_WD, ADAM_STEP = 0.001, 0.9, 0.999, 1e-08, 0.01, 10

CB_GATES = 0
CB_AZ, CB_BZ, CB_MZ = 6, 7, 8
CB_MERGE = 0
CB_BQKV = 3
CB_AQ, CB_AK, CB_AV = 12, 13, 14
CB_AQKV = 4
CB_MQ = 15
CB_SMALL = 64
W_IN_PIECES = ((0, 512, 6144), (512, 512, 6656), (1024, 512, 7168), (1536, 8, 8192), (1544, 512, 3072),
               (2056, 512, 4608), (2568, 512, 5120), (3080, 512, 5632), (3592, 4, 8200), (3596, 4, 8204),
               (3600, 512, 3584), (4112, 512, 7680), (4624, 512, 4096), (5136, 3072, 0))
LANE_AF, LANE_BA, LANE_BB = 0, 8, 12

NN = ((1,), (0,))
NT = ((1,), (1,))
TN = ((0,), (0,))

VMEM_LIMIT_BYTES = 56 * 1024 * 1024


def _dot(a, b, dims=NN, prec=None):
    return lax.dot_general(a, b, (dims, ((), ())), preferred_element_type=F32, precision=prec)


def _bdot(a, b, ca, cb, prec=None):
    return lax.dot_general(a, b, (((ca,), (cb,)), ((0,), (0,))), preferred_element_type=F32,
                           precision=prec)


def _b16(a):
    return a.astype(BF16)


def _eye(n, dtype=F32):
    r = lax.broadcasted_iota(jnp.int32, (n, n), 0)
    c = lax.broadcasted_iota(jnp.int32, (n, n), 1)
    return jnp.where(r == c, 1.0, 0.0).astype(dtype)


def _transpose_exact(x):
    return _dot(_eye(x.shape[1]), x, NT, HIGHEST)


def _col_to_row(col):
    n = col.shape[0]
    return jnp.sum(jnp.where(_eye(n) > 0.5, col, 0.0), axis=0, keepdims=True)


def _row_to_col(row):
    n = row.shape[1]
    return jnp.sum(jnp.where(_eye(n) > 0.5, row, 0.0), axis=1, keepdims=True)


def _sigmoid(x):
    return 1.0 / (1.0 + jnp.exp(-x))


def _softplus(x):
    return jnp.maximum(x, 0.0) + jnp.log(1.0 + jnp.exp(-jnp.abs(x)))


def _silu_and_grad(x):
    s = _sigmoid(x)
    return x * s, s * (1.0 + x * (1.0 - s))


def _params(semantics):
    return pltpu.CompilerParams(dimension_semantics=semantics, vmem_limit_bytes=VMEM_LIMIT_BYTES)


def _rows(a, ts):
    nd = a.ndim
    return (a, (ts,) + a.shape[1:], lambda i, nd=nd: (i,) + (0,) * (nd - 1))


def _cols(a, ts, width, cb):
    return (a, (ts, width), lambda i, cb=cb: (i, cb))


def _full(a):
    nd = a.ndim
    return (a, a.shape, lambda i, nd=nd: (0,) * nd)


def _orow(S, tail, dtype, ts):
    nd = 1 + len(tail)
    return ((S,) + tuple(tail), dtype, (ts,) + tuple(tail), lambda i, nd=nd: (i,) + (0,) * (nd - 1))


def _oacc(shape, dtype):
    nd = len(shape)
    return (tuple(shape), dtype, tuple(shape), lambda i, nd=nd: (0,) * nd)


def _tiled(name, body, n_steps, ins, outs, scratch=(), reverse=False, fill=None):
    def rev(imap):
        if not reverse:
            return imap
        return lambda i: imap(n_steps - 1 - i)

    in_specs = [pl.BlockSpec(blk, rev(imap)) for (_, blk, imap) in ins]
    out_specs = [pl.BlockSpec(blk, rev(imap)) for (_, _, blk, imap) in outs]
    out_shape = [jax.ShapeDtypeStruct(shape, dt) for (shape, dt, _, _) in outs]
    n_in, n_out = len(ins), len(outs)
    arrays = [a for (a, _, _) in ins]
    aliases = {}
    n_extra = 0
    if fill is not None:
        arrays.append(fill[0])
        in_specs.append(pl.BlockSpec(memory_space=pl.ANY))
        aliases = {n_in: fill[1]}
        n_extra = 1

    def kern(*refs):
        step = pl.program_id(0)
        t = (n_steps - 1 - step) if reverse else step
        lo = n_in + n_extra
        body(t, step == 0, refs[:n_in], refs[lo:lo + n_out], refs[lo + n_out:])

    res = pl.pallas_call(
        kern, name=name, grid=(n_steps,), in_specs=in_specs, out_specs=out_specs,
        out_shape=out_shape, scratch_shapes=list(scratch), input_output_aliases=aliases,
        compiler_params=_params(("arbitrary",)),
    )(*arrays)
    return res


def _pick(n, pref):
    if n <= pref:
        return n
    best = None
    for t in range(128, pref + 1, 128):
        if n % t == 0:
            best = t
    assert best is not None, (n, pref)
    return best


def _mm(name, a, b, ta=False, tb=False, out_dtype=F32, tm=1024, tn=1024, tk=1024):
    if ta:
        K, M = a.shape
    else:
        M, K = a.shape
    if tb:
        N, K2 = b.shape
    else:
        K2, N = b.shape
    assert K == K2, (a.shape, b.shape, ta, tb)
    tm, tn, tk = _pick(M, tm), _pick(N, tn), _pick(K, tk)
    nk = K // tk
    a_spec = (pl.BlockSpec((tk, tm), lambda i, j, k: (k, i)) if ta
              else pl.BlockSpec((tm, tk), lambda i, j, k: (i, k)))
    b_spec = (pl.BlockSpec((tn, tk), lambda i, j, k: (j, k)) if tb
              else pl.BlockSpec((tk, tn), lambda i, j, k: (k, j)))
    dims = ((0,) if ta else (1,), (1,) if tb else (0,))

    def kern_single(a_ref, b_ref, o_ref):
        o_ref[...] = _dot(_b16(a_ref[...]), _b16(b_ref[...]), dims).astype(o_ref.dtype)

    def kern_acc(a_ref, b_ref, o_ref, acc_ref):
        k = pl.program_id(2)

        @pl.when(k == 0)
        def _():
            acc_ref[...] = jnp.zeros_like(acc_ref)

        acc_ref[...] += _dot(_b16(a_ref[...]), _b16(b_ref[...]), dims)

        @pl.when(k == nk - 1)
        def _():
            o_ref[...] = acc_ref[...].astype(o_ref.dtype)

    return pl.pallas_call(
        kern_single if nk == 1 else kern_acc, name=name, grid=(M // tm, N // tn, nk),
        in_specs=[a_spec, b_spec],
        out_specs=pl.BlockSpec((tm, tn), lambda i, j, k: (i, j)),
        out_shape=jax.ShapeDtypeStruct((M, N), out_dtype),
        scratch_shapes=[] if nk == 1 else [pltpu.VMEM((tm, tn), F32)],
        compiler_params=_params(("parallel", "parallel", "arbitrary")),
    )(a, b)


def _rms_fwd(name, x, g, ts):
    S, D = x.shape

    def body(t, first, ins, outs, scratch):
        x_ref, g_ref = ins
        h_ref, r_ref = outs
        xv = x_ref[...]
        r = lax.rsqrt(jnp.mean(xv * xv, axis=1, keepdims=True) + EPS)
        h_ref[...] = (xv * r * g_ref[...]).astype(h_ref.dtype)
        r_ref[...] = r

    return _tiled(name, body, S // ts, [_rows(x, ts), _full(g.reshape(1, D))],
                  [_orow(S, (D,), BF16, ts), _orow(S, (1,), F32, ts)])


def _rms_bwd(name, dh, x, rstd, g, dres, ts):
    S, D = x.shape

    def body(t, first, ins, outs, scratch):
        dh_ref, x_ref, r_ref, g_ref, dres_ref = ins
        dx_ref, dg_ref = outs
        r = r_ref[...]
        xh = x_ref[...] * r
        dhv = dh_ref[...]
        dxh = dhv * g_ref[...]
        dx_ref[...] = dres_ref[...] + r * (dxh - xh * jnp.mean(dxh * xh, axis=1, keepdims=True))

        @pl.when(first)
        def _():
            dg_ref[...] = jnp.zeros_like(dg_ref)

        dg_ref[0:1, :] += jnp.sum(dhv * xh, axis=0, keepdims=True)

    dx, dg = _tiled(name, body, S // ts,
                    [_rows(dh, ts), _rows(x, ts), _rows(rstd, ts), _full(g.reshape(1, D)), _rows(dres, ts)],
                    [_orow(S, (D,), F32, ts), _oacc((8, D), F32)])
    return dx, dg[0]


def _loss_head(x, g, target, ts):
    S, D = x.shape

    def body(t, first, ins, outs, scratch):
        x_ref, g_ref, tgt_ref = ins
        dx_ref, dg_ref, loss_ref = outs
        xv = x_ref[...]
        gv = g_ref[...]
        r = lax.rsqrt(jnp.mean(xv * xv, axis=1, keepdims=True) + EPS)
        xh = xv * r
        err = xh * gv - tgt_ref[...]
        dy = err * (1.0 / D)
        dxh = dy * gv
        dx_ref[...] = r * (dxh - xh * jnp.mean(dxh * xh, axis=1, keepdims=True))

        @pl.when(first)
        def _():
            dg_ref[...] = jnp.zeros_like(dg_ref)
            loss_ref[...] = jnp.zeros_like(loss_ref)

        dg_ref[0:1, :] += jnp.sum(dy * xh, axis=0, keepdims=True)
        per_lane = jnp.sum(err * err, axis=0, keepdims=True)
        loss_ref[0:1, :] += per_lane * (0.5 / D)

    dx, dg, loss = _tiled("loss_head", body, S // ts,
                          [_rows(x, ts), _full(g.reshape(1, D)), _rows(target, ts)],
                          [_orow(S, (D,), F32, ts), _oacc((8, D), F32), _oacc((8, D), F32)])
    return dx, dg[0], loss[0]


def _scan_rows(x, length, seg, reverse=False):
    row = lax.broadcasted_iota(jnp.int32, x.shape, 0) % seg
    k = 1
    while k < seg:
        if reverse:
            x = x + jnp.where(row < seg - k, pltpu.roll(x, length - k, 0), 0.0)
        else:
            x = x + jnp.where(row >= k, pltpu.roll(x, k, 0), 0.0)
        k *= 2
    return x


def _fox_decay(z, b_fg128, ts):
    S = z.shape[0]

    def body(t, first, ins, outs, scratch):
        zs_ref, b_ref = ins
        hi_ref, mid_ref, lo_ref = outs
        (carry,) = scratch

        @pl.when(first)
        def _():
            carry[...] = jnp.zeros_like(carry)

        logf = -_softplus(-(zs_ref[...] + b_ref[...]))
        run = _scan_rows(logf, ts, ts) + carry[0:1, :]
        carry[0:1, :] = run[ts - 1:ts, :]
        f2 = run * LOG2E
        hi = f2.astype(BF16)
        r1 = f2 - hi.astype(F32)
        mid = r1.astype(BF16)
        lo = (r1 - mid.astype(F32)).astype(BF16)
        eye = _eye(128, BF16)
        hi_ref[...] = _dot(eye, hi, NT).astype(BF16)
        mid_ref[...] = _dot(eye, mid, NT).astype(BF16)
        lo_ref[...] = _dot(eye, lo, NT).astype(BF16)

    tcol = lambda dt: ((128, S), dt, (128, ts), lambda i: (0, i))
    return _tiled("fox_decay", body, S // ts,
                  [_cols(z, ts, 128, CB_SMALL), _full(b_fg128)],
                  [tcol(BF16), tcol(BF16), tcol(BF16)], scratch=[pltpu.VMEM((8, 128), F32)])


def _fox_decay_bwd(dfk_rows, dfq_rows, z, b_fg128, ts):
    S = z.shape[0]
    H = dfk_rows.shape[0]

    def body(t, first, ins, outs, scratch):
        dfk_ref, dfq_ref, zs_ref, b_ref = ins
        daf_ref, db_ref = outs
        (carry,) = scratch

        @pl.when(first)
        def _():
            carry[...] = jnp.zeros_like(carry)
            db_ref[...] = jnp.zeros_like(db_ref)

        r = lax.broadcasted_iota(jnp.int32, (H, 128), 0)
        c = lax.broadcasted_iota(jnp.int32, (H, 128), 1)
        place = jnp.where(r == c, 1.0, 0.0)
        df = _dot(dfk_ref[...] + dfq_ref[...], place, TN, HIGHEST)
        run = _scan_rows(df, ts, ts, reverse=True) + carry[0:1, :]
        carry[0:1, :] = run[0:1, :]
        daf = run * _sigmoid(-(zs_ref[...] + b_ref[...]))
        daf_ref[...] = daf
        db_ref[0:1, :] += jnp.sum(daf, axis=0, keepdims=True)

    rowsin = lambda a: (a, (H, ts), lambda i: (0, i))
    daf, db = _tiled("fox_decay_bwd", body, S // ts,
                     [rowsin(dfk_rows), rowsin(dfq_rows), _cols(z, ts, 128, CB_SMALL), _full(b_fg128)],
                     [_orow(S, (128,), F32, ts), _oacc((8, 128), F32)],
                     scratch=[pltpu.VMEM((8, 128), F32)], reverse=True)
    return daf, db[0]


FOX_AUG = 80


def _fox_fwd(q_aug, kT_aug, v_aug, tq):
    H, S, da = q_aug.shape
    dv = v_aug.shape[2]
    d = FOX_DIM
    tk = tq // 2
    qscale = (d ** -0.5) * LOG2E

    def kern(q_ref, kT_ref, v_ref, o_ref, lse_ref, qs_ref, s_buf, p_buf, m_scr, acc_scr):
        i = pl.program_id(1)
        col = lax.broadcasted_iota(jnp.int32, (1, da), 1)
        qb = _b16(q_ref[...] * jnp.where(col < d, qscale, 1.0))
        qs_ref[...] = qb

        def keys(t):
            return pl.ds(pl.multiple_of(t * tk, tk), tk)

        def stage(t, slot, mask_off, look_ahead):
            if look_ahead:
                s_buf[1 - slot] = _dot(qb, kT_ref[:, keys(t + 1)])
            pv = _dot(p_buf[1 - slot], v_ref[keys(jnp.maximum(t - 1, 0)), :])

            def scores():
                s = s_buf[slot]
                if mask_off is None:
                    return s
                r = lax.broadcasted_iota(jnp.int32, (tq, tk), 0)
                c = lax.broadcasted_iota(jnp.int32, (tq, tk), 1)
                return jnp.where(c + mask_off <= r, s, NEG)

            m = m_scr[...]
            m_new = jnp.maximum(m, jnp.max(scores(), axis=1, keepdims=True))
            alpha = jnp.exp2(m - m_new)
            p_buf[slot] = _b16(jnp.exp2(scores() - m_new))
            m_scr[...] = m_new
            acc_scr[...] = (acc_scr[...] + pv) * alpha

        s_buf[0] = _dot(qb, kT_ref[:, keys(0)])
        p_buf[1] = jnp.zeros((tq, tk), BF16)
        m_scr[...] = jnp.full((tq, 1), NEG, F32)
        acc_scr[...] = jnp.zeros((tq, dv), F32)

        def pair(n):
            stage(2 * n, 0, None, True)
            stage(2 * n + 1, 1, None, True)

        def quad(m, _):
            pair(2 * m)
            pair(2 * m + 1)
            return 0

        lax.fori_loop(0, i // 2, quad, 0)

        @pl.when(i % 2 == 1)
        def _():
            pair(i - 1)

        stage(2 * i, 0, 0, True)
        stage(2 * i + 1, 1, tk, False)
        acc = acc_scr[...] + _dot(p_buf[1], v_ref[keys(2 * i + 1), :])
        l = acc[:, d:d + 1]
        o_ref[...] = acc[:, :d] / l
        lse_ref[...] = _col_to_row(m_scr[...] + jnp.log(l) * LOG2E)

    return pl.pallas_call(
        kern, name="fox_fwd", grid=(H, S // tq),
        in_specs=[pl.BlockSpec((None, tq, da), lambda h, i: (h, i, 0)),
                  pl.BlockSpec((None, da, S), lambda h, i: (h, 0, 0)),
                  pl.BlockSpec((None, S, dv), lambda h, i: (h, 0, 0))],
        out_specs=[pl.BlockSpec((None, tq, d), lambda h, i: (h, i, 0)),
                   pl.BlockSpec((None, 1, tq), lambda h, i: (h, 0, i)),
                   pl.BlockSpec((None, tq, da), lambda h, i: (h, i, 0))],
        out_shape=[jax.ShapeDtypeStruct((H, S, d), F32), jax.ShapeDtypeStruct((H, 1, S), F32),
                   jax.ShapeDtypeStruct((H, S, da), BF16)],
        scratch_shapes=[pltpu.VMEM((2, tq, tk), F32), pltpu.VMEM((2, tq, tk), BF16),
                        pltpu.VMEM((tq, 1), F32), pltpu.VMEM((tq, dv), F32)],
        compiler_params=_params(("parallel", "arbitrary")),
    )(q_aug, kT_aug, v_aug)


def _fox_bwd(qs, k_aug, kT, v, do, lse_row, delta_row, tq):
    H, S, da = qs.shape
    d = FOX_DIM
    tk = tq
    nq = S // tq
    scale = d ** -0.5

    ts2 = tq // 2
    last = 2 * nq - 1

    def kern(q_ref, k_ref, kT_ref, v_ref, do_ref, lse_ref, dl_ref,
             dqT_ref, dk_ref, dv_ref, dfk_ref, dfq_ref,
             kq_buf, dp_buf, pb_buf, ds_buf, dk_scr, dv_scr, dfk_scr):
        j = pl.program_id(1)

        @pl.when(j == 0)
        def _():
            dqT_ref[...] = jnp.zeros_like(dqT_ref)
            dfq_ref[...] = jnp.zeros_like(dfq_ref)

        kb = k_ref[...]
        kTb = kT_ref[...]
        vb = v_ref[:, :d]
        dk_scr[...] = jnp.zeros_like(dk_scr)
        dv_scr[...] = jnp.zeros_like(dv_scr)
        dfk_scr[...] = jnp.zeros_like(dfk_scr)

        def queries(t):
            return pl.ds(pl.multiple_of(t * ts2, ts2), ts2)

        def products(t, slot):
            rows = queries(t)
            kq_buf[slot] = _dot(kb, q_ref[rows, :], NT)
            dp_buf[slot] = _dot(vb, do_ref[rows, :], NT)

        def pointwise(t, slot, mask_off):
            rows = queries(t)
            sT = kq_buf[slot]
            if mask_off is not None:
                r = lax.broadcasted_iota(jnp.int32, (tk, ts2), 0)
                c = lax.broadcasted_iota(jnp.int32, (tk, ts2), 1)
                sT = jnp.where(r <= c + mask_off, sT, NEG)
            pT = jnp.exp2(sT - lse_ref[:, rows])
            dsT = pT * (dp_buf[slot] - dl_ref[:, rows])
            pb_buf[slot] = _b16(pT)
            ds_buf[slot] = _b16(dsT)
            dfk_scr[...] -= jnp.sum(dsT, axis=1, keepdims=True)
            dfq_ref[:, rows] += jnp.sum(dsT, axis=0, keepdims=True)

        def accumulate(t, slot):
            rows = queries(t)
            dsb = ds_buf[slot]
            dv_scr[...] += _dot(pb_buf[slot], do_ref[rows, :])
            dk_scr[...] += _dot(dsb, q_ref[rows, :])
            dqT_ref[:, rows] += _dot(kTb, dsb) * scale

        def stage(t, slot, mask_off, has_prev):
            products(jnp.minimum(t + 1, last), 1 - slot)
            if has_prev:
                accumulate(t - 1, 1 - slot)
            pointwise(t, slot, mask_off)

        products(2 * j, 0)
        stage(2 * j, 0, 0, False)
        stage(2 * j + 1, 1, ts2, True)

        def pair(n):
            stage(2 * n, 0, None, True)
            stage(2 * n + 1, 1, None, True)

        def quad(m, _):
            pair(j + 1 + 2 * m)
            pair(j + 2 + 2 * m)
            return 0

        n_rest = nq - 1 - j
        lax.fori_loop(0, n_rest // 2, quad, 0)

        @pl.when(n_rest % 2 == 1)
        def _():
            pair(nq - 1)

        accumulate(last, 1)
        dk_ref[...] = dk_scr[:, :d] * LN2
        dv_ref[...] = dv_scr[...]
        dfk_ref[...] = _col_to_row(dfk_scr[...])

    tile = lambda h, j: (h, j, 0)
    whole = lambda h, j: (h, 0, 0)
    rowtile = lambda h, j: (h, 0, j)
    return pl.pallas_call(
        kern, name="fox_bwd", grid=(H, S // tk),
        in_specs=[pl.BlockSpec((None, S, da), whole),
                  pl.BlockSpec((None, tk, da), tile),
                  pl.BlockSpec((None, d, tk), lambda h, j: (h, 0, j)),
                  pl.BlockSpec((None, tk, 128), tile),
                  pl.BlockSpec((None, S, d), whole),
                  pl.BlockSpec((None, 1, S), whole),
                  pl.BlockSpec((None, 1, S), whole)],
        out_specs=[pl.BlockSpec((None, d, S), whole),
                   pl.BlockSpec((None, tk, d), tile),
                   pl.BlockSpec((None, tk, d), tile),
                   pl.BlockSpec((None, 1, tk), rowtile),
                   pl.BlockSpec((None, 1, S), whole)],
        out_shape=[jax.ShapeDtypeStruct((H, d, S), F32), jax.ShapeDtypeStruct((H, S, d), F32),
                   jax.ShapeDtypeStruct((H, S, d), F32), jax.ShapeDtypeStruct((H, 1, S), F32),
                   jax.ShapeDtypeStruct((H, 1, S), F32)],
        scratch_shapes=[pltpu.VMEM((2, tk, ts2), F32), pltpu.VMEM((2, tk, ts2), F32),
                        pltpu.VMEM((2, tk, ts2), BF16), pltpu.VMEM((2, tk, ts2), BF16),
                        pltpu.VMEM((tk, da), F32), pltpu.VMEM((tk, d), F32), pltpu.VMEM((tk, 1), F32)],
        compiler_params=_params(("parallel", "arbitrary")),
    )(qs, k_aug, kT, v, do, lse_row, delta_row)


def _heads_major(a, H, d):
    S = a.shape[0]
    return a.reshape(S, H, d).transpose(1, 0, 2)


def _heads_minor(a):
    H, S, d = a.shape
    return a.transpose(1, 0, 2).reshape(S, H * d)


def _lane_pick(x128, lane):
    return x128[:, lane:lane + 1]


def _l2_fwd(y):
    return lax.rsqrt(jnp.sum(y * y, axis=1, keepdims=True) + EPS)


def _gdn_prep(z, conv_w, a128, dt128, ts):
    S = z.shape[0]
    C3 = 3 * WIDTH
    hb = ts // 8

    def body(t, first, ins, outs, scratch):
        x_ref, halo_ref, zs_ref, w_ref, a_ref, dt_ref = ins
        qkv_ref, c_ref, gb_ref, gbT_ref = outs
        halo = jnp.where(t > 0, halo_ref[...], 0.0)
        xe = jnp.concatenate([halo, x_ref[...]], axis=0)
        w = w_ref[...]
        c = w[3:4, :] * xe[8:, :]
        for back in (1, 2, 3):
            c = c + w[3 - back:4 - back, :] * pltpu.roll(xe, back, 0)[8:, :]
        c_ref[...] = c
        y = c * _sigmoid(c)
        for h in range(GDN_HEADS):
            lo = h * GDN_DIM
            yq = y[:, lo:lo + GDN_DIM]
            qkv_ref[:, lo:lo + GDN_DIM] = yq * (_l2_fwd(yq) * (GDN_DIM ** -0.5))
            yk = y[:, WIDTH + lo:WIDTH + lo + GDN_DIM]
            qkv_ref[:, WIDTH + lo:WIDTH + lo + GDN_DIM] = yk * _l2_fwd(yk)
        qkv_ref[:, 2 * WIDTH:] = y[:, 2 * WIDTH:]
        zs = zs_ref[...]
        lane = lax.broadcasted_iota(jnp.int32, zs.shape, 1)
        g = -jnp.exp(a_ref[...]) * _softplus(zs + dt_ref[...])
        G = _scan_rows(g, ts, CHUNK)
        beta = _sigmoid(zs)
        out = jnp.where(lane < 8, pltpu.roll(g, 128 - LANE_BA, 1), jnp.where(lane < LANE_BB, G, beta))
        gb_ref[...] = out
        gbT_ref[...] = _transpose_exact(out)

    x_in = (z, (ts, C3), lambda i: (i, CB_BQKV))
    halo_in = (z, (8, C3), lambda i: (jnp.maximum(i * hb - 1, 0), CB_BQKV))
    return _tiled("gdn_prep", body, S // ts,
                  [x_in, halo_in, _cols(z, ts, 128, CB_SMALL), _full(conv_w), _full(a128), _full(dt128)],
                  [_orow(S, (C3,), F32, ts), _orow(S, (C3,), F32, ts), _orow(S, (128,), F32, ts),
                   ((128, S), F32, (128, ts), lambda i: (0, i))])


def _chunk_masks(nc):
    r = lax.broadcasted_iota(jnp.int32, (nc, CHUNK, CHUNK), 1)
    c = lax.broadcasted_iota(jnp.int32, (nc, CHUNK, CHUNK), 2)
    return c <= r, c < r, c == r


def _chunk_local(qh, kh, vh, Gc, Gr, beta):
    nc = qh.shape[0]
    incl, strict, _ = _chunk_masks(nc)
    gamma = jnp.exp(jnp.where(incl, Gc - Gr, NEG))
    kb = kh * beta
    P = _bdot(_b16(kb), _b16(kh), 2, 2)
    Qk = _bdot(_b16(qh), _b16(kh), 2, 2)
    eG = jnp.exp(Gc)
    Gl = Gc[:, CHUNK - 1:CHUNK, :]
    edec = jnp.exp(Gl - Gc)
    return incl, strict, gamma, kb, P, Qk, eG, edec


def _gdn_local_fwd(qkv, gb, grow, ts):
    S = qkv.shape[0]
    nc = ts // CHUNK

    def body(t, first, ins, outs, scratch):
        q_ref, k_ref, v_ref, gb_ref, gr_ref = ins
        u_ref, w_ref, qd_ref, kd_ref, aqk_ref, T_ref = outs
        gbv = gb_ref[...]
        heads = range(GDN_HEADS)
        _, _, eye = _chunk_masks(nc)
        T, X, rhs_u, rhs_w = [], [], [], []
        for h in heads:
            lo = h * GDN_DIM
            qh = q_ref[:, lo:lo + GDN_DIM].reshape(nc, CHUNK, GDN_DIM)
            kh = k_ref[:, lo:lo + GDN_DIM].reshape(nc, CHUNK, GDN_DIM)
            vh = v_ref[:, lo:lo + GDN_DIM].reshape(nc, CHUNK, GDN_DIM)
            Gc = _lane_pick(gbv, LANE_BA + h).reshape(nc, CHUNK, 1)
            beta = _lane_pick(gbv, LANE_BB + h).reshape(nc, CHUNK, 1)
            Gr = gr_ref[h].reshape(nc, 1, CHUNK)
            incl, strict, gamma, kb, P, Qk, eG, edec = _chunk_local(qh, kh, vh, Gc, Gr, beta)
            A = jnp.where(strict, P * gamma, 0.0)
            T.append(jnp.where(eye, 1.0, 0.0) - A)
            X.append(A)
            rhs_u.append(vh * beta)
            rhs_w.append(kb * eG)
            qd_ref[:, lo:lo + GDN_DIM] = _b16(qh * eG).reshape(ts, GDN_DIM)
            kd_ref[:, lo:lo + GDN_DIM] = _b16(kh * edec).reshape(ts, GDN_DIM)
            aqk_ref[h] = _b16(jnp.where(incl, Qk * gamma, 0.0)).reshape(ts, CHUNK)
        for _ in range(5):
            X = [_bdot(X[h], X[h], 2, 1, PREC_UT) for h in heads]
            T = [T[h] + _bdot(T[h], X[h], 2, 1, PREC_UT) for h in heads]
        u = [_bdot(T[h], rhs_u[h], 2, 1, PREC_UT) for h in heads]
        w = [_bdot(T[h], rhs_w[h], 2, 1, PREC_UT) for h in heads]
        for h in heads:
            lo = h * GDN_DIM
            u_ref[:, lo:lo + GDN_DIM] = u[h].reshape(ts, GDN_DIM)
            w_ref[:, lo:lo + GDN_DIM] = _b16(w[h]).reshape(ts, GDN_DIM)
            T_ref[h] = T[h].reshape(ts, CHUNK)

    wide = lambda dt: _orow(S, (WIDTH,), dt, ts)
    perhead = lambda dt: ((GDN_HEADS, S, CHUNK), dt, (GDN_HEADS, ts, CHUNK), lambda i: (0, i, 0))
    return _tiled("gdn_local_fwd", body, S // ts,
                  [_cols(qkv, ts, WIDTH, 0), _cols(qkv, ts, WIDTH, 1), _cols(qkv, ts, WIDTH, 2),
                   _rows(gb, ts), (grow, (GDN_HEADS, nc, CHUNK), lambda i: (0, i, 0))],
                  [wide(F32), wide(BF16), wide(BF16), wide(BF16), perhead(BF16), perhead(F32)])


def _gdn_scan_fwd(u, w, qd, kd, aqk, gb, ts):
    S = u.shape[0]
    nc = ts // CHUNK
    N = S // CHUNK

    def body(t, first, ins, outs, scratch):
        u_ref, w_ref, qd_ref, kd_ref, aqk_ref, gb_ref = ins
        o_ref, vn_ref, st_ref = outs
        (state,) = scratch

        @pl.when(first)
        def _():
            state[...] = jnp.zeros_like(state)

        def chunk(c, _):
            r0 = pl.multiple_of(c * CHUNK, CHUNK)
            rows = pl.ds(r0, CHUNK)
            glast = gb_ref[pl.ds(r0 + CHUNK - 1, 1), :]
            heads = range(GDN_HEADS)
            cols = [slice(h * GDN_DIM, (h + 1) * GDN_DIM) for h in heads]
            S_old = [state[h] for h in heads]
            u_h = [u_ref[rows, cols[h]] for h in heads]
            w_h = [_b16(w_ref[rows, cols[h]]) for h in heads]
            qd_h = [_b16(qd_ref[rows, cols[h]]) for h in heads]
            kd_h = [_b16(kd_ref[rows, cols[h]]) for h in heads]
            aqk_h = [_b16(aqk_ref[h, rows, :]) for h in heads]
            both = [_dot(jnp.concatenate([w_h[h], qd_h[h]], axis=0), _b16(S_old[h]))
                    for h in heads]
            vn_h = [u_h[h] - both[h][:CHUNK] for h in heads]
            vnb = [_b16(vn_h[h]) for h in heads]
            intra = [_dot(aqk_h[h], vnb[h]) for h in heads]
            outer = [_dot(kd_h[h], vnb[h], TN) for h in heads]
            o_h = [both[h][CHUNK:] + intra[h] for h in heads]
            S_new = [S_old[h] * jnp.exp(glast[:, LANE_BA + h:LANE_BA + h + 1]) + outer[h] for h in heads]
            for h in heads:
                st_ref[c, h] = S_old[h]
                state[h] = S_new[h]
                o_ref[rows, cols[h]] = o_h[h]
                vn_ref[rows, cols[h]] = vnb[h]
            return 0

        lax.fori_loop(0, nc, chunk, 0)

    wide_in = lambda a: _rows(a, ts)
    wide = lambda dt: _orow(S, (WIDTH,), dt, ts)
    states = ((N, GDN_HEADS, GDN_DIM, GDN_DIM), F32, (nc, GDN_HEADS, GDN_DIM, GDN_DIM),
              lambda i: (i, 0, 0, 0))
    return _tiled("gdn_scan_fwd", body, S // ts,
                  [wide_in(u), wide_in(w), wide_in(qd), wide_in(kd),
                   (aqk, (GDN_HEADS, ts, CHUNK), lambda i: (0, i, 0)), _rows(gb, ts)],
                  [wide(F32), wide(BF16), states],
                  scratch=[pltpu.VMEM((GDN_HEADS, GDN_DIM, GDN_DIM), F32)])


def _gdn_scan_bwd(do, w, qd, kd, aqk, vn, states, gb, ts):
    S = do.shape[0]
    nc = ts // CHUNK
    N = S // CHUNK

    def body(t, first, ins, outs, scratch):
        do_ref, w_ref, qd_ref, kd_ref, aqk_ref, vn_ref, st_ref, gb_ref = ins
        du_ref, dw_ref, dqd_ref, dkd_ref, daqk_ref, dgl_ref = outs
        (dstate,) = scratch

        @pl.when(first)
        def _():
            dstate[...] = jnp.zeros_like(dstate)

        r = lax.broadcasted_iota(jnp.int32, (CHUNK, CHUNK), 0)
        cc = lax.broadcasted_iota(jnp.int32, (CHUNK, CHUNK), 1)
        incl = cc <= r
        lane = lax.broadcasted_iota(jnp.int32, (1, 128), 1)

        def chunk(k, _):
            c = nc - 1 - k
            r0 = pl.multiple_of(c * CHUNK, CHUNK)
            rows = pl.ds(r0, CHUNK)
            glast = gb_ref[pl.ds(r0 + CHUNK - 1, 1), :]
            dgl_row = jnp.zeros((1, 128), F32)
            heads = range(GDN_HEADS)
            cols = [slice(h * GDN_DIM, (h + 1) * GDN_DIM) for h in heads]
            S_h = [st_ref[c, h] for h in heads]
            dS_h = [dstate[h] for h in heads]
            do_h = [_b16(do_ref[rows, cols[h]]) for h in heads]
            aqk_h = [_b16(aqk_ref[h, rows, :]) for h in heads]
            vn_h = [_b16(vn_ref[rows, cols[h]]) for h in heads]
            kd_h = [_b16(kd_ref[rows, cols[h]]) for h in heads]
            qd_h = [_b16(qd_ref[rows, cols[h]]) for h in heads]
            w_h = [_b16(w_ref[rows, cols[h]]) for h in heads]
            Sb = [_b16(S_h[h]) for h in heads]
            dSb = [_b16(dS_h[h]) for h in heads]
            dvn_a = [_dot(aqk_h[h], do_h[h], TN) for h in heads]
            dvn_b = [_dot(kd_h[h], dSb[h]) for h in heads]
            daqk_h = [jnp.where(incl, _dot(do_h[h], vn_h[h], NT), 0.0) for h in heads]
            dkd_h = [_dot(vn_h[h], dSb[h], NT) for h in heads]
            dvn_h = [dvn_a[h] + dvn_b[h] for h in heads]
            both = [jnp.concatenate([do_h[h], _b16(dvn_h[h])], axis=0) for h in heads]
            by_state = [_dot(both[h], Sb[h], NT) for h in heads]
            dS_dot = [_dot(jnp.concatenate([qd_h[h], -w_h[h]], axis=0), both[h], TN) for h in heads]
            res = []
            for h in heads:
                egl = jnp.exp(glast[:, LANE_BA + h:LANE_BA + h + 1])
                dgl = egl * jnp.sum(jnp.sum(dS_h[h] * S_h[h], axis=1, keepdims=True), axis=0,
                                    keepdims=True)
                dgl_row = jnp.where(lane == h, dgl, dgl_row)
                res.append((daqk_h[h], by_state[h][:CHUNK], dkd_h[h], -by_state[h][CHUNK:], dvn_h[h],
                            dS_dot[h] + egl * dS_h[h]))
            for h in heads:
                daqk, dqd, dkd, dw, dvn, dS_new = res[h]
                daqk_ref[h, rows, :] = daqk
                dqd_ref[rows, cols[h]] = dqd
                dkd_ref[rows, cols[h]] = dkd
                dw_ref[rows, cols[h]] = dw
                du_ref[rows, cols[h]] = dvn
                dstate[h] = dS_new
            dgl_ref[pl.ds(c, 1), :] = dgl_row
            return 0

        lax.fori_loop(0, nc, chunk, 0)

    wide_in = lambda a: _rows(a, ts)
    wide = _orow(S, (WIDTH,), F32, ts)
    perhead_in = lambda a: (a, (GDN_HEADS, ts, CHUNK), lambda i: (0, i, 0))
    perhead = ((GDN_HEADS, S, CHUNK), F32, (GDN_HEADS, ts, CHUNK), lambda i: (0, i, 0))
    return _tiled("gdn_scan_bwd", body, S // ts,
                  [wide_in(do), wide_in(w), wide_in(qd), wide_in(kd), perhead_in(aqk), wide_in(vn),
                   (states, (nc, GDN_HEADS, GDN_DIM, GDN_DIM), lambda i: (i, 0, 0, 0)), _rows(gb, ts)],
                  [wide, wide, wide, wide, perhead, ((N, 128), F32, (nc, 128), lambda i: (i, 0))],
                  scratch=[pltpu.VMEM((GDN_HEADS, GDN_DIM, GDN_DIM), F32)], reverse=True)


def _gdn_local_bwd(qkv, gb, grow, T, du, dw, dqd, dkd, daqk, dgl, ts):
    S = qkv.shape[0]
    nc = ts // CHUNK

    def body(t, first, ins, outs, scratch):
        (q_ref, k_ref, v_ref, gb_ref, gr_ref, T_ref, du_ref, dw_ref, dqd_ref, dkd_ref,
         daqk_ref, dgl_ref) = ins
        dqkv_ref, dgb_ref = outs
        gbv = gb_ref[...]
        dglv = dgl_ref[...]
        lane = lax.broadcasted_iota(jnp.int32, (ts, 128), 1)
        dG_all = jnp.zeros((ts, 128), F32)
        dbeta_all = jnp.zeros((ts, 128), F32)
        heads = range(GDN_HEADS)
        _, _, eye = _chunk_masks(nc)
        pre = []
        for h in heads:
            lo = h * GDN_DIM
            cols = slice(lo, lo + GDN_DIM)
            r3 = lambda ref: ref[:, cols].reshape(nc, CHUNK, GDN_DIM)
            qh, kh, vh = r3(q_ref), r3(k_ref), r3(v_ref)
            duh, dwh, dqdh, dkdh = r3(du_ref), r3(dw_ref), r3(dqd_ref), r3(dkd_ref)
            Gc = _lane_pick(gbv, LANE_BA + h).reshape(nc, CHUNK, 1)
            beta = _lane_pick(gbv, LANE_BB + h).reshape(nc, CHUNK, 1)
            Gr = gr_ref[h].reshape(nc, 1, CHUNK)
            Th = T_ref[h].reshape(nc, CHUNK, CHUNK)
            daq = daqk_ref[h].reshape(nc, CHUNK, CHUNK)
            local = _chunk_local(qh, kh, vh, Gc, Gr, beta)
            kb, eG = local[3], local[6]
            vb = vh * beta
            kbg = kb * eG
            dvb = _bdot(Th, duh, 1, 1, PREC_UT)
            dkbg = _bdot(Th, dwh, 1, 1, PREC_UT)
            dT = _bdot(duh, vb, 2, 2, PREC_UT) + _bdot(dwh, kbg, 2, 2, PREC_UT)
            pre.append((qh, kh, vh, dqdh, dkdh, beta, Th, daq, local, kbg, dvb, dkbg, dT))
        M1s = [_bdot(pre[h][6], pre[h][12], 1, 1, PREC_UT) for h in heads]
        dAs = [_bdot(M1s[h], pre[h][6], 2, 2, PREC_UT) for h in heads]
        for h in heads:
            lo = h * GDN_DIM
            cols = slice(lo, lo + GDN_DIM)
            qh, kh, vh, dqdh, dkdh, beta, Th, daq, local, kbg, dvb, dkbg, dT = pre[h]
            incl, strict, gamma, kb, P, Qk, eG, edec = local
            dA = jnp.where(strict, -dAs[h], 0.0)
            dP = dA * gamma
            dQ = daq * gamma
            dgam = (dA * P + daq * Qk) * gamma
            dPb, dQb = _b16(dP), _b16(dQ)
            khb, qhb, kbb = _b16(kh), _b16(qh), _b16(kb)
            dq = _bdot(dQb, khb, 2, 1) + dqdh * eG
            dkb = _bdot(dPb, khb, 2, 1) + dkbg * eG
            dk = (_bdot(dQb, qhb, 1, 1) + _bdot(dPb, kbb, 1, 1) + dkdh * edec + dkb * beta)
            dbeta = (jnp.sum(dkb * kh, axis=2, keepdims=True) + jnp.sum(dvb * vh, axis=2, keepdims=True))
            dv = dvb * beta
            col_as_col = jnp.sum(jnp.where(eye, jnp.sum(dgam, axis=1, keepdims=True), 0.0),
                                 axis=2, keepdims=True)
            kd_term = jnp.sum(dkdh * kh * edec, axis=2, keepdims=True)
            dG = (jnp.sum(dgam, axis=2, keepdims=True) - col_as_col
                  + jnp.sum(dqdh * qh * eG, axis=2, keepdims=True)
                  + jnp.sum(dkbg * kbg, axis=2, keepdims=True) - kd_term)
            dgl_h = dglv[:, h:h + 1].reshape(nc, 1, 1) + jnp.sum(kd_term, axis=1, keepdims=True)
            last = lax.broadcasted_iota(jnp.int32, (nc, CHUNK, 1), 1) == CHUNK - 1
            dG = dG + jnp.where(last, dgl_h, 0.0)
            dqkv_ref[:, cols] = dq.reshape(ts, GDN_DIM)
            dqkv_ref[:, WIDTH + lo:WIDTH + lo + GDN_DIM] = dk.reshape(ts, GDN_DIM)
            dqkv_ref[:, 2 * WIDTH + lo:2 * WIDTH + lo + GDN_DIM] = dv.reshape(ts, GDN_DIM)
            dG_all = jnp.where(lane == LANE_BA + h, dG.reshape(ts, 1), dG_all)
            dbeta_all = jnp.where(lane == LANE_BB + h, dbeta.reshape(ts, 1), dbeta_all)
        dg_all = _scan_rows(dG_all, ts, CHUNK, reverse=True)
        dgb_ref[...] = jnp.where(lane < LANE_BB, dg_all, dbeta_all)

    wide_in = lambda a: _rows(a, ts)
    perhead_in = lambda a: (a, (GDN_HEADS, ts, CHUNK), lambda i: (0, i, 0))
    return _tiled("gdn_local_bwd", body, S // ts,
                  [_cols(qkv, ts, WIDTH, 0), _cols(qkv, ts, WIDTH, 1), _cols(qkv, ts, WIDTH, 2),
                   _rows(gb, ts), (grow, (GDN_HEADS, nc, CHUNK), lambda i: (0, i, 0)), perhead_in(T),
                   wide_in(du), wide_in(dw), wide_in(dqd), wide_in(dkd), perhead_in(daqk),
                   (dgl, (nc, 128), lambda i: (i, 0))],
                  [_orow(S, (3 * WIDTH,), F32, ts), _orow(S, (128,), F32, ts)])


def _gdn_prep_bwd(dqkv, dgb, cpre, z, conv_w, a128, dt128, dz, ts):
    S = z.shape[0]
    C3 = 3 * WIDTH
    hb = ts // 8
    n_tiles = S // ts

    def dpre(dq, c):
        y, dsil = _silu_and_grad(c)
        parts = []
        for h in range(GDN_HEADS):
            lo = h * GDN_DIM
            yq = y[:, lo:lo + GDN_DIM]
            rq = _l2_fwd(yq)
            nq = yq * rq
            dn = dq[:, lo:lo + GDN_DIM] * (GDN_DIM ** -0.5)
            parts.append(rq * (dn - nq * jnp.sum(dn * nq, axis=1, keepdims=True)))
        for h in range(GDN_HEADS):
            lo = WIDTH + h * GDN_DIM
            yk = y[:, lo:lo + GDN_DIM]
            rk = _l2_fwd(yk)
            nk = yk * rk
            dn = dq[:, lo:lo + GDN_DIM]
            parts.append(rk * (dn - nk * jnp.sum(dn * nk, axis=1, keepdims=True)))
        parts.append(dq[:, 2 * WIDTH:])
        return jnp.concatenate(parts, axis=1) * dsil

    def body(t, first, ins, outs, scratch):
        (dq_ref, dqn_ref, c_ref, cn_ref, x_ref, xp_ref, zs_ref, dgb_ref, w_ref, a_ref, dt_ref) = ins
        dx_ref, dzs_ref, dw_ref, dad_ref = outs

        @pl.when(first)
        def _():
            dw_ref[...] = jnp.zeros_like(dw_ref)
            dad_ref[...] = jnp.zeros_like(dad_ref)

        dc = dpre(dq_ref[...], c_ref[...])
        dcn = jnp.where(t < n_tiles - 1, dpre(dqn_ref[...], cn_ref[...]), 0.0)
        dce = jnp.concatenate([dc, dcn], axis=0)
        w = w_ref[...]
        dx = w[3:4, :] * dc
        for back in (1, 2, 3):
            dx = dx + w[3 - back:4 - back, :] * pltpu.roll(dce, ts + 8 - back, 0)[:ts, :]
        dx_ref[...] = _b16(dx)
        halo = jnp.where(t > 0, xp_ref[...], 0.0)
        xe = jnp.concatenate([halo, x_ref[...]], axis=0)
        dw_ref[3:4, :] += jnp.sum(dc * xe[8:, :], axis=0, keepdims=True)
        for back in (1, 2, 3):
            dw_ref[3 - back:4 - back, :] += jnp.sum(dc * pltpu.roll(xe, back, 0)[8:, :], axis=0,
                                                     keepdims=True)
        zs = zs_ref[...]
        dgb = dgb_ref[...]
        lane = lax.broadcasted_iota(jnp.int32, zs.shape, 1)
        arg = zs + dt_ref[...]
        nega = -jnp.exp(a_ref[...])
        dba = dgb * nega * _sigmoid(arg)
        beta = _sigmoid(zs)
        dbb = dgb * beta * (1.0 - beta)
        dzs_ref[...] = jnp.where((lane >= LANE_BA) & (lane < LANE_BB), dba,
                                 jnp.where((lane >= LANE_BB) & (lane < LANE_BB + 4), dbb, 0.0))
        dad_ref[0:1, :] += jnp.sum(dgb * nega * _softplus(arg), axis=0, keepdims=True)
        dad_ref[1:2, :] += jnp.sum(dba, axis=0, keepdims=True)

    nxt = lambda i: (jnp.minimum((i + 1) * hb, S // 8 - 1), 0)
    prv = lambda i: (jnp.maximum(i * hb - 1, 0), CB_BQKV)
    return _tiled("gdn_prep_bwd", body, n_tiles,
                  [_rows(dqkv, ts), (dqkv, (8, C3), nxt), _rows(cpre, ts), (cpre, (8, C3), nxt),
                   (z, (ts, C3), lambda i: (i, CB_BQKV)), (z, (8, C3), prv),
                   _cols(z, ts, 128, CB_SMALL), _rows(dgb, ts), _full(conv_w), _full(a128), _full(dt128)],
                  [((S, N_AL), BF16, (ts, C3), lambda i: (i, CB_BQKV)), _orow(S, (128,), F32, ts),
                   _oacc((8, C3), F32), _oacc((8, 128), F32)],
                  fill=(dz, 0))


def _mem_attn_fwd(z, mk, mv, ts):
    S = z.shape[0]

    def body(t, first, ins, outs, scratch):
        q_ref, mk_ref, mv_ref = ins
        (o_ref,) = outs
        heads = range(MEM_HEADS)
        cols = [slice(h * MEM_DIM, (h + 1) * MEM_DIM) for h in heads]
        s = [_dot(_b16(q_ref[:, cols[h]]), _b16(mk_ref[:, cols[h]]), NT) * (MEM_DIM ** -0.5)
             for h in heads]
        p = []
        for h in heads:
            e = jnp.exp(s[h] - jnp.max(s[h], axis=1, keepdims=True))
            p.append(_b16(e / jnp.sum(e, axis=1, keepdims=True)))
        o = [_dot(p[h], _b16(mv_ref[:, cols[h]])) for h in heads]
        for h in heads:
            o_ref[:, cols[h]] = o[h]

    (o,) = _tiled("mem_attn_fwd", body, S // ts, [_cols(z, ts, WIDTH, CB_MQ), _full(mk), _full(mv)],
                  [_orow(S, (WIDTH,), F32, ts)])
    return o


def _mem_attn_bwd(do, z, mk, mv, dz, ts):
    S = z.shape[0]
    M = mk.shape[0]

    def body(t, first, ins, outs, scratch):
        do_ref, q_ref, mk_ref, mv_ref = ins
        dq_ref, dmk_ref, dmv_ref = outs

        @pl.when(first)
        def _():
            dmk_ref[...] = jnp.zeros_like(dmk_ref)
            dmv_ref[...] = jnp.zeros_like(dmv_ref)

        scale = MEM_DIM ** -0.5
        heads = range(MEM_HEADS)
        cols = [slice(h * MEM_DIM, (h + 1) * MEM_DIM) for h in heads]
        qb = [_b16(q_ref[:, cols[h]]) for h in heads]
        kb = [_b16(mk_ref[:, cols[h]]) for h in heads]
        dob = [_b16(do_ref[:, cols[h]]) for h in heads]
        s = [_dot(qb[h], kb[h], NT) * scale for h in heads]
        dp = [_dot(dob[h], _b16(mv_ref[:, cols[h]]), NT) for h in heads]
        p = []
        for h in heads:
            e = jnp.exp(s[h] - jnp.max(s[h], axis=1, keepdims=True))
            p.append(e / jnp.sum(e, axis=1, keepdims=True))
        dsb = [_b16(p[h] * (dp[h] - jnp.sum(dp[h] * p[h], axis=1, keepdims=True)) * scale) for h in heads]
        dmv = [_dot(_b16(p[h]), dob[h], TN) for h in heads]
        dq = [_dot(dsb[h], kb[h]) for h in heads]
        dmk = [_dot(dsb[h], qb[h], TN) for h in heads]
        for h in heads:
            dmv_ref[:, cols[h]] += dmv[h]
            dq_ref[:, cols[h]] = _b16(dq[h])
            dmk_ref[:, cols[h]] += dmk[h]

    return _tiled("mem_attn_bwd", body, S // ts,
                  [_rows(do, ts), _cols(z, ts, WIDTH, CB_MQ), _full(mk), _full(mv)],
                  [((S, N_AL), BF16, (ts, WIDTH), lambda i: (i, CB_MQ)), _oacc((M, WIDTH), F32),
                   _oacc((M, WIDTH), F32)],
                  fill=(dz, 0))


def _head_norm(ob, g):
    xs, rs = [], []
    for h in range(GDN_HEADS):
        o = ob[:, h * GDN_DIM:(h + 1) * GDN_DIM]
        r = lax.rsqrt(jnp.mean(o * o, axis=1, keepdims=True) + EPS)
        xs.append(o * r)
        rs.append(r)
    return xs, rs


def _merge_fwd(x, z, o_a, o_b, o_m, gdn_g, b_merge, wb, wout, ts):
    S, D = x.shape

    def body(t, first, ins, outs, scratch):
        (x_ref, g_ref, oa_ref, az_ref, ob_ref, bz_ref, om_ref, mz_ref, gg_ref, bm_ref, wb_ref,
         wo_ref) = ins
        xo_ref, ya_ref, yb_ref, ym_ref, mg_ref = outs
        ya = oa_ref[...] * _silu_and_grad(az_ref[...])[0]
        xs, _ = _head_norm(ob_ref[...], None)
        nb = jnp.concatenate([xh * gg_ref[...] for xh in xs], axis=1)
        yb = nb * _silu_and_grad(bz_ref[...])[0]
        ym = om_ref[...] * _silu_and_grad(mz_ref[...])[0]
        merged = jnp.zeros((ts, D), F32)
        for n, (y, y_ref) in enumerate(((ya, ya_ref), (yb, yb_ref), (ym, ym_ref))):
            yb16 = _b16(y)
            y_ref[...] = yb16
            gate = _sigmoid(g_ref[:, n * D:(n + 1) * D] + bm_ref[:, n * D:(n + 1) * D])
            merged = merged + gate * _dot(yb16, wb_ref[n])
        mb = _b16(merged)
        mg_ref[...] = mb
        xo_ref[...] = x_ref[...] + _dot(mb, wo_ref[...])

    half = lambda a: _rows(a, ts)
    return _tiled("merge_fwd", body, S // ts,
                  [_rows(x, ts), _cols(z, ts, 3 * D, CB_GATES), half(o_a), _cols(z, ts, WIDTH, CB_AZ),
                   half(o_b), _cols(z, ts, WIDTH, CB_BZ), half(o_m), _cols(z, ts, WIDTH, CB_MZ),
                   _full(gdn_g.reshape(1, GDN_DIM)), _full(b_merge.reshape(1, 3 * D)), _full(wb), _full(wout)],
                  [_orow(S, (D,), F32, ts), _orow(S, (WIDTH,), BF16, ts), _orow(S, (WIDTH,), BF16, ts),
                   _orow(S, (WIDTH,), BF16, ts), _orow(S, (D,), BF16, ts)])


def _merge_bwd(dout, z, o_a, o_b, o_m, ya, yb, ym, gdn_g, b_merge, wb, wout, hsum, ts):
    S, D = dout.shape

    def body(t, first, ins, outs, scratch):
        (do_ref, g_ref, oa_ref, az_ref, ob_ref, bz_ref, om_ref, mz_ref, ya_ref, yb_ref, ym_ref,
         gg_ref, bm_ref, wb_ref, wo_ref, hs_ref) = ins
        (dg_ref, dpa_ref, dpb_ref, dpm_ref, doa_ref, dob_ref, dom_ref, dl_ref, dbm_ref, dgg_ref) = outs
        G3 = 3 * D

        @pl.when(first)
        def _():
            dbm_ref[...] = jnp.zeros_like(dbm_ref)
            dgg_ref[...] = jnp.zeros_like(dgg_ref)

        dmerged = _dot(_b16(do_ref[...]), wo_ref[...], NT)
        dys = []
        for n, (y_ref, dp_ref) in enumerate(((ya_ref, dpa_ref), (yb_ref, dpb_ref), (ym_ref, dpm_ref))):
            sl = slice(n * D, (n + 1) * D)
            gate = _sigmoid(g_ref[:, sl] + bm_ref[:, sl])
            proj = _dot(y_ref[...], wb_ref[n])
            dproj = _b16(gate * dmerged)
            dp_ref[...] = dproj
            dgp = dmerged * proj * gate * (1.0 - gate)
            dg_ref[:, sl] = dgp.astype(dg_ref.dtype)
            dbm_ref[0:1, sl] += jnp.sum(dgp, axis=0, keepdims=True)
            dys.append(_dot(dproj, wb_ref[n], NT))
        dya, dyb, dym = dys
        sa, dsa = _silu_and_grad(az_ref[...])
        oa = oa_ref[...]
        doa = dya * sa
        doa_ref[...] = _b16(doa)
        dg_ref[:, G3:G3 + WIDTH] = _b16(dya * oa * dsa)
        dl_ref[...] = _dot(hs_ref[...], doa * oa, NT, HIGHEST)
        sm, dsm = _silu_and_grad(mz_ref[...])
        dom_ref[...] = _b16(dym * sm)
        dg_ref[:, G3 + 2 * WIDTH:G3 + 3 * WIDTH] = _b16(dym * om_ref[...] * dsm)
        sb, dsb = _silu_and_grad(bz_ref[...])
        xs, rs = _head_norm(ob_ref[...], None)
        gg = gg_ref[...]
        dgg = jnp.zeros((1, GDN_DIM), F32)
        for h in range(GDN_HEADS):
            cols = slice(h * GDN_DIM, (h + 1) * GDN_DIM)
            dn = dyb[:, cols] * sb[:, cols]
            dg_ref[:, G3 + WIDTH + h * GDN_DIM:G3 + WIDTH + (h + 1) * GDN_DIM] = _b16(
                dyb[:, cols] * (xs[h] * gg) * dsb[:, cols])
            dgg = dgg + jnp.sum(dn * xs[h], axis=0, keepdims=True)
            dxh = dn * gg
            dob_ref[:, cols] = _b16(rs[h] * (dxh - xs[h] * jnp.mean(dxh * xs[h], axis=1, keepdims=True)))
        dgg_ref[0:1, :] += dgg

    half = lambda a: _rows(a, ts)
    w512 = lambda dt: _orow(S, (WIDTH,), dt, ts)
    return _tiled("merge_bwd", body, S // ts,
                  [_rows(dout, ts), _cols(z, ts, 3 * D, CB_GATES), half(o_a), _cols(z, ts, WIDTH, CB_AZ),
                   half(o_b), _cols(z, ts, WIDTH, CB_BZ), half(o_m), _cols(z, ts, WIDTH, CB_MZ),
                   half(ya), half(yb), half(ym), _full(gdn_g.reshape(1, GDN_DIM)),
                   _full(b_merge.reshape(1, 3 * D)), _full(wb), _full(wout), _full(hsum)],
                  [((S, N_AL), BF16, (ts, 3 * D + 3 * WIDTH), lambda i: (i, CB_MERGE)),
                   _orow(S, (D,), BF16, ts), _orow(S, (D,), BF16, ts),
                   _orow(S, (D,), BF16, ts), w512(BF16), w512(BF16), w512(BF16),
                   ((128, S), F32, (128, ts), lambda i: (0, i)), _oacc((8, 3 * D), F32),
                   _oacc((8, GDN_DIM), F32)])


def _to_aligned(w):
    parts = [w[..., lo:lo + n] for lo, n, _ in sorted(W_IN_PIECES, key=lambda p: p[2])]
    parts.append(jnp.zeros(w.shape[:-1] + (N_AL - N_IN,), w.dtype))
    return jnp.concatenate(parts, axis=-1)


def _from_aligned(w):
    return jnp.concatenate([w[..., al:al + n] for _, n, al in W_IN_PIECES], axis=-1)


def _lanes128(v, lane0):
    return jnp.pad(v.astype(F32)[None, :], ((0, 0), (lane0, 128 - lane0 - v.shape[0])))


def _tiles(S):
    ts = min(512, S // 2)
    return dict(ts=ts, ts_small=min(256, S // 2), tq=min(512, S // 4), tq_fwd=min(1024, S // 2))


def _layer_fwd(x, mem, p):
    S = x.shape[0]
    tl = _tiles(S)
    ts, tss, tq = tl["ts"], tl["ts_small"], tl["tq"]
    h, rstd = _rms_fwd("norm_fwd", x, p["norm_g"], ts)
    z = _mm("in_proj", h, p["w_in_al"], tm=2048, tn=1664)

    b_fg128 = _lanes128(p["b_fg"], LANE_AF)
    f_hi, f_mid, f_lo = _fox_decay(z, b_fg128, ts)
    aq = z[:, CB_AQ * WIDTH:(CB_AQ + 1) * WIDTH]
    ak = z[:, CB_AK * WIDTH:(CB_AK + 1) * WIDTH]
    av = z[:, CB_AV * WIDTH:(CB_AV + 1) * WIDTH]
    q32 = _heads_major(aq, FOX_HEADS, FOX_DIM)
    kh = _heads_major(ak, FOX_HEADS, FOX_DIM).astype(BF16)
    vh = _heads_major(av, FOX_HEADS, FOX_DIM).astype(BF16)
    piecesT = jnp.stack([f[:FOX_HEADS] for f in (f_hi, f_mid, f_lo)], axis=1)
    pieces = piecesT.transpose(0, 2, 1)
    ones3 = jnp.ones((FOX_HEADS, S, 3), BF16)
    padk = jnp.zeros((FOX_HEADS, S, FOX_AUG - FOX_DIM - 6), BF16)
    q_aug = jnp.concatenate([q32, pieces.astype(F32), ones3.astype(F32), padk.astype(F32)], axis=-1)
    k_aug = jnp.concatenate([kh, ones3, -pieces, padk], axis=-1)
    kT_aug = jnp.concatenate([kh.transpose(0, 2, 1), ones3.transpose(0, 2, 1), -piecesT,
                              padk.transpose(0, 2, 1)], axis=1)
    v_aug = jnp.concatenate([vh, ones3[:, :, :1], jnp.zeros((FOX_HEADS, S, 128 - FOX_DIM - 1), BF16)],
                            axis=-1)
    o_h, lse, qs = _fox_fwd(q_aug, kT_aug, v_aug, tl["tq_fwd"])
    o_a = _heads_minor(o_h)

    a128 = _lanes128(p["a_log"], LANE_BA)
    dt128 = _lanes128(p["dt_bias"], LANE_BA)
    qkv, cpre, gb, gbT = _gdn_prep(z, p["conv_w"], a128, dt128, ts)
    grow = gbT[LANE_BA:LANE_BA + GDN_HEADS].reshape(GDN_HEADS, S // CHUNK, CHUNK)
    u, w, qd, kd, aqk, T = _gdn_local_fwd(qkv, gb, grow, ts)
    o_b, vn, states = _gdn_scan_fwd(u, w, qd, kd, aqk, gb, ts)

    mem_h, mem_r = _rms_fwd("mem_norm_fwd", mem, p["mem_norm_g"], mem.shape[0])
    mkv = _mm("mem_kv", mem_h, p["w_mem_kv"])
    mk, mv = mkv[:, :WIDTH], mkv[:, WIDTH:]
    o_m = _mem_attn_fwd(z, mk, mv, ts)

    x_next, ya, yb, ym, merged = _merge_fwd(x, z, o_a, o_b, o_m, p["gdn_norm_g"], p["b_merge"],
                                            p["w_branch"], p["w_out"], ts)
    saved = dict(x=x, h=h, rstd=rstd, z=z, b_fg128=b_fg128, qs=qs, k_aug=k_aug, kT_aug=kT_aug, v_aug=v_aug, lse=lse, o_a=o_a, a128=a128, dt128=dt128, qkv=qkv, cpre=cpre, gb=gb,
                 grow=grow, w=w, qd=qd, kd=kd, aqk=aqk, T=T, o_b=o_b, vn=vn, states=states,
                 mem_h=mem_h, mem_r=mem_r, mk=mk, mv=mv, o_m=o_m, ya=ya, yb=yb, ym=ym, merged=merged)
    return x_next, saved


def _layer_bwd(dout, mem, p, s):
    S = dout.shape[0]
    tl = _tiles(S)
    ts, tss, tq = tl["ts"], tl["ts_small"], tl["tq"]
    z = s["z"]
    hsum = (jnp.arange(128)[:, None] == jnp.arange(WIDTH)[None, :] // FOX_DIM).astype(F32)
    (dz, dpa, dpb, dpm, do_a, do_b, do_m, deltaT, db_merge, dgdn_g) = _merge_bwd(
        dout, z, s["o_a"], s["o_b"], s["o_m"], s["ya"], s["yb"], s["ym"], p["gdn_norm_g"],
        p["b_merge"], p["w_branch"], p["w_out"], hsum, tss)
    g = {}
    g["b_merge"] = db_merge[0]
    g["gdn_norm_g"] = dgdn_g[0]
    g["w_out"] = _mm("dw_out", s["merged"], dout, ta=True)
    g["w_branch"] = jnp.stack([_mm("dw_branch", y, dp, ta=True)
                               for y, dp in ((s["ya"], dpa), (s["yb"], dpb), (s["ym"], dpm))])

    do_h = _heads_major(do_a, FOX_HEADS, FOX_DIM).astype(BF16)
    delta_row = deltaT[:FOX_HEADS, None, :]
    dqT, dk_h, dv_h, dfk, dfq = _fox_bwd(s["qs"], s["k_aug"], s["kT_aug"], s["v_aug"], do_h, s["lse"],
                                         delta_row, tq)
    daq = _heads_minor(dqT.transpose(0, 2, 1))
    dak = _heads_minor(dk_h)
    dav = _heads_minor(dv_h)
    daf128, db_fg = _fox_decay_bwd(dfk[:, 0, :], dfq[:, 0, :], z, s["b_fg128"], ts)
    g["b_fg"] = db_fg[:FOX_HEADS]

    du, dw, dqd, dkd, daqk, dgl = _gdn_scan_bwd(do_b, s["w"], s["qd"], s["kd"], s["aqk"], s["vn"],
                                                s["states"], s["gb"], ts)
    dqkv, dgb = _gdn_local_bwd(s["qkv"], s["gb"], s["grow"], s["T"], du, dw, dqd, dkd, daqk, dgl, ts)
    dz, dzs_b, dconv, dad = _gdn_prep_bwd(dqkv, dgb, s["cpre"], z, p["conv_w"], s["a128"],
                                          s["dt128"], dz, ts)
    g["conv_w"] = dconv[:4]
    g["a_log"] = dad[0, LANE_BA:LANE_BA + GDN_HEADS]
    g["dt_bias"] = dad[1, LANE_BA:LANE_BA + GDN_HEADS]

    dz, dmk, dmv = _mem_attn_bwd(do_m, z, s["mk"], s["mv"], dz, ts)
    dmkv = jnp.concatenate([dmk, dmv], axis=1)
    g["w_mem_kv"] = _mm("dw_mem_kv", s["mem_h"], dmkv, ta=True)
    dmem_h = _mm("dmem_h", dmkv, p["w_mem_kv"], tb=True)
    M = mem.shape[0]
    _, g["mem_norm_g"] = _rms_bwd("mem_norm_bwd", dmem_h, mem, s["mem_r"], p["mem_norm_g"],
                                  jnp.zeros_like(mem), M)

    lane = jnp.arange(128)[None, :]
    dsmall = jnp.where(lane < 8, daf128, dzs_b)
    daqkv = jnp.concatenate([_b16(daq), _b16(dak), _b16(dav)], axis=1)
    dz = lax.dynamic_update_slice(dz, daqkv, (0, CB_AQKV * 3 * WIDTH))
    dz = lax.dynamic_update_slice(dz, _b16(dsmall), (0, CB_SMALL * 128))
    g["w_in_al"] = _mm("dw_in", s["h"], dz, ta=True, tn=1664, tk=2048)
    dh = _mm("dh", dz, p["w_in_al"], tb=True, tm=2048, tk=1664)
    dx, g["norm_g"] = _rms_bwd("norm_bwd", dh, s["x"], s["rstd"], p["norm_g"], dout, ts)
    return dx, g


def _local_step(x, mem, layers, final_norm_g, loss_target):
    S = x.shape[0]
    saves = []
    cur = x
    for p in layers:
        cur, sv = _layer_fwd(cur, mem, p)
        saves.append(sv)
    dx, dgf, loss_lanes = _loss_head(cur, final_norm_g, loss_target, _tiles(S)["ts"])
    grads = [None] * len(layers)
    for l in reversed(range(len(layers))):
        dx, grads[l] = _layer_bwd(dx, mem, layers[l], saves[l])
    return loss_lanes, dx, grads, dgf


HBM_SPEC = pl.BlockSpec(memory_space=pltpu.HBM)


def _mesh_pos():
    return lax.axis_index("x"), lax.axis_index("y"), lax.axis_index("c")


def _comm_call(name, body, arrays, out_shapes, n_remote, n_local):
    n = len(arrays)

    def kern(*refs):
        body(refs[:n], refs[n:2 * n], refs[2 * n], refs[2 * n + 1], refs[2 * n + 2])

    return pl.pallas_call(
        kern, name=name, out_shape=out_shapes, in_specs=[HBM_SPEC] * n, out_specs=[HBM_SPEC] * n,
        scratch_shapes=[pltpu.SemaphoreType.DMA((n_remote,)), pltpu.SemaphoreType.DMA((n_remote,)),
                        pltpu.SemaphoreType.DMA((max(n_local, 1),))],
    )(*arrays)


def _remote(src, dst, send_sems, recv_sems, k, to):
    return pltpu.make_async_remote_copy(src_ref=src, dst_ref=dst, send_sem=send_sems.at[k],
                                        recv_sem=recv_sems.at[k], device_id=to, device_id_type=MESH_ID)


def _other_chips(mx, my):
    return [(1 - mx, my), (mx, 1 - my), (1 - mx, 1 - my)]


def _gather_chips(name, shards):
    n = len(shards)

    def body(ins, outs, send_sems, recv_sems, local_sems):
        mx, my, mc = _mesh_pos()
        me = 2 * mx + my
        sibling = (mx, my, 1 - mc)
        chips = _other_chips(mx, my)
        sends = []
        for a in range(n):
            for k, (px, py) in enumerate(chips):
                cp = _remote(ins[a].at[mc], outs[a].at[me, mc], send_sems, recv_sems, 6 * a + k,
                             (px, py, mc))
                cp.start()
                sends.append(cp)
        for a in range(n):
            for k, (px, py) in enumerate(chips):
                j = 2 * px + py
                _remote(ins[a].at[mc], outs[a].at[j, mc], send_sems, recv_sems, 6 * a + k,
                        (px, py, mc)).wait_recv()
                cp = _remote(outs[a].at[j, mc], outs[a].at[j, mc], send_sems, recv_sems, 6 * a + 3 + k,
                             sibling)
                cp.start()
                sends.append(cp)
        for a in range(n):
            for k, (px, py) in enumerate(chips):
                j = 2 * px + py
                _remote(outs[a].at[j, 1 - mc], outs[a].at[j, 1 - mc], send_sems, recv_sems,
                        6 * a + 3 + k, sibling).wait_recv()
        for cp in sends:
            cp.wait_send()

    shapes = [jax.ShapeDtypeStruct((N_CHIPS,) + s.shape, s.dtype) for s in shards]
    outs = _comm_call(name, body, shards, shapes, 6 * n, 0)
    me = 2 * lax.axis_index("x") + lax.axis_index("y")
    return [lax.dynamic_update_index_in_dim(o, s, me, 0) for o, s in zip(outs, shards)]


def _sibling_swap(gs):
    n = len(gs)

    def body(ins, outs, send_sems, recv_sems, local_sems):
        mx, my, mc = _mesh_pos()
        sends = []
        for a in range(n):
            cp = _remote(ins[a].at[:, 1 - mc], outs[a], send_sems, recv_sems, a, (mx, my, 1 - mc))
            cp.start()
            sends.append(cp)
        for cp in sends:
            cp.wait()

    shapes = [jax.ShapeDtypeStruct((g.shape[0],) + g.shape[2:], g.dtype) for g in gs]
    return _comm_call("grad_sibling_swap", body, gs, shapes, n, 0)


def _chip_exchange(ps):
    n = len(ps)

    def body(ins, outs, send_sems, recv_sems, local_sems):
        mx, my, mc = _mesh_pos()
        me = 2 * mx + my
        chips = _other_chips(mx, my)
        sends = []
        for a in range(n):
            for k, (px, py) in enumerate(chips):
                cp = _remote(ins[a].at[2 * px + py], outs[a].at[me], send_sems, recv_sems, 3 * a + k,
                             (px, py, mc))
                cp.start()
                sends.append(cp)
        for a in range(n):
            for k, (px, py) in enumerate(chips):
                _remote(ins[a].at[me], outs[a].at[2 * px + py], send_sems, recv_sems, 3 * a + k,
                        (px, py, mc)).wait_recv()
        for cp in sends:
            cp.wait_send()

    shapes = [jax.ShapeDtypeStruct(p.shape, p.dtype) for p in ps]
    outs = _comm_call("grad_chip_exchange", body, ps, shapes, 3 * n, 0)
    me = 2 * lax.axis_index("x") + lax.axis_index("y")
    return [lax.dynamic_update_index_in_dim(o, lax.dynamic_index_in_dim(p, me, 0, keepdims=False), me, 0)
            for o, p in zip(outs, ps)]


def _sibling_gather(hs):
    n = len(hs)

    def body(ins, outs, send_sems, recv_sems, local_sems):
        mx, my, mc = _mesh_pos()
        sends = []
        for a in range(n):
            cp = _remote(ins[a], outs[a], send_sems, recv_sems, a, (mx, my, 1 - mc))
            cp.start()
            sends.append(cp)
        for cp in sends:
            cp.wait()

    shapes = [jax.ShapeDtypeStruct(h.shape, h.dtype) for h in hs]
    theirs = _comm_call("grad_sibling_gather", body, hs, shapes, n, 0)
    first = lax.axis_index("c") == 0
    return [jnp.stack([jnp.where(first, h, t), jnp.where(first, t, h)]) for h, t in zip(hs, theirs)]


def _add_pairs(a, b, tr, out_dtype):
    n, H, C = a.shape

    def kern(a_ref, b_ref, o_ref):
        o_ref[...] = (a_ref[...] + b_ref[...]).astype(o_ref.dtype)

    spec = pl.BlockSpec((None, tr, C), lambda j, i: (j, i, 0))
    return pl.pallas_call(
        kern, name="grad_pair_sum", grid=(n, H // tr), in_specs=[spec, spec], out_specs=spec,
        out_shape=jax.ShapeDtypeStruct((n, H, C), out_dtype),
        compiler_params=_params(("parallel", "parallel")),
    )(a, b)


def _sum_slots(r4, tr):
    n, H, C = r4.shape

    def kern(r_ref, o_ref):
        f = lambda k: r_ref[k].astype(F32)
        o_ref[...] = ((f(0) + f(1)) + f(2)) + f(3)

    return pl.pallas_call(
        kern, name="grad_chip_sum", grid=(H // tr,),
        in_specs=[pl.BlockSpec((n, tr, C), lambda i: (0, i, 0))],
        out_specs=pl.BlockSpec((tr, C), lambda i: (i, 0)),
        out_shape=jax.ShapeDtypeStruct((H, C), F32),
        compiler_params=_params(("parallel",)),
    )(r4)


def _adamw(w, g, m, v, tr):
    R, C = w.shape
    c1 = 1.0 - ADAM_B1
    c2 = 1.0 - ADAM_B2
    bc1 = 1.0 - ADAM_B1 ** ADAM_STEP
    bc2 = 1.0 - ADAM_B2 ** ADAM_STEP

    def kern(w_ref, g_ref, m_ref, v_ref, d_ref, mo_ref, vo_ref):
        gv = g_ref[...]
        mn = ADAM_B1 * m_ref[...] + c1 * gv
        vn = ADAM_B2 * v_ref[...] + c2 * (gv * gv)
        m_hat = mn / bc1
        v_hat = vn / bc2
        d_ref[...] = -ADAM_LR * (m_hat / (jnp.sqrt(v_hat) + ADAM_EPS) + ADAM_WD * w_ref[...])
        mo_ref[...] = mn
        vo_ref[...] = vn

    spec = pl.BlockSpec((tr, C), lambda i: (i, 0))
    shape = jax.ShapeDtypeStruct((R, C), F32)
    return pl.pallas_call(
        kern, name="adamw", grid=(R // tr,), in_specs=[spec] * 4, out_specs=[spec] * 3,
        out_shape=[shape] * 3, compiler_params=_params(("parallel",)),
    )(w, g, m, v)


PACK_COLS = 1024
PACK_ROWS = 512
W_SHARD = N_IN // N_CHIPS
SLAB = ("conv_w", "w_mem_kv", "w_branch", "w_out")
SMALL =("norm_g", "b_fg", "b_merge", "a_log", "dt_bias", "gdn_norm_g", "mem_norm_g", "final_norm_g")
ALL_WEIGHTS = ("norm_g", "w_in", "b_fg", "b_merge", "conv_w", "a_log", "dt_bias", "gdn_norm_g",
               "mem_norm_g", "w_mem_kv", "w_branch", "w_out", "final_norm_g")
SHARD_AXIS = {"w_in": 2, "conv_w": 2, "w_mem_kv": 1, "w_branch": 3, "w_out": 1}


def _pack(arrays, row_multiple):
    flat = jnp.concatenate([a.reshape(-1) for a in arrays])
    n = flat.shape[0]
    rows = -(-n // PACK_COLS)
    rows = -(-rows // row_multiple) * row_multiple
    flat = jnp.pad(flat, (0, rows * PACK_COLS - n))
    return flat.reshape(rows, PACK_COLS)


def _unpack(slab, shapes):
    out, off = [], 0
    for shp in shapes:
        n = 1
        for d in shp:
            n *= d
        r0, r1 = off // PACK_COLS, -(-(off + n) // PACK_COLS)
        rows = slab[r0:r1].reshape(-1)
        out.append(rows[off - r0 * PACK_COLS:off - r0 * PACK_COLS + n].reshape(shp))
        off += n
    return out


def _shard_of(full, name, j):
    ax = SHARD_AXIS[name]
    n = full.shape[ax] // N_CHIPS
    return lax.slice_in_dim(full, j * n, (j + 1) * n, axis=ax)


def _aligned_from_shards(shards):
    def cols(lo, n):
        parts = []
        while n > 0:
            j, off = divmod(lo, W_SHARD)
            take = min(n, W_SHARD - off)
            parts.append(shards[j][..., off:off + take])
            lo, n = lo + take, n - take
        return parts

    out = []
    for lo, n, _ in sorted(W_IN_PIECES, key=lambda p: p[2]):
        out += cols(lo, n)
    out.append(jnp.zeros(shards[0].shape[:-1] + (N_AL - N_IN,), shards[0].dtype))
    return jnp.concatenate(out, axis=-1)


def _shard_from_aligned(w_al, j):
    lo_j, hi_j = j * W_SHARD, (j + 1) * W_SHARD
    parts = []
    for lo, n, al in W_IN_PIECES:
        a, b = max(lo, lo_j), min(lo + n, hi_j)
        if a < b:
            parts.append(w_al[..., al + a - lo:al + b - lo])
    return jnp.concatenate(parts, axis=-1)


def kernel(x, mem, norm_g, w_in, b_fg, b_merge, conv_w, a_log, dt_bias, gdn_norm_g, mem_norm_g, w_mem_kv, w_branch, w_out, final_norm_g, loss_target, m_norm_g, m_w_in, m_b_fg, m_b_merge, m_conv_w, m_a_log, m_dt_bias, m_gdn_norm_g, m_mem_norm_g, m_w_mem_kv, m_w_branch, m_w_out, m_final_norm_g, v_norm_g, v_w_in, v_b_fg, v_b_merge, v_conv_w, v_a_log, v_dt_bias, v_gdn_norm_g, v_mem_norm_g, v_w_mem_kv, v_w_branch, v_w_out, v_final_norm_g):
    wts = dict(norm_g=norm_g, w_in=w_in, b_fg=b_fg, b_merge=b_merge, conv_w=conv_w, a_log=a_log,
               dt_bias=dt_bias, gdn_norm_g=gdn_norm_g, mem_norm_g=mem_norm_g, w_mem_kv=w_mem_kv,
               w_branch=w_branch, w_out=w_out, final_norm_g=final_norm_g)
    mom = dict(norm_g=m_norm_g, w_in=m_w_in, b_fg=m_b_fg, b_merge=m_b_merge, conv_w=m_conv_w,
               a_log=m_a_log, dt_bias=m_dt_bias, gdn_norm_g=m_gdn_norm_g, mem_norm_g=m_mem_norm_g,
               w_mem_kv=m_w_mem_kv, w_branch=m_w_branch, w_out=m_w_out, final_norm_g=m_final_norm_g)
    vel = dict(norm_g=v_norm_g, w_in=v_w_in, b_fg=v_b_fg, b_merge=v_b_merge, conv_w=v_conv_w,
               a_log=v_a_log, dt_bias=v_dt_bias, gdn_norm_g=v_gdn_norm_g, mem_norm_g=v_mem_norm_g,
               w_mem_kv=v_w_mem_kv, w_branch=v_w_branch, w_out=v_w_out, final_norm_g=v_final_norm_g)

    big = ("w_in", "w_mem_kv", "w_branch", "w_out")
    gathered = _gather_chips("weight_gather", [wts[n].astype(BF16) for n in big] + [conv_w])
    all_w = dict(zip(big + ("conv_w",), gathered))
    w_in_al = _aligned_from_shards([all_w["w_in"][j] for j in range(N_CHIPS)])

    layers = []
    for l in range(DEPTH):
        rows_of = lambda n: all_w[n][:, l].reshape(D_MODEL, D_MODEL)
        last_of = lambda n: jnp.concatenate([all_w[n][j, l] for j in range(N_CHIPS)], axis=-1)
        layers.append(dict(norm_g=norm_g[l], w_in_al=w_in_al[l], b_fg=b_fg[l], b_merge=b_merge[l],
                           conv_w=jnp.pad(last_of("conv_w"), ((0, 4), (0, 0))), a_log=a_log[l],
                           dt_bias=dt_bias[l], gdn_norm_g=gdn_norm_g[l], mem_norm_g=mem_norm_g[l],
                           w_mem_kv=rows_of("w_mem_kv"), w_branch=last_of("w_branch"),
                           w_out=rows_of("w_out")))

    loss_lanes, dx, grads, dgf = _local_step(x[0], mem[0], layers, final_norm_g, loss_target[0])

    gfull = {n: jnp.stack([grads[l][n] for l in range(DEPTH)])
             for n in ("norm_g", "b_fg", "b_merge", "conv_w", "a_log", "dt_bias", "gdn_norm_g",
                       "mem_norm_g", "w_mem_kv", "w_branch", "w_out")}
    gfull["final_norm_g"] = dgf
    loss_local = jnp.sum(loss_lanes).reshape(1)
    small_g = [gfull[n] for n in SMALL] + [loss_local]
    dw_al = jnp.stack([grads[l]["w_in_al"] for l in range(DEPTH)])
    ga = jnp.stack([_shard_from_aligned(dw_al, j) for j in range(N_CHIPS)])
    mats = ("w_mem_kv", "w_branch", "w_out")
    rest = ("conv_w",) + SMALL
    gb = jnp.stack([_pack([_shard_of(gfull[n], n, j) for n in mats], PACK_ROWS)
                    for j in range(N_CHIPS)])
    gc = jnp.stack([_pack([_shard_of(gfull["conv_w"], "conv_w", j)] + small_g, 16)
                    for j in range(N_CHIPS)])
    halves = lambda g: g.reshape(N_CHIPS, 2, g.shape[1] // 2, PACK_COLS)
    gb, gc = halves(gb), halves(gc)

    mc = lax.axis_index("c")
    tr = 256
    trs = (tr, tr, 8)
    from_sibling = _sibling_swap([ga, gb, gc])
    mine = [lax.dynamic_index_in_dim(g, mc, axis=1, keepdims=False) for g in (ga, gb, gc)]
    pair = [_add_pairs(a, b, t, dt) for a, b, t, dt in zip(mine, from_sibling, trs, (BF16, BF16, F32))]
    slots = _chip_exchange(pair)
    half = [_sum_slots(s, t) for s, t in zip(slots, trs)]
    ga_sum, gb_sum, gc_sum = _sibling_gather(half)
    flat = lambda g: g.reshape(-1, PACK_COLS)

    g_un = dict(zip(mats, _unpack(flat(gb_sum), [wts[n].shape for n in mats])))
    g_un.update(zip(rest + ("loss",), _unpack(flat(gc_sum), [wts[n].shape for n in rest] + [(1,)])))
    g_un["w_in"] = ga_sum
    d_un, m_un, v_un = {}, {}, {}
    rows2d = lambda a: a.reshape(-1, a.shape[-1])
    for n in ("w_in", "w_mem_kv", "w_branch", "w_out"):
        res = _adamw(rows2d(wts[n]), rows2d(g_un[n]), rows2d(mom[n]), rows2d(vel[n]), tr)
        d_un[n], m_un[n], v_un[n] = [r.reshape(wts[n].shape) for r in res]
    little = ("conv_w",) + SMALL
    slab = lambda d: _pack([d[n] for n in little], 8)
    res = _adamw(slab(wts), slab(g_un), slab(mom), slab(vel), 8)
    little_shapes = [wts[n].shape for n in little]
    for out, r in zip((d_un, m_un, v_un), res):
        out.update(zip(little, _unpack(r, little_shapes)))

    loss = g_un["loss"][0]
    return (loss, dx[None], *[g_un[n] for n in ALL_WEIGHTS], *[d_un[n] for n in ALL_WEIGHTS],
            *[m_un[n] for n in ALL_WEIGHTS], *[v_un[n] for n in ALL_WEIGHTS])
```

```python
import functools

import jax
import jax.numpy as jnp
from jax import lax
from jax.experimental import pallas as pl
from jax.experimental.pallas import tpu as pltpu

F32 = jnp.float32
BF16 = jnp.bfloat16
HIGHEST = lax.Precision.HIGHEST
PREC_UT = lax.Precision.HIGH
MESH_ID = pl.DeviceIdType.MESH

D_MODEL = 1024
DEPTH = 2
CHUNK = 64
EPS = 1e-6
FOX_HEADS, FOX_DIM = 8, 64
GDN_HEADS, GDN_DIM = 4, 128
MEM_HEADS, MEM_DIM = 4, 128
WIDTH = 512
N_BRANCH = 3
N_IN = 8208
N_AL = 8320
N_CHIPS = 4
NEG = -1e30
LOG2E = 1.4426950408889634
LN2 = 0.6931471805599453

ADAM_LR, ADAM_B1, ADAM_B2, ADAM_EPS, ADAM_WD, ADAM_STEP = 0.001, 0.9, 0.999, 1e-08, 0.01, 10

CB_GATES = 0
CB_AZ, CB_BZ, CB_MZ = 6, 7, 8
CB_MERGE = 0
CB_BQKV = 3
CB_AQ, CB_AK, CB_AV = 12, 13, 14
CB_AQKV = 4
CB_MQ = 15
CB_SMALL = 64
W_IN_PIECES = ((0, 512, 6144), (512, 512, 6656), (1024, 512, 7168), (1536, 8, 8192), (1544, 512, 3072),
               (2056, 512, 4608), (2568, 512, 5120), (3080, 512, 5632), (3592, 4, 8200), (3596, 4, 8204),
               (3600, 512, 3584), (4112, 512, 7680), (4624, 512, 4096), (5136, 3072, 0))
LANE_AF, LANE_BA, LANE_BB = 0, 8, 12

NN = ((1,), (0,))
NT = ((1,), (1,))
TN = ((0,), (0,))

VMEM_LIMIT_BYTES = 56 * 1024 * 1024


def _dot(a, b, dims=NN, prec=None):
    return lax.dot_general(a, b, (dims, ((), ())), preferred_element_type=F32, precision=prec)


def _bdot(a, b, ca, cb, prec=None):
    return lax.dot_general(a, b, (((ca,), (cb,)), ((0,), (0,))), preferred_element_type=F32,
                           precision=prec)


def _b16(a):
    return a.astype(BF16)


def _eye(n, dtype=F32):
    r = lax.broadcasted_iota(jnp.int32, (n, n), 0)
    c = lax.broadcasted_iota(jnp.int32, (n, n), 1)
    return jnp.where(r == c, 1.0, 0.0).astype(dtype)


def _transpose_exact(x):
    return _dot(_eye(x.shape[1]), x, NT, HIGHEST)


def _col_to_row(col):
    n = col.shape[0]
    wide = jnp.broadcast_to(col, (n, 128))
    return _dot(_eye(128)[0:8], wide, NT, HIGHEST)[0:1, :]


def _row_to_col(row):
    n = row.shape[1]
    return jnp.sum(jnp.where(_eye(n) > 0.5, row, 0.0), axis=1, keepdims=True)


def _sigmoid(x):
    return 1.0 / (1.0 + jnp.exp(-x))


def _softplus(x):
    return jnp.maximum(x, 0.0) + jnp.log(1.0 + jnp.exp(-jnp.abs(x)))


def _silu_and_grad(x):
    s = _sigmoid(x)
    return x * s, s * (1.0 + x * (1.0 - s))


def _params(semantics):
    return pltpu.CompilerParams(dimension_semantics=semantics, vmem_limit_bytes=VMEM_LIMIT_BYTES)


def _rows(a, ts):
    nd = a.ndim
    return (a, (ts,) + a.shape[1:], lambda i, nd=nd: (i,) + (0,) * (nd - 1))


def _cols(a, ts, width, cb):
    return (a, (ts, width), lambda i, cb=cb: (i, cb))


def _full(a):
    nd = a.ndim
    return (a, a.shape, lambda i, nd=nd: (0,) * nd)


def _orow(S, tail, dtype, ts):
    nd = 1 + len(tail)
    return ((S,) + tuple(tail), dtype, (ts,) + tuple(tail), lambda i, nd=nd: (i,) + (0,) * (nd - 1))


def _oacc(shape, dtype):
    nd = len(shape)
    return (tuple(shape), dtype, tuple(shape), lambda i, nd=nd: (0,) * nd)


def _tiled(name, body, n_steps, ins, outs, scratch=(), reverse=False, fill=None):
    def rev(imap):
        if not reverse:
            return imap
        return lambda i: imap(n_steps - 1 - i)

    in_specs = [pl.BlockSpec(blk, rev(imap)) for (_, blk, imap) in ins]
    out_specs = [pl.BlockSpec(blk, rev(imap)) for (_, _, blk, imap) in outs]
    out_shape = [jax.ShapeDtypeStruct(shape, dt) for (shape, dt, _, _) in outs]
    n_in, n_out = len(ins), len(outs)
    arrays = [a for (a, _, _) in ins]
    aliases = {}
    n_extra = 0
    if fill is not None:
        arrays.append(fill[0])
        in_specs.append(pl.BlockSpec(memory_space=pl.ANY))
        aliases = {n_in: fill[1]}
        n_extra = 1

    def kern(*refs):
        step = pl.program_id(0)
        t = (n_steps - 1 - step) if reverse else step
        lo = n_in + n_extra
        body(t, step == 0, refs[:n_in], refs[lo:lo + n_out], refs[lo + n_out:])

    res = pl.pallas_call(
        kern, name=name, grid=(n_steps,), in_specs=in_specs, out_specs=out_specs,
        out_shape=out_shape, scratch_shapes=list(scratch), input_output_aliases=aliases,
        compiler_params=_params(("arbitrary",)),
    )(*arrays)
    return res


def _pick(n, pref):
    if n <= pref:
        return n
    best = None
    for t in range(128, pref + 1, 128):
        if n % t == 0:
            best = t
    assert best is not None, (n, pref)
    return best


def _mm(name, a, b, ta=False, tb=False, out_dtype=F32, tm=1024, tn=1024, tk=1024):
    if ta:
        K, M = a.shape
    else:
        M, K = a.shape
    if tb:
        N, K2 = b.shape
    else:
        K2, N = b.shape
    assert K == K2, (a.shape, b.shape, ta, tb)
    tm, tn, tk = _pick(M, tm), _pick(N, tn), _pick(K, tk)
    nk = K // tk
    a_spec = (pl.BlockSpec((tk, tm), lambda i, j, k: (k, i)) if ta
              else pl.BlockSpec((tm, tk), lambda i, j, k: (i, k)))
    b_spec = (pl.BlockSpec((tn, tk), lambda i, j, k: (j, k)) if tb
              else pl.BlockSpec((tk, tn), lambda i, j, k: (k, j)))
    dims = ((0,) if ta else (1,), (1,) if tb else (0,))

    def kern_single(a_ref, b_ref, o_ref):
        o_ref[...] = _dot(_b16(a_ref[...]), _b16(b_ref[...]), dims).astype(o_ref.dtype)

    def kern_acc(a_ref, b_ref, o_ref, acc_ref):
        k = pl.program_id(2)

        @pl.when(k == 0)
        def _():
            acc_ref[...] = jnp.zeros_like(acc_ref)

        acc_ref[...] += _dot(_b16(a_ref[...]), _b16(b_ref[...]), dims)

        @pl.when(k == nk - 1)
        def _():
            o_ref[...] = acc_ref[...].astype(o_ref.dtype)

    return pl.pallas_call(
        kern_single if nk == 1 else kern_acc, name=name, grid=(M // tm, N // tn, nk),
        in_specs=[a_spec, b_spec],
        out_specs=pl.BlockSpec((tm, tn), lambda i, j, k: (i, j)),
        out_shape=jax.ShapeDtypeStruct((M, N), out_dtype),
        scratch_shapes=[] if nk == 1 else [pltpu.VMEM((tm, tn), F32)],
        compiler_params=_params(("parallel", "parallel", "arbitrary")),
    )(a, b)


def _rms_fwd(name, x, g, ts):
    S, D = x.shape

    def body(t, first, ins, outs, scratch):
        x_ref, g_ref = ins
        h_ref, r_ref = outs
        xv = x_ref[...]
        r = lax.rsqrt(jnp.mean(xv * xv, axis=1, keepdims=True) + EPS)
        h_ref[...] = (xv * r * g_ref[...]).astype(h_ref.dtype)
        r_ref[...] = r

    return _tiled(name, body, S // ts, [_rows(x, ts), _full(g.reshape(1, D))],
                  [_orow(S, (D,), BF16, ts), _orow(S, (1,), F32, ts)])


def _rms_bwd(name, dh, x, rstd, g, dres, ts):
    S, D = x.shape

    def body(t, first, ins, outs, scratch):
        dh_ref, x_ref, r_ref, g_ref, dres_ref = ins
        dx_ref, dg_ref = outs
        r = r_ref[...]
        xh = x_ref[...] * r
        dhv = dh_ref[...]
        dxh = dhv * g_ref[...]
        dx_ref[...] = dres_ref[...] + r * (dxh - xh * jnp.mean(dxh * xh, axis=1, keepdims=True))

        @pl.when(first)
        def _():
            dg_ref[...] = jnp.zeros_like(dg_ref)

        dg_ref[0:1, :] += jnp.sum(dhv * xh, axis=0, keepdims=True)

    dx, dg = _tiled(name, body, S // ts,
                    [_rows(dh, ts), _rows(x, ts), _rows(rstd, ts), _full(g.reshape(1, D)), _rows(dres, ts)],
                    [_orow(S, (D,), F32, ts), _oacc((8, D), F32)])
    return dx, dg[0]


def _loss_head(x, g, target, ts):
    S, D = x.shape

    def body(t, first, ins, outs, scratch):
        x_ref, g_ref, tgt_ref = ins
        dx_ref, dg_ref, loss_ref = outs
        xv = x_ref[...]
        gv = g_ref[...]
        r = lax.rsqrt(jnp.mean(xv * xv, axis=1, keepdims=True) + EPS)
        xh = xv * r
        err = xh * gv - tgt_ref[...]
        dy = err * (1.0 / D)
        dxh = dy * gv
        dx_ref[...] = r * (dxh - xh * jnp.mean(dxh * xh, axis=1, keepdims=True))

        @pl.when(first)
        def _():
            dg_ref[...] = jnp.zeros_like(dg_ref)
            loss_ref[...] = jnp.zeros_like(loss_ref)

        dg_ref[0:1, :] += jnp.sum(dy * xh, axis=0, keepdims=True)
        per_lane = jnp.sum(err * err, axis=0, keepdims=True)
        loss_ref[0:1, :] += per_lane * (0.5 / D)

    dx, dg, loss = _tiled("loss_head", body, S // ts,
                          [_rows(x, ts), _full(g.reshape(1, D)), _rows(target, ts)],
                          [_orow(S, (D,), F32, ts), _oacc((8, D), F32), _oacc((8, D), F32)])
    return dx, dg[0], loss[0]


def _scan_rows(x, length, seg, reverse=False):
    row = lax.broadcasted_iota(jnp.int32, x.shape, 0) % seg
    k = 1
    while k < seg:
        if reverse:
            x = x + jnp.where(row < seg - k, pltpu.roll(x, length - k, 0), 0.0)
        else:
            x = x + jnp.where(row >= k, pltpu.roll(x, k, 0), 0.0)
        k *= 2
    return x


def _fox_decay(z, b_fg128, ts):
    S = z.shape[0]

    def body(t, first, ins, outs, scratch):
        zs_ref, b_ref = ins
        hi_ref, mid_ref, lo_ref = outs
        (carry,) = scratch

        @pl.when(first)
        def _():
            carry[...] = jnp.zeros_like(carry)

        logf = -_softplus(-(zs_ref[...] + b_ref[...]))
        run = _scan_rows(logf, ts, ts) + carry[0:1, :]
        carry[0:1, :] = run[ts - 1:ts, :]
        f2 = run * LOG2E
        hi = f2.astype(BF16)
        r1 = f2 - hi.astype(F32)
        mid = r1.astype(BF16)
        lo = (r1 - mid.astype(F32)).astype(BF16)
        eye = _eye(128, BF16)
        hi_ref[...] = _dot(eye, hi, NT).astype(BF16)
        mid_ref[...] = _dot(eye, mid, NT).astype(BF16)
        lo_ref[...] = _dot(eye, lo, NT).astype(BF16)

    tcol = lambda dt: ((128, S), dt, (128, ts), lambda i: (0, i))
    return _tiled("fox_decay", body, S // ts,
                  [_cols(z, ts, 128, CB_SMALL), _full(b_fg128)],
                  [tcol(BF16), tcol(BF16), tcol(BF16)], scratch=[pltpu.VMEM((8, 128), F32)])


def _fox_decay_bwd(dfk_rows, dfq_rows, z, b_fg128, ts):
    S = z.shape[0]
    H = dfk_rows.shape[0]

    def body(t, first, ins, outs, scratch):
        dfk_ref, dfq_ref, zs_ref, b_ref = ins
        daf_ref, db_ref = outs
        (carry,) = scratch

        @pl.when(first)
        def _():
            carry[...] = jnp.zeros_like(carry)
            db_ref[...] = jnp.zeros_like(db_ref)

        r = lax.broadcasted_iota(jnp.int32, (H, 128), 0)
        c = lax.broadcasted_iota(jnp.int32, (H, 128), 1)
        place = jnp.where(r == c, 1.0, 0.0)
        df = _dot(dfk_ref[...] + dfq_ref[...], place, TN, HIGHEST)
        run = _scan_rows(df, ts, ts, reverse=True) + carry[0:1, :]
        carry[0:1, :] = run[0:1, :]
        daf = run * _sigmoid(-(zs_ref[...] + b_ref[...]))
        daf_ref[...] = daf
        db_ref[0:1, :] += jnp.sum(daf, axis=0, keepdims=True)

    rowsin = lambda a: (a, (H, ts), lambda i: (0, i))
    daf, db = _tiled("fox_decay_bwd", body, S // ts,
                     [rowsin(dfk_rows), rowsin(dfq_rows), _cols(z, ts, 128, CB_SMALL), _full(b_fg128)],
                     [_orow(S, (128,), F32, ts), _oacc((8, 128), F32)],
                     scratch=[pltpu.VMEM((8, 128), F32)], reverse=True)
    return daf, db[0]


FOX_AUG = 80


def _fox_fwd(q_aug, kT_aug, v_aug, tq):
    H, S, da = q_aug.shape
    dv = v_aug.shape[2]
    d = FOX_DIM
    tk = tq // 2
    qscale = (d ** -0.5) * LOG2E

    def kern(q_ref, kT_ref, v_ref, o_ref, lse_ref, qs_ref, s_buf, p_buf, m_scr, acc_scr):
        i = pl.program_id(1)
        col = lax.broadcasted_iota(jnp.int32, (1, da), 1)
        qb = _b16(q_ref[...] * jnp.where(col < d, qscale, 1.0))
        qs_ref[...] = qb

        def keys(t):
            return pl.ds(pl.multiple_of(t * tk, tk), tk)

        def stage(t, slot, mask_off, look_ahead):
            if look_ahead:
                s_buf[1 - slot] = _dot(qb, kT_ref[:, keys(t + 1)])
            pv = _dot(p_buf[1 - slot], v_ref[keys(jnp.maximum(t - 1, 0)), :])

            def scores():
                s = s_buf[slot]
                if mask_off is None:
                    return s
                r = lax.broadcasted_iota(jnp.int32, (tq, tk), 0)
                c = lax.broadcasted_iota(jnp.int32, (tq, tk), 1)
                return jnp.where(c + mask_off <= r, s, NEG)

            m = m_scr[...]
            m_new = jnp.maximum(m, jnp.max(scores(), axis=1, keepdims=True))
            alpha = jnp.exp2(m - m_new)
            p_buf[slot] = _b16(jnp.exp2(scores() - m_new))
            m_scr[...] = m_new
            acc_scr[...] = (acc_scr[...] + pv) * alpha

        s_buf[0] = _dot(qb, kT_ref[:, keys(0)])
        p_buf[1] = jnp.zeros((tq, tk), BF16)
        m_scr[...] = jnp.full((tq, 1), NEG, F32)
        acc_scr[...] = jnp.zeros((tq, dv), F32)

        def pair(n):
            stage(2 * n, 0, None, True)
            stage(2 * n + 1, 1, None, True)

        def quad(m, _):
            pair(2 * m)
            pair(2 * m + 1)
            return 0

        lax.fori_loop(0, i // 2, quad, 0)

        @pl.when(i % 2 == 1)
        def _():
            pair(i - 1)

        stage(2 * i, 0, 0, True)
        stage(2 * i + 1, 1, tk, False)
        acc = acc_scr[...] + _dot(p_buf[1], v_ref[keys(2 * i + 1), :])
        l = acc[:, d:d + 1]
        o_ref[...] = acc[:, :d] / l
        lse_ref[...] = _col_to_row(m_scr[...] + jnp.log(l) * LOG2E)

    return pl.pallas_call(
        kern, name="fox_fwd", grid=(H, S // tq),
        in_specs=[pl.BlockSpec((None, tq, da), lambda h, i: (h, i, 0)),
                  pl.BlockSpec((None, da, S), lambda h, i: (h, 0, 0)),
                  pl.BlockSpec((None, S, dv), lambda h, i: (h, 0, 0))],
        out_specs=[pl.BlockSpec((None, tq, d), lambda h, i: (h, i, 0)),
                   pl.BlockSpec((None, 1, tq), lambda h, i: (h, 0, i)),
                   pl.BlockSpec((None, tq, da), lambda h, i: (h, i, 0))],
        out_shape=[jax.ShapeDtypeStruct((H, S, d), F32), jax.ShapeDtypeStruct((H, 1, S), F32),
                   jax.ShapeDtypeStruct((H, S, da), BF16)],
        scratch_shapes=[pltpu.VMEM((2, tq, tk), F32), pltpu.VMEM((2, tq, tk), BF16),
                        pltpu.VMEM((tq, 1), F32), pltpu.VMEM((tq, dv), F32)],
        compiler_params=_params(("parallel", "arbitrary")),
    )(q_aug, kT_aug, v_aug)


def _fox_bwd(qs, k_aug, kT, v, do, lse_row, delta_row, tq):
    H, S, da = qs.shape
    d = FOX_DIM
    tk = tq
    nq = S // tq
    scale = d ** -0.5

    ts2 = tq // 2
    last = 2 * nq - 1

    def kern(q_ref, k_ref, kT_ref, v_ref, do_ref, lse_ref, dl_ref,
             dqT_ref, dk_ref, dv_ref, dfk_ref, dfq_ref,
             kq_buf, dp_buf, pb_buf, ds_buf, dk_scr, dv_scr, dfk_scr):
        j = pl.program_id(1)

        @pl.when(j == 0)
        def _():
            dqT_ref[...] = jnp.zeros_like(dqT_ref)
            dfq_ref[...] = jnp.zeros_like(dfq_ref)

        kb = k_ref[...]
        kTb = kT_ref[...]
        vb = v_ref[:, :d]
        dk_scr[...] = jnp.zeros_like(dk_scr)
        dv_scr[...] = jnp.zeros_like(dv_scr)
        dfk_scr[...] = jnp.zeros_like(dfk_scr)

        def queries(t):
            return pl.ds(pl.multiple_of(t * ts2, ts2), ts2)

        def products(t, slot):
            rows = queries(t)
            kq_buf[slot] = _dot(kb, q_ref[rows, :], NT)
            dp_buf[slot] = _dot(vb, do_ref[rows, :], NT)

        def pointwise(t, slot, mask_off):
            rows = queries(t)
            sT = kq_buf[slot]
            if mask_off is not None:
                r = lax.broadcasted_iota(jnp.int32, (tk, ts2), 0)
                c = lax.broadcasted_iota(jnp.int32, (tk, ts2), 1)
                sT = jnp.where(r <= c + mask_off, sT, NEG)
            pT = jnp.exp2(sT - lse_ref[:, rows])
            dsT = pT * (dp_buf[slot] - dl_ref[:, rows])
            pb_buf[slot] = _b16(pT)
            ds_buf[slot] = _b16(dsT)
            dfk_scr[...] -= jnp.sum(dsT, axis=1, keepdims=True)
            dfq_ref[:, rows] += jnp.sum(dsT, axis=0, keepdims=True)

        def accumulate(t, slot):
            rows = queries(t)
            dsb = ds_buf[slot]
            dv_scr[...] += _dot(pb_buf[slot], do_ref[rows, :])
            dk_scr[...] += _dot(dsb, q_ref[rows, :])
            dqT_ref[:, rows] += _dot(kTb, dsb) * scale

        def stage(t, slot, mask_off, has_prev):
            products(jnp.minimum(t + 1, last), 1 - slot)
            if has_prev:
                accumulate(t - 1, 1 - slot)
            pointwise(t, slot, mask_off)

        products(2 * j, 0)
        stage(2 * j, 0, 0, False)
        stage(2 * j + 1, 1, ts2, True)

        def pair(n):
            stage(2 * n, 0, None, True)
            stage(2 * n + 1, 1, None, True)

        def quad(m, _):
            pair(j + 1 + 2 * m)
            pair(j + 2 + 2 * m)
            return 0

        n_rest = nq - 1 - j
        lax.fori_loop(0, n_rest // 2, quad, 0)

        @pl.when(n_rest % 2 == 1)
        def _():
            pair(nq - 1)

        accumulate(last, 1)
        dk_ref[...] = dk_scr[:, :d] * LN2
        dv_ref[...] = dv_scr[...]
        dfk_ref[...] = _col_to_row(dfk_scr[...])

    tile = lambda h, j: (h, j, 0)
    whole = lambda h, j: (h, 0, 0)
    rowtile = lambda h, j: (h, 0, j)
    return pl.pallas_call(
        kern, name="fox_bwd", grid=(H, S // tk),
        in_specs=[pl.BlockSpec((None, S, da), whole),
                  pl.BlockSpec((None, tk, da), tile),
                  pl.BlockSpec((None, d, tk), lambda h, j: (h, 0, j)),
                  pl.BlockSpec((None, tk, 128), tile),
                  pl.BlockSpec((None, S, d), whole),
                  pl.BlockSpec((None, 1, S), whole),
                  pl.BlockSpec((None, 1, S), whole)],
        out_specs=[pl.BlockSpec((None, d, S), whole),
                   pl.BlockSpec((None, tk, d), tile),
                   pl.BlockSpec((None, tk, d), tile),
                   pl.BlockSpec((None, 1, tk), rowtile),
                   pl.BlockSpec((None, 1, S), whole)],
        out_shape=[jax.ShapeDtypeStruct((H, d, S), F32), jax.ShapeDtypeStruct((H, S, d), F32),
                   jax.ShapeDtypeStruct((H, S, d), F32), jax.ShapeDtypeStruct((H, 1, S), F32),
                   jax.ShapeDtypeStruct((H, 1, S), F32)],
        scratch_shapes=[pltpu.VMEM((2, tk, ts2), F32), pltpu.VMEM((2, tk, ts2), F32),
                        pltpu.VMEM((2, tk, ts2), BF16), pltpu.VMEM((2, tk, ts2), BF16),
                        pltpu.VMEM((tk, da), F32), pltpu.VMEM((tk, d), F32), pltpu.VMEM((tk, 1), F32)],
        compiler_params=_params(("parallel", "arbitrary")),
    )(qs, k_aug, kT, v, do, lse_row, delta_row)


def _heads_major(a, H, d):
    S = a.shape[0]
    return a.reshape(S, H, d).transpose(1, 0, 2)


def _heads_minor(a):
    H, S, d = a.shape
    return a.transpose(1, 0, 2).reshape(S, H * d)


def _lane_pick(x128, lane):
    return x128[:, lane:lane + 1]


def _l2_fwd(y):
    return lax.rsqrt(jnp.sum(y * y, axis=1, keepdims=True) + EPS)


def _gdn_prep(z, conv_w, a128, dt128, ts):
    S = z.shape[0]
    C3 = 3 * WIDTH
    hb = ts // 8

    def body(t, first, ins, outs, scratch):
        x_ref, halo_ref, zs_ref, w_ref, a_ref, dt_ref = ins
        qkv_ref, c_ref, gb_ref, gbT_ref = outs
        halo = jnp.where(t > 0, halo_ref[...], 0.0)
        xe = jnp.concatenate([halo, x_ref[...]], axis=0)
        w = w_ref[...]
        c = w[3:4, :] * xe[8:, :]
        for back in (1, 2, 3):
            c = c + w[3 - back:4 - back, :] * pltpu.roll(xe, back, 0)[8:, :]
        c_ref[...] = c
        y = c * _sigmoid(c)
        for h in range(GDN_HEADS):
            lo = h * GDN_DIM
            yq = y[:, lo:lo + GDN_DIM]
            qkv_ref[:, lo:lo + GDN_DIM] = yq * (_l2_fwd(yq) * (GDN_DIM ** -0.5))
            yk = y[:, WIDTH + lo:WIDTH + lo + GDN_DIM]
            qkv_ref[:, WIDTH + lo:WIDTH + lo + GDN_DIM] = yk * _l2_fwd(yk)
        qkv_ref[:, 2 * WIDTH:] = y[:, 2 * WIDTH:]
        zs = zs_ref[...]
        lane = lax.broadcasted_iota(jnp.int32, zs.shape, 1)
        g = -jnp.exp(a_ref[...]) * _softplus(zs + dt_ref[...])
        G = _scan_rows(g, ts, CHUNK)
        beta = _sigmoid(zs)
        out = jnp.where(lane < 8, pltpu.roll(g, 128 - LANE_BA, 1), jnp.where(lane < LANE_BB, G, beta))
        gb_ref[...] = out
        gbT_ref[...] = _transpose_exact(out)

    x_in = (z, (ts, C3), lambda i: (i, CB_BQKV))
    halo_in = (z, (8, C3), lambda i: (jnp.maximum(i * hb - 1, 0), CB_BQKV))
    return _tiled("gdn_prep", body, S // ts,
                  [x_in, halo_in, _cols(z, ts, 128, CB_SMALL), _full(conv_w), _full(a128), _full(dt128)],
                  [_orow(S, (C3,), F32, ts), _orow(S, (C3,), F32, ts), _orow(S, (128,), F32, ts),
                   ((128, S), F32, (128, ts), lambda i: (0, i))])


def _chunk_masks(nc):
    r = lax.broadcasted_iota(jnp.int32, (nc, CHUNK, CHUNK), 1)
    c = lax.broadcasted_iota(jnp.int32, (nc, CHUNK, CHUNK), 2)
    return c <= r, c < r, c == r


def _chunk_local(qh, kh, vh, Gc, Gr, beta):
    nc = qh.shape[0]
    incl, strict, _ = _chunk_masks(nc)
    gamma = jnp.exp(jnp.where(incl, Gc - Gr, NEG))
    kb = kh * beta
    P = _bdot(_b16(kb), _b16(kh), 2, 2)
    Qk = _bdot(_b16(qh), _b16(kh), 2, 2)
    eG = jnp.exp(Gc)
    Gl = Gc[:, CHUNK - 1:CHUNK, :]
    edec = jnp.exp(Gl - Gc)
    return incl, strict, gamma, kb, P, Qk, eG, edec


def _gdn_local_fwd(qkv, gb, grow, ts):
    S = qkv.shape[0]
    nc = ts // CHUNK

    def body(t, first, ins, outs, scratch):
        q_ref, k_ref, v_ref, gb_ref, gr_ref = ins
        u_ref, w_ref, qd_ref, kd_ref, aqk_ref, T_ref = outs
        gbv = gb_ref[...]
        heads = range(GDN_HEADS)
        _, _, eye = _chunk_masks(nc)
        T, X, rhs_u, rhs_w = [], [], [], []
        for h in heads:
            lo = h * GDN_DIM
            qh = q_ref[:, lo:lo + GDN_DIM].reshape(nc, CHUNK, GDN_DIM)
            kh = k_ref[:, lo:lo + GDN_DIM].reshape(nc, CHUNK, GDN_DIM)
            vh = v_ref[:, lo:lo + GDN_DIM].reshape(nc, CHUNK, GDN_DIM)
            Gc = _lane_pick(gbv, LANE_BA + h).reshape(nc, CHUNK, 1)
            beta = _lane_pick(gbv, LANE_BB + h).reshape(nc, CHUNK, 1)
            Gr = gr_ref[h].reshape(nc, 1, CHUNK)
            incl, strict, gamma, kb, P, Qk, eG, edec = _chunk_local(qh, kh, vh, Gc, Gr, beta)
            A = jnp.where(strict, P * gamma, 0.0)
            T.append(jnp.where(eye, 1.0, 0.0) - A)
            X.append(A)
            rhs_u.append(vh * beta)
            rhs_w.append(kb * eG)
            qd_ref[:, lo:lo + GDN_DIM] = _b16(qh * eG).reshape(ts, GDN_DIM)
            kd_ref[:, lo:lo + GDN_DIM] = _b16(kh * edec).reshape(ts, GDN_DIM)
            aqk_ref[h] = _b16(jnp.where(incl, Qk * gamma, 0.0)).reshape(ts, CHUNK)
        for _ in range(5):
            X = [_bdot(X[h], X[h], 2, 1, PREC_UT) for h in heads]
            T = [T[h] + _bdot(T[h], X[h], 2, 1, PREC_UT) for h in heads]
        u = [_bdot(T[h], rhs_u[h], 2, 1, PREC_UT) for h in heads]
        w = [_bdot(T[h], rhs_w[h], 2, 1, PREC_UT) for h in heads]
        for h in heads:
            lo = h * GDN_DIM
            u_ref[:, lo:lo + GDN_DIM] = u[h].reshape(ts, GDN_DIM)
            w_ref[:, lo:lo + GDN_DIM] = _b16(w[h]).reshape(ts, GDN_DIM)
            T_ref[h] = T[h].reshape(ts, CHUNK)

    wide = lambda dt: _orow(S, (WIDTH,), dt, ts)
    perhead = lambda dt: ((GDN_HEADS, S, CHUNK), dt, (GDN_HEADS, ts, CHUNK), lambda i: (0, i, 0))
    return _tiled("gdn_local_fwd", body, S // ts,
                  [_cols(qkv, ts, WIDTH, 0), _cols(qkv, ts, WIDTH, 1), _cols(qkv, ts, WIDTH, 2),
                   _rows(gb, ts), (grow, (GDN_HEADS, nc, CHUNK), lambda i: (0, i, 0))],
                  [wide(F32), wide(BF16), wide(BF16), wide(BF16), perhead(BF16), perhead(F32)])


def _gdn_scan_fwd(u, w, qd, kd, aqk, gb, ts):
    S = u.shape[0]
    nc = ts // CHUNK
    N = S // CHUNK

    def body(t, first, ins, outs, scratch):
        u_ref, w_ref, qd_ref, kd_ref, aqk_ref, gb_ref = ins
        o_ref, vn_ref, st_ref = outs
        (state,) = scratch

        @pl.when(first)
        def _():
            state[...] = jnp.zeros_like(state)

        def chunk(c, _):
            r0 = pl.multiple_of(c * CHUNK, CHUNK)
            rows = pl.ds(r0, CHUNK)
            glast = gb_ref[pl.ds(r0 + CHUNK - 1, 1), :]
            heads = range(GDN_HEADS)
            cols = [slice(h * GDN_DIM, (h + 1) * GDN_DIM) for h in heads]
            S_old = [state[h] for h in heads]
            u_h = [u_ref[rows, cols[h]] for h in heads]
            w_h = [_b16(w_ref[rows, cols[h]]) for h in heads]
            qd_h = [_b16(qd_ref[rows, cols[h]]) for h in heads]
            kd_h = [_b16(kd_ref[rows, cols[h]]) for h in heads]
            aqk_h = [_b16(aqk_ref[h, rows, :]) for h in heads]
            both = [_dot(jnp.concatenate([w_h[h], qd_h[h]], axis=0), _b16(S_old[h]))
                    for h in heads]
            vn_h = [u_h[h] - both[h][:CHUNK] for h in heads]
            vnb = [_b16(vn_h[h]) for h in heads]
            intra = [_dot(aqk_h[h], vnb[h]) for h in heads]
            outer = [_dot(kd_h[h], vnb[h], TN) for h in heads]
            o_h = [both[h][CHUNK:] + intra[h] for h in heads]
            S_new = [S_old[h] * jnp.exp(glast[:, LANE_BA + h:LANE_BA + h + 1]) + outer[h] for h in heads]
            for h in heads:
                st_ref[c, h] = S_old[h]
                state[h] = S_new[h]
                o_ref[rows, cols[h]] = o_h[h]
                vn_ref[rows, cols[h]] = vnb[h]
            return 0

        lax.fori_loop(0, nc, chunk, 0)

    wide_in = lambda a: _rows(a, ts)
    wide = lambda dt: _orow(S, (WIDTH,), dt, ts)
    states = ((N, GDN_HEADS, GDN_DIM, GDN_DIM), F32, (nc, GDN_HEADS, GDN_DIM, GDN_DIM),
              lambda i: (i, 0, 0, 0))
    return _tiled("gdn_scan_fwd", body, S // ts,
                  [wide_in(u), wide_in(w), wide_in(qd), wide_in(kd),
                   (aqk, (GDN_HEADS, ts, CHUNK), lambda i: (0, i, 0)), _rows(gb, ts)],
                  [wide(F32), wide(BF16), states],
                  scratch=[pltpu.VMEM((GDN_HEADS, GDN_DIM, GDN_DIM), F32)])


def _gdn_scan_bwd(do, w, qd, kd, aqk, vn, states, gb, ts):
    S = do.shape[0]
    nc = ts // CHUNK
    N = S // CHUNK

    def body(t, first, ins, outs, scratch):
        do_ref, w_ref, qd_ref, kd_ref, aqk_ref, vn_ref, st_ref, gb_ref = ins
        du_ref, dw_ref, dqd_ref, dkd_ref, daqk_ref, dgl_ref = outs
        (dstate,) = scratch

        @pl.when(first)
        def _():
            dstate[...] = jnp.zeros_like(dstate)

        r = lax.broadcasted_iota(jnp.int32, (CHUNK, CHUNK), 0)
        cc = lax.broadcasted_iota(jnp.int32, (CHUNK, CHUNK), 1)
        incl = cc <= r
        lane = lax.broadcasted_iota(jnp.int32, (1, 128), 1)

        def chunk(k, _):
            c = nc - 1 - k
            r0 = pl.multiple_of(c * CHUNK, CHUNK)
            rows = pl.ds(r0, CHUNK)
            glast = gb_ref[pl.ds(r0 + CHUNK - 1, 1), :]
            dgl_row = jnp.zeros((1, 128), F32)
            heads = range(GDN_HEADS)
            cols = [slice(h * GDN_DIM, (h + 1) * GDN_DIM) for h in heads]
            S_h = [st_ref[c, h] for h in heads]
            dS_h = [dstate[h] for h in heads]
            do_h = [_b16(do_ref[rows, cols[h]]) for h in heads]
            aqk_h = [_b16(aqk_ref[h, rows, :]) for h in heads]
            vn_h = [_b16(vn_ref[rows, cols[h]]) for h in heads]
            kd_h = [_b16(kd_ref[rows, cols[h]]) for h in heads]
            qd_h = [_b16(qd_ref[rows, cols[h]]) for h in heads]
            w_h = [_b16(w_ref[rows, cols[h]]) for h in heads]
            Sb = [_b16(S_h[h]) for h in heads]
            dSb = [_b16(dS_h[h]) for h in heads]
            dvn_a = [_dot(aqk_h[h], do_h[h], TN) for h in heads]
            dvn_b = [_dot(kd_h[h], dSb[h]) for h in heads]
            daqk_h = [jnp.where(incl, _dot(do_h[h], vn_h[h], NT), 0.0) for h in heads]
            dkd_h = [_dot(vn_h[h], dSb[h], NT) for h in heads]
            dvn_h = [dvn_a[h] + dvn_b[h] for h in heads]
            both = [jnp.concatenate([do_h[h], _b16(dvn_h[h])], axis=0) for h in heads]
            by_state = [_dot(both[h], Sb[h], NT) for h in heads]
            dS_dot = [_dot(jnp.concatenate([qd_h[h], -w_h[h]], axis=0), both[h], TN) for h in heads]
            res = []
            for h in heads:
                egl = jnp.exp(glast[:, LANE_BA + h:LANE_BA + h + 1])
                dgl = egl * jnp.sum(jnp.sum(dS_h[h] * S_h[h], axis=1, keepdims=True), axis=0,
                                    keepdims=True)
                dgl_row = jnp.where(lane == h, dgl, dgl_row)
                res.append((daqk_h[h], by_state[h][:CHUNK], dkd_h[h], -by_state[h][CHUNK:], dvn_h[h],
                            dS_dot[h] + egl * dS_h[h]))
            for h in heads:
                daqk, dqd, dkd, dw, dvn, dS_new = res[h]
                daqk_ref[h, rows, :] = daqk
                dqd_ref[rows, cols[h]] = dqd
                dkd_ref[rows, cols[h]] = dkd
                dw_ref[rows, cols[h]] = dw
                du_ref[rows, cols[h]] = dvn
                dstate[h] = dS_new
            dgl_ref[pl.ds(c, 1), :] = dgl_row
            return 0

        lax.fori_loop(0, nc, chunk, 0)

    wide_in = lambda a: _rows(a, ts)
    wide = _orow(S, (WIDTH,), F32, ts)
    perhead_in = lambda a: (a, (GDN_HEADS, ts, CHUNK), lambda i: (0, i, 0))
    perhead = ((GDN_HEADS, S, CHUNK), F32, (GDN_HEADS, ts, CHUNK), lambda i: (0, i, 0))
    return _tiled("gdn_scan_bwd", body, S // ts,
                  [wide_in(do), wide_in(w), wide_in(qd), wide_in(kd), perhead_in(aqk), wide_in(vn),
                   (states, (nc, GDN_HEADS, GDN_DIM, GDN_DIM), lambda i: (i, 0, 0, 0)), _rows(gb, ts)],
                  [wide, wide, wide, wide, perhead, ((N, 128), F32, (nc, 128), lambda i: (i, 0))],
                  scratch=[pltpu.VMEM((GDN_HEADS, GDN_DIM, GDN_DIM), F32)], reverse=True)


def _gdn_local_bwd(qkv, gb, grow, T, du, dw, dqd, dkd, daqk, dgl, ts):
    S = qkv.shape[0]
    nc = ts // CHUNK

    def body(t, first, ins, outs, scratch):
        (q_ref, k_ref, v_ref, gb_ref, gr_ref, T_ref, du_ref, dw_ref, dqd_ref, dkd_ref,
         daqk_ref, dgl_ref) = ins
        dqkv_ref, dgb_ref = outs
        gbv = gb_ref[...]
        dglv = dgl_ref[...]
        lane = lax.broadcasted_iota(jnp.int32, (ts, 128), 1)
        dG_all = jnp.zeros((ts, 128), F32)
        dbeta_all = jnp.zeros((ts, 128), F32)
        heads = range(GDN_HEADS)
        _, _, eye = _chunk_masks(nc)
        pre = []
        for h in heads:
            lo = h * GDN_DIM
            cols = slice(lo, lo + GDN_DIM)
            r3 = lambda ref: ref[:, cols].reshape(nc, CHUNK, GDN_DIM)
            qh, kh, vh = r3(q_ref), r3(k_ref), r3(v_ref)
            duh, dwh, dqdh, dkdh = r3(du_ref), r3(dw_ref), r3(dqd_ref), r3(dkd_ref)
            Gc = _lane_pick(gbv, LANE_BA + h).reshape(nc, CHUNK, 1)
            beta = _lane_pick(gbv, LANE_BB + h).reshape(nc, CHUNK, 1)
            Gr = gr_ref[h].reshape(nc, 1, CHUNK)
            Th = T_ref[h].reshape(nc, CHUNK, CHUNK)
            daq = daqk_ref[h].reshape(nc, CHUNK, CHUNK)
            local = _chunk_local(qh, kh, vh, Gc, Gr, beta)
            kb, eG = local[3], local[6]
            vb = vh * beta
            kbg = kb * eG
            dvb = _bdot(Th, duh, 1, 1, PREC_UT)
            dkbg = _bdot(Th, dwh, 1, 1, PREC_UT)
            dT = _bdot(duh, vb, 2, 2, PREC_UT) + _bdot(dwh, kbg, 2, 2, PREC_UT)
            pre.append((qh, kh, vh, dqdh, dkdh, beta, Th, daq, local, kbg, dvb, dkbg, dT))
        M1s = [_bdot(pre[h][6], pre[h][12], 1, 1, PREC_UT) for h in heads]
        dAs = [_bdot(M1s[h], pre[h][6], 2, 2, PREC_UT) for h in heads]
        for h in heads:
            lo = h * GDN_DIM
            cols = slice(lo, lo + GDN_DIM)
            qh, kh, vh, dqdh, dkdh, beta, Th, daq, local, kbg, dvb, dkbg, dT = pre[h]
            incl, strict, gamma, kb, P, Qk, eG, edec = local
            dA = jnp.where(strict, -dAs[h], 0.0)
            dP = dA * gamma
            dQ = daq * gamma
            dgam = (dA * P + daq * Qk) * gamma
            dPb, dQb = _b16(dP), _b16(dQ)
            khb, qhb, kbb = _b16(kh), _b16(qh), _b16(kb)
            dq = _bdot(dQb, khb, 2, 1) + dqdh * eG
            dkb = _bdot(dPb, khb, 2, 1) + dkbg * eG
            dk = (_bdot(dQb, qhb, 1, 1) + _bdot(dPb, kbb, 1, 1) + dkdh * edec + dkb * beta)
            dbeta = (jnp.sum(dkb * kh, axis=2, keepdims=True) + jnp.sum(dvb * vh, axis=2, keepdims=True))
            dv = dvb * beta
            col_as_col = jnp.sum(jnp.where(eye, jnp.sum(dgam, axis=1, keepdims=True), 0.0),
                                 axis=2, keepdims=True)
            kd_term = jnp.sum(dkdh * kh * edec, axis=2, keepdims=True)
            dG = (jnp.sum(dgam, axis=2, keepdims=True) - col_as_col
                  + jnp.sum(dqdh * qh * eG, axis=2, keepdims=True)
                  + jnp.sum(dkbg * kbg, axis=2, keepdims=True) - kd_term)
            dgl_h = dglv[:, h:h + 1].reshape(nc, 1, 1) + jnp.sum(kd_term, axis=1, keepdims=True)
            last = lax.broadcasted_iota(jnp.int32, (nc, CHUNK, 1), 1) == CHUNK - 1
            dG = dG + jnp.where(last, dgl_h, 0.0)
            dqkv_ref[:, cols] = dq.reshape(ts, GDN_DIM)
            dqkv_ref[:, WIDTH + lo:WIDTH + lo + GDN_DIM] = dk.reshape(ts, GDN_DIM)
            dqkv_ref[:, 2 * WIDTH + lo:2 * WIDTH + lo + GDN_DIM] = dv.reshape(ts, GDN_DIM)
            dG_all = jnp.where(lane == LANE_BA + h, dG.reshape(ts, 1), dG_all)
            dbeta_all = jnp.where(lane == LANE_BB + h, dbeta.reshape(ts, 1), dbeta_all)
        dg_all = _scan_rows(dG_all, ts, CHUNK, reverse=True)
        dgb_ref[...] = jnp.where(lane < LANE_BB, dg_all, dbeta_all)

    wide_in = lambda a: _rows(a, ts)
    perhead_in = lambda a: (a, (GDN_HEADS, ts, CHUNK), lambda i: (0, i, 0))
    return _tiled("gdn_local_bwd", body, S // ts,
                  [_cols(qkv, ts, WIDTH, 0), _cols(qkv, ts, WIDTH, 1), _cols(qkv, ts, WIDTH, 2),
                   _rows(gb, ts), (grow, (GDN_HEADS, nc, CHUNK), lambda i: (0, i, 0)), perhead_in(T),
                   wide_in(du), wide_in(dw), wide_in(dqd), wide_in(dkd), perhead_in(daqk),
                   (dgl, (nc, 128), lambda i: (i, 0))],
                  [_orow(S, (3 * WIDTH,), F32, ts), _orow(S, (128,), F32, ts)])


def _gdn_prep_bwd(dqkv, dgb, cpre, z, conv_w, a128, dt128, dz, ts):
    S = z.shape[0]
    C3 = 3 * WIDTH
    hb = ts // 8
    n_tiles = S // ts

    def dpre(dq, c):
        y, dsil = _silu_and_grad(c)
        parts = []
        for h in range(GDN_HEADS):
            lo = h * GDN_DIM
            yq = y[:, lo:lo + GDN_DIM]
            rq = _l2_fwd(yq)
            nq = yq * rq
            dn = dq[:, lo:lo + GDN_DIM] * (GDN_DIM ** -0.5)
            parts.append(rq * (dn - nq * jnp.sum(dn * nq, axis=1, keepdims=True)))
        for h in range(GDN_HEADS):
            lo = WIDTH + h * GDN_DIM
            yk = y[:, lo:lo + GDN_DIM]
            rk = _l2_fwd(yk)
            nk = yk * rk
            dn = dq[:, lo:lo + GDN_DIM]
            parts.append(rk * (dn - nk * jnp.sum(dn * nk, axis=1, keepdims=True)))
        parts.append(dq[:, 2 * WIDTH:])
        return jnp.concatenate(parts, axis=1) * dsil

    def body(t, first, ins, outs, scratch):
        (dq_ref, dqn_ref, c_ref, cn_ref, x_ref, xp_ref, zs_ref, dgb_ref, w_ref, a_ref, dt_ref) = ins
        dx_ref, dzs_ref, dw_ref, dad_ref = outs

        @pl.when(first)
        def _():
            dw_ref[...] = jnp.zeros_like(dw_ref)
            dad_ref[...] = jnp.zeros_like(dad_ref)

        dc = dpre(dq_ref[...], c_ref[...])
        dcn = jnp.where(t < n_tiles - 1, dpre(dqn_ref[...], cn_ref[...]), 0.0)
        dce = jnp.concatenate([dc, dcn], axis=0)
        w = w_ref[...]
        dx = w[3:4, :] * dc
        for back in (1, 2, 3):
            dx = dx + w[3 - back:4 - back, :] * pltpu.roll(dce, ts + 8 - back, 0)[:ts, :]
        dx_ref[...] = _b16(dx)
        halo = jnp.where(t > 0, xp_ref[...], 0.0)
        xe = jnp.concatenate([halo, x_ref[...]], axis=0)
        dw_ref[3:4, :] += jnp.sum(dc * xe[8:, :], axis=0, keepdims=True)
        for back in (1, 2, 3):
            dw_ref[3 - back:4 - back, :] += jnp.sum(dc * pltpu.roll(xe, back, 0)[8:, :], axis=0,
                                                     keepdims=True)
        zs = zs_ref[...]
        dgb = dgb_ref[...]
        lane = lax.broadcasted_iota(jnp.int32, zs.shape, 1)
        arg = zs + dt_ref[...]
        nega = -jnp.exp(a_ref[...])
        dba = dgb * nega * _sigmoid(arg)
        beta = _sigmoid(zs)
        dbb = dgb * beta * (1.0 - beta)
        dzs_ref[...] = jnp.where((lane >= LANE_BA) & (lane < LANE_BB), dba,
                                 jnp.where((lane >= LANE_BB) & (lane < LANE_BB + 4), dbb, 0.0))
        dad_ref[0:1, :] += jnp.sum(dgb * nega * _softplus(arg), axis=0, keepdims=True)
        dad_ref[1:2, :] += jnp.sum(dba, axis=0, keepdims=True)

    nxt = lambda i: (jnp.minimum((i + 1) * hb, S // 8 - 1), 0)
    prv = lambda i: (jnp.maximum(i * hb - 1, 0), CB_BQKV)
    return _tiled("gdn_prep_bwd", body, n_tiles,
                  [_rows(dqkv, ts), (dqkv, (8, C3), nxt), _rows(cpre, ts), (cpre, (8, C3), nxt),
                   (z, (ts, C3), lambda i: (i, CB_BQKV)), (z, (8, C3), prv),
                   _cols(z, ts, 128, CB_SMALL), _rows(dgb, ts), _full(conv_w), _full(a128), _full(dt128)],
                  [((S, N_AL), BF16, (ts, C3), lambda i: (i, CB_BQKV)), _orow(S, (128,), F32, ts),
                   _oacc((8, C3), F32), _oacc((8, 128), F32)],
                  fill=(dz, 0))


def _mem_attn_fwd(z, mk, mv, ts):
    S = z.shape[0]

    def body(t, first, ins, outs, scratch):
        q_ref, mk_ref, mv_ref = ins
        (o_ref,) = outs
        heads = range(MEM_HEADS)
        cols = [slice(h * MEM_DIM, (h + 1) * MEM_DIM) for h in heads]
        s = [_dot(_b16(q_ref[:, cols[h]]), _b16(mk_ref[:, cols[h]]), NT) * (MEM_DIM ** -0.5)
             for h in heads]
        p = []
        for h in heads:
            e = jnp.exp(s[h] - jnp.max(s[h], axis=1, keepdims=True))
            p.append(_b16(e / jnp.sum(e, axis=1, keepdims=True)))
        o = [_dot(p[h], _b16(mv_ref[:, cols[h]])) for h in heads]
        for h in heads:
            o_ref[:, cols[h]] = o[h]

    (o,) = _tiled("mem_attn_fwd", body, S // ts, [_cols(z, ts, WIDTH, CB_MQ), _full(mk), _full(mv)],
                  [_orow(S, (WIDTH,), F32, ts)])
    return o


def _mem_attn_bwd(do, z, mk, mv, dz, ts):
    S = z.shape[0]
    M = mk.shape[0]

    def body(t, first, ins, outs, scratch):
        do_ref, q_ref, mk_ref, mv_ref = ins
        dq_ref, dmk_ref, dmv_ref = outs

        @pl.when(first)
        def _():
            dmk_ref[...] = jnp.zeros_like(dmk_ref)
            dmv_ref[...] = jnp.zeros_like(dmv_ref)

        scale = MEM_DIM ** -0.5
        heads = range(MEM_HEADS)
        cols = [slice(h * MEM_DIM, (h + 1) * MEM_DIM) for h in heads]
        qb = [_b16(q_ref[:, cols[h]]) for h in heads]
        kb = [_b16(mk_ref[:, cols[h]]) for h in heads]
        dob = [_b16(do_ref[:, cols[h]]) for h in heads]
        s = [_dot(qb[h], kb[h], NT) * scale for h in heads]
        dp = [_dot(dob[h], _b16(mv_ref[:, cols[h]]), NT) for h in heads]
        p = []
        for h in heads:
            e = jnp.exp(s[h] - jnp.max(s[h], axis=1, keepdims=True))
            p.append(e / jnp.sum(e, axis=1, keepdims=True))
        dsb = [_b16(p[h] * (dp[h] - jnp.sum(dp[h] * p[h], axis=1, keepdims=True)) * scale) for h in heads]
        dmv = [_dot(_b16(p[h]), dob[h], TN) for h in heads]
        dq = [_dot(dsb[h], kb[h]) for h in heads]
        dmk = [_dot(dsb[h], qb[h], TN) for h in heads]
        for h in heads:
            dmv_ref[:, cols[h]] += dmv[h]
            dq_ref[:, cols[h]] = _b16(dq[h])
            dmk_ref[:, cols[h]] += dmk[h]

    return _tiled("mem_attn_bwd", body, S // ts,
                  [_rows(do, ts), _cols(z, ts, WIDTH, CB_MQ), _full(mk), _full(mv)],
                  [((S, N_AL), BF16, (ts, WIDTH), lambda i: (i, CB_MQ)), _oacc((M, WIDTH), F32),
                   _oacc((M, WIDTH), F32)],
                  fill=(dz, 0))


def _head_norm(ob, g):
    xs, rs = [], []
    for h in range(GDN_HEADS):
        o = ob[:, h * GDN_DIM:(h + 1) * GDN_DIM]
        r = lax.rsqrt(jnp.mean(o * o, axis=1, keepdims=True) + EPS)
        xs.append(o * r)
        rs.append(r)
    return xs, rs


def _merge_fwd(x, z, o_a, o_b, o_m, gdn_g, b_merge, wb, wout, ts):
    S, D = x.shape

    def body(t, first, ins, outs, scratch):
        (x_ref, g_ref, oa_ref, az_ref, ob_ref, bz_ref, om_ref, mz_ref, gg_ref, bm_ref, wb_ref,
         wo_ref) = ins
        xo_ref, ya_ref, yb_ref, ym_ref, mg_ref = outs
        ya = oa_ref[...] * _silu_and_grad(az_ref[...])[0]
        xs, _ = _head_norm(ob_ref[...], None)
        nb = jnp.concatenate([xh * gg_ref[...] for xh in xs], axis=1)
        yb = nb * _silu_and_grad(bz_ref[...])[0]
        ym = om_ref[...] * _silu_and_grad(mz_ref[...])[0]
        merged = jnp.zeros((ts, D), F32)
        for n, (y, y_ref) in enumerate(((ya, ya_ref), (yb, yb_ref), (ym, ym_ref))):
            yb16 = _b16(y)
            y_ref[...] = yb16
            gate = _sigmoid(g_ref[:, n * D:(n + 1) * D] + bm_ref[:, n * D:(n + 1) * D])
            merged = merged + gate * _dot(yb16, wb_ref[n])
        mb = _b16(merged)
        mg_ref[...] = mb
        xo_ref[...] = x_ref[...] + _dot(mb, wo_ref[...])

    half = lambda a: _rows(a, ts)
    return _tiled("merge_fwd", body, S // ts,
                  [_rows(x, ts), _cols(z, ts, 3 * D, CB_GATES), half(o_a), _cols(z, ts, WIDTH, CB_AZ),
                   half(o_b), _cols(z, ts, WIDTH, CB_BZ), half(o_m), _cols(z, ts, WIDTH, CB_MZ),
                   _full(gdn_g.reshape(1, GDN_DIM)), _full(b_merge.reshape(1, 3 * D)), _full(wb), _full(wout)],
                  [_orow(S, (D,), F32, ts), _orow(S, (WIDTH,), BF16, ts), _orow(S, (WIDTH,), BF16, ts),
                   _orow(S, (WIDTH,), BF16, ts), _orow(S, (D,), BF16, ts)])


def _merge_bwd(dout, z, o_a, o_b, o_m, ya, yb, ym, gdn_g, b_merge, wb, wout, hsum, ts):
    S, D = dout.shape

    def body(t, first, ins, outs, scratch):
        (do_ref, g_ref, oa_ref, az_ref, ob_ref, bz_ref, om_ref, mz_ref, ya_ref, yb_ref, ym_ref,
         gg_ref, bm_ref, wb_ref, wo_ref, hs_ref) = ins
        (dg_ref, dpa_ref, dpb_ref, dpm_ref, doa_ref, dob_ref, dom_ref, dl_ref, dbm_ref, dgg_ref) = outs
        G3 = 3 * D

        @pl.when(first)
        def _():
            dbm_ref[...] = jnp.zeros_like(dbm_ref)
            dgg_ref[...] = jnp.zeros_like(dgg_ref)

        dmerged = _dot(_b16(do_ref[...]), wo_ref[...], NT)
        dys = []
        for n, (y_ref, dp_ref) in enumerate(((ya_ref, dpa_ref), (yb_ref, dpb_ref), (ym_ref, dpm_ref))):
            sl = slice(n * D, (n + 1) * D)
            gate = _sigmoid(g_ref[:, sl] + bm_ref[:, sl])
            proj = _dot(y_ref[...], wb_ref[n])
            dproj = _b16(gate * dmerged)
            dp_ref[...] = dproj
            dgp = dmerged * proj * gate * (1.0 - gate)
            dg_ref[:, sl] = dgp.astype(dg_ref.dtype)
            dbm_ref[0:1, sl] += jnp.sum(dgp, axis=0, keepdims=True)
            dys.append(_dot(dproj, wb_ref[n], NT))
        dya, dyb, dym = dys
        sa, dsa = _silu_and_grad(az_ref[...])
        oa = oa_ref[...]
        doa = dya * sa
        doa_ref[...] = _b16(doa)
        dg_ref[:, G3:G3 + WIDTH] = _b16(dya * oa * dsa)
        dl_ref[...] = _dot(hs_ref[...], doa * oa, NT, HIGHEST)
        sm, dsm = _silu_and_grad(mz_ref[...])
        dom_ref[...] = _b16(dym * sm)
        dg_ref[:, G3 + 2 * WIDTH:G3 + 3 * WIDTH] = _b16(dym * om_ref[...] * dsm)
        sb, dsb = _silu_and_grad(bz_ref[...])
        xs, rs = _head_norm(ob_ref[...], None)
        gg = gg_ref[...]
        dgg = jnp.zeros((1, GDN_DIM), F32)
        for h in range(GDN_HEADS):
            cols = slice(h * GDN_DIM, (h + 1) * GDN_DIM)
            dn = dyb[:, cols] * sb[:, cols]
            dg_ref[:, G3 + WIDTH + h * GDN_DIM:G3 + WIDTH + (h + 1) * GDN_DIM] = _b16(
                dyb[:, cols] * (xs[h] * gg) * dsb[:, cols])
            dgg = dgg + jnp.sum(dn * xs[h], axis=0, keepdims=True)
            dxh = dn * gg
            dob_ref[:, cols] = _b16(rs[h] * (dxh - xs[h] * jnp.mean(dxh * xs[h], axis=1, keepdims=True)))
        dgg_ref[0:1, :] += dgg

    half = lambda a: _rows(a, ts)
    w512 = lambda dt: _orow(S, (WIDTH,), dt, ts)
    return _tiled("merge_bwd", body, S // ts,
                  [_rows(dout, ts), _cols(z, ts, 3 * D, CB_GATES), half(o_a), _cols(z, ts, WIDTH, CB_AZ),
                   half(o_b), _cols(z, ts, WIDTH, CB_BZ), half(o_m), _cols(z, ts, WIDTH, CB_MZ),
                   half(ya), half(yb), half(ym), _full(gdn_g.reshape(1, GDN_DIM)),
                   _full(b_merge.reshape(1, 3 * D)), _full(wb), _full(wout), _full(hsum)],
                  [((S, N_AL), BF16, (ts, 3 * D + 3 * WIDTH), lambda i: (i, CB_MERGE)),
                   _orow(S, (D,), BF16, ts), _orow(S, (D,), BF16, ts),
                   _orow(S, (D,), BF16, ts), w512(BF16), w512(BF16), w512(BF16),
                   ((128, S), F32, (128, ts), lambda i: (0, i)), _oacc((8, 3 * D), F32),
                   _oacc((8, GDN_DIM), F32)])


def _to_aligned(w):
    parts = [w[..., lo:lo + n] for lo, n, _ in sorted(W_IN_PIECES, key=lambda p: p[2])]
    parts.append(jnp.zeros(w.shape[:-1] + (N_AL - N_IN,), w.dtype))
    return jnp.concatenate(parts, axis=-1)


def _from_aligned(w):
    return jnp.concatenate([w[..., al:al + n] for _, n, al in W_IN_PIECES], axis=-1)


def _lanes128(v, lane0):
    return jnp.pad(v.astype(F32)[None, :], ((0, 0), (lane0, 128 - lane0 - v.shape[0])))


def _tiles(S):
    ts = min(512, S // 2)
    return dict(ts=ts, ts_small=min(256, S // 2), tq=min(512, S // 4), tq_fwd=min(1024, S // 2))


def _layer_fwd(x, mem, p):
    S = x.shape[0]
    tl = _tiles(S)
    ts, tss, tq = tl["ts"], tl["ts_small"], tl["tq"]
    h, rstd = _rms_fwd("norm_fwd", x, p["norm_g"], ts)
    z = _mm("in_proj", h, p["w_in_al"], tm=2048, tn=1664)

    b_fg128 = _lanes128(p["b_fg"], LANE_AF)
    f_hi, f_mid, f_lo = _fox_decay(z, b_fg128, ts)
    aq = z[:, CB_AQ * WIDTH:(CB_AQ + 1) * WIDTH]
    ak = z[:, CB_AK * WIDTH:(CB_AK + 1) * WIDTH]
    av = z[:, CB_AV * WIDTH:(CB_AV + 1) * WIDTH]
    q32 = _heads_major(aq, FOX_HEADS, FOX_DIM)
    kh = _heads_major(ak, FOX_HEADS, FOX_DIM).astype(BF16)
    vh = _heads_major(av, FOX_HEADS, FOX_DIM).astype(BF16)
    piecesT = jnp.stack([f[:FOX_HEADS] for f in (f_hi, f_mid, f_lo)], axis=1)
    pieces = piecesT.transpose(0, 2, 1)
    ones3 = jnp.ones((FOX_HEADS, S, 3), BF16)
    padk = jnp.zeros((FOX_HEADS, S, FOX_AUG - FOX_DIM - 6), BF16)
    q_aug = jnp.concatenate([q32, pieces.astype(F32), ones3.astype(F32), padk.astype(F32)], axis=-1)
    k_aug = jnp.concatenate([kh, ones3, -pieces, padk], axis=-1)
    kT_aug = jnp.concatenate([kh.transpose(0, 2, 1), ones3.transpose(0, 2, 1), -piecesT,
                              padk.transpose(0, 2, 1)], axis=1)
    v_aug = jnp.concatenate([vh, ones3[:, :, :1], jnp.zeros((FOX_HEADS, S, 128 - FOX_DIM - 1), BF16)],
                            axis=-1)
    o_h, lse, qs = _fox_fwd(q_aug, kT_aug, v_aug, tl["tq_fwd"])
    o_a = _heads_minor(o_h)

    a128 = _lanes128(p["a_log"], LANE_BA)
    dt128 = _lanes128(p["dt_bias"], LANE_BA)
    qkv, cpre, gb, gbT = _gdn_prep(z, p["conv_w"], a128, dt128, ts)
    grow = gbT[LANE_BA:LANE_BA + GDN_HEADS].reshape(GDN_HEADS, S // CHUNK, CHUNK)
    u, w, qd, kd, aqk, T = _gdn_local_fwd(qkv, gb, grow, ts)
    o_b, vn, states = _gdn_scan_fwd(u, w, qd, kd, aqk, gb, ts)

    mem_h, mem_r = _rms_fwd("mem_norm_fwd", mem, p["mem_norm_g"], mem.shape[0])
    mkv = _mm("mem_kv", mem_h, p["w_mem_kv"])
    mk, mv = mkv[:, :WIDTH], mkv[:, WIDTH:]
    o_m = _mem_attn_fwd(z, mk, mv, ts)

    x_next, ya, yb, ym, merged = _merge_fwd(x, z, o_a, o_b, o_m, p["gdn_norm_g"], p["b_merge"],
                                            p["w_branch"], p["w_out"], ts)
    saved = dict(x=x, h=h, rstd=rstd, z=z, b_fg128=b_fg128, qs=qs, k_aug=k_aug, kT_aug=kT_aug, v_aug=v_aug, lse=lse, o_a=o_a, a128=a128, dt128=dt128, qkv=qkv, cpre=cpre, gb=gb,
                 grow=grow, w=w, qd=qd, kd=kd, aqk=aqk, T=T, o_b=o_b, vn=vn, states=states,
                 mem_h=mem_h, mem_r=mem_r, mk=mk, mv=mv, o_m=o_m, ya=ya, yb=yb, ym=ym, merged=merged)
    return x_next, saved


def _layer_bwd(dout, mem, p, s):
    S = dout.shape[0]
    tl = _tiles(S)
    ts, tss, tq = tl["ts"], tl["ts_small"], tl["tq"]
    z = s["z"]
    hsum = (jnp.arange(128)[:, None] == jnp.arange(WIDTH)[None, :] // FOX_DIM).astype(F32)
    (dz, dpa, dpb, dpm, do_a, do_b, do_m, deltaT, db_merge, dgdn_g) = _merge_bwd(
        dout, z, s["o_a"], s["o_b"], s["o_m"], s["ya"], s["yb"], s["ym"], p["gdn_norm_g"],
        p["b_merge"], p["w_branch"], p["w_out"], hsum, tss)
    g = {}
    g["b_merge"] = db_merge[0]
    g["gdn_norm_g"] = dgdn_g[0]
    g["w_out"] = _mm("dw_out", s["merged"], dout, ta=True)
    g["w_branch"] = jnp.stack([_mm("dw_branch", y, dp, ta=True)
                               for y, dp in ((s["ya"], dpa), (s["yb"], dpb), (s["ym"], dpm))])

    do_h = _heads_major(do_a, FOX_HEADS, FOX_DIM).astype(BF16)
    delta_row = deltaT[:FOX_HEADS, None, :]
    dqT, dk_h, dv_h, dfk, dfq = _fox_bwd(s["qs"], s["k_aug"], s["kT_aug"], s["v_aug"], do_h, s["lse"],
                                         delta_row, tq)
    daq = _heads_minor(dqT.transpose(0, 2, 1))
    dak = _heads_minor(dk_h)
    dav = _heads_minor(dv_h)
    daf128, db_fg = _fox_decay_bwd(dfk[:, 0, :], dfq[:, 0, :], z, s["b_fg128"], ts)
    g["b_fg"] = db_fg[:FOX_HEADS]

    du, dw, dqd, dkd, daqk, dgl = _gdn_scan_bwd(do_b, s["w"], s["qd"], s["kd"], s["aqk"], s["vn"],
                                                s["states"], s["gb"], ts)
    dqkv, dgb = _gdn_local_bwd(s["qkv"], s["gb"], s["grow"], s["T"], du, dw, dqd, dkd, daqk, dgl, ts)
    dz, dzs_b, dconv, dad = _gdn_prep_bwd(dqkv, dgb, s["cpre"], z, p["conv_w"], s["a128"],
                                          s["dt128"], dz, ts)
    g["conv_w"] = dconv[:4]
    g["a_log"] = dad[0, LANE_BA:LANE_BA + GDN_HEADS]
    g["dt_bias"] = dad[1, LANE_BA:LANE_BA + GDN_HEADS]

    dz, dmk, dmv = _mem_attn_bwd(do_m, z, s["mk"], s["mv"], dz, ts)
    dmkv = jnp.concatenate([dmk, dmv], axis=1)
    g["w_mem_kv"] = _mm("dw_mem_kv", s["mem_h"], dmkv, ta=True)
    dmem_h = _mm("dmem_h", dmkv, p["w_mem_kv"], tb=True)
    M = mem.shape[0]
    _, g["mem_norm_g"] = _rms_bwd("mem_norm_bwd", dmem_h, mem, s["mem_r"], p["mem_norm_g"],
                                  jnp.zeros_like(mem), M)

    lane = jnp.arange(128)[None, :]
    dsmall = jnp.where(lane < 8, daf128, dzs_b)
    daqkv = jnp.concatenate([_b16(daq), _b16(dak), _b16(dav)], axis=1)
    dz = lax.dynamic_update_slice(dz, daqkv, (0, CB_AQKV * 3 * WIDTH))
    dz = lax.dynamic_update_slice(dz, _b16(dsmall), (0, CB_SMALL * 128))
    g["w_in_al"] = _mm("dw_in", s["h"], dz, ta=True, tn=1664, tk=2048)
    dh = _mm("dh", dz, p["w_in_al"], tb=True, tm=2048, tk=1664)
    dx, g["norm_g"] = _rms_bwd("norm_bwd", dh, s["x"], s["rstd"], p["norm_g"], dout, ts)
    return dx, g


def _local_step(x, mem, layers, final_norm_g, loss_target):
    S = x.shape[0]
    saves = []
    cur = x
    for p in layers:
        cur, sv = _layer_fwd(cur, mem, p)
        saves.append(sv)
    dx, dgf, loss_lanes = _loss_head(cur, final_norm_g, loss_target, _tiles(S)["ts"])
    grads = [None] * len(layers)
    for l in reversed(range(len(layers))):
        dx, grads[l] = _layer_bwd(dx, mem, layers[l], saves[l])
    return loss_lanes, dx, grads, dgf


HBM_SPEC = pl.BlockSpec(memory_space=pltpu.HBM)


def _mesh_pos():
    return lax.axis_index("x"), lax.axis_index("y"), lax.axis_index("c")


def _comm_call(name, body, arrays, out_shapes, n_remote, n_local):
    n = len(arrays)

    def kern(*refs):
        body(refs[:n], refs[n:2 * n], refs[2 * n], refs[2 * n + 1], refs[2 * n + 2])

    return pl.pallas_call(
        kern, name=name, out_shape=out_shapes, in_specs=[HBM_SPEC] * n, out_specs=[HBM_SPEC] * n,
        scratch_shapes=[pltpu.SemaphoreType.DMA((n_remote,)), pltpu.SemaphoreType.DMA((n_remote,)),
                        pltpu.SemaphoreType.DMA((max(n_local, 1),))],
    )(*arrays)


def _remote(src, dst, send_sems, recv_sems, k, to):
    return pltpu.make_async_remote_copy(src_ref=src, dst_ref=dst, send_sem=send_sems.at[k],
                                        recv_sem=recv_sems.at[k], device_id=to, device_id_type=MESH_ID)


def _other_chips(mx, my):
    return [(1 - mx, my), (mx, 1 - my), (1 - mx, 1 - my)]


def _gather_chips(name, shards):
    n = len(shards)

    def body(ins, outs, send_sems, recv_sems, local_sems):
        mx, my, mc = _mesh_pos()
        me = 2 * mx + my
        sibling = (mx, my, 1 - mc)
        chips = _other_chips(mx, my)
        sends = []
        for a in range(n):
            for k, (px, py) in enumerate(chips):
                cp = _remote(ins[a].at[mc], outs[a].at[me, mc], send_sems, recv_sems, 6 * a + k,
                             (px, py, mc))
                cp.start()
                sends.append(cp)
        for a in range(n):
            for k, (px, py) in enumerate(chips):
                j = 2 * px + py
                _remote(ins[a].at[mc], outs[a].at[j, mc], send_sems, recv_sems, 6 * a + k,
                        (px, py, mc)).wait_recv()
                cp = _remote(outs[a].at[j, mc], outs[a].at[j, mc], send_sems, recv_sems, 6 * a + 3 + k,
                             sibling)
                cp.start()
                sends.append(cp)
        for a in range(n):
            for k, (px, py) in enumerate(chips):
                j = 2 * px + py
                _remote(outs[a].at[j, 1 - mc], outs[a].at[j, 1 - mc], send_sems, recv_sems,
                        6 * a + 3 + k, sibling).wait_recv()
        for cp in sends:
            cp.wait_send()

    shapes = [jax.ShapeDtypeStruct((N_CHIPS,) + s.shape, s.dtype) for s in shards]
    outs = _comm_call(name, body, shards, shapes, 6 * n, 0)
    me = 2 * lax.axis_index("x") + lax.axis_index("y")
    return [lax.dynamic_update_index_in_dim(o, s, me, 0) for o, s in zip(outs, shards)]


def _sibling_swap(gs):
    n = len(gs)

    def body(ins, outs, send_sems, recv_sems, local_sems):
        mx, my, mc = _mesh_pos()
        sends = []
        for a in range(n):
            cp = _remote(ins[a].at[:, 1 - mc], outs[a], send_sems, recv_sems, a, (mx, my, 1 - mc))
            cp.start()
            sends.append(cp)
        for cp in sends:
            cp.wait()

    shapes = [jax.ShapeDtypeStruct((g.shape[0],) + g.shape[2:], g.dtype) for g in gs]
    return _comm_call("grad_sibling_swap", body, gs, shapes, n, 0)


def _chip_exchange(ps):
    n = len(ps)

    def body(ins, outs, send_sems, recv_sems, local_sems):
        mx, my, mc = _mesh_pos()
        me = 2 * mx + my
        chips = _other_chips(mx, my)
        sends = []
        for a in range(n):
            for k, (px, py) in enumerate(chips):
                cp = _remote(ins[a].at[2 * px + py], outs[a].at[me], send_sems, recv_sems, 3 * a + k,
                             (px, py, mc))
                cp.start()
                sends.append(cp)
        for a in range(n):
            for k, (px, py) in enumerate(chips):
                _remote(ins[a].at[me], outs[a].at[2 * px + py], send_sems, recv_sems, 3 * a + k,
                        (px, py, mc)).wait_recv()
        for cp in sends:
            cp.wait_send()

    shapes = [jax.ShapeDtypeStruct(p.shape, p.dtype) for p in ps]
    outs = _comm_call("grad_chip_exchange", body, ps, shapes, 3 * n, 0)
    me = 2 * lax.axis_index("x") + lax.axis_index("y")
    return [lax.dynamic_update_index_in_dim(o, lax.dynamic_index_in_dim(p, me, 0, keepdims=False), me, 0)
            for o, p in zip(outs, ps)]


def _sibling_gather(hs):
    n = len(hs)

    def body(ins, outs, send_sems, recv_sems, local_sems):
        mx, my, mc = _mesh_pos()
        sends = []
        for a in range(n):
            cp = _remote(ins[a], outs[a], send_sems, recv_sems, a, (mx, my, 1 - mc))
            cp.start()
            sends.append(cp)
        for cp in sends:
            cp.wait()

    shapes = [jax.ShapeDtypeStruct(h.shape, h.dtype) for h in hs]
    theirs = _comm_call("grad_sibling_gather", body, hs, shapes, n, 0)
    first = lax.axis_index("c") == 0
    return [jnp.stack([jnp.where(first, h, t), jnp.where(first, t, h)]) for h, t in zip(hs, theirs)]


def _add_pairs(a, b, tr, out_dtype):
    n, H, C = a.shape

    def kern(a_ref, b_ref, o_ref):
        o_ref[...] = (a_ref[...] + b_ref[...]).astype(o_ref.dtype)

    spec = pl.BlockSpec((None, tr, C), lambda j, i: (j, i, 0))
    return pl.pallas_call(
        kern, name="grad_pair_sum", grid=(n, H // tr), in_specs=[spec, spec], out_specs=spec,
        out_shape=jax.ShapeDtypeStruct((n, H, C), out_dtype),
        compiler_params=_params(("parallel", "parallel")),
    )(a, b)


def _sum_slots(r4, tr):
    n, H, C = r4.shape

    def kern(r_ref, o_ref):
        f = lambda k: r_ref[k].astype(F32)
        o_ref[...] = ((f(0) + f(1)) + f(2)) + f(3)

    return pl.pallas_call(
        kern, name="grad_chip_sum", grid=(H // tr,),
        in_specs=[pl.BlockSpec((n, tr, C), lambda i: (0, i, 0))],
        out_specs=pl.BlockSpec((tr, C), lambda i: (i, 0)),
        out_shape=jax.ShapeDtypeStruct((H, C), F32),
        compiler_params=_params(("parallel",)),
    )(r4)


def _adamw(w, g, m, v, tr):
    R, C = w.shape
    c1 = 1.0 - ADAM_B1
    c2 = 1.0 - ADAM_B2
    bc1 = 1.0 - ADAM_B1 ** ADAM_STEP
    bc2 = 1.0 - ADAM_B2 ** ADAM_STEP

    def kern(w_ref, g_ref, m_ref, v_ref, d_ref, mo_ref, vo_ref):
        gv = g_ref[...]
        mn = ADAM_B1 * m_ref[...] + c1 * gv
        vn = ADAM_B2 * v_ref[...] + c2 * (gv * gv)
        m_hat = mn / bc1
        v_hat = vn / bc2
        d_ref[...] = -ADAM_LR * (m_hat / (jnp.sqrt(v_hat) + ADAM_EPS) + ADAM_WD * w_ref[...])
        mo_ref[...] = mn
        vo_ref[...] = vn

    spec = pl.BlockSpec((tr, C), lambda i: (i, 0))
    shape = jax.ShapeDtypeStruct((R, C), F32)
    return pl.pallas_call(
        kern, name="adamw", grid=(R // tr,), in_specs=[spec] * 4, out_specs=[spec] * 3,
        out_shape=[shape] * 3, compiler_params=_params(("parallel",)),
    )(w, g, m, v)


PACK_COLS = 1024
PACK_ROWS = 512
W_SHARD = N_IN // N_CHIPS
SLAB = ("conv_w", "w_mem_kv", "w_branch", "w_out")
SMALL =("norm_g", "b_fg", "b_merge", "a_log", "dt_bias", "gdn_norm_g", "mem_norm_g", "final_norm_g")
ALL_WEIGHTS = ("norm_g", "w_in", "b_fg", "b_merge", "conv_w", "a_log", "dt_bias", "gdn_norm_g",
               "mem_norm_g", "w_mem_kv", "w_branch", "w_out", "final_norm_g")
SHARD_AXIS = {"w_in": 2, "conv_w": 2, "w_mem_kv": 1, "w_branch": 3, "w_out": 1}


def _pack(arrays, row_multiple):
    flat = jnp.concatenate([a.reshape(-1) for a in arrays])
    n = flat.shape[0]
    rows = -(-n // PACK_COLS)
    rows = -(-rows // row_multiple) * row_multiple
    flat = jnp.pad(flat, (0, rows * PACK_COLS - n))
    return flat.reshape(rows, PACK_COLS)


def _unpack(slab, shapes):
    out, off = [], 0
    for shp in shapes:
        n = 1
        for d in shp:
            n *= d
        r0, r1 = off // PACK_COLS, -(-(off + n) // PACK_COLS)
        rows = slab[r0:r1].reshape(-1)
        out.append(rows[off - r0 * PACK_COLS:off - r0 * PACK_COLS + n].reshape(shp))
        off += n
    return out


def _shard_of(full, name, j):
    ax = SHARD_AXIS[name]
    n = full.shape[ax] // N_CHIPS
    return lax.slice_in_dim(full, j * n, (j + 1) * n, axis=ax)


def _aligned_from_shards(shards):
    def cols(lo, n):
        parts = []
        while n > 0:
            j, off = divmod(lo, W_SHARD)
            take = min(n, W_SHARD - off)
            parts.append(shards[j][..., off:off + take])
            lo, n = lo + take, n - take
        return parts

    out = []
    for lo, n, _ in sorted(W_IN_PIECES, key=lambda p: p[2]):
        out += cols(lo, n)
    out.append(jnp.zeros(shards[0].shape[:-1] + (N_AL - N_IN,), shards[0].dtype))
    return jnp.concatenate(out, axis=-1)


def _shard_from_aligned(w_al, j):
    lo_j, hi_j = j * W_SHARD, (j + 1) * W_SHARD
    parts = []
    for lo, n, al in W_IN_PIECES:
        a, b = max(lo, lo_j), min(lo + n, hi_j)
        if a < b:
            parts.append(w_al[..., al + a - lo:al + b - lo])
    return jnp.concatenate(parts, axis=-1)


def kernel(x, mem, norm_g, w_in, b_fg, b_merge, conv_w, a_log, dt_bias, gdn_norm_g, mem_norm_g, w_mem_kv, w_branch, w_out, final_norm_g, loss_target, m_norm_g, m_w_in, m_b_fg, m_b_merge, m_conv_w, m_a_log, m_dt_bias, m_gdn_norm_g, m_mem_norm_g, m_w_mem_kv, m_w_branch, m_w_out, m_final_norm_g, v_norm_g, v_w_in, v_b_fg, v_b_merge, v_conv_w, v_a_log, v_dt_bias, v_gdn_norm_g, v_mem_norm_g, v_w_mem_kv, v_w_branch, v_w_out, v_final_norm_g):
    wts = dict(norm_g=norm_g, w_in=w_in, b_fg=b_fg, b_merge=b_merge, conv_w=conv_w, a_log=a_log,
               dt_bias=dt_bias, gdn_norm_g=gdn_norm_g, mem_norm_g=mem_norm_g, w_mem_kv=w_mem_kv,
               w_branch=w_branch, w_out=w_out, final_norm_g=final_norm_g)
    mom = dict(norm_g=m_norm_g, w_in=m_w_in, b_fg=m_b_fg, b_merge=m_b_merge, conv_w=m_conv_w,
               a_log=m_a_log, dt_bias=m_dt_bias, gdn_norm_g=m_gdn_norm_g, mem_norm_g=m_mem_norm_g,
               w_mem_kv=m_w_mem_kv, w_branch=m_w_branch, w_out=m_w_out, final_norm_g=m_final_norm_g)
    vel = dict(norm_g=v_norm_g, w_in=v_w_in, b_fg=v_b_fg, b_merge=v_b_merge, conv_w=v_conv_w,
               a_log=v_a_log, dt_bias=v_dt_bias, gdn_norm_g=v_gdn_norm_g, mem_norm_g=v_mem_norm_g,
               w_mem_kv=v_w_mem_kv, w_branch=v_w_branch, w_out=v_w_out, final_norm_g=v_final_norm_g)

    big = ("w_in", "w_mem_kv", "w_branch", "w_out")
    gathered = _gather_chips("weight_gather", [wts[n].astype(BF16) for n in big] + [conv_w])
    all_w = dict(zip(big + ("conv_w",), gathered))
    w_in_al = _aligned_from_shards([all_w["w_in"][j] for j in range(N_CHIPS)])

    layers = []
    for l in range(DEPTH):
        rows_of = lambda n: all_w[n][:, l].reshape(D_MODEL, D_MODEL)
        last_of = lambda n: jnp.concatenate([all_w[n][j, l] for j in range(N_CHIPS)], axis=-1)
        layers.append(dict(norm_g=norm_g[l], w_in_al=w_in_al[l], b_fg=b_fg[l], b_merge=b_merge[l],
                           conv_w=jnp.pad(last_of("conv_w"), ((0, 4), (0, 0))), a_log=a_log[l],
                           dt_bias=dt_bias[l], gdn_norm_g=gdn_norm_g[l], mem_norm_g=mem_norm_g[l],
                           w_mem_kv=rows_of("w_mem_kv"), w_branch=last_of("w_branch"),
                           w_out=rows_of("w_out")))

    loss_lanes, dx, grads, dgf = _local_step(x[0], mem[0], layers, final_norm_g, loss_target[0])

    gfull = {n: jnp.stack([grads[l][n] for l in range(DEPTH)])
             for n in ("norm_g", "b_fg", "b_merge", "conv_w", "a_log", "dt_bias", "gdn_norm_g",
                       "mem_norm_g", "w_mem_kv", "w_branch", "w_out")}
    gfull["final_norm_g"] = dgf
    loss_local = jnp.sum(loss_lanes).reshape(1)
    small_g = [gfull[n] for n in SMALL] + [loss_local]
    dw_al = jnp.stack([grads[l]["w_in_al"] for l in range(DEPTH)])
    ga = jnp.stack([_shard_from_aligned(dw_al, j) for j in range(N_CHIPS)])
    mats = ("w_mem_kv", "w_branch", "w_out")
    rest = ("conv_w",) + SMALL
    gb = jnp.stack([_pack([_shard_of(gfull[n], n, j) for n in mats], PACK_ROWS)
                    for j in range(N_CHIPS)])
    gc = jnp.stack([_pack([_shard_of(gfull["conv_w"], "conv_w", j)] + small_g, 16)
                    for j in range(N_CHIPS)])
    halves = lambda g: g.reshape(N_CHIPS, 2, g.shape[1] // 2, PACK_COLS)
    gb, gc = halves(gb), halves(gc)

    mc = lax.axis_index("c")
    tr = 256
    trs = (tr, tr, 8)
    from_sibling = _sibling_swap([ga, gb, gc])
    mine = [lax.dynamic_index_in_dim(g, mc, axis=1, keepdims=False) for g in (ga, gb, gc)]
    pair = [_add_pairs(a, b, t, dt) for a, b, t, dt in zip(mine, from_sibling, trs, (BF16, BF16, F32))]
    slots = _chip_exchange(pair)
    half = [_sum_slots(s, t) for s, t in zip(slots, trs)]
    ga_sum, gb_sum, gc_sum = _sibling_gather(half)
    flat = lambda g: g.reshape(-1, PACK_COLS)

    g_un = dict(zip(mats, _unpack(flat(gb_sum), [wts[n].shape for n in mats])))
    g_un.update(zip(rest + ("loss",), _unpack(flat(gc_sum), [wts[n].shape for n in rest] + [(1,)])))
    g_un["w_in"] = ga_sum
    d_un, m_un, v_un = {}, {}, {}
    rows2d = lambda a: a.reshape(-1, a.shape[-1])
    for n in ("w_in", "w_mem_kv", "w_branch", "w_out"):
        res = _adamw(rows2d(wts[n]), rows2d(g_un[n]), rows2d(mom[n]), rows2d(vel[n]), tr)
        d_un[n], m_un[n], v_un[n] = [r.reshape(wts[n].shape) for r in res]
    little = ("conv_w",) + SMALL
    slab = lambda d: _pack([d[n] for n in little], 8)
    res = _adamw(slab(wts), slab(g_un), slab(mom), slab(vel), 8)
    little_shapes = [wts[n].shape for n in little]
    for out, r in zip((d_un, m_un, v_un), res):
        out.update(zip(little, _unpack(r, little_shapes)))

    loss = g_un["loss"][0]
    return (loss, dx[None], *[g_un[n] for n in ALL_WEIGHTS], *[d_un[n] for n in ALL_WEIGHTS],
            *[m_un[n] for n in ALL_WEIGHTS], *[v_un[n] for n in ALL_WEIGHTS])
```
